```python
import jax, jax.numpy as jnp
from jax import lax
import numpy as np

D_MODEL = 1024
BATCH = 32
SEQ = 2048
DEPTH = 1

RWKV_WIDTH = 512
HEAD_DIM = 64
N_HEADS = RWKV_WIDTH // HEAD_DIM
CONV_WIDTH = D_MODEL - RWKV_WIDTH
CONV_KERNEL = 31
CONV_PAD = CONV_KERNEL // 2
DECAY_LORA = 64
ICLR_LORA = 64
GATE_LORA = 160
N_DIR = 2
D_FF = 2816
LN_EPS = 1e-5
GN_EPS = 64e-5
NORM_EPS = 1e-12
DEEPNORM_ALPHA = (2.0 * DEPTH) ** 0.25
DEEPNORM_BETA = (8.0 * DEPTH) ** -0.25

SHIFT_COLS = 3 * RWKV_WIDTH + N_DIR * DECAY_LORA + N_DIR * ICLR_LORA + GATE_LORA
IN_COLS = SHIFT_COLS + 2 * CONV_WIDTH
RWKV_SPLITS = (RWKV_WIDTH, 2 * RWKV_WIDTH, 3 * RWKV_WIDTH,
               3 * RWKV_WIDTH + N_DIR * DECAY_LORA,
               3 * RWKV_WIDTH + N_DIR * DECAY_LORA + N_DIR * ICLR_LORA)

kernel_name = "hybrid_rwkv7_conformer_conv_deepnorm_encoder"


def layer_norm(x, g, b, eps=LN_EPS):
    xf = x.astype(jnp.float32)
    mu = jnp.mean(xf, axis=-1, keepdims=True)
    var = jnp.mean(jnp.square(xf - mu), axis=-1, keepdims=True)
    return ((xf - mu) * lax.rsqrt(var + eps) * g + b).astype(x.dtype)


def swiglu_ffn(x, w_in, w_out):
    gate, up = jnp.split(x @ w_in, 2, axis=-1)
    return (jax.nn.silu(gate) * up) @ w_out


def centred_shift(p, mu_prev, mu_next):
    zero = jnp.zeros_like(p[:, :1])
    prev = jnp.concatenate([zero, p[:, :-1]], axis=1)
    nxt = jnp.concatenate([p[:, 1:], zero], axis=1)
    return p + mu_prev * (prev - p) + mu_next * (nxt - p)


def heads(z):
    return z.reshape(*z.shape[:-1], N_HEADS, HEAD_DIM)


def wkv7_bidir_scan(r, w, k, v, kk, a):
    def prep(z):
        z = jnp.stack([z[0], jnp.flip(z[1], axis=1)])
        return jnp.moveaxis(z, 2, 0)

    def both(z):
        return prep(jnp.stack([z, z]))

    xs = (both(r), prep(w), prep(k), both(v), both(kk), prep(a))

    def step(S, inp):
        rt, wt, kt, vt, kkt, at = inp
        sa = jnp.einsum('dbhij,dbhj->dbhi', S, -kkt)
        S = (S * wt[..., None, :] + sa[..., :, None] * (kkt * at)[..., None, :]
             + vt[..., :, None] * kt[..., None, :])
        y = jnp.einsum('dbhij,dbhj->dbhi', S, rt)
        return S, y

    b = r.shape[0]
    s0 = jnp.zeros((N_DIR, b, N_HEADS, HEAD_DIM, HEAD_DIM), jnp.float32)
    _, ys = lax.scan(step, s0, xs)
    y = ys[:, 0] + jnp.flip(ys[:, 1], axis=0)
    return jnp.moveaxis(y, 0, 1)


def rwkv7_mixer(ps, w0, w2, a0, a2, g2, k_k, k_a, r_k, lnx_g, lnx_b):
    ps = ps.astype(jnp.float32)
    bsz, t = ps.shape[0], ps.shape[1]
    r, k, v, wd, ad, gd = jnp.split(ps, RWKV_SPLITS, axis=-1)
    wd = wd.reshape(bsz, t, N_DIR, DECAY_LORA)
    ad = ad.reshape(bsz, t, N_DIR, ICLR_LORA)
    lw = jnp.einsum('btdr,drc->dbtc', jnp.tanh(wd), w2) + w0[:, None, None, :]
    decay = jnp.exp(-jnp.exp(-jax.nn.softplus(-lw) - 0.5))
    a = jax.nn.sigmoid(jnp.einsum('btdr,drc->dbtc', ad, a2) + a0[:, None, None, :])
    g = jax.nn.sigmoid(gd) @ g2
    kk = heads(k * k_k)
    kk = kk / jnp.maximum(jnp.linalg.norm(kk, axis=-1, keepdims=True), NORM_EPS)
    kd = k[None] * (1.0 + (a - 1.0) * k_a)
    rh, vh, kdh = heads(r), heads(v), heads(kd)
    y = wkv7_bidir_scan(rh, heads(decay), kdh, vh, kk, heads(a))
    mu = jnp.mean(y, axis=-1, keepdims=True)
    var = jnp.mean(jnp.square(y - mu), axis=-1, keepdims=True)
    yn = ((y - mu) * lax.rsqrt(var + GN_EPS)).reshape(bsz, t, RWKV_WIDTH) * lnx_g + lnx_b
    bonus = jnp.sum(rh[None] * kdh * r_k, axis=(0, -1))[..., None] * vh
    return (yn + bonus.reshape(bsz, t, RWKV_WIDTH)) * g


def conformer_conv(pc, dw, db, ln_g, ln_b):
    u = pc[..., :CONV_WIDTH] * jax.nn.sigmoid(pc[..., CONV_WIDTH:])
    y = lax.conv_general_dilated(u, dw[:, None, :].astype(u.dtype), window_strides=(1,),
                                 padding=[(CONV_PAD, CONV_PAD)],
                                 dimension_numbers=('NWC', 'WIO', 'NWC'),
                                 feature_group_count=CONV_WIDTH) + db
    return jax.nn.silu(layer_norm(y, ln_g, ln_b))


def _fwd_setup_inputs(seed: int = 0) -> dict:
    key = jax.random.key(seed)
    ks = jax.random.split(key, 32)
    L, D, C = DEPTH, D_MODEL, RWKV_WIDTH
    f32 = jnp.float32

    def nrm(k, shape, scale):
        return jax.random.normal(k, shape, f32) * scale

    col_scale = jnp.ones((IN_COLS,), f32).at[2 * C:3 * C].set(DEEPNORM_BETA)
    return {
        "x": nrm(ks[0], (BATCH, SEQ, D), 1.0),
        "ffn1_w_in": nrm(ks[1], (L, D, 2 * D_FF), D ** -0.5 * DEEPNORM_BETA),
        "ffn1_w_out": nrm(ks[2], (L, D_FF, D), D_FF ** -0.5 * DEEPNORM_BETA),
        "w_in": nrm(ks[3], (L, D, IN_COLS), D ** -0.5) * col_scale,
        "mu_prev": jax.random.uniform(ks[4], (L, SHIFT_COLS), f32, 0.0, 0.4),
        "mu_next": jax.random.uniform(ks[5], (L, SHIFT_COLS), f32, 0.0, 0.4),
        "w0": jax.random.uniform(ks[6], (L, N_DIR, C), f32, -3.0, 1.0),
        "w2": nrm(ks[7], (L, N_DIR, DECAY_LORA, C), 0.1 * DECAY_LORA ** -0.5),
        "a0": nrm(ks[8], (L, N_DIR, C), 0.1),
        "a2": nrm(ks[9], (L, N_DIR, ICLR_LORA, C), 0.5 * ICLR_LORA ** -0.5),
        "g2": nrm(ks[10], (L, GATE_LORA, C), GATE_LORA ** -0.5),
        "k_k": 0.85 + nrm(ks[11], (L, C), 0.05),
        "k_a": 1.0 + nrm(ks[12], (L, C), 0.05),
        "r_k": nrm(ks[13], (L, N_HEADS, HEAD_DIM), 0.1),
        "lnx_g": 1.0 + nrm(ks[14], (L, C), 0.05),
        "lnx_b": nrm(ks[15], (L, C), 0.02),
        "conv_dw": nrm(ks[16], (L, CONV_KERNEL, CONV_WIDTH), CONV_KERNEL ** -0.5),
        "conv_b": nrm(ks[17], (L, CONV_WIDTH), 0.02),
        "conv_ln_g": 1.0 + nrm(ks[18], (L, CONV_WIDTH), 0.05),
        "conv_ln_b": nrm(ks[19], (L, CONV_WIDTH), 0.02),
        "w_out": nrm(ks[20], (L, D, D), D ** -0.5 * DEEPNORM_BETA),
        "ffn2_w_in": nrm(ks[21], (L, D, 2 * D_FF), D ** -0.5 * DEEPNORM_BETA),
        "ffn2_w_out": nrm(ks[22], (L, D_FF, D), D_FF ** -0.5 * DEEPNORM_BETA),
        "ln1_g": 1.0 + nrm(ks[23], (L, D), 0.05),
        "ln1_b": nrm(ks[24], (L, D), 0.02),
        "ln2_g": 1.0 + nrm(ks[25], (L, D), 0.05),
        "ln2_b": nrm(ks[26], (L, D), 0.02),
        "ln3_g": 1.0 + nrm(ks[27], (L, D), 0.05),
        "ln3_b": nrm(ks[28], (L, D), 0.02),
    }


def _fwd_reference(x, ffn1_w_in, ffn1_w_out, w_in, mu_prev, mu_next, w0, w2, a0, a2, g2,
              k_k, k_a, r_k, lnx_g, lnx_b, conv_dw, conv_b, conv_ln_g, conv_ln_b, w_out,
              ffn2_w_in, ffn2_w_out, ln1_g, ln1_b, ln2_g, ln2_b, ln3_g, ln3_b):
    for l in range(DEPTH):
        x = layer_norm(DEEPNORM_ALPHA * x + 0.5 * swiglu_ffn(x, ffn1_w_in[l], ffn1_w_out[l]),
                       ln1_g[l], ln1_b[l])
        p = x @ w_in[l]
        ps = centred_shift(p[..., :SHIFT_COLS], mu_prev[l], mu_next[l])
        y_rwkv = rwkv7_mixer(ps, w0[l], w2[l], a0[l], a2[l], g2[l], k_k[l], k_a[l],
                             r_k[l], lnx_g[l], lnx_b[l]).astype(x.dtype)
        y_conv = conformer_conv(p[..., SHIFT_COLS:], conv_dw[l], conv_b[l],
                                conv_ln_g[l], conv_ln_b[l])
        mix = jnp.concatenate([y_rwkv, y_conv], axis=-1) @ w_out[l]
        x = layer_norm(DEEPNORM_ALPHA * x + mix, ln2_g[l], ln2_b[l])
        x = layer_norm(DEEPNORM_ALPHA * x + 0.5 * swiglu_ffn(x, ffn2_w_in[l], ffn2_w_out[l]),
                       ln3_g[l], ln3_b[l])
    return x


import jax as _jax
import jax.numpy as _jnp

TWIN_FORMAT = 'train_step'
FWD_PARAMS = ['x', 'ffn1_w_in', 'ffn1_w_out', 'w_in', 'mu_prev', 'mu_next', 'w0', 'w2', 'a0', 'a2', 'g2', 'k_k', 'k_a', 'r_k', 'lnx_g', 'lnx_b', 'conv_dw', 'conv_b', 'conv_ln_g', 'conv_ln_b', 'w_out', 'ffn2_w_in', 'ffn2_w_out', 'ln1_g', 'ln1_b', 'ln2_g', 'ln2_b', 'ln3_g', 'ln3_b']
TWIN_WEIGHTS = ['ffn1_w_in', 'ffn1_w_out', 'w_in', 'mu_prev', 'mu_next', 'w0', 'w2', 'a0', 'a2', 'g2', 'k_k', 'k_a', 'r_k', 'lnx_g', 'lnx_b', 'conv_dw', 'conv_b', 'conv_ln_g', 'conv_ln_b', 'w_out', 'ffn2_w_in', 'ffn2_w_out', 'ln1_g', 'ln1_b', 'ln2_g', 'ln2_b', 'ln3_g', 'ln3_b']
TWIN_DIFF_INPUT = 'x'
TWIN_INPUTS = ['x', 'ffn1_w_in', 'ffn1_w_out', 'w_in', 'mu_prev', 'mu_next', 'w0', 'w2', 'a0', 'a2', 'g2', 'k_k', 'k_a', 'r_k', 'lnx_g', 'lnx_b', 'conv_dw', 'conv_b', 'conv_ln_g', 'conv_ln_b', 'w_out', 'ffn2_w_in', 'ffn2_w_out', 'ln1_g', 'ln1_b', 'ln2_g', 'ln2_b', 'ln3_g', 'ln3_b', 'loss_target', 'm_ffn1_w_in', 'm_ffn1_w_out', 'm_w_in', 'm_mu_prev', 'm_mu_next', 'm_w0', 'm_w2', 'm_a0', 'm_a2', 'm_g2', 'm_k_k', 'm_k_a', 'm_r_k', 'm_lnx_g', 'm_lnx_b', 'm_conv_dw', 'm_conv_b', 'm_conv_ln_g', 'm_conv_ln_b', 'm_w_out', 'm_ffn2_w_in', 'm_ffn2_w_out', 'm_ln1_g', 'm_ln1_b', 'm_ln2_g', 'm_ln2_b', 'm_ln3_g', 'm_ln3_b', 'v_ffn1_w_in', 'v_ffn1_w_out', 'v_w_in', 'v_mu_prev', 'v_mu_next', 'v_w0', 'v_w2', 'v_a0', 'v_a2', 'v_g2', 'v_k_k', 'v_k_a', 'v_r_k', 'v_lnx_g', 'v_lnx_b', 'v_conv_dw', 'v_conv_b', 'v_conv_ln_g', 'v_conv_ln_b', 'v_w_out', 'v_ffn2_w_in', 'v_ffn2_w_out', 'v_ln1_g', 'v_ln1_b', 'v_ln2_g', 'v_ln2_b', 'v_ln3_g', 'v_ln3_b']
TWIN_OUTPUTS = ['loss', 'grad_x', 'grad_ffn1_w_in', 'grad_ffn1_w_out', 'grad_w_in', 'grad_mu_prev', 'grad_mu_next', 'grad_w0', 'grad_w2', 'grad_a0', 'grad_a2', 'grad_g2', 'grad_k_k', 'grad_k_a', 'grad_r_k', 'grad_lnx_g', 'grad_lnx_b', 'grad_conv_dw', 'grad_conv_b', 'grad_conv_ln_g', 'grad_conv_ln_b', 'grad_w_out', 'grad_ffn2_w_in', 'grad_ffn2_w_out', 'grad_ln1_g', 'grad_ln1_b', 'grad_ln2_g', 'grad_ln2_b', 'grad_ln3_g', 'grad_ln3_b', 'delta_ffn1_w_in', 'delta_ffn1_w_out', 'delta_w_in', 'delta_mu_prev', 'delta_mu_next', 'delta_w0', 'delta_w2', 'delta_a0', 'delta_a2', 'delta_g2', 'delta_k_k', 'delta_k_a', 'delta_r_k', 'delta_lnx_g', 'delta_lnx_b', 'delta_conv_dw', 'delta_conv_b', 'delta_conv_ln_g', 'delta_conv_ln_b', 'delta_w_out', 'delta_ffn2_w_in', 'delta_ffn2_w_out', 'delta_ln1_g', 'delta_ln1_b', 'delta_ln2_g', 'delta_ln2_b', 'delta_ln3_g', 'delta_ln3_b', 'new_m_ffn1_w_in', 'new_m_ffn1_w_out', 'new_m_w_in', 'new_m_mu_prev', 'new_m_mu_next', 'new_m_w0', 'new_m_w2', 'new_m_a0', 'new_m_a2', 'new_m_g2', 'new_m_k_k', 'new_m_k_a', 'new_m_r_k', 'new_m_lnx_g', 'new_m_lnx_b', 'new_m_conv_dw', 'new_m_conv_b', 'new_m_conv_ln_g', 'new_m_conv_ln_b', 'new_m_w_out', 'new_m_ffn2_w_in', 'new_m_ffn2_w_out', 'new_m_ln1_g', 'new_m_ln1_b', 'new_m_ln2_g', 'new_m_ln2_b', 'new_m_ln3_g', 'new_m_ln3_b', 'new_v_ffn1_w_in', 'new_v_ffn1_w_out', 'new_v_w_in', 'new_v_mu_prev', 'new_v_mu_next', 'new_v_w0', 'new_v_w2', 'new_v_a0', 'new_v_a2', 'new_v_g2', 'new_v_k_k', 'new_v_k_a', 'new_v_r_k', 'new_v_lnx_g', 'new_v_lnx_b', 'new_v_conv_dw', 'new_v_conv_b', 'new_v_conv_ln_g', 'new_v_conv_ln_b', 'new_v_w_out', 'new_v_ffn2_w_in', 'new_v_ffn2_w_out', 'new_v_ln1_g', 'new_v_ln1_b', 'new_v_ln2_g', 'new_v_ln2_b', 'new_v_ln3_g', 'new_v_ln3_b']
TWIN_LEAF_KINDS = {'loss': 'loss', 'grad_x': 'grad_x', 'grad_ffn1_w_in': 'grad_w', 'grad_ffn1_w_out': 'grad_w', 'grad_w_in': 'grad_w', 'grad_mu_prev': 'grad_w', 'grad_mu_next': 'grad_w', 'grad_w0': 'grad_w', 'grad_w2': 'grad_w', 'grad_a0': 'grad_w', 'grad_a2': 'grad_w', 'grad_g2': 'grad_w', 'grad_k_k': 'grad_w', 'grad_k_a': 'grad_w', 'grad_r_k': 'grad_w', 'grad_lnx_g': 'grad_w', 'grad_lnx_b': 'grad_w', 'grad_conv_dw': 'grad_w', 'grad_conv_b': 'grad_w', 'grad_conv_ln_g': 'grad_w', 'grad_conv_ln_b': 'grad_w', 'grad_w_out': 'grad_w', 'grad_ffn2_w_in': 'grad_w', 'grad_ffn2_w_out': 'grad_w', 'grad_ln1_g': 'grad_w', 'grad_ln1_b': 'grad_w', 'grad_ln2_g': 'grad_w', 'grad_ln2_b': 'grad_w', 'grad_ln3_g': 'grad_w', 'grad_ln3_b': 'grad_w', 'delta_ffn1_w_in': 'delta_w', 'delta_ffn1_w_out': 'delta_w', 'delta_w_in': 'delta_w', 'delta_mu_prev': 'delta_w', 'delta_mu_next': 'delta_w', 'delta_w0': 'delta_w', 'delta_w2': 'delta_w', 'delta_a0': 'delta_w', 'delta_a2': 'delta_w', 'delta_g2': 'delta_w', 'delta_k_k': 'delta_w', 'delta_k_a': 'delta_w', 'delta_r_k': 'delta_w', 'delta_lnx_g': 'delta_w', 'delta_lnx_b': 'delta_w', 'delta_conv_dw': 'delta_w', 'delta_conv_b': 'delta_w', 'delta_conv_ln_g': 'delta_w', 'delta_conv_ln_b': 'delta_w', 'delta_w_out': 'delta_w', 'delta_ffn2_w_in': 'delta_w', 'delta_ffn2_w_out': 'delta_w', 'delta_ln1_g': 'delta_w', 'delta_ln1_b': 'delta_w', 'delta_ln2_g': 'delta_w', 'delta_ln2_b': 'delta_w', 'delta_ln3_g': 'delta_w', 'delta_ln3_b': 'delta_w', 'new_m_ffn1_w_in': 'new_m', 'new_m_ffn1_w_out': 'new_m', 'new_m_w_in': 'new_m', 'new_m_mu_prev': 'new_m', 'new_m_mu_next': 'new_m', 'new_m_w0': 'new_m', 'new_m_w2': 'new_m', 'new_m_a0': 'new_m', 'new_m_a2': 'new_m', 'new_m_g2': 'new_m', 'new_m_k_k': 'new_m', 'new_m_k_a': 'new_m', 'new_m_r_k': 'new_m', 'new_m_lnx_g': 'new_m', 'new_m_lnx_b': 'new_m', 'new_m_conv_dw': 'new_m', 'new_m_conv_b': 'new_m', 'new_m_conv_ln_g': 'new_m', 'new_m_conv_ln_b': 'new_m', 'new_m_w_out': 'new_m', 'new_m_ffn2_w_in': 'new_m', 'new_m_ffn2_w_out': 'new_m', 'new_m_ln1_g': 'new_m', 'new_m_ln1_b': 'new_m', 'new_m_ln2_g': 'new_m', 'new_m_ln2_b': 'new_m', 'new_m_ln3_g': 'new_m', 'new_m_ln3_b': 'new_m', 'new_v_ffn1_w_in': 'new_v', 'new_v_ffn1_w_out': 'new_v', 'new_v_w_in': 'new_v', 'new_v_mu_prev': 'new_v', 'new_v_mu_next': 'new_v', 'new_v_w0': 'new_v', 'new_v_w2': 'new_v', 'new_v_a0': 'new_v', 'new_v_a2': 'new_v', 'new_v_g2': 'new_v', 'new_v_k_k': 'new_v', 'new_v_k_a': 'new_v', 'new_v_r_k': 'new_v', 'new_v_lnx_g': 'new_v', 'new_v_lnx_b': 'new_v', 'new_v_conv_dw': 'new_v', 'new_v_conv_b': 'new_v', 'new_v_conv_ln_g': 'new_v', 'new_v_conv_ln_b': 'new_v', 'new_v_w_out': 'new_v', 'new_v_ffn2_w_in': 'new_v', 'new_v_ffn2_w_out': 'new_v', 'new_v_ln1_g': 'new_v', 'new_v_ln1_b': 'new_v', 'new_v_ln2_g': 'new_v', 'new_v_ln2_b': 'new_v', 'new_v_ln3_g': 'new_v', 'new_v_ln3_b': 'new_v'}


def _forward(args):
    return _fwd_reference(*[args[k] for k in FWD_PARAMS])


def _output_shape():
    out = _jax.eval_shape(lambda: _forward(_fwd_setup_inputs(0)))
    return out.shape, out.dtype

N_MICROBATCH = 1
ADAM_LR = 0.001
ADAM_B1 = 0.9
ADAM_B2 = 0.999
ADAM_EPS = 1e-08
ADAM_WD = 0.01
ADAM_STEP = 10
PER_EXAMPLE_BATCH_AXIS = {'x': 0, 'loss_target': 0}
SHARED_INPUTS = []
_WEIGHT_DTYPES = {'ffn1_w_in': _jnp.float32, 'ffn1_w_out': _jnp.float32, 'w_in': _jnp.float32, 'mu_prev': _jnp.float32, 'mu_next': _jnp.float32, 'w0': _jnp.float32, 'w2': _jnp.float32, 'a0': _jnp.float32, 'a2': _jnp.float32, 'g2': _jnp.float32, 'k_k': _jnp.float32, 'k_a': _jnp.float32, 'r_k': _jnp.float32, 'lnx_g': _jnp.float32, 'lnx_b': _jnp.float32, 'conv_dw': _jnp.float32, 'conv_b': _jnp.float32, 'conv_ln_g': _jnp.float32, 'conv_ln_b': _jnp.float32, 'w_out': _jnp.float32, 'ffn2_w_in': _jnp.float32, 'ffn2_w_out': _jnp.float32, 'ln1_g': _jnp.float32, 'ln1_b': _jnp.float32, 'ln2_g': _jnp.float32, 'ln2_b': _jnp.float32, 'ln3_g': _jnp.float32, 'ln3_b': _jnp.float32}
MOMENT_SCALE = {'ffn1_w_in': 1.306417e-02, 'ffn1_w_out': 2.140063e-02, 'w_in': 6.907105e-02, 'mu_prev': 1.083799e-01, 'mu_next': 1.038533e-01, 'w0': 2.307482e-02, 'w2': 4.653132e-03, 'a0': 2.117275e-02, 'a2': 1.404894e-02, 'g2': 6.994254e-02, 'k_k': 2.626671e-02, 'k_a': 7.823305e-02, 'r_k': 9.769283e-02, 'lnx_g': 6.463873e-02, 'lnx_b': 2.560674e-01, 'conv_dw': 7.819308e-02, 'conv_b': 3.593331e-01, 'conv_ln_g': 1.460801e-01, 'conv_ln_b': 2.008627e-01, 'w_out': 1.338249e-01, 'ffn2_w_in': 1.258429e-02, 'ffn2_w_out': 2.063701e-02, 'ln1_g': 5.185371e+00, 'ln1_b': 7.537667e-01, 'ln2_g': 5.634578e+00, 'ln2_b': 8.301598e-01, 'ln3_g': 6.457934e+01, 'ln3_b': 4.087947e+00}


def _to_microbatches(a, axis):
    t = _jnp.moveaxis(a, axis, 0)
    t = t.reshape((N_MICROBATCH, t.shape[0] // N_MICROBATCH) + t.shape[1:])
    return _jnp.moveaxis(t, 1, axis + 1)


def setup_inputs(seed: int = 0) -> dict:
    inp = _fwd_setup_inputs(seed)
    key = _jax.random.fold_in(_jax.random.key(seed), 7919)
    shape, _ = _output_shape()
    out = dict(inp)
    out["loss_target"] = _jax.random.normal(_jax.random.fold_in(key, 0), shape, _jnp.float32)
    for i, name in enumerate(TWIN_WEIGHTS):
        w = inp[name].astype(_jnp.float32)
        if MOMENT_SCALE is None:
            s = _jnp.sqrt(_jnp.mean(_jnp.square(w)) + 1e-30)
        else:
            s = MOMENT_SCALE[name]
        km, kv = _jax.random.split(_jax.random.fold_in(key, i + 1))
        out[name] = w
        out["m_" + name] = s * _jax.random.normal(km, w.shape, _jnp.float32)
        out["v_" + name] = (s * s) * _jax.random.uniform(kv, w.shape, _jnp.float32, 0.5, 1.5)
    if N_MICROBATCH > 1:
        for name, axis in PER_EXAMPLE_BATCH_AXIS.items():
            out[name] = _to_microbatches(out[name], axis)
    return {'x': out['x'], 'ffn1_w_in': out['ffn1_w_in'], 'ffn1_w_out': out['ffn1_w_out'], 'w_in': out['w_in'], 'mu_prev': out['mu_prev'], 'mu_next': out['mu_next'], 'w0': out['w0'], 'w2': out['w2'], 'a0': out['a0'], 'a2': out['a2'], 'g2': out['g2'], 'k_k': out['k_k'], 'k_a': out['k_a'], 'r_k': out['r_k'], 'lnx_g': out['lnx_g'], 'lnx_b': out['lnx_b'], 'conv_dw': out['conv_dw'], 'conv_b': out['conv_b'], 'conv_ln_g': out['conv_ln_g'], 'conv_ln_b': out['conv_ln_b'], 'w_out': out['w_out'], 'ffn2_w_in': out['ffn2_w_in'], 'ffn2_w_out': out['ffn2_w_out'], 'ln1_g': out['ln1_g'], 'ln1_b': out['ln1_b'], 'ln2_g': out['ln2_g'], 'ln2_b': out['ln2_b'], 'ln3_g': out['ln3_g'], 'ln3_b': out['ln3_b'], 'loss_target': out['loss_target'], 'm_ffn1_w_in': out['m_ffn1_w_in'], 'm_ffn1_w_out': out['m_ffn1_w_out'], 'm_w_in': out['m_w_in'], 'm_mu_prev': out['m_mu_prev'], 'm_mu_next': out['m_mu_next'], 'm_w0': out['m_w0'], 'm_w2': out['m_w2'], 'm_a0': out['m_a0'], 'm_a2': out['m_a2'], 'm_g2': out['m_g2'], 'm_k_k': out['m_k_k'], 'm_k_a': out['m_k_a'], 'm_r_k': out['m_r_k'], 'm_lnx_g': out['m_lnx_g'], 'm_lnx_b': out['m_lnx_b'], 'm_conv_dw': out['m_conv_dw'], 'm_conv_b': out['m_conv_b'], 'm_conv_ln_g': out['m_conv_ln_g'], 'm_conv_ln_b': out['m_conv_ln_b'], 'm_w_out': out['m_w_out'], 'm_ffn2_w_in': out['m_ffn2_w_in'], 'm_ffn2_w_out': out['m_ffn2_w_out'], 'm_ln1_g': out['m_ln1_g'], 'm_ln1_b': out['m_ln1_b'], 'm_ln2_g': out['m_ln2_g'], 'm_ln2_b': out['m_ln2_b'], 'm_ln3_g': out['m_ln3_g'], 'm_ln3_b': out['m_ln3_b'], 'v_ffn1_w_in': out['v_ffn1_w_in'], 'v_ffn1_w_out': out['v_ffn1_w_out'], 'v_w_in': out['v_w_in'], 'v_mu_prev': out['v_mu_prev'], 'v_mu_next': out['v_mu_next'], 'v_w0': out['v_w0'], 'v_w2': out['v_w2'], 'v_a0': out['v_a0'], 'v_a2': out['v_a2'], 'v_g2': out['v_g2'], 'v_k_k': out['v_k_k'], 'v_k_a': out['v_k_a'], 'v_r_k': out['v_r_k'], 'v_lnx_g': out['v_lnx_g'], 'v_lnx_b': out['v_lnx_b'], 'v_conv_dw': out['v_conv_dw'], 'v_conv_b': out['v_conv_b'], 'v_conv_ln_g': out['v_conv_ln_g'], 'v_conv_ln_b': out['v_conv_ln_b'], 'v_w_out': out['v_w_out'], 'v_ffn2_w_in': out['v_ffn2_w_in'], 'v_ffn2_w_out': out['v_ffn2_w_out'], 'v_ln1_g': out['v_ln1_g'], 'v_ln1_b': out['v_ln1_b'], 'v_ln2_g': out['v_ln2_g'], 'v_ln2_b': out['v_ln2_b'], 'v_ln3_g': out['v_ln3_g'], 'v_ln3_b': out['v_ln3_b']}


def _loss(weights, diff, rest, loss_target):
    with _jax.named_scope("forward"):
        args = {**rest, TWIN_DIFF_INPUT: diff, **{k: w.astype(_WEIGHT_DTYPES[k]) for k, w in weights.items()}}
        y = _forward(args)
    with _jax.named_scope("loss_head"):
        err = _jnp.square(y.astype(_jnp.float32) - loss_target)
        return 0.5 * _jnp.sum(_jnp.mean(err, axis=-1)) if err.ndim else 0.5 * err


def _adamw(w, g, m, v):
    m = ADAM_B1 * m + (1.0 - ADAM_B1) * g
    v = ADAM_B2 * v + (1.0 - ADAM_B2) * _jnp.square(g)
    m_hat = m / (1.0 - ADAM_B1 ** ADAM_STEP)
    v_hat = v / (1.0 - ADAM_B2 ** ADAM_STEP)
    delta = -ADAM_LR * (m_hat / (_jnp.sqrt(v_hat) + ADAM_EPS) + ADAM_WD * w)
    return delta, m, v


def reference(x, ffn1_w_in, ffn1_w_out, w_in, mu_prev, mu_next, w0, w2, a0, a2, g2, k_k, k_a, r_k, lnx_g, lnx_b, conv_dw, conv_b, conv_ln_g, conv_ln_b, w_out, ffn2_w_in, ffn2_w_out, ln1_g, ln1_b, ln2_g, ln2_b, ln3_g, ln3_b, loss_target, m_ffn1_w_in, m_ffn1_w_out, m_w_in, m_mu_prev, m_mu_next, m_w0, m_w2, m_a0, m_a2, m_g2, m_k_k, m_k_a, m_r_k, m_lnx_g, m_lnx_b, m_conv_dw, m_conv_b, m_conv_ln_g, m_conv_ln_b, m_w_out, m_ffn2_w_in, m_ffn2_w_out, m_ln1_g, m_ln1_b, m_ln2_g, m_ln2_b, m_ln3_g, m_ln3_b, v_ffn1_w_in, v_ffn1_w_out, v_w_in, v_mu_prev, v_mu_next, v_w0, v_w2, v_a0, v_a2, v_g2, v_k_k, v_k_a, v_r_k, v_lnx_g, v_lnx_b, v_conv_dw, v_conv_b, v_conv_ln_g, v_conv_ln_b, v_w_out, v_ffn2_w_in, v_ffn2_w_out, v_ln1_g, v_ln1_b, v_ln2_g, v_ln2_b, v_ln3_g, v_ln3_b):
    given = dict(x=x, ffn1_w_in=ffn1_w_in, ffn1_w_out=ffn1_w_out, w_in=w_in, mu_prev=mu_prev, mu_next=mu_next, w0=w0, w2=w2, a0=a0, a2=a2, g2=g2, k_k=k_k, k_a=k_a, r_k=r_k, lnx_g=lnx_g, lnx_b=lnx_b, conv_dw=conv_dw, conv_b=conv_b, conv_ln_g=conv_ln_g, conv_ln_b=conv_ln_b, w_out=w_out, ffn2_w_in=ffn2_w_in, ffn2_w_out=ffn2_w_out, ln1_g=ln1_g, ln1_b=ln1_b, ln2_g=ln2_g, ln2_b=ln2_b, ln3_g=ln3_g, ln3_b=ln3_b, loss_target=loss_target, m_ffn1_w_in=m_ffn1_w_in, m_ffn1_w_out=m_ffn1_w_out, m_w_in=m_w_in, m_mu_prev=m_mu_prev, m_mu_next=m_mu_next, m_w0=m_w0, m_w2=m_w2, m_a0=m_a0, m_a2=m_a2, m_g2=m_g2, m_k_k=m_k_k, m_k_a=m_k_a, m_r_k=m_r_k, m_lnx_g=m_lnx_g, m_lnx_b=m_lnx_b, m_conv_dw=m_conv_dw, m_conv_b=m_conv_b, m_conv_ln_g=m_conv_ln_g, m_conv_ln_b=m_conv_ln_b, m_w_out=m_w_out, m_ffn2_w_in=m_ffn2_w_in, m_ffn2_w_out=m_ffn2_w_out, m_ln1_g=m_ln1_g, m_ln1_b=m_ln1_b, m_ln2_g=m_ln2_g, m_ln2_b=m_ln2_b, m_ln3_g=m_ln3_g, m_ln3_b=m_ln3_b, v_ffn1_w_in=v_ffn1_w_in, v_ffn1_w_out=v_ffn1_w_out, v_w_in=v_w_in, v_mu_prev=v_mu_prev, v_mu_next=v_mu_next, v_w0=v_w0, v_w2=v_w2, v_a0=v_a0, v_a2=v_a2, v_g2=v_g2, v_k_k=v_k_k, v_k_a=v_k_a, v_r_k=v_r_k, v_lnx_g=v_lnx_g, v_lnx_b=v_lnx_b, v_conv_dw=v_conv_dw, v_conv_b=v_conv_b, v_conv_ln_g=v_conv_ln_g, v_conv_ln_b=v_conv_ln_b, v_w_out=v_w_out, v_ffn2_w_in=v_ffn2_w_in, v_ffn2_w_out=v_ffn2_w_out, v_ln1_g=v_ln1_g, v_ln1_b=v_ln1_b, v_ln2_g=v_ln2_g, v_ln2_b=v_ln2_b, v_ln3_g=v_ln3_g, v_ln3_b=v_ln3_b)
    weights = {n: given[n] for n in TWIN_WEIGHTS}
    shared = {n: given[n] for n in SHARED_INPUTS}
    per_example = {n: given[n] for n in ['x']}
    grad_fn = _jax.value_and_grad(_loss, argnums=(0, 1))

    def one_microbatch(ex, loss_target):
        ex = dict(ex)
        diff = ex.pop(TWIN_DIFF_INPUT)
        return grad_fn(weights, diff, {**shared, **ex}, loss_target)

    if N_MICROBATCH == 1:
        loss, (grad_w, grad_x) = one_microbatch(per_example, given["loss_target"])
    else:
        def body(carry, xs):
            loss_sum, grad_sum = carry
            l_k, (gw_k, gx_k) = one_microbatch(xs[0], xs[1])
            with _jax.named_scope("update"):
                return (loss_sum + l_k, _jax.tree.map(_jnp.add, grad_sum, gw_k)), gx_k

        init = (_jnp.zeros((), _jnp.float32), _jax.tree.map(_jnp.zeros_like, weights))
        (loss, grad_w), grad_x = _jax.lax.scan(body, init, (per_example, given["loss_target"]))
    with _jax.named_scope("update"):
        delta_w, new_m, new_v = {}, {}, {}
        for n in TWIN_WEIGHTS:
            delta_w[n], new_m[n], new_v[n] = _adamw(weights[n], grad_w[n], given["m_" + n], given["v_" + n])
    return (loss, grad_x, *[grad_w[n] for n in TWIN_WEIGHTS], *[delta_w[n] for n in TWIN_WEIGHTS],
            *[new_m[n] for n in TWIN_WEIGHTS], *[new_v[n] for n in TWIN_WEIGHTS])
```

```python
import functools
import math

import jax
import jax.numpy as jnp
from jax import lax
from jax.experimental import pallas as pl
from jax.experimental.pallas import tpu as pltpu

F32 = jnp.float32
BF16 = jnp.bfloat16

N_DEV = 8
D_MODEL = 1024
RW = 512
N_HEADS = 8
HEAD = 64
CW = 512
CONV_K = 31
CONV_PAD = 15
D_FF = 2816
FF_BLK = 704
GATE_LORA = 160
SHIFT_COLS = 1952
SHIFT_PAD = 2048
IN_COLS = 2976
IN_PAD = 3072
LN_EPS = 1e-5
GN_EPS = 64e-5
NORM_EPS = 1e-12
ALPHA = 2.0 ** 0.25
DECAY_SCALE = math.exp(-0.5)
CHUNK = 64
ADAM_LR, ADAM_B1, ADAM_B2, ADAM_EPS, ADAM_WD, ADAM_STEP = 0.001, 0.9, 0.999, 1e-8, 0.01, 10
VMEM_LIMIT = 56 * 1024 * 1024

_HI = lax.Precision.HIGHEST
_DN = {"nn": (((1,), (0,)), ((), ())), "nt": (((1,), (1,)), ((), ())), "tn": (((0,), (0,)), ((), ()))}


def _params():
    return pltpu.CompilerParams(vmem_limit_bytes=VMEM_LIMIT)


def _dot(a, b, dims="nn"):
    return lax.dot_general(a, b, _DN[dims], preferred_element_type=F32)


def _hdot(a, b, dims="nn"):
    return lax.dot_general(a, b, _DN[dims], precision=_HI, preferred_element_type=F32)


def _sigmoid(x):
    return 1.0 / (1.0 + jnp.exp(-x))


def _mm_call(name, dims, grid, red_axis, ins, in_specs, out_shape, out_spec, acc_shape,
             scale=1.0, add_scale=None):
    nred = grid[red_axis]

    def body(*refs):
        if add_scale is None:
            a_ref, b_ref, o_ref, acc_ref = refs
            add_ref = None
        else:
            a_ref, b_ref, add_ref, o_ref, acc_ref = refs
        k = pl.program_id(red_axis)

        @pl.when(k == 0)
        def _():
            acc_ref[...] = jnp.zeros_like(acc_ref)

        acc_ref[...] += _dot(a_ref[...].astype(BF16), b_ref[...].astype(BF16), dims)

        @pl.when(k == nred - 1)
        def _():
            r = acc_ref[...]
            if scale != 1.0:
                r = r * scale
            if add_ref is not None:
                r = r + add_scale * add_ref[...].astype(F32)
            o_ref[...] = r.astype(o_ref.dtype)

    return pl.pallas_call(
        body, name=name, grid=grid, in_specs=in_specs, out_specs=out_spec, out_shape=out_shape,
        scratch_shapes=[pltpu.VMEM(acc_shape, F32)], compiler_params=_params())(*ins)


def _tile(dim, cap):
    t = min(dim, cap)
    while dim % t or t % 128:
        t -= 128
        assert t > 0, (dim, cap)
    return t


def _mm(name, a, b, dims, out_dtype, scale=1.0, add=None, add_scale=None, tm=512, tn=1024, tk=1024):
    if dims == "tn":
        kd, m = a.shape
        n = b.shape[1]
    else:
        m, kd = a.shape
        n = b.shape[1] if dims == "nn" else b.shape[0]
    tm, tn, tk = _tile(m, tm), _tile(n, tn), _tile(kd, tk)
    a_spec = (pl.BlockSpec((tk, tm), lambda i, j, k: (k, i)) if dims == "tn"
              else pl.BlockSpec((tm, tk), lambda i, j, k: (i, k)))
    b_spec = (pl.BlockSpec((tn, tk), lambda i, j, k: (j, k)) if dims == "nt"
              else pl.BlockSpec((tk, tn), lambda i, j, k: (k, j)))
    o_spec = pl.BlockSpec((tm, tn), lambda i, j, k: (i, j))
    ins, specs = [a, b], [a_spec, b_spec]
    if add is not None:
        ins.append(add)
        specs.append(o_spec)
    return _mm_call(name, dims, (m // tm, n // tn, kd // tk), 2, ins, specs,
                    jax.ShapeDtypeStruct((m, n), out_dtype), o_spec, (tm, tn),
                    scale=scale, add_scale=add_scale if add is not None else None)


def _ffn_in(name, x, wg, tm=512):
    n = x.shape[0]
    nj = N_DEV // 2

    def body(x_ref, wgate_ref, wup_ref, hg_ref, hu_ref, act_ref):
        xb = x_ref[...].astype(BF16)
        g = _dot(xb, wgate_ref[...])
        u = _dot(xb, wup_ref[...])
        hg_ref[...] = g.astype(BF16)
        hu_ref[...] = u.astype(BF16)
        act_ref[...] = (g * _sigmoid(g) * u).astype(BF16)

    blk = pl.BlockSpec((None, tm, FF_BLK), lambda j, i: (j, i, 0))
    shp = jax.ShapeDtypeStruct((nj, n, FF_BLK), BF16)
    return pl.pallas_call(
        body, name=name, grid=(nj, n // tm),
        in_specs=[pl.BlockSpec((tm, D_MODEL), lambda j, i: (i, 0)),
                  pl.BlockSpec((None, D_MODEL, FF_BLK), lambda j, i: (j, 0, 0)),
                  pl.BlockSpec((None, D_MODEL, FF_BLK), lambda j, i: (j + nj, 0, 0))],
        out_specs=[blk, blk, blk], out_shape=[shp, shp, shp], compiler_params=_params())(x, wg, wg)


def _ffn_out_bwd(name, dz, wout, hg, hu, tm=512):
    n = dz.shape[0]
    nj = N_DEV // 2

    def body(dz_ref, w_ref, hg_ref, hu_ref, dhg_ref, dhu_ref):
        dact = 0.5 * _dot(dz_ref[...].astype(BF16), w_ref[...], "nt")
        g = hg_ref[...].astype(F32)
        u = hu_ref[...].astype(F32)
        s = _sigmoid(g)
        dhg_ref[...] = (dact * u * (s * (1.0 + g * (1.0 - s)))).astype(BF16)
        dhu_ref[...] = (dact * (g * s)).astype(BF16)

    blk = pl.BlockSpec((None, tm, FF_BLK), lambda j, i: (j, i, 0))
    shp = jax.ShapeDtypeStruct((nj, n, FF_BLK), BF16)
    return pl.pallas_call(
        body, name=name, grid=(nj, n // tm),
        in_specs=[pl.BlockSpec((tm, D_MODEL), lambda j, i: (i, 0)),
                  pl.BlockSpec((FF_BLK, D_MODEL), lambda j, i: (j, 0)), blk, blk],
        out_specs=[blk, blk], out_shape=[shp, shp], compiler_params=_params())(dz, wout, hg, hu)


def _mm_ln(name, a, b, xres, g, beta, c, tgt=None, a_blocked=False, tm=512, tk=512):
    if a_blocked:
        nk, n, kb = a.shape
        a_spec = pl.BlockSpec((None, tm, kb), lambda i, k: (k, i, 0))
    else:
        n, kd = a.shape
        kb = _tile(kd, tk)
        nk = kd // kb
        a_spec = pl.BlockSpec((tm, kb), lambda i, k: (i, k))
    d = b.shape[1]
    with_loss = tgt is not None

    def body(*refs):
        if with_loss:
            a_ref, b_ref, x_ref, g_ref, be_ref, t_ref, o_ref, z_ref, l_ref, acc_ref = refs
        else:
            a_ref, b_ref, x_ref, g_ref, be_ref, o_ref, z_ref, acc_ref = refs
        i, k = pl.program_id(0), pl.program_id(1)

        @pl.when(k == 0)
        def _():
            acc_ref[...] = jnp.zeros_like(acc_ref)

        acc_ref[...] += _dot(a_ref[...].astype(BF16), b_ref[...].astype(BF16))

        @pl.when(k == nk - 1)
        def _():
            z = ALPHA * x_ref[...] + c * acc_ref[...]
            z_ref[...] = z
            mu = jnp.mean(z, axis=-1, keepdims=True)
            zc = z - mu
            var = jnp.mean(zc * zc, axis=-1, keepdims=True)
            y = zc * lax.rsqrt(var + LN_EPS) * g_ref[...] + be_ref[...]
            if with_loss:
                err = y - t_ref[...]
                o_ref[...] = err * (1.0 / d)
                part = 0.5 * jnp.sum(jnp.sum(err * err, axis=-1, keepdims=True), axis=0, keepdims=True) * (1.0 / d)

                @pl.when(i == 0)
                def _():
                    l_ref[...] = jnp.zeros_like(l_ref)

                l_ref[...] += jnp.broadcast_to(part, l_ref.shape)
            else:
                o_ref[...] = y

    row = pl.BlockSpec((tm, d), lambda i, k: (i, 0))
    vec = pl.BlockSpec((1, d), lambda i, k: (0, 0))
    ins = [a, b, xres, g, beta]
    in_specs = [a_spec, pl.BlockSpec((kb, d), lambda i, k: (k, 0)), row, vec, vec]
    out_specs = [row, row]
    out_shape = [jax.ShapeDtypeStruct((n, d), F32), jax.ShapeDtypeStruct((n, d), F32)]
    if with_loss:
        ins.append(tgt)
        in_specs.append(row)
        out_specs.append(pl.BlockSpec((1, 128), lambda i, k: (0, 0)))
        out_shape.append(jax.ShapeDtypeStruct((1, 128), F32))
    return pl.pallas_call(
        body, name=name, grid=(n // tm, nk), in_specs=in_specs, out_specs=out_specs, out_shape=out_shape,
        scratch_shapes=[pltpu.VMEM((tm, d), F32)], compiler_params=_params())(*ins)


def _rowwise(name, fn, rows, params, out_rows, out_accs, tm=256):
    specs, ins = [], []
    for r in rows:
        arr, w, cb = r if isinstance(r, tuple) else (r, r.shape[1], 0)
        ins.append(arr)
        specs.append(pl.BlockSpec((tm, w), functools.partial(lambda i, cb: (i, cb), cb=cb)))
    n = ins[0].shape[0]
    for p in params:
        ins.append(p)
        specs.append(pl.BlockSpec(p.shape, lambda i: (0, 0)))
    n_in, n_or = len(ins), len(out_rows)

    def body(*refs):
        outs = fn(*[r[...] for r in refs[:n_in]])
        o_refs = refs[n_in:]
        for o_ref, o in zip(o_refs[:n_or], outs[:n_or]):
            o_ref[...] = o.astype(o_ref.dtype)
        if out_accs:
            @pl.when(pl.program_id(0) == 0)
            def _():
                for a_ref in o_refs[n_or:]:
                    a_ref[...] = jnp.zeros_like(a_ref)

            for a_ref, a in zip(o_refs[n_or:], outs[n_or:]):
                a_ref[...] += a.astype(F32)

    out_specs = [pl.BlockSpec((tm, w), lambda i: (i, 0)) for w, _ in out_rows]
    out_specs += [pl.BlockSpec(s, lambda i: (0, 0)) for s in out_accs]
    out_shape = [jax.ShapeDtypeStruct((n, w), dt) for w, dt in out_rows]
    out_shape += [jax.ShapeDtypeStruct(s, F32) for s in out_accs]
    return pl.pallas_call(body, name=name, grid=(n // tm,), in_specs=specs, out_specs=out_specs,
                          out_shape=out_shape, compiler_params=_params())(*ins)


def _vjp_of(fn, n_in):
    def g(*args):
        ins, cts = args[:n_in], args[n_in:]
        outs, pull = jax.vjp(fn, *ins)
        return pull(tuple(c.astype(o.dtype) for c, o in zip(cts, outs)))
    return g


def _ln_bwd(name, z, g, ct_a, sa, ct_b=None, sb=1.0):
    def fn(*vals):
        if ct_b is None:
            zt, ca, gt = vals
            ct = sa * ca
        else:
            zt, ca, cb, gt = vals
            ct = sa * ca + sb * cb
        mu = jnp.mean(zt, axis=-1, keepdims=True)
        zc = zt - mu
        rstd = lax.rsqrt(jnp.mean(zc * zc, axis=-1, keepdims=True) + LN_EPS)
        xh = zc * rstd
        dxh = ct * gt
        dz = rstd * (dxh - jnp.mean(dxh, axis=-1, keepdims=True)
                     - xh * jnp.mean(dxh * xh, axis=-1, keepdims=True))
        return dz, jnp.sum(ct * xh, axis=0, keepdims=True), jnp.sum(ct, axis=0, keepdims=True)

    d = z.shape[1]
    rows = [z, ct_a] + ([ct_b] if ct_b is not None else [])
    return _rowwise(name, fn, rows, [g], [(d, F32)], [(1, d), (1, d)])


def _shift(name, src, mu_a, mu_b, q=None, tt=256):
    bsz, t, _ = src.shape
    nt, r8, w = t // tt, tt // 8, SHIFT_PAD
    with_q = q is not None

    def body(cur_ref, prev_ref, next_ref, *rest):
        b, i = pl.program_id(0), pl.program_id(1)
        cur = cur_ref[...]
        prow = jnp.where(i > 0, prev_ref[7:8, :], 0.0)
        nrow = jnp.where(i < nt - 1, next_ref[0:1, :], 0.0)
        rid = lax.broadcasted_iota(jnp.int32, cur.shape, 0)
        dprev = jnp.where(rid == 0, prow, pltpu.roll(cur, 1, 0)) - cur
        dnext = jnp.where(rid == tt - 1, nrow, pltpu.roll(cur, tt - 1, 0)) - cur
        if with_q:
            q_ref, da_ref, db_ref = rest

            @pl.when((b == 0) & (i == 0))
            def _():
                da_ref[...] = jnp.zeros_like(da_ref)
                db_ref[...] = jnp.zeros_like(db_ref)

            qv = q_ref[...]
            da_ref[...] += jnp.sum(qv * dprev, axis=0, keepdims=True)
            db_ref[...] += jnp.sum(qv * dnext, axis=0, keepdims=True)
        else:
            ma_ref, mb_ref, o_ref = rest
            o_ref[...] = cur + ma_ref[...] * dprev + mb_ref[...] * dnext

    cur_spec = pl.BlockSpec((None, tt, w), lambda b, i: (b, i, 0))
    in_specs = [cur_spec,
                pl.BlockSpec((None, 8, w), lambda b, i: (b, jnp.maximum(i * r8 - 1, 0), 0)),
                pl.BlockSpec((None, 8, w), lambda b, i: (b, jnp.minimum((i + 1) * r8, t // 8 - 1), 0))]
    vec = pl.BlockSpec((1, w), lambda b, i: (0, 0))
    if with_q:
        return pl.pallas_call(
            body, name=name, grid=(bsz, nt), in_specs=in_specs + [cur_spec], out_specs=[vec, vec],
            out_shape=[jax.ShapeDtypeStruct((1, w), F32)] * 2, compiler_params=_params())(src, src, src, q)
    return pl.pallas_call(
        body, name=name, grid=(bsz, nt), in_specs=in_specs + [vec, vec], out_specs=cur_spec,
        out_shape=jax.ShapeDtypeStruct((bsz, t, w), F32), compiler_params=_params())(src, src, src, mu_a, mu_b)


def _halo_specs(t, tt, w):
    r16 = tt // 16
    return [pl.BlockSpec((None, tt, w), lambda b, i: (b, i, 0)),
            pl.BlockSpec((None, 16, w), lambda b, i: (b, jnp.maximum(i * r16 - 1, 0), 0)),
            pl.BlockSpec((None, 16, w), lambda b, i: (b, jnp.minimum((i + 1) * r16, t // 16 - 1), 0))]


def _fill_pad(pad_ref, cur_ref, prev_ref, next_ref, i, nt, tt):
    pad_ref[0:16, :] = jnp.where(i > 0, prev_ref[...], 0.0)
    pad_ref[16:16 + tt, :] = cur_ref[...]
    pad_ref[16 + tt:32 + tt, :] = jnp.where(i < nt - 1, next_ref[...], 0.0)


def _dwconv(name, u, dw32, bias, flip, tt=512):
    bsz, t, w = u.shape
    tt = min(tt, t)
    nt = t // tt

    def body(cur_ref, prev_ref, next_ref, dw_ref, b_ref, o_ref, pad_ref):
        i = pl.program_id(1)
        _fill_pad(pad_ref, cur_ref, prev_ref, next_ref, i, nt, tt)
        acc = jnp.broadcast_to(b_ref[...], (tt, w))
        for k in range(CONV_K):
            kk = CONV_K - 1 - k if flip else k
            acc = acc + pad_ref[pl.ds(1 + k, tt), :] * dw_ref[kk:kk + 1, :]
        o_ref[...] = acc

    return pl.pallas_call(
        body, name=name, grid=(bsz, nt),
        in_specs=_halo_specs(t, tt, w) + [pl.BlockSpec((32, w), lambda b, i: (0, 0)),
                                          pl.BlockSpec((1, w), lambda b, i: (0, 0))],
        out_specs=pl.BlockSpec((None, tt, w), lambda b, i: (b, i, 0)),
        out_shape=jax.ShapeDtypeStruct((bsz, t, w), F32),
        scratch_shapes=[pltpu.VMEM((tt + 32, w), F32)], compiler_params=_params())(u, u, u, dw32, bias)


def _dwconv_dw(name, u, dc, tt=512):
    bsz, t, w = u.shape
    tt = min(tt, t)
    nt = t // tt

    def body(cur_ref, prev_ref, next_ref, dc_ref, ddw_ref, db_ref, pad_ref):
        b, i = pl.program_id(0), pl.program_id(1)
        _fill_pad(pad_ref, cur_ref, prev_ref, next_ref, i, nt, tt)

        @pl.when((b == 0) & (i == 0))
        def _():
            ddw_ref[...] = jnp.zeros_like(ddw_ref)
            db_ref[...] = jnp.zeros_like(db_ref)

        dcv = dc_ref[...]
        db_ref[...] += jnp.sum(dcv, axis=0, keepdims=True)
        for k in range(CONV_K):
            ddw_ref[k:k + 1, :] += jnp.sum(dcv * pad_ref[pl.ds(1 + k, tt), :], axis=0, keepdims=True)

    return pl.pallas_call(
        body, name=name, grid=(bsz, nt),
        in_specs=_halo_specs(t, tt, w) + [pl.BlockSpec((None, tt, w), lambda b, i: (b, i, 0))],
        out_specs=[pl.BlockSpec((32, w), lambda b, i: (0, 0)), pl.BlockSpec((1, w), lambda b, i: (0, 0))],
        out_shape=[jax.ShapeDtypeStruct((32, w), F32), jax.ShapeDtypeStruct((1, w), F32)],
        scratch_shapes=[pltpu.VMEM((tt + 32, w), F32)], compiler_params=_params())(u, u, u, dc)


def _chunk(s, r, lw, k, v, kk, a, sgn):
    n = CHUNK
    row = lax.broadcasted_iota(jnp.int32, (n, n), 0)
    col = lax.broadcasted_iota(jnp.int32, (n, n), 1)
    dlt = (row - col) * sgn
    incl, strict = dlt >= 0, dlt > 0
    cum = _hdot(jnp.where(incl, 1.0, 0.0), lw)
    tot = jnp.sum(lw, axis=0, keepdims=True)
    e_neg = jnp.exp(-cum)
    e_rest = jnp.exp(tot - cum)
    beta = kk * a
    a_t = -kk * jnp.exp(cum - lw)
    b_t, k_t, r_t = beta * e_neg, k * e_neg, r * jnp.exp(cum)
    l_ab = jnp.where(strict, _hdot(a_t, b_t, "nt"), 0.0)
    l_ak = jnp.where(strict, _hdot(a_t, k_t, "nt"), 0.0)
    m_rb = jnp.where(incl, _hdot(r_t, b_t, "nt"), 0.0)
    m_rk = jnp.where(incl, _hdot(r_t, k_t, "nt"), 0.0)
    tm = jnp.where(row == col, 1.0, 0.0) + l_ab
    lp = l_ab
    for _ in range(int(math.log2(n)) - 1):
        lp = _hdot(lp, lp)
        tm = tm + _hdot(tm, lp)
    u = _hdot(tm, _hdot(a_t, s, "nt") + _hdot(l_ak, v))
    y = _hdot(r_t, s, "nt") + _hdot(m_rb, u) + _hdot(m_rk, v)
    s_new = s * jnp.exp(tot) + _hdot(u, beta * e_rest, "tn") + _hdot(v, k * e_rest, "tn")
    return y, s_new


def _scan_specs(nc, order):
    shared = pl.BlockSpec((None, None, CHUNK, HEAD), lambda d, b, h, c: (b, h, order(d, c), 0))
    per_dir = pl.BlockSpec((None, None, None, CHUNK, HEAD), lambda d, b, h, c: (d, b, h, order(d, c), 0))
    state = pl.BlockSpec((None, None, None, None, HEAD, HEAD), lambda d, b, h, c: (d, b, h, order(d, c), 0, 0))
    return shared, per_dir, state


def _scan_fwd(r, v, kk, lw, kd, a):
    bsz, nh, t, _ = r.shape
    nc = t // CHUNK

    def order(d, c):
        return c + d * (nc - 1 - 2 * c)

    def body(r_ref, v_ref, kk_ref, lw_ref, kd_ref, a_ref, y_ref, s0_ref, s_ref):
        d, c = pl.program_id(0), pl.program_id(3)

        @pl.when(c == 0)
        def _():
            s_ref[...] = jnp.zeros_like(s_ref)

        s = s_ref[...]
        s0_ref[...] = s
        y, s_new = _chunk(s, r_ref[...], lw_ref[...], kd_ref[...], v_ref[...], kk_ref[...], a_ref[...], 1 - 2 * d)
        y_ref[...] = y
        s_ref[...] = s_new

    shared, per_dir, state = _scan_specs(nc, order)
    return pl.pallas_call(
        body, name="scan_fwd", grid=(2, bsz, nh, nc),
        in_specs=[shared, shared, shared, per_dir, per_dir, per_dir], out_specs=[per_dir, state],
        out_shape=[jax.ShapeDtypeStruct((2, bsz, nh, t, HEAD), F32),
                   jax.ShapeDtypeStruct((2, bsz, nh, nc, HEAD, HEAD), F32)],
        scratch_shapes=[pltpu.VMEM((HEAD, HEAD), F32)], compiler_params=_params())(r, v, kk, lw, kd, a)


def _scan_bwd(r, v, kk, lw, kd, a, s0, dy):
    bsz, nh, t, _ = r.shape
    nc = t // CHUNK

    def order(d, c):
        cc = nc - 1 - c
        return cc + d * (nc - 1 - 2 * cc)

    def body(r_ref, v_ref, kk_ref, dy_ref, lw_ref, kd_ref, a_ref, s0_ref,
             dr_ref, dv_ref, dkk_ref, dlw_ref, dkd_ref, da_ref, ds_ref):
        d, c = pl.program_id(0), pl.program_id(3)

        @pl.when(c == 0)
        def _():
            ds_ref[...] = jnp.zeros_like(ds_ref)

        sgn = 1 - 2 * d
        _, pull = jax.vjp(lambda s, r_, lw_, k_, v_, kk_, a_: _chunk(s, r_, lw_, k_, v_, kk_, a_, sgn),
                          s0_ref[...], r_ref[...], lw_ref[...], kd_ref[...], v_ref[...], kk_ref[...], a_ref[...])
        ds, dr, dlw, dk, dv, dkk, da = pull((dy_ref[...], ds_ref[...]))
        ds_ref[...] = ds
        dr_ref[...] = dr
        dv_ref[...] = dv
        dkk_ref[...] = dkk
        dlw_ref[...] = dlw
        dkd_ref[...] = dk
        da_ref[...] = da

    shared, per_dir, state = _scan_specs(nc, order)
    shp = jax.ShapeDtypeStruct((2, bsz, nh, t, HEAD), F32)
    return pl.pallas_call(
        body, name="scan_bwd", grid=(2, bsz, nh, nc),
        in_specs=[shared, shared, shared, shared, per_dir, per_dir, per_dir, state],
        out_specs=[per_dir] * 6, out_shape=[shp] * 6,
        scratch_shapes=[pltpu.VMEM((HEAD, HEAD), F32)], compiler_params=_params())(r, v, kk, dy, lw, kd, a, s0)


def _prep_fn(ps, w0, w2bd, a0, a2bd, g2p, k_k, k_a, hsum):
    r, k, v = ps[:, 0:RW], ps[:, RW:2 * RW], ps[:, 2 * RW:3 * RW]
    wd, ad, gd = ps[:, 1536:1664], ps[:, 1664:1792], ps[:, 1792:2048]
    logw = -DECAY_SCALE * _sigmoid(_hdot(jnp.tanh(wd), w2bd) + w0)
    a = _sigmoid(_hdot(ad, a2bd) + a0)
    g = _hdot(_sigmoid(gd), g2p)
    kkr = k * k_k
    kk = kkr / jnp.maximum(jnp.sqrt(_hdot(kkr * kkr, hsum)), NORM_EPS)
    k2 = jnp.concatenate([k, k], axis=1)
    ka2 = jnp.concatenate([k_a, k_a], axis=1)
    kd = k2 * (1.0 + (a - 1.0) * ka2)
    return r, v, kk, logw, a, kd, g


def _post_fn(y0, y1, r, v, kd, g, lnx_g, lnx_b, r_k, hsum):
    y = y0 + y1
    mu = _hdot(y, hsum) * (1.0 / HEAD)
    yc = y - mu
    var = _hdot(yc * yc, hsum) * (1.0 / HEAD)
    yn = yc * lax.rsqrt(var + GN_EPS) * lnx_g + lnx_b
    bonus = _hdot(r * (kd[:, 0:RW] + kd[:, RW:2 * RW]) * r_k, hsum) * v
    return ((yn + bonus) * g,)


def _glu_fn(pa, pb):
    return (pa * _sigmoid(pb),)


def _conv_out_fn(cv, ln_g, ln_b):
    mu = jnp.mean(cv, axis=-1, keepdims=True)
    cc = cv - mu
    var = jnp.mean(cc * cc, axis=-1, keepdims=True)
    y = cc * lax.rsqrt(var + LN_EPS) * ln_g + ln_b
    return (y * _sigmoid(y),)


def _to_heads(x, bsz, t):
    return x.reshape(bsz, t, N_HEADS, HEAD).transpose(0, 2, 1, 3)


def _to_heads2(x, bsz, t):
    return x.reshape(bsz, t, 2, N_HEADS, HEAD).transpose(2, 0, 3, 1, 4)


def _from_heads(x, bsz, t):
    lead = x.shape[:-4]
    nl = len(lead)
    perm = tuple(range(nl)) + (nl, nl + 2, nl + 1, nl + 3)
    return x.transpose(perm).reshape(lead + (bsz * t, RW))


def _local_step(x, tgt, w):
    bsz, t, d = x.shape
    n = bsz * t
    x2d, tgt2d = x.reshape(n, d), tgt.reshape(n, d)
    hsum = jnp.kron(jnp.eye(N_HEADS, dtype=F32), jnp.ones((HEAD, HEAD), F32))

    hg1, hu1, act1 = _ffn_in("ffn1_in", x2d, w["ffn1_w_in"])
    x1, z1 = _mm_ln("ffn1_out_ln1", act1, w["ffn1_w_out"], x2d, w["ln1_g"], w["ln1_b"], 0.5, a_blocked=True)
    p = _mm("w_in_proj", x1, w["w_in"], "nn", F32)
    p3 = p.reshape(bsz, t, IN_PAD)
    ps = _shift("shift_fwd", p3, w["mu_prev"], w["mu_next"]).reshape(n, SHIFT_PAD)
    prep_params = [w["w0"], w["w2"], w["a0"], w["a2"], w["g2"], w["k_k"], w["k_a"], hsum]
    r, v, kk, logw, a, kd, g = _rowwise(
        "rwkv_prep", _prep_fn, [ps], prep_params,
        [(RW, F32), (RW, F32), (RW, F32), (2 * RW, F32), (2 * RW, F32), (2 * RW, F32), (RW, F32)], [])
    rh, vh, kkh = (_to_heads(z, bsz, t) for z in (r, v, kk))
    lwh, ah, kdh = (_to_heads2(z, bsz, t) for z in (logw, a, kd))
    yh, s0 = _scan_fwd(rh, vh, kkh, lwh, kdh, ah)
    y01 = _from_heads(yh, bsz, t)
    post_params = [w["lnx_g"], w["lnx_b"], w["r_k"], hsum]
    (y_rwkv,) = _rowwise("rwkv_post", _post_fn, [y01[0], y01[1], r, v, kd, g], post_params, [(RW, BF16)], [])
    (u,) = _rowwise("conv_glu", _glu_fn, [(p, CW, 4), (p, CW, 5)], [], [(CW, F32)], [])
    cv = _dwconv("conv_dw", u.reshape(bsz, t, CW), w["conv_dw"], w["conv_b"], False).reshape(n, CW)
    (y_conv,) = _rowwise("conv_out", _conv_out_fn, [cv], [w["conv_ln_g"], w["conv_ln_b"]], [(CW, BF16)], [])
    ycat = jnp.concatenate([y_rwkv, y_conv], axis=1)
    x2, z2 = _mm_ln("w_out_ln2", ycat, w["w_out"], x1, w["ln2_g"], w["ln2_b"], 1.0)
    hg2, hu2, act2 = _ffn_in("ffn2_in", x2, w["ffn2_w_in"])
    dx3, z3, loss = _mm_ln("ffn2_out_ln3", act2, w["ffn2_w_out"], x2, w["ln3_g"], w["ln3_b"], 0.5,
                           tgt=tgt2d, a_blocked=True)

    gr = {}
    dz3, gr["ln3_g"], gr["ln3_b"] = _ln_bwd("ln3_bwd", z3, w["ln3_g"], dx3, 1.0)
    dx2 = _ffn_bwd("ffn2", gr, dz3, x2, w["ffn2_w_in"], w["ffn2_w_out"], hg2, hu2, act2)
    dz2, gr["ln2_g"], gr["ln2_b"] = _ln_bwd("ln2_bwd", z2, w["ln2_g"], dx2, 1.0)
    gr["w_out"] = _mm("w_out_wgrad", ycat, dz2, "tn", F32, tk=512)
    dycat = _mm("w_out_dgrad", dz2, w["w_out"], "nt", F32)
    conv_out_bwd = _vjp_of(_conv_out_fn, 3)
    dcv, gr["conv_ln_g"], gr["conv_ln_b"] = _rowwise(
        "conv_out_bwd", lambda cv_, ct_, g_, b_: conv_out_bwd(cv_, g_, b_, ct_),
        [cv, (dycat, CW, 1)], [w["conv_ln_g"], w["conv_ln_b"]], [(CW, F32)], [(1, CW), (1, CW)])
    dcv3 = dcv.reshape(bsz, t, CW)
    gr["conv_dw"], gr["conv_b"] = _dwconv_dw("conv_dw_wgrad", u.reshape(bsz, t, CW), dcv3)
    du = _dwconv("conv_dw_dgrad", dcv3, w["conv_dw"], jnp.zeros((1, CW), F32), True).reshape(n, CW)
    glu_bwd = _vjp_of(_glu_fn, 2)
    dpa, dpb = _rowwise("conv_glu_bwd", glu_bwd, [(p, CW, 4), (p, CW, 5), du], [], [(CW, F32), (CW, F32)], [])
    post_bwd = _vjp_of(_post_fn, 10)
    dy0, _, dr_post, dv_post, dkd_post, dg, gr["lnx_g"], gr["lnx_b"], gr["r_k"], _ = _rowwise(
        "rwkv_post_bwd", lambda *z: post_bwd(*z[:6], *z[7:], z[6]),
        [y01[0], y01[1], r, v, kd, g, (dycat, RW, 0)], post_params,
        [(RW, F32), (RW, F32), (RW, F32), (RW, F32), (2 * RW, F32), (RW, F32)],
        [(1, RW), (1, RW), (1, RW), (RW, RW)])
    dyh = _to_heads(dy0, bsz, t)
    drh, dvh, dkkh, dlwh, dkdh, dah = _scan_bwd(rh, vh, kkh, lwh, kdh, ah, s0, dyh)
    dr_s, dv_s, dkk_s = (_from_heads(z, bsz, t) for z in (drh, dvh, dkkh))
    dlw, dkd_s, da = (_from_heads(z, bsz, t) for z in (dlwh, dkdh, dah))
    cat2 = lambda z: jnp.concatenate([z[0], z[1]], axis=1)

    def prep_bwd(ps_, dr0, dr1, dr2, dv0, dv1, dv2, dkk0, dkk1, dlw_, da_, dkd0, dkd1, dg_, *prm):
        return _vjp_of(_prep_fn, 9)(ps_, *prm, dr0 + dr1 + dr2, dv0 + dv1 + dv2, dkk0 + dkk1,
                                    dlw_, da_, dkd0 + dkd1, dg_)

    dps, gr["w0"], gr["w2"], gr["a0"], gr["a2"], gr["g2"], gr["k_k"], gr["k_a"], _ = _rowwise(
        "rwkv_prep_bwd", prep_bwd,
        [ps, dr_s[0], dr_s[1], dr_post, dv_s[0], dv_s[1], dv_post, dkk_s[0], dkk_s[1],
         cat2(dlw), cat2(da), cat2(dkd_s), dkd_post, dg], prep_params,
        [(SHIFT_PAD, F32)], [q.shape for q in prep_params])
    dps3 = dps.reshape(bsz, t, SHIFT_PAD)
    gr["mu_prev"], gr["mu_next"] = _shift("shift_dmu", p3, None, None, q=dps3)
    dp_shift = _shift("shift_bwd", dps3, w["mu_next"], w["mu_prev"]).reshape(n, SHIFT_PAD)
    dp = jnp.concatenate([dp_shift, dpa, dpb], axis=1)
    gr["w_in"] = _mm("w_in_wgrad", x1, dp, "tn", F32, tk=512)
    dx1 = _mm("w_in_dgrad", dp, w["w_in"], "nt", F32, add=dz2, add_scale=ALPHA)
    dz1, gr["ln1_g"], gr["ln1_b"] = _ln_bwd("ln1_bwd", z1, w["ln1_g"], dx1, 1.0)
    grad_x = _ffn_bwd("ffn1", gr, dz1, x2d, w["ffn1_w_in"], w["ffn1_w_out"], hg1, hu1, act1)
    return loss[0, 0], grad_x.reshape(bsz, t, d), gr


def _ffn_bwd(tag, gr, dz, xin, wg, wout, hg, hu, act, tk=512):
    n = xin.shape[0]
    nj = N_DEV // 2
    nt = n // tk
    dhg, dhu = _ffn_out_bwd(tag + "_out_dgrad", dz, wout, hg, hu)
    gr[tag + "_w_out"] = _mm_call(
        tag + "_out_wgrad", "tn", (nj, nt), 1, [act, dz],
        [pl.BlockSpec((None, tk, FF_BLK), lambda j, i: (j, i, 0)), pl.BlockSpec((tk, D_MODEL), lambda j, i: (i, 0))],
        jax.ShapeDtypeStruct((D_FF, D_MODEL), F32), pl.BlockSpec((FF_BLK, D_MODEL), lambda j, i: (j, 0)),
        (FF_BLK, D_MODEL), scale=0.5)
    dw = []
    for nm, dh in (("gate", dhg), ("up", dhu)):
        dw.append(_mm_call(
            tag + "_in_wgrad_" + nm, "tn", (nj, nt), 1, [xin, dh],
            [pl.BlockSpec((tk, D_MODEL), lambda j, i: (i, 0)), pl.BlockSpec((None, tk, FF_BLK), lambda j, i: (j, i, 0))],
            jax.ShapeDtypeStruct((nj, D_MODEL, FF_BLK), F32), pl.BlockSpec((None, D_MODEL, FF_BLK), lambda j, i: (j, 0, 0)),
            (D_MODEL, FF_BLK)))
    gr[tag + "_w_in"] = dw
    dx = dz
    tm = 512
    for off, (nm, dh) in enumerate((("gate", dhg), ("up", dhu))):
        dx = _mm_call(
            tag + "_in_dgrad_" + nm, "nt", (n // tm, nj), 1, [dh, wg, dx],
            [pl.BlockSpec((None, tm, FF_BLK), lambda i, j: (j, i, 0)),
             pl.BlockSpec((None, D_MODEL, FF_BLK), functools.partial(lambda i, j, o: (j + o, 0, 0), o=off * nj)),
             pl.BlockSpec((tm, D_MODEL), lambda i, j: (i, 0))],
            jax.ShapeDtypeStruct((n, D_MODEL), F32), pl.BlockSpec((tm, D_MODEL), lambda i, j: (i, 0)),
            (tm, D_MODEL), add_scale=ALPHA if off == 0 else 1.0)
    return dx


def _mesh_pos():
    return lax.axis_index("x"), lax.axis_index("y"), lax.axis_index("c")


def _peer(pos, q):
    x, y, c = pos
    return (1 - x if q & 4 else x, 1 - y if q & 2 else y, 1 - c if q & 1 else c)


def _linear(pos):
    return 4 * pos[0] + 2 * pos[1] + pos[2]


def _exchange(name, xs, gather):
    nt = len(xs)

    def body(*refs):
        x_refs, o_refs = refs[:nt], refs[nt:2 * nt]
        send_sems, recv_sems, local_sems = refs[2 * nt:]
        pos = _mesh_pos()
        me = _linear(pos)
        local, sends, recvs = [], [], []
        for t in range(nt):
            src = x_refs[t] if gather else x_refs[t].at[me]
            cp = pltpu.make_async_copy(src, o_refs[t].at[me], local_sems.at[t])
            cp.start()
            local.append(cp)
        for q in range(1, N_DEV):
            peer = _peer(pos, q)
            for t in range(nt):
                src = x_refs[t] if gather else x_refs[t].at[_linear(peer)]
                cp = pltpu.make_async_remote_copy(
                    src_ref=src, dst_ref=o_refs[t].at[me], send_sem=send_sems.at[t, q - 1],
                    recv_sem=recv_sems.at[t, q - 1], device_id=peer, device_id_type=pl.DeviceIdType.MESH)
                cp.start()
                sends.append(cp)
                recvs.append(pltpu.make_async_remote_copy(
                    src_ref=src, dst_ref=o_refs[t].at[_linear(peer)], send_sem=send_sems.at[t, q - 1],
                    recv_sem=recv_sems.at[t, q - 1], device_id=peer, device_id_type=pl.DeviceIdType.MESH))
        for cp in recvs:
            cp.wait_recv()
        for cp in sends:
            cp.wait_send()
        for cp in local:
            cp.wait()

    out_shape = [jax.ShapeDtypeStruct((N_DEV,) + (x.shape if gather else x.shape[1:]), x.dtype) for x in xs]
    any_spec = pl.BlockSpec(memory_space=pl.ANY)
    return pl.pallas_call(
        body, name=name, in_specs=[any_spec] * nt, out_specs=[any_spec] * nt, out_shape=out_shape,
        scratch_shapes=[pltpu.SemaphoreType.DMA((nt, N_DEV - 1)), pltpu.SemaphoreType.DMA((nt, N_DEV - 1)),
                        pltpu.SemaphoreType.DMA((nt,))])(*xs)


def _adam_math(g, w, m, v):
    m = ADAM_B1 * m + (1.0 - ADAM_B1) * g
    v = ADAM_B2 * v + (1.0 - ADAM_B2) * (g * g)
    m_hat = m / (1.0 - ADAM_B1 ** ADAM_STEP)
    v_hat = v / (1.0 - ADAM_B2 ** ADAM_STEP)
    delta = -ADAM_LR * (m_hat / (jnp.sqrt(v_hat) + ADAM_EPS) + ADAM_WD * w)
    return delta, m, v


def _adam(name, parts, w, m, v, tr=128):
    rows, cols = w.shape
    tr = min(tr, rows)
    while rows % tr:
        tr -= 8

    def body(p_ref, w_ref, m_ref, v_ref, g_ref, d_ref, mo_ref, vo_ref):
        g = p_ref[0]
        for s in range(1, N_DEV):
            g = g + p_ref[s]
        g_ref[...] = g
        d_ref[...], mo_ref[...], vo_ref[...] = _adam_math(g, w_ref[...], m_ref[...], v_ref[...])

    blk = pl.BlockSpec((tr, cols), lambda i: (i, 0))
    shp = jax.ShapeDtypeStruct((rows, cols), F32)
    return pl.pallas_call(
        body, name=name, grid=(rows // tr,),
        in_specs=[pl.BlockSpec((N_DEV, tr, cols), lambda i: (0, i, 0)), blk, blk, blk],
        out_specs=[blk] * 4, out_shape=[shp] * 4, compiler_params=_params())(parts, w, m, v)


def _sum8(name, parts):
    _, rows, cols = parts.shape

    def body(p_ref, o_ref):
        g = p_ref[0]
        for s in range(1, N_DEV):
            g = g + p_ref[s]
        o_ref[...] = g

    return pl.pallas_call(body, name=name, out_shape=jax.ShapeDtypeStruct((rows, cols), F32),
                          compiler_params=_params())(parts)


def _adam_small(name, g, w, m, v):
    def body(g_ref, w_ref, m_ref, v_ref, d_ref, mo_ref, vo_ref):
        d_ref[...], mo_ref[...], vo_ref[...] = _adam_math(g_ref[...], w_ref[...], m_ref[...], v_ref[...])

    shp = jax.ShapeDtypeStruct(g.shape, F32)
    return pl.pallas_call(body, name=name, out_shape=[shp] * 3, compiler_params=_params())(g, w, m, v)


def _pack(arrs, lane=128):
    flat = jnp.concatenate([a.reshape(-1).astype(F32) for a in arrs])
    pad = (-flat.shape[0]) % (8 * lane)
    return jnp.pad(flat, (0, pad)).reshape(-1, lane)


def _unpack(packed, shapes):
    flat, out, off = packed.reshape(-1), [], 0
    for s in shapes:
        sz = math.prod(s)
        out.append(flat[off:off + sz].reshape(s))
        off += sz
    return out


def _pad_in_cols(wfull):
    zeros = jnp.zeros((wfull.shape[0], SHIFT_PAD - SHIFT_COLS), wfull.dtype)
    return jnp.concatenate([wfull[:, :SHIFT_COLS], zeros, wfull[:, SHIFT_COLS:]], axis=1)


def _unpad_in_cols(gfull):
    return jnp.concatenate([gfull[:, :SHIFT_COLS], gfull[:, SHIFT_PAD:]], axis=1)


def _block_diag2(wd):
    z = jnp.zeros_like(wd[0])
    return jnp.concatenate([jnp.concatenate([wd[0], z], axis=1), jnp.concatenate([z, wd[1]], axis=1)], axis=0)


def _unblock_diag2(g):
    return jnp.stack([g[0:64, 0:RW], g[64:128, RW:2 * RW]])


SMALL_SHARDED = ("w0", "w2", "a0", "a2", "g2", "conv_dw")
SMALL_REPL = ("mu_prev", "mu_next", "k_k", "k_a", "r_k", "lnx_g", "lnx_b", "conv_b", "conv_ln_g", "conv_ln_b",
              "ln1_g", "ln1_b", "ln2_g", "ln2_b", "ln3_g", "ln3_b")
BIG = ("ffn1_w_in", "ffn1_w_out", "w_in", "w_out", "ffn2_w_in", "ffn2_w_out")
WEIGHTS = ("ffn1_w_in", "ffn1_w_out", "w_in", "mu_prev", "mu_next", "w0", "w2", "a0", "a2", "g2", "k_k", "k_a",
           "r_k", "lnx_g", "lnx_b", "conv_dw", "conv_b", "conv_ln_g", "conv_ln_b", "w_out", "ffn2_w_in",
           "ffn2_w_out", "ln1_g", "ln1_b", "ln2_g", "ln2_b", "ln3_g", "ln3_b")


def _full_small(name, full):
    if name in ("w0", "a0"):
        return full.reshape(1, 2 * RW)
    if name in ("w2", "a2"):
        return _block_diag2(full)
    if name == "g2":
        return jnp.pad(full, ((0, 256 - GATE_LORA), (0, 0)))
    if name == "conv_dw":
        return jnp.pad(full, ((0, 1), (0, 0)))
    if name in ("mu_prev", "mu_next"):
        return jnp.pad(full.reshape(1, SHIFT_COLS), ((0, 0), (0, SHIFT_PAD - SHIFT_COLS)))
    return full.reshape(1, -1)


def _grad_small(name, g):
    if name in ("w0", "a0"):
        return g.reshape(2, RW)
    if name in ("w2", "a2"):
        return _unblock_diag2(g)
    if name == "g2":
        return g[:GATE_LORA]
    if name == "conv_dw":
        return g[:CONV_K]
    if name in ("mu_prev", "mu_next"):
        return g[0, :SHIFT_COLS]
    if name == "r_k":
        return g.reshape(N_HEADS, HEAD)
    return g.reshape(-1)


def kernel(x, ffn1_w_in, ffn1_w_out, w_in, mu_prev, mu_next, w0, w2, a0, a2, g2, k_k, k_a, r_k, lnx_g, lnx_b, conv_dw, conv_b, conv_ln_g, conv_ln_b, w_out, ffn2_w_in, ffn2_w_out, ln1_g, ln1_b, ln2_g, ln2_b, ln3_g, ln3_b, loss_target, m_ffn1_w_in, m_ffn1_w_out, m_w_in, m_mu_prev, m_mu_next, m_w0, m_w2, m_a0, m_a2, m_g2, m_k_k, m_k_a, m_r_k, m_lnx_g, m_lnx_b, m_conv_dw, m_conv_b, m_conv_ln_g, m_conv_ln_b, m_w_out, m_ffn2_w_in, m_ffn2_w_out, m_ln1_g, m_ln1_b, m_ln2_g, m_ln2_b, m_ln3_g, m_ln3_b, v_ffn1_w_in, v_ffn1_w_out, v_w_in, v_mu_prev, v_mu_next, v_w0, v_w2, v_a0, v_a2, v_g2, v_k_k, v_k_a, v_r_k, v_lnx_g, v_lnx_b, v_conv_dw, v_conv_b, v_conv_ln_g, v_conv_ln_b, v_w_out, v_ffn2_w_in, v_ffn2_w_out, v_ln1_g, v_ln1_b, v_ln2_g, v_ln2_b, v_ln3_g, v_ln3_b):
    args = dict(locals())
    wsh = {n: args[n][0] for n in WEIGHTS}
    msh = {n: args["m_" + n][0] for n in WEIGHTS}
    vsh = {n: args["v_" + n][0] for n in WEIGHTS}
    me = 4 * lax.axis_index("x") + 2 * lax.axis_index("y") + lax.axis_index("c")

    big = _exchange("gather_big", [wsh[n].astype(BF16) for n in BIG], True)
    gb = dict(zip(BIG, big))
    small_shapes = [wsh[n].shape for n in SMALL_SHARDED]
    small = _exchange("gather_small", [_pack([wsh[n] for n in SMALL_SHARDED])], True)[0]
    w = {}
    for n, s, shp in zip(SMALL_SHARDED, zip(*[_unpack(small[dv], small_shapes) for dv in range(N_DEV)]), small_shapes):
        w[n] = _full_small(n, jnp.concatenate(s, axis=-1))
    for n in SMALL_REPL:
        w[n] = _full_small(n, wsh[n])
    for tag in ("ffn1", "ffn2"):
        w[tag + "_w_in"] = gb[tag + "_w_in"]
        w[tag + "_w_out"] = gb[tag + "_w_out"].reshape(D_FF, D_MODEL)
    w["w_in"] = _pad_in_cols(gb["w_in"].transpose(1, 0, 2).reshape(D_MODEL, IN_COLS))
    w["w_out"] = gb["w_out"].reshape(D_MODEL, D_MODEL)

    loss_part, grad_x, gr = _local_step(x, loss_target, w)
    loss = lax.psum(loss_part, ("x", "y", "c"))

    gw_in = _unpad_in_cols(gr["w_in"]).reshape(D_MODEL, N_DEV, IN_COLS // N_DEV).transpose(1, 0, 2)
    sends = [gr["ffn1_w_in"][0], gr["ffn1_w_in"][1], gr["ffn1_w_out"].reshape(N_DEV, D_FF // N_DEV, D_MODEL),
             gw_in, gr["w_out"].reshape(N_DEV, D_MODEL // N_DEV, D_MODEL),
             gr["ffn2_w_in"][0], gr["ffn2_w_in"][1], gr["ffn2_w_out"].reshape(N_DEV, D_FF // N_DEV, D_MODEL)]
    out = {}
    parts = _exchange_grads(sends)
    for n, p in zip(BIG, parts):
        out[n] = _adam("adam_" + n, p, wsh[n], msh[n], vsh[n])

    small_names = SMALL_SHARDED + SMALL_REPL
    gsmall = [_grad_small(n, gr[n]) for n in small_names]
    gathered = _exchange("gather_small_grads", [_pack(gsmall)], True)[0]
    summed = _unpack(_sum8("sum_small_grads", gathered), [g.shape for g in gsmall])
    mine = []
    for n, g in zip(small_names, summed):
        if n in SMALL_SHARDED:
            g = lax.dynamic_slice_in_dim(g, me * HEAD, HEAD, axis=g.ndim - 1)
        mine.append(g)
    shapes = [g.shape for g in mine]
    d_s, m_s, v_s = _adam_small("adam_small", _pack(mine), _pack([wsh[n] for n in small_names]),
                                _pack([msh[n] for n in small_names]), _pack([vsh[n] for n in small_names]))
    for n, g, dl, mn, vn in zip(small_names, mine, _unpack(d_s, shapes), _unpack(m_s, shapes), _unpack(v_s, shapes)):
        out[n] = (g, dl, mn, vn)

    res = [loss, grad_x]
    for k in range(4):
        res += [out[n][k][None] for n in WEIGHTS]
    return tuple(res)


def _exchange_grads(sends):
    f1g, f1u, f1o, win, wout, f2g, f2u, f2o = sends
    f1 = jnp.concatenate([f1g, f1u], axis=0)
    f2 = jnp.concatenate([f2g, f2u], axis=0)
    return _exchange("scatter_big_grads", [f1, f1o, win, wout, f2, f2o], False)
```

```python
import functools
import math

import jax
import jax.numpy as jnp
from jax import lax
from jax.experimental import pallas as pl
from jax.experimental.pallas import tpu as pltpu

F32 = jnp.float32
BF16 = jnp.bfloat16

N_DEV = 8
D_MODEL = 1024
RW = 512
N_HEADS = 8
HEAD = 64
CW = 512
CONV_K = 31
D_FF = 2816
FF_BLK = 704
GATE_LORA = 160
SHIFT_COLS = 1952
SHIFT_PAD = 2048
IN_COLS = 2976
IN_PAD = 3072
LN_EPS = 1e-5
GN_EPS = 64e-5
NORM_EPS = 1e-12
ALPHA = 2.0 ** 0.25
DECAY_SCALE = math.exp(-0.5)
CHUNK = 64
ADAM_LR, ADAM_B1, ADAM_B2, ADAM_EPS, ADAM_WD, ADAM_STEP = 0.001, 0.9, 0.999, 1e-8, 0.01, 10
VMEM_LIMIT = 56 * 1024 * 1024

_DN = {"nn": (((1,), (0,)), ((), ())), "nt": (((1,), (1,)), ((), ())), "tn": (((0,), (0,)), ((), ()))}


def _params():
    return pltpu.CompilerParams(vmem_limit_bytes=VMEM_LIMIT)


def _dot(a, b, dims="nn"):
    return lax.dot_general(a, b, _DN[dims], preferred_element_type=F32)


def _split(x):
    hi = x.astype(BF16)
    return hi, (x - hi.astype(F32)).astype(BF16)


def _dot3_impl(a, b, dims):
    ah, al = _split(a)
    bh, bl = _split(b)
    return _dot(ah, bh, dims) + (_dot(ah, bl, dims) + _dot(al, bh, dims))


@functools.partial(jax.custom_vjp, nondiff_argnums=(2,))
def _dot3(a, b, dims="nn"):
    return _dot3_impl(a, b, dims)


def _dot3_fwd(a, b, dims):
    return _dot3_impl(a, b, dims), (a, b)


def _dot3_bwd(dims, res, ct):
    a, b = res
    if dims == "nn":
        return _dot3_impl(ct, b, "nt"), _dot3_impl(a, ct, "tn")
    if dims == "nt":
        return _dot3_impl(ct, b, "nn"), _dot3_impl(ct, a, "tn")
    return _dot3_impl(b, ct, "nt"), _dot3_impl(a, ct, "nn")


_dot3.defvjp(_dot3_fwd, _dot3_bwd)


def _ones_impl(x, g):
    x1 = x.astype(BF16)
    r1 = x - x1.astype(F32)
    x2 = r1.astype(BF16)
    x3 = (r1 - x2.astype(F32)).astype(BF16)
    return _dot(x1, g) + (_dot(x2, g) + _dot(x3, g))


@jax.custom_vjp
def _head_sum(x, g):
    return _ones_impl(x, g)


_head_sum.defvjp(lambda x, g: (_ones_impl(x, g), g), lambda g, ct: (_ones_impl(ct, g), jnp.zeros_like(g)))


def _sigmoid(x):
    return 1.0 / (1.0 + jnp.exp(-x))


def _mesh_pos():
    return lax.axis_index("x"), lax.axis_index("y"), lax.axis_index("c")


def _peer(pos, q):
    x, y, c = pos
    return (1 - x if q & 4 else x, 1 - y if q & 2 else y, 1 - c if q & 1 else c)


def _linear(pos):
    return 4 * pos[0] + 2 * pos[1] + pos[2]


def _exchange_copies(x_refs, o_refs, send_sems, recv_sems, local_sems, gather):
    pos = _mesh_pos()
    me = _linear(pos)
    starts, wait_recv, wait_send, wait_local = [], [], [], []
    for t in range(len(x_refs)):
        src = x_refs[t] if gather else x_refs[t].at[me]
        cp = pltpu.make_async_copy(src, o_refs[t].at[me], local_sems.at[t])
        starts.append(cp.start)
        wait_local.append(cp.wait)
    for q in range(1, N_DEV):
        peer = _peer(pos, q)
        for t in range(len(x_refs)):
            src = x_refs[t] if gather else x_refs[t].at[_linear(peer)]
            sems = dict(send_sem=send_sems.at[t, q - 1], recv_sem=recv_sems.at[t, q - 1],
                        device_id=peer, device_id_type=pl.DeviceIdType.MESH)
            send = pltpu.make_async_remote_copy(src_ref=src, dst_ref=o_refs[t].at[me], **sems)
            recv = pltpu.make_async_remote_copy(src_ref=src, dst_ref=o_refs[t].at[_linear(peer)], **sems)
            starts.append(send.start)
            wait_recv.append(recv.wait_recv)
            wait_send.append(send.wait_send)
    return starts, wait_recv + wait_send + wait_local


def _exchange_shapes(xs, gather):
    return [jax.ShapeDtypeStruct((N_DEV,) + (x.shape if gather else x.shape[1:]), x.dtype) for x in xs]


def _exchange_sems(nt):
    return [pltpu.SemaphoreType.DMA((nt, N_DEV - 1)), pltpu.SemaphoreType.DMA((nt, N_DEV - 1)),
            pltpu.SemaphoreType.DMA((nt,))]


def _exchange(name, xs, gather):
    nt = len(xs)

    def body(*refs):
        starts, waits = _exchange_copies(refs[:nt], refs[nt:2 * nt], *refs[2 * nt:], gather)
        for f in starts:
            f()
        for f in waits:
            f()

    any_spec = pl.BlockSpec(memory_space=pl.ANY)
    return pl.pallas_call(
        body, name=name, in_specs=[any_spec] * nt, out_specs=[any_spec] * nt,
        out_shape=_exchange_shapes(xs, gather), scratch_shapes=_exchange_sems(nt))(*xs)


def _call(name, body, grid, ins, in_specs, out_specs, out_shape, scratch=(), exch=None):
    if exch is None:
        return pl.pallas_call(body, name=name, grid=grid, in_specs=in_specs, out_specs=out_specs,
                              out_shape=out_shape, scratch_shapes=list(scratch), compiler_params=_params())(*ins)
    xs, gather = exch
    single = not isinstance(out_shape, (list, tuple))
    o_specs = [out_specs] if single else list(out_specs)
    o_shape = [out_shape] if single else list(out_shape)
    n_in, n_out, n_x, n_scr = len(ins), len(o_shape), len(xs), len(scratch)

    def wrapped(*refs):
        in_refs = refs[:n_in]
        x_refs = refs[n_in:n_in + n_x]
        out_refs = refs[n_in + n_x:n_in + n_x + n_out]
        got_refs = refs[n_in + n_x + n_out:n_in + 2 * n_x + n_out]
        rest = refs[n_in + 2 * n_x + n_out:]
        starts, waits = _exchange_copies(x_refs, got_refs, *rest[n_scr:], gather)
        ids = [pl.program_id(i) for i in range(len(grid))]
        first = functools.reduce(lambda p, q: p & q, [i == 0 for i in ids])
        last = functools.reduce(lambda p, q: p & q, [i == g - 1 for i, g in zip(ids, grid)])

        @pl.when(first)
        def _():
            for f in starts:
                f()

        body(*in_refs, *out_refs, *rest[:n_scr])

        @pl.when(last)
        def _():
            for f in waits:
                f()

    any_spec = pl.BlockSpec(memory_space=pl.ANY)
    outs = pl.pallas_call(
        wrapped, name=name, grid=grid, in_specs=list(in_specs) + [any_spec] * n_x,
        out_specs=o_specs + [any_spec] * n_x, out_shape=o_shape + _exchange_shapes(xs, gather),
        scratch_shapes=list(scratch) + _exchange_sems(n_x), compiler_params=_params())(*ins, *xs)
    res = outs[:n_out]
    return (res[0] if single else res), outs[n_out:]


def _mm_call(name, dims, grid, red_axis, ins, in_specs, out_shape, out_spec, acc_shape,
             scale=1.0, add_scale=None, exch=None):
    nred = grid[red_axis]

    def body(*refs):
        if add_scale is None:
            a_ref, b_ref, o_ref, acc_ref = refs
            add_ref = None
        else:
            a_ref, b_ref, add_ref, o_ref, acc_ref = refs
        k = pl.program_id(red_axis)

        @pl.when(k == 0)
        def _():
            acc_ref[...] = jnp.zeros_like(acc_ref)

        acc_ref[...] += _dot(a_ref[...].astype(BF16), b_ref[...].astype(BF16), dims)

        @pl.when(k == nred - 1)
        def _():
            r = acc_ref[...]
            if scale != 1.0:
                r = r * scale
            if add_ref is not None:
                r = r + add_scale * add_ref[...].astype(F32)
            o_ref[...] = r.astype(o_ref.dtype)

    return _call(name, body, grid, ins, in_specs, out_spec, out_shape, [pltpu.VMEM(acc_shape, F32)], exch)


def _tile(dim, cap):
    t = min(dim, cap)
    while dim % t or t % 128:
        t -= 128
        assert t > 0, (dim, cap)
    return t


def _mm(name, a, b, dims, out_dtype, scale=1.0, add=None, add_scale=None, tm=512, tn=1024, tk=1024):
    if dims == "tn":
        kd, m = a.shape
        n = b.shape[1]
    else:
        m, kd = a.shape
        n = b.shape[1] if dims == "nn" else b.shape[0]
    tm, tn, tk = _tile(m, tm), _tile(n, tn), _tile(kd, tk)
    a_spec = (pl.BlockSpec((tk, tm), lambda i, j, k: (k, i)) if dims == "tn"
              else pl.BlockSpec((tm, tk), lambda i, j, k: (i, k)))
    b_spec = (pl.BlockSpec((tn, tk), lambda i, j, k: (j, k)) if dims == "nt"
              else pl.BlockSpec((tk, tn), lambda i, j, k: (k, j)))
    o_spec = pl.BlockSpec((tm, tn), lambda i, j, k: (i, j))
    ins, specs = [a, b], [a_spec, b_spec]
    if add is not None:
        ins.append(add)
        specs.append(o_spec)
    return _mm_call(name, dims, (m // tm, n // tn, kd // tk), 2, ins, specs,
                    jax.ShapeDtypeStruct((m, n), out_dtype), o_spec, (tm, tn),
                    scale=scale, add_scale=add_scale if add is not None else None)


def _ffn_in(name, x, wg, tm=512, exch=None):
    n = x.shape[0]
    nj = N_DEV // 2

    def body(x_ref, wgate_ref, wup_ref, hg_ref, hu_ref, act_ref):
        xb = x_ref[...].astype(BF16)
        g = _dot(xb, wgate_ref[...])
        u = _dot(xb, wup_ref[...])
        hg_ref[...] = g.astype(BF16)
        hu_ref[...] = u.astype(BF16)
        act_ref[...] = (g * _sigmoid(g) * u).astype(BF16)

    blk = pl.BlockSpec((None, tm, FF_BLK), lambda j, i: (j, i, 0))
    shp = jax.ShapeDtypeStruct((nj, n, FF_BLK), BF16)
    return _call(name, body, (nj, n // tm), [x, wg, wg],
                 [pl.BlockSpec((tm, D_MODEL), lambda j, i: (i, 0)),
                  pl.BlockSpec((None, D_MODEL, FF_BLK), lambda j, i: (j, 0, 0)),
                  pl.BlockSpec((None, D_MODEL, FF_BLK), lambda j, i: (j + nj, 0, 0))],
                 [blk, blk, blk], [shp, shp, shp], exch=exch)


def _ffn_out_bwd(name, dz, wout, hg, hu, tm=512, exch=None):
    n = dz.shape[0]
    nj = N_DEV // 2

    def body(dz_ref, w_ref, hg_ref, hu_ref, dhg_ref, dhu_ref):
        dact = 0.5 * _dot(dz_ref[...].astype(BF16), w_ref[...], "nt")
        g = hg_ref[...].astype(F32)
        u = hu_ref[...].astype(F32)
        s = _sigmoid(g)
        dhg_ref[...] = (dact * u * (s * (1.0 + g * (1.0 - s)))).astype(BF16)
        dhu_ref[...] = (dact * (g * s)).astype(BF16)

    blk = pl.BlockSpec((None, tm, FF_BLK), lambda j, i: (j, i, 0))
    shp = jax.ShapeDtypeStruct((nj, n, FF_BLK), BF16)
    return _call(name, body, (nj, n // tm), [dz, wout, hg, hu],
                 [pl.BlockSpec((tm, D_MODEL), lambda j, i: (i, 0)),
                  pl.BlockSpec((FF_BLK, D_MODEL), lambda j, i: (j, 0)), blk, blk],
                 [blk, blk], [shp, shp], exch=exch)


def _mm_ln(name, a, b, xres, g, beta, c, tgt=None, a_blocked=False, tm=512, tk=512):
    if a_blocked:
        nk, n, kb = a.shape
        a_spec = pl.BlockSpec((None, tm, kb), lambda i, k: (k, i, 0))
    else:
        n, kd = a.shape
        kb = _tile(kd, tk)
        nk = kd // kb
        a_spec = pl.BlockSpec((tm, kb), lambda i, k: (i, k))
    d = b.shape[1]
    with_loss = tgt is not None

    def body(*refs):
        if with_loss:
            a_ref, b_ref, x_ref, g_ref, be_ref, t_ref, o_ref, z_ref, l_ref, acc_ref = refs
        else:
            a_ref, b_ref, x_ref, g_ref, be_ref, o_ref, z_ref, acc_ref = refs
        i, k = pl.program_id(0), pl.program_id(1)

        @pl.when(k == 0)
        def _():
            acc_ref[...] = jnp.zeros_like(acc_ref)

        acc_ref[...] += _dot(a_ref[...].astype(BF16), b_ref[...].astype(BF16))

        @pl.when(k == nk - 1)
        def _():
            z = ALPHA * x_ref[...] + c * acc_ref[...]
            z_ref[...] = z
            mu = jnp.mean(z, axis=-1, keepdims=True)
            zc = z - mu
            var = jnp.mean(zc * zc, axis=-1, keepdims=True)
            y = zc * lax.rsqrt(var + LN_EPS) * g_ref[...] + be_ref[...]
            if with_loss:
                err = y - t_ref[...]
                o_ref[...] = err * (1.0 / d)
                part = 0.5 * jnp.sum(jnp.sum(err * err, axis=-1, keepdims=True), axis=0, keepdims=True) * (1.0 / d)

                @pl.when(i == 0)
                def _():
                    l_ref[...] = jnp.zeros_like(l_ref)

                l_ref[...] += jnp.broadcast_to(part, l_ref.shape)
            else:
                o_ref[...] = y

    row = pl.BlockSpec((tm, d), lambda i, k: (i, 0))
    vec = pl.BlockSpec((1, d), lambda i, k: (0, 0))
    ins = [a, b, xres, g, beta]
    in_specs = [a_spec, pl.BlockSpec((kb, d), lambda i, k: (k, 0)), row, vec, vec]
    out_specs = [row, row]
    out_shape = [jax.ShapeDtypeStruct((n, d), F32), jax.ShapeDtypeStruct((n, d), F32)]
    if with_loss:
        ins.append(tgt)
        in_specs.append(row)
        out_specs.append(pl.BlockSpec((1, 128), lambda i, k: (0, 0)))
        out_shape.append(jax.ShapeDtypeStruct((1, 128), F32))
    return _call(name, body, (n // tm, nk), ins, in_specs, out_specs, out_shape, [pltpu.VMEM((tm, d), F32)])


def _rowwise(name, fn, rows, params, out_rows, out_accs, tm=256, exch=None):
    specs, ins = [], []
    for r in rows:
        arr, w, cb = r if isinstance(r, tuple) else (r, r.shape[1], 0)
        ins.append(arr)
        specs.append(pl.BlockSpec((tm, w), functools.partial(lambda i, cb: (i, cb), cb=cb)))
    n = ins[0].shape[0]
    for p in params:
        ins.append(p)
        specs.append(pl.BlockSpec(p.shape, lambda i: (0, 0)))
    n_in, n_or = len(ins), len(out_rows)

    def body(*refs):
        outs = fn(*[r[...] for r in refs[:n_in]])
        o_refs = refs[n_in:]
        for o_ref, o in zip(o_refs[:n_or], outs[:n_or]):
            o_ref[...] = o.astype(o_ref.dtype)
        if out_accs:
            @pl.when(pl.program_id(0) == 0)
            def _():
                for a_ref in o_refs[n_or:]:
                    a_ref[...] = jnp.zeros_like(a_ref)

            for a_ref, a in zip(o_refs[n_or:], outs[n_or:]):
                a_ref[...] += a.astype(F32)

    out_specs = [pl.BlockSpec((tm, w), lambda i: (i, 0)) for w, _ in out_rows]
    out_specs += [pl.BlockSpec(s, lambda i: (0, 0)) for s in out_accs]
    out_shape = [jax.ShapeDtypeStruct((n, w), dt) for w, dt in out_rows]
    out_shape += [jax.ShapeDtypeStruct(s, F32) for s in out_accs]
    return _call(name, body, (n // tm,), ins, specs, out_specs, out_shape, exch=exch)


def _vjp_of(fn, n_in):
    def g(*args):
        ins, cts = args[:n_in], args[n_in:]
        outs, pull = jax.vjp(fn, *ins)
        return pull(tuple(c.astype(o.dtype) for c, o in zip(cts, outs)))
    return g


def _ln_bwd(name, z, g, ct):
    def fn(zt, ct_, gt):
        mu = jnp.mean(zt, axis=-1, keepdims=True)
        zc = zt - mu
        rstd = lax.rsqrt(jnp.mean(zc * zc, axis=-1, keepdims=True) + LN_EPS)
        xh = zc * rstd
        dxh = ct_ * gt
        dz = rstd * (dxh - jnp.mean(dxh, axis=-1, keepdims=True)
                     - xh * jnp.mean(dxh * xh, axis=-1, keepdims=True))
        return dz, jnp.sum(ct_ * xh, axis=0, keepdims=True), jnp.sum(ct_, axis=0, keepdims=True)

    d = z.shape[1]
    return _rowwise(name, fn, [z, ct], [g], [(d, F32)], [(1, d), (1, d)])


def _shift(name, src, mu_a, mu_b, q=None, tt=256):
    bsz, t, _ = src.shape
    nt, r8, w = t // tt, tt // 8, SHIFT_PAD
    with_q = q is not None

    def body(cur_ref, prev_ref, next_ref, *rest):
        b, i = pl.program_id(0), pl.program_id(1)
        cur = cur_ref[...]
        prow = jnp.where(i > 0, prev_ref[7:8, :], 0.0)
        nrow = jnp.where(i < nt - 1, next_ref[0:1, :], 0.0)
        rid = lax.broadcasted_iota(jnp.int32, cur.shape, 0)
        dprev = jnp.where(rid == 0, prow, pltpu.roll(cur, 1, 0)) - cur
        dnext = jnp.where(rid == tt - 1, nrow, pltpu.roll(cur, tt - 1, 0)) - cur
        if with_q:
            q_ref, da_ref, db_ref = rest

            @pl.when((b == 0) & (i == 0))
            def _():
                da_ref[...] = jnp.zeros_like(da_ref)
                db_ref[...] = jnp.zeros_like(db_ref)

            qv = q_ref[...]
            da_ref[...] += jnp.sum(qv * dprev, axis=0, keepdims=True)
            db_ref[...] += jnp.sum(qv * dnext, axis=0, keepdims=True)
        else:
            ma_ref, mb_ref, o_ref = rest
            o_ref[...] = cur + ma_ref[...] * dprev + mb_ref[...] * dnext

    cur_spec = pl.BlockSpec((None, tt, w), lambda b, i: (b, i, 0))
    in_specs = [cur_spec,
                pl.BlockSpec((None, 8, w), lambda b, i: (b, jnp.maximum(i * r8 - 1, 0), 0)),
                pl.BlockSpec((None, 8, w), lambda b, i: (b, jnp.minimum((i + 1) * r8, t // 8 - 1), 0))]
    vec = pl.BlockSpec((1, w), lambda b, i: (0, 0))
    if with_q:
        return _call(name, body, (bsz, nt), [src, src, src, q], in_specs + [cur_spec], [vec, vec],
                     [jax.ShapeDtypeStruct((1, w), F32)] * 2)
    return _call(name, body, (bsz, nt), [src, src, src, mu_a, mu_b], in_specs + [vec, vec], cur_spec,
                 jax.ShapeDtypeStruct((bsz, t, w), F32))


def _halo_specs(t, tt, w):
    r16 = tt // 16
    return [pl.BlockSpec((None, tt, w), lambda b, i: (b, i, 0)),
            pl.BlockSpec((None, 16, w), lambda b, i: (b, jnp.maximum(i * r16 - 1, 0), 0)),
            pl.BlockSpec((None, 16, w), lambda b, i: (b, jnp.minimum((i + 1) * r16, t // 16 - 1), 0))]


def _fill_pad(pad_ref, cur_ref, prev_ref, next_ref, i, nt, tt):
    pad_ref[0:16, :] = jnp.where(i > 0, prev_ref[...], 0.0)
    pad_ref[16:16 + tt, :] = cur_ref[...]
    pad_ref[16 + tt:32 + tt, :] = jnp.where(i < nt - 1, next_ref[...], 0.0)


def _dwconv(name, u, dw32, bias, flip, tt=512):
    bsz, t, w = u.shape
    tt = min(tt, t)
    nt = t // tt

    def body(cur_ref, prev_ref, next_ref, dw_ref, b_ref, o_ref, pad_ref):
        i = pl.program_id(1)
        _fill_pad(pad_ref, cur_ref, prev_ref, next_ref, i, nt, tt)
        acc = jnp.broadcast_to(b_ref[...], (tt, w))
        for k in range(CONV_K):
            kk = CONV_K - 1 - k if flip else k
            acc = acc + pad_ref[pl.ds(1 + k, tt), :] * dw_ref[kk:kk + 1, :]
        o_ref[...] = acc

    return _call(name, body, (bsz, nt), [u, u, u, dw32, bias],
                 _halo_specs(t, tt, w) + [pl.BlockSpec((32, w), lambda b, i: (0, 0)),
                                          pl.BlockSpec((1, w), lambda b, i: (0, 0))],
                 pl.BlockSpec((None, tt, w), lambda b, i: (b, i, 0)), jax.ShapeDtypeStruct((bsz, t, w), F32),
                 [pltpu.VMEM((tt + 32, w), F32)])


def _dwconv_dw(name, u, dc, tt=512):
    bsz, t, w = u.shape
    tt = min(tt, t)
    nt = t // tt

    def body(cur_ref, prev_ref, next_ref, dc_ref, ddw_ref, db_ref, pad_ref):
        b, i = pl.program_id(0), pl.program_id(1)
        _fill_pad(pad_ref, cur_ref, prev_ref, next_ref, i, nt, tt)

        @pl.when((b == 0) & (i == 0))
        def _():
            ddw_ref[...] = jnp.zeros_like(ddw_ref)
            db_ref[...] = jnp.zeros_like(db_ref)

        dcv = dc_ref[...]
        db_ref[...] += jnp.sum(dcv, axis=0, keepdims=True)
        for k in range(CONV_K):
            ddw_ref[k:k + 1, :] += jnp.sum(dcv * pad_ref[pl.ds(1 + k, tt), :], axis=0, keepdims=True)

    return _call(name, body, (bsz, nt), [u, u, u, dc],
                 _halo_specs(t, tt, w) + [pl.BlockSpec((None, tt, w), lambda b, i: (b, i, 0))],
                 [pl.BlockSpec((32, w), lambda b, i: (0, 0)), pl.BlockSpec((1, w), lambda b, i: (0, 0))],
                 [jax.ShapeDtypeStruct((32, w), F32), jax.ShapeDtypeStruct((1, w), F32)],
                 [pltpu.VMEM((tt + 32, w), F32)])


PAIR = 2 * HEAD
N_PAIRS = RW // PAIR


def _chunk_pairs(s, r, lw, k, v, kk, a, sgn):
    n, m = CHUNK, 2 * CHUNK
    in_a = lax.broadcasted_iota(jnp.int32, (n, PAIR), 1) < HEAD

    def stack2(z):
        return jnp.concatenate([jnp.where(in_a, z, 0.0), jnp.where(in_a, 0.0, z)], axis=0)

    def each(f, *lists):
        return [f(*z) for z in zip(*lists)]

    row = lax.broadcasted_iota(jnp.int32, (n, n), 0)
    col = lax.broadcasted_iota(jnp.int32, (n, n), 1)
    tri = jnp.where((row - col) * sgn >= 0, 1.0, 0.0)
    row2 = lax.broadcasted_iota(jnp.int32, (m, m), 0)
    col2 = lax.broadcasted_iota(jnp.int32, (m, m), 1)
    same = (row2 >= n) == (col2 >= n)
    dlt = ((row2 & (n - 1)) - (col2 & (n - 1))) * sgn
    incl, strict = same & (dlt >= 0), same & (dlt > 0)
    eye = jnp.where(row2 == col2, 1.0, 0.0)

    cum = each(lambda lw_: _dot3(tri, lw_), lw)
    tot = each(lambda lw_: jnp.sum(lw_, axis=0, keepdims=True), lw)
    e_neg = each(lambda c_: jnp.exp(-c_), cum)
    e_rest = each(lambda t_, c_: jnp.exp(t_ - c_), tot, cum)
    beta = each(lambda kk_, a_: kk_ * a_, kk, a)
    lhs = each(lambda kk_, c_, lw_, r_: jnp.concatenate(
        [stack2(-kk_ * jnp.exp(c_ - lw_)), stack2(r_ * jnp.exp(c_))], axis=0), kk, cum, lw, r)
    rhs = each(lambda b_, k_, e_: jnp.concatenate([stack2(b_ * e_), stack2(k_ * e_)], axis=0), beta, k, e_neg)
    sc = each(lambda l_, r_: _dot3(l_, r_, "nt"), lhs, rhs)
    l_ab = each(lambda sc_: jnp.where(strict, sc_[0:m, 0:m], 0.0), sc)
    l_ak = each(lambda sc_: jnp.where(strict, sc_[0:m, m:2 * m], 0.0), sc)
    m_r = each(lambda sc_: jnp.where(jnp.concatenate([incl, incl], axis=1), sc_[m:2 * m, :], 0.0), sc)
    tm = each(lambda l_: eye + l_, l_ab)
    lp = l_ab
    for _ in range(int(math.log2(n)) - 1):
        lp = each(lambda l_: _dot3(l_, l_), lp)
        tm = each(lambda t_, l_: t_ + _dot3(t_, l_), tm, lp)
    z = each(lambda l_, s_: _dot3(l_, s_, "nt"), lhs, s)
    v2 = each(stack2, v)
    u2 = each(lambda t_, z_, l_, v_: _dot3(t_, z_[0:m] + _dot3(l_, v_)), tm, z, l_ak, v2)
    uv = each(lambda u_, v_: jnp.concatenate([u_, v_], axis=0), u2, v2)
    y2 = each(lambda z_, m_, uv_: z_[m:2 * m] + _dot3(m_, uv_), z, m_r, uv)
    s_new = each(lambda s_, t_, uv_, b_, k_, e_: s_ * jnp.exp(t_) + _dot3(
        uv_, jnp.concatenate([stack2(b_ * e_), stack2(k_ * e_)], axis=0), "tn"), s, tot, uv, beta, k, e_rest)
    return each(lambda y_: y_[0:n] + y_[n:m], y2), s_new


def _pair_tiles(ref):
    return [ref[:, p * PAIR:(p + 1) * PAIR] for p in range(N_PAIRS)]


def _scan_specs(nc, order):
    shared = pl.BlockSpec((CHUNK, RW), lambda d, b, c: (b * nc + order(d, c), 0))
    per_dir = pl.BlockSpec((CHUNK, RW), lambda d, b, c: (b * nc + order(d, c), d))
    state = pl.BlockSpec((None, None, N_PAIRS, PAIR, PAIR), lambda d, b, c: (d, b * nc + order(d, c), 0, 0, 0))
    return shared, per_dir, state


def _scan_fwd(r, v, kk, lw, kd, a, bsz, exch=None):
    n = r.shape[0]
    nc = n // bsz // CHUNK

    def order(d, c):
        return c + d * (nc - 1 - 2 * c)

    def body(r_ref, v_ref, kk_ref, lw_ref, kd_ref, a_ref, y_ref, s0_ref, s_ref):
        d, c = pl.program_id(0), pl.program_id(2)

        @pl.when(c == 0)
        def _():
            s_ref[...] = jnp.zeros_like(s_ref)

        s = [s_ref[p] for p in range(N_PAIRS)]
        y, s_new = _chunk_pairs(s, *[_pair_tiles(ref) for ref in (r_ref, lw_ref, kd_ref, v_ref, kk_ref, a_ref)],
                                1 - 2 * d)
        for p in range(N_PAIRS):
            s0_ref[p] = s[p]
            y_ref[:, p * PAIR:(p + 1) * PAIR] = y[p]
            s_ref[p] = s_new[p]

    shared, per_dir, state = _scan_specs(nc, order)
    return _call("scan_fwd", body, (2, bsz, nc), [r, v, kk, lw, kd, a],
                 [shared, shared, shared, per_dir, per_dir, per_dir], [per_dir, state],
                 [jax.ShapeDtypeStruct((n, 2 * RW), F32),
                  jax.ShapeDtypeStruct((2, n // CHUNK, N_PAIRS, PAIR, PAIR), F32)],
                 [pltpu.VMEM((N_PAIRS, PAIR, PAIR), F32)], exch)


def _scan_bwd(r, v, kk, lw, kd, a, s0, dy, bsz, exch=None):
    n = r.shape[0]
    nc = n // bsz // CHUNK

    def order(d, c):
        cc = nc - 1 - c
        return cc + d * (nc - 1 - 2 * cc)

    def body(r_ref, v_ref, kk_ref, lw_ref, kd_ref, a_ref, dy_ref, s0_ref,
             dr_ref, dv_ref, dkk_ref, dlw_ref, dkd_ref, da_ref, ds_ref):
        d, c = pl.program_id(0), pl.program_id(2)

        @pl.when(c == 0)
        def _():
            ds_ref[...] = jnp.zeros_like(ds_ref)

        sgn = 1 - 2 * d
        _, pull = jax.vjp(lambda *ops: _chunk_pairs(*ops, sgn), [s0_ref[p] for p in range(N_PAIRS)],
                          *[_pair_tiles(ref) for ref in (r_ref, lw_ref, kd_ref, v_ref, kk_ref, a_ref)])
        grads = pull((_pair_tiles(dy_ref), [ds_ref[p] for p in range(N_PAIRS)]))
        for p in range(N_PAIRS):
            ds_ref[p] = grads[0][p]
            for o_ref, gx in zip((dr_ref, dlw_ref, dkd_ref, dv_ref, dkk_ref, da_ref), grads[1:]):
                o_ref[:, p * PAIR:(p + 1) * PAIR] = gx[p]

    shared, per_dir, state = _scan_specs(nc, order)
    shp = jax.ShapeDtypeStruct((n, 2 * RW), F32)
    return _call("scan_bwd", body, (2, bsz, nc), [r, v, kk, lw, kd, a, dy, s0],
                 [shared, shared, shared, per_dir, per_dir, per_dir, per_dir, state],
                 [per_dir] * 6, [shp] * 6, [pltpu.VMEM((N_PAIRS, PAIR, PAIR), F32)], exch)


def _prep_fn(ps, w0, w2bd, a0, a2bd, g2p, k_k, k_a, hsum):
    head_sum = lambda z: _head_sum(z, hsum)
    r, k, v = ps[:, 0:RW], ps[:, RW:2 * RW], ps[:, 2 * RW:3 * RW]
    wd, ad, gd = ps[:, 1536:1664], ps[:, 1664:1792], ps[:, 1792:2048]
    logw = -DECAY_SCALE * _sigmoid(_dot3(jnp.tanh(wd), w2bd) + w0)
    a = _sigmoid(_dot3(ad, a2bd) + a0)
    g = _dot3(_sigmoid(gd), g2p)
    kkr = k * k_k
    kk = kkr / jnp.maximum(jnp.sqrt(head_sum(kkr * kkr)), NORM_EPS)
    k2 = jnp.concatenate([k, k], axis=1)
    ka2 = jnp.concatenate([k_a, k_a], axis=1)
    kd = k2 * (1.0 + (a - 1.0) * ka2)
    return r, v, kk, logw, a, kd, g


def _post_fn(y2, r, v, kd, g, lnx_g, lnx_b, r_k, hsum):
    head_sum = lambda z: _head_sum(z, hsum)
    y = y2[:, 0:RW] + y2[:, RW:2 * RW]
    mu = head_sum(y) * (1.0 / HEAD)
    yc = y - mu
    var = head_sum(yc * yc) * (1.0 / HEAD)
    yn = yc * lax.rsqrt(var + GN_EPS) * lnx_g + lnx_b
    bonus = head_sum(r * (kd[:, 0:RW] + kd[:, RW:2 * RW]) * r_k) * v
    return ((yn + bonus) * g,)


def _glu_fn(pa, pb):
    return (pa * _sigmoid(pb),)


def _conv_out_fn(cv, ln_g, ln_b):
    mu = jnp.mean(cv, axis=-1, keepdims=True)
    cc = cv - mu
    var = jnp.mean(cc * cc, axis=-1, keepdims=True)
    y = cc * lax.rsqrt(var + LN_EPS) * ln_g + ln_b
    return (y * _sigmoid(y),)


def _local_step(x, tgt, w, ex=None):
    bsz, t, d = x.shape
    n = bsz * t
    x2d, tgt2d = x.reshape(n, d), tgt.reshape(n, d)
    hsum = jnp.kron(jnp.eye(N_HEADS, dtype=BF16), jnp.ones((HEAD, HEAD), BF16))
    w = dict(w)
    parts = {} if ex else None

    def hosted(result, finish=None):
        if not ex:
            return result
        outs, got = result
        if finish is not None:
            w.update(finish(got))
        return outs

    hg1, hu1, act1 = hosted(_ffn_in("ffn1_in", x2d, w["ffn1_w_in"], exch=(ex["g1"][0], True) if ex else None),
                            ex["g1"][1] if ex else None)
    x1, z1 = _mm_ln("ffn1_out_ln1", act1, w["ffn1_w_out"], x2d, w["ln1_g"], w["ln1_b"], 0.5, a_blocked=True)
    p = _mm("w_in_proj", x1, w["w_in"], "nn", F32)
    p3 = p.reshape(bsz, t, IN_PAD)
    ps = _shift("shift_fwd", p3, w["mu_prev"], w["mu_next"]).reshape(n, SHIFT_PAD)
    prep_params = [w["w0"], w["w2"], w["a0"], w["a2"], w["g2"], w["k_k"], w["k_a"], hsum]
    r, v, kk, logw, a, kd, g = _rowwise(
        "rwkv_prep", _prep_fn, [ps], prep_params,
        [(RW, F32), (RW, F32), (RW, F32), (2 * RW, F32), (2 * RW, F32), (2 * RW, F32), (RW, F32)], [])
    y2, s0 = hosted(_scan_fwd(r, v, kk, logw, kd, a, bsz, exch=(ex["g2"][0], True) if ex else None),
                    ex["g2"][1] if ex else None)
    post_params = [w["lnx_g"], w["lnx_b"], w["r_k"], hsum]
    (y_rwkv,) = _rowwise("rwkv_post", _post_fn, [y2, r, v, kd, g], post_params, [(RW, BF16)], [])
    (u,) = _rowwise("conv_glu", _glu_fn, [(p, CW, 4), (p, CW, 5)], [], [(CW, F32)], [])
    cv = _dwconv("conv_dw", u.reshape(bsz, t, CW), w["conv_dw"], w["conv_b"], False).reshape(n, CW)
    (y_conv,) = _rowwise("conv_out", _conv_out_fn, [cv], [w["conv_ln_g"], w["conv_ln_b"]], [(CW, BF16)], [])
    ycat = jnp.concatenate([y_rwkv, y_conv], axis=1)
    x2, z2 = _mm_ln("w_out_ln2", ycat, w["w_out"], x1, w["ln2_g"], w["ln2_b"], 1.0)
    hg2, hu2, act2 = _ffn_in("ffn2_in", x2, w["ffn2_w_in"])
    dx3, z3, loss = _mm_ln("ffn2_out_ln3", act2, w["ffn2_w_out"], x2, w["ln3_g"], w["ln3_b"], 0.5,
                           tgt=tgt2d, a_blocked=True)

    gr = {}
    dz3, gr["ln3_g"], gr["ln3_b"] = _ln_bwd("ln3_bwd", z3, w["ln3_g"], dx3)
    dx2, _ = _ffn_bwd("ffn2", gr, dz3, x2, w["ffn2_w_in"], w["ffn2_w_out"], hg2, hu2, act2)
    dz2, gr["ln2_g"], gr["ln2_b"] = _ln_bwd("ln2_bwd", z2, w["ln2_g"], dx2)
    gr["w_out"] = _mm("w_out_wgrad", ycat, dz2, "tn", BF16, tk=512)
    dycat = _mm("w_out_dgrad", dz2, w["w_out"], "nt", F32)
    conv_out_bwd = _vjp_of(_conv_out_fn, 3)
    dcv, gr["conv_ln_g"], gr["conv_ln_b"] = _rowwise(
        "conv_out_bwd", lambda cv_, ct_, g_, b_: conv_out_bwd(cv_, g_, b_, ct_),
        [cv, (dycat, CW, 1)], [w["conv_ln_g"], w["conv_ln_b"]], [(CW, F32)], [(1, CW), (1, CW)])
    dcv3 = dcv.reshape(bsz, t, CW)
    gr["conv_dw"], gr["conv_b"] = _dwconv_dw("conv_dw_wgrad", u.reshape(bsz, t, CW), dcv3)
    du = _dwconv("conv_dw_dgrad", dcv3, w["conv_dw"], jnp.zeros((1, CW), F32), True).reshape(n, CW)

    def glu_bwd(pa, pb, ct):
        return (jnp.concatenate(_vjp_of(_glu_fn, 2)(pa, pb, ct), axis=1),)

    (dp_conv,) = _rowwise("conv_glu_bwd", glu_bwd, [(p, CW, 4), (p, CW, 5), du], [], [(2 * CW, F32)], [])

    def post_bwd(y2_, r_, v_, kd_, g_, ct, lg, lb, rk, hs):
        return _vjp_of(lambda *z: _post_fn(*z, hs), 8)(y2_, r_, v_, kd_, g_, lg, lb, rk, ct)

    dy2, dr_post, dv_post, dkd_post, dg, gr["lnx_g"], gr["lnx_b"], gr["r_k"] = _rowwise(
        "rwkv_post_bwd", post_bwd, [y2, r, v, kd, g, (dycat, RW, 0)], post_params,
        [(2 * RW, F32), (RW, F32), (RW, F32), (2 * RW, F32), (RW, F32)], [(1, RW), (1, RW), (1, RW)])
    sends = [jnp.concatenate(gr["ffn2_w_in"], axis=0), gr["ffn2_w_out"].reshape(N_DEV, D_FF // N_DEV, D_MODEL),
             gr["w_out"].reshape(N_DEV, D_MODEL // N_DEV, D_MODEL)]
    res = _scan_bwd(r, v, kk, logw, kd, a, s0, dy2, bsz, exch=(sends, False) if ex else None)
    if ex:
        res, got = res
        parts.update(zip(("ffn2_w_in", "ffn2_w_out", "w_out"), got))
    dr_s, dv_s, dkk_s, dlw, dkd_s, da = res

    def prep_bwd(ps_, dr2, dr1, dv2, dv1, dkk2, dlw_, da_, dkd2, dkd1, dg_, *prm):
        half = lambda z: z[:, 0:RW] + z[:, RW:2 * RW]
        return _vjp_of(lambda *z: _prep_fn(*z, prm[-1]), 8)(
            ps_, *prm[:-1], half(dr2) + dr1, half(dv2) + dv1, half(dkk2), dlw_, da_, dkd2 + dkd1, dg_)

    dps, gr["w0"], gr["w2"], gr["a0"], gr["a2"], gr["g2"], gr["k_k"], gr["k_a"] = _rowwise(
        "rwkv_prep_bwd", prep_bwd,
        [ps, dr_s, dr_post, dv_s, dv_post, dkk_s, dlw, da, dkd_s, dkd_post, dg], prep_params,
        [(SHIFT_PAD, F32)], [q.shape for q in prep_params[:-1]], tm=128)
    dps3 = dps.reshape(bsz, t, SHIFT_PAD)
    gr["mu_prev"], gr["mu_next"] = _shift("shift_dmu", p3, None, None, q=dps3)
    dp_shift = _shift("shift_bwd", dps3, w["mu_next"], w["mu_prev"]).reshape(n, SHIFT_PAD)
    dp = jnp.concatenate([dp_shift, dp_conv], axis=1)
    gr["w_in"] = _mm("w_in_wgrad", x1, dp, "tn", BF16, tk=512)
    dx1 = _mm("w_in_dgrad", dp, w["w_in"], "nt", F32, add=dz2, add_scale=ALPHA)
    dz1, gr["ln1_g"], gr["ln1_b"] = _ln_bwd("ln1_bwd", z1, w["ln1_g"], dx1)
    riders = None
    if ex:
        gw_in = _unpad_in_cols(gr["w_in"]).reshape(D_MODEL, N_DEV, IN_COLS // N_DEV).transpose(1, 0, 2)
        small = _pack([_grad_small(nm, gr[nm]) for nm in SMALL_SHARDED + SMALL_REPL])
        riders = {"out_dgrad": ([gw_in], False), "out_wgrad": ([small], True)}
    grad_x, got = _ffn_bwd("ffn1", gr, dz1, x2d, w["ffn1_w_in"], w["ffn1_w_out"], hg1, hu1, act1, riders)
    if ex:
        parts.update(w_in=got["out_dgrad"][0], small=got["out_wgrad"][0],
                     ffn1_w_out=got["w_out"], ffn1_w_in=got["w_in"])
    return loss[0, 0], grad_x.reshape(bsz, t, d), gr, parts


def _ffn_bwd(tag, gr, dz, xin, wg, wout, hg, hu, act, riders=None, tk=512):
    n = xin.shape[0]
    nj = N_DEV // 2
    nt = n // tk
    got = {}

    def hosted(key, result, on):
        if not on:
            return result
        got[key] = result[1][0] if key in ("w_out", "w_in") else result[1]
        return result[0]

    riders = riders or {}
    own = bool(riders)
    dhg, dhu = hosted("out_dgrad", _ffn_out_bwd(tag + "_out_dgrad", dz, wout, hg, hu, exch=riders.get("out_dgrad")),
                      "out_dgrad" in riders)
    gr[tag + "_w_out"] = hosted("out_wgrad", _mm_call(
        tag + "_out_wgrad", "tn", (nj, nt), 1, [act, dz],
        [pl.BlockSpec((None, tk, FF_BLK), lambda j, i: (j, i, 0)), pl.BlockSpec((tk, D_MODEL), lambda j, i: (i, 0))],
        jax.ShapeDtypeStruct((D_FF, D_MODEL), BF16), pl.BlockSpec((FF_BLK, D_MODEL), lambda j, i: (j, 0)),
        (FF_BLK, D_MODEL), scale=0.5, exch=riders.get("out_wgrad")), "out_wgrad" in riders)
    dw = []
    for nm, dh in (("gate", dhg), ("up", dhu)):
        send = ([gr[tag + "_w_out"].reshape(N_DEV, D_FF // N_DEV, D_MODEL)], False) if own and nm == "gate" else None
        dw.append(hosted("w_out", _mm_call(
            tag + "_in_wgrad_" + nm, "tn", (nj, nt), 1, [xin, dh],
            [pl.BlockSpec((tk, D_MODEL), lambda j, i: (i, 0)), pl.BlockSpec((None, tk, FF_BLK), lambda j, i: (j, i, 0))],
            jax.ShapeDtypeStruct((nj, D_MODEL, FF_BLK), BF16), pl.BlockSpec((None, D_MODEL, FF_BLK), lambda j, i: (j, 0, 0)),
            (D_MODEL, FF_BLK), exch=send), send is not None))
    gr[tag + "_w_in"] = dw
    dx = dz
    tm = 512
    for off, (nm, dh) in enumerate((("gate", dhg), ("up", dhu))):
        send = ([jnp.concatenate(dw, axis=0)], False) if own and nm == "gate" else None
        dx = hosted("w_in", _mm_call(
            tag + "_in_dgrad_" + nm, "nt", (n // tm, nj), 1, [dh, wg, dx],
            [pl.BlockSpec((None, tm, FF_BLK), lambda i, j: (j, i, 0)),
             pl.BlockSpec((None, D_MODEL, FF_BLK), functools.partial(lambda i, j, o: (j + o, 0, 0), o=off * nj)),
             pl.BlockSpec((tm, D_MODEL), lambda i, j: (i, 0))],
            jax.ShapeDtypeStruct((n, D_MODEL), F32), pl.BlockSpec((tm, D_MODEL), lambda i, j: (i, 0)),
            (tm, D_MODEL), add_scale=ALPHA if off == 0 else 1.0, exch=send), send is not None)
    return dx, got


def _adam_math(g, w, m, v):
    m = ADAM_B1 * m + (1.0 - ADAM_B1) * g
    v = ADAM_B2 * v + (1.0 - ADAM_B2) * (g * g)
    m_hat = m / (1.0 - ADAM_B1 ** ADAM_STEP)
    v_hat = v / (1.0 - ADAM_B2 ** ADAM_STEP)
    delta = -ADAM_LR * (m_hat / (jnp.sqrt(v_hat) + ADAM_EPS) + ADAM_WD * w)
    return delta, m, v


def _adam(name, parts, w, m, v, tr=128):
    rows, cols = w.shape
    tr = min(tr, rows)
    while rows % tr:
        tr -= 8

    def body(p_ref, w_ref, m_ref, v_ref, g_ref, d_ref, mo_ref, vo_ref):
        g = p_ref[0].astype(F32)
        for s in range(1, N_DEV):
            g = g + p_ref[s].astype(F32)
        g_ref[...] = g
        d_ref[...], mo_ref[...], vo_ref[...] = _adam_math(g, w_ref[...], m_ref[...], v_ref[...])

    blk = pl.BlockSpec((tr, cols), lambda i: (i, 0))
    shp = jax.ShapeDtypeStruct((rows, cols), F32)
    return _call(name, body, (rows // tr,), [parts, w, m, v],
                 [pl.BlockSpec((N_DEV, tr, cols), lambda i: (0, i, 0)), blk, blk, blk], [blk] * 4, [shp] * 4)


def _sum8(name, parts):
    _, rows, cols = parts.shape

    def body(p_ref, o_ref):
        g = p_ref[0]
        for s in range(1, N_DEV):
            g = g + p_ref[s]
        o_ref[...] = g

    return pl.pallas_call(body, name=name, out_shape=jax.ShapeDtypeStruct((rows, cols), F32),
                          compiler_params=_params())(parts)


def _adam_small(name, g, w, m, v):
    def body(g_ref, w_ref, m_ref, v_ref, d_ref, mo_ref, vo_ref):
        d_ref[...], mo_ref[...], vo_ref[...] = _adam_math(g_ref[...], w_ref[...], m_ref[...], v_ref[...])

    shp = jax.ShapeDtypeStruct(g.shape, F32)
    return pl.pallas_call(body, name=name, out_shape=[shp] * 3, compiler_params=_params())(g, w, m, v)


def _pack(arrs, lane=128):
    flat = jnp.concatenate([a.reshape(-1).astype(F32) for a in arrs])
    pad = (-flat.shape[0]) % (8 * lane)
    return jnp.pad(flat, (0, pad)).reshape(-1, lane)


def _unpack(packed, shapes):
    flat, out, off = packed.reshape(-1), [], 0
    for s in shapes:
        sz = math.prod(s)
        out.append(flat[off:off + sz].reshape(s))
        off += sz
    return out


def _pad_in_cols(wfull):
    zeros = jnp.zeros((wfull.shape[0], SHIFT_PAD - SHIFT_COLS), wfull.dtype)
    return jnp.concatenate([wfull[:, :SHIFT_COLS], zeros, wfull[:, SHIFT_COLS:]], axis=1)


def _unpad_in_cols(gfull):
    return jnp.concatenate([gfull[:, :SHIFT_COLS], gfull[:, SHIFT_PAD:]], axis=1)


def _block_diag2(wd):
    z = jnp.zeros_like(wd[0])
    return jnp.concatenate([jnp.concatenate([wd[0], z], axis=1), jnp.concatenate([z, wd[1]], axis=1)], axis=0)


def _unblock_diag2(g):
    return jnp.stack([g[0:64, 0:RW], g[64:128, RW:2 * RW]])


SMALL_SHARDED = ("w0", "w2", "a0", "a2", "g2", "conv_dw")
SMALL_REPL = ("mu_prev", "mu_next", "k_k", "k_a", "r_k", "lnx_g", "lnx_b", "conv_b", "conv_ln_g", "conv_ln_b",
              "ln1_g", "ln1_b", "ln2_g", "ln2_b", "ln3_g", "ln3_b")
BIG = ("ffn1_w_in", "ffn1_w_out", "w_in", "w_out", "ffn2_w_in", "ffn2_w_out")
WEIGHTS = ("ffn1_w_in", "ffn1_w_out", "w_in", "mu_prev", "mu_next", "w0", "w2", "a0", "a2", "g2", "k_k", "k_a",
           "r_k", "lnx_g", "lnx_b", "conv_dw", "conv_b", "conv_ln_g", "conv_ln_b", "w_out", "ffn2_w_in",
           "ffn2_w_out", "ln1_g", "ln1_b", "ln2_g", "ln2_b", "ln3_g", "ln3_b")


def _full_small(name, full):
    if name in ("w0", "a0"):
        return full.reshape(1, 2 * RW)
    if name in ("w2", "a2"):
        return _block_diag2(full)
    if name == "g2":
        return jnp.pad(full, ((0, 256 - GATE_LORA), (0, 0)))
    if name == "conv_dw":
        return jnp.pad(full, ((0, 1), (0, 0)))
    if name in ("mu_prev", "mu_next"):
        return jnp.pad(full.reshape(1, SHIFT_COLS), ((0, 0), (0, SHIFT_PAD - SHIFT_COLS)))
    return full.reshape(1, -1)


def _grad_small(name, g):
    if name in ("w0", "a0"):
        return g.reshape(2, RW)
    if name in ("w2", "a2"):
        return _unblock_diag2(g)
    if name == "g2":
        return g[:GATE_LORA]
    if name == "conv_dw":
        return g[:CONV_K]
    if name in ("mu_prev", "mu_next"):
        return g[0, :SHIFT_COLS]
    if name == "r_k":
        return g.reshape(N_HEADS, HEAD)
    return g.reshape(-1)


def kernel(x, ffn1_w_in, ffn1_w_out, w_in, mu_prev, mu_next, w0, w2, a0, a2, g2, k_k, k_a, r_k, lnx_g, lnx_b, conv_dw, conv_b, conv_ln_g, conv_ln_b, w_out, ffn2_w_in, ffn2_w_out, ln1_g, ln1_b, ln2_g, ln2_b, ln3_g, ln3_b, loss_target, m_ffn1_w_in, m_ffn1_w_out, m_w_in, m_mu_prev, m_mu_next, m_w0, m_w2, m_a0, m_a2, m_g2, m_k_k, m_k_a, m_r_k, m_lnx_g, m_lnx_b, m_conv_dw, m_conv_b, m_conv_ln_g, m_conv_ln_b, m_w_out, m_ffn2_w_in, m_ffn2_w_out, m_ln1_g, m_ln1_b, m_ln2_g, m_ln2_b, m_ln3_g, m_ln3_b, v_ffn1_w_in, v_ffn1_w_out, v_w_in, v_mu_prev, v_mu_next, v_w0, v_w2, v_a0, v_a2, v_g2, v_k_k, v_k_a, v_r_k, v_lnx_g, v_lnx_b, v_conv_dw, v_conv_b, v_conv_ln_g, v_conv_ln_b, v_w_out, v_ffn2_w_in, v_ffn2_w_out, v_ln1_g, v_ln1_b, v_ln2_g, v_ln2_b, v_ln3_g, v_ln3_b):
    args = dict(locals())
    wsh = {n: args[n][0] for n in WEIGHTS}
    msh = {n: args["m_" + n][0] for n in WEIGHTS}
    vsh = {n: args["v_" + n][0] for n in WEIGHTS}
    me = 4 * lax.axis_index("x") + 2 * lax.axis_index("y") + lax.axis_index("c")
    bf = {n: wsh[n].astype(BF16) for n in BIG}

    f1_in, f1_out = _exchange("gather_ffn1", [bf["ffn1_w_in"], bf["ffn1_w_out"]], True)
    w = {"ffn1_w_in": f1_in, "ffn1_w_out": f1_out.reshape(D_FF, D_MODEL)}
    for n in SMALL_REPL:
        w[n] = _full_small(n, wsh[n])
    small_shapes = [wsh[n].shape for n in SMALL_SHARDED]

    def finish1(got):
        w_in_g, small = got
        cols = zip(*[_unpack(small[dv], small_shapes) for dv in range(N_DEV)])
        out = {n: _full_small(n, jnp.concatenate(s, axis=-1)) for n, s in zip(SMALL_SHARDED, cols)}
        out["w_in"] = _pad_in_cols(w_in_g.transpose(1, 0, 2).reshape(D_MODEL, IN_COLS))
        return out

    def finish2(got):
        w_out_g, f2_in, f2_out = got
        return {"w_out": w_out_g.reshape(D_MODEL, D_MODEL), "ffn2_w_in": f2_in,
                "ffn2_w_out": f2_out.reshape(D_FF, D_MODEL)}

    ex = {"g1": ([bf["w_in"], _pack([wsh[n] for n in SMALL_SHARDED])], finish1),
          "g2": ([bf["w_out"], bf["ffn2_w_in"], bf["ffn2_w_out"]], finish2)}
    loss_part, grad_x, gr, parts = _local_step(x, loss_target, w, ex)
    loss = lax.psum(loss_part, ("x", "y", "c"))

    out = {n: _adam("adam_" + n, parts[n], wsh[n], msh[n], vsh[n]) for n in BIG}
    small_names = SMALL_SHARDED + SMALL_REPL
    full_shapes = [_grad_small(n, gr[n]).shape for n in small_names]
    summed = _unpack(_sum8("sum_small_grads", parts["small"]), full_shapes)
    mine = []
    for n, g in zip(small_names, summed):
        if n in SMALL_SHARDED:
            g = lax.dynamic_slice_in_dim(g, me * HEAD, HEAD, axis=g.ndim - 1)
        mine.append(g)
    shapes = [g.shape for g in mine]
    d_s, m_s, v_s = _adam_small("adam_small", _pack(mine), _pack([wsh[n] for n in small_names]),
                                _pack([msh[n] for n in small_names]), _pack([vsh[n] for n in small_names]))
    for n, g, dl, mn, vn in zip(small_names, mine, _unpack(d_s, shapes), _unpack(m_s, shapes), _unpack(v_s, shapes)):
        out[n] = (g, dl, mn, vn)

    res = [loss, grad_x]
    for k in range(4):
        res += [out[n][k][None] for n in WEIGHTS]
    return tuple(res)
```

```python
import functools
import math

import jax
import jax.numpy as jnp
from jax import lax
from jax.experimental import pallas as pl
from jax.experimental.pallas import tpu as pltpu

F32 = jnp.float32
BF16 = jnp.bfloat16

N_DEV = 8
D_MODEL = 1024
RW = 512
N_HEADS = 8
HEAD = 64
CW = 512
CONV_K = 31
D_FF = 2816
FF_BLK = 704
GATE_LORA = 160
SHIFT_COLS = 1952
SHIFT_PAD = 2048
IN_COLS = 2976
IN_PAD = 3072
LN_EPS = 1e-5
GN_EPS = 64e-5
NORM_EPS = 1e-12
ALPHA = 2.0 ** 0.25
DECAY_SCALE = math.exp(-0.5)
CHUNK = 64
ADAM_LR, ADAM_B1, ADAM_B2, ADAM_EPS, ADAM_WD, ADAM_STEP = 0.001, 0.9, 0.999, 1e-8, 0.01, 10
VMEM_LIMIT = 56 * 1024 * 1024

_DN = {"nn": (((1,), (0,)), ((), ())), "nt": (((1,), (1,)), ((), ())), "tn": (((0,), (0,)), ((), ()))}


def _params():
    return pltpu.CompilerParams(vmem_limit_bytes=VMEM_LIMIT)


def _dot(a, b, dims="nn"):
    return lax.dot_general(a, b, _DN[dims], preferred_element_type=F32)


def _split(x):
    hi = x.astype(BF16)
    return hi, (x - hi.astype(F32)).astype(BF16)


def _dot3_impl(a, b, dims):
    ah, al = _split(a)
    bh, bl = _split(b)
    return _dot(ah, bh, dims) + (_dot(ah, bl, dims) + _dot(al, bh, dims))


@functools.partial(jax.custom_vjp, nondiff_argnums=(2,))
def _dot3(a, b, dims="nn"):
    return _dot3_impl(a, b, dims)


def _dot3_fwd(a, b, dims):
    return _dot3_impl(a, b, dims), (a, b)


def _dot3_bwd(dims, res, ct):
    a, b = res
    if dims == "nn":
        return _dot3_impl(ct, b, "nt"), _dot3_impl(a, ct, "tn")
    if dims == "nt":
        return _dot3_impl(ct, b, "nn"), _dot3_impl(ct, a, "tn")
    return _dot3_impl(b, ct, "nt"), _dot3_impl(a, ct, "nn")


_dot3.defvjp(_dot3_fwd, _dot3_bwd)


def _dot1_impl(a, b, dims):
    return _dot(a.astype(BF16), b.astype(BF16), dims)


@functools.partial(jax.custom_vjp, nondiff_argnums=(2,))
def _dot1(a, b, dims="nn"):
    return _dot1_impl(a, b, dims)


def _dot1_bwd(dims, res, ct):
    a, b = res
    if dims == "nn":
        return _dot1_impl(ct, b, "nt"), _dot1_impl(a, ct, "tn")
    if dims == "nt":
        return _dot1_impl(ct, b, "nn"), _dot1_impl(ct, a, "tn")
    return _dot1_impl(b, ct, "nt"), _dot1_impl(a, ct, "nn")


_dot1.defvjp(lambda a, b, dims: (_dot1_impl(a, b, dims), (a, b)), _dot1_bwd)


def _tri_inv_impl(l, eye):
    tm = eye + l
    lp = l
    for _ in range(int(math.log2(CHUNK)) - 1):
        lp = _dot1(lp, lp)
        tm = tm + _dot1(tm, lp)
    return tm


@jax.custom_vjp
def _tri_inv(l, eye):
    return _tri_inv_impl(l, eye)


def _tri_inv_fwd(l, eye):
    tm = _tri_inv_impl(l, eye)
    return tm, (tm, eye)


def _tri_inv_bwd(res, ct):
    tm, eye = res
    return _dot1(_dot1(tm, ct, "tn"), tm, "nt"), jnp.zeros_like(eye)


_tri_inv.defvjp(_tri_inv_fwd, _tri_inv_bwd)


def _ones_impl(x, g):
    x1 = x.astype(BF16)
    r1 = x - x1.astype(F32)
    x2 = r1.astype(BF16)
    x3 = (r1 - x2.astype(F32)).astype(BF16)
    return _dot(x1, g) + (_dot(x2, g) + _dot(x3, g))


@jax.custom_vjp
def _head_sum(x, g):
    return _ones_impl(x, g)


_head_sum.defvjp(lambda x, g: (_ones_impl(x, g), g), lambda g, ct: (_ones_impl(ct, g), jnp.zeros_like(g)))


def _sigmoid(x):
    return 1.0 / (1.0 + jnp.exp(-x))


def _mesh_pos():
    return lax.axis_index("x"), lax.axis_index("y"), lax.axis_index("c")


def _peer(pos, q):
    x, y, c = pos
    return (1 - x if q & 4 else x, 1 - y if q & 2 else y, 1 - c if q & 1 else c)


def _linear(pos):
    return 4 * pos[0] + 2 * pos[1] + pos[2]


def _exchange_copies(x_refs, o_refs, send_sems, recv_sems, local_sems, gather):
    pos = _mesh_pos()
    me = _linear(pos)
    starts, wait_recv, wait_send, wait_local = [], [], [], []
    for t in range(len(x_refs)):
        src = x_refs[t] if gather else x_refs[t].at[me]
        cp = pltpu.make_async_copy(src, o_refs[t].at[me], local_sems.at[t])
        starts.append(cp.start)
        wait_local.append(cp.wait)
    for q in range(1, N_DEV):
        peer = _peer(pos, q)
        for t in range(len(x_refs)):
            src = x_refs[t] if gather else x_refs[t].at[_linear(peer)]
            sems = dict(send_sem=send_sems.at[t, q - 1], recv_sem=recv_sems.at[t, q - 1],
                        device_id=peer, device_id_type=pl.DeviceIdType.MESH)
            send = pltpu.make_async_remote_copy(src_ref=src, dst_ref=o_refs[t].at[me], **sems)
            recv = pltpu.make_async_remote_copy(src_ref=src, dst_ref=o_refs[t].at[_linear(peer)], **sems)
            starts.append(send.start)
            wait_recv.append(recv.wait_recv)
            wait_send.append(send.wait_send)
    return starts, wait_recv + wait_send + wait_local


def _exchange_shapes(xs, gather):
    return [jax.ShapeDtypeStruct((N_DEV,) + (x.shape if gather else x.shape[1:]), x.dtype) for x in xs]


def _exchange_sems(nt):
    return [pltpu.SemaphoreType.DMA((nt, N_DEV - 1)), pltpu.SemaphoreType.DMA((nt, N_DEV - 1)),
            pltpu.SemaphoreType.DMA((nt,))]


def _exchange(name, xs, gather):
    nt = len(xs)

    def body(*refs):
        starts, waits = _exchange_copies(refs[:nt], refs[nt:2 * nt], *refs[2 * nt:], gather)
        for f in starts:
            f()
        for f in waits:
            f()

    any_spec = pl.BlockSpec(memory_space=pl.ANY)
    return pl.pallas_call(
        body, name=name, in_specs=[any_spec] * nt, out_specs=[any_spec] * nt,
        out_shape=_exchange_shapes(xs, gather), scratch_shapes=_exchange_sems(nt))(*xs)


def _call(name, body, grid, ins, in_specs, out_specs, out_shape, scratch=(), exch=None):
    if exch is None:
        return pl.pallas_call(body, name=name, grid=grid, in_specs=in_specs, out_specs=out_specs,
                              out_shape=out_shape, scratch_shapes=list(scratch), compiler_params=_params())(*ins)
    xs, gather = exch
    single = not isinstance(out_shape, (list, tuple))
    o_specs = [out_specs] if single else list(out_specs)
    o_shape = [out_shape] if single else list(out_shape)
    n_in, n_out, n_x, n_scr = len(ins), len(o_shape), len(xs), len(scratch)

    def wrapped(*refs):
        in_refs = refs[:n_in]
        x_refs = refs[n_in:n_in + n_x]
        out_refs = refs[n_in + n_x:n_in + n_x + n_out]
        got_refs = refs[n_in + n_x + n_out:n_in + 2 * n_x + n_out]
        rest = refs[n_in + 2 * n_x + n_out:]
        starts, waits = _exchange_copies(x_refs, got_refs, *rest[n_scr:], gather)
        ids = [pl.program_id(i) for i in range(len(grid))]
        first = functools.reduce(lambda p, q: p & q, [i == 0 for i in ids])
        last = functools.reduce(lambda p, q: p & q, [i == g - 1 for i, g in zip(ids, grid)])

        @pl.when(first)
        def _():
            for f in starts:
                f()

        body(*in_refs, *out_refs, *rest[:n_scr])

        @pl.when(last)
        def _():
            for f in waits:
                f()

    any_spec = pl.BlockSpec(memory_space=pl.ANY)
    outs = pl.pallas_call(
        wrapped, name=name, grid=grid, in_specs=list(in_specs) + [any_spec] * n_x,
        out_specs=o_specs + [any_spec] * n_x, out_shape=o_shape + _exchange_shapes(xs, gather),
        scratch_shapes=list(scratch) + _exchange_sems(n_x), compiler_params=_params())(*ins, *xs)
    res = outs[:n_out]
    return (res[0] if single else res), outs[n_out:]


def _mm_call(name, dims, grid, red_axis, ins, in_specs, out_shape, out_spec, acc_shape,
             scale=1.0, add_scale=None, exch=None):
    nred = grid[red_axis]

    def body(*refs):
        if add_scale is None:
            a_ref, b_ref, o_ref, acc_ref = refs
            add_ref = None
        else:
            a_ref, b_ref, add_ref, o_ref, acc_ref = refs
        k = pl.program_id(red_axis)

        @pl.when(k == 0)
        def _():
            acc_ref[...] = jnp.zeros_like(acc_ref)

        acc_ref[...] += _dot(a_ref[...].astype(BF16), b_ref[...].astype(BF16), dims)

        @pl.when(k == nred - 1)
        def _():
            r = acc_ref[...]
            if scale != 1.0:
                r = r * scale
            if add_ref is not None:
                r = r + add_scale * add_ref[...].astype(F32)
            o_ref[...] = r.astype(o_ref.dtype)

    return _call(name, body, grid, ins, in_specs, out_spec, out_shape, [pltpu.VMEM(acc_shape, F32)], exch)


def _tile(dim, cap):
    t = min(dim, cap)
    while dim % t or t % 128:
        t -= 128
        assert t > 0, (dim, cap)
    return t


def _mm(name, a, b, dims, out_dtype, scale=1.0, add=None, add_scale=None, tm=512, tn=1024, tk=1024):
    if dims == "tn":
        kd, m = a.shape
        n = b.shape[1]
    else:
        m, kd = a.shape
        n = b.shape[1] if dims == "nn" else b.shape[0]
    tm, tn, tk = _tile(m, tm), _tile(n, tn), _tile(kd, tk)
    a_spec = (pl.BlockSpec((tk, tm), lambda i, j, k: (k, i)) if dims == "tn"
              else pl.BlockSpec((tm, tk), lambda i, j, k: (i, k)))
    b_spec = (pl.BlockSpec((tn, tk), lambda i, j, k: (j, k)) if dims == "nt"
              else pl.BlockSpec((tk, tn), lambda i, j, k: (k, j)))
    o_spec = pl.BlockSpec((tm, tn), lambda i, j, k: (i, j))
    ins, specs = [a, b], [a_spec, b_spec]
    if add is not None:
        ins.append(add)
        specs.append(o_spec)
    return _mm_call(name, dims, (m // tm, n // tn, kd // tk), 2, ins, specs,
                    jax.ShapeDtypeStruct((m, n), out_dtype), o_spec, (tm, tn),
                    scale=scale, add_scale=add_scale if add is not None else None)


def _ffn_in(name, x, wg, tm=512, exch=None):
    n = x.shape[0]
    nj = N_DEV // 2

    def body(x_ref, wgate_ref, wup_ref, hg_ref, hu_ref, act_ref):
        xb = x_ref[...].astype(BF16)
        g = _dot(xb, wgate_ref[...])
        u = _dot(xb, wup_ref[...])
        hg_ref[...] = g.astype(BF16)
        hu_ref[...] = u.astype(BF16)
        act_ref[...] = (g * _sigmoid(g) * u).astype(BF16)

    blk = pl.BlockSpec((None, tm, FF_BLK), lambda j, i: (j, i, 0))
    shp = jax.ShapeDtypeStruct((nj, n, FF_BLK), BF16)
    return _call(name, body, (nj, n // tm), [x, wg, wg],
                 [pl.BlockSpec((tm, D_MODEL), lambda j, i: (i, 0)),
                  pl.BlockSpec((None, D_MODEL, FF_BLK), lambda j, i: (j, 0, 0)),
                  pl.BlockSpec((None, D_MODEL, FF_BLK), lambda j, i: (j + nj, 0, 0))],
                 [blk, blk, blk], [shp, shp, shp], exch=exch)


def _ffn_out_bwd(name, dz, wout, hg, hu, tm=512, exch=None):
    n = dz.shape[0]
    nj = N_DEV // 2

    def body(dz_ref, w_ref, hg_ref, hu_ref, dhg_ref, dhu_ref):
        dact = 0.5 * _dot(dz_ref[...].astype(BF16), w_ref[...], "nt")
        g = hg_ref[...].astype(F32)
        u = hu_ref[...].astype(F32)
        s = _sigmoid(g)
        dhg_ref[...] = (dact * u * (s * (1.0 + g * (1.0 - s)))).astype(BF16)
        dhu_ref[...] = (dact * (g * s)).astype(BF16)

    blk = pl.BlockSpec((None, tm, FF_BLK), lambda j, i: (j, i, 0))
    shp = jax.ShapeDtypeStruct((nj, n, FF_BLK), BF16)
    return _call(name, body, (nj, n // tm), [dz, wout, hg, hu],
                 [pl.BlockSpec((tm, D_MODEL), lambda j, i: (i, 0)),
                  pl.BlockSpec((FF_BLK, D_MODEL), lambda j, i: (j, 0)), blk, blk],
                 [blk, blk], [shp, shp], exch=exch)


def _mm_ln(name, a, b, xres, g, beta, c, tgt=None, a_blocked=False, tm=512, tk=512):
    if a_blocked:
        nk, n, kb = a.shape
        a_spec = pl.BlockSpec((None, tm, kb), lambda i, k: (k, i, 0))
    else:
        n, kd = a.shape
        kb = _tile(kd, tk)
        nk = kd // kb
        a_spec = pl.BlockSpec((tm, kb), lambda i, k: (i, k))
    d = b.shape[1]
    with_loss = tgt is not None

    def body(*refs):
        if with_loss:
            a_ref, b_ref, x_ref, g_ref, be_ref, t_ref, o_ref, z_ref, l_ref, acc_ref = refs
        else:
            a_ref, b_ref, x_ref, g_ref, be_ref, o_ref, z_ref, acc_ref = refs
        i, k = pl.program_id(0), pl.program_id(1)

        @pl.when(k == 0)
        def _():
            acc_ref[...] = jnp.zeros_like(acc_ref)

        acc_ref[...] += _dot(a_ref[...].astype(BF16), b_ref[...].astype(BF16))

        @pl.when(k == nk - 1)
        def _():
            z = ALPHA * x_ref[...] + c * acc_ref[...]
            z_ref[...] = z
            mu = jnp.mean(z, axis=-1, keepdims=True)
            zc = z - mu
            var = jnp.mean(zc * zc, axis=-1, keepdims=True)
            y = zc * lax.rsqrt(var + LN_EPS) * g_ref[...] + be_ref[...]
            if with_loss:
                err = y - t_ref[...]
                o_ref[...] = err * (1.0 / d)
                part = 0.5 * jnp.sum(jnp.sum(err * err, axis=-1, keepdims=True), axis=0, keepdims=True) * (1.0 / d)

                @pl.when(i == 0)
                def _():
                    l_ref[...] = jnp.zeros_like(l_ref)

                l_ref[...] += jnp.broadcast_to(part, l_ref.shape)
            else:
                o_ref[...] = y

    row = pl.BlockSpec((tm, d), lambda i, k: (i, 0))
    vec = pl.BlockSpec((1, d), lambda i, k: (0, 0))
    ins = [a, b, xres, g, beta]
    in_specs = [a_spec, pl.BlockSpec((kb, d), lambda i, k: (k, 0)), row, vec, vec]
    out_specs = [row, row]
    out_shape = [jax.ShapeDtypeStruct((n, d), F32), jax.ShapeDtypeStruct((n, d), F32)]
    if with_loss:
        ins.append(tgt)
        in_specs.append(row)
        out_specs.append(pl.BlockSpec((1, 128), lambda i, k: (0, 0)))
        out_shape.append(jax.ShapeDtypeStruct((1, 128), F32))
    return _call(name, body, (n // tm, nk), ins, in_specs, out_specs, out_shape, [pltpu.VMEM((tm, d), F32)])


def _rowwise(name, fn, rows, params, out_rows, out_accs, tm=256, exch=None):
    specs, ins = [], []
    for r in rows:
        arr, w, cb = r if isinstance(r, tuple) else (r, r.shape[1], 0)
        ins.append(arr)
        specs.append(pl.BlockSpec((tm, w), functools.partial(lambda i, cb: (i, cb), cb=cb)))
    n = ins[0].shape[0]
    for p in params:
        ins.append(p)
        specs.append(pl.BlockSpec(p.shape, lambda i: (0, 0)))
    n_in, n_or = len(ins), len(out_rows)

    def body(*refs):
        outs = fn(*[r[...] for r in refs[:n_in]])
        o_refs = refs[n_in:]
        for o_ref, o in zip(o_refs[:n_or], outs[:n_or]):
            o_ref[...] = o.astype(o_ref.dtype)
        if out_accs:
            @pl.when(pl.program_id(0) == 0)
            def _():
                for a_ref in o_refs[n_or:]:
                    a_ref[...] = jnp.zeros_like(a_ref)

            for a_ref, a in zip(o_refs[n_or:], outs[n_or:]):
                a_ref[...] += a.astype(F32)

    out_specs = [pl.BlockSpec((tm, w), lambda i: (i, 0)) for w, _ in out_rows]
    out_specs += [pl.BlockSpec(s, lambda i: (0, 0)) for s in out_accs]
    out_shape = [jax.ShapeDtypeStruct((n, w), dt) for w, dt in out_rows]
    out_shape += [jax.ShapeDtypeStruct(s, F32) for s in out_accs]
    return _call(name, body, (n // tm,), ins, specs, out_specs, out_shape, exch=exch)


def _vjp_of(fn, n_in):
    def g(*args):
        ins, cts = args[:n_in], args[n_in:]
        outs, pull = jax.vjp(fn, *ins)
        return pull(tuple(c.astype(o.dtype) for c, o in zip(cts, outs)))
    return g


def _ln_bwd(name, z, g, ct):
    def fn(zt, ct_, gt):
        mu = jnp.mean(zt, axis=-1, keepdims=True)
        zc = zt - mu
        rstd = lax.rsqrt(jnp.mean(zc * zc, axis=-1, keepdims=True) + LN_EPS)
        xh = zc * rstd
        dxh = ct_ * gt
        dz = rstd * (dxh - jnp.mean(dxh, axis=-1, keepdims=True)
                     - xh * jnp.mean(dxh * xh, axis=-1, keepdims=True))
        return dz, jnp.sum(ct_ * xh, axis=0, keepdims=True), jnp.sum(ct_, axis=0, keepdims=True)

    d = z.shape[1]
    return _rowwise(name, fn, [z, ct], [g], [(d, F32)], [(1, d), (1, d)])


def _shift(name, src, mu_a, mu_b, q=None, tt=256):
    bsz, t, _ = src.shape
    nt, r8, w = t // tt, tt // 8, SHIFT_PAD
    with_q = q is not None

    def body(cur_ref, prev_ref, next_ref, *rest):
        b, i = pl.program_id(0), pl.program_id(1)
        cur = cur_ref[...]
        prow = jnp.where(i > 0, prev_ref[7:8, :], 0.0)
        nrow = jnp.where(i < nt - 1, next_ref[0:1, :], 0.0)
        rid = lax.broadcasted_iota(jnp.int32, cur.shape, 0)
        dprev = jnp.where(rid == 0, prow, pltpu.roll(cur, 1, 0)) - cur
        dnext = jnp.where(rid == tt - 1, nrow, pltpu.roll(cur, tt - 1, 0)) - cur
        if with_q:
            q_ref, da_ref, db_ref = rest

            @pl.when((b == 0) & (i == 0))
            def _():
                da_ref[...] = jnp.zeros_like(da_ref)
                db_ref[...] = jnp.zeros_like(db_ref)

            qv = q_ref[...]
            da_ref[...] += jnp.sum(qv * dprev, axis=0, keepdims=True)
            db_ref[...] += jnp.sum(qv * dnext, axis=0, keepdims=True)
        else:
            ma_ref, mb_ref, o_ref = rest
            o_ref[...] = cur + ma_ref[...] * dprev + mb_ref[...] * dnext

    cur_spec = pl.BlockSpec((None, tt, w), lambda b, i: (b, i, 0))
    in_specs = [cur_spec,
                pl.BlockSpec((None, 8, w), lambda b, i: (b, jnp.maximum(i * r8 - 1, 0), 0)),
                pl.BlockSpec((None, 8, w), lambda b, i: (b, jnp.minimum((i + 1) * r8, t // 8 - 1), 0))]
    vec = pl.BlockSpec((1, w), lambda b, i: (0, 0))
    if with_q:
        return _call(name, body, (bsz, nt), [src, src, src, q], in_specs + [cur_spec], [vec, vec],
                     [jax.ShapeDtypeStruct((1, w), F32)] * 2)
    return _call(name, body, (bsz, nt), [src, src, src, mu_a, mu_b], in_specs + [vec, vec], cur_spec,
                 jax.ShapeDtypeStruct((bsz, t, w), F32))


def _halo_specs(t, tt, w):
    r16 = tt // 16
    return [pl.BlockSpec((None, tt, w), lambda b, i: (b, i, 0)),
            pl.BlockSpec((None, 16, w), lambda b, i: (b, jnp.maximum(i * r16 - 1, 0), 0)),
            pl.BlockSpec((None, 16, w), lambda b, i: (b, jnp.minimum((i + 1) * r16, t // 16 - 1), 0))]


def _fill_pad(pad_ref, cur_ref, prev_ref, next_ref, i, nt, tt):
    pad_ref[0:16, :] = jnp.where(i > 0, prev_ref[...], 0.0)
    pad_ref[16:16 + tt, :] = cur_ref[...]
    pad_ref[16 + tt:32 + tt, :] = jnp.where(i < nt - 1, next_ref[...], 0.0)


def _dwconv(name, u, dw32, bias, flip, tt=512):
    bsz, t, w = u.shape
    tt = min(tt, t)
    nt = t // tt

    def body(cur_ref, prev_ref, next_ref, dw_ref, b_ref, o_ref, pad_ref):
        i = pl.program_id(1)
        _fill_pad(pad_ref, cur_ref, prev_ref, next_ref, i, nt, tt)
        acc = jnp.broadcast_to(b_ref[...], (tt, w))
        for k in range(CONV_K):
            kk = CONV_K - 1 - k if flip else k
            acc = acc + pad_ref[pl.ds(1 + k, tt), :] * dw_ref[kk:kk + 1, :]
        o_ref[...] = acc

    return _call(name, body, (bsz, nt), [u, u, u, dw32, bias],
                 _halo_specs(t, tt, w) + [pl.BlockSpec((32, w), lambda b, i: (0, 0)),
                                          pl.BlockSpec((1, w), lambda b, i: (0, 0))],
                 pl.BlockSpec((None, tt, w), lambda b, i: (b, i, 0)), jax.ShapeDtypeStruct((bsz, t, w), F32),
                 [pltpu.VMEM((tt + 32, w), F32)])


def _dwconv_dw(name, u, dc, tt=512):
    bsz, t, w = u.shape
    tt = min(tt, t)
    nt = t // tt

    def body(cur_ref, prev_ref, next_ref, dc_ref, ddw_ref, db_ref, pad_ref):
        b, i = pl.program_id(0), pl.program_id(1)
        _fill_pad(pad_ref, cur_ref, prev_ref, next_ref, i, nt, tt)

        @pl.when((b == 0) & (i == 0))
        def _():
            ddw_ref[...] = jnp.zeros_like(ddw_ref)
            db_ref[...] = jnp.zeros_like(db_ref)

        dcv = dc_ref[...]
        db_ref[...] += jnp.sum(dcv, axis=0, keepdims=True)
        for k in range(CONV_K):
            ddw_ref[k:k + 1, :] += jnp.sum(dcv * pad_ref[pl.ds(1 + k, tt), :], axis=0, keepdims=True)

    return _call(name, body, (bsz, nt), [u, u, u, dc],
                 _halo_specs(t, tt, w) + [pl.BlockSpec((None, tt, w), lambda b, i: (b, i, 0))],
                 [pl.BlockSpec((32, w), lambda b, i: (0, 0)), pl.BlockSpec((1, w), lambda b, i: (0, 0))],
                 [jax.ShapeDtypeStruct((32, w), F32), jax.ShapeDtypeStruct((1, w), F32)],
                 [pltpu.VMEM((tt + 32, w), F32)])


PAIR = 2 * HEAD
N_PAIRS = RW // PAIR


def _chunk_pairs(s, r, lw, k, v, kk, a, sgn):
    n, m = CHUNK, 2 * CHUNK
    in_a = lax.broadcasted_iota(jnp.int32, (n, PAIR), 1) < HEAD

    def stack2(z):
        return jnp.concatenate([jnp.where(in_a, z, 0.0), jnp.where(in_a, 0.0, z)], axis=0)

    def each(f, *lists):
        return [f(*z) for z in zip(*lists)]

    row = lax.broadcasted_iota(jnp.int32, (n, n), 0)
    col = lax.broadcasted_iota(jnp.int32, (n, n), 1)
    tri = jnp.where((row - col) * sgn >= 0, 1.0, 0.0)
    row2 = lax.broadcasted_iota(jnp.int32, (m, m), 0)
    col2 = lax.broadcasted_iota(jnp.int32, (m, m), 1)
    same = (row2 >= n) == (col2 >= n)
    dlt = ((row2 & (n - 1)) - (col2 & (n - 1))) * sgn
    incl, strict = same & (dlt >= 0), same & (dlt > 0)
    eye = jnp.where(row2 == col2, 1.0, 0.0)

    cum = each(lambda lw_: _dot3(tri, lw_), lw)
    tot = each(lambda lw_: jnp.sum(lw_, axis=0, keepdims=True), lw)
    e_neg = each(lambda c_: jnp.exp(-c_), cum)
    e_rest = each(lambda t_, c_: jnp.exp(t_ - c_), tot, cum)
    beta = each(lambda kk_, a_: kk_ * a_, kk, a)
    lhs = each(lambda kk_, c_, lw_, r_: jnp.concatenate(
        [stack2(-kk_ * jnp.exp(c_ - lw_)), stack2(r_ * jnp.exp(c_))], axis=0), kk, cum, lw, r)
    rhs = each(lambda b_, k_, e_: jnp.concatenate([stack2(b_ * e_), stack2(k_ * e_)], axis=0), beta, k, e_neg)
    sc = each(lambda l_, r_: _dot3(l_, r_, "nt"), lhs, rhs)
    l_ab = each(lambda sc_: jnp.where(strict, sc_[0:m, 0:m], 0.0), sc)
    l_ak = each(lambda sc_: jnp.where(strict, sc_[0:m, m:2 * m], 0.0), sc)
    m_r = each(lambda sc_: jnp.where(jnp.concatenate([incl, incl], axis=1), sc_[m:2 * m, :], 0.0), sc)
    tm = each(lambda l_: _tri_inv(l_, eye), l_ab)
    z = each(lambda l_, s_: _dot1(l_, s_, "nt"), lhs, s)
    v2 = each(stack2, v)
    u2 = each(lambda t_, z_, l_, v_: _dot1(t_, z_[0:m] + _dot1(l_, v_)), tm, z, l_ak, v2)
    uv = each(lambda u_, v_: jnp.concatenate([u_, v_], axis=0), u2, v2)
    y2 = each(lambda z_, m_, uv_: z_[m:2 * m] + _dot1(m_, uv_), z, m_r, uv)
    s_new = each(lambda s_, t_, uv_, b_, k_, e_: s_ * jnp.exp(t_) + _dot1(
        uv_, jnp.concatenate([stack2(b_ * e_), stack2(k_ * e_)], axis=0), "tn"), s, tot, uv, beta, k, e_rest)
    return each(lambda y_: y_[0:n] + y_[n:m], y2), s_new


def _pair_tiles(ref):
    return [ref[:, p * PAIR:(p + 1) * PAIR] for p in range(N_PAIRS)]


def _scan_specs(nc, order):
    shared = pl.BlockSpec((CHUNK, RW), lambda d, b, c: (b * nc + order(d, c), 0))
    per_dir = pl.BlockSpec((CHUNK, RW), lambda d, b, c: (b * nc + order(d, c), d))
    state = pl.BlockSpec((None, None, N_PAIRS, PAIR, PAIR), lambda d, b, c: (d, b * nc + order(d, c), 0, 0, 0))
    return shared, per_dir, state


def _scan_fwd(r, v, kk, lw, kd, a, bsz, exch=None):
    n = r.shape[0]
    nc = n // bsz // CHUNK

    def order(d, c):
        return c + d * (nc - 1 - 2 * c)

    def body(r_ref, v_ref, kk_ref, lw_ref, kd_ref, a_ref, y_ref, s0_ref, s_ref):
        d, c = pl.program_id(0), pl.program_id(2)

        @pl.when(c == 0)
        def _():
            s_ref[...] = jnp.zeros_like(s_ref)

        s = [s_ref[p] for p in range(N_PAIRS)]
        y, s_new = _chunk_pairs(s, *[_pair_tiles(ref) for ref in (r_ref, lw_ref, kd_ref, v_ref, kk_ref, a_ref)],
                                1 - 2 * d)
        for p in range(N_PAIRS):
            s0_ref[p] = s[p]
            y_ref[:, p * PAIR:(p + 1) * PAIR] = y[p]
            s_ref[p] = s_new[p]

    shared, per_dir, state = _scan_specs(nc, order)
    return _call("scan_fwd", body, (2, bsz, nc), [r, v, kk, lw, kd, a],
                 [shared, shared, shared, per_dir, per_dir, per_dir], [per_dir, state],
                 [jax.ShapeDtypeStruct((n, 2 * RW), F32),
                  jax.ShapeDtypeStruct((2, n // CHUNK, N_PAIRS, PAIR, PAIR), F32)],
                 [pltpu.VMEM((N_PAIRS, PAIR, PAIR), F32)], exch)


def _scan_bwd(r, v, kk, lw, kd, a, s0, dy, bsz, exch=None):
    n = r.shape[0]
    nc = n // bsz // CHUNK

    def order(d, c):
        cc = nc - 1 - c
        return cc + d * (nc - 1 - 2 * cc)

    def body(r_ref, v_ref, kk_ref, lw_ref, kd_ref, a_ref, dy_ref, s0_ref,
             dr_ref, dv_ref, dkk_ref, dlw_ref, dkd_ref, da_ref, ds_ref):
        d, c = pl.program_id(0), pl.program_id(2)

        @pl.when(c == 0)
        def _():
            ds_ref[...] = jnp.zeros_like(ds_ref)

        sgn = 1 - 2 * d
        _, pull = jax.vjp(lambda *ops: _chunk_pairs(*ops, sgn), [s0_ref[p] for p in range(N_PAIRS)],
                          *[_pair_tiles(ref) for ref in (r_ref, lw_ref, kd_ref, v_ref, kk_ref, a_ref)])
        grads = pull((_pair_tiles(dy_ref), [ds_ref[p] for p in range(N_PAIRS)]))
        for p in range(N_PAIRS):
            ds_ref[p] = grads[0][p]
            for o_ref, gx in zip((dr_ref, dlw_ref, dkd_ref, dv_ref, dkk_ref, da_ref), grads[1:]):
                o_ref[:, p * PAIR:(p + 1) * PAIR] = gx[p]

    shared, per_dir, state = _scan_specs(nc, order)
    shp = jax.ShapeDtypeStruct((n, 2 * RW), F32)
    return _call("scan_bwd", body, (2, bsz, nc), [r, v, kk, lw, kd, a, dy, s0],
                 [shared, shared, shared, per_dir, per_dir, per_dir, per_dir, state],
                 [per_dir] * 6, [shp] * 6, [pltpu.VMEM((N_PAIRS, PAIR, PAIR), F32)], exch)


def _prep_fn(ps, w0, w2bd, a0, a2bd, g2p, k_k, k_a, hsum):
    head_sum = lambda z: _head_sum(z, hsum)
    r, k, v = ps[:, 0:RW], ps[:, RW:2 * RW], ps[:, 2 * RW:3 * RW]
    wd, ad, gd = ps[:, 1536:1664], ps[:, 1664:1792], ps[:, 1792:2048]
    logw = -DECAY_SCALE * _sigmoid(_dot3(jnp.tanh(wd), w2bd) + w0)
    a = _sigmoid(_dot3(ad, a2bd) + a0)
    g = _dot3(_sigmoid(gd), g2p)
    kkr = k * k_k
    kk = kkr / jnp.maximum(jnp.sqrt(head_sum(kkr * kkr)), NORM_EPS)
    k2 = jnp.concatenate([k, k], axis=1)
    ka2 = jnp.concatenate([k_a, k_a], axis=1)
    kd = k2 * (1.0 + (a - 1.0) * ka2)
    return r, v, kk, logw, a, kd, g


def _post_fn(y2, r, v, kd, g, lnx_g, lnx_b, r_k, hsum):
    head_sum = lambda z: _head_sum(z, hsum)
    y = y2[:, 0:RW] + y2[:, RW:2 * RW]
    mu = head_sum(y) * (1.0 / HEAD)
    yc = y - mu
    var = head_sum(yc * yc) * (1.0 / HEAD)
    yn = yc * lax.rsqrt(var + GN_EPS) * lnx_g + lnx_b
    bonus = head_sum(r * (kd[:, 0:RW] + kd[:, RW:2 * RW]) * r_k) * v
    return ((yn + bonus) * g,)


def _glu_fn(pa, pb):
    return (pa * _sigmoid(pb),)


def _conv_out_fn(cv, ln_g, ln_b):
    mu = jnp.mean(cv, axis=-1, keepdims=True)
    cc = cv - mu
    var = jnp.mean(cc * cc, axis=-1, keepdims=True)
    y = cc * lax.rsqrt(var + LN_EPS) * ln_g + ln_b
    return (y * _sigmoid(y),)


def _local_step(x, tgt, w, ex=None):
    bsz, t, d = x.shape
    n = bsz * t
    x2d, tgt2d = x.reshape(n, d), tgt.reshape(n, d)
    hsum = jnp.kron(jnp.eye(N_HEADS, dtype=BF16), jnp.ones((HEAD, HEAD), BF16))
    w = dict(w)
    parts = {} if ex else None

    def hosted(result, finish=None):
        if not ex:
            return result
        outs, got = result
        if finish is not None:
            w.update(finish(got))
        return outs

    hg1, hu1, act1 = hosted(_ffn_in("ffn1_in", x2d, w["ffn1_w_in"], exch=(ex["g1"][0], True) if ex else None),
                            ex["g1"][1] if ex else None)
    x1, z1 = _mm_ln("ffn1_out_ln1", act1, w["ffn1_w_out"], x2d, w["ln1_g"], w["ln1_b"], 0.5, a_blocked=True)
    p = _mm("w_in_proj", x1, w["w_in"], "nn", F32)
    p3 = p.reshape(bsz, t, IN_PAD)
    ps = _shift("shift_fwd", p3, w["mu_prev"], w["mu_next"]).reshape(n, SHIFT_PAD)
    prep_params = [w["w0"], w["w2"], w["a0"], w["a2"], w["g2"], w["k_k"], w["k_a"], hsum]
    r, v, kk, logw, a, kd, g = _rowwise(
        "rwkv_prep", _prep_fn, [ps], prep_params,
        [(RW, F32), (RW, F32), (RW, F32), (2 * RW, F32), (2 * RW, F32), (2 * RW, F32), (RW, F32)], [])
    y2, s0 = hosted(_scan_fwd(r, v, kk, logw, kd, a, bsz, exch=(ex["g2"][0], True) if ex else None),
                    ex["g2"][1] if ex else None)
    post_params = [w["lnx_g"], w["lnx_b"], w["r_k"], hsum]
    (y_rwkv,) = _rowwise("rwkv_post", _post_fn, [y2, r, v, kd, g], post_params, [(RW, BF16)], [])
    (u,) = _rowwise("conv_glu", _glu_fn, [(p, CW, 4), (p, CW, 5)], [], [(CW, F32)], [])
    cv = _dwconv("conv_dw", u.reshape(bsz, t, CW), w["conv_dw"], w["conv_b"], False).reshape(n, CW)
    (y_conv,) = _rowwise("conv_out", _conv_out_fn, [cv], [w["conv_ln_g"], w["conv_ln_b"]], [(CW, BF16)], [])
    ycat = jnp.concatenate([y_rwkv, y_conv], axis=1)
    x2, z2 = _mm_ln("w_out_ln2", ycat, w["w_out"], x1, w["ln2_g"], w["ln2_b"], 1.0)
    hg2, hu2, act2 = _ffn_in("ffn2_in", x2, w["ffn2_w_in"])
    dx3, z3, loss = _mm_ln("ffn2_out_ln3", act2, w["ffn2_w_out"], x2, w["ln3_g"], w["ln3_b"], 0.5,
                           tgt=tgt2d, a_blocked=True)

    gr = {}
    dz3, gr["ln3_g"], gr["ln3_b"] = _ln_bwd("ln3_bwd", z3, w["ln3_g"], dx3)
    dx2, _ = _ffn_bwd("ffn2", gr, dz3, x2, w["ffn2_w_in"], w["ffn2_w_out"], hg2, hu2, act2)
    dz2, gr["ln2_g"], gr["ln2_b"] = _ln_bwd("ln2_bwd", z2, w["ln2_g"], dx2)
    gr["w_out"] = _mm("w_out_wgrad", ycat, dz2, "tn", BF16, tk=512)
    dycat = _mm("w_out_dgrad", dz2, w["w_out"], "nt", F32)
    conv_out_bwd = _vjp_of(_conv_out_fn, 3)
    dcv, gr["conv_ln_g"], gr["conv_ln_b"] = _rowwise(
        "conv_out_bwd", lambda cv_, ct_, g_, b_: conv_out_bwd(cv_, g_, b_, ct_),
        [cv, (dycat, CW, 1)], [w["conv_ln_g"], w["conv_ln_b"]], [(CW, F32)], [(1, CW), (1, CW)])
    dcv3 = dcv.reshape(bsz, t, CW)
    gr["conv_dw"], gr["conv_b"] = _dwconv_dw("conv_dw_wgrad", u.reshape(bsz, t, CW), dcv3)
    du = _dwconv("conv_dw_dgrad", dcv3, w["conv_dw"], jnp.zeros((1, CW), F32), True).reshape(n, CW)

    def glu_bwd(pa, pb, ct):
        return (jnp.concatenate(_vjp_of(_glu_fn, 2)(pa, pb, ct), axis=1),)

    (dp_conv,) = _rowwise("conv_glu_bwd", glu_bwd, [(p, CW, 4), (p, CW, 5), du], [], [(2 * CW, F32)], [])

    def post_bwd(y2_, r_, v_, kd_, g_, ct, lg, lb, rk, hs):
        return _vjp_of(lambda *z: _post_fn(*z, hs), 8)(y2_, r_, v_, kd_, g_, lg, lb, rk, ct)

    dy2, dr_post, dv_post, dkd_post, dg, gr["lnx_g"], gr["lnx_b"], gr["r_k"] = _rowwise(
        "rwkv_post_bwd", post_bwd, [y2, r, v, kd, g, (dycat, RW, 0)], post_params,
        [(2 * RW, F32), (RW, F32), (RW, F32), (2 * RW, F32), (RW, F32)], [(1, RW), (1, RW), (1, RW)])
    sends = [jnp.concatenate(gr["ffn2_w_in"], axis=0), gr["ffn2_w_out"].reshape(N_DEV, D_FF // N_DEV, D_MODEL),
             gr["w_out"].reshape(N_DEV, D_MODEL // N_DEV, D_MODEL)]
    res = _scan_bwd(r, v, kk, logw, kd, a, s0, dy2, bsz, exch=(sends, False) if ex else None)
    if ex:
        res, got = res
        parts.update(zip(("ffn2_w_in", "ffn2_w_out", "w_out"), got))
    dr_s, dv_s, dkk_s, dlw, dkd_s, da = res

    def prep_bwd(ps_, dr2, dr1, dv2, dv1, dkk2, dlw_, da_, dkd2, dkd1, dg_, *prm):
        half = lambda z: z[:, 0:RW] + z[:, RW:2 * RW]
        return _vjp_of(lambda *z: _prep_fn(*z, prm[-1]), 8)(
            ps_, *prm[:-1], half(dr2) + dr1, half(dv2) + dv1, half(dkk2), dlw_, da_, dkd2 + dkd1, dg_)

    dps, gr["w0"], gr["w2"], gr["a0"], gr["a2"], gr["g2"], gr["k_k"], gr["k_a"] = _rowwise(
        "rwkv_prep_bwd", prep_bwd,
        [ps, dr_s, dr_post, dv_s, dv_post, dkk_s, dlw, da, dkd_s, dkd_post, dg], prep_params,
        [(SHIFT_PAD, F32)], [q.shape for q in prep_params[:-1]], tm=128)
    dps3 = dps.reshape(bsz, t, SHIFT_PAD)
    gr["mu_prev"], gr["mu_next"] = _shift("shift_dmu", p3, None, None, q=dps3)
    dp_shift = _shift("shift_bwd", dps3, w["mu_next"], w["mu_prev"]).reshape(n, SHIFT_PAD)
    dp = jnp.concatenate([dp_shift, dp_conv], axis=1)
    gr["w_in"] = _mm("w_in_wgrad", x1, dp, "tn", BF16, tk=512)
    dx1 = _mm("w_in_dgrad", dp, w["w_in"], "nt", F32, add=dz2, add_scale=ALPHA)
    dz1, gr["ln1_g"], gr["ln1_b"] = _ln_bwd("ln1_bwd", z1, w["ln1_g"], dx1)
    riders = None
    if ex:
        gw_in = _unpad_in_cols(gr["w_in"]).reshape(D_MODEL, N_DEV, IN_COLS // N_DEV).transpose(1, 0, 2)
        small = _pack([_grad_small(nm, gr[nm]) for nm in SMALL_SHARDED + SMALL_REPL])
        riders = {"out_dgrad": ([gw_in], False), "out_wgrad": ([small], True)}
    grad_x, got = _ffn_bwd("ffn1", gr, dz1, x2d, w["ffn1_w_in"], w["ffn1_w_out"], hg1, hu1, act1, riders)
    if ex:
        parts.update(w_in=got["out_dgrad"][0], small=got["out_wgrad"][0],
                     ffn1_w_out=got["w_out"], ffn1_w_in=got["w_in"])
    return loss[0, 0], grad_x.reshape(bsz, t, d), gr, parts


def _ffn_bwd(tag, gr, dz, xin, wg, wout, hg, hu, act, riders=None, tk=512):
    n = xin.shape[0]
    nj = N_DEV // 2
    nt = n // tk
    got = {}

    def hosted(key, result, on):
        if not on:
            return result
        got[key] = result[1][0] if key in ("w_out", "w_in") else result[1]
        return result[0]

    riders = riders or {}
    own = bool(riders)
    dhg, dhu = hosted("out_dgrad", _ffn_out_bwd(tag + "_out_dgrad", dz, wout, hg, hu, exch=riders.get("out_dgrad")),
                      "out_dgrad" in riders)
    gr[tag + "_w_out"] = hosted("out_wgrad", _mm_call(
        tag + "_out_wgrad", "tn", (nj, nt), 1, [act, dz],
        [pl.BlockSpec((None, tk, FF_BLK), lambda j, i: (j, i, 0)), pl.BlockSpec((tk, D_MODEL), lambda j, i: (i, 0))],
        jax.ShapeDtypeStruct((D_FF, D_MODEL), BF16), pl.BlockSpec((FF_BLK, D_MODEL), lambda j, i: (j, 0)),
        (FF_BLK, D_MODEL), scale=0.5, exch=riders.get("out_wgrad")), "out_wgrad" in riders)
    dw = []
    for nm, dh in (("gate", dhg), ("up", dhu)):
        send = ([gr[tag + "_w_out"].reshape(N_DEV, D_FF // N_DEV, D_MODEL)], False) if own and nm == "gate" else None
        dw.append(hosted("w_out", _mm_call(
            tag + "_in_wgrad_" + nm, "tn", (nj, nt), 1, [xin, dh],
            [pl.BlockSpec((tk, D_MODEL), lambda j, i: (i, 0)), pl.BlockSpec((None, tk, FF_BLK), lambda j, i: (j, i, 0))],
            jax.ShapeDtypeStruct((nj, D_MODEL, FF_BLK), BF16), pl.BlockSpec((None, D_MODEL, FF_BLK), lambda j, i: (j, 0, 0)),
            (D_MODEL, FF_BLK), exch=send), send is not None))
    gr[tag + "_w_in"] = dw
    dx = dz
    tm = 512
    for off, (nm, dh) in enumerate((("gate", dhg), ("up", dhu))):
        send = ([jnp.concatenate(dw, axis=0)], False) if own and nm == "gate" else None
        dx = hosted("w_in", _mm_call(
            tag + "_in_dgrad_" + nm, "nt", (n // tm, nj), 1, [dh, wg, dx],
            [pl.BlockSpec((None, tm, FF_BLK), lambda i, j: (j, i, 0)),
             pl.BlockSpec((None, D_MODEL, FF_BLK), functools.partial(lambda i, j, o: (j + o, 0, 0), o=off * nj)),
             pl.BlockSpec((tm, D_MODEL), lambda i, j: (i, 0))],
            jax.ShapeDtypeStruct((n, D_MODEL), F32), pl.BlockSpec((tm, D_MODEL), lambda i, j: (i, 0)),
            (tm, D_MODEL), add_scale=ALPHA if off == 0 else 1.0, exch=send), send is not None)
    return dx, got


def _adam_math(g, w, m, v):
    m = ADAM_B1 * m + (1.0 - ADAM_B1) * g
    v = ADAM_B2 * v + (1.0 - ADAM_B2) * (g * g)
    m_hat = m / (1.0 - ADAM_B1 ** ADAM_STEP)
    v_hat = v / (1.0 - ADAM_B2 ** ADAM_STEP)
    delta = -ADAM_LR * (m_hat / (jnp.sqrt(v_hat) + ADAM_EPS) + ADAM_WD * w)
    return delta, m, v


def _adam(name, parts, w, m, v, tr=128):
    rows, cols = w.shape
    tr = min(tr, rows)
    while rows % tr:
        tr -= 8

    def body(p_ref, w_ref, m_ref, v_ref, g_ref, d_ref, mo_ref, vo_ref):
        g = p_ref[0].astype(F32)
        for s in range(1, N_DEV):
            g = g + p_ref[s].astype(F32)
        g_ref[...] = g
        d_ref[...], mo_ref[...], vo_ref[...] = _adam_math(g, w_ref[...], m_ref[...], v_ref[...])

    blk = pl.BlockSpec((tr, cols), lambda i: (i, 0))
    shp = jax.ShapeDtypeStruct((rows, cols), F32)
    return _call(name, body, (rows // tr,), [parts, w, m, v],
                 [pl.BlockSpec((N_DEV, tr, cols), lambda i: (0, i, 0)), blk, blk, blk], [blk] * 4, [shp] * 4)


def _sum8(name, parts):
    _, rows, cols = parts.shape

    def body(p_ref, o_ref):
        g = p_ref[0]
        for s in range(1, N_DEV):
            g = g + p_ref[s]
        o_ref[...] = g

    return pl.pallas_call(body, name=name, out_shape=jax.ShapeDtypeStruct((rows, cols), F32),
                          compiler_params=_params())(parts)


def _adam_small(name, g, w, m, v):
    def body(g_ref, w_ref, m_ref, v_ref, d_ref, mo_ref, vo_ref):
        d_ref[...], mo_ref[...], vo_ref[...] = _adam_math(g_ref[...], w_ref[...], m_ref[...], v_ref[...])

    shp = jax.ShapeDtypeStruct(g.shape, F32)
    return pl.pallas_call(body, name=name, out_shape=[shp] * 3, compiler_params=_params())(g, w, m, v)


def _pack(arrs, lane=128):
    flat = jnp.concatenate([a.reshape(-1).astype(F32) for a in arrs])
    pad = (-flat.shape[0]) % (8 * lane)
    return jnp.pad(flat, (0, pad)).reshape(-1, lane)


def _unpack(packed, shapes):
    flat, out, off = packed.reshape(-1), [], 0
    for s in shapes:
        sz = math.prod(s)
        out.append(flat[off:off + sz].reshape(s))
        off += sz
    return out


def _pad_in_cols(wfull):
    zeros = jnp.zeros((wfull.shape[0], SHIFT_PAD - SHIFT_COLS), wfull.dtype)
    return jnp.concatenate([wfull[:, :SHIFT_COLS], zeros, wfull[:, SHIFT_COLS:]], axis=1)


def _unpad_in_cols(gfull):
    return jnp.concatenate([gfull[:, :SHIFT_COLS], gfull[:, SHIFT_PAD:]], axis=1)


def _block_diag2(wd):
    z = jnp.zeros_like(wd[0])
    return jnp.concatenate([jnp.concatenate([wd[0], z], axis=1), jnp.concatenate([z, wd[1]], axis=1)], axis=0)


def _unblock_diag2(g):
    return jnp.stack([g[0:64, 0:RW], g[64:128, RW:2 * RW]])


SMALL_SHARDED = ("w0", "w2", "a0", "a2", "g2", "conv_dw")
SMALL_REPL = ("mu_prev", "mu_next", "k_k", "k_a", "r_k", "lnx_g", "lnx_b", "conv_b", "conv_ln_g", "conv_ln_b",
              "ln1_g", "ln1_b", "ln2_g", "ln2_b", "ln3_g", "ln3_b")
BIG = ("ffn1_w_in", "ffn1_w_out", "w_in", "w_out", "ffn2_w_in", "ffn2_w_out")
WEIGHTS = ("ffn1_w_in", "ffn1_w_out", "w_in", "mu_prev", "mu_next", "w0", "w2", "a0", "a2", "g2", "k_k", "k_a",
           "r_k", "lnx_g", "lnx_b", "conv_dw", "conv_b", "conv_ln_g", "conv_ln_b", "w_out", "ffn2_w_in",
           "ffn2_w_out", "ln1_g", "ln1_b", "ln2_g", "ln2_b", "ln3_g", "ln3_b")


def _full_small(name, full):
    if name in ("w0", "a0"):
        return full.reshape(1, 2 * RW)
    if name in ("w2", "a2"):
        return _block_diag2(full)
    if name == "g2":
        return jnp.pad(full, ((0, 256 - GATE_LORA), (0, 0)))
    if name == "conv_dw":
        return jnp.pad(full, ((0, 1), (0, 0)))
    if name in ("mu_prev", "mu_next"):
        return jnp.pad(full.reshape(1, SHIFT_COLS), ((0, 0), (0, SHIFT_PAD - SHIFT_COLS)))
    return full.reshape(1, -1)


def _grad_small(name, g):
    if name in ("w0", "a0"):
        return g.reshape(2, RW)
    if name in ("w2", "a2"):
        return _unblock_diag2(g)
    if name == "g2":
        return g[:GATE_LORA]
    if name == "conv_dw":
        return g[:CONV_K]
    if name in ("mu_prev", "mu_next"):
        return g[0, :SHIFT_COLS]
    if name == "r_k":
        return g.reshape(N_HEADS, HEAD)
    return g.reshape(-1)


def kernel(x, ffn1_w_in, ffn1_w_out, w_in, mu_prev, mu_next, w0, w2, a0, a2, g2, k_k, k_a, r_k, lnx_g, lnx_b, conv_dw, conv_b, conv_ln_g, conv_ln_b, w_out, ffn2_w_in, ffn2_w_out, ln1_g, ln1_b, ln2_g, ln2_b, ln3_g, ln3_b, loss_target, m_ffn1_w_in, m_ffn1_w_out, m_w_in, m_mu_prev, m_mu_next, m_w0, m_w2, m_a0, m_a2, m_g2, m_k_k, m_k_a, m_r_k, m_lnx_g, m_lnx_b, m_conv_dw, m_conv_b, m_conv_ln_g, m_conv_ln_b, m_w_out, m_ffn2_w_in, m_ffn2_w_out, m_ln1_g, m_ln1_b, m_ln2_g, m_ln2_b, m_ln3_g, m_ln3_b, v_ffn1_w_in, v_ffn1_w_out, v_w_in, v_mu_prev, v_mu_next, v_w0, v_w2, v_a0, v_a2, v_g2, v_k_k, v_k_a, v_r_k, v_lnx_g, v_lnx_b, v_conv_dw, v_conv_b, v_conv_ln_g, v_conv_ln_b, v_w_out, v_ffn2_w_in, v_ffn2_w_out, v_ln1_g, v_ln1_b, v_ln2_g, v_ln2_b, v_ln3_g, v_ln3_b):
    args = dict(locals())
    wsh = {n: args[n][0] for n in WEIGHTS}
    msh = {n: args["m_" + n][0] for n in WEIGHTS}
    vsh = {n: args["v_" + n][0] for n in WEIGHTS}
    me = 4 * lax.axis_index("x") + 2 * lax.axis_index("y") + lax.axis_index("c")
    bf = {n: wsh[n].astype(BF16) for n in BIG}

    w = {"ffn1_w_in": _exchange("gather_ffn1_w_in", [bf["ffn1_w_in"]], True)[0]}
    for n in SMALL_REPL:
        w[n] = _full_small(n, wsh[n])
    small_shapes = [wsh[n].shape for n in SMALL_SHARDED]

    def finish1(got):
        f1_out, w_in_g, small = got
        cols = zip(*[_unpack(small[dv], small_shapes) for dv in range(N_DEV)])
        out = {n: _full_small(n, jnp.concatenate(s, axis=-1)) for n, s in zip(SMALL_SHARDED, cols)}
        out["ffn1_w_out"] = f1_out.reshape(D_FF, D_MODEL)
        out["w_in"] = _pad_in_cols(w_in_g.transpose(1, 0, 2).reshape(D_MODEL, IN_COLS))
        return out

    def finish2(got):
        w_out_g, f2_in, f2_out = got
        return {"w_out": w_out_g.reshape(D_MODEL, D_MODEL), "ffn2_w_in": f2_in,
                "ffn2_w_out": f2_out.reshape(D_FF, D_MODEL)}

    ex = {"g1": ([bf["ffn1_w_out"], bf["w_in"], _pack([wsh[n] for n in SMALL_SHARDED])], finish1),
          "g2": ([bf["w_out"], bf["ffn2_w_in"], bf["ffn2_w_out"]], finish2)}
    loss_part, grad_x, gr, parts = _local_step(x, loss_target, w, ex)
    loss = lax.psum(loss_part, ("x", "y", "c"))

    out = {n: _adam("adam_" + n, parts[n], wsh[n], msh[n], vsh[n]) for n in BIG}
    small_names = SMALL_SHARDED + SMALL_REPL
    full_shapes = [_grad_small(n, gr[n]).shape for n in small_names]
    summed = _unpack(_sum8("sum_small_grads", parts["small"]), full_shapes)
    mine = []
    for n, g in zip(small_names, summed):
        if n in SMALL_SHARDED:
            g = lax.dynamic_slice_in_dim(g, me * HEAD, HEAD, axis=g.ndim - 1)
        mine.append(g)
    shapes = [g.shape for g in mine]
    d_s, m_s, v_s = _adam_small("adam_small", _pack(mine), _pack([wsh[n] for n in small_names]),
                                _pack([msh[n] for n in small_names]), _pack([vsh[n] for n in small_names]))
    for n, g, dl, mn, vn in zip(small_names, mine, _unpack(d_s, shapes), _unpack(m_s, shapes), _unpack(v_s, shapes)):
        out[n] = (g, dl, mn, vn)

    res = [loss, grad_x]
    for k in range(4):
        res += [out[n][k][None] for n in WEIGHTS]
    return tuple(res)
```

```python
import functools
import math

import jax
import jax.numpy as jnp
from jax import lax
from jax.experimental import pallas as pl
from jax.experimental.pallas import tpu as pltpu

F32 = jnp.float32
BF16 = jnp.bfloat16

N_DEV = 8
D_MODEL = 1024
RW = 512
N_HEADS = 8
HEAD = 64
CW = 512
CONV_K = 31
D_FF = 2816
FF_BLK = 704
GATE_LORA = 160
SHIFT_COLS = 1952
SHIFT_PAD = 2048
IN_COLS = 2976
IN_PAD = 3072
LN_EPS = 1e-5
GN_EPS = 64e-5
NORM_EPS = 1e-12
ALPHA = 2.0 ** 0.25
DECAY_SCALE = math.exp(-0.5)
CHUNK = 64
ADAM_LR, ADAM_B1, ADAM_B2, ADAM_EPS, ADAM_WD, ADAM_STEP = 0.001, 0.9, 0.999, 1e-8, 0.01, 10
VMEM_LIMIT = 56 * 1024 * 1024

_DN = {"nn": (((1,), (0,)), ((), ())), "nt": (((1,), (1,)), ((), ())), "tn": (((0,), (0,)), ((), ()))}


def _params():
    return pltpu.CompilerParams(vmem_limit_bytes=VMEM_LIMIT)


def _dot(a, b, dims="nn"):
    return lax.dot_general(a, b, _DN[dims], preferred_element_type=F32)


def _split(x):
    hi = x.astype(BF16)
    return hi, (x - hi.astype(F32)).astype(BF16)


def _dot3_impl(a, b, dims):
    ah, al = _split(a)
    bh, bl = _split(b)
    ka, kb = _DN[dims][0][0][0], _DN[dims][0][1][0]
    return _dot(jnp.concatenate([ah, ah, al], axis=ka), jnp.concatenate([bh, bl, bh], axis=kb), dims)


@functools.partial(jax.custom_vjp, nondiff_argnums=(2,))
def _dot3(a, b, dims="nn"):
    return _dot3_impl(a, b, dims)


def _dot3_fwd(a, b, dims):
    return _dot3_impl(a, b, dims), (a, b)


def _dot3_bwd(dims, res, ct):
    a, b = res
    if dims == "nn":
        return _dot3_impl(ct, b, "nt"), _dot3_impl(a, ct, "tn")
    if dims == "nt":
        return _dot3_impl(ct, b, "nn"), _dot3_impl(ct, a, "tn")
    return _dot3_impl(b, ct, "nt"), _dot3_impl(a, ct, "nn")


_dot3.defvjp(_dot3_fwd, _dot3_bwd)


def _dot1_impl(a, b, dims):
    return _dot(a.astype(BF16), b.astype(BF16), dims)


@functools.partial(jax.custom_vjp, nondiff_argnums=(2,))
def _dot1(a, b, dims="nn"):
    return _dot1_impl(a, b, dims)


def _dot1_bwd(dims, res, ct):
    a, b = res
    if dims == "nn":
        return _dot1_impl(ct, b, "nt"), _dot1_impl(a, ct, "tn")
    if dims == "nt":
        return _dot1_impl(ct, b, "nn"), _dot1_impl(ct, a, "tn")
    return _dot1_impl(b, ct, "nt"), _dot1_impl(a, ct, "nn")


_dot1.defvjp(lambda a, b, dims: (_dot1_impl(a, b, dims), (a, b)), _dot1_bwd)


def _dot1_two(a, b, dims="nn"):
    ax_a, ax_b = {"nn": (0, 1), "nt": (0, 0), "tn": (1, 1)}[dims]
    out = []
    for i in range(0, len(a), 2):
        r = _dot1(jnp.concatenate(a[i:i + 2], axis=ax_a), jnp.concatenate(b[i:i + 2], axis=ax_b), dims)
        m, n = r.shape[0] // 2, r.shape[1] // 2
        out += [r[:m, :n], r[m:, n:]]
    return out


def _tri_inv_impl(l, eye):
    steps = int(math.log2(CHUNK)) - 1
    tm = [eye + x for x in l]
    lp = _dot1_two(l, l)
    for k in range(steps):
        tm_lp = _dot1_two(tm, lp)
        lp = _dot1_two(lp, lp) if k < steps - 1 else None
        tm = [t + x for t, x in zip(tm, tm_lp)]
    return tm


@jax.custom_vjp
def _tri_inv(l, eye):
    return _tri_inv_impl(l, eye)


def _tri_inv_fwd(l, eye):
    tm = _tri_inv_impl(l, eye)
    return tm, (tm, eye)


def _tri_inv_bwd(res, ct):
    tm, eye = res
    return _dot1_two(_dot1_two(tm, ct, "tn"), tm, "nt"), jnp.zeros_like(eye)


_tri_inv.defvjp(_tri_inv_fwd, _tri_inv_bwd)


def _ones_impl(x, g3):
    x1 = x.astype(BF16)
    r1 = x - x1.astype(F32)
    x2 = r1.astype(BF16)
    x3 = (r1 - x2.astype(F32)).astype(BF16)
    return _dot(jnp.concatenate([x1, x2, x3], axis=1), g3)


@jax.custom_vjp
def _head_sum(x, g3):
    return _ones_impl(x, g3)


_head_sum.defvjp(lambda x, g3: (_ones_impl(x, g3), g3), lambda g3, ct: (_ones_impl(ct, g3), jnp.zeros_like(g3)))


def _prefix_sum(x):
    row = lax.broadcasted_iota(jnp.int32, x.shape, 0)
    sh = 1
    while sh < x.shape[0]:
        x = x + jnp.where(row >= sh, pltpu.roll(x, sh, 0), 0.0)
        sh *= 2
    return x


def _dir_cumsum_impl(x, sgn):
    pre = _prefix_sum(x)
    return jnp.where(sgn > 0.0, pre, jnp.sum(x, axis=0, keepdims=True) - pre + x)


@jax.custom_vjp
def _dir_cumsum(x, sgn):
    return _dir_cumsum_impl(x, sgn)


_dir_cumsum.defvjp(lambda x, sgn: (_dir_cumsum_impl(x, sgn), sgn),
                   lambda sgn, ct: (_dir_cumsum_impl(ct, -sgn), jnp.zeros_like(sgn)))


def _sigmoid(x):
    return 1.0 / (1.0 + jnp.exp(-x))


def _mesh_pos():
    return lax.axis_index("x"), lax.axis_index("y"), lax.axis_index("c")


def _peer(pos, q):
    x, y, c = pos
    return (1 - x if q & 4 else x, 1 - y if q & 2 else y, 1 - c if q & 1 else c)


def _linear(pos):
    return 4 * pos[0] + 2 * pos[1] + pos[2]


def _exchange_copies(x_refs, o_refs, send_sems, recv_sems, local_sems, gather):
    pos = _mesh_pos()
    me = _linear(pos)
    starts, wait_recv, wait_send, wait_local = [], [], [], []
    for t in range(len(x_refs)):
        src = x_refs[t] if gather else x_refs[t].at[me]
        cp = pltpu.make_async_copy(src, o_refs[t].at[me], local_sems.at[t])
        starts.append(cp.start)
        wait_local.append(cp.wait)
    for q in range(1, N_DEV):
        peer = _peer(pos, q)
        for t in range(len(x_refs)):
            src = x_refs[t] if gather else x_refs[t].at[_linear(peer)]
            sems = dict(send_sem=send_sems.at[t, q - 1], recv_sem=recv_sems.at[t, q - 1],
                        device_id=peer, device_id_type=pl.DeviceIdType.MESH)
            send = pltpu.make_async_remote_copy(src_ref=src, dst_ref=o_refs[t].at[me], **sems)
            recv = pltpu.make_async_remote_copy(src_ref=src, dst_ref=o_refs[t].at[_linear(peer)], **sems)
            starts.append(send.start)
            wait_recv.append(recv.wait_recv)
            wait_send.append(send.wait_send)
    return starts, wait_recv + wait_send + wait_local


def _exchange_shapes(xs, gather):
    return [jax.ShapeDtypeStruct((N_DEV,) + (x.shape if gather else x.shape[1:]), x.dtype) for x in xs]


def _exchange_sems(nt):
    return [pltpu.SemaphoreType.DMA((nt, N_DEV - 1)), pltpu.SemaphoreType.DMA((nt, N_DEV - 1)),
            pltpu.SemaphoreType.DMA((nt,))]


def _exchange(name, xs, gather):
    nt = len(xs)

    def body(*refs):
        starts, waits = _exchange_copies(refs[:nt], refs[nt:2 * nt], *refs[2 * nt:], gather)
        for f in starts:
            f()
        for f in waits:
            f()

    any_spec = pl.BlockSpec(memory_space=pl.ANY)
    return pl.pallas_call(
        body, name=name, in_specs=[any_spec] * nt, out_specs=[any_spec] * nt,
        out_shape=_exchange_shapes(xs, gather), scratch_shapes=_exchange_sems(nt))(*xs)


def _call(name, body, grid, ins, in_specs, out_specs, out_shape, scratch=(), exch=None):
    if exch is None:
        return pl.pallas_call(body, name=name, grid=grid, in_specs=in_specs, out_specs=out_specs,
                              out_shape=out_shape, scratch_shapes=list(scratch), compiler_params=_params())(*ins)
    xs, gather = exch
    single = not isinstance(out_shape, (list, tuple))
    o_specs = [out_specs] if single else list(out_specs)
    o_shape = [out_shape] if single else list(out_shape)
    n_in, n_out, n_x, n_scr = len(ins), len(o_shape), len(xs), len(scratch)

    def wrapped(*refs):
        in_refs = refs[:n_in]
        x_refs = refs[n_in:n_in + n_x]
        out_refs = refs[n_in + n_x:n_in + n_x + n_out]
        got_refs = refs[n_in + n_x + n_out:n_in + 2 * n_x + n_out]
        rest = refs[n_in + 2 * n_x + n_out:]
        starts, waits = _exchange_copies(x_refs, got_refs, *rest[n_scr:], gather)
        ids = [pl.program_id(i) for i in range(len(grid))]
        first = functools.reduce(lambda p, q: p & q, [i == 0 for i in ids])
        last = functools.reduce(lambda p, q: p & q, [i == g - 1 for i, g in zip(ids, grid)])

        @pl.when(first)
        def _():
            for f in starts:
                f()

        body(*in_refs, *out_refs, *rest[:n_scr])

        @pl.when(last)
        def _():
            for f in waits:
                f()

    any_spec = pl.BlockSpec(memory_space=pl.ANY)
    outs = pl.pallas_call(
        wrapped, name=name, grid=grid, in_specs=list(in_specs) + [any_spec] * n_x,
        out_specs=o_specs + [any_spec] * n_x, out_shape=o_shape + _exchange_shapes(xs, gather),
        scratch_shapes=list(scratch) + _exchange_sems(n_x), compiler_params=_params())(*ins, *xs)
    res = outs[:n_out]
    return (res[0] if single else res), outs[n_out:]


def _mm_call(name, dims, grid, red_axis, ins, in_specs, out_shape, out_spec, acc_shape,
             scale=1.0, add_scale=None, exch=None):
    nred = grid[red_axis]

    def body(*refs):
        if add_scale is None:
            a_ref, b_ref, o_ref, acc_ref = refs
            add_ref = None
        else:
            a_ref, b_ref, add_ref, o_ref, acc_ref = refs
        k = pl.program_id(red_axis)

        @pl.when(k == 0)
        def _():
            acc_ref[...] = jnp.zeros_like(acc_ref)

        acc_ref[...] += _dot(a_ref[...].astype(BF16), b_ref[...].astype(BF16), dims)

        @pl.when(k == nred - 1)
        def _():
            r = acc_ref[...]
            if scale != 1.0:
                r = r * scale
            if add_ref is not None:
                r = r + add_scale * add_ref[...].astype(F32)
            o_ref[...] = r.astype(o_ref.dtype)

    return _call(name, body, grid, ins, in_specs, out_spec, out_shape, [pltpu.VMEM(acc_shape, F32)], exch)


def _tile(dim, cap):
    t = min(dim, cap)
    while dim % t or t % 128:
        t -= 128
        assert t > 0, (dim, cap)
    return t


def _mm(name, a, b, dims, out_dtype, scale=1.0, add=None, add_scale=None, tm=512, tn=1024, tk=1024):
    if dims == "tn":
        kd, m = a.shape
        n = b.shape[1]
    else:
        m, kd = a.shape
        n = b.shape[1] if dims == "nn" else b.shape[0]
    tm, tn, tk = _tile(m, tm), _tile(n, tn), _tile(kd, tk)
    a_spec = (pl.BlockSpec((tk, tm), lambda i, j, k: (k, i)) if dims == "tn"
              else pl.BlockSpec((tm, tk), lambda i, j, k: (i, k)))
    b_spec = (pl.BlockSpec((tn, tk), lambda i, j, k: (j, k)) if dims == "nt"
              else pl.BlockSpec((tk, tn), lambda i, j, k: (k, j)))
    o_spec = pl.BlockSpec((tm, tn), lambda i, j, k: (i, j))
    ins, specs = [a, b], [a_spec, b_spec]
    if add is not None:
        ins.append(add)
        specs.append(o_spec)
    return _mm_call(name, dims, (m // tm, n // tn, kd // tk), 2, ins, specs,
                    jax.ShapeDtypeStruct((m, n), out_dtype), o_spec, (tm, tn),
                    scale=scale, add_scale=add_scale if add is not None else None)


def _ffn_in(name, x, wg, tm=512, exch=None):
    n = x.shape[0]
    nj = N_DEV // 2

    def body(x_ref, wgate_ref, wup_ref, hg_ref, hu_ref, act_ref):
        xb = x_ref[...].astype(BF16)
        g = _dot(xb, wgate_ref[...])
        u = _dot(xb, wup_ref[...])
        hg_ref[...] = g.astype(BF16)
        hu_ref[...] = u.astype(BF16)
        act_ref[...] = (g * _sigmoid(g) * u).astype(BF16)

    blk = pl.BlockSpec((None, tm, FF_BLK), lambda j, i: (j, i, 0))
    shp = jax.ShapeDtypeStruct((nj, n, FF_BLK), BF16)
    return _call(name, body, (nj, n // tm), [x, wg, wg],
                 [pl.BlockSpec((tm, D_MODEL), lambda j, i: (i, 0)),
                  pl.BlockSpec((None, D_MODEL, FF_BLK), lambda j, i: (j, 0, 0)),
                  pl.BlockSpec((None, D_MODEL, FF_BLK), lambda j, i: (j + nj, 0, 0))],
                 [blk, blk, blk], [shp, shp, shp], exch=exch)


def _ffn_out_bwd(name, dz, wout, hg, hu, tm=512, exch=None):
    n = dz.shape[0]
    nj = N_DEV // 2

    def body(dz_ref, w_ref, hg_ref, hu_ref, dhg_ref, dhu_ref):
        dact = 0.5 * _dot(dz_ref[...].astype(BF16), w_ref[...], "nt")
        g = hg_ref[...].astype(F32)
        u = hu_ref[...].astype(F32)
        s = _sigmoid(g)
        dhg_ref[...] = (dact * u * (s * (1.0 + g * (1.0 - s)))).astype(BF16)
        dhu_ref[...] = (dact * (g * s)).astype(BF16)

    blk = pl.BlockSpec((None, tm, FF_BLK), lambda j, i: (j, i, 0))
    shp = jax.ShapeDtypeStruct((nj, n, FF_BLK), BF16)
    return _call(name, body, (nj, n // tm), [dz, wout, hg, hu],
                 [pl.BlockSpec((tm, D_MODEL), lambda j, i: (i, 0)),
                  pl.BlockSpec((FF_BLK, D_MODEL), lambda j, i: (j, 0)), blk, blk],
                 [blk, blk], [shp, shp], exch=exch)


def _mm_ln(name, a, b, xres, g, beta, c, tgt=None, a_blocked=False, tm=512, tk=512):
    if a_blocked:
        nk, n, kb = a.shape
        a_spec = pl.BlockSpec((None, tm, kb), lambda i, k: (k, i, 0))
    else:
        n, kd = a.shape
        kb = _tile(kd, tk)
        nk = kd // kb
        a_spec = pl.BlockSpec((tm, kb), lambda i, k: (i, k))
    d = b.shape[1]
    with_loss = tgt is not None

    def body(*refs):
        if with_loss:
            a_ref, b_ref, x_ref, g_ref, be_ref, t_ref, o_ref, z_ref, l_ref, acc_ref = refs
        else:
            a_ref, b_ref, x_ref, g_ref, be_ref, o_ref, z_ref, acc_ref = refs
        i, k = pl.program_id(0), pl.program_id(1)

        @pl.when(k == 0)
        def _():
            acc_ref[...] = jnp.zeros_like(acc_ref)

        acc_ref[...] += _dot(a_ref[...].astype(BF16), b_ref[...].astype(BF16))

        @pl.when(k == nk - 1)
        def _():
            z = ALPHA * x_ref[...] + c * acc_ref[...]
            z_ref[...] = z
            mu = jnp.mean(z, axis=-1, keepdims=True)
            zc = z - mu
            var = jnp.mean(zc * zc, axis=-1, keepdims=True)
            y = zc * lax.rsqrt(var + LN_EPS) * g_ref[...] + be_ref[...]
            if with_loss:
                err = y - t_ref[...]
                o_ref[...] = err * (1.0 / d)
                part = 0.5 * jnp.sum(jnp.sum(err * err, axis=-1, keepdims=True), axis=0, keepdims=True) * (1.0 / d)

                @pl.when(i == 0)
                def _():
                    l_ref[...] = jnp.zeros_like(l_ref)

                l_ref[...] += jnp.broadcast_to(part, l_ref.shape)
            else:
                o_ref[...] = y

    row = pl.BlockSpec((tm, d), lambda i, k: (i, 0))
    vec = pl.BlockSpec((1, d), lambda i, k: (0, 0))
    ins = [a, b, xres, g, beta]
    in_specs = [a_spec, pl.BlockSpec((kb, d), lambda i, k: (k, 0)), row, vec, vec]
    out_specs = [row, row]
    out_shape = [jax.ShapeDtypeStruct((n, d), F32), jax.ShapeDtypeStruct((n, d), F32)]
    if with_loss:
        ins.append(tgt)
        in_specs.append(row)
        out_specs.append(pl.BlockSpec((1, 128), lambda i, k: (0, 0)))
        out_shape.append(jax.ShapeDtypeStruct((1, 128), F32))
    return _call(name, body, (n // tm, nk), ins, in_specs, out_specs, out_shape, [pltpu.VMEM((tm, d), F32)])


def _rowwise(name, fn, rows, params, out_rows, out_accs, tm=256, exch=None):
    specs, ins = [], []
    for r in rows:
        arr, w, cb = r if isinstance(r, tuple) else (r, r.shape[1], 0)
        ins.append(arr)
        specs.append(pl.BlockSpec((tm, w), functools.partial(lambda i, cb: (i, cb), cb=cb)))
    n = ins[0].shape[0]
    for p in params:
        ins.append(p)
        specs.append(pl.BlockSpec(p.shape, lambda i: (0, 0)))
    n_in, n_or = len(ins), len(out_rows)

    def body(*refs):
        outs = fn(*[r[...] for r in refs[:n_in]])
        o_refs = refs[n_in:]
        for o_ref, o in zip(o_refs[:n_or], outs[:n_or]):
            o_ref[...] = o.astype(o_ref.dtype)
        if out_accs:
            @pl.when(pl.program_id(0) == 0)
            def _():
                for a_ref in o_refs[n_or:]:
                    a_ref[...] = jnp.zeros_like(a_ref)

            for a_ref, a in zip(o_refs[n_or:], outs[n_or:]):
                a_ref[...] += a.astype(F32)

    out_specs = [pl.BlockSpec((tm, w), lambda i: (i, 0)) for w, _ in out_rows]
    out_specs += [pl.BlockSpec(s, lambda i: (0, 0)) for s in out_accs]
    out_shape = [jax.ShapeDtypeStruct((n, w), dt) for w, dt in out_rows]
    out_shape += [jax.ShapeDtypeStruct(s, F32) for s in out_accs]
    return _call(name, body, (n // tm,), ins, specs, out_specs, out_shape, exch=exch)


def _vjp_of(fn, n_in):
    def g(*args):
        ins, cts = args[:n_in], args[n_in:]
        outs, pull = jax.vjp(fn, *ins)
        return pull(tuple(c.astype(o.dtype) for c, o in zip(cts, outs)))
    return g


def _ln_bwd(name, z, g, ct):
    def fn(zt, ct_, gt):
        mu = jnp.mean(zt, axis=-1, keepdims=True)
        zc = zt - mu
        rstd = lax.rsqrt(jnp.mean(zc * zc, axis=-1, keepdims=True) + LN_EPS)
        xh = zc * rstd
        dxh = ct_ * gt
        dz = rstd * (dxh - jnp.mean(dxh, axis=-1, keepdims=True)
                     - xh * jnp.mean(dxh * xh, axis=-1, keepdims=True))
        return dz, jnp.sum(ct_ * xh, axis=0, keepdims=True), jnp.sum(ct_, axis=0, keepdims=True)

    d = z.shape[1]
    return _rowwise(name, fn, [z, ct], [g], [(d, F32)], [(1, d), (1, d)])


def _shift(name, src, mu_a, mu_b, q=None, tt=256):
    bsz, t, _ = src.shape
    nt, r8, w = t // tt, tt // 8, SHIFT_PAD
    with_q = q is not None

    def body(cur_ref, prev_ref, next_ref, *rest):
        b, i = pl.program_id(0), pl.program_id(1)
        cur = cur_ref[...]
        prow = jnp.where(i > 0, prev_ref[7:8, :], 0.0)
        nrow = jnp.where(i < nt - 1, next_ref[0:1, :], 0.0)
        rid = lax.broadcasted_iota(jnp.int32, cur.shape, 0)
        dprev = jnp.where(rid == 0, prow, pltpu.roll(cur, 1, 0)) - cur
        dnext = jnp.where(rid == tt - 1, nrow, pltpu.roll(cur, tt - 1, 0)) - cur
        if with_q:
            q_ref, da_ref, db_ref = rest

            @pl.when((b == 0) & (i == 0))
            def _():
                da_ref[...] = jnp.zeros_like(da_ref)
                db_ref[...] = jnp.zeros_like(db_ref)

            qv = q_ref[...]
            da_ref[...] += jnp.sum(qv * dprev, axis=0, keepdims=True)
            db_ref[...] += jnp.sum(qv * dnext, axis=0, keepdims=True)
        else:
            ma_ref, mb_ref, o_ref = rest
            o_ref[...] = cur + ma_ref[...] * dprev + mb_ref[...] * dnext

    cur_spec = pl.BlockSpec((None, tt, w), lambda b, i: (b, i, 0))
    in_specs = [cur_spec,
                pl.BlockSpec((None, 8, w), lambda b, i: (b, jnp.maximum(i * r8 - 1, 0), 0)),
                pl.BlockSpec((None, 8, w), lambda b, i: (b, jnp.minimum((i + 1) * r8, t // 8 - 1), 0))]
    vec = pl.BlockSpec((1, w), lambda b, i: (0, 0))
    if with_q:
        return _call(name, body, (bsz, nt), [src, src, src, q], in_specs + [cur_spec], [vec, vec],
                     [jax.ShapeDtypeStruct((1, w), F32)] * 2)
    return _call(name, body, (bsz, nt), [src, src, src, mu_a, mu_b], in_specs + [vec, vec], cur_spec,
                 jax.ShapeDtypeStruct((bsz, t, w), F32))


def _halo_specs(t, tt, w):
    r16 = tt // 16
    return [pl.BlockSpec((None, tt, w), lambda b, i: (b, i, 0)),
            pl.BlockSpec((None, 16, w), lambda b, i: (b, jnp.maximum(i * r16 - 1, 0), 0)),
            pl.BlockSpec((None, 16, w), lambda b, i: (b, jnp.minimum((i + 1) * r16, t // 16 - 1), 0))]


def _fill_pad(pad_ref, cur_ref, prev_ref, next_ref, i, nt, tt):
    pad_ref[0:16, :] = jnp.where(i > 0, prev_ref[...], 0.0)
    pad_ref[16:16 + tt, :] = cur_ref[...]
    pad_ref[16 + tt:32 + tt, :] = jnp.where(i < nt - 1, next_ref[...], 0.0)


def _dwconv(name, u, dw32, bias, flip, tt=512):
    bsz, t, w = u.shape
    tt = min(tt, t)
    nt = t // tt

    def body(cur_ref, prev_ref, next_ref, dw_ref, b_ref, o_ref, pad_ref):
        i = pl.program_id(1)
        _fill_pad(pad_ref, cur_ref, prev_ref, next_ref, i, nt, tt)
        acc = jnp.broadcast_to(b_ref[...], (tt, w))
        for k in range(CONV_K):
            kk = CONV_K - 1 - k if flip else k
            acc = acc + pad_ref[pl.ds(1 + k, tt), :] * dw_ref[kk:kk + 1, :]
        o_ref[...] = acc

    return _call(name, body, (bsz, nt), [u, u, u, dw32, bias],
                 _halo_specs(t, tt, w) + [pl.BlockSpec((32, w), lambda b, i: (0, 0)),
                                          pl.BlockSpec((1, w), lambda b, i: (0, 0))],
                 pl.BlockSpec((None, tt, w), lambda b, i: (b, i, 0)), jax.ShapeDtypeStruct((bsz, t, w), F32),
                 [pltpu.VMEM((tt + 32, w), F32)])


def _dwconv_dw(name, u, dc, tt=512):
    bsz, t, w = u.shape
    tt = min(tt, t)
    nt = t // tt

    def body(cur_ref, prev_ref, next_ref, dc_ref, ddw_ref, db_ref, pad_ref):
        b, i = pl.program_id(0), pl.program_id(1)
        _fill_pad(pad_ref, cur_ref, prev_ref, next_ref, i, nt, tt)

        @pl.when((b == 0) & (i == 0))
        def _():
            ddw_ref[...] = jnp.zeros_like(ddw_ref)
            db_ref[...] = jnp.zeros_like(db_ref)

        dcv = dc_ref[...]
        db_ref[...] += jnp.sum(dcv, axis=0, keepdims=True)
        for k in range(CONV_K):
            ddw_ref[k:k + 1, :] += jnp.sum(dcv * pad_ref[pl.ds(1 + k, tt), :], axis=0, keepdims=True)

    return _call(name, body, (bsz, nt), [u, u, u, dc],
                 _halo_specs(t, tt, w) + [pl.BlockSpec((None, tt, w), lambda b, i: (b, i, 0))],
                 [pl.BlockSpec((32, w), lambda b, i: (0, 0)), pl.BlockSpec((1, w), lambda b, i: (0, 0))],
                 [jax.ShapeDtypeStruct((32, w), F32), jax.ShapeDtypeStruct((1, w), F32)],
                 [pltpu.VMEM((tt + 32, w), F32)])


PAIR = 2 * HEAD
N_PAIRS = RW // PAIR


def _chunk_pairs(s, r, lw, k, v, kk, a, sgn):
    n, m = CHUNK, 2 * CHUNK
    in_a = lax.broadcasted_iota(jnp.int32, (n, PAIR), 1) < HEAD

    def stack2(z):
        return jnp.concatenate([jnp.where(in_a, z, 0.0), jnp.where(in_a, 0.0, z)], axis=0)

    def each(f, *lists):
        return [f(*z) for z in zip(*lists)]

    sgn_f = sgn.astype(F32)
    row2 = lax.broadcasted_iota(jnp.int32, (m, m), 0)
    col2 = lax.broadcasted_iota(jnp.int32, (m, m), 1)
    same = (row2 >= n) == (col2 >= n)
    dlt = ((row2 & (n - 1)) - (col2 & (n - 1))) * sgn
    incl, strict = same & (dlt >= 0), same & (dlt > 0)
    eye = jnp.where(row2 == col2, 1.0, 0.0)

    cum = each(lambda lw_: _dir_cumsum(lw_, sgn_f), lw)
    tot = each(lambda lw_: jnp.sum(lw_, axis=0, keepdims=True), lw)
    e_neg = each(lambda c_: jnp.exp(-c_), cum)
    e_rest = each(lambda t_, c_: jnp.exp(t_ - c_), tot, cum)
    beta = each(lambda kk_, a_: kk_ * a_, kk, a)
    lhs = each(lambda kk_, c_, lw_, r_: jnp.concatenate(
        [stack2(-kk_ * jnp.exp(c_ - lw_)), stack2(r_ * jnp.exp(c_))], axis=0), kk, cum, lw, r)
    rhs = each(lambda b_, k_, e_: jnp.concatenate([stack2(b_ * e_), stack2(k_ * e_)], axis=0), beta, k, e_neg)
    sc = each(lambda l_, r_: _dot3(l_, r_, "nt"), lhs, rhs)
    l_ab = each(lambda sc_: jnp.where(strict, sc_[0:m, 0:m], 0.0), sc)
    l_ak = each(lambda sc_: jnp.where(strict, sc_[0:m, m:2 * m], 0.0), sc)
    m_r = each(lambda sc_: jnp.where(jnp.concatenate([incl, incl], axis=1), sc_[m:2 * m, :], 0.0), sc)
    tm = _tri_inv(l_ab, eye)
    z = _dot1_two(lhs, s, "nt")
    v2 = each(stack2, v)
    u2 = _dot1_two(tm, each(lambda z_, lv_: z_[0:m] + lv_, z, _dot1_two(l_ak, v2)))
    uv = each(lambda u_, v_: jnp.concatenate([u_, v_], axis=0), u2, v2)
    y2 = each(lambda z_, mu_: z_[m:2 * m] + mu_, z, _dot1_two(m_r, uv))
    bk = each(lambda b_, k_, e_: jnp.concatenate([stack2(b_ * e_), stack2(k_ * e_)], axis=0), beta, k, e_rest)
    s_new = each(lambda s_, t_, d_: s_ * jnp.exp(t_) + d_, s, tot, _dot1_two(uv, bk, "tn"))
    return each(lambda y_: y_[0:n] + y_[n:m], y2), s_new


SCAN_SEQS = 2
N_CHAINS = SCAN_SEQS * N_PAIRS


def _pair_tiles(ref):
    return [ref[q, :, p * PAIR:(p + 1) * PAIR] for q in range(SCAN_SEQS) for p in range(N_PAIRS)]


def _store_tiles(ref, tiles):
    for q in range(SCAN_SEQS):
        for p in range(N_PAIRS):
            ref[q, :, p * PAIR:(p + 1) * PAIR] = tiles[q * N_PAIRS + p]


def _scan_specs(order):
    shared = pl.BlockSpec((SCAN_SEQS, CHUNK, RW), lambda d, b, c: (b, order(d, c), 0))
    per_dir = pl.BlockSpec((SCAN_SEQS, CHUNK, RW), lambda d, b, c: (b, order(d, c), d))
    state = pl.BlockSpec((None, SCAN_SEQS, None, N_PAIRS, PAIR, PAIR), lambda d, b, c: (d, b, order(d, c), 0, 0, 0))
    return shared, per_dir, state


def _scan_fwd(r, v, kk, lw, kd, a, bsz, exch=None):
    n = r.shape[0]
    t = n // bsz
    nc = t // CHUNK

    def order(d, c):
        return c + d * (nc - 1 - 2 * c)

    def body(r_ref, v_ref, kk_ref, lw_ref, kd_ref, a_ref, y_ref, s0_ref, s_ref):
        d, c = pl.program_id(0), pl.program_id(2)

        @pl.when(c == 0)
        def _():
            s_ref[...] = jnp.zeros_like(s_ref)

        s = [s_ref[i] for i in range(N_CHAINS)]
        y, s_new = _chunk_pairs(s, *[_pair_tiles(ref) for ref in (r_ref, lw_ref, kd_ref, v_ref, kk_ref, a_ref)],
                                1 - 2 * d)
        _store_tiles(y_ref, y)
        for i in range(N_CHAINS):
            s0_ref[i // N_PAIRS, i % N_PAIRS] = s[i]
            s_ref[i] = s_new[i]

    shared, per_dir, state = _scan_specs(order)
    seq = lambda z: z.reshape(bsz, t, z.shape[1])
    res = _call("scan_fwd", body, (2, bsz // SCAN_SEQS, nc), [seq(z) for z in (r, v, kk, lw, kd, a)],
                [shared, shared, shared, per_dir, per_dir, per_dir], [per_dir, state],
                [jax.ShapeDtypeStruct((bsz, t, 2 * RW), F32),
                 jax.ShapeDtypeStruct((2, bsz, nc, N_PAIRS, PAIR, PAIR), F32)],
                [pltpu.VMEM((N_CHAINS, PAIR, PAIR), F32)], exch)
    (y, s0), got = res if exch else (res, None)
    y = y.reshape(n, 2 * RW)
    return ([y, s0], got) if exch else [y, s0]


def _scan_bwd(r, v, kk, lw, kd, a, s0, dy, bsz, exch=None):
    n = r.shape[0]
    t = n // bsz
    nc = t // CHUNK

    def order(d, c):
        cc = nc - 1 - c
        return cc + d * (nc - 1 - 2 * cc)

    def body(r_ref, v_ref, kk_ref, lw_ref, kd_ref, a_ref, dy_ref, s0_ref,
             dr_ref, dv_ref, dkk_ref, dlw_ref, dkd_ref, da_ref, ds_ref):
        d, c = pl.program_id(0), pl.program_id(2)

        @pl.when(c == 0)
        def _():
            ds_ref[...] = jnp.zeros_like(ds_ref)

        sgn = 1 - 2 * d
        _, pull = jax.vjp(lambda *ops: _chunk_pairs(*ops, sgn),
                          [s0_ref[i // N_PAIRS, i % N_PAIRS] for i in range(N_CHAINS)],
                          *[_pair_tiles(ref) for ref in (r_ref, lw_ref, kd_ref, v_ref, kk_ref, a_ref)])
        grads = pull((_pair_tiles(dy_ref), [ds_ref[i] for i in range(N_CHAINS)]))
        for i in range(N_CHAINS):
            ds_ref[i] = grads[0][i]
        for o_ref, gx in zip((dr_ref, dlw_ref, dkd_ref, dv_ref, dkk_ref, da_ref), grads[1:]):
            _store_tiles(o_ref, gx)

    shared, per_dir, state = _scan_specs(order)
    shp = jax.ShapeDtypeStruct((bsz, t, 2 * RW), F32)
    seq = lambda z: z.reshape(bsz, t, z.shape[1])
    res = _call("scan_bwd", body, (2, bsz // SCAN_SEQS, nc), [seq(z) for z in (r, v, kk, lw, kd, a, dy)] + [s0],
                [shared, shared, shared, per_dir, per_dir, per_dir, per_dir, state],
                [per_dir] * 6, [shp] * 6, [pltpu.VMEM((N_CHAINS, PAIR, PAIR), F32)], exch)
    outs, got = res if exch else (res, None)
    outs = [z.reshape(n, 2 * RW) for z in outs]
    return (outs, got) if exch else outs


def _prep_fn(ps, w0, w2bd, a0, a2bd, g2p, k_k, k_a, hsum):
    head_sum = lambda z: _head_sum(z, hsum)
    r, k, v = ps[:, 0:RW], ps[:, RW:2 * RW], ps[:, 2 * RW:3 * RW]
    wd, ad, gd = ps[:, 1536:1664], ps[:, 1664:1792], ps[:, 1792:2048]
    logw = -DECAY_SCALE * _sigmoid(_dot1(jnp.tanh(wd), w2bd) + w0)
    a = _sigmoid(_dot1(ad, a2bd) + a0)
    g = _dot1(_sigmoid(gd), g2p)
    kkr = k * k_k
    kk = kkr / jnp.maximum(jnp.sqrt(head_sum(kkr * kkr)), NORM_EPS)
    k2 = jnp.concatenate([k, k], axis=1)
    ka2 = jnp.concatenate([k_a, k_a], axis=1)
    kd = k2 * (1.0 + (a - 1.0) * ka2)
    return r, v, kk, logw, a, kd, g


def _post_fn(y2, r, v, kd, g, lnx_g, lnx_b, r_k, hsum):
    head_sum = lambda z: _head_sum(z, hsum)
    y = y2[:, 0:RW] + y2[:, RW:2 * RW]
    mu = head_sum(y) * (1.0 / HEAD)
    yc = y - mu
    var = head_sum(yc * yc) * (1.0 / HEAD)
    yn = yc * lax.rsqrt(var + GN_EPS) * lnx_g + lnx_b
    bonus = head_sum(r * (kd[:, 0:RW] + kd[:, RW:2 * RW]) * r_k) * v
    return ((yn + bonus) * g,)


def _glu_fn(pa, pb):
    return (pa * _sigmoid(pb),)


def _conv_out_fn(cv, ln_g, ln_b):
    mu = jnp.mean(cv, axis=-1, keepdims=True)
    cc = cv - mu
    var = jnp.mean(cc * cc, axis=-1, keepdims=True)
    y = cc * lax.rsqrt(var + LN_EPS) * ln_g + ln_b
    return (y * _sigmoid(y),)


def _local_step(x, tgt, w, ex=None):
    bsz, t, d = x.shape
    n = bsz * t
    x2d, tgt2d = x.reshape(n, d), tgt.reshape(n, d)
    hsum = jnp.tile(jnp.kron(jnp.eye(N_HEADS, dtype=BF16), jnp.ones((HEAD, HEAD), BF16)), (3, 1))
    w = dict(w)
    parts = {} if ex else None

    def hosted(result, finish=None):
        if not ex:
            return result
        outs, got = result
        if finish is not None:
            w.update(finish(got))
        return outs

    hg1, hu1, act1 = hosted(_ffn_in("ffn1_in", x2d, w["ffn1_w_in"], exch=(ex["g1"][0], True) if ex else None),
                            ex["g1"][1] if ex else None)
    x1, z1 = _mm_ln("ffn1_out_ln1", act1, w["ffn1_w_out"], x2d, w["ln1_g"], w["ln1_b"], 0.5, a_blocked=True)
    p = _mm("w_in_proj", x1, w["w_in"], "nn", F32)
    p3 = p.reshape(bsz, t, IN_PAD)
    ps = _shift("shift_fwd", p3, w["mu_prev"], w["mu_next"]).reshape(n, SHIFT_PAD)
    prep_params = [w["w0"], w["w2"], w["a0"], w["a2"], w["g2"], w["k_k"], w["k_a"], hsum]
    r, v, kk, logw, a, kd, g = _rowwise(
        "rwkv_prep", _prep_fn, [ps], prep_params,
        [(RW, F32), (RW, F32), (RW, F32), (2 * RW, F32), (2 * RW, F32), (2 * RW, F32), (RW, F32)], [])
    y2, s0 = hosted(_scan_fwd(r, v, kk, logw, kd, a, bsz, exch=(ex["g2"][0], True) if ex else None),
                    ex["g2"][1] if ex else None)
    post_params = [w["lnx_g"], w["lnx_b"], w["r_k"], hsum]
    (y_rwkv,) = _rowwise("rwkv_post", _post_fn, [y2, r, v, kd, g], post_params, [(RW, BF16)], [])
    (u,) = _rowwise("conv_glu", _glu_fn, [(p, CW, 4), (p, CW, 5)], [], [(CW, F32)], [])
    cv = _dwconv("conv_dw", u.reshape(bsz, t, CW), w["conv_dw"], w["conv_b"], False).reshape(n, CW)
    (y_conv,) = _rowwise("conv_out", _conv_out_fn, [cv], [w["conv_ln_g"], w["conv_ln_b"]], [(CW, BF16)], [])
    ycat = jnp.concatenate([y_rwkv, y_conv], axis=1)
    x2, z2 = _mm_ln("w_out_ln2", ycat, w["w_out"], x1, w["ln2_g"], w["ln2_b"], 1.0)
    hg2, hu2, act2 = _ffn_in("ffn2_in", x2, w["ffn2_w_in"])
    dx3, z3, loss = _mm_ln("ffn2_out_ln3", act2, w["ffn2_w_out"], x2, w["ln3_g"], w["ln3_b"], 0.5,
                           tgt=tgt2d, a_blocked=True)

    gr = {}
    dz3, gr["ln3_g"], gr["ln3_b"] = _ln_bwd("ln3_bwd", z3, w["ln3_g"], dx3)
    dx2, _ = _ffn_bwd("ffn2", gr, dz3, x2, w["ffn2_w_in"], w["ffn2_w_out"], hg2, hu2, act2)
    dz2, gr["ln2_g"], gr["ln2_b"] = _ln_bwd("ln2_bwd", z2, w["ln2_g"], dx2)
    gr["w_out"] = _mm("w_out_wgrad", ycat, dz2, "tn", BF16, tk=512)
    dycat = _mm("w_out_dgrad", dz2, w["w_out"], "nt", F32)
    conv_out_bwd = _vjp_of(_conv_out_fn, 3)
    dcv, gr["conv_ln_g"], gr["conv_ln_b"] = _rowwise(
        "conv_out_bwd", lambda cv_, ct_, g_, b_: conv_out_bwd(cv_, g_, b_, ct_),
        [cv, (dycat, CW, 1)], [w["conv_ln_g"], w["conv_ln_b"]], [(CW, F32)], [(1, CW), (1, CW)])
    dcv3 = dcv.reshape(bsz, t, CW)
    gr["conv_dw"], gr["conv_b"] = _dwconv_dw("conv_dw_wgrad", u.reshape(bsz, t, CW), dcv3)
    du = _dwconv("conv_dw_dgrad", dcv3, w["conv_dw"], jnp.zeros((1, CW), F32), True).reshape(n, CW)

    def glu_bwd(pa, pb, ct):
        return (jnp.concatenate(_vjp_of(_glu_fn, 2)(pa, pb, ct), axis=1),)

    (dp_conv,) = _rowwise("conv_glu_bwd", glu_bwd, [(p, CW, 4), (p, CW, 5), du], [], [(2 * CW, F32)], [])

    def post_bwd(y2_, r_, v_, kd_, g_, ct, lg, lb, rk, hs):
        return _vjp_of(lambda *z: _post_fn(*z, hs), 8)(y2_, r_, v_, kd_, g_, lg, lb, rk, ct)

    dy2, dr_post, dv_post, dkd_post, dg, gr["lnx_g"], gr["lnx_b"], gr["r_k"] = _rowwise(
        "rwkv_post_bwd", post_bwd, [y2, r, v, kd, g, (dycat, RW, 0)], post_params,
        [(2 * RW, F32), (RW, F32), (RW, F32), (2 * RW, F32), (RW, F32)], [(1, RW), (1, RW), (1, RW)])
    sends = [jnp.concatenate(gr["ffn2_w_in"], axis=0), gr["ffn2_w_out"].reshape(N_DEV, D_FF // N_DEV, D_MODEL),
             gr["w_out"].reshape(N_DEV, D_MODEL // N_DEV, D_MODEL)]
    res = _scan_bwd(r, v, kk, logw, kd, a, s0, dy2, bsz, exch=(sends, False) if ex else None)
    if ex:
        res, got = res
        parts.update(zip(("ffn2_w_in", "ffn2_w_out", "w_out"), got))
    dr_s, dv_s, dkk_s, dlw, dkd_s, da = res

    def prep_bwd(ps_, dr2, dr1, dv2, dv1, dkk2, dlw_, da_, dkd2, dkd1, dg_, *prm):
        half = lambda z: z[:, 0:RW] + z[:, RW:2 * RW]
        return _vjp_of(lambda *z: _prep_fn(*z, prm[-1]), 8)(
            ps_, *prm[:-1], half(dr2) + dr1, half(dv2) + dv1, half(dkk2), dlw_, da_, dkd2 + dkd1, dg_)

    dps, gr["w0"], gr["w2"], gr["a0"], gr["a2"], gr["g2"], gr["k_k"], gr["k_a"] = _rowwise(
        "rwkv_prep_bwd", prep_bwd,
        [ps, dr_s, dr_post, dv_s, dv_post, dkk_s, dlw, da, dkd_s, dkd_post, dg], prep_params,
        [(SHIFT_PAD, F32)], [q.shape for q in prep_params[:-1]])
    dps3 = dps.reshape(bsz, t, SHIFT_PAD)
    gr["mu_prev"], gr["mu_next"] = _shift("shift_dmu", p3, None, None, q=dps3)
    dp_shift = _shift("shift_bwd", dps3, w["mu_next"], w["mu_prev"]).reshape(n, SHIFT_PAD)
    dp = jnp.concatenate([dp_shift, dp_conv], axis=1)
    gr["w_in"] = _mm("w_in_wgrad", x1, dp, "tn", BF16, tk=512)
    dx1 = _mm("w_in_dgrad", dp, w["w_in"], "nt", F32, add=dz2, add_scale=ALPHA)
    dz1, gr["ln1_g"], gr["ln1_b"] = _ln_bwd("ln1_bwd", z1, w["ln1_g"], dx1)
    riders = None
    if ex:
        gw_in = _unpad_in_cols(gr["w_in"]).reshape(D_MODEL, N_DEV, IN_COLS // N_DEV).transpose(1, 0, 2)
        small = _pack([_grad_small(nm, gr[nm]) for nm in SMALL_SHARDED + SMALL_REPL])
        riders = {"out_dgrad": ([gw_in], False), "out_wgrad": ([small], True)}
    grad_x, got = _ffn_bwd("ffn1", gr, dz1, x2d, w["ffn1_w_in"], w["ffn1_w_out"], hg1, hu1, act1, riders)
    if ex:
        parts.update(w_in=got["out_dgrad"][0], small=got["out_wgrad"][0],
                     ffn1_w_out=got["w_out"], ffn1_w_in=got["w_in"])
    return loss[0, 0], grad_x.reshape(bsz, t, d), gr, parts


def _ffn_bwd(tag, gr, dz, xin, wg, wout, hg, hu, act, riders=None, tk=512):
    n = xin.shape[0]
    nj = N_DEV // 2
    nt = n // tk
    got = {}

    def hosted(key, result, on):
        if not on:
            return result
        got[key] = result[1][0] if key in ("w_out", "w_in") else result[1]
        return result[0]

    riders = riders or {}
    own = bool(riders)
    dhg, dhu = hosted("out_dgrad", _ffn_out_bwd(tag + "_out_dgrad", dz, wout, hg, hu, exch=riders.get("out_dgrad")),
                      "out_dgrad" in riders)
    gr[tag + "_w_out"] = hosted("out_wgrad", _mm_call(
        tag + "_out_wgrad", "tn", (nj, nt), 1, [act, dz],
        [pl.BlockSpec((None, tk, FF_BLK), lambda j, i: (j, i, 0)), pl.BlockSpec((tk, D_MODEL), lambda j, i: (i, 0))],
        jax.ShapeDtypeStruct((D_FF, D_MODEL), BF16), pl.BlockSpec((FF_BLK, D_MODEL), lambda j, i: (j, 0)),
        (FF_BLK, D_MODEL), scale=0.5, exch=riders.get("out_wgrad")), "out_wgrad" in riders)
    dw = []
    for nm, dh in (("gate", dhg), ("up", dhu)):
        send = ([gr[tag + "_w_out"].reshape(N_DEV, D_FF // N_DEV, D_MODEL)], False) if own and nm == "gate" else None
        dw.append(hosted("w_out", _mm_call(
            tag + "_in_wgrad_" + nm, "tn", (nj, nt), 1, [xin, dh],
            [pl.BlockSpec((tk, D_MODEL), lambda j, i: (i, 0)), pl.BlockSpec((None, tk, FF_BLK), lambda j, i: (j, i, 0))],
            jax.ShapeDtypeStruct((nj, D_MODEL, FF_BLK), BF16), pl.BlockSpec((None, D_MODEL, FF_BLK), lambda j, i: (j, 0, 0)),
            (D_MODEL, FF_BLK), exch=send), send is not None))
    gr[tag + "_w_in"] = dw
    dx = dz
    tm = 512
    for off, (nm, dh) in enumerate((("gate", dhg), ("up", dhu))):
        send = ([jnp.concatenate(dw, axis=0)], False) if own and nm == "gate" else None
        dx = hosted("w_in", _mm_call(
            tag + "_in_dgrad_" + nm, "nt", (n // tm, nj), 1, [dh, wg, dx],
            [pl.BlockSpec((None, tm, FF_BLK), lambda i, j: (j, i, 0)),
             pl.BlockSpec((None, D_MODEL, FF_BLK), functools.partial(lambda i, j, o: (j + o, 0, 0), o=off * nj)),
             pl.BlockSpec((tm, D_MODEL), lambda i, j: (i, 0))],
            jax.ShapeDtypeStruct((n, D_MODEL), F32), pl.BlockSpec((tm, D_MODEL), lambda i, j: (i, 0)),
            (tm, D_MODEL), add_scale=ALPHA if off == 0 else 1.0, exch=send), send is not None)
    return dx, got


def _adam_math(g, w, m, v):
    m = ADAM_B1 * m + (1.0 - ADAM_B1) * g
    v = ADAM_B2 * v + (1.0 - ADAM_B2) * (g * g)
    m_hat = m / (1.0 - ADAM_B1 ** ADAM_STEP)
    v_hat = v / (1.0 - ADAM_B2 ** ADAM_STEP)
    delta = -ADAM_LR * (m_hat / (jnp.sqrt(v_hat) + ADAM_EPS) + ADAM_WD * w)
    return delta, m, v


def _adam(name, parts, w, m, v, tr=128):
    rows, cols = w.shape
    tr = min(tr, rows)
    while rows % tr:
        tr -= 8

    def body(p_ref, w_ref, m_ref, v_ref, g_ref, d_ref, mo_ref, vo_ref):
        g = p_ref[0].astype(F32)
        for s in range(1, N_DEV):
            g = g + p_ref[s].astype(F32)
        g_ref[...] = g
        d_ref[...], mo_ref[...], vo_ref[...] = _adam_math(g, w_ref[...], m_ref[...], v_ref[...])

    blk = pl.BlockSpec((tr, cols), lambda i: (i, 0))
    shp = jax.ShapeDtypeStruct((rows, cols), F32)
    return _call(name, body, (rows // tr,), [parts, w, m, v],
                 [pl.BlockSpec((N_DEV, tr, cols), lambda i: (0, i, 0)), blk, blk, blk], [blk] * 4, [shp] * 4)


def _sum8(name, parts):
    _, rows, cols = parts.shape

    def body(p_ref, o_ref):
        g = p_ref[0]
        for s in range(1, N_DEV):
            g = g + p_ref[s]
        o_ref[...] = g

    return pl.pallas_call(body, name=name, out_shape=jax.ShapeDtypeStruct((rows, cols), F32),
                          compiler_params=_params())(parts)


def _adam_small(name, g, w, m, v):
    def body(g_ref, w_ref, m_ref, v_ref, d_ref, mo_ref, vo_ref):
        d_ref[...], mo_ref[...], vo_ref[...] = _adam_math(g_ref[...], w_ref[...], m_ref[...], v_ref[...])

    shp = jax.ShapeDtypeStruct(g.shape, F32)
    return pl.pallas_call(body, name=name, out_shape=[shp] * 3, compiler_params=_params())(g, w, m, v)


def _pack(arrs, lane=128):
    flat = jnp.concatenate([a.reshape(-1).astype(F32) for a in arrs])
    pad = (-flat.shape[0]) % (8 * lane)
    return jnp.pad(flat, (0, pad)).reshape(-1, lane)


def _unpack(packed, shapes):
    flat, out, off = packed.reshape(-1), [], 0
    for s in shapes:
        sz = math.prod(s)
        out.append(flat[off:off + sz].reshape(s))
        off += sz
    return out


def _pad_in_cols(wfull):
    zeros = jnp.zeros((wfull.shape[0], SHIFT_PAD - SHIFT_COLS), wfull.dtype)
    return jnp.concatenate([wfull[:, :SHIFT_COLS], zeros, wfull[:, SHIFT_COLS:]], axis=1)


def _unpad_in_cols(gfull):
    return jnp.concatenate([gfull[:, :SHIFT_COLS], gfull[:, SHIFT_PAD:]], axis=1)


def _block_diag2(wd):
    z = jnp.zeros_like(wd[0])
    return jnp.concatenate([jnp.concatenate([wd[0], z], axis=1), jnp.concatenate([z, wd[1]], axis=1)], axis=0)


def _unblock_diag2(g):
    return jnp.stack([g[0:64, 0:RW], g[64:128, RW:2 * RW]])


SMALL_SHARDED = ("w0", "w2", "a0", "a2", "g2", "conv_dw")
SMALL_REPL = ("mu_prev", "mu_next", "k_k", "k_a", "r_k", "lnx_g", "lnx_b", "conv_b", "conv_ln_g", "conv_ln_b",
              "ln1_g", "ln1_b", "ln2_g", "ln2_b", "ln3_g", "ln3_b")
BIG = ("ffn1_w_in", "ffn1_w_out", "w_in", "w_out", "ffn2_w_in", "ffn2_w_out")
WEIGHTS = ("ffn1_w_in", "ffn1_w_out", "w_in", "mu_prev", "mu_next", "w0", "w2", "a0", "a2", "g2", "k_k", "k_a",
           "r_k", "lnx_g", "lnx_b", "conv_dw", "conv_b", "conv_ln_g", "conv_ln_b", "w_out", "ffn2_w_in",
           "ffn2_w_out", "ln1_g", "ln1_b", "ln2_g", "ln2_b", "ln3_g", "ln3_b")


def _full_small(name, full):
    if name in ("w0", "a0"):
        return full.reshape(1, 2 * RW)
    if name in ("w2", "a2"):
        return _block_diag2(full)
    if name == "g2":
        return jnp.pad(full, ((0, 256 - GATE_LORA), (0, 0)))
    if name == "conv_dw":
        return jnp.pad(full, ((0, 1), (0, 0)))
    if name in ("mu_prev", "mu_next"):
        return jnp.pad(full.reshape(1, SHIFT_COLS), ((0, 0), (0, SHIFT_PAD - SHIFT_COLS)))
    return full.reshape(1, -1)


def _grad_small(name, g):
    if name in ("w0", "a0"):
        return g.reshape(2, RW)
    if name in ("w2", "a2"):
        return _unblock_diag2(g)
    if name == "g2":
        return g[:GATE_LORA]
    if name == "conv_dw":
        return g[:CONV_K]
    if name in ("mu_prev", "mu_next"):
        return g[0, :SHIFT_COLS]
    if name == "r_k":
        return g.reshape(N_HEADS, HEAD)
    return g.reshape(-1)


def kernel(x, ffn1_w_in, ffn1_w_out, w_in, mu_prev, mu_next, w0, w2, a0, a2, g2, k_k, k_a, r_k, lnx_g, lnx_b, conv_dw, conv_b, conv_ln_g, conv_ln_b, w_out, ffn2_w_in, ffn2_w_out, ln1_g, ln1_b, ln2_g, ln2_b, ln3_g, ln3_b, loss_target, m_ffn1_w_in, m_ffn1_w_out, m_w_in, m_mu_prev, m_mu_next, m_w0, m_w2, m_a0, m_a2, m_g2, m_k_k, m_k_a, m_r_k, m_lnx_g, m_lnx_b, m_conv_dw, m_conv_b, m_conv_ln_g, m_conv_ln_b, m_w_out, m_ffn2_w_in, m_ffn2_w_out, m_ln1_g, m_ln1_b, m_ln2_g, m_ln2_b, m_ln3_g, m_ln3_b, v_ffn1_w_in, v_ffn1_w_out, v_w_in, v_mu_prev, v_mu_next, v_w0, v_w2, v_a0, v_a2, v_g2, v_k_k, v_k_a, v_r_k, v_lnx_g, v_lnx_b, v_conv_dw, v_conv_b, v_conv_ln_g, v_conv_ln_b, v_w_out, v_ffn2_w_in, v_ffn2_w_out, v_ln1_g, v_ln1_b, v_ln2_g, v_ln2_b, v_ln3_g, v_ln3_b):
    args = dict(locals())
    wsh = {n: args[n][0] for n in WEIGHTS}
    msh = {n: args["m_" + n][0] for n in WEIGHTS}
    vsh = {n: args["v_" + n][0] for n in WEIGHTS}
    me = 4 * lax.axis_index("x") + 2 * lax.axis_index("y") + lax.axis_index("c")
    bf = {n: wsh[n].astype(BF16) for n in BIG}

    w = {"ffn1_w_in": _exchange("gather_ffn1_w_in", [bf["ffn1_w_in"]], True)[0]}
    for n in SMALL_REPL:
        w[n] = _full_small(n, wsh[n])
    small_shapes = [wsh[n].shape for n in SMALL_SHARDED]

    def finish1(got):
        f1_out, w_in_g, small = got
        cols = zip(*[_unpack(small[dv], small_shapes) for dv in range(N_DEV)])
        out = {n: _full_small(n, jnp.concatenate(s, axis=-1)) for n, s in zip(SMALL_SHARDED, cols)}
        out["ffn1_w_out"] = f1_out.reshape(D_FF, D_MODEL)
        out["w_in"] = _pad_in_cols(w_in_g.transpose(1, 0, 2).reshape(D_MODEL, IN_COLS))
        return out

    def finish2(got):
        w_out_g, f2_in, f2_out = got
        return {"w_out": w_out_g.reshape(D_MODEL, D_MODEL), "ffn2_w_in": f2_in,
                "ffn2_w_out": f2_out.reshape(D_FF, D_MODEL)}

    ex = {"g1": ([bf["ffn1_w_out"], bf["w_in"], _pack([wsh[n] for n in SMALL_SHARDED])], finish1),
          "g2": ([bf["w_out"], bf["ffn2_w_in"], bf["ffn2_w_out"]], finish2)}
    loss_part, grad_x, gr, parts = _local_step(x, loss_target, w, ex)
    loss = lax.psum(loss_part, ("x", "y", "c"))

    out = {n: _adam("adam_" + n, parts[n], wsh[n], msh[n], vsh[n]) for n in BIG}
    small_names = SMALL_SHARDED + SMALL_REPL
    full_shapes = [_grad_small(n, gr[n]).shape for n in small_names]
    summed = _unpack(_sum8("sum_small_grads", parts["small"]), full_shapes)
    mine = []
    for n, g in zip(small_names, summed):
        if n in SMALL_SHARDED:
            g = lax.dynamic_slice_in_dim(g, me * HEAD, HEAD, axis=g.ndim - 1)
        mine.append(g)
    shapes = [g.shape for g in mine]
    d_s, m_s, v_s = _adam_small("adam_small", _pack(mine), _pack([wsh[n] for n in small_names]),
                                _pack([msh[n] for n in small_names]), _pack([vsh[n] for n in small_names]))
    for n, g, dl, mn, vn in zip(small_names, mine, _unpack(d_s, shapes), _unpack(m_s, shapes), _unpack(v_s, shapes)):
        out[n] = (g, dl, mn, vn)

    res = [loss, grad_x]
    for k in range(4):
        res += [out[n][k][None] for n in WEIGHTS]
    return tuple(res)
```

```python
import functools
import math

import jax
import jax.numpy as jnp
from jax import lax
from jax.experimental import pallas as pl
from jax.experimental.pallas import tpu as pltpu

F32 = jnp.float32
BF16 = jnp.bfloat16

N_DEV = 8
D_MODEL = 1024
RW = 512
N_HEADS = 8
HEAD = 64
CW = 512
CONV_K = 31
D_FF = 2816
FF_BLK = 704
GATE_LORA = 160
SHIFT_COLS = 1952
SHIFT_PAD = 2048
IN_COLS = 2976
IN_PAD = 3072
LN_EPS = 1e-5
GN_EPS = 64e-5
NORM_EPS = 1e-12
ALPHA = 2.0 ** 0.25
DECAY_SCALE = math.exp(-0.5)
CHUNK = 64
ADAM_LR, ADAM_B1, ADAM_B2, ADAM_EPS, ADAM_WD, ADAM_STEP = 0.001, 0.9, 0.999, 1e-8, 0.01, 10
VMEM_LIMIT = 56 * 1024 * 1024

_DN = {"nn": (((1,), (0,)), ((), ())), "nt": (((1,), (1,)), ((), ())), "tn": (((0,), (0,)), ((), ()))}


def _params():
    return pltpu.CompilerParams(vmem_limit_bytes=VMEM_LIMIT)


def _dot(a, b, dims="nn"):
    return lax.dot_general(a, b, _DN[dims], preferred_element_type=F32)


def _split(x):
    hi = x.astype(BF16)
    return hi, (x - hi.astype(F32)).astype(BF16)


def _dot3_impl(a, b, dims):
    ah, al = _split(a)
    bh, bl = _split(b)
    ka, kb = _DN[dims][0][0][0], _DN[dims][0][1][0]
    return _dot(jnp.concatenate([ah, ah, al], axis=ka), jnp.concatenate([bh, bl, bh], axis=kb), dims)


@functools.partial(jax.custom_vjp, nondiff_argnums=(2,))
def _dot3(a, b, dims="nn"):
    return _dot3_impl(a, b, dims)


def _dot3_fwd(a, b, dims):
    return _dot3_impl(a, b, dims), (a, b)


def _dot3_bwd(dims, res, ct):
    a, b = res
    if dims == "nn":
        return _dot3_impl(ct, b, "nt"), _dot3_impl(a, ct, "tn")
    if dims == "nt":
        return _dot3_impl(ct, b, "nn"), _dot3_impl(ct, a, "tn")
    return _dot3_impl(b, ct, "nt"), _dot3_impl(a, ct, "nn")


_dot3.defvjp(_dot3_fwd, _dot3_bwd)


def _dot1_impl(a, b, dims):
    return _dot(a.astype(BF16), b.astype(BF16), dims)


@functools.partial(jax.custom_vjp, nondiff_argnums=(2,))
def _dot1(a, b, dims="nn"):
    return _dot1_impl(a, b, dims)


def _dot1_bwd(dims, res, ct):
    a, b = res
    if dims == "nn":
        return _dot1_impl(ct, b, "nt"), _dot1_impl(a, ct, "tn")
    if dims == "nt":
        return _dot1_impl(ct, b, "nn"), _dot1_impl(ct, a, "tn")
    return _dot1_impl(b, ct, "nt"), _dot1_impl(a, ct, "nn")


_dot1.defvjp(lambda a, b, dims: (_dot1_impl(a, b, dims), (a, b)), _dot1_bwd)


def _dot1_two(a, b, dims="nn"):
    ax_a, ax_b = {"nn": (0, 1), "nt": (0, 0), "tn": (1, 1)}[dims]
    out = []
    for i in range(0, len(a), 2):
        r = _dot1(jnp.concatenate(a[i:i + 2], axis=ax_a), jnp.concatenate(b[i:i + 2], axis=ax_b), dims)
        m, n = r.shape[0] // 2, r.shape[1] // 2
        out += [r[:m, :n], r[m:, n:]]
    return out


def _tri_inv_impl(l, eye):
    steps = int(math.log2(CHUNK)) - 1
    m = l[0].shape[0]
    tm = [eye + x for x in l]
    lp = _dot1_two(l, l)
    for k in range(steps):
        if k < steps - 1:
            both = _dot1_two([jnp.concatenate([t, p], axis=0) for t, p in zip(tm, lp)], lp)
            tm = [t + b[:m] for t, b in zip(tm, both)]
            lp = [b[m:] for b in both]
        else:
            tm = [t + x for t, x in zip(tm, _dot1_two(tm, lp))]
    return tm


@jax.custom_vjp
def _tri_inv(l, eye):
    return _tri_inv_impl(l, eye)


def _tri_inv_fwd(l, eye):
    tm = _tri_inv_impl(l, eye)
    return tm, (tm, eye)


def _tri_inv_bwd(res, ct):
    tm, eye = res
    return _dot1_two(_dot1_two(tm, ct, "tn"), tm, "nt"), jnp.zeros_like(eye)


_tri_inv.defvjp(_tri_inv_fwd, _tri_inv_bwd)


def _ones_impl(x, g3):
    x1 = x.astype(BF16)
    r1 = x - x1.astype(F32)
    x2 = r1.astype(BF16)
    x3 = (r1 - x2.astype(F32)).astype(BF16)
    return _dot(jnp.concatenate([x1, x2, x3], axis=1), g3)


@jax.custom_vjp
def _head_sum(x, g3):
    return _ones_impl(x, g3)


_head_sum.defvjp(lambda x, g3: (_ones_impl(x, g3), g3), lambda g3, ct: (_ones_impl(ct, g3), jnp.zeros_like(g3)))


def _prefix_sum(x):
    row = lax.broadcasted_iota(jnp.int32, x.shape, 0)
    sh = 1
    while sh < x.shape[0]:
        x = x + jnp.where(row >= sh, pltpu.roll(x, sh, 0), 0.0)
        sh *= 2
    return x


def _dir_cumsum_impl(x, sgn):
    pre = _prefix_sum(x)
    return jnp.where(sgn > 0.0, pre, jnp.sum(x, axis=0, keepdims=True) - pre + x)


@jax.custom_vjp
def _dir_cumsum(x, sgn):
    return _dir_cumsum_impl(x, sgn)


_dir_cumsum.defvjp(lambda x, sgn: (_dir_cumsum_impl(x, sgn), sgn),
                   lambda sgn, ct: (_dir_cumsum_impl(ct, -sgn), jnp.zeros_like(sgn)))


def _sigmoid(x):
    return 1.0 / (1.0 + jnp.exp(-x))


def _mesh_pos():
    return lax.axis_index("x"), lax.axis_index("y"), lax.axis_index("c")


def _peer(pos, q):
    x, y, c = pos
    return (1 - x if q & 4 else x, 1 - y if q & 2 else y, 1 - c if q & 1 else c)


def _linear(pos):
    return 4 * pos[0] + 2 * pos[1] + pos[2]


def _exchange_copies(x_refs, o_refs, send_sems, recv_sems, local_sems, gather):
    pos = _mesh_pos()
    me = _linear(pos)
    starts, wait_recv, wait_send, wait_local = [], [], [], []
    for t in range(len(x_refs)):
        src = x_refs[t] if gather else x_refs[t].at[me]
        cp = pltpu.make_async_copy(src, o_refs[t].at[me], local_sems.at[t])
        starts.append(cp.start)
        wait_local.append(cp.wait)
    for q in range(1, N_DEV):
        peer = _peer(pos, q)
        for t in range(len(x_refs)):
            src = x_refs[t] if gather else x_refs[t].at[_linear(peer)]
            sems = dict(send_sem=send_sems.at[t, q - 1], recv_sem=recv_sems.at[t, q - 1],
                        device_id=peer, device_id_type=pl.DeviceIdType.MESH)
            send = pltpu.make_async_remote_copy(src_ref=src, dst_ref=o_refs[t].at[me], **sems)
            recv = pltpu.make_async_remote_copy(src_ref=src, dst_ref=o_refs[t].at[_linear(peer)], **sems)
            starts.append(send.start)
            wait_recv.append(recv.wait_recv)
            wait_send.append(send.wait_send)
    return starts, wait_recv + wait_send + wait_local


def _exchange_shapes(xs, gather):
    return [jax.ShapeDtypeStruct((N_DEV,) + (x.shape if gather else x.shape[1:]), x.dtype) for x in xs]


def _exchange_sems(nt):
    return [pltpu.SemaphoreType.DMA((nt, N_DEV - 1)), pltpu.SemaphoreType.DMA((nt, N_DEV - 1)),
            pltpu.SemaphoreType.DMA((nt,))]


def _gather_two_level(name, x):
    def body(x_ref, out_ref, send_sems, recv_sems, local_sem):
        px, py, pc = _mesh_pos()
        me, sibling = (px, py, pc), (px, py, 1 - pc)
        chips = [(1 - px, py), (px, 1 - py), (1 - px, 1 - py)]

        def slot(pos):
            return out_ref.at[_linear(pos)]

        def copy(k, block, to, src=None):
            return pltpu.make_async_remote_copy(
                src_ref=slot(block) if src is None else src, dst_ref=slot(block), send_sem=send_sems.at[k],
                recv_sem=recv_sems.at[k], device_id=to, device_id_type=pl.DeviceIdType.MESH)

        mine = pltpu.make_async_copy(x_ref, slot(me), local_sem)
        mine.start()
        first = [copy(0, me, sibling, src=x_ref)]
        first += [copy(1 + j, me, (*chip, pc), src=x_ref) for j, chip in enumerate(chips)]
        for cp in first:
            cp.start()
        passed = [copy(4 + j, (*chip, pc), sibling) for j, chip in enumerate(chips)]
        for j, chip in enumerate(chips):
            copy(1 + j, (*chip, pc), me).wait_recv()
            passed[j].start()
        copy(0, sibling, me).wait_recv()
        for j, chip in enumerate(chips):
            copy(4 + j, (*chip, 1 - pc), me).wait_recv()
        for cp in first + passed:
            cp.wait_send()
        mine.wait()

    any_spec = pl.BlockSpec(memory_space=pl.ANY)
    return pl.pallas_call(
        body, name=name, in_specs=[any_spec], out_specs=any_spec,
        out_shape=jax.ShapeDtypeStruct((N_DEV,) + x.shape, x.dtype),
        scratch_shapes=[pltpu.SemaphoreType.DMA((N_DEV - 1,)), pltpu.SemaphoreType.DMA((N_DEV - 1,)),
                        pltpu.SemaphoreType.DMA])(x)


def _call(name, body, grid, ins, in_specs, out_specs, out_shape, scratch=(), exch=None):
    if exch is None:
        return pl.pallas_call(body, name=name, grid=grid, in_specs=in_specs, out_specs=out_specs,
                              out_shape=out_shape, scratch_shapes=list(scratch), compiler_params=_params())(*ins)
    xs, gather = exch
    single = not isinstance(out_shape, (list, tuple))
    o_specs = [out_specs] if single else list(out_specs)
    o_shape = [out_shape] if single else list(out_shape)
    n_in, n_out, n_x, n_scr = len(ins), len(o_shape), len(xs), len(scratch)

    def wrapped(*refs):
        in_refs = refs[:n_in]
        x_refs = refs[n_in:n_in + n_x]
        out_refs = refs[n_in + n_x:n_in + n_x + n_out]
        got_refs = refs[n_in + n_x + n_out:n_in + 2 * n_x + n_out]
        rest = refs[n_in + 2 * n_x + n_out:]
        starts, waits = _exchange_copies(x_refs, got_refs, *rest[n_scr:], gather)
        ids = [pl.program_id(i) for i in range(len(grid))]
        first = functools.reduce(lambda p, q: p & q, [i == 0 for i in ids])
        last = functools.reduce(lambda p, q: p & q, [i == g - 1 for i, g in zip(ids, grid)])

        @pl.when(first)
        def _():
            for f in starts:
                f()

        body(*in_refs, *out_refs, *rest[:n_scr])

        @pl.when(last)
        def _():
            for f in waits:
                f()

    any_spec = pl.BlockSpec(memory_space=pl.ANY)
    outs = pl.pallas_call(
        wrapped, name=name, grid=grid, in_specs=list(in_specs) + [any_spec] * n_x,
        out_specs=o_specs + [any_spec] * n_x, out_shape=o_shape + _exchange_shapes(xs, gather),
        scratch_shapes=list(scratch) + _exchange_sems(n_x), compiler_params=_params())(*ins, *xs)
    res = outs[:n_out]
    return (res[0] if single else res), outs[n_out:]


def _mm_call(name, dims, grid, red_axis, ins, in_specs, out_shape, out_spec, acc_shape,
             scale=1.0, add_scale=None, exch=None):
    nred = grid[red_axis]

    def body(*refs):
        if add_scale is None:
            a_ref, b_ref, o_ref, acc_ref = refs
            add_ref = None
        else:
            a_ref, b_ref, add_ref, o_ref, acc_ref = refs
        k = pl.program_id(red_axis)

        @pl.when(k == 0)
        def _():
            acc_ref[...] = jnp.zeros_like(acc_ref)

        acc_ref[...] += _dot(a_ref[...].astype(BF16), b_ref[...].astype(BF16), dims)

        @pl.when(k == nred - 1)
        def _():
            r = acc_ref[...]
            if scale != 1.0:
                r = r * scale
            if add_ref is not None:
                r = r + add_scale * add_ref[...].astype(F32)
            o_ref[...] = r.astype(o_ref.dtype)

    return _call(name, body, grid, ins, in_specs, out_spec, out_shape, [pltpu.VMEM(acc_shape, F32)], exch)


def _tile(dim, cap):
    t = min(dim, cap)
    while dim % t or t % 128:
        t -= 128
        assert t > 0, (dim, cap)
    return t


def _mm(name, a, b, dims, out_dtype, scale=1.0, add=None, add_scale=None, tm=512, tn=1024, tk=1024):
    if dims == "tn":
        kd, m = a.shape
        n = b.shape[1]
    else:
        m, kd = a.shape
        n = b.shape[1] if dims == "nn" else b.shape[0]
    tm, tn, tk = _tile(m, tm), _tile(n, tn), _tile(kd, tk)
    a_spec = (pl.BlockSpec((tk, tm), lambda i, j, k: (k, i)) if dims == "tn"
              else pl.BlockSpec((tm, tk), lambda i, j, k: (i, k)))
    b_spec = (pl.BlockSpec((tn, tk), lambda i, j, k: (j, k)) if dims == "nt"
              else pl.BlockSpec((tk, tn), lambda i, j, k: (k, j)))
    o_spec = pl.BlockSpec((tm, tn), lambda i, j, k: (i, j))
    ins, specs = [a, b], [a_spec, b_spec]
    if add is not None:
        ins.append(add)
        specs.append(o_spec)
    return _mm_call(name, dims, (m // tm, n // tn, kd // tk), 2, ins, specs,
                    jax.ShapeDtypeStruct((m, n), out_dtype), o_spec, (tm, tn),
                    scale=scale, add_scale=add_scale if add is not None else None)


def _ffn_in(name, x, wg, tm=512, exch=None):
    n = x.shape[0]
    nj = N_DEV // 2

    def body(x_ref, wgate_ref, wup_ref, hg_ref, hu_ref, act_ref):
        xb = x_ref[...].astype(BF16)
        g = _dot(xb, wgate_ref[...])
        u = _dot(xb, wup_ref[...])
        hg_ref[...] = g.astype(BF16)
        hu_ref[...] = u.astype(BF16)
        act_ref[...] = (g * _sigmoid(g) * u).astype(BF16)

    blk = pl.BlockSpec((None, tm, FF_BLK), lambda j, i: (j, i, 0))
    shp = jax.ShapeDtypeStruct((nj, n, FF_BLK), BF16)
    return _call(name, body, (nj, n // tm), [x, wg, wg],
                 [pl.BlockSpec((tm, D_MODEL), lambda j, i: (i, 0)),
                  pl.BlockSpec((None, D_MODEL, FF_BLK), lambda j, i: (j, 0, 0)),
                  pl.BlockSpec((None, D_MODEL, FF_BLK), lambda j, i: (j + nj, 0, 0))],
                 [blk, blk, blk], [shp, shp, shp], exch=exch)


def _ffn_out_bwd(name, dz, wout, hg, hu, act, tm=512, exch=None):
    n = dz.shape[0]
    nj, ni = N_DEV // 2, n // tm

    def body(dz_ref, w_ref, hg_ref, hu_ref, act_ref, dhg_ref, dhu_ref, dw_ref, acc_ref):
        i = pl.program_id(1)
        dzb = dz_ref[...].astype(BF16)
        dact = 0.5 * _dot(dzb, w_ref[...], "nt")
        g = hg_ref[...].astype(F32)
        u = hu_ref[...].astype(F32)
        s = _sigmoid(g)
        dhg_ref[...] = (dact * u * (s * (1.0 + g * (1.0 - s)))).astype(BF16)
        dhu_ref[...] = (dact * (g * s)).astype(BF16)

        @pl.when(i == 0)
        def _():
            acc_ref[...] = jnp.zeros_like(acc_ref)

        acc_ref[...] += _dot(act_ref[...], dzb, "tn")

        @pl.when(i == ni - 1)
        def _():
            dw_ref[...] = (0.5 * acc_ref[...]).astype(dw_ref.dtype)

    blk = pl.BlockSpec((None, tm, FF_BLK), lambda j, i: (j, i, 0))
    wblk = pl.BlockSpec((FF_BLK, D_MODEL), lambda j, i: (j, 0))
    shp = jax.ShapeDtypeStruct((nj, n, FF_BLK), BF16)
    return _call(name, body, (nj, ni), [dz, wout, hg, hu, act],
                 [pl.BlockSpec((tm, D_MODEL), lambda j, i: (i, 0)), wblk, blk, blk, blk],
                 [blk, blk, wblk], [shp, shp, jax.ShapeDtypeStruct((D_FF, D_MODEL), BF16)],
                 [pltpu.VMEM((FF_BLK, D_MODEL), F32)], exch=exch)


def _mm_ln(name, a, b, xres, g, beta, c, tgt=None, a_blocked=False, tm=512, tk=512):
    if a_blocked:
        nk, n, kb = a.shape
        a_spec = pl.BlockSpec((None, tm, kb), lambda i, k: (k, i, 0))
    else:
        n, kd = a.shape
        kb = _tile(kd, tk)
        nk = kd // kb
        a_spec = pl.BlockSpec((tm, kb), lambda i, k: (i, k))
    d = b.shape[1]
    with_loss = tgt is not None

    def body(*refs):
        if with_loss:
            a_ref, b_ref, x_ref, g_ref, be_ref, t_ref, o_ref, z_ref, l_ref, acc_ref = refs
        else:
            a_ref, b_ref, x_ref, g_ref, be_ref, o_ref, z_ref, acc_ref = refs
        i, k = pl.program_id(0), pl.program_id(1)

        @pl.when(k == 0)
        def _():
            acc_ref[...] = jnp.zeros_like(acc_ref)

        acc_ref[...] += _dot(a_ref[...].astype(BF16), b_ref[...].astype(BF16))

        @pl.when(k == nk - 1)
        def _():
            z = ALPHA * x_ref[...] + c * acc_ref[...]
            z_ref[...] = z
            mu = jnp.mean(z, axis=-1, keepdims=True)
            zc = z - mu
            var = jnp.mean(zc * zc, axis=-1, keepdims=True)
            y = zc * lax.rsqrt(var + LN_EPS) * g_ref[...] + be_ref[...]
            if with_loss:
                err = y - t_ref[...]
                o_ref[...] = err * (1.0 / d)
                part = 0.5 * jnp.sum(jnp.sum(err * err, axis=-1, keepdims=True), axis=0, keepdims=True) * (1.0 / d)

                @pl.when(i == 0)
                def _():
                    l_ref[...] = jnp.zeros_like(l_ref)

                l_ref[...] += jnp.broadcast_to(part, l_ref.shape)
            else:
                o_ref[...] = y

    row = pl.BlockSpec((tm, d), lambda i, k: (i, 0))
    vec = pl.BlockSpec((1, d), lambda i, k: (0, 0))
    ins = [a, b, xres, g, beta]
    in_specs = [a_spec, pl.BlockSpec((kb, d), lambda i, k: (k, 0)), row, vec, vec]
    out_specs = [row, row]
    out_shape = [jax.ShapeDtypeStruct((n, d), F32), jax.ShapeDtypeStruct((n, d), F32)]
    if with_loss:
        ins.append(tgt)
        in_specs.append(row)
        out_specs.append(pl.BlockSpec((1, 128), lambda i, k: (0, 0)))
        out_shape.append(jax.ShapeDtypeStruct((1, 128), F32))
    return _call(name, body, (n // tm, nk), ins, in_specs, out_specs, out_shape, [pltpu.VMEM((tm, d), F32)])


def _rowwise(name, fn, rows, params, out_rows, out_accs, tm=256, exch=None):
    specs, ins = [], []
    for r in rows:
        arr, w, cb = r if isinstance(r, tuple) else (r, r.shape[1], 0)
        ins.append(arr)
        specs.append(pl.BlockSpec((tm, w), functools.partial(lambda i, cb: (i, cb), cb=cb)))
    n = ins[0].shape[0]
    for p in params:
        ins.append(p)
        specs.append(pl.BlockSpec(p.shape, lambda i: (0, 0)))
    n_in, n_or = len(ins), len(out_rows)

    def body(*refs):
        outs = fn(*[r[...] for r in refs[:n_in]])
        o_refs = refs[n_in:]
        for o_ref, o in zip(o_refs[:n_or], outs[:n_or]):
            o_ref[...] = o.astype(o_ref.dtype)
        if out_accs:
            @pl.when(pl.program_id(0) == 0)
            def _():
                for a_ref in o_refs[n_or:]:
                    a_ref[...] = jnp.zeros_like(a_ref)

            for a_ref, a in zip(o_refs[n_or:], outs[n_or:]):
                a_ref[...] += a.astype(F32)

    out_specs = [pl.BlockSpec((tm, w), lambda i: (i, 0)) for w, _ in out_rows]
    out_specs += [pl.BlockSpec(s, lambda i: (0, 0)) for s in out_accs]
    out_shape = [jax.ShapeDtypeStruct((n, w), dt) for w, dt in out_rows]
    out_shape += [jax.ShapeDtypeStruct(s, F32) for s in out_accs]
    return _call(name, body, (n // tm,), ins, specs, out_specs, out_shape, exch=exch)


def _vjp_of(fn, n_in):
    def g(*args):
        ins, cts = args[:n_in], args[n_in:]
        outs, pull = jax.vjp(fn, *ins)
        return pull(tuple(c.astype(o.dtype) for c, o in zip(cts, outs)))
    return g


def _ln_bwd(name, z, g, ct):
    def fn(zt, ct_, gt):
        mu = jnp.mean(zt, axis=-1, keepdims=True)
        zc = zt - mu
        rstd = lax.rsqrt(jnp.mean(zc * zc, axis=-1, keepdims=True) + LN_EPS)
        xh = zc * rstd
        dxh = ct_ * gt
        dz = rstd * (dxh - jnp.mean(dxh, axis=-1, keepdims=True)
                     - xh * jnp.mean(dxh * xh, axis=-1, keepdims=True))
        return dz, jnp.sum(ct_ * xh, axis=0, keepdims=True), jnp.sum(ct_, axis=0, keepdims=True)

    d = z.shape[1]
    return _rowwise(name, fn, [z, ct], [g], [(d, F32)], [(1, d), (1, d)])


def _shift(name, src, mu_a, mu_b, q=None, tt=256):
    bsz, t, _ = src.shape
    nt, r8, w = t // tt, tt // 8, SHIFT_PAD
    with_q = q is not None

    def body(cur_ref, prev_ref, next_ref, *rest):
        b, i = pl.program_id(0), pl.program_id(1)
        cur = cur_ref[...]
        prow = jnp.where(i > 0, prev_ref[7:8, :], 0.0)
        nrow = jnp.where(i < nt - 1, next_ref[0:1, :], 0.0)
        rid = lax.broadcasted_iota(jnp.int32, cur.shape, 0)
        dprev = jnp.where(rid == 0, prow, pltpu.roll(cur, 1, 0)) - cur
        dnext = jnp.where(rid == tt - 1, nrow, pltpu.roll(cur, tt - 1, 0)) - cur
        if with_q:
            q_ref, da_ref, db_ref = rest

            @pl.when((b == 0) & (i == 0))
            def _():
                da_ref[...] = jnp.zeros_like(da_ref)
                db_ref[...] = jnp.zeros_like(db_ref)

            qv = q_ref[...]
            da_ref[...] += jnp.sum(qv * dprev, axis=0, keepdims=True)
            db_ref[...] += jnp.sum(qv * dnext, axis=0, keepdims=True)
        else:
            ma_ref, mb_ref, o_ref = rest
            o_ref[...] = cur + ma_ref[...] * dprev + mb_ref[...] * dnext

    cur_spec = pl.BlockSpec((None, tt, w), lambda b, i: (b, i, 0))
    in_specs = [cur_spec,
                pl.BlockSpec((None, 8, w), lambda b, i: (b, jnp.maximum(i * r8 - 1, 0), 0)),
                pl.BlockSpec((None, 8, w), lambda b, i: (b, jnp.minimum((i + 1) * r8, t // 8 - 1), 0))]
    vec = pl.BlockSpec((1, w), lambda b, i: (0, 0))
    if with_q:
        return _call(name, body, (bsz, nt), [src, src, src, q], in_specs + [cur_spec], [vec, vec],
                     [jax.ShapeDtypeStruct((1, w), F32)] * 2)
    return _call(name, body, (bsz, nt), [src, src, src, mu_a, mu_b], in_specs + [vec, vec], cur_spec,
                 jax.ShapeDtypeStruct((bsz, t, w), F32))


CONV_BLK = 128


def _halo_specs(t, tt, w):
    r16 = tt // 16
    return [pl.BlockSpec((None, tt, w), lambda b, i: (b, i, 0)),
            pl.BlockSpec((None, 16, w), lambda b, i: (b, jnp.maximum(i * r16 - 1, 0), 0)),
            pl.BlockSpec((None, 16, w), lambda b, i: (b, jnp.minimum((i + 1) * r16, t // 16 - 1), 0))]


def _fill_pad(pad_ref, cur_ref, prev_ref, next_ref, i, nt, tt):
    pad_ref[0:16, :] = jnp.where(i > 0, prev_ref[...], 0.0)
    pad_ref[16:16 + tt, :] = cur_ref[...]
    pad_ref[16 + tt:32 + tt, :] = jnp.where(i < nt - 1, next_ref[...], 0.0)


def _dwconv(name, u, dw32, bias, flip, tt=512):
    bsz, t, w = u.shape
    tt = min(tt, t)
    nt = t // tt

    def body(cur_ref, prev_ref, next_ref, dw_ref, b_ref, o_ref, pad_ref):
        i = pl.program_id(1)
        _fill_pad(pad_ref, cur_ref, prev_ref, next_ref, i, nt, tt)
        for r0 in range(0, tt, CONV_BLK):
            for cs in (slice(c0, c0 + CONV_BLK) for c0 in range(0, w, CONV_BLK)):
                acc = jnp.broadcast_to(b_ref[:, cs], (CONV_BLK, CONV_BLK))
                for k in range(CONV_K):
                    kk = CONV_K - 1 - k if flip else k
                    acc = acc + pad_ref[pl.ds(r0 + 1 + k, CONV_BLK), cs] * dw_ref[kk:kk + 1, cs]
                o_ref[r0:r0 + CONV_BLK, cs] = acc

    return _call(name, body, (bsz, nt), [u, u, u, dw32, bias],
                 _halo_specs(t, tt, w) + [pl.BlockSpec((32, w), lambda b, i: (0, 0)),
                                          pl.BlockSpec((1, w), lambda b, i: (0, 0))],
                 pl.BlockSpec((None, tt, w), lambda b, i: (b, i, 0)), jax.ShapeDtypeStruct((bsz, t, w), F32),
                 [pltpu.VMEM((tt + 32, w), F32)])


def _dwconv_dw(name, u, dc, tt=512):
    bsz, t, w = u.shape
    tt = min(tt, t)
    nt = t // tt

    def body(cur_ref, prev_ref, next_ref, dc_ref, ddw_ref, db_ref, pad_ref):
        b, i = pl.program_id(0), pl.program_id(1)
        _fill_pad(pad_ref, cur_ref, prev_ref, next_ref, i, nt, tt)

        @pl.when((b == 0) & (i == 0))
        def _():
            ddw_ref[...] = jnp.zeros_like(ddw_ref)
            db_ref[...] = jnp.zeros_like(db_ref)

        dcv = dc_ref[...]
        db_ref[...] += jnp.sum(dcv, axis=0, keepdims=True)
        for k in range(CONV_K):
            ddw_ref[k:k + 1, :] += jnp.sum(dcv * pad_ref[pl.ds(1 + k, tt), :], axis=0, keepdims=True)

    return _call(name, body, (bsz, nt), [u, u, u, dc],
                 _halo_specs(t, tt, w) + [pl.BlockSpec((None, tt, w), lambda b, i: (b, i, 0))],
                 [pl.BlockSpec((32, w), lambda b, i: (0, 0)), pl.BlockSpec((1, w), lambda b, i: (0, 0))],
                 [jax.ShapeDtypeStruct((32, w), F32), jax.ShapeDtypeStruct((1, w), F32)],
                 [pltpu.VMEM((tt + 32, w), F32)])


PAIR = 2 * HEAD
N_PAIRS = RW // PAIR


def _chunk_pairs(s, r, lw, k, v, kk, a, sgn):
    n, m = CHUNK, 2 * CHUNK
    in_a = lax.broadcasted_iota(jnp.int32, (n, PAIR), 1) < HEAD

    def stack2(z):
        return jnp.concatenate([jnp.where(in_a, z, 0.0), jnp.where(in_a, 0.0, z)], axis=0)

    def each(f, *lists):
        return [f(*z) for z in zip(*lists)]

    sgn_f = sgn.astype(F32)
    row2 = lax.broadcasted_iota(jnp.int32, (m, m), 0)
    col2 = lax.broadcasted_iota(jnp.int32, (m, m), 1)
    same = (row2 >= n) == (col2 >= n)
    dlt = ((row2 & (n - 1)) - (col2 & (n - 1))) * sgn
    incl, strict = same & (dlt >= 0), same & (dlt > 0)
    eye = jnp.where(row2 == col2, 1.0, 0.0)

    cum = each(lambda lw_: _dir_cumsum(lw_, sgn_f), lw)
    tot = each(lambda lw_: jnp.sum(lw_, axis=0, keepdims=True), lw)
    e_neg = each(lambda c_: jnp.exp(-c_), cum)
    e_rest = each(lambda t_, c_: jnp.exp(t_ - c_), tot, cum)
    beta = each(lambda kk_, a_: kk_ * a_, kk, a)
    lhs = each(lambda kk_, c_, lw_, r_: jnp.concatenate(
        [stack2(-kk_ * jnp.exp(c_ - lw_)), stack2(r_ * jnp.exp(c_))], axis=0), kk, cum, lw, r)
    rhs = each(lambda b_, k_, e_: jnp.concatenate([stack2(b_ * e_), stack2(k_ * e_)], axis=0), beta, k, e_neg)
    sc = each(lambda l_, r_: _dot3(l_, r_, "nt"), lhs, rhs)
    l_ab = each(lambda sc_: jnp.where(strict, sc_[0:m, 0:m], 0.0), sc)
    l_ak = each(lambda sc_: jnp.where(strict, sc_[0:m, m:2 * m], 0.0), sc)
    m_r = each(lambda sc_: jnp.where(jnp.concatenate([incl, incl], axis=1), sc_[m:2 * m, :], 0.0), sc)
    tm = _tri_inv(l_ab, eye)
    z = _dot1_two(lhs, s, "nt")
    v2 = each(stack2, v)
    u2 = _dot1_two(tm, each(lambda z_, lv_: z_[0:m] + lv_, z, _dot1_two(l_ak, v2)))
    uv = each(lambda u_, v_: jnp.concatenate([u_, v_], axis=0), u2, v2)
    y2 = each(lambda z_, mu_: z_[m:2 * m] + mu_, z, _dot1_two(m_r, uv))
    bk = each(lambda b_, k_, e_: jnp.concatenate([stack2(b_ * e_), stack2(k_ * e_)], axis=0), beta, k, e_rest)
    s_new = each(lambda s_, t_, d_: s_ * jnp.exp(t_) + d_, s, tot, _dot1_two(uv, bk, "tn"))
    return each(lambda y_: y_[0:n] + y_[n:m], y2), s_new


SCAN_SEQS = 4
N_CHAINS = SCAN_SEQS * N_PAIRS


def _pair_tiles(ref):
    return [ref[q, :, p * PAIR:(p + 1) * PAIR] for q in range(SCAN_SEQS) for p in range(N_PAIRS)]


def _store_tiles(ref, tiles):
    for q in range(SCAN_SEQS):
        for p in range(N_PAIRS):
            ref[q, :, p * PAIR:(p + 1) * PAIR] = tiles[q * N_PAIRS + p]


def _scan_specs(order):
    shared = pl.BlockSpec((SCAN_SEQS, CHUNK, RW), lambda d, b, c: (b, order(d, c), 0))
    per_dir = pl.BlockSpec((SCAN_SEQS, CHUNK, RW), lambda d, b, c: (b, order(d, c), d))
    state = pl.BlockSpec((None, SCAN_SEQS, None, N_PAIRS, PAIR, PAIR), lambda d, b, c: (d, b, order(d, c), 0, 0, 0))
    return shared, per_dir, state


def _scan_fwd(r, v, kk, lw, kd, a, bsz, exch=None):
    n = r.shape[0]
    t = n // bsz
    nc = t // CHUNK

    def order(d, c):
        return c + d * (nc - 1 - 2 * c)

    def body(r_ref, v_ref, kk_ref, lw_ref, kd_ref, a_ref, y_ref, s0_ref, s_ref):
        d, c = pl.program_id(0), pl.program_id(2)

        @pl.when(c == 0)
        def _():
            s_ref[...] = jnp.zeros_like(s_ref)

        s = [s_ref[i] for i in range(N_CHAINS)]
        y, s_new = _chunk_pairs(s, *[_pair_tiles(ref) for ref in (r_ref, lw_ref, kd_ref, v_ref, kk_ref, a_ref)],
                                1 - 2 * d)
        _store_tiles(y_ref, y)
        for i in range(N_CHAINS):
            s0_ref[i // N_PAIRS, i % N_PAIRS] = s[i]
            s_ref[i] = s_new[i]

    shared, per_dir, state = _scan_specs(order)
    seq = lambda z: z.reshape(bsz, t, z.shape[1])
    res = _call("scan_fwd", body, (2, bsz // SCAN_SEQS, nc), [seq(z) for z in (r, v, kk, lw, kd, a)],
                [shared, shared, shared, per_dir, per_dir, per_dir], [per_dir, state],
                [jax.ShapeDtypeStruct((bsz, t, 2 * RW), F32),
                 jax.ShapeDtypeStruct((2, bsz, nc, N_PAIRS, PAIR, PAIR), F32)],
                [pltpu.VMEM((N_CHAINS, PAIR, PAIR), F32)], exch)
    (y, s0), got = res if exch else (res, None)
    y = y.reshape(n, 2 * RW)
    return ([y, s0], got) if exch else [y, s0]


def _scan_bwd(r, v, kk, lw, kd, a, s0, dy, bsz, exch=None):
    n = r.shape[0]
    t = n // bsz
    nc = t // CHUNK

    def order(d, c):
        cc = nc - 1 - c
        return cc + d * (nc - 1 - 2 * cc)

    def body(r_ref, v_ref, kk_ref, lw_ref, kd_ref, a_ref, dy_ref, s0_ref,
             dr_ref, dv_ref, dkk_ref, dlw_ref, dkd_ref, da_ref, ds_ref):
        d, c = pl.program_id(0), pl.program_id(2)

        @pl.when(c == 0)
        def _():
            ds_ref[...] = jnp.zeros_like(ds_ref)

        sgn = 1 - 2 * d
        _, pull = jax.vjp(lambda *ops: _chunk_pairs(*ops, sgn),
                          [s0_ref[i // N_PAIRS, i % N_PAIRS] for i in range(N_CHAINS)],
                          *[_pair_tiles(ref) for ref in (r_ref, lw_ref, kd_ref, v_ref, kk_ref, a_ref)])
        grads = pull((_pair_tiles(dy_ref), [ds_ref[i] for i in range(N_CHAINS)]))
        for i in range(N_CHAINS):
            ds_ref[i] = grads[0][i]
        for o_ref, gx in zip((dr_ref, dlw_ref, dkd_ref, dv_ref, dkk_ref, da_ref), grads[1:]):
            _store_tiles(o_ref, gx)

    shared, per_dir, state = _scan_specs(order)
    shp = jax.ShapeDtypeStruct((bsz, t, 2 * RW), F32)
    seq = lambda z: z.reshape(bsz, t, z.shape[1])
    res = _call("scan_bwd", body, (2, bsz // SCAN_SEQS, nc), [seq(z) for z in (r, v, kk, lw, kd, a, dy)] + [s0],
                [shared, shared, shared, per_dir, per_dir, per_dir, per_dir, state],
                [per_dir] * 6, [shp] * 6, [pltpu.VMEM((N_CHAINS, PAIR, PAIR), F32)], exch)
    outs, got = res if exch else (res, None)
    outs = [z.reshape(n, 2 * RW) for z in outs]
    return (outs, got) if exch else outs


def _prep_fn(ps, w0, w2bd, a0, a2bd, g2p, k_k, k_a, hsum):
    head_sum = lambda z: _head_sum(z, hsum)
    r, k, v = ps[:, 0:RW], ps[:, RW:2 * RW], ps[:, 2 * RW:3 * RW]
    wd, ad, gd = ps[:, 1536:1664], ps[:, 1664:1792], ps[:, 1792:2048]
    logw = -DECAY_SCALE * _sigmoid(_dot1(jnp.tanh(wd), w2bd) + w0)
    a = _sigmoid(_dot1(ad, a2bd) + a0)
    g = _dot1(_sigmoid(gd), g2p)
    kkr = k * k_k
    kk = kkr / jnp.maximum(jnp.sqrt(head_sum(kkr * kkr)), NORM_EPS)
    k2 = jnp.concatenate([k, k], axis=1)
    ka2 = jnp.concatenate([k_a, k_a], axis=1)
    kd = k2 * (1.0 + (a - 1.0) * ka2)
    return r, v, kk, logw, a, kd, g


def _post_fn(y2, r, v, kd, g, lnx_g, lnx_b, r_k, hsum):
    head_sum = lambda z: _head_sum(z, hsum)
    y = y2[:, 0:RW] + y2[:, RW:2 * RW]
    mu = head_sum(y) * (1.0 / HEAD)
    yc = y - mu
    var = head_sum(yc * yc) * (1.0 / HEAD)
    yn = yc * lax.rsqrt(var + GN_EPS) * lnx_g + lnx_b
    bonus = head_sum(r * (kd[:, 0:RW] + kd[:, RW:2 * RW]) * r_k) * v
    return ((yn + bonus) * g,)


def _glu_fn(pa, pb):
    return (pa * _sigmoid(pb),)


def _conv_out_fn(cv, ln_g, ln_b):
    mu = jnp.mean(cv, axis=-1, keepdims=True)
    cc = cv - mu
    var = jnp.mean(cc * cc, axis=-1, keepdims=True)
    y = cc * lax.rsqrt(var + LN_EPS) * ln_g + ln_b
    return (y * _sigmoid(y),)


def _local_step(x, tgt, w, ex=None):
    bsz, t, d = x.shape
    n = bsz * t
    x2d, tgt2d = x.reshape(n, d), tgt.reshape(n, d)
    hsum = jnp.tile(jnp.kron(jnp.eye(N_HEADS, dtype=BF16), jnp.ones((HEAD, HEAD), BF16)), (3, 1))
    w = dict(w)
    parts = {} if ex else None

    def hosted(result, finish=None):
        if not ex:
            return result
        outs, got = result
        if finish is not None:
            w.update(finish(got))
        return outs

    hg1, hu1, act1 = hosted(_ffn_in("ffn1_in", x2d, w["ffn1_w_in"], exch=(ex["g1"][0], True) if ex else None),
                            ex["g1"][1] if ex else None)
    x1, z1 = _mm_ln("ffn1_out_ln1", act1, w["ffn1_w_out"], x2d, w["ln1_g"], w["ln1_b"], 0.5, a_blocked=True)
    p = _mm("w_in_proj", x1, w["w_in"], "nn", F32)
    p3 = p.reshape(bsz, t, IN_PAD)
    ps = _shift("shift_fwd", p3, w["mu_prev"], w["mu_next"]).reshape(n, SHIFT_PAD)
    prep_params = [w["w0"], w["w2"], w["a0"], w["a2"], w["g2"], w["k_k"], w["k_a"], hsum]
    r, v, kk, logw, a, kd, g = _rowwise(
        "rwkv_prep", _prep_fn, [ps], prep_params,
        [(RW, F32), (RW, F32), (RW, F32), (2 * RW, F32), (2 * RW, F32), (2 * RW, F32), (RW, F32)], [])
    y2, s0 = hosted(_scan_fwd(r, v, kk, logw, kd, a, bsz, exch=(ex["g2"][0], True) if ex else None),
                    ex["g2"][1] if ex else None)
    post_params = [w["lnx_g"], w["lnx_b"], w["r_k"], hsum]
    (y_rwkv,) = _rowwise("rwkv_post", _post_fn, [y2, r, v, kd, g], post_params, [(RW, BF16)], [])
    (u,) = _rowwise("conv_glu", _glu_fn, [(p, CW, 4), (p, CW, 5)], [], [(CW, F32)], [])
    cv = _dwconv("conv_dw", u.reshape(bsz, t, CW), w["conv_dw"], w["conv_b"], False).reshape(n, CW)
    (y_conv,) = _rowwise("conv_out", _conv_out_fn, [cv], [w["conv_ln_g"], w["conv_ln_b"]], [(CW, BF16)], [])
    ycat = jnp.concatenate([y_rwkv, y_conv], axis=1)
    x2, z2 = _mm_ln("w_out_ln2", ycat, w["w_out"], x1, w["ln2_g"], w["ln2_b"], 1.0)
    hg2, hu2, act2 = _ffn_in("ffn2_in", x2, w["ffn2_w_in"])
    dx3, z3, loss = _mm_ln("ffn2_out_ln3", act2, w["ffn2_w_out"], x2, w["ln3_g"], w["ln3_b"], 0.5,
                           tgt=tgt2d, a_blocked=True)

    gr = {}
    dz3, gr["ln3_g"], gr["ln3_b"] = _ln_bwd("ln3_bwd", z3, w["ln3_g"], dx3)
    dx2, _ = _ffn_bwd("ffn2", gr, dz3, x2, w["ffn2_w_in"], w["ffn2_w_out"], hg2, hu2, act2)
    dz2, gr["ln2_g"], gr["ln2_b"] = _ln_bwd("ln2_bwd", z2, w["ln2_g"], dx2)
    gr["w_out"] = _mm("w_out_wgrad", ycat, dz2, "tn", BF16, tk=512)
    dycat = _mm("w_out_dgrad", dz2, w["w_out"], "nt", F32)
    conv_out_bwd = _vjp_of(_conv_out_fn, 3)
    dcv, gr["conv_ln_g"], gr["conv_ln_b"] = _rowwise(
        "conv_out_bwd", lambda cv_, ct_, g_, b_: conv_out_bwd(cv_, g_, b_, ct_),
        [cv, (dycat, CW, 1)], [w["conv_ln_g"], w["conv_ln_b"]], [(CW, F32)], [(1, CW), (1, CW)])
    dcv3 = dcv.reshape(bsz, t, CW)
    gr["conv_dw"], gr["conv_b"] = _dwconv_dw("conv_dw_wgrad", u.reshape(bsz, t, CW), dcv3)
    du = _dwconv("conv_dw_dgrad", dcv3, w["conv_dw"], jnp.zeros((1, CW), F32), True).reshape(n, CW)

    def glu_bwd(pa, pb, ct):
        return (jnp.concatenate(_vjp_of(_glu_fn, 2)(pa, pb, ct), axis=1),)

    (dp_conv,) = _rowwise("conv_glu_bwd", glu_bwd, [(p, CW, 4), (p, CW, 5), du], [], [(2 * CW, F32)], [])

    def post_bwd(y2_, r_, v_, kd_, g_, ct, lg, lb, rk, hs):
        return _vjp_of(lambda *z: _post_fn(*z, hs), 8)(y2_, r_, v_, kd_, g_, lg, lb, rk, ct)

    dy2, dr_post, dv_post, dkd_post, dg, gr["lnx_g"], gr["lnx_b"], gr["r_k"] = _rowwise(
        "rwkv_post_bwd", post_bwd, [y2, r, v, kd, g, (dycat, RW, 0)], post_params,
        [(2 * RW, F32), (RW, F32), (RW, F32), (2 * RW, F32), (RW, F32)], [(1, RW), (1, RW), (1, RW)])
    sends = [jnp.concatenate(gr["ffn2_w_in"], axis=0), gr["ffn2_w_out"].reshape(N_DEV, D_FF // N_DEV, D_MODEL),
             gr["w_out"].reshape(N_DEV, D_MODEL // N_DEV, D_MODEL)]
    res = _scan_bwd(r, v, kk, logw, kd, a, s0, dy2, bsz, exch=(sends, False) if ex else None)
    if ex:
        res, got = res
        parts.update(zip(("ffn2_w_in", "ffn2_w_out", "w_out"), got))
    dr_s, dv_s, dkk_s, dlw, dkd_s, da = res

    def prep_bwd(ps_, dr2, dr1, dv2, dv1, dkk2, dlw_, da_, dkd2, dkd1, dg_, *prm):
        half = lambda z: z[:, 0:RW] + z[:, RW:2 * RW]
        return _vjp_of(lambda *z: _prep_fn(*z, prm[-1]), 8)(
            ps_, *prm[:-1], half(dr2) + dr1, half(dv2) + dv1, half(dkk2), dlw_, da_, dkd2 + dkd1, dg_)

    dps, gr["w0"], gr["w2"], gr["a0"], gr["a2"], gr["g2"], gr["k_k"], gr["k_a"] = _rowwise(
        "rwkv_prep_bwd", prep_bwd,
        [ps, dr_s, dr_post, dv_s, dv_post, dkk_s, dlw, da, dkd_s, dkd_post, dg], prep_params,
        [(SHIFT_PAD, F32)], [q.shape for q in prep_params[:-1]])
    dps3 = dps.reshape(bsz, t, SHIFT_PAD)
    gr["mu_prev"], gr["mu_next"] = _shift("shift_dmu", p3, None, None, q=dps3)
    dp_shift = _shift("shift_bwd", dps3, w["mu_next"], w["mu_prev"]).reshape(n, SHIFT_PAD)
    dp = jnp.concatenate([dp_shift, dp_conv], axis=1)
    gr["w_in"] = _mm("w_in_wgrad", x1, dp, "tn", BF16, tk=512)
    dx1 = _mm("w_in_dgrad", dp, w["w_in"], "nt", F32, add=dz2, add_scale=ALPHA)
    dz1, gr["ln1_g"], gr["ln1_b"] = _ln_bwd("ln1_bwd", z1, w["ln1_g"], dx1)
    riders = None
    if ex:
        gw_in = _unpad_in_cols(gr["w_in"]).reshape(D_MODEL, N_DEV, IN_COLS // N_DEV).transpose(1, 0, 2)
        small = _pack([_grad_small(nm, gr[nm]) for nm in SMALL_SHARDED + SMALL_REPL])
        riders = {"out_bwd": ([gw_in], False), "in_wgrad_up": ([small], True)}
    grad_x, got = _ffn_bwd("ffn1", gr, dz1, x2d, w["ffn1_w_in"], w["ffn1_w_out"], hg1, hu1, act1, riders)
    if ex:
        parts.update(w_in=got["out_bwd"][0], small=got["in_wgrad_up"][0],
                     ffn1_w_out=got["w_out"], ffn1_w_in=got["w_in"])
    return loss[0, 0], grad_x.reshape(bsz, t, d), gr, parts


def _ffn_bwd(tag, gr, dz, xin, wg, wout, hg, hu, act, riders=None, tk=512):
    n = xin.shape[0]
    nj = N_DEV // 2
    nt = n // tk
    got = {}

    def hosted(key, result, on):
        if not on:
            return result
        got[key] = result[1][0] if key in ("w_out", "w_in") else result[1]
        return result[0]

    riders = riders or {}
    own = bool(riders)
    dhg, dhu, gr[tag + "_w_out"] = hosted(
        "out_bwd", _ffn_out_bwd(tag + "_out_bwd", dz, wout, hg, hu, act, exch=riders.get("out_bwd")),
        "out_bwd" in riders)
    dw = []
    for nm, dh in (("gate", dhg), ("up", dhu)):
        if own and nm == "gate":
            key, send = "w_out", ([gr[tag + "_w_out"].reshape(N_DEV, D_FF // N_DEV, D_MODEL)], False)
        else:
            key, send = "in_wgrad_up", riders.get("in_wgrad_up") if nm == "up" else None
        dw.append(hosted(key, _mm_call(
            tag + "_in_wgrad_" + nm, "tn", (nj, nt), 1, [xin, dh],
            [pl.BlockSpec((tk, D_MODEL), lambda j, i: (i, 0)), pl.BlockSpec((None, tk, FF_BLK), lambda j, i: (j, i, 0))],
            jax.ShapeDtypeStruct((nj, D_MODEL, FF_BLK), BF16), pl.BlockSpec((None, D_MODEL, FF_BLK), lambda j, i: (j, 0, 0)),
            (D_MODEL, FF_BLK), exch=send), send is not None))
    gr[tag + "_w_in"] = dw
    dx = dz
    tm = 512
    for off, (nm, dh) in enumerate((("gate", dhg), ("up", dhu))):
        send = ([jnp.concatenate(dw, axis=0)], False) if own and nm == "gate" else None
        dx = hosted("w_in", _mm_call(
            tag + "_in_dgrad_" + nm, "nt", (n // tm, nj), 1, [dh, wg, dx],
            [pl.BlockSpec((None, tm, FF_BLK), lambda i, j: (j, i, 0)),
             pl.BlockSpec((None, D_MODEL, FF_BLK), functools.partial(lambda i, j, o: (j + o, 0, 0), o=off * nj)),
             pl.BlockSpec((tm, D_MODEL), lambda i, j: (i, 0))],
            jax.ShapeDtypeStruct((n, D_MODEL), F32), pl.BlockSpec((tm, D_MODEL), lambda i, j: (i, 0)),
            (tm, D_MODEL), add_scale=ALPHA if off == 0 else 1.0, exch=send), send is not None)
    return dx, got


def _adam_math(g, w, m, v):
    m = ADAM_B1 * m + (1.0 - ADAM_B1) * g
    v = ADAM_B2 * v + (1.0 - ADAM_B2) * (g * g)
    m_hat = m / (1.0 - ADAM_B1 ** ADAM_STEP)
    v_hat = v / (1.0 - ADAM_B2 ** ADAM_STEP)
    delta = -ADAM_LR * (m_hat / (jnp.sqrt(v_hat) + ADAM_EPS) + ADAM_WD * w)
    return delta, m, v


def _adam(name, parts, w, m, v, tr=128):
    rows, cols = w.shape
    tr = min(tr, rows)
    while rows % tr:
        tr -= 8

    def body(p_ref, w_ref, m_ref, v_ref, g_ref, d_ref, mo_ref, vo_ref):
        g = p_ref[0].astype(F32)
        for s in range(1, N_DEV):
            g = g + p_ref[s].astype(F32)
        g_ref[...] = g
        d_ref[...], mo_ref[...], vo_ref[...] = _adam_math(g, w_ref[...], m_ref[...], v_ref[...])

    blk = pl.BlockSpec((tr, cols), lambda i: (i, 0))
    shp = jax.ShapeDtypeStruct((rows, cols), F32)
    return _call(name, body, (rows // tr,), [parts, w, m, v],
                 [pl.BlockSpec((N_DEV, tr, cols), lambda i: (0, i, 0)), blk, blk, blk], [blk] * 4, [shp] * 4)


def _sum8(name, parts):
    _, rows, cols = parts.shape

    def body(p_ref, o_ref):
        g = p_ref[0]
        for s in range(1, N_DEV):
            g = g + p_ref[s]
        o_ref[...] = g

    return pl.pallas_call(body, name=name, out_shape=jax.ShapeDtypeStruct((rows, cols), F32),
                          compiler_params=_params())(parts)


def _adam_small(name, g, w, m, v):
    def body(g_ref, w_ref, m_ref, v_ref, d_ref, mo_ref, vo_ref):
        d_ref[...], mo_ref[...], vo_ref[...] = _adam_math(g_ref[...], w_ref[...], m_ref[...], v_ref[...])

    shp = jax.ShapeDtypeStruct(g.shape, F32)
    return pl.pallas_call(body, name=name, out_shape=[shp] * 3, compiler_params=_params())(g, w, m, v)


def _pack(arrs, lane=128):
    flat = jnp.concatenate([a.reshape(-1).astype(F32) for a in arrs])
    pad = (-flat.shape[0]) % (8 * lane)
    return jnp.pad(flat, (0, pad)).reshape(-1, lane)


def _unpack(packed, shapes):
    flat, out, off = packed.reshape(-1), [], 0
    for s in shapes:
        sz = math.prod(s)
        out.append(flat[off:off + sz].reshape(s))
        off += sz
    return out


def _pad_in_cols(wfull):
    zeros = jnp.zeros((wfull.shape[0], SHIFT_PAD - SHIFT_COLS), wfull.dtype)
    return jnp.concatenate([wfull[:, :SHIFT_COLS], zeros, wfull[:, SHIFT_COLS:]], axis=1)


def _unpad_in_cols(gfull):
    return jnp.concatenate([gfull[:, :SHIFT_COLS], gfull[:, SHIFT_PAD:]], axis=1)


def _block_diag2(wd):
    z = jnp.zeros_like(wd[0])
    return jnp.concatenate([jnp.concatenate([wd[0], z], axis=1), jnp.concatenate([z, wd[1]], axis=1)], axis=0)


def _unblock_diag2(g):
    return jnp.stack([g[0:64, 0:RW], g[64:128, RW:2 * RW]])


SMALL_SHARDED = ("w0", "w2", "a0", "a2", "g2", "conv_dw")
SMALL_REPL = ("mu_prev", "mu_next", "k_k", "k_a", "r_k", "lnx_g", "lnx_b", "conv_b", "conv_ln_g", "conv_ln_b",
              "ln1_g", "ln1_b", "ln2_g", "ln2_b", "ln3_g", "ln3_b")
BIG = ("ffn1_w_in", "ffn1_w_out", "w_in", "w_out", "ffn2_w_in", "ffn2_w_out")
WEIGHTS = ("ffn1_w_in", "ffn1_w_out", "w_in", "mu_prev", "mu_next", "w0", "w2", "a0", "a2", "g2", "k_k", "k_a",
           "r_k", "lnx_g", "lnx_b", "conv_dw", "conv_b", "conv_ln_g", "conv_ln_b", "w_out", "ffn2_w_in",
           "ffn2_w_out", "ln1_g", "ln1_b", "ln2_g", "ln2_b", "ln3_g", "ln3_b")


def _full_small(name, full):
    if name in ("w0", "a0"):
        return full.reshape(1, 2 * RW)
    if name in ("w2", "a2"):
        return _block_diag2(full)
    if name == "g2":
        return jnp.pad(full, ((0, 256 - GATE_LORA), (0, 0)))
    if name == "conv_dw":
        return jnp.pad(full, ((0, 1), (0, 0)))
    if name in ("mu_prev", "mu_next"):
        return jnp.pad(full.reshape(1, SHIFT_COLS), ((0, 0), (0, SHIFT_PAD - SHIFT_COLS)))
    return full.reshape(1, -1)


def _grad_small(name, g):
    if name in ("w0", "a0"):
        return g.reshape(2, RW)
    if name in ("w2", "a2"):
        return _unblock_diag2(g)
    if name == "g2":
        return g[:GATE_LORA]
    if name == "conv_dw":
        return g[:CONV_K]
    if name in ("mu_prev", "mu_next"):
        return g[0, :SHIFT_COLS]
    if name == "r_k":
        return g.reshape(N_HEADS, HEAD)
    return g.reshape(-1)


def kernel(x, ffn1_w_in, ffn1_w_out, w_in, mu_prev, mu_next, w0, w2, a0, a2, g2, k_k, k_a, r_k, lnx_g, lnx_b, conv_dw, conv_b, conv_ln_g, conv_ln_b, w_out, ffn2_w_in, ffn2_w_out, ln1_g, ln1_b, ln2_g, ln2_b, ln3_g, ln3_b, loss_target, m_ffn1_w_in, m_ffn1_w_out, m_w_in, m_mu_prev, m_mu_next, m_w0, m_w2, m_a0, m_a2, m_g2, m_k_k, m_k_a, m_r_k, m_lnx_g, m_lnx_b, m_conv_dw, m_conv_b, m_conv_ln_g, m_conv_ln_b, m_w_out, m_ffn2_w_in, m_ffn2_w_out, m_ln1_g, m_ln1_b, m_ln2_g, m_ln2_b, m_ln3_g, m_ln3_b, v_ffn1_w_in, v_ffn1_w_out, v_w_in, v_mu_prev, v_mu_next, v_w0, v_w2, v_a0, v_a2, v_g2, v_k_k, v_k_a, v_r_k, v_lnx_g, v_lnx_b, v_conv_dw, v_conv_b, v_conv_ln_g, v_conv_ln_b, v_w_out, v_ffn2_w_in, v_ffn2_w_out, v_ln1_g, v_ln1_b, v_ln2_g, v_ln2_b, v_ln3_g, v_ln3_b):
    args = dict(locals())
    wsh = {n: args[n][0] for n in WEIGHTS}
    msh = {n: args["m_" + n][0] for n in WEIGHTS}
    vsh = {n: args["v_" + n][0] for n in WEIGHTS}
    me = 4 * lax.axis_index("x") + 2 * lax.axis_index("y") + lax.axis_index("c")
    bf = {n: wsh[n].astype(BF16) for n in BIG}

    w = {"ffn1_w_in": _gather_two_level("gather_ffn1_w_in", bf["ffn1_w_in"])}
    for n in SMALL_REPL:
        w[n] = _full_small(n, wsh[n])
    small_shapes = [wsh[n].shape for n in SMALL_SHARDED]

    def finish1(got):
        f1_out, w_in_g, small = got
        cols = zip(*[_unpack(small[dv], small_shapes) for dv in range(N_DEV)])
        out = {n: _full_small(n, jnp.concatenate(s, axis=-1)) for n, s in zip(SMALL_SHARDED, cols)}
        out["ffn1_w_out"] = f1_out.reshape(D_FF, D_MODEL)
        out["w_in"] = _pad_in_cols(w_in_g.transpose(1, 0, 2).reshape(D_MODEL, IN_COLS))
        return out

    def finish2(got):
        w_out_g, f2_in, f2_out = got
        return {"w_out": w_out_g.reshape(D_MODEL, D_MODEL), "ffn2_w_in": f2_in,
                "ffn2_w_out": f2_out.reshape(D_FF, D_MODEL)}

    ex = {"g1": ([bf["ffn1_w_out"], bf["w_in"], _pack([wsh[n] for n in SMALL_SHARDED])], finish1),
          "g2": ([bf["w_out"], bf["ffn2_w_in"], bf["ffn2_w_out"]], finish2)}
    loss_part, grad_x, gr, parts = _local_step(x, loss_target, w, ex)
    loss = lax.psum(loss_part, ("x", "y", "c"))

    out = {n: _adam("adam_" + n, parts[n], wsh[n], msh[n], vsh[n]) for n in BIG}
    small_names = SMALL_SHARDED + SMALL_REPL
    full_shapes = [_grad_small(n, gr[n]).shape for n in small_names]
    summed = _unpack(_sum8("sum_small_grads", parts["small"]), full_shapes)
    mine = []
    for n, g in zip(small_names, summed):
        if n in SMALL_SHARDED:
            g = lax.dynamic_slice_in_dim(g, me * HEAD, HEAD, axis=g.ndim - 1)
        mine.append(g)
    shapes = [g.shape for g in mine]
    d_s, m_s, v_s = _adam_small("adam_small", _pack(mine), _pack([wsh[n] for n in small_names]),
                                _pack([msh[n] for n in small_names]), _pack([vsh[n] for n in small_names]))
    for n, g, dl, mn, vn in zip(small_names, mine, _unpack(d_s, shapes), _unpack(m_s, shapes), _unpack(v_s, shapes)):
        out[n] = (g, dl, mn, vn)

    res = [loss, grad_x]
    for k in range(4):
        res += [out[n][k][None] for n in WEIGHTS]
    return tuple(res)
```

```python
import functools
import math

import jax
import jax.numpy as jnp
from jax import lax
from jax.experimental import pallas as pl
from jax.experimental.pallas import tpu as pltpu

F32 = jnp.float32
BF16 = jnp.bfloat16

N_DEV = 8
D_MODEL = 1024
RW = 512
N_HEADS = 8
HEAD = 64
CW = 512
CONV_K = 31
D_FF = 2816
FF_BLK = 704
GATE_LORA = 160
SHIFT_COLS = 1952
SHIFT_PAD = 2048
IN_COLS = 2976
IN_PAD = 3072
LN_EPS = 1e-5
GN_EPS = 64e-5
NORM_EPS = 1e-12
ALPHA = 2.0 ** 0.25
DECAY_SCALE = math.exp(-0.5)
CHUNK = 64
ADAM_LR, ADAM_B1, ADAM_B2, ADAM_EPS, ADAM_WD, ADAM_STEP = 0.001, 0.9, 0.999, 1e-8, 0.01, 10
VMEM_LIMIT = 56 * 1024 * 1024

_DN = {"nn": (((1,), (0,)), ((), ())), "nt": (((1,), (1,)), ((), ())), "tn": (((0,), (0,)), ((), ()))}


def _params():
    return pltpu.CompilerParams(vmem_limit_bytes=VMEM_LIMIT)


def _dot(a, b, dims="nn"):
    return lax.dot_general(a, b, _DN[dims], preferred_element_type=F32)


def _split(x):
    hi = x.astype(BF16)
    return hi, (x - hi.astype(F32)).astype(BF16)


def _dot3_impl(a, b, dims):
    ah, al = _split(a)
    bh, bl = _split(b)
    ka, kb = _DN[dims][0][0][0], _DN[dims][0][1][0]
    return _dot(jnp.concatenate([ah, ah, al], axis=ka), jnp.concatenate([bh, bl, bh], axis=kb), dims)


@functools.partial(jax.custom_vjp, nondiff_argnums=(2,))
def _dot3(a, b, dims="nn"):
    return _dot3_impl(a, b, dims)


def _dot3_fwd(a, b, dims):
    return _dot3_impl(a, b, dims), (a, b)


def _dot3_bwd(dims, res, ct):
    a, b = res
    if dims == "nn":
        return _dot3_impl(ct, b, "nt"), _dot3_impl(a, ct, "tn")
    if dims == "nt":
        return _dot3_impl(ct, b, "nn"), _dot3_impl(ct, a, "tn")
    return _dot3_impl(b, ct, "nt"), _dot3_impl(a, ct, "nn")


_dot3.defvjp(_dot3_fwd, _dot3_bwd)


def _dot1_impl(a, b, dims):
    return _dot(a.astype(BF16), b.astype(BF16), dims)


@functools.partial(jax.custom_vjp, nondiff_argnums=(2,))
def _dot1(a, b, dims="nn"):
    return _dot1_impl(a, b, dims)


def _dot1_bwd(dims, res, ct):
    a, b = res
    if dims == "nn":
        return _dot1_impl(ct, b, "nt"), _dot1_impl(a, ct, "tn")
    if dims == "nt":
        return _dot1_impl(ct, b, "nn"), _dot1_impl(ct, a, "tn")
    return _dot1_impl(b, ct, "nt"), _dot1_impl(a, ct, "nn")


_dot1.defvjp(lambda a, b, dims: (_dot1_impl(a, b, dims), (a, b)), _dot1_bwd)


def _dot1_two(a, b, dims="nn"):
    ax_a, ax_b = {"nn": (0, 1), "nt": (0, 0), "tn": (1, 1)}[dims]
    out = []
    for i in range(0, len(a), 2):
        r = _dot1(jnp.concatenate(a[i:i + 2], axis=ax_a), jnp.concatenate(b[i:i + 2], axis=ax_b), dims)
        m, n = r.shape[0] // 2, r.shape[1] // 2
        out += [r[:m, :n], r[m:, n:]]
    return out


def _tri_inv_impl(l, eye):
    steps = int(math.log2(CHUNK)) - 1
    m = l[0].shape[0]
    tm = [eye + x for x in l]
    lp = _dot1_two(l, l)
    for k in range(steps):
        if k < steps - 1:
            both = _dot1_two([jnp.concatenate([t, p], axis=0) for t, p in zip(tm, lp)], lp)
            tm = [t + b[:m] for t, b in zip(tm, both)]
            lp = [b[m:] for b in both]
        else:
            tm = [t + x for t, x in zip(tm, _dot1_two(tm, lp))]
    return tm


@jax.custom_vjp
def _tri_inv(l, eye):
    return _tri_inv_impl(l, eye)


def _tri_inv_fwd(l, eye):
    tm = _tri_inv_impl(l, eye)
    return tm, (tm, eye)


def _tri_inv_bwd(res, ct):
    tm, eye = res
    return _dot1_two(_dot1_two(tm, ct, "tn"), tm, "nt"), jnp.zeros_like(eye)


_tri_inv.defvjp(_tri_inv_fwd, _tri_inv_bwd)


@jax.custom_vjp
def _tri_inv_known(l, tm):
    return tm


_tri_inv_known.defvjp(lambda l, tm: (tm, tm),
                      lambda tm, ct: (_dot1_two(_dot1_two(tm, ct, "tn"), tm, "nt"), [jnp.zeros_like(t) for t in tm]))


def _ones_impl(x, g3):
    x1 = x.astype(BF16)
    r1 = x - x1.astype(F32)
    x2 = r1.astype(BF16)
    x3 = (r1 - x2.astype(F32)).astype(BF16)
    return _dot(jnp.concatenate([x1, x2, x3], axis=1), g3)


@jax.custom_vjp
def _head_sum(x, g3):
    return _ones_impl(x, g3)


_head_sum.defvjp(lambda x, g3: (_ones_impl(x, g3), g3), lambda g3, ct: (_ones_impl(ct, g3), jnp.zeros_like(g3)))


def _prefix_sum(x):
    row = lax.broadcasted_iota(jnp.int32, x.shape, 0)
    sh = 1
    while sh < x.shape[0]:
        x = x + jnp.where(row >= sh, pltpu.roll(x, sh, 0), 0.0)
        sh *= 2
    return x


def _dir_cumsum_impl(x, sgn):
    pre = _prefix_sum(x)
    return jnp.where(sgn > 0.0, pre, jnp.sum(x, axis=0, keepdims=True) - pre + x)


@jax.custom_vjp
def _dir_cumsum(x, sgn):
    return _dir_cumsum_impl(x, sgn)


_dir_cumsum.defvjp(lambda x, sgn: (_dir_cumsum_impl(x, sgn), sgn),
                   lambda sgn, ct: (_dir_cumsum_impl(ct, -sgn), jnp.zeros_like(sgn)))


def _sigmoid(x):
    return 1.0 / (1.0 + jnp.exp(-x))


def _mesh_pos():
    return lax.axis_index("x"), lax.axis_index("y"), lax.axis_index("c")


def _peer(pos, q):
    x, y, c = pos
    return (1 - x if q & 4 else x, 1 - y if q & 2 else y, 1 - c if q & 1 else c)


def _linear(pos):
    return 4 * pos[0] + 2 * pos[1] + pos[2]


def _exchange_copies(x_refs, o_refs, send_sems, recv_sems, local_sems, gather):
    pos = _mesh_pos()
    me = _linear(pos)
    starts, wait_recv, wait_send, wait_local = [], [], [], []
    for t in range(len(x_refs)):
        src = x_refs[t] if gather else x_refs[t].at[me]
        cp = pltpu.make_async_copy(src, o_refs[t].at[me], local_sems.at[t])
        starts.append(cp.start)
        wait_local.append(cp.wait)
    for q in range(1, N_DEV):
        peer = _peer(pos, q)
        for t in range(len(x_refs)):
            src = x_refs[t] if gather else x_refs[t].at[_linear(peer)]
            sems = dict(send_sem=send_sems.at[t, q - 1], recv_sem=recv_sems.at[t, q - 1],
                        device_id=peer, device_id_type=pl.DeviceIdType.MESH)
            send = pltpu.make_async_remote_copy(src_ref=src, dst_ref=o_refs[t].at[me], **sems)
            recv = pltpu.make_async_remote_copy(src_ref=src, dst_ref=o_refs[t].at[_linear(peer)], **sems)
            starts.append(send.start)
            wait_recv.append(recv.wait_recv)
            wait_send.append(send.wait_send)
    return starts, wait_recv + wait_send + wait_local


def _exchange_shapes(xs, gather):
    return [jax.ShapeDtypeStruct((N_DEV,) + (x.shape if gather else x.shape[1:]), x.dtype) for x in xs]


def _exchange_sems(nt):
    return [pltpu.SemaphoreType.DMA((nt, N_DEV - 1)), pltpu.SemaphoreType.DMA((nt, N_DEV - 1)),
            pltpu.SemaphoreType.DMA((nt,))]


def _gather_two_level(name, x):
    def body(x_ref, out_ref, send_sems, recv_sems, local_sem):
        px, py, pc = _mesh_pos()
        me, sibling = (px, py, pc), (px, py, 1 - pc)
        chips = [(1 - px, py), (px, 1 - py), (1 - px, 1 - py)]

        def slot(pos):
            return out_ref.at[_linear(pos)]

        def copy(k, block, to, src=None):
            return pltpu.make_async_remote_copy(
                src_ref=slot(block) if src is None else src, dst_ref=slot(block), send_sem=send_sems.at[k],
                recv_sem=recv_sems.at[k], device_id=to, device_id_type=pl.DeviceIdType.MESH)

        mine = pltpu.make_async_copy(x_ref, slot(me), local_sem)
        mine.start()
        first = [copy(0, me, sibling, src=x_ref)]
        first += [copy(1 + j, me, (*chip, pc), src=x_ref) for j, chip in enumerate(chips)]
        for cp in first:
            cp.start()
        passed = [copy(4 + j, (*chip, pc), sibling) for j, chip in enumerate(chips)]
        for j, chip in enumerate(chips):
            copy(1 + j, (*chip, pc), me).wait_recv()
            passed[j].start()
        copy(0, sibling, me).wait_recv()
        for j, chip in enumerate(chips):
            copy(4 + j, (*chip, 1 - pc), me).wait_recv()
        for cp in first + passed:
            cp.wait_send()
        mine.wait()

    any_spec = pl.BlockSpec(memory_space=pl.ANY)
    return pl.pallas_call(
        body, name=name, in_specs=[any_spec], out_specs=any_spec,
        out_shape=jax.ShapeDtypeStruct((N_DEV,) + x.shape, x.dtype),
        scratch_shapes=[pltpu.SemaphoreType.DMA((N_DEV - 1,)), pltpu.SemaphoreType.DMA((N_DEV - 1,)),
                        pltpu.SemaphoreType.DMA])(x)


def _call(name, body, grid, ins, in_specs, out_specs, out_shape, scratch=(), exch=None):
    if exch is None:
        return pl.pallas_call(body, name=name, grid=grid, in_specs=in_specs, out_specs=out_specs,
                              out_shape=out_shape, scratch_shapes=list(scratch), compiler_params=_params())(*ins)
    xs, gather = exch
    single = not isinstance(out_shape, (list, tuple))
    o_specs = [out_specs] if single else list(out_specs)
    o_shape = [out_shape] if single else list(out_shape)
    n_in, n_out, n_x, n_scr = len(ins), len(o_shape), len(xs), len(scratch)

    def wrapped(*refs):
        in_refs = refs[:n_in]
        x_refs = refs[n_in:n_in + n_x]
        out_refs = refs[n_in + n_x:n_in + n_x + n_out]
        got_refs = refs[n_in + n_x + n_out:n_in + 2 * n_x + n_out]
        rest = refs[n_in + 2 * n_x + n_out:]
        starts, waits = _exchange_copies(x_refs, got_refs, *rest[n_scr:], gather)
        ids = [pl.program_id(i) for i in range(len(grid))]
        first = functools.reduce(lambda p, q: p & q, [i == 0 for i in ids])
        last = functools.reduce(lambda p, q: p & q, [i == g - 1 for i, g in zip(ids, grid)])

        @pl.when(first)
        def _():
            for f in starts:
                f()

        body(*in_refs, *out_refs, *rest[:n_scr])

        @pl.when(last)
        def _():
            for f in waits:
                f()

    any_spec = pl.BlockSpec(memory_space=pl.ANY)
    outs = pl.pallas_call(
        wrapped, name=name, grid=grid, in_specs=list(in_specs) + [any_spec] * n_x,
        out_specs=o_specs + [any_spec] * n_x, out_shape=o_shape + _exchange_shapes(xs, gather),
        scratch_shapes=list(scratch) + _exchange_sems(n_x), compiler_params=_params())(*ins, *xs)
    res = outs[:n_out]
    return (res[0] if single else res), outs[n_out:]


def _mm_call(name, dims, grid, red_axis, ins, in_specs, out_shape, out_spec, acc_shape,
             scale=1.0, add_scale=None, exch=None):
    nred = grid[red_axis]

    def body(*refs):
        if add_scale is None:
            a_ref, b_ref, o_ref, acc_ref = refs
            add_ref = None
        else:
            a_ref, b_ref, add_ref, o_ref, acc_ref = refs
        k = pl.program_id(red_axis)

        @pl.when(k == 0)
        def _():
            acc_ref[...] = jnp.zeros_like(acc_ref)

        acc_ref[...] += _dot(a_ref[...].astype(BF16), b_ref[...].astype(BF16), dims)

        @pl.when(k == nred - 1)
        def _():
            r = acc_ref[...]
            if scale != 1.0:
                r = r * scale
            if add_ref is not None:
                r = r + add_scale * add_ref[...].astype(F32)
            o_ref[...] = r.astype(o_ref.dtype)

    return _call(name, body, grid, ins, in_specs, out_spec, out_shape, [pltpu.VMEM(acc_shape, F32)], exch)


def _tile(dim, cap):
    t = min(dim, cap)
    while dim % t or t % 128:
        t -= 128
        assert t > 0, (dim, cap)
    return t


def _mm(name, a, b, dims, out_dtype, scale=1.0, add=None, add_scale=None, tm=512, tn=1024, tk=1024):
    if dims == "tn":
        kd, m = a.shape
        n = b.shape[1]
    else:
        m, kd = a.shape
        n = b.shape[1] if dims == "nn" else b.shape[0]
    tm, tn, tk = _tile(m, tm), _tile(n, tn), _tile(kd, tk)
    a_spec = (pl.BlockSpec((tk, tm), lambda i, j, k: (k, i)) if dims == "tn"
              else pl.BlockSpec((tm, tk), lambda i, j, k: (i, k)))
    b_spec = (pl.BlockSpec((tn, tk), lambda i, j, k: (j, k)) if dims == "nt"
              else pl.BlockSpec((tk, tn), lambda i, j, k: (k, j)))
    o_spec = pl.BlockSpec((tm, tn), lambda i, j, k: (i, j))
    ins, specs = [a, b], [a_spec, b_spec]
    if add is not None:
        ins.append(add)
        specs.append(o_spec)
    return _mm_call(name, dims, (m // tm, n // tn, kd // tk), 2, ins, specs,
                    jax.ShapeDtypeStruct((m, n), out_dtype), o_spec, (tm, tn),
                    scale=scale, add_scale=add_scale if add is not None else None)


def _ffn_in(name, x, wg, tm=512, exch=None):
    n = x.shape[0]
    nj = N_DEV // 2

    def body(x_ref, wgate_ref, wup_ref, hg_ref, hu_ref, act_ref):
        xb = x_ref[...].astype(BF16)
        g = _dot(xb, wgate_ref[...])
        u = _dot(xb, wup_ref[...])
        hg_ref[...] = g.astype(BF16)
        hu_ref[...] = u.astype(BF16)
        act_ref[...] = (g * _sigmoid(g) * u).astype(BF16)

    blk = pl.BlockSpec((None, tm, FF_BLK), lambda j, i: (j, i, 0))
    shp = jax.ShapeDtypeStruct((nj, n, FF_BLK), BF16)
    return _call(name, body, (nj, n // tm), [x, wg, wg],
                 [pl.BlockSpec((tm, D_MODEL), lambda j, i: (i, 0)),
                  pl.BlockSpec((None, D_MODEL, FF_BLK), lambda j, i: (j, 0, 0)),
                  pl.BlockSpec((None, D_MODEL, FF_BLK), lambda j, i: (j + nj, 0, 0))],
                 [blk, blk, blk], [shp, shp, shp], exch=exch)


def _ffn_out_bwd(name, dz, wout, hg, hu, act, tm=512, exch=None):
    n = dz.shape[0]
    nj, ni = N_DEV // 2, n // tm

    def body(dz_ref, w_ref, hg_ref, hu_ref, act_ref, dhg_ref, dhu_ref, dw_ref, acc_ref):
        i = pl.program_id(1)
        dzb = dz_ref[...].astype(BF16)
        dact = 0.5 * _dot(dzb, w_ref[...], "nt")
        g = hg_ref[...].astype(F32)
        u = hu_ref[...].astype(F32)
        s = _sigmoid(g)
        dhg_ref[...] = (dact * u * (s * (1.0 + g * (1.0 - s)))).astype(BF16)
        dhu_ref[...] = (dact * (g * s)).astype(BF16)

        @pl.when(i == 0)
        def _():
            acc_ref[...] = jnp.zeros_like(acc_ref)

        acc_ref[...] += _dot(act_ref[...], dzb, "tn")

        @pl.when(i == ni - 1)
        def _():
            dw_ref[...] = (0.5 * acc_ref[...]).astype(dw_ref.dtype)

    blk = pl.BlockSpec((None, tm, FF_BLK), lambda j, i: (j, i, 0))
    wblk = pl.BlockSpec((FF_BLK, D_MODEL), lambda j, i: (j, 0))
    shp = jax.ShapeDtypeStruct((nj, n, FF_BLK), BF16)
    return _call(name, body, (nj, ni), [dz, wout, hg, hu, act],
                 [pl.BlockSpec((tm, D_MODEL), lambda j, i: (i, 0)), wblk, blk, blk, blk],
                 [blk, blk, wblk], [shp, shp, jax.ShapeDtypeStruct((D_FF, D_MODEL), BF16)],
                 [pltpu.VMEM((FF_BLK, D_MODEL), F32)], exch=exch)


def _mm_ln(name, a, b, xres, g, beta, c, tgt=None, a_blocked=False, tm=512, tk=512):
    if a_blocked:
        nk, n, kb = a.shape
        a_spec = pl.BlockSpec((None, tm, kb), lambda i, k: (k, i, 0))
    else:
        n, kd = a.shape
        kb = _tile(kd, tk)
        nk = kd // kb
        a_spec = pl.BlockSpec((tm, kb), lambda i, k: (i, k))
    d = b.shape[1]
    with_loss = tgt is not None

    def body(*refs):
        if with_loss:
            a_ref, b_ref, x_ref, g_ref, be_ref, t_ref, o_ref, z_ref, l_ref, acc_ref = refs
        else:
            a_ref, b_ref, x_ref, g_ref, be_ref, o_ref, ob_ref, z_ref, acc_ref = refs
        i, k = pl.program_id(0), pl.program_id(1)

        @pl.when(k == 0)
        def _():
            acc_ref[...] = jnp.zeros_like(acc_ref)

        acc_ref[...] += _dot(a_ref[...].astype(BF16), b_ref[...].astype(BF16))

        @pl.when(k == nk - 1)
        def _():
            z = ALPHA * x_ref[...] + c * acc_ref[...]
            z_ref[...] = z
            mu = jnp.mean(z, axis=-1, keepdims=True)
            zc = z - mu
            var = jnp.mean(zc * zc, axis=-1, keepdims=True)
            y = zc * lax.rsqrt(var + LN_EPS) * g_ref[...] + be_ref[...]
            if with_loss:
                err = y - t_ref[...]
                o_ref[...] = err * (1.0 / d)
                part = 0.5 * jnp.sum(jnp.sum(err * err, axis=-1, keepdims=True), axis=0, keepdims=True) * (1.0 / d)

                @pl.when(i == 0)
                def _():
                    l_ref[...] = jnp.zeros_like(l_ref)

                l_ref[...] += jnp.broadcast_to(part, l_ref.shape)
            else:
                o_ref[...] = y
                ob_ref[...] = y.astype(BF16)

    row = pl.BlockSpec((tm, d), lambda i, k: (i, 0))
    vec = pl.BlockSpec((1, d), lambda i, k: (0, 0))
    ins = [a, b, xres, g, beta]
    in_specs = [a_spec, pl.BlockSpec((kb, d), lambda i, k: (k, 0)), row, vec, vec]
    out_specs = [row, row]
    out_shape = [jax.ShapeDtypeStruct((n, d), F32), jax.ShapeDtypeStruct((n, d), F32)]
    if not with_loss:
        out_specs.insert(1, row)
        out_shape.insert(1, jax.ShapeDtypeStruct((n, d), BF16))
    if with_loss:
        ins.append(tgt)
        in_specs.append(row)
        out_specs.append(pl.BlockSpec((1, 128), lambda i, k: (0, 0)))
        out_shape.append(jax.ShapeDtypeStruct((1, 128), F32))
    return _call(name, body, (n // tm, nk), ins, in_specs, out_specs, out_shape, [pltpu.VMEM((tm, d), F32)])


def _rowwise(name, fn, rows, params, out_rows, out_accs, tm=256, exch=None):
    specs, ins = [], []
    for r in rows:
        arr, w, cb = r if isinstance(r, tuple) else (r, r.shape[1], 0)
        ins.append(arr)
        specs.append(pl.BlockSpec((tm, w), functools.partial(lambda i, cb: (i, cb), cb=cb)))
    n = ins[0].shape[0]
    for p in params:
        ins.append(p)
        specs.append(pl.BlockSpec(p.shape, lambda i: (0, 0)))
    n_in, n_or = len(ins), len(out_rows)

    def body(*refs):
        outs = fn(*[r[...] for r in refs[:n_in]])
        o_refs = refs[n_in:]
        for o_ref, o in zip(o_refs[:n_or], outs[:n_or]):
            o_ref[...] = o.astype(o_ref.dtype)
        if out_accs:
            @pl.when(pl.program_id(0) == 0)
            def _():
                for a_ref in o_refs[n_or:]:
                    a_ref[...] = jnp.zeros_like(a_ref)

            for a_ref, a in zip(o_refs[n_or:], outs[n_or:]):
                a_ref[...] += a.astype(F32)

    out_specs = [pl.BlockSpec((tm, w), lambda i: (i, 0)) for w, _ in out_rows]
    out_specs += [pl.BlockSpec(s, lambda i: (0, 0)) for s in out_accs]
    out_shape = [jax.ShapeDtypeStruct((n, w), dt) for w, dt in out_rows]
    out_shape += [jax.ShapeDtypeStruct(s, F32) for s in out_accs]
    return _call(name, body, (n // tm,), ins, specs, out_specs, out_shape, exch=exch)


def _vjp_of(fn, n_in):
    def g(*args):
        ins, cts = args[:n_in], args[n_in:]
        outs, pull = jax.vjp(fn, *ins)
        return pull(tuple(c.astype(o.dtype) for c, o in zip(cts, outs)))
    return g


def _ln_bwd(name, z, g, ct):
    def fn(zt, ct_, gt):
        mu = jnp.mean(zt, axis=-1, keepdims=True)
        zc = zt - mu
        rstd = lax.rsqrt(jnp.mean(zc * zc, axis=-1, keepdims=True) + LN_EPS)
        xh = zc * rstd
        dxh = ct_ * gt
        dz = rstd * (dxh - jnp.mean(dxh, axis=-1, keepdims=True)
                     - xh * jnp.mean(dxh * xh, axis=-1, keepdims=True))
        return dz, dz, jnp.sum(ct_ * xh, axis=0, keepdims=True), jnp.sum(ct_, axis=0, keepdims=True)

    d = z.shape[1]
    return _rowwise(name, fn, [z, ct], [g], [(d, F32), (d, BF16)], [(1, d), (1, d)])


def _shift(name, src, mu_a, mu_b, q=None, tt=256, out_dtype=F32):
    bsz, t, _ = src.shape
    nt, r8, w = t // tt, tt // 8, SHIFT_PAD
    with_q = q is not None

    def body(cur_ref, prev_ref, next_ref, *rest):
        b, i = pl.program_id(0), pl.program_id(1)
        cur = cur_ref[...]
        prow = jnp.where(i > 0, prev_ref[7:8, :], 0.0)
        nrow = jnp.where(i < nt - 1, next_ref[0:1, :], 0.0)
        rid = lax.broadcasted_iota(jnp.int32, cur.shape, 0)
        dprev = jnp.where(rid == 0, prow, pltpu.roll(cur, 1, 0)) - cur
        dnext = jnp.where(rid == tt - 1, nrow, pltpu.roll(cur, tt - 1, 0)) - cur
        if with_q:
            q_ref, da_ref, db_ref = rest

            @pl.when((b == 0) & (i == 0))
            def _():
                da_ref[...] = jnp.zeros_like(da_ref)
                db_ref[...] = jnp.zeros_like(db_ref)

            qv = q_ref[...]
            da_ref[...] += jnp.sum(qv * dprev, axis=0, keepdims=True)
            db_ref[...] += jnp.sum(qv * dnext, axis=0, keepdims=True)
        else:
            ma_ref, mb_ref, o_ref = rest
            o_ref[...] = (cur + ma_ref[...] * dprev + mb_ref[...] * dnext).astype(o_ref.dtype)

    cur_spec = pl.BlockSpec((None, tt, w), lambda b, i: (b, i, 0))
    in_specs = [cur_spec,
                pl.BlockSpec((None, 8, w), lambda b, i: (b, jnp.maximum(i * r8 - 1, 0), 0)),
                pl.BlockSpec((None, 8, w), lambda b, i: (b, jnp.minimum((i + 1) * r8, t // 8 - 1), 0))]
    vec = pl.BlockSpec((1, w), lambda b, i: (0, 0))
    if with_q:
        return _call(name, body, (bsz, nt), [src, src, src, q], in_specs + [cur_spec], [vec, vec],
                     [jax.ShapeDtypeStruct((1, w), F32)] * 2)
    return _call(name, body, (bsz, nt), [src, src, src, mu_a, mu_b], in_specs + [vec, vec], cur_spec,
                 jax.ShapeDtypeStruct((bsz, t, w), out_dtype))


CONV_BLK = 128


def _halo_specs(t, tt, w):
    r16 = tt // 16
    return [pl.BlockSpec((None, tt, w), lambda b, i: (b, i, 0)),
            pl.BlockSpec((None, 16, w), lambda b, i: (b, jnp.maximum(i * r16 - 1, 0), 0)),
            pl.BlockSpec((None, 16, w), lambda b, i: (b, jnp.minimum((i + 1) * r16, t // 16 - 1), 0))]


def _fill_pad(pad_ref, cur_ref, prev_ref, next_ref, i, nt, tt):
    pad_ref[0:16, :] = jnp.where(i > 0, prev_ref[...], 0.0)
    pad_ref[16:16 + tt, :] = cur_ref[...]
    pad_ref[16 + tt:32 + tt, :] = jnp.where(i < nt - 1, next_ref[...], 0.0)


def _dwconv(name, u, dw32, bias, flip, tt=512):
    bsz, t, w = u.shape
    tt = min(tt, t)
    nt = t // tt

    def body(cur_ref, prev_ref, next_ref, dw_ref, b_ref, o_ref, pad_ref):
        i = pl.program_id(1)
        _fill_pad(pad_ref, cur_ref, prev_ref, next_ref, i, nt, tt)
        for r0 in range(0, tt, CONV_BLK):
            for cs in (slice(c0, c0 + CONV_BLK) for c0 in range(0, w, CONV_BLK)):
                acc = jnp.broadcast_to(b_ref[:, cs], (CONV_BLK, CONV_BLK))
                for k in range(CONV_K):
                    kk = CONV_K - 1 - k if flip else k
                    acc = acc + pad_ref[pl.ds(r0 + 1 + k, CONV_BLK), cs] * dw_ref[kk:kk + 1, cs]
                o_ref[r0:r0 + CONV_BLK, cs] = acc

    return _call(name, body, (bsz, nt), [u, u, u, dw32, bias],
                 _halo_specs(t, tt, w) + [pl.BlockSpec((32, w), lambda b, i: (0, 0)),
                                          pl.BlockSpec((1, w), lambda b, i: (0, 0))],
                 pl.BlockSpec((None, tt, w), lambda b, i: (b, i, 0)), jax.ShapeDtypeStruct((bsz, t, w), F32),
                 [pltpu.VMEM((tt + 32, w), F32)])


def _dwconv_dw(name, u, dc, tt=512):
    bsz, t, w = u.shape
    tt = min(tt, t)
    nt = t // tt

    def body(cur_ref, prev_ref, next_ref, dc_ref, ddw_ref, db_ref, pad_ref):
        b, i = pl.program_id(0), pl.program_id(1)
        _fill_pad(pad_ref, cur_ref, prev_ref, next_ref, i, nt, tt)

        @pl.when((b == 0) & (i == 0))
        def _():
            ddw_ref[...] = jnp.zeros_like(ddw_ref)
            db_ref[...] = jnp.zeros_like(db_ref)

        dcv = dc_ref[...]
        db_ref[...] += jnp.sum(dcv, axis=0, keepdims=True)
        for k in range(CONV_K):
            ddw_ref[k:k + 1, :] += jnp.sum(dcv * pad_ref[pl.ds(1 + k, tt), :], axis=0, keepdims=True)

    return _call(name, body, (bsz, nt), [u, u, u, dc],
                 _halo_specs(t, tt, w) + [pl.BlockSpec((None, tt, w), lambda b, i: (b, i, 0))],
                 [pl.BlockSpec((32, w), lambda b, i: (0, 0)), pl.BlockSpec((1, w), lambda b, i: (0, 0))],
                 [jax.ShapeDtypeStruct((32, w), F32), jax.ShapeDtypeStruct((1, w), F32)],
                 [pltpu.VMEM((tt + 32, w), F32)])


PAIR = 2 * HEAD
N_PAIRS = RW // PAIR


def _chunk_pairs(s, r, lw, k, v, kk, a, sgn, tm_known=None):
    n, m = CHUNK, 2 * CHUNK
    in_a = lax.broadcasted_iota(jnp.int32, (n, PAIR), 1) < HEAD

    def stack2(z):
        return jnp.concatenate([jnp.where(in_a, z, 0.0), jnp.where(in_a, 0.0, z)], axis=0)

    def each(f, *lists):
        return [f(*z) for z in zip(*lists)]

    sgn_f = sgn.astype(F32)
    row2 = lax.broadcasted_iota(jnp.int32, (m, m), 0)
    col2 = lax.broadcasted_iota(jnp.int32, (m, m), 1)
    same = (row2 >= n) == (col2 >= n)
    dlt = ((row2 & (n - 1)) - (col2 & (n - 1))) * sgn
    incl, strict = same & (dlt >= 0), same & (dlt > 0)
    eye = jnp.where(row2 == col2, 1.0, 0.0)

    cum = each(lambda lw_: _dir_cumsum(lw_, sgn_f), lw)
    tot = each(lambda lw_: jnp.sum(lw_, axis=0, keepdims=True), lw)
    e_neg = each(lambda c_: jnp.exp(-c_), cum)
    e_rest = each(lambda t_, c_: jnp.exp(t_ - c_), tot, cum)
    beta = each(lambda kk_, a_: kk_ * a_, kk, a)
    lhs = each(lambda kk_, c_, lw_, r_: jnp.concatenate(
        [stack2(-kk_ * jnp.exp(c_ - lw_)), stack2(r_ * jnp.exp(c_))], axis=0), kk, cum, lw, r)
    rhs = each(lambda b_, k_, e_: jnp.concatenate([stack2(b_ * e_), stack2(k_ * e_)], axis=0), beta, k, e_neg)
    sc = each(lambda l_, r_: _dot3(l_, r_, "nt"), lhs, rhs)
    l_ab = each(lambda sc_: jnp.where(strict, sc_[0:m, 0:m], 0.0), sc)
    l_ak = each(lambda sc_: jnp.where(strict, sc_[0:m, m:2 * m], 0.0), sc)
    m_r = each(lambda sc_: jnp.where(jnp.concatenate([incl, incl], axis=1), sc_[m:2 * m, :], 0.0), sc)
    tm = _tri_inv(l_ab, eye) if tm_known is None else _tri_inv_known(l_ab, tm_known)
    z = _dot1_two(lhs, s, "nt")
    v2 = each(stack2, v)
    u2 = _dot1_two(tm, each(lambda z_, lv_: z_[0:m] + lv_, z, _dot1_two(l_ak, v2)))
    uv = each(lambda u_, v_: jnp.concatenate([u_, v_], axis=0), u2, v2)
    y2 = each(lambda z_, mu_: z_[m:2 * m] + mu_, z, _dot1_two(m_r, uv))
    bk = each(lambda b_, k_, e_: jnp.concatenate([stack2(b_ * e_), stack2(k_ * e_)], axis=0), beta, k, e_rest)
    s_new = each(lambda s_, t_, d_: s_ * jnp.exp(t_) + d_, s, tot, _dot1_two(uv, bk, "tn"))
    return each(lambda y_: y_[0:n] + y_[n:m], y2), s_new, tm


SCAN_SEQS = 4
N_CHAINS = SCAN_SEQS * N_PAIRS


def _pair_tiles(ref):
    return [ref[q, :, p * PAIR:(p + 1) * PAIR] for q in range(SCAN_SEQS) for p in range(N_PAIRS)]


def _store_tiles(ref, tiles):
    for q in range(SCAN_SEQS):
        for p in range(N_PAIRS):
            ref[q, :, p * PAIR:(p + 1) * PAIR] = tiles[q * N_PAIRS + p]


def _scan_specs(order):
    shared = pl.BlockSpec((SCAN_SEQS, CHUNK, RW), lambda d, b, c: (b, order(d, c), 0))
    per_dir = pl.BlockSpec((SCAN_SEQS, CHUNK, RW), lambda d, b, c: (b, order(d, c), d))
    state = pl.BlockSpec((None, SCAN_SEQS, None, N_PAIRS, PAIR, PAIR), lambda d, b, c: (d, b, order(d, c), 0, 0, 0))
    return shared, per_dir, state


def _scan_fwd(r, v, kk, lw, kd, a, bsz, exch=None):
    n = r.shape[0]
    t = n // bsz
    nc = t // CHUNK

    def order(d, c):
        return c + d * (nc - 1 - 2 * c)

    def body(r_ref, v_ref, kk_ref, lw_ref, kd_ref, a_ref, y_ref, s0_ref, tm_ref, s_ref):
        d, c = pl.program_id(0), pl.program_id(2)

        @pl.when(c == 0)
        def _():
            s_ref[...] = jnp.zeros_like(s_ref)

        s = [s_ref[i] for i in range(N_CHAINS)]
        y, s_new, tm = _chunk_pairs(s, *[_pair_tiles(ref) for ref in (r_ref, lw_ref, kd_ref, v_ref, kk_ref, a_ref)],
                                    1 - 2 * d)
        _store_tiles(y_ref, y)
        for i in range(N_CHAINS):
            s0_ref[i // N_PAIRS, i % N_PAIRS] = s[i]
            tm_ref[i // N_PAIRS, i % N_PAIRS] = tm[i].astype(BF16)
            s_ref[i] = s_new[i]

    shared, per_dir, state = _scan_specs(order)
    seq = lambda z: z.reshape(bsz, t, z.shape[1])
    res = _call("scan_fwd", body, (2, bsz // SCAN_SEQS, nc), [seq(z) for z in (r, v, kk, lw, kd, a)],
                [shared, shared, shared, per_dir, per_dir, per_dir], [per_dir, state, state],
                [jax.ShapeDtypeStruct((bsz, t, 2 * RW), F32),
                 jax.ShapeDtypeStruct((2, bsz, nc, N_PAIRS, PAIR, PAIR), F32),
                 jax.ShapeDtypeStruct((2, bsz, nc, N_PAIRS, PAIR, PAIR), BF16)],
                [pltpu.VMEM((N_CHAINS, PAIR, PAIR), F32)], exch)
    (y, s0, tm), got = res if exch else (res, None)
    y = y.reshape(n, 2 * RW)
    return ([y, s0, tm], got) if exch else [y, s0, tm]


def _scan_bwd(r, v, kk, lw, kd, a, s0, tm, dy, bsz, exch=None):
    n = r.shape[0]
    t = n // bsz
    nc = t // CHUNK

    def order(d, c):
        cc = nc - 1 - c
        return cc + d * (nc - 1 - 2 * cc)

    def body(r_ref, v_ref, kk_ref, lw_ref, kd_ref, a_ref, dy_ref, s0_ref, tm_ref,
             dr_ref, dv_ref, dkk_ref, dlw_ref, dkd_ref, da_ref, ds_ref):
        d, c = pl.program_id(0), pl.program_id(2)

        @pl.when(c == 0)
        def _():
            ds_ref[...] = jnp.zeros_like(ds_ref)

        sgn = 1 - 2 * d
        tm_known = [tm_ref[i // N_PAIRS, i % N_PAIRS].astype(F32) for i in range(N_CHAINS)]
        _, pull = jax.vjp(lambda *ops: _chunk_pairs(*ops, sgn, tm_known)[:2],
                          [s0_ref[i // N_PAIRS, i % N_PAIRS] for i in range(N_CHAINS)],
                          *[_pair_tiles(ref) for ref in (r_ref, lw_ref, kd_ref, v_ref, kk_ref, a_ref)])
        grads = pull((_pair_tiles(dy_ref), [ds_ref[i] for i in range(N_CHAINS)]))
        for i in range(N_CHAINS):
            ds_ref[i] = grads[0][i]
        for o_ref, gx in zip((dr_ref, dlw_ref, dkd_ref, dv_ref, dkk_ref, da_ref), grads[1:]):
            _store_tiles(o_ref, gx)

    shared, per_dir, state = _scan_specs(order)
    shp = jax.ShapeDtypeStruct((bsz, t, 2 * RW), F32)
    seq = lambda z: z.reshape(bsz, t, z.shape[1])
    res = _call("scan_bwd", body, (2, bsz // SCAN_SEQS, nc), [seq(z) for z in (r, v, kk, lw, kd, a, dy)] + [s0, tm],
                [shared, shared, shared, per_dir, per_dir, per_dir, per_dir, state, state],
                [per_dir] * 6, [shp] * 6, [pltpu.VMEM((N_CHAINS, PAIR, PAIR), F32)], exch)
    outs, got = res if exch else (res, None)
    outs = [z.reshape(n, 2 * RW) for z in outs]
    return (outs, got) if exch else outs


def _prep_fn(ps, w0, w2bd, a0, a2bd, g2p, k_k, k_a, hsum):
    head_sum = lambda z: _head_sum(z, hsum)
    r, k, v = ps[:, 0:RW], ps[:, RW:2 * RW], ps[:, 2 * RW:3 * RW]
    wd, ad, gd = ps[:, 1536:1664], ps[:, 1664:1792], ps[:, 1792:2048]
    logw = -DECAY_SCALE * _sigmoid(_dot1(jnp.tanh(wd), w2bd) + w0)
    a = _sigmoid(_dot1(ad, a2bd) + a0)
    g = _dot1(_sigmoid(gd), g2p)
    kkr = k * k_k
    kk = kkr / jnp.maximum(jnp.sqrt(head_sum(kkr * kkr)), NORM_EPS)
    k2 = jnp.concatenate([k, k], axis=1)
    ka2 = jnp.concatenate([k_a, k_a], axis=1)
    kd = k2 * (1.0 + (a - 1.0) * ka2)
    return r, v, kk, logw, a, kd, g


def _post_fn(y2, r, v, kd, g, lnx_g, lnx_b, r_k, hsum):
    head_sum = lambda z: _head_sum(z, hsum)
    y = y2[:, 0:RW] + y2[:, RW:2 * RW]
    mu = head_sum(y) * (1.0 / HEAD)
    yc = y - mu
    var = head_sum(yc * yc) * (1.0 / HEAD)
    yn = yc * lax.rsqrt(var + GN_EPS) * lnx_g + lnx_b
    bonus = head_sum(r * (kd[:, 0:RW] + kd[:, RW:2 * RW]) * r_k) * v
    return ((yn + bonus) * g,)


def _glu_fn(pa, pb):
    return (pa * _sigmoid(pb),)


def _conv_out_fn(cv, ln_g, ln_b):
    mu = jnp.mean(cv, axis=-1, keepdims=True)
    cc = cv - mu
    var = jnp.mean(cc * cc, axis=-1, keepdims=True)
    y = cc * lax.rsqrt(var + LN_EPS) * ln_g + ln_b
    return (y * _sigmoid(y),)


def _local_step(x, tgt, w, ex=None):
    bsz, t, d = x.shape
    n = bsz * t
    x2d, tgt2d = x.reshape(n, d), tgt.reshape(n, d)
    hsum = jnp.tile(jnp.kron(jnp.eye(N_HEADS, dtype=BF16), jnp.ones((HEAD, HEAD), BF16)), (3, 1))
    w = dict(w)
    parts = {} if ex else None

    def hosted(result, finish=None):
        if not ex:
            return result
        outs, got = result
        if finish is not None:
            w.update(finish(got))
        return outs

    xb = x2d.astype(BF16)
    hg1, hu1, act1 = hosted(_ffn_in("ffn1_in", xb, w["ffn1_w_in"], exch=(ex["g1"][0], True) if ex else None),
                            ex["g1"][1] if ex else None)
    x1, x1b, z1 = _mm_ln("ffn1_out_ln1", act1, w["ffn1_w_out"], x2d, w["ln1_g"], w["ln1_b"], 0.5, a_blocked=True)
    p = _mm("w_in_proj", x1b, w["w_in"], "nn", F32)
    p3 = p.reshape(bsz, t, IN_PAD)
    ps = _shift("shift_fwd", p3, w["mu_prev"], w["mu_next"]).reshape(n, SHIFT_PAD)
    prep_params = [w["w0"], w["w2"], w["a0"], w["a2"], w["g2"], w["k_k"], w["k_a"], hsum]
    r, v, kk, logw, a, kd, g = _rowwise(
        "rwkv_prep", _prep_fn, [ps], prep_params,
        [(RW, F32), (RW, F32), (RW, F32), (2 * RW, F32), (2 * RW, F32), (2 * RW, F32), (RW, F32)], [])
    y2, s0, tm = hosted(_scan_fwd(r, v, kk, logw, kd, a, bsz, exch=(ex["g2"][0], True) if ex else None),
                    ex["g2"][1] if ex else None)
    post_params = [w["lnx_g"], w["lnx_b"], w["r_k"], hsum]
    (y_rwkv,) = _rowwise("rwkv_post", _post_fn, [y2, r, v, kd, g], post_params, [(RW, BF16)], [])
    (u,) = _rowwise("conv_glu", _glu_fn, [(p, CW, 4), (p, CW, 5)], [], [(CW, F32)], [])
    cv = _dwconv("conv_dw", u.reshape(bsz, t, CW), w["conv_dw"], w["conv_b"], False).reshape(n, CW)
    (y_conv,) = _rowwise("conv_out", _conv_out_fn, [cv], [w["conv_ln_g"], w["conv_ln_b"]], [(CW, BF16)], [])
    ycat = jnp.concatenate([y_rwkv, y_conv], axis=1)
    x2, x2b, z2 = _mm_ln("w_out_ln2", ycat, w["w_out"], x1, w["ln2_g"], w["ln2_b"], 1.0)
    hg2, hu2, act2 = _ffn_in("ffn2_in", x2b, w["ffn2_w_in"])
    dx3, z3, loss = _mm_ln("ffn2_out_ln3", act2, w["ffn2_w_out"], x2, w["ln3_g"], w["ln3_b"], 0.5,
                           tgt=tgt2d, a_blocked=True)

    gr = {}
    dz3, dz3b, gr["ln3_g"], gr["ln3_b"] = _ln_bwd("ln3_bwd", z3, w["ln3_g"], dx3)
    dx2, _ = _ffn_bwd("ffn2", gr, dz3, dz3b, x2b, w["ffn2_w_in"], w["ffn2_w_out"], hg2, hu2, act2)
    dz2, dz2b, gr["ln2_g"], gr["ln2_b"] = _ln_bwd("ln2_bwd", z2, w["ln2_g"], dx2)
    gr["w_out"] = _mm("w_out_wgrad", ycat, dz2b, "tn", BF16, tk=512)
    dycat = _mm("w_out_dgrad", dz2b, w["w_out"], "nt", F32)
    conv_out_bwd = _vjp_of(_conv_out_fn, 3)
    dcv, gr["conv_ln_g"], gr["conv_ln_b"] = _rowwise(
        "conv_out_bwd", lambda cv_, ct_, g_, b_: conv_out_bwd(cv_, g_, b_, ct_),
        [cv, (dycat, CW, 1)], [w["conv_ln_g"], w["conv_ln_b"]], [(CW, F32)], [(1, CW), (1, CW)])
    dcv3 = dcv.reshape(bsz, t, CW)
    gr["conv_dw"], gr["conv_b"] = _dwconv_dw("conv_dw_wgrad", u.reshape(bsz, t, CW), dcv3)
    du = _dwconv("conv_dw_dgrad", dcv3, w["conv_dw"], jnp.zeros((1, CW), F32), True).reshape(n, CW)

    def glu_bwd(pa, pb, ct):
        return (jnp.concatenate(_vjp_of(_glu_fn, 2)(pa, pb, ct), axis=1),)

    (dp_conv,) = _rowwise("conv_glu_bwd", glu_bwd, [(p, CW, 4), (p, CW, 5), du], [], [(2 * CW, BF16)], [])

    def post_bwd(y2_, r_, v_, kd_, g_, ct, lg, lb, rk, hs):
        return _vjp_of(lambda *z: _post_fn(*z, hs), 8)(y2_, r_, v_, kd_, g_, lg, lb, rk, ct)

    dy2, dr_post, dv_post, dkd_post, dg, gr["lnx_g"], gr["lnx_b"], gr["r_k"] = _rowwise(
        "rwkv_post_bwd", post_bwd, [y2, r, v, kd, g, (dycat, RW, 0)], post_params,
        [(2 * RW, F32), (RW, F32), (RW, F32), (2 * RW, F32), (RW, F32)], [(1, RW), (1, RW), (1, RW)])
    sends = [jnp.concatenate(gr["ffn2_w_in"], axis=0), gr["ffn2_w_out"].reshape(N_DEV, D_FF // N_DEV, D_MODEL),
             gr["w_out"].reshape(N_DEV, D_MODEL // N_DEV, D_MODEL)]
    res = _scan_bwd(r, v, kk, logw, kd, a, s0, tm, dy2, bsz, exch=(sends, False) if ex else None)
    if ex:
        res, got = res
        parts.update(zip(("ffn2_w_in", "ffn2_w_out", "w_out"), got))
    dr_s, dv_s, dkk_s, dlw, dkd_s, da = res

    def prep_bwd(ps_, dr2, dr1, dv2, dv1, dkk2, dlw_, da_, dkd2, dkd1, dg_, *prm):
        half = lambda z: z[:, 0:RW] + z[:, RW:2 * RW]
        return _vjp_of(lambda *z: _prep_fn(*z, prm[-1]), 8)(
            ps_, *prm[:-1], half(dr2) + dr1, half(dv2) + dv1, half(dkk2), dlw_, da_, dkd2 + dkd1, dg_)

    dps, gr["w0"], gr["w2"], gr["a0"], gr["a2"], gr["g2"], gr["k_k"], gr["k_a"] = _rowwise(
        "rwkv_prep_bwd", prep_bwd,
        [ps, dr_s, dr_post, dv_s, dv_post, dkk_s, dlw, da, dkd_s, dkd_post, dg], prep_params,
        [(SHIFT_PAD, F32)], [q.shape for q in prep_params[:-1]])
    dps3 = dps.reshape(bsz, t, SHIFT_PAD)
    gr["mu_prev"], gr["mu_next"] = _shift("shift_dmu", p3, None, None, q=dps3)
    dp_shift = _shift("shift_bwd", dps3, w["mu_next"], w["mu_prev"], out_dtype=BF16).reshape(n, SHIFT_PAD)
    dp = jnp.concatenate([dp_shift, dp_conv], axis=1)
    gr["w_in"] = _mm("w_in_wgrad", x1b, dp, "tn", BF16, tk=512)
    dx1 = _mm("w_in_dgrad", dp, w["w_in"], "nt", F32, add=dz2, add_scale=ALPHA)
    dz1, dz1b, gr["ln1_g"], gr["ln1_b"] = _ln_bwd("ln1_bwd", z1, w["ln1_g"], dx1)
    riders = None
    if ex:
        gw_in = _unpad_in_cols(gr["w_in"]).reshape(D_MODEL, N_DEV, IN_COLS // N_DEV).transpose(1, 0, 2)
        small = _pack([_grad_small(nm, gr[nm]) for nm in SMALL_SHARDED + SMALL_REPL])
        riders = {"out_bwd": ([gw_in], False), "in_wgrad_up": ([small], True)}
    grad_x, got = _ffn_bwd("ffn1", gr, dz1, dz1b, xb, w["ffn1_w_in"], w["ffn1_w_out"], hg1, hu1, act1, riders)
    if ex:
        parts.update(w_in=got["out_bwd"][0], small=got["in_wgrad_up"][0],
                     ffn1_w_out=got["w_out"], ffn1_w_in=got["w_in"])
    return loss[0, 0], grad_x.reshape(bsz, t, d), gr, parts


def _ffn_bwd(tag, gr, dz, dzb, xin, wg, wout, hg, hu, act, riders=None, tk=512):
    n = xin.shape[0]
    nj = N_DEV // 2
    nt = n // tk
    got = {}

    def hosted(key, result, on):
        if not on:
            return result
        got[key] = result[1][0] if key in ("w_out", "w_in") else result[1]
        return result[0]

    riders = riders or {}
    own = bool(riders)
    dhg, dhu, gr[tag + "_w_out"] = hosted(
        "out_bwd", _ffn_out_bwd(tag + "_out_bwd", dzb, wout, hg, hu, act, exch=riders.get("out_bwd")),
        "out_bwd" in riders)
    dw = []
    for nm, dh in (("gate", dhg), ("up", dhu)):
        if own and nm == "gate":
            key, send = "w_out", ([gr[tag + "_w_out"].reshape(N_DEV, D_FF // N_DEV, D_MODEL)], False)
        else:
            key, send = "in_wgrad_up", riders.get("in_wgrad_up") if nm == "up" else None
        dw.append(hosted(key, _mm_call(
            tag + "_in_wgrad_" + nm, "tn", (nj, nt), 1, [xin, dh],
            [pl.BlockSpec((tk, D_MODEL), lambda j, i: (i, 0)), pl.BlockSpec((None, tk, FF_BLK), lambda j, i: (j, i, 0))],
            jax.ShapeDtypeStruct((nj, D_MODEL, FF_BLK), BF16), pl.BlockSpec((None, D_MODEL, FF_BLK), lambda j, i: (j, 0, 0)),
            (D_MODEL, FF_BLK), exch=send), send is not None))
    gr[tag + "_w_in"] = dw
    dx = dz
    tm = 512
    for off, (nm, dh) in enumerate((("gate", dhg), ("up", dhu))):
        send = ([jnp.concatenate(dw, axis=0)], False) if own and nm == "gate" else None
        dx = hosted("w_in", _mm_call(
            tag + "_in_dgrad_" + nm, "nt", (n // tm, nj), 1, [dh, wg, dx],
            [pl.BlockSpec((None, tm, FF_BLK), lambda i, j: (j, i, 0)),
             pl.BlockSpec((None, D_MODEL, FF_BLK), functools.partial(lambda i, j, o: (j + o, 0, 0), o=off * nj)),
             pl.BlockSpec((tm, D_MODEL), lambda i, j: (i, 0))],
            jax.ShapeDtypeStruct((n, D_MODEL), F32), pl.BlockSpec((tm, D_MODEL), lambda i, j: (i, 0)),
            (tm, D_MODEL), add_scale=ALPHA if off == 0 else 1.0, exch=send), send is not None)
    return dx, got


def _adam_math(g, w, m, v):
    m = ADAM_B1 * m + (1.0 - ADAM_B1) * g
    v = ADAM_B2 * v + (1.0 - ADAM_B2) * (g * g)
    m_hat = m / (1.0 - ADAM_B1 ** ADAM_STEP)
    v_hat = v / (1.0 - ADAM_B2 ** ADAM_STEP)
    delta = -ADAM_LR * (m_hat / (jnp.sqrt(v_hat) + ADAM_EPS) + ADAM_WD * w)
    return delta, m, v


def _adam(name, parts, w, m, v, tr=128):
    rows, cols = w.shape
    tr = min(tr, rows)
    while rows % tr:
        tr -= 8

    def body(p_ref, w_ref, m_ref, v_ref, g_ref, d_ref, mo_ref, vo_ref):
        g = p_ref[0].astype(F32)
        for s in range(1, N_DEV):
            g = g + p_ref[s].astype(F32)
        g_ref[...] = g
        d_ref[...], mo_ref[...], vo_ref[...] = _adam_math(g, w_ref[...], m_ref[...], v_ref[...])

    blk = pl.BlockSpec((tr, cols), lambda i: (i, 0))
    shp = jax.ShapeDtypeStruct((rows, cols), F32)
    return _call(name, body, (rows // tr,), [parts, w, m, v],
                 [pl.BlockSpec((N_DEV, tr, cols), lambda i: (0, i, 0)), blk, blk, blk], [blk] * 4, [shp] * 4)


def _sum8(name, parts):
    _, rows, cols = parts.shape

    def body(p_ref, o_ref):
        g = p_ref[0]
        for s in range(1, N_DEV):
            g = g + p_ref[s]
        o_ref[...] = g

    return pl.pallas_call(body, name=name, out_shape=jax.ShapeDtypeStruct((rows, cols), F32),
                          compiler_params=_params())(parts)


def _adam_small(name, g, w, m, v):
    def body(g_ref, w_ref, m_ref, v_ref, d_ref, mo_ref, vo_ref):
        d_ref[...], mo_ref[...], vo_ref[...] = _adam_math(g_ref[...], w_ref[...], m_ref[...], v_ref[...])

    shp = jax.ShapeDtypeStruct(g.shape, F32)
    return pl.pallas_call(body, name=name, out_shape=[shp] * 3, compiler_params=_params())(g, w, m, v)


def _pack(arrs, lane=128):
    flat = jnp.concatenate([a.reshape(-1).astype(F32) for a in arrs])
    pad = (-flat.shape[0]) % (8 * lane)
    return jnp.pad(flat, (0, pad)).reshape(-1, lane)


def _unpack(packed, shapes):
    flat, out, off = packed.reshape(-1), [], 0
    for s in shapes:
        sz = math.prod(s)
        out.append(flat[off:off + sz].reshape(s))
        off += sz
    return out


def _pad_in_cols(wfull):
    zeros = jnp.zeros((wfull.shape[0], SHIFT_PAD - SHIFT_COLS), wfull.dtype)
    return jnp.concatenate([wfull[:, :SHIFT_COLS], zeros, wfull[:, SHIFT_COLS:]], axis=1)


def _unpad_in_cols(gfull):
    return jnp.concatenate([gfull[:, :SHIFT_COLS], gfull[:, SHIFT_PAD:]], axis=1)


def _block_diag2(wd):
    z = jnp.zeros_like(wd[0])
    return jnp.concatenate([jnp.concatenate([wd[0], z], axis=1), jnp.concatenate([z, wd[1]], axis=1)], axis=0)


def _unblock_diag2(g):
    return jnp.stack([g[0:64, 0:RW], g[64:128, RW:2 * RW]])


SMALL_SHARDED = ("w0", "w2", "a0", "a2", "g2", "conv_dw")
SMALL_REPL = ("mu_prev", "mu_next", "k_k", "k_a", "r_k", "lnx_g", "lnx_b", "conv_b", "conv_ln_g", "conv_ln_b",
              "ln1_g", "ln1_b", "ln2_g", "ln2_b", "ln3_g", "ln3_b")
BIG = ("ffn1_w_in", "ffn1_w_out", "w_in", "w_out", "ffn2_w_in", "ffn2_w_out")
WEIGHTS = ("ffn1_w_in", "ffn1_w_out", "w_in", "mu_prev", "mu_next", "w0", "w2", "a0", "a2", "g2", "k_k", "k_a",
           "r_k", "lnx_g", "lnx_b", "conv_dw", "conv_b", "conv_ln_g", "conv_ln_b", "w_out", "ffn2_w_in",
           "ffn2_w_out", "ln1_g", "ln1_b", "ln2_g", "ln2_b", "ln3_g", "ln3_b")


def _full_small(name, full):
    if name in ("w0", "a0"):
        return full.reshape(1, 2 * RW)
    if name in ("w2", "a2"):
        return _block_diag2(full)
    if name == "g2":
        return jnp.pad(full, ((0, 256 - GATE_LORA), (0, 0)))
    if name == "conv_dw":
        return jnp.pad(full, ((0, 1), (0, 0)))
    if name in ("mu_prev", "mu_next"):
        return jnp.pad(full.reshape(1, SHIFT_COLS), ((0, 0), (0, SHIFT_PAD - SHIFT_COLS)))
    return full.reshape(1, -1)


def _grad_small(name, g):
    if name in ("w0", "a0"):
        return g.reshape(2, RW)
    if name in ("w2", "a2"):
        return _unblock_diag2(g)
    if name == "g2":
        return g[:GATE_LORA]
    if name == "conv_dw":
        return g[:CONV_K]
    if name in ("mu_prev", "mu_next"):
        return g[0, :SHIFT_COLS]
    if name == "r_k":
        return g.reshape(N_HEADS, HEAD)
    return g.reshape(-1)


def kernel(x, ffn1_w_in, ffn1_w_out, w_in, mu_prev, mu_next, w0, w2, a0, a2, g2, k_k, k_a, r_k, lnx_g, lnx_b, conv_dw, conv_b, conv_ln_g, conv_ln_b, w_out, ffn2_w_in, ffn2_w_out, ln1_g, ln1_b, ln2_g, ln2_b, ln3_g, ln3_b, loss_target, m_ffn1_w_in, m_ffn1_w_out, m_w_in, m_mu_prev, m_mu_next, m_w0, m_w2, m_a0, m_a2, m_g2, m_k_k, m_k_a, m_r_k, m_lnx_g, m_lnx_b, m_conv_dw, m_conv_b, m_conv_ln_g, m_conv_ln_b, m_w_out, m_ffn2_w_in, m_ffn2_w_out, m_ln1_g, m_ln1_b, m_ln2_g, m_ln2_b, m_ln3_g, m_ln3_b, v_ffn1_w_in, v_ffn1_w_out, v_w_in, v_mu_prev, v_mu_next, v_w0, v_w2, v_a0, v_a2, v_g2, v_k_k, v_k_a, v_r_k, v_lnx_g, v_lnx_b, v_conv_dw, v_conv_b, v_conv_ln_g, v_conv_ln_b, v_w_out, v_ffn2_w_in, v_ffn2_w_out, v_ln1_g, v_ln1_b, v_ln2_g, v_ln2_b, v_ln3_g, v_ln3_b):
    args = dict(locals())
    wsh = {n: args[n][0] for n in WEIGHTS}
    msh = {n: args["m_" + n][0] for n in WEIGHTS}
    vsh = {n: args["v_" + n][0] for n in WEIGHTS}
    me = 4 * lax.axis_index("x") + 2 * lax.axis_index("y") + lax.axis_index("c")
    bf = {n: wsh[n].astype(BF16) for n in BIG}

    w = {"ffn1_w_in": _gather_two_level("gather_ffn1_w_in", bf["ffn1_w_in"])}
    for n in SMALL_REPL:
        w[n] = _full_small(n, wsh[n])
    small_shapes = [wsh[n].shape for n in SMALL_SHARDED]

    def finish1(got):
        f1_out, w_in_g, small = got
        cols = zip(*[_unpack(small[dv], small_shapes) for dv in range(N_DEV)])
        out = {n: _full_small(n, jnp.concatenate(s, axis=-1)) for n, s in zip(SMALL_SHARDED, cols)}
        out["ffn1_w_out"] = f1_out.reshape(D_FF, D_MODEL)
        out["w_in"] = _pad_in_cols(w_in_g.transpose(1, 0, 2).reshape(D_MODEL, IN_COLS))
        return out

    def finish2(got):
        w_out_g, f2_in, f2_out = got
        return {"w_out": w_out_g.reshape(D_MODEL, D_MODEL), "ffn2_w_in": f2_in,
                "ffn2_w_out": f2_out.reshape(D_FF, D_MODEL)}

    ex = {"g1": ([bf["ffn1_w_out"], bf["w_in"], _pack([wsh[n] for n in SMALL_SHARDED])], finish1),
          "g2": ([bf["w_out"], bf["ffn2_w_in"], bf["ffn2_w_out"]], finish2)}
    loss_part, grad_x, gr, parts = _local_step(x, loss_target, w, ex)
    loss = lax.psum(loss_part, ("x", "y", "c"))

    out = {n: _adam("adam_" + n, parts[n], wsh[n], msh[n], vsh[n]) for n in BIG}
    small_names = SMALL_SHARDED + SMALL_REPL
    full_shapes = [_grad_small(n, gr[n]).shape for n in small_names]
    summed = _unpack(_sum8("sum_small_grads", parts["small"]), full_shapes)
    mine = []
    for n, g in zip(small_names, summed):
        if n in SMALL_SHARDED:
            g = lax.dynamic_slice_in_dim(g, me * HEAD, HEAD, axis=g.ndim - 1)
        mine.append(g)
    shapes = [g.shape for g in mine]
    d_s, m_s, v_s = _adam_small("adam_small", _pack(mine), _pack([wsh[n] for n in small_names]),
                                _pack([msh[n] for n in small_names]), _pack([vsh[n] for n in small_names]))
    for n, g, dl, mn, vn in zip(small_names, mine, _unpack(d_s, shapes), _unpack(m_s, shapes), _unpack(v_s, shapes)):
        out[n] = (g, dl, mn, vn)

    res = [loss, grad_x]
    for k in range(4):
        res += [out[n][k][None] for n in WEIGHTS]
    return tuple(res)
```

```python
import functools
import math

import jax
import jax.numpy as jnp
from jax import lax
from jax.experimental import pallas as pl
from jax.experimental.pallas import tpu as pltpu

F32 = jnp.float32
BF16 = jnp.bfloat16

N_DEV = 8
D_MODEL = 1024
RW = 512
N_HEADS = 8
HEAD = 64
CW = 512
CONV_K = 31
D_FF = 2816
FF_BLK = 704
GATE_LORA = 160
SHIFT_COLS = 1952
SHIFT_PAD = 2048
IN_COLS = 2976
IN_PAD = 3072
LN_EPS = 1e-5
GN_EPS = 64e-5
NORM_EPS = 1e-12
ALPHA = 2.0 ** 0.25
DECAY_SCALE = math.exp(-0.5)
CHUNK = 64
ADAM_LR, ADAM_B1, ADAM_B2, ADAM_EPS, ADAM_WD, ADAM_STEP = 0.001, 0.9, 0.999, 1e-8, 0.01, 10
VMEM_LIMIT = 56 * 1024 * 1024

_DN = {"nn": (((1,), (0,)), ((), ())), "nt": (((1,), (1,)), ((), ())), "tn": (((0,), (0,)), ((), ()))}


def _params():
    return pltpu.CompilerParams(vmem_limit_bytes=VMEM_LIMIT)


def _dot(a, b, dims="nn"):
    return lax.dot_general(a, b, _DN[dims], preferred_element_type=F32)


def _split(x):
    hi = x.astype(BF16)
    return hi, (x - hi.astype(F32)).astype(BF16)


def _dot3_impl(a, b, dims):
    ah, al = _split(a)
    bh, bl = _split(b)
    ka, kb = _DN[dims][0][0][0], _DN[dims][0][1][0]
    return _dot(jnp.concatenate([ah, ah, al], axis=ka), jnp.concatenate([bh, bl, bh], axis=kb), dims)


@functools.partial(jax.custom_vjp, nondiff_argnums=(2,))
def _dot3(a, b, dims="nn"):
    return _dot3_impl(a, b, dims)


def _dot3_fwd(a, b, dims):
    return _dot3_impl(a, b, dims), (a, b)


def _dot3_bwd(dims, res, ct):
    a, b = res
    if dims == "nn":
        return _dot3_impl(ct, b, "nt"), _dot3_impl(a, ct, "tn")
    if dims == "nt":
        return _dot3_impl(ct, b, "nn"), _dot3_impl(ct, a, "tn")
    return _dot3_impl(b, ct, "nt"), _dot3_impl(a, ct, "nn")


_dot3.defvjp(_dot3_fwd, _dot3_bwd)


def _dot1_impl(a, b, dims):
    return _dot(a.astype(BF16), b.astype(BF16), dims)


@functools.partial(jax.custom_vjp, nondiff_argnums=(2,))
def _dot1(a, b, dims="nn"):
    return _dot1_impl(a, b, dims)


def _dot1_bwd(dims, res, ct):
    a, b = res
    if dims == "nn":
        return _dot1_impl(ct, b, "nt"), _dot1_impl(a, ct, "tn")
    if dims == "nt":
        return _dot1_impl(ct, b, "nn"), _dot1_impl(ct, a, "tn")
    return _dot1_impl(b, ct, "nt"), _dot1_impl(a, ct, "nn")


_dot1.defvjp(lambda a, b, dims: (_dot1_impl(a, b, dims), (a, b)), _dot1_bwd)


def _dot1_two(a, b, dims="nn"):
    ax_a, ax_b = {"nn": (0, 1), "nt": (0, 0), "tn": (1, 1)}[dims]
    out = []
    for i in range(0, len(a), 2):
        r = _dot1(jnp.concatenate(a[i:i + 2], axis=ax_a), jnp.concatenate(b[i:i + 2], axis=ax_b), dims)
        m, n = r.shape[0] // 2, r.shape[1] // 2
        out += [r[:m, :n], r[m:, n:]]
    return out


def _tri_inv_impl(l, eye):
    steps = int(math.log2(CHUNK)) - 1
    m = l[0].shape[0]
    tm = [eye + x for x in l]
    lp = _dot1_two(l, l)
    for k in range(steps):
        if k < steps - 1:
            both = _dot1_two([jnp.concatenate([t, p], axis=0) for t, p in zip(tm, lp)], lp)
            tm = [t + b[:m] for t, b in zip(tm, both)]
            lp = [b[m:] for b in both]
        else:
            tm = [t + x for t, x in zip(tm, _dot1_two(tm, lp))]
    return tm


@jax.custom_vjp
def _tri_inv(l, eye):
    return _tri_inv_impl(l, eye)


def _tri_inv_fwd(l, eye):
    tm = _tri_inv_impl(l, eye)
    return tm, (tm, eye)


def _tri_inv_bwd(res, ct):
    tm, eye = res
    return _dot1_two(_dot1_two(tm, ct, "tn"), tm, "nt"), jnp.zeros_like(eye)


_tri_inv.defvjp(_tri_inv_fwd, _tri_inv_bwd)


@jax.custom_vjp
def _tri_inv_known(l, tm):
    return tm


_tri_inv_known.defvjp(lambda l, tm: (tm, tm),
                      lambda tm, ct: (_dot1_two(_dot1_two(tm, ct, "tn"), tm, "nt"), [jnp.zeros_like(t) for t in tm]))


def _ones_impl(x, g3):
    x1 = x.astype(BF16)
    r1 = x - x1.astype(F32)
    x2 = r1.astype(BF16)
    x3 = (r1 - x2.astype(F32)).astype(BF16)
    return _dot(jnp.concatenate([x1, x2, x3], axis=1), g3)


@jax.custom_vjp
def _head_sum(x, g3):
    return _ones_impl(x, g3)


_head_sum.defvjp(lambda x, g3: (_ones_impl(x, g3), g3), lambda g3, ct: (_ones_impl(ct, g3), jnp.zeros_like(g3)))


def _prefix_sum(x):
    row = lax.broadcasted_iota(jnp.int32, x.shape, 0)
    sh = 1
    while sh < x.shape[0]:
        x = x + jnp.where(row >= sh, pltpu.roll(x, sh, 0), 0.0)
        sh *= 2
    return x


def _dir_cumsum_impl(x, sgn):
    pre = _prefix_sum(x)
    return jnp.where(sgn > 0.0, pre, jnp.sum(x, axis=0, keepdims=True) - pre + x)


@jax.custom_vjp
def _dir_cumsum(x, sgn):
    return _dir_cumsum_impl(x, sgn)


_dir_cumsum.defvjp(lambda x, sgn: (_dir_cumsum_impl(x, sgn), sgn),
                   lambda sgn, ct: (_dir_cumsum_impl(ct, -sgn), jnp.zeros_like(sgn)))


def _sigmoid(x):
    return 1.0 / (1.0 + jnp.exp(-x))


def _mesh_pos():
    return lax.axis_index("x"), lax.axis_index("y"), lax.axis_index("c")


def _peer(pos, q):
    x, y, c = pos
    return (1 - x if q & 4 else x, 1 - y if q & 2 else y, 1 - c if q & 1 else c)


def _linear(pos):
    return 4 * pos[0] + 2 * pos[1] + pos[2]


def _exchange_copies(x_refs, o_refs, send_sems, recv_sems, local_sems, gather):
    pos = _mesh_pos()
    me = _linear(pos)
    starts, wait_recv, wait_send, wait_local = [], [], [], []
    for t in range(len(x_refs)):
        src = x_refs[t] if gather else x_refs[t].at[me]
        cp = pltpu.make_async_copy(src, o_refs[t].at[me], local_sems.at[t])
        starts.append(cp.start)
        wait_local.append(cp.wait)
    for q in range(1, N_DEV):
        peer = _peer(pos, q)
        for t in range(len(x_refs)):
            src = x_refs[t] if gather else x_refs[t].at[_linear(peer)]
            sems = dict(send_sem=send_sems.at[t, q - 1], recv_sem=recv_sems.at[t, q - 1],
                        device_id=peer, device_id_type=pl.DeviceIdType.MESH)
            send = pltpu.make_async_remote_copy(src_ref=src, dst_ref=o_refs[t].at[me], **sems)
            recv = pltpu.make_async_remote_copy(src_ref=src, dst_ref=o_refs[t].at[_linear(peer)], **sems)
            starts.append(send.start)
            wait_recv.append(recv.wait_recv)
            wait_send.append(send.wait_send)
    return starts, wait_recv + wait_send + wait_local


def _exchange_shapes(xs, gather):
    return [jax.ShapeDtypeStruct((N_DEV,) + (x.shape if gather else x.shape[1:]), x.dtype) for x in xs]


def _exchange_sems(nt):
    return [pltpu.SemaphoreType.DMA((nt, N_DEV - 1)), pltpu.SemaphoreType.DMA((nt, N_DEV - 1)),
            pltpu.SemaphoreType.DMA((nt,))]


def _gather_two_level(name, x):
    def body(x_ref, out_ref, send_sems, recv_sems, local_sem):
        px, py, pc = _mesh_pos()
        me, sibling = (px, py, pc), (px, py, 1 - pc)
        chips = [(1 - px, py), (px, 1 - py), (1 - px, 1 - py)]

        def slot(pos):
            return out_ref.at[_linear(pos)]

        def copy(k, block, to, src=None):
            return pltpu.make_async_remote_copy(
                src_ref=slot(block) if src is None else src, dst_ref=slot(block), send_sem=send_sems.at[k],
                recv_sem=recv_sems.at[k], device_id=to, device_id_type=pl.DeviceIdType.MESH)

        mine = pltpu.make_async_copy(x_ref, slot(me), local_sem)
        mine.start()
        first = [copy(0, me, sibling, src=x_ref)]
        first += [copy(1 + j, me, (*chip, pc), src=x_ref) for j, chip in enumerate(chips)]
        for cp in first:
            cp.start()
        passed = [copy(4 + j, (*chip, pc), sibling) for j, chip in enumerate(chips)]
        for j, chip in enumerate(chips):
            copy(1 + j, (*chip, pc), me).wait_recv()
            passed[j].start()
        copy(0, sibling, me).wait_recv()
        for j, chip in enumerate(chips):
            copy(4 + j, (*chip, 1 - pc), me).wait_recv()
        for cp in first + passed:
            cp.wait_send()
        mine.wait()

    any_spec = pl.BlockSpec(memory_space=pl.ANY)
    return pl.pallas_call(
        body, name=name, in_specs=[any_spec], out_specs=any_spec,
        out_shape=jax.ShapeDtypeStruct((N_DEV,) + x.shape, x.dtype),
        scratch_shapes=[pltpu.SemaphoreType.DMA((N_DEV - 1,)), pltpu.SemaphoreType.DMA((N_DEV - 1,)),
                        pltpu.SemaphoreType.DMA])(x)


def _call(name, body, grid, ins, in_specs, out_specs, out_shape, scratch=(), exch=None):
    if exch is None:
        return pl.pallas_call(body, name=name, grid=grid, in_specs=in_specs, out_specs=out_specs,
                              out_shape=out_shape, scratch_shapes=list(scratch), compiler_params=_params())(*ins)
    xs, gather = exch
    single = not isinstance(out_shape, (list, tuple))
    o_specs = [out_specs] if single else list(out_specs)
    o_shape = [out_shape] if single else list(out_shape)
    n_in, n_out, n_x, n_scr = len(ins), len(o_shape), len(xs), len(scratch)

    def wrapped(*refs):
        in_refs = refs[:n_in]
        x_refs = refs[n_in:n_in + n_x]
        out_refs = refs[n_in + n_x:n_in + n_x + n_out]
        got_refs = refs[n_in + n_x + n_out:n_in + 2 * n_x + n_out]
        rest = refs[n_in + 2 * n_x + n_out:]
        starts, waits = _exchange_copies(x_refs, got_refs, *rest[n_scr:], gather)
        ids = [pl.program_id(i) for i in range(len(grid))]
        first = functools.reduce(lambda p, q: p & q, [i == 0 for i in ids])
        last = functools.reduce(lambda p, q: p & q, [i == g - 1 for i, g in zip(ids, grid)])

        @pl.when(first)
        def _():
            for f in starts:
                f()

        body(*in_refs, *out_refs, *rest[:n_scr])

        @pl.when(last)
        def _():
            for f in waits:
                f()

    any_spec = pl.BlockSpec(memory_space=pl.ANY)
    outs = pl.pallas_call(
        wrapped, name=name, grid=grid, in_specs=list(in_specs) + [any_spec] * n_x,
        out_specs=o_specs + [any_spec] * n_x, out_shape=o_shape + _exchange_shapes(xs, gather),
        scratch_shapes=list(scratch) + _exchange_sems(n_x), compiler_params=_params())(*ins, *xs)
    res = outs[:n_out]
    return (res[0] if single else res), outs[n_out:]


def _mm_call(name, dims, grid, red_axis, ins, in_specs, out_shape, out_spec, acc_shape,
             scale=1.0, add_scale=None, exch=None):
    nred = grid[red_axis]

    def body(*refs):
        if add_scale is None:
            a_ref, b_ref, o_ref, acc_ref = refs
            add_ref = None
        else:
            a_ref, b_ref, add_ref, o_ref, acc_ref = refs
        k = pl.program_id(red_axis)

        @pl.when(k == 0)
        def _():
            acc_ref[...] = jnp.zeros_like(acc_ref)

        acc_ref[...] += _dot(a_ref[...].astype(BF16), b_ref[...].astype(BF16), dims)

        @pl.when(k == nred - 1)
        def _():
            r = acc_ref[...]
            if scale != 1.0:
                r = r * scale
            if add_ref is not None:
                r = r + add_scale * add_ref[...].astype(F32)
            o_ref[...] = r.astype(o_ref.dtype)

    return _call(name, body, grid, ins, in_specs, out_spec, out_shape, [pltpu.VMEM(acc_shape, F32)], exch)


def _tile(dim, cap):
    t = min(dim, cap)
    while dim % t or t % 128:
        t -= 128
        assert t > 0, (dim, cap)
    return t


def _mm(name, a, b, dims, out_dtype, scale=1.0, add=None, add_scale=None, tm=512, tn=1024, tk=1024):
    if dims == "tn":
        kd, m = a.shape
        n = b.shape[1]
    else:
        m, kd = a.shape
        n = b.shape[1] if dims == "nn" else b.shape[0]
    tm, tn, tk = _tile(m, tm), _tile(n, tn), _tile(kd, tk)
    a_spec = (pl.BlockSpec((tk, tm), lambda i, j, k: (k, i)) if dims == "tn"
              else pl.BlockSpec((tm, tk), lambda i, j, k: (i, k)))
    b_spec = (pl.BlockSpec((tn, tk), lambda i, j, k: (j, k)) if dims == "nt"
              else pl.BlockSpec((tk, tn), lambda i, j, k: (k, j)))
    o_spec = pl.BlockSpec((tm, tn), lambda i, j, k: (i, j))
    ins, specs = [a, b], [a_spec, b_spec]
    if add is not None:
        ins.append(add)
        specs.append(o_spec)
    return _mm_call(name, dims, (m // tm, n // tn, kd // tk), 2, ins, specs,
                    jax.ShapeDtypeStruct((m, n), out_dtype), o_spec, (tm, tn),
                    scale=scale, add_scale=add_scale if add is not None else None)


def _ffn_in(name, x, wg, tm=512, exch=None):
    n = x.shape[0]
    nj = N_DEV // 2

    def body(x_ref, wgate_ref, wup_ref, hg_ref, hu_ref, act_ref):
        xb = x_ref[...].astype(BF16)
        g = _dot(xb, wgate_ref[...])
        u = _dot(xb, wup_ref[...])
        hg_ref[...] = g.astype(BF16)
        hu_ref[...] = u.astype(BF16)
        act_ref[...] = (g * _sigmoid(g) * u).astype(BF16)

    blk = pl.BlockSpec((None, tm, FF_BLK), lambda j, i: (j, i, 0))
    shp = jax.ShapeDtypeStruct((nj, n, FF_BLK), BF16)
    return _call(name, body, (nj, n // tm), [x, wg, wg],
                 [pl.BlockSpec((tm, D_MODEL), lambda j, i: (i, 0)),
                  pl.BlockSpec((None, D_MODEL, FF_BLK), lambda j, i: (j, 0, 0)),
                  pl.BlockSpec((None, D_MODEL, FF_BLK), lambda j, i: (j + nj, 0, 0))],
                 [blk, blk, blk], [shp, shp, shp], exch=exch)


def _ffn_out_bwd(name, dz, wout, hg, hu, act, tm=512, exch=None):
    n = dz.shape[0]
    nj, ni = N_DEV // 2, n // tm

    def body(dz_ref, w_ref, hg_ref, hu_ref, act_ref, dhg_ref, dhu_ref, dw_ref, acc_ref):
        i = pl.program_id(1)
        dzb = dz_ref[...].astype(BF16)
        dact = 0.5 * _dot(dzb, w_ref[...], "nt")
        g = hg_ref[...].astype(F32)
        u = hu_ref[...].astype(F32)
        s = _sigmoid(g)
        dhg_ref[...] = (dact * u * (s * (1.0 + g * (1.0 - s)))).astype(BF16)
        dhu_ref[...] = (dact * (g * s)).astype(BF16)

        @pl.when(i == 0)
        def _():
            acc_ref[...] = jnp.zeros_like(acc_ref)

        acc_ref[...] += _dot(act_ref[...], dzb, "tn")

        @pl.when(i == ni - 1)
        def _():
            dw_ref[...] = (0.5 * acc_ref[...]).astype(dw_ref.dtype)

    blk = pl.BlockSpec((None, tm, FF_BLK), lambda j, i: (j, i, 0))
    wblk = pl.BlockSpec((FF_BLK, D_MODEL), lambda j, i: (j, 0))
    shp = jax.ShapeDtypeStruct((nj, n, FF_BLK), BF16)
    return _call(name, body, (nj, ni), [dz, wout, hg, hu, act],
                 [pl.BlockSpec((tm, D_MODEL), lambda j, i: (i, 0)), wblk, blk, blk, blk],
                 [blk, blk, wblk], [shp, shp, jax.ShapeDtypeStruct((D_FF, D_MODEL), BF16)],
                 [pltpu.VMEM((FF_BLK, D_MODEL), F32)], exch=exch)


def _mm_ln(name, a, b, xres, g, beta, c, tgt=None, a_blocked=False, tm=512, tk=512):
    if a_blocked:
        nk, n, kb = a.shape
        a_spec = pl.BlockSpec((None, tm, kb), lambda i, k: (k, i, 0))
    else:
        n, kd = a.shape
        kb = _tile(kd, tk)
        nk = kd // kb
        a_spec = pl.BlockSpec((tm, kb), lambda i, k: (i, k))
    d = b.shape[1]
    with_loss = tgt is not None

    def body(*refs):
        if with_loss:
            a_ref, b_ref, x_ref, g_ref, be_ref, t_ref, o_ref, z_ref, l_ref, acc_ref = refs
        else:
            a_ref, b_ref, x_ref, g_ref, be_ref, o_ref, ob_ref, z_ref, acc_ref = refs
        i, k = pl.program_id(0), pl.program_id(1)

        @pl.when(k == 0)
        def _():
            acc_ref[...] = jnp.zeros_like(acc_ref)

        acc_ref[...] += _dot(a_ref[...].astype(BF16), b_ref[...].astype(BF16))

        @pl.when(k == nk - 1)
        def _():
            z = ALPHA * x_ref[...] + c * acc_ref[...]
            z_ref[...] = z
            mu = jnp.mean(z, axis=-1, keepdims=True)
            zc = z - mu
            var = jnp.mean(zc * zc, axis=-1, keepdims=True)
            y = zc * lax.rsqrt(var + LN_EPS) * g_ref[...] + be_ref[...]
            if with_loss:
                err = y - t_ref[...]
                o_ref[...] = err * (1.0 / d)
                part = 0.5 * jnp.sum(jnp.sum(err * err, axis=-1, keepdims=True), axis=0, keepdims=True) * (1.0 / d)

                @pl.when(i == 0)
                def _():
                    l_ref[...] = jnp.zeros_like(l_ref)

                l_ref[...] += jnp.broadcast_to(part, l_ref.shape)
            else:
                o_ref[...] = y
                ob_ref[...] = y.astype(BF16)

    row = pl.BlockSpec((tm, d), lambda i, k: (i, 0))
    vec = pl.BlockSpec((1, d), lambda i, k: (0, 0))
    ins = [a, b, xres, g, beta]
    in_specs = [a_spec, pl.BlockSpec((kb, d), lambda i, k: (k, 0)), row, vec, vec]
    out_specs = [row, row]
    out_shape = [jax.ShapeDtypeStruct((n, d), F32), jax.ShapeDtypeStruct((n, d), F32)]
    if not with_loss:
        out_specs.insert(1, row)
        out_shape.insert(1, jax.ShapeDtypeStruct((n, d), BF16))
    if with_loss:
        ins.append(tgt)
        in_specs.append(row)
        out_specs.append(pl.BlockSpec((1, 128), lambda i, k: (0, 0)))
        out_shape.append(jax.ShapeDtypeStruct((1, 128), F32))
    return _call(name, body, (n // tm, nk), ins, in_specs, out_specs, out_shape, [pltpu.VMEM((tm, d), F32)])


def _rowwise(name, fn, rows, params, out_rows, out_accs, tm=256, exch=None):
    specs, ins = [], []
    for r in rows:
        arr, w, cb = r if isinstance(r, tuple) else (r, r.shape[1], 0)
        ins.append(arr)
        specs.append(pl.BlockSpec((tm, w), functools.partial(lambda i, cb: (i, cb), cb=cb)))
    n = ins[0].shape[0]
    for p in params:
        ins.append(p)
        specs.append(pl.BlockSpec(p.shape, lambda i: (0, 0)))
    n_in, n_or = len(ins), len(out_rows)

    def body(*refs):
        outs = fn(*[r[...] for r in refs[:n_in]])
        o_refs = refs[n_in:]
        for o_ref, o in zip(o_refs[:n_or], outs[:n_or]):
            o_ref[...] = o.astype(o_ref.dtype)
        if out_accs:
            @pl.when(pl.program_id(0) == 0)
            def _():
                for a_ref in o_refs[n_or:]:
                    a_ref[...] = jnp.zeros_like(a_ref)

            for a_ref, a in zip(o_refs[n_or:], outs[n_or:]):
                a_ref[...] += a.astype(F32)

    out_specs = [pl.BlockSpec((tm, w), lambda i: (i, 0)) for w, _ in out_rows]
    out_specs += [pl.BlockSpec(s, lambda i: (0, 0)) for s in out_accs]
    out_shape = [jax.ShapeDtypeStruct((n, w), dt) for w, dt in out_rows]
    out_shape += [jax.ShapeDtypeStruct(s, F32) for s in out_accs]
    return _call(name, body, (n // tm,), ins, specs, out_specs, out_shape, exch=exch)


def _vjp_of(fn, n_in):
    def g(*args):
        ins, cts = args[:n_in], args[n_in:]
        outs, pull = jax.vjp(fn, *ins)
        return pull(tuple(c.astype(o.dtype) for c, o in zip(cts, outs)))
    return g


def _ln_bwd(name, z, g, ct):
    def fn(zt, ct_, gt):
        mu = jnp.mean(zt, axis=-1, keepdims=True)
        zc = zt - mu
        rstd = lax.rsqrt(jnp.mean(zc * zc, axis=-1, keepdims=True) + LN_EPS)
        xh = zc * rstd
        dxh = ct_ * gt
        dz = rstd * (dxh - jnp.mean(dxh, axis=-1, keepdims=True)
                     - xh * jnp.mean(dxh * xh, axis=-1, keepdims=True))
        return dz, dz, jnp.sum(ct_ * xh, axis=0, keepdims=True), jnp.sum(ct_, axis=0, keepdims=True)

    d = z.shape[1]
    return _rowwise(name, fn, [z, ct], [g], [(d, F32), (d, BF16)], [(1, d), (1, d)])


def _shift(name, src, mu_a, mu_b, q=None, tt=256, out_dtype=F32):
    bsz, t, _ = src.shape
    nt, r8, w = t // tt, tt // 8, SHIFT_PAD
    with_q = q is not None

    def body(cur_ref, prev_ref, next_ref, *rest):
        b, i = pl.program_id(0), pl.program_id(1)
        cur = cur_ref[...]
        prow = jnp.where(i > 0, prev_ref[7:8, :], 0.0)
        nrow = jnp.where(i < nt - 1, next_ref[0:1, :], 0.0)
        rid = lax.broadcasted_iota(jnp.int32, cur.shape, 0)
        dprev = jnp.where(rid == 0, prow, pltpu.roll(cur, 1, 0)) - cur
        dnext = jnp.where(rid == tt - 1, nrow, pltpu.roll(cur, tt - 1, 0)) - cur
        if with_q:
            q_ref, da_ref, db_ref = rest

            @pl.when((b == 0) & (i == 0))
            def _():
                da_ref[...] = jnp.zeros_like(da_ref)
                db_ref[...] = jnp.zeros_like(db_ref)

            qv = q_ref[...]
            da_ref[...] += jnp.sum(qv * dprev, axis=0, keepdims=True)
            db_ref[...] += jnp.sum(qv * dnext, axis=0, keepdims=True)
        else:
            ma_ref, mb_ref, o_ref = rest
            o_ref[...] = (cur + ma_ref[...] * dprev + mb_ref[...] * dnext).astype(o_ref.dtype)

    cur_spec = pl.BlockSpec((None, tt, w), lambda b, i: (b, i, 0))
    in_specs = [cur_spec,
                pl.BlockSpec((None, 8, w), lambda b, i: (b, jnp.maximum(i * r8 - 1, 0), 0)),
                pl.BlockSpec((None, 8, w), lambda b, i: (b, jnp.minimum((i + 1) * r8, t // 8 - 1), 0))]
    vec = pl.BlockSpec((1, w), lambda b, i: (0, 0))
    if with_q:
        return _call(name, body, (bsz, nt), [src, src, src, q], in_specs + [cur_spec], [vec, vec],
                     [jax.ShapeDtypeStruct((1, w), F32)] * 2)
    return _call(name, body, (bsz, nt), [src, src, src, mu_a, mu_b], in_specs + [vec, vec], cur_spec,
                 jax.ShapeDtypeStruct((bsz, t, w), out_dtype))


CONV_BLK = 128


def _halo_specs(t, tt, w):
    r16 = tt // 16
    return [pl.BlockSpec((None, tt, w), lambda b, i: (b, i, 0)),
            pl.BlockSpec((None, 16, w), lambda b, i: (b, jnp.maximum(i * r16 - 1, 0), 0)),
            pl.BlockSpec((None, 16, w), lambda b, i: (b, jnp.minimum((i + 1) * r16, t // 16 - 1), 0))]


def _fill_pad(pad_ref, cur_ref, prev_ref, next_ref, i, nt, tt):
    pad_ref[0:16, :] = jnp.where(i > 0, prev_ref[...], 0.0)
    pad_ref[16:16 + tt, :] = cur_ref[...]
    pad_ref[16 + tt:32 + tt, :] = jnp.where(i < nt - 1, next_ref[...], 0.0)


def _dwconv(name, u, dw32, bias, flip, tt=512):
    bsz, t, w = u.shape
    tt = min(tt, t)
    nt = t // tt

    def body(cur_ref, prev_ref, next_ref, dw_ref, b_ref, o_ref, pad_ref):
        i = pl.program_id(1)
        _fill_pad(pad_ref, cur_ref, prev_ref, next_ref, i, nt, tt)
        for r0 in range(0, tt, CONV_BLK):
            for cs in (slice(c0, c0 + CONV_BLK) for c0 in range(0, w, CONV_BLK)):
                acc = jnp.broadcast_to(b_ref[:, cs], (CONV_BLK, CONV_BLK))
                for k in range(CONV_K):
                    kk = CONV_K - 1 - k if flip else k
                    acc = acc + pad_ref[pl.ds(r0 + 1 + k, CONV_BLK), cs] * dw_ref[kk:kk + 1, cs]
                o_ref[r0:r0 + CONV_BLK, cs] = acc

    return _call(name, body, (bsz, nt), [u, u, u, dw32, bias],
                 _halo_specs(t, tt, w) + [pl.BlockSpec((32, w), lambda b, i: (0, 0)),
                                          pl.BlockSpec((1, w), lambda b, i: (0, 0))],
                 pl.BlockSpec((None, tt, w), lambda b, i: (b, i, 0)), jax.ShapeDtypeStruct((bsz, t, w), F32),
                 [pltpu.VMEM((tt + 32, w), F32)])


def _dwconv_dw(name, u, dc, tt=512):
    bsz, t, w = u.shape
    tt = min(tt, t)
    nt = t // tt

    def body(cur_ref, prev_ref, next_ref, dc_ref, ddw_ref, db_ref, pad_ref):
        b, i = pl.program_id(0), pl.program_id(1)
        _fill_pad(pad_ref, cur_ref, prev_ref, next_ref, i, nt, tt)

        @pl.when((b == 0) & (i == 0))
        def _():
            ddw_ref[...] = jnp.zeros_like(ddw_ref)
            db_ref[...] = jnp.zeros_like(db_ref)

        dcv = dc_ref[...]
        db_ref[...] += jnp.sum(dcv, axis=0, keepdims=True)
        for k in range(CONV_K):
            ddw_ref[k:k + 1, :] += jnp.sum(dcv * pad_ref[pl.ds(1 + k, tt), :], axis=0, keepdims=True)

    return _call(name, body, (bsz, nt), [u, u, u, dc],
                 _halo_specs(t, tt, w) + [pl.BlockSpec((None, tt, w), lambda b, i: (b, i, 0))],
                 [pl.BlockSpec((32, w), lambda b, i: (0, 0)), pl.BlockSpec((1, w), lambda b, i: (0, 0))],
                 [jax.ShapeDtypeStruct((32, w), F32), jax.ShapeDtypeStruct((1, w), F32)],
                 [pltpu.VMEM((tt + 32, w), F32)])


PAIR = 2 * HEAD
N_PAIRS = RW // PAIR


def _chunk_pairs(s, r, lw, k, v, kk, a, sgn, tm_known=None):
    n, m = CHUNK, 2 * CHUNK
    in_a = lax.broadcasted_iota(jnp.int32, (n, PAIR), 1) < HEAD

    def stack2(z):
        return jnp.concatenate([jnp.where(in_a, z, 0.0), jnp.where(in_a, 0.0, z)], axis=0)

    def each(f, *lists):
        return [f(*z) for z in zip(*lists)]

    sgn_f = sgn.astype(F32)
    row2 = lax.broadcasted_iota(jnp.int32, (m, m), 0)
    col2 = lax.broadcasted_iota(jnp.int32, (m, m), 1)
    same = (row2 >= n) == (col2 >= n)
    dlt = ((row2 & (n - 1)) - (col2 & (n - 1))) * sgn
    incl, strict = same & (dlt >= 0), same & (dlt > 0)
    eye = jnp.where(row2 == col2, 1.0, 0.0)

    cum = each(lambda lw_: _dir_cumsum(lw_, sgn_f), lw)
    tot = each(lambda lw_: jnp.sum(lw_, axis=0, keepdims=True), lw)
    e_neg = each(lambda c_: jnp.exp(-c_), cum)
    e_rest = each(lambda t_, c_: jnp.exp(t_ - c_), tot, cum)
    beta = each(lambda kk_, a_: kk_ * a_, kk, a)
    lhs = each(lambda kk_, c_, lw_, r_: jnp.concatenate(
        [stack2(-kk_ * jnp.exp(c_ - lw_)), stack2(r_ * jnp.exp(c_))], axis=0), kk, cum, lw, r)
    rhs = each(lambda b_, k_, e_: jnp.concatenate([stack2(b_ * e_), stack2(k_ * e_)], axis=0), beta, k, e_neg)
    sc = each(lambda l_, r_: _dot3(l_, r_, "nt"), lhs, rhs)
    l_ab = each(lambda sc_: jnp.where(strict, sc_[0:m, 0:m], 0.0), sc)
    l_ak = each(lambda sc_: jnp.where(strict, sc_[0:m, m:2 * m], 0.0), sc)
    m_r = each(lambda sc_: jnp.where(jnp.concatenate([incl, incl], axis=1), sc_[m:2 * m, :], 0.0), sc)
    tm = _tri_inv(l_ab, eye) if tm_known is None else _tri_inv_known(l_ab, tm_known)
    z = _dot1_two(lhs, s, "nt")
    v2 = each(stack2, v)
    u2 = _dot1_two(tm, each(lambda z_, lv_: z_[0:m] + lv_, z, _dot1_two(l_ak, v2)))
    uv = each(lambda u_, v_: jnp.concatenate([u_, v_], axis=0), u2, v2)
    y2 = each(lambda z_, mu_: z_[m:2 * m] + mu_, z, _dot1_two(m_r, uv))
    bk = each(lambda b_, k_, e_: jnp.concatenate([stack2(b_ * e_), stack2(k_ * e_)], axis=0), beta, k, e_rest)
    s_new = each(lambda s_, t_, d_: s_ * jnp.exp(t_) + d_, s, tot, _dot1_two(uv, bk, "tn"))
    return each(lambda y_: y_[0:n] + y_[n:m], y2), s_new, tm


SCAN_SEQS = 4
N_CHAINS = SCAN_SEQS * N_PAIRS


def _pair_tiles(ref):
    return [ref[q, :, p * PAIR:(p + 1) * PAIR] for q in range(SCAN_SEQS) for p in range(N_PAIRS)]


def _store_tiles(ref, tiles):
    for q in range(SCAN_SEQS):
        for p in range(N_PAIRS):
            ref[q, :, p * PAIR:(p + 1) * PAIR] = tiles[q * N_PAIRS + p]


def _scan_specs(order):
    shared = pl.BlockSpec((SCAN_SEQS, CHUNK, RW), lambda d, b, c: (b, order(d, c), 0))
    per_dir = pl.BlockSpec((SCAN_SEQS, CHUNK, RW), lambda d, b, c: (b, order(d, c), d))
    state = pl.BlockSpec((None, SCAN_SEQS, None, N_PAIRS, PAIR, PAIR), lambda d, b, c: (d, b, order(d, c), 0, 0, 0))
    return shared, per_dir, state


def _scan_fwd(r, v, kk, lw, kd, a, bsz, exch=None):
    n = r.shape[0]
    t = n // bsz
    nc = t // CHUNK

    def order(d, c):
        return c + d * (nc - 1 - 2 * c)

    def body(r_ref, v_ref, kk_ref, lw_ref, kd_ref, a_ref, y_ref, s0_ref, tm_ref, s_ref):
        d, c = pl.program_id(0), pl.program_id(2)

        @pl.when(c == 0)
        def _():
            s_ref[...] = jnp.zeros_like(s_ref)

        s = [s_ref[i] for i in range(N_CHAINS)]
        y, s_new, tm = _chunk_pairs(s, *[_pair_tiles(ref) for ref in (r_ref, lw_ref, kd_ref, v_ref, kk_ref, a_ref)],
                                    1 - 2 * d)
        _store_tiles(y_ref, y)
        for i in range(N_CHAINS):
            s0_ref[i // N_PAIRS, i % N_PAIRS] = s[i]
            tm_ref[i // N_PAIRS, i % N_PAIRS] = tm[i].astype(BF16)
            s_ref[i] = s_new[i]

    shared, per_dir, state = _scan_specs(order)
    seq = lambda z: z.reshape(bsz, t, z.shape[1])
    res = _call("scan_fwd", body, (2, bsz // SCAN_SEQS, nc), [seq(z) for z in (r, v, kk, lw, kd, a)],
                [shared, shared, shared, per_dir, per_dir, per_dir], [per_dir, state, state],
                [jax.ShapeDtypeStruct((bsz, t, 2 * RW), F32),
                 jax.ShapeDtypeStruct((2, bsz, nc, N_PAIRS, PAIR, PAIR), F32),
                 jax.ShapeDtypeStruct((2, bsz, nc, N_PAIRS, PAIR, PAIR), BF16)],
                [pltpu.VMEM((N_CHAINS, PAIR, PAIR), F32)], exch)
    (y, s0, tm), got = res if exch else (res, None)
    y = y.reshape(n, 2 * RW)
    return ([y, s0, tm], got) if exch else [y, s0, tm]


def _scan_bwd(r, v, kk, lw, kd, a, s0, tm, dy, bsz, exch=None):
    n = r.shape[0]
    t = n // bsz
    nc = t // CHUNK

    def order(d, c):
        cc = nc - 1 - c
        return cc + d * (nc - 1 - 2 * cc)

    def body(r_ref, v_ref, kk_ref, lw_ref, kd_ref, a_ref, dy_ref, s0_ref, tm_ref,
             dr_ref, dv_ref, dkk_ref, dlw_ref, dkd_ref, da_ref, ds_ref):
        d, c = pl.program_id(0), pl.program_id(2)

        @pl.when(c == 0)
        def _():
            ds_ref[...] = jnp.zeros_like(ds_ref)

        sgn = 1 - 2 * d
        tm_known = [tm_ref[i // N_PAIRS, i % N_PAIRS].astype(F32) for i in range(N_CHAINS)]
        _, pull = jax.vjp(lambda *ops: _chunk_pairs(*ops, sgn, tm_known)[:2],
                          [s0_ref[i // N_PAIRS, i % N_PAIRS] for i in range(N_CHAINS)],
                          *[_pair_tiles(ref) for ref in (r_ref, lw_ref, kd_ref, v_ref, kk_ref, a_ref)])
        grads = pull((_pair_tiles(dy_ref), [ds_ref[i] for i in range(N_CHAINS)]))
        for i in range(N_CHAINS):
            ds_ref[i] = grads[0][i]
        for o_ref, gx in zip((dr_ref, dlw_ref, dkd_ref, dv_ref, dkk_ref, da_ref), grads[1:]):
            _store_tiles(o_ref, gx)

    shared, per_dir, state = _scan_specs(order)
    shp = jax.ShapeDtypeStruct((bsz, t, 2 * RW), F32)
    seq = lambda z: z.reshape(bsz, t, z.shape[1])
    res = _call("scan_bwd", body, (2, bsz // SCAN_SEQS, nc), [seq(z) for z in (r, v, kk, lw, kd, a, dy)] + [s0, tm],
                [shared, shared, shared, per_dir, per_dir, per_dir, per_dir, state, state],
                [per_dir] * 6, [shp] * 6, [pltpu.VMEM((N_CHAINS, PAIR, PAIR), F32)], exch)
    outs, got = res if exch else (res, None)
    outs = [z.reshape(n, 2 * RW) for z in outs]
    return (outs, got) if exch else outs


def _prep_fn(ps, w0, w2bd, a0, a2bd, g2p, k_k, k_a, hsum):
    head_sum = lambda z: _head_sum(z, hsum)
    r, k, v = ps[:, 0:RW], ps[:, RW:2 * RW], ps[:, 2 * RW:3 * RW]
    wd, ad, gd = ps[:, 1536:1664], ps[:, 1664:1792], ps[:, 1792:2048]
    logw = -DECAY_SCALE * _sigmoid(_dot1(jnp.tanh(wd), w2bd) + w0)
    a = _sigmoid(_dot1(ad, a2bd) + a0)
    g = _dot1(_sigmoid(gd), g2p)
    kkr = k * k_k
    kk = kkr / jnp.maximum(jnp.sqrt(head_sum(kkr * kkr)), NORM_EPS)
    k2 = jnp.concatenate([k, k], axis=1)
    ka2 = jnp.concatenate([k_a, k_a], axis=1)
    kd = k2 * (1.0 + (a - 1.0) * ka2)
    return r, v, kk, logw, a, kd, g


def _post_fn(y2, r, v, kd, g, lnx_g, lnx_b, r_k, hsum):
    head_sum = lambda z: _head_sum(z, hsum)
    y = y2[:, 0:RW] + y2[:, RW:2 * RW]
    mu = head_sum(y) * (1.0 / HEAD)
    yc = y - mu
    var = head_sum(yc * yc) * (1.0 / HEAD)
    yn = yc * lax.rsqrt(var + GN_EPS) * lnx_g + lnx_b
    bonus = head_sum(r * (kd[:, 0:RW] + kd[:, RW:2 * RW]) * r_k) * v
    return ((yn + bonus) * g,)


def _glu_fn(pa, pb):
    return (pa * _sigmoid(pb),)


def _conv_out_fn(cv, ln_g, ln_b):
    mu = jnp.mean(cv, axis=-1, keepdims=True)
    cc = cv - mu
    var = jnp.mean(cc * cc, axis=-1, keepdims=True)
    y = cc * lax.rsqrt(var + LN_EPS) * ln_g + ln_b
    return (y * _sigmoid(y),)


def _local_step(x, tgt, w, ex=None):
    bsz, t, d = x.shape
    n = bsz * t
    x2d, tgt2d = x.reshape(n, d), tgt.reshape(n, d)
    hsum = jnp.tile(jnp.kron(jnp.eye(N_HEADS, dtype=BF16), jnp.ones((HEAD, HEAD), BF16)), (3, 1))
    w = dict(w)
    parts = {} if ex else None

    def hosted(result, finish=None):
        if not ex:
            return result
        outs, got = result
        if finish is not None:
            w.update(finish(got))
        return outs

    xb = x2d.astype(BF16)
    hg1, hu1, act1 = hosted(_ffn_in("ffn1_in", xb, w["ffn1_w_in"], exch=(ex["g1"][0], True) if ex else None),
                            ex["g1"][1] if ex else None)
    x1, x1b, z1 = _mm_ln("ffn1_out_ln1", act1, w["ffn1_w_out"], x2d, w["ln1_g"], w["ln1_b"], 0.5, a_blocked=True)
    p = _mm("w_in_proj", x1b, w["w_in"], "nn", F32)
    p3 = p.reshape(bsz, t, IN_PAD)
    ps = _shift("shift_fwd", p3, w["mu_prev"], w["mu_next"]).reshape(n, SHIFT_PAD)
    prep_params = [w["w0"], w["w2"], w["a0"], w["a2"], w["g2"], w["k_k"], w["k_a"], hsum]
    r, v, kk, logw, a, kd, g = _rowwise(
        "rwkv_prep", _prep_fn, [ps], prep_params,
        [(RW, F32), (RW, F32), (RW, F32), (2 * RW, F32), (2 * RW, F32), (2 * RW, F32), (RW, F32)], [])
    y2, s0, tm = hosted(_scan_fwd(r, v, kk, logw, kd, a, bsz, exch=(ex["g2"][0], True) if ex else None),
                    ex["g2"][1] if ex else None)
    post_params = [w["lnx_g"], w["lnx_b"], w["r_k"], hsum]
    (y_rwkv,) = _rowwise("rwkv_post", _post_fn, [y2, r, v, kd, g], post_params, [(RW, BF16)], [])
    (u,) = _rowwise("conv_glu", _glu_fn, [(p, CW, 4), (p, CW, 5)], [], [(CW, F32)], [])
    cv = _dwconv("conv_dw", u.reshape(bsz, t, CW), w["conv_dw"], w["conv_b"], False).reshape(n, CW)
    (y_conv,) = _rowwise("conv_out", _conv_out_fn, [cv], [w["conv_ln_g"], w["conv_ln_b"]], [(CW, BF16)], [])
    ycat = jnp.concatenate([y_rwkv, y_conv], axis=1)
    x2, x2b, z2 = _mm_ln("w_out_ln2", ycat, w["w_out"], x1, w["ln2_g"], w["ln2_b"], 1.0)
    hg2, hu2, act2 = _ffn_in("ffn2_in", x2b, w["ffn2_w_in"])
    dx3, z3, loss = _mm_ln("ffn2_out_ln3", act2, w["ffn2_w_out"], x2, w["ln3_g"], w["ln3_b"], 0.5,
                           tgt=tgt2d, a_blocked=True)

    gr = {}
    dz3, dz3b, gr["ln3_g"], gr["ln3_b"] = _ln_bwd("ln3_bwd", z3, w["ln3_g"], dx3)
    dx2, _ = _ffn_bwd("ffn2", gr, dz3, dz3b, x2b, w["ffn2_w_in"], w["ffn2_w_out"], hg2, hu2, act2)
    dz2, dz2b, gr["ln2_g"], gr["ln2_b"] = _ln_bwd("ln2_bwd", z2, w["ln2_g"], dx2)
    gr["w_out"] = _mm("w_out_wgrad", ycat, dz2b, "tn", BF16, tk=512)
    dycat = _mm("w_out_dgrad", dz2b, w["w_out"], "nt", F32)
    conv_out_bwd = _vjp_of(_conv_out_fn, 3)
    dcv, gr["conv_ln_g"], gr["conv_ln_b"] = _rowwise(
        "conv_out_bwd", lambda cv_, ct_, g_, b_: conv_out_bwd(cv_, g_, b_, ct_),
        [cv, (dycat, CW, 1)], [w["conv_ln_g"], w["conv_ln_b"]], [(CW, F32)], [(1, CW), (1, CW)])
    dcv3 = dcv.reshape(bsz, t, CW)
    gr["conv_dw"], gr["conv_b"] = _dwconv_dw("conv_dw_wgrad", u.reshape(bsz, t, CW), dcv3)
    du = _dwconv("conv_dw_dgrad", dcv3, w["conv_dw"], jnp.zeros((1, CW), F32), True).reshape(n, CW)

    def glu_bwd(pa, pb, ct):
        return (jnp.concatenate(_vjp_of(_glu_fn, 2)(pa, pb, ct), axis=1),)

    (dp_conv,) = _rowwise("conv_glu_bwd", glu_bwd, [(p, CW, 4), (p, CW, 5), du], [], [(2 * CW, BF16)], [])

    def post_bwd(y2_, r_, v_, kd_, g_, ct, lg, lb, rk, hs):
        return _vjp_of(lambda *z: _post_fn(*z, hs), 8)(y2_, r_, v_, kd_, g_, lg, lb, rk, ct)

    dy2, dr_post, dv_post, dkd_post, dg, gr["lnx_g"], gr["lnx_b"], gr["r_k"] = _rowwise(
        "rwkv_post_bwd", post_bwd, [y2, r, v, kd, g, (dycat, RW, 0)], post_params,
        [(2 * RW, F32), (RW, F32), (RW, F32), (2 * RW, F32), (RW, F32)], [(1, RW), (1, RW), (1, RW)])
    sends = [jnp.concatenate(gr["ffn2_w_in"], axis=0), gr["ffn2_w_out"].reshape(N_DEV, D_FF // N_DEV, D_MODEL),
             gr["w_out"].reshape(N_DEV, D_MODEL // N_DEV, D_MODEL)]
    res = _scan_bwd(r, v, kk, logw, kd, a, s0, tm, dy2, bsz, exch=(sends, False) if ex else None)
    if ex:
        res, got = res
        parts.update(zip(("ffn2_w_in", "ffn2_w_out", "w_out"), got))
    dr_s, dv_s, dkk_s, dlw, dkd_s, da = res

    def prep_bwd(ps_, dr2, dr1, dv2, dv1, dkk2, dlw_, da_, dkd2, dkd1, dg_, *prm):
        half = lambda z: z[:, 0:RW] + z[:, RW:2 * RW]
        return _vjp_of(lambda *z: _prep_fn(*z, prm[-1]), 8)(
            ps_, *prm[:-1], half(dr2) + dr1, half(dv2) + dv1, half(dkk2), dlw_, da_, dkd2 + dkd1, dg_)

    dps, gr["w0"], gr["w2"], gr["a0"], gr["a2"], gr["g2"], gr["k_k"], gr["k_a"] = _rowwise(
        "rwkv_prep_bwd", prep_bwd,
        [ps, dr_s, dr_post, dv_s, dv_post, dkk_s, dlw, da, dkd_s, dkd_post, dg], prep_params,
        [(SHIFT_PAD, F32)], [q.shape for q in prep_params[:-1]])
    dps3 = dps.reshape(bsz, t, SHIFT_PAD)
    gr["mu_prev"], gr["mu_next"] = _shift("shift_dmu", p3, None, None, q=dps3)
    dp_shift = _shift("shift_bwd", dps3, w["mu_next"], w["mu_prev"], out_dtype=BF16).reshape(n, SHIFT_PAD)
    dp = jnp.concatenate([dp_shift, dp_conv], axis=1)
    gr["w_in"] = _mm("w_in_wgrad", x1b, dp, "tn", BF16, tk=512)
    dx1 = _mm("w_in_dgrad", dp, w["w_in"], "nt", F32, add=dz2, add_scale=ALPHA)
    dz1, dz1b, gr["ln1_g"], gr["ln1_b"] = _ln_bwd("ln1_bwd", z1, w["ln1_g"], dx1)
    riders = None
    if ex:
        gw_in = _unpad_in_cols(gr["w_in"]).reshape(D_MODEL, N_DEV, IN_COLS // N_DEV).transpose(1, 0, 2)
        small = _pack([_grad_small(nm, gr[nm]) for nm in SMALL_SHARDED + SMALL_REPL])
        riders = {"out_bwd": [gw_in], "in_wgrad": [jnp.broadcast_to(small[None], (N_DEV,) + small.shape)]}
    grad_x, got = _ffn_bwd("ffn1", gr, dz1, dz1b, xb, w["ffn1_w_in"], w["ffn1_w_out"], hg1, hu1, act1, riders)
    if ex:
        parts.update(w_in=got["out_bwd"][0], ffn1_w_out=got["in_wgrad"][0], small=got["in_wgrad"][1],
                     ffn1_w_in=got["in_dgrad"][0])
    return loss[0, 0], grad_x.reshape(bsz, t, d), gr, parts


def _ffn_in_wgrad(name, xin, dhg, dhu, tk=512, exch=None):
    n = xin.shape[0]
    nj, nt = N_DEV // 2, n // tk

    def body(x_ref, g_ref, u_ref, og_ref, ou_ref, accg_ref, accu_ref):
        i = pl.program_id(1)

        @pl.when(i == 0)
        def _():
            accg_ref[...] = jnp.zeros_like(accg_ref)
            accu_ref[...] = jnp.zeros_like(accu_ref)

        xt = x_ref[...].astype(BF16).T
        accg_ref[...] += _dot(xt, g_ref[...])
        accu_ref[...] += _dot(xt, u_ref[...])

        @pl.when(i == nt - 1)
        def _():
            og_ref[...] = accg_ref[...].astype(og_ref.dtype)
            ou_ref[...] = accu_ref[...].astype(ou_ref.dtype)

    dh_blk = pl.BlockSpec((None, tk, FF_BLK), lambda j, i: (j, i, 0))
    o_blk = pl.BlockSpec((None, D_MODEL, FF_BLK), lambda j, i: (j, 0, 0))
    shp = jax.ShapeDtypeStruct((nj, D_MODEL, FF_BLK), BF16)
    return _call(name, body, (nj, nt), [xin, dhg, dhu],
                 [pl.BlockSpec((tk, D_MODEL), lambda j, i: (i, 0)), dh_blk, dh_blk], [o_blk, o_blk], [shp, shp],
                 [pltpu.VMEM((D_MODEL, FF_BLK), F32)] * 2, exch)


def _ffn_in_dgrad(name, dhg, dhu, wg, add, add_scale, tm=512, exch=None):
    n = add.shape[0]
    nj = N_DEV // 2

    def body(g_ref, u_ref, wgate_ref, wup_ref, add_ref, o_ref, acc_ref):
        j = pl.program_id(1)

        @pl.when(j == 0)
        def _():
            acc_ref[...] = jnp.zeros_like(acc_ref)

        acc_ref[...] += _dot(g_ref[...], wgate_ref[...], "nt") + _dot(u_ref[...], wup_ref[...], "nt")

        @pl.when(j == nj - 1)
        def _():
            o_ref[...] = acc_ref[...] + add_scale * add_ref[...]

    dh_blk = pl.BlockSpec((None, tm, FF_BLK), lambda i, j: (j, i, 0))
    row = pl.BlockSpec((tm, D_MODEL), lambda i, j: (i, 0))
    return _call(name, body, (n // tm, nj), [dhg, dhu, wg, wg, add],
                 [dh_blk, dh_blk, pl.BlockSpec((None, D_MODEL, FF_BLK), lambda i, j: (j, 0, 0)),
                  pl.BlockSpec((None, D_MODEL, FF_BLK), lambda i, j: (j + nj, 0, 0)), row],
                 row, jax.ShapeDtypeStruct((n, D_MODEL), F32), [pltpu.VMEM((tm, D_MODEL), F32)], exch)


def _ffn_bwd(tag, gr, dz, dzb, xin, wg, wout, hg, hu, act, riders=None):
    own = riders is not None

    def hosted(result):
        return result if own else (result, None)

    (dhg, dhu, gwo), got_a = hosted(_ffn_out_bwd(
        tag + "_out_bwd", dzb, wout, hg, hu, act, exch=(riders["out_bwd"], False) if own else None))
    gr[tag + "_w_out"] = gwo
    send = ([gwo.reshape(N_DEV, D_FF // N_DEV, D_MODEL)] + riders["in_wgrad"], False) if own else None
    dw, got_b = hosted(_ffn_in_wgrad(tag + "_in_wgrad", xin, dhg, dhu, exch=send))
    gr[tag + "_w_in"] = dw
    send = ([jnp.concatenate(dw, axis=0)], False) if own else None
    dx, got_c = hosted(_ffn_in_dgrad(tag + "_in_dgrad", dhg, dhu, wg, dz, ALPHA, exch=send))
    return dx, {"out_bwd": got_a, "in_wgrad": got_b, "in_dgrad": got_c}


def _adam_math(g, w, m, v):
    m = ADAM_B1 * m + (1.0 - ADAM_B1) * g
    v = ADAM_B2 * v + (1.0 - ADAM_B2) * (g * g)
    m_hat = m / (1.0 - ADAM_B1 ** ADAM_STEP)
    v_hat = v / (1.0 - ADAM_B2 ** ADAM_STEP)
    delta = -ADAM_LR * (m_hat / (jnp.sqrt(v_hat) + ADAM_EPS) + ADAM_WD * w)
    return delta, m, v


def _adam(name, parts, w, m, v, tr=128):
    rows, cols = w.shape
    tr = min(tr, rows)
    while rows % tr:
        tr -= 8

    def body(p_ref, w_ref, m_ref, v_ref, g_ref, d_ref, mo_ref, vo_ref):
        g = p_ref[0].astype(F32)
        for s in range(1, N_DEV):
            g = g + p_ref[s].astype(F32)
        g_ref[...] = g
        d_ref[...], mo_ref[...], vo_ref[...] = _adam_math(g, w_ref[...], m_ref[...], v_ref[...])

    blk = pl.BlockSpec((tr, cols), lambda i: (i, 0))
    shp = jax.ShapeDtypeStruct((rows, cols), F32)
    return _call(name, body, (rows // tr,), [parts, w, m, v],
                 [pl.BlockSpec((N_DEV, tr, cols), lambda i: (0, i, 0)), blk, blk, blk], [blk] * 4, [shp] * 4)


def _sum8(name, parts):
    _, rows, cols = parts.shape

    def body(p_ref, o_ref):
        g = p_ref[0]
        for s in range(1, N_DEV):
            g = g + p_ref[s]
        o_ref[...] = g

    return pl.pallas_call(body, name=name, out_shape=jax.ShapeDtypeStruct((rows, cols), F32),
                          compiler_params=_params())(parts)


def _adam_small(name, g, w, m, v):
    def body(g_ref, w_ref, m_ref, v_ref, d_ref, mo_ref, vo_ref):
        d_ref[...], mo_ref[...], vo_ref[...] = _adam_math(g_ref[...], w_ref[...], m_ref[...], v_ref[...])

    shp = jax.ShapeDtypeStruct(g.shape, F32)
    return pl.pallas_call(body, name=name, out_shape=[shp] * 3, compiler_params=_params())(g, w, m, v)


def _pack(arrs, lane=128):
    flat = jnp.concatenate([a.reshape(-1).astype(F32) for a in arrs])
    pad = (-flat.shape[0]) % (8 * lane)
    return jnp.pad(flat, (0, pad)).reshape(-1, lane)


def _unpack(packed, shapes):
    flat, out, off = packed.reshape(-1), [], 0
    for s in shapes:
        sz = math.prod(s)
        out.append(flat[off:off + sz].reshape(s))
        off += sz
    return out


def _pad_in_cols(wfull):
    zeros = jnp.zeros((wfull.shape[0], SHIFT_PAD - SHIFT_COLS), wfull.dtype)
    return jnp.concatenate([wfull[:, :SHIFT_COLS], zeros, wfull[:, SHIFT_COLS:]], axis=1)


def _unpad_in_cols(gfull):
    return jnp.concatenate([gfull[:, :SHIFT_COLS], gfull[:, SHIFT_PAD:]], axis=1)


def _block_diag2(wd):
    z = jnp.zeros_like(wd[0])
    return jnp.concatenate([jnp.concatenate([wd[0], z], axis=1), jnp.concatenate([z, wd[1]], axis=1)], axis=0)


def _unblock_diag2(g):
    return jnp.stack([g[0:64, 0:RW], g[64:128, RW:2 * RW]])


SMALL_SHARDED = ("w0", "w2", "a0", "a2", "g2", "conv_dw")
SMALL_REPL = ("mu_prev", "mu_next", "k_k", "k_a", "r_k", "lnx_g", "lnx_b", "conv_b", "conv_ln_g", "conv_ln_b",
              "ln1_g", "ln1_b", "ln2_g", "ln2_b", "ln3_g", "ln3_b")
BIG = ("ffn1_w_in", "ffn1_w_out", "w_in", "w_out", "ffn2_w_in", "ffn2_w_out")
WEIGHTS = ("ffn1_w_in", "ffn1_w_out", "w_in", "mu_prev", "mu_next", "w0", "w2", "a0", "a2", "g2", "k_k", "k_a",
           "r_k", "lnx_g", "lnx_b", "conv_dw", "conv_b", "conv_ln_g", "conv_ln_b", "w_out", "ffn2_w_in",
           "ffn2_w_out", "ln1_g", "ln1_b", "ln2_g", "ln2_b", "ln3_g", "ln3_b")


def _full_small(name, full):
    if name in ("w0", "a0"):
        return full.reshape(1, 2 * RW)
    if name in ("w2", "a2"):
        return _block_diag2(full)
    if name == "g2":
        return jnp.pad(full, ((0, 256 - GATE_LORA), (0, 0)))
    if name == "conv_dw":
        return jnp.pad(full, ((0, 1), (0, 0)))
    if name in ("mu_prev", "mu_next"):
        return jnp.pad(full.reshape(1, SHIFT_COLS), ((0, 0), (0, SHIFT_PAD - SHIFT_COLS)))
    return full.reshape(1, -1)


def _grad_small(name, g):
    if name in ("w0", "a0"):
        return g.reshape(2, RW)
    if name in ("w2", "a2"):
        return _unblock_diag2(g)
    if name == "g2":
        return g[:GATE_LORA]
    if name == "conv_dw":
        return g[:CONV_K]
    if name in ("mu_prev", "mu_next"):
        return g[0, :SHIFT_COLS]
    if name == "r_k":
        return g.reshape(N_HEADS, HEAD)
    return g.reshape(-1)


def kernel(x, ffn1_w_in, ffn1_w_out, w_in, mu_prev, mu_next, w0, w2, a0, a2, g2, k_k, k_a, r_k, lnx_g, lnx_b, conv_dw, conv_b, conv_ln_g, conv_ln_b, w_out, ffn2_w_in, ffn2_w_out, ln1_g, ln1_b, ln2_g, ln2_b, ln3_g, ln3_b, loss_target, m_ffn1_w_in, m_ffn1_w_out, m_w_in, m_mu_prev, m_mu_next, m_w0, m_w2, m_a0, m_a2, m_g2, m_k_k, m_k_a, m_r_k, m_lnx_g, m_lnx_b, m_conv_dw, m_conv_b, m_conv_ln_g, m_conv_ln_b, m_w_out, m_ffn2_w_in, m_ffn2_w_out, m_ln1_g, m_ln1_b, m_ln2_g, m_ln2_b, m_ln3_g, m_ln3_b, v_ffn1_w_in, v_ffn1_w_out, v_w_in, v_mu_prev, v_mu_next, v_w0, v_w2, v_a0, v_a2, v_g2, v_k_k, v_k_a, v_r_k, v_lnx_g, v_lnx_b, v_conv_dw, v_conv_b, v_conv_ln_g, v_conv_ln_b, v_w_out, v_ffn2_w_in, v_ffn2_w_out, v_ln1_g, v_ln1_b, v_ln2_g, v_ln2_b, v_ln3_g, v_ln3_b):
    args = dict(locals())
    drop = lambda z: z.reshape(z.shape[1:])
    wsh = {n: drop(args[n]) for n in WEIGHTS}
    msh = {n: drop(args["m_" + n]) for n in WEIGHTS}
    vsh = {n: drop(args["v_" + n]) for n in WEIGHTS}
    me = 4 * lax.axis_index("x") + 2 * lax.axis_index("y") + lax.axis_index("c")
    bf = {n: wsh[n].astype(BF16) for n in BIG}

    w = {"ffn1_w_in": _gather_two_level("gather_ffn1_w_in", bf["ffn1_w_in"])}
    for n in SMALL_REPL:
        w[n] = _full_small(n, wsh[n])
    small_shapes = [wsh[n].shape for n in SMALL_SHARDED]

    def finish1(got):
        f1_out, w_in_g, small = got
        cols = zip(*[_unpack(small[dv], small_shapes) for dv in range(N_DEV)])
        out = {n: _full_small(n, jnp.concatenate(s, axis=-1)) for n, s in zip(SMALL_SHARDED, cols)}
        out["ffn1_w_out"] = f1_out.reshape(D_FF, D_MODEL)
        out["w_in"] = _pad_in_cols(w_in_g.transpose(1, 0, 2).reshape(D_MODEL, IN_COLS))
        return out

    def finish2(got):
        w_out_g, f2_in, f2_out = got
        return {"w_out": w_out_g.reshape(D_MODEL, D_MODEL), "ffn2_w_in": f2_in,
                "ffn2_w_out": f2_out.reshape(D_FF, D_MODEL)}

    ex = {"g1": ([bf["ffn1_w_out"], bf["w_in"], _pack([wsh[n] for n in SMALL_SHARDED])], finish1),
          "g2": ([bf["w_out"], bf["ffn2_w_in"], bf["ffn2_w_out"]], finish2)}
    loss_part, grad_x, gr, parts = _local_step(x, loss_target, w, ex)
    loss = lax.psum(loss_part, ("x", "y", "c"))

    out = {n: _adam("adam_" + n, parts[n], wsh[n], msh[n], vsh[n]) for n in BIG}
    small_names = SMALL_SHARDED + SMALL_REPL
    full_shapes = [_grad_small(n, gr[n]).shape for n in small_names]
    summed = _unpack(_sum8("sum_small_grads", parts["small"]), full_shapes)
    mine = []
    for n, g in zip(small_names, summed):
        if n in SMALL_SHARDED:
            g = lax.dynamic_slice_in_dim(g, me * HEAD, HEAD, axis=g.ndim - 1)
        mine.append(g)
    shapes = [g.shape for g in mine]
    d_s, m_s, v_s = _adam_small("adam_small", _pack(mine), _pack([wsh[n] for n in small_names]),
                                _pack([msh[n] for n in small_names]), _pack([vsh[n] for n in small_names]))
    for n, g, dl, mn, vn in zip(small_names, mine, _unpack(d_s, shapes), _unpack(m_s, shapes), _unpack(v_s, shapes)):
        out[n] = (g, dl, mn, vn)

    res = [loss, grad_x]
    for k in range(4):
        res += [out[n][k].reshape((1,) + out[n][k].shape) for n in WEIGHTS]
    return tuple(res)
```

```python
import functools
import math

import jax
import jax.numpy as jnp
from jax import lax
from jax.experimental import pallas as pl
from jax.experimental.pallas import tpu as pltpu

F32 = jnp.float32
BF16 = jnp.bfloat16

N_DEV = 8
D_MODEL = 1024
RW = 512
N_HEADS = 8
HEAD = 64
CW = 512
CONV_K = 31
D_FF = 2816
FF_BLK = 704
GATE_LORA = 160
SHIFT_COLS = 1952
SHIFT_PAD = 2048
IN_COLS = 2976
IN_PAD = 3072
LN_EPS = 1e-5
GN_EPS = 64e-5
NORM_EPS = 1e-12
ALPHA = 2.0 ** 0.25
DECAY_SCALE = math.exp(-0.5)
CHUNK = 64
ADAM_LR, ADAM_B1, ADAM_B2, ADAM_EPS, ADAM_WD, ADAM_STEP = 0.001, 0.9, 0.999, 1e-8, 0.01, 10
VMEM_LIMIT = 56 * 1024 * 1024
MXU_DIM = 256

_DN = {"nn": (((1,), (0,)), ((), ())), "nt": (((1,), (1,)), ((), ())), "tn": (((0,), (0,)), ((), ()))}


def _params():
    return pltpu.CompilerParams(vmem_limit_bytes=VMEM_LIMIT)


def _dot(a, b, dims="nn"):
    return lax.dot_general(a, b, _DN[dims], preferred_element_type=F32)


def _split(x):
    hi = x.astype(BF16)
    return hi, (x - hi.astype(F32)).astype(BF16)


def _dot3_impl(a, b, dims):
    ah, al = _split(a)
    bh, bl = _split(b)
    ka, kb = _DN[dims][0][0][0], _DN[dims][0][1][0]
    return _dot(jnp.concatenate([ah, ah, al], axis=ka), jnp.concatenate([bh, bl, bh], axis=kb), dims)


@functools.partial(jax.custom_vjp, nondiff_argnums=(2,))
def _dot3(a, b, dims="nn"):
    return _dot3_impl(a, b, dims)


def _dot3_fwd(a, b, dims):
    return _dot3_impl(a, b, dims), (a, b)


def _dot3_bwd(dims, res, ct):
    a, b = res
    if dims == "nn":
        return _dot3_impl(ct, b, "nt"), _dot3_impl(a, ct, "tn")
    if dims == "nt":
        return _dot3_impl(ct, b, "nn"), _dot3_impl(ct, a, "tn")
    return _dot3_impl(b, ct, "nt"), _dot3_impl(a, ct, "nn")


_dot3.defvjp(_dot3_fwd, _dot3_bwd)


def _dot1_impl(a, b, dims):
    return _dot(a.astype(BF16), b.astype(BF16), dims)


@functools.partial(jax.custom_vjp, nondiff_argnums=(2,))
def _dot1(a, b, dims="nn"):
    return _dot1_impl(a, b, dims)


def _dot1_bwd(dims, res, ct):
    a, b = res
    if dims == "nn":
        return _dot1_impl(ct, b, "nt"), _dot1_impl(a, ct, "tn")
    if dims == "nt":
        return _dot1_impl(ct, b, "nn"), _dot1_impl(ct, a, "tn")
    return _dot1_impl(b, ct, "nt"), _dot1_impl(a, ct, "nn")


_dot1.defvjp(lambda a, b, dims: (_dot1_impl(a, b, dims), (a, b)), _dot1_bwd)


def _dot1_two(a, b, dims="nn"):
    ax_a, ax_b = {"nn": (0, 1), "nt": (0, 0), "tn": (1, 1)}[dims]
    shallow = dims != "tn" and 2 * a[0].shape[1] <= MXU_DIM
    out = []
    for i in range(0, len(a), 2):
        if shallow:
            z = jnp.zeros_like(b[i])
            bd = jnp.concatenate([jnp.concatenate([b[i], z], axis=1), jnp.concatenate([z, b[i + 1]], axis=1)], axis=0)
            r = _dot1(jnp.concatenate(a[i:i + 2], axis=1), bd, dims)
            n = r.shape[1] // 2
            out += [r[:, :n], r[:, n:]]
        else:
            r = _dot1(jnp.concatenate(a[i:i + 2], axis=ax_a), jnp.concatenate(b[i:i + 2], axis=ax_b), dims)
            m, n = r.shape[0] // 2, r.shape[1] // 2
            out += [r[:m, :n], r[m:, n:]]
    return out


def _tri_inv_impl(l, eye):
    steps = int(math.log2(CHUNK)) - 1
    m = l[0].shape[0]
    tm = [eye + x for x in l]
    lp = _dot1_two(l, l)
    for k in range(steps):
        if k < steps - 1:
            both = _dot1_two([jnp.concatenate([t, p], axis=0) for t, p in zip(tm, lp)], lp)
            tm = [t + b[:m] for t, b in zip(tm, both)]
            lp = [b[m:] for b in both]
        else:
            tm = [t + x for t, x in zip(tm, _dot1_two(tm, lp))]
    return tm


@jax.custom_vjp
def _tri_inv(l, eye):
    return _tri_inv_impl(l, eye)


def _tri_inv_fwd(l, eye):
    tm = _tri_inv_impl(l, eye)
    return tm, (tm, eye)


def _tri_inv_bwd(res, ct):
    tm, eye = res
    return _dot1_two(_dot1_two(tm, ct, "tn"), tm, "nt"), jnp.zeros_like(eye)


_tri_inv.defvjp(_tri_inv_fwd, _tri_inv_bwd)


@jax.custom_vjp
def _tri_inv_known(l, tm):
    return tm


_tri_inv_known.defvjp(lambda l, tm: (tm, tm),
                      lambda tm, ct: (_dot1_two(_dot1_two(tm, ct, "tn"), tm, "nt"), [jnp.zeros_like(t) for t in tm]))


def _ones_impl(x, g3):
    x1 = x.astype(BF16)
    r1 = x - x1.astype(F32)
    x2 = r1.astype(BF16)
    x3 = (r1 - x2.astype(F32)).astype(BF16)
    return _dot(jnp.concatenate([x1, x2, x3], axis=1), g3)


@jax.custom_vjp
def _head_sum(x, g3):
    return _ones_impl(x, g3)


_head_sum.defvjp(lambda x, g3: (_ones_impl(x, g3), g3), lambda g3, ct: (_ones_impl(ct, g3), jnp.zeros_like(g3)))


def _prefix_sum(x):
    row = lax.broadcasted_iota(jnp.int32, x.shape, 0)
    sh = 1
    while sh < x.shape[0]:
        x = x + jnp.where(row >= sh, pltpu.roll(x, sh, 0), 0.0)
        sh *= 2
    return x


def _dir_cumsum_impl(x, sgn):
    pre = _prefix_sum(x)
    return jnp.where(sgn > 0.0, pre, jnp.sum(x, axis=0, keepdims=True) - pre + x)


@jax.custom_vjp
def _dir_cumsum(x, sgn):
    return _dir_cumsum_impl(x, sgn)


_dir_cumsum.defvjp(lambda x, sgn: (_dir_cumsum_impl(x, sgn), sgn),
                   lambda sgn, ct: (_dir_cumsum_impl(ct, -sgn), jnp.zeros_like(sgn)))


def _sigmoid(x):
    return 1.0 / (1.0 + jnp.exp(-x))


def _mesh_pos():
    return lax.axis_index("x"), lax.axis_index("y"), lax.axis_index("c")


def _peer(pos, q):
    x, y, c = pos
    return (1 - x if q & 4 else x, 1 - y if q & 2 else y, 1 - c if q & 1 else c)


def _linear(pos):
    return 4 * pos[0] + 2 * pos[1] + pos[2]


def _exchange_copies(x_refs, o_refs, send_sems, recv_sems, local_sems, gather):
    pos = _mesh_pos()
    me = _linear(pos)
    starts, wait_recv, wait_send, wait_local = [], [], [], []
    for t in range(len(x_refs)):
        src = x_refs[t] if gather else x_refs[t].at[me]
        cp = pltpu.make_async_copy(src, o_refs[t].at[me], local_sems.at[t])
        starts.append(cp.start)
        wait_local.append(cp.wait)
    for q in range(1, N_DEV):
        peer = _peer(pos, q)
        for t in range(len(x_refs)):
            src = x_refs[t] if gather else x_refs[t].at[_linear(peer)]
            sems = dict(send_sem=send_sems.at[t, q - 1], recv_sem=recv_sems.at[t, q - 1],
                        device_id=peer, device_id_type=pl.DeviceIdType.MESH)
            send = pltpu.make_async_remote_copy(src_ref=src, dst_ref=o_refs[t].at[me], **sems)
            recv = pltpu.make_async_remote_copy(src_ref=src, dst_ref=o_refs[t].at[_linear(peer)], **sems)
            starts.append(send.start)
            wait_recv.append(recv.wait_recv)
            wait_send.append(send.wait_send)
    return starts, wait_recv + wait_send + wait_local


def _exchange_shapes(xs, gather):
    return [jax.ShapeDtypeStruct((N_DEV,) + (x.shape if gather else x.shape[1:]), x.dtype) for x in xs]


def _exchange_sems(nt):
    return [pltpu.SemaphoreType.DMA((nt, N_DEV - 1)), pltpu.SemaphoreType.DMA((nt, N_DEV - 1)),
            pltpu.SemaphoreType.DMA((nt,))]


def _gather_two_level(name, x):
    def body(x_ref, out_ref, send_sems, recv_sems, local_sem):
        px, py, pc = _mesh_pos()
        me, sibling = (px, py, pc), (px, py, 1 - pc)
        chips = [(1 - px, py), (px, 1 - py), (1 - px, 1 - py)]

        def slot(pos):
            return out_ref.at[_linear(pos)]

        def copy(k, block, to, src=None):
            return pltpu.make_async_remote_copy(
                src_ref=slot(block) if src is None else src, dst_ref=slot(block), send_sem=send_sems.at[k],
                recv_sem=recv_sems.at[k], device_id=to, device_id_type=pl.DeviceIdType.MESH)

        mine = pltpu.make_async_copy(x_ref, slot(me), local_sem)
        mine.start()
        first = [copy(0, me, sibling, src=x_ref)]
        first += [copy(1 + j, me, (*chip, pc), src=x_ref) for j, chip in enumerate(chips)]
        for cp in first:
            cp.start()
        passed = [copy(4 + j, (*chip, pc), sibling) for j, chip in enumerate(chips)]
        for j, chip in enumerate(chips):
            copy(1 + j, (*chip, pc), me).wait_recv()
            passed[j].start()
        copy(0, sibling, me).wait_recv()
        for j, chip in enumerate(chips):
            copy(4 + j, (*chip, 1 - pc), me).wait_recv()
        for cp in first + passed:
            cp.wait_send()
        mine.wait()

    any_spec = pl.BlockSpec(memory_space=pl.ANY)
    return pl.pallas_call(
        body, name=name, in_specs=[any_spec], out_specs=any_spec,
        out_shape=jax.ShapeDtypeStruct((N_DEV,) + x.shape, x.dtype),
        scratch_shapes=[pltpu.SemaphoreType.DMA((N_DEV - 1,)), pltpu.SemaphoreType.DMA((N_DEV - 1,)),
                        pltpu.SemaphoreType.DMA])(x)


def _call(name, body, grid, ins, in_specs, out_specs, out_shape, scratch=(), exch=None):
    if exch is None:
        return pl.pallas_call(body, name=name, grid=grid, in_specs=in_specs, out_specs=out_specs,
                              out_shape=out_shape, scratch_shapes=list(scratch), compiler_params=_params())(*ins)
    xs, gather = exch
    single = not isinstance(out_shape, (list, tuple))
    o_specs = [out_specs] if single else list(out_specs)
    o_shape = [out_shape] if single else list(out_shape)
    n_in, n_out, n_x, n_scr = len(ins), len(o_shape), len(xs), len(scratch)

    def wrapped(*refs):
        in_refs = refs[:n_in]
        x_refs = refs[n_in:n_in + n_x]
        out_refs = refs[n_in + n_x:n_in + n_x + n_out]
        got_refs = refs[n_in + n_x + n_out:n_in + 2 * n_x + n_out]
        rest = refs[n_in + 2 * n_x + n_out:]
        starts, waits = _exchange_copies(x_refs, got_refs, *rest[n_scr:], gather)
        ids = [pl.program_id(i) for i in range(len(grid))]
        first = functools.reduce(lambda p, q: p & q, [i == 0 for i in ids])
        last = functools.reduce(lambda p, q: p & q, [i == g - 1 for i, g in zip(ids, grid)])

        @pl.when(first)
        def _():
            for f in starts:
                f()

        body(*in_refs, *out_refs, *rest[:n_scr])

        @pl.when(last)
        def _():
            for f in waits:
                f()

    any_spec = pl.BlockSpec(memory_space=pl.ANY)
    outs = pl.pallas_call(
        wrapped, name=name, grid=grid, in_specs=list(in_specs) + [any_spec] * n_x,
        out_specs=o_specs + [any_spec] * n_x, out_shape=o_shape + _exchange_shapes(xs, gather),
        scratch_shapes=list(scratch) + _exchange_sems(n_x), compiler_params=_params())(*ins, *xs)
    res = outs[:n_out]
    return (res[0] if single else res), outs[n_out:]


def _mm_call(name, dims, grid, red_axis, ins, in_specs, out_shape, out_spec, acc_shape,
             scale=1.0, add_scale=None, exch=None):
    nred = grid[red_axis]

    def body(*refs):
        if add_scale is None:
            a_ref, b_ref, o_ref, acc_ref = refs
            add_ref = None
        else:
            a_ref, b_ref, add_ref, o_ref, acc_ref = refs
        k = pl.program_id(red_axis)

        @pl.when(k == 0)
        def _():
            acc_ref[...] = jnp.zeros_like(acc_ref)

        acc_ref[...] += _dot(a_ref[...].astype(BF16), b_ref[...].astype(BF16), dims)

        @pl.when(k == nred - 1)
        def _():
            r = acc_ref[...]
            if scale != 1.0:
                r = r * scale
            if add_ref is not None:
                r = r + add_scale * add_ref[...].astype(F32)
            o_ref[...] = r.astype(o_ref.dtype)

    return _call(name, body, grid, ins, in_specs, out_spec, out_shape, [pltpu.VMEM(acc_shape, F32)], exch)


def _tile(dim, cap):
    t = min(dim, cap)
    while dim % t or t % 128:
        t -= 128
        assert t > 0, (dim, cap)
    return t


def _mm(name, a, b, dims, out_dtype, scale=1.0, add=None, add_scale=None, tm=512, tn=1024, tk=1024):
    if dims == "tn":
        kd, m = a.shape
        n = b.shape[1]
    else:
        m, kd = a.shape
        n = b.shape[1] if dims == "nn" else b.shape[0]
    tm, tn, tk = _tile(m, tm), _tile(n, tn), _tile(kd, tk)
    a_spec = (pl.BlockSpec((tk, tm), lambda i, j, k: (k, i)) if dims == "tn"
              else pl.BlockSpec((tm, tk), lambda i, j, k: (i, k)))
    b_spec = (pl.BlockSpec((tn, tk), lambda i, j, k: (j, k)) if dims == "nt"
              else pl.BlockSpec((tk, tn), lambda i, j, k: (k, j)))
    o_spec = pl.BlockSpec((tm, tn), lambda i, j, k: (i, j))
    ins, specs = [a, b], [a_spec, b_spec]
    if add is not None:
        ins.append(add)
        specs.append(o_spec)
    return _mm_call(name, dims, (m // tm, n // tn, kd // tk), 2, ins, specs,
                    jax.ShapeDtypeStruct((m, n), out_dtype), o_spec, (tm, tn),
                    scale=scale, add_scale=add_scale if add is not None else None)


def _ffn_in(name, x, wg, tm=512, exch=None):
    n = x.shape[0]
    nj = N_DEV // 2

    def body(x_ref, wgate_ref, wup_ref, hg_ref, hu_ref, act_ref):
        xb = x_ref[...].astype(BF16)
        g = _dot(xb, wgate_ref[...])
        u = _dot(xb, wup_ref[...])
        hg_ref[...] = g.astype(BF16)
        hu_ref[...] = u.astype(BF16)
        act_ref[...] = (g * _sigmoid(g) * u).astype(BF16)

    blk = pl.BlockSpec((None, tm, FF_BLK), lambda j, i: (j, i, 0))
    shp = jax.ShapeDtypeStruct((nj, n, FF_BLK), BF16)
    return _call(name, body, (nj, n // tm), [x, wg, wg],
                 [pl.BlockSpec((tm, D_MODEL), lambda j, i: (i, 0)),
                  pl.BlockSpec((None, D_MODEL, FF_BLK), lambda j, i: (j, 0, 0)),
                  pl.BlockSpec((None, D_MODEL, FF_BLK), lambda j, i: (j + nj, 0, 0))],
                 [blk, blk, blk], [shp, shp, shp], exch=exch)


def _ffn_out_bwd(name, dz, wout, hg, hu, act, tm=512, exch=None):
    n = dz.shape[0]
    nj, ni = N_DEV // 2, n // tm

    def body(dz_ref, w_ref, hg_ref, hu_ref, act_ref, dhg_ref, dhu_ref, dw_ref, acc_ref):
        i = pl.program_id(1)
        for r0 in range(0, tm, tm // 4):
            rows = slice(r0, r0 + tm // 4)
            dact = 0.5 * _dot(dz_ref[rows, :].astype(BF16), w_ref[...], "nt")
            g = hg_ref[rows, :].astype(F32)
            u = hu_ref[rows, :].astype(F32)
            s = _sigmoid(g)
            dhg_ref[rows, :] = (dact * u * (s * (1.0 + g * (1.0 - s)))).astype(BF16)
            dhu_ref[rows, :] = (dact * (g * s)).astype(BF16)

        @pl.when(i == 0)
        def _():
            acc_ref[...] = jnp.zeros_like(acc_ref)

        acc_ref[...] += _dot(act_ref[...], dz_ref[...].astype(BF16), "tn")

        @pl.when(i == ni - 1)
        def _():
            dw_ref[...] = (0.5 * acc_ref[...]).astype(dw_ref.dtype)

    blk = pl.BlockSpec((None, tm, FF_BLK), lambda j, i: (j, i, 0))
    wblk = pl.BlockSpec((FF_BLK, D_MODEL), lambda j, i: (j, 0))
    shp = jax.ShapeDtypeStruct((nj, n, FF_BLK), BF16)
    return _call(name, body, (nj, ni), [dz, wout, hg, hu, act],
                 [pl.BlockSpec((tm, D_MODEL), lambda j, i: (i, 0)), wblk, blk, blk, blk],
                 [blk, blk, wblk], [shp, shp, jax.ShapeDtypeStruct((D_FF, D_MODEL), BF16)],
                 [pltpu.VMEM((FF_BLK, D_MODEL), F32)], exch=exch)


def _mm_ln(name, a, b, xres, g, beta, c, tgt=None, a_blocked=False, tm=512, tk=512):
    if a_blocked:
        nk, n, kb = a.shape
        a_spec = pl.BlockSpec((None, tm, kb), lambda i, k: (k, i, 0))
    else:
        n, kd = a.shape
        kb = _tile(kd, tk)
        nk = kd // kb
        a_spec = pl.BlockSpec((tm, kb), lambda i, k: (i, k))
    d = b.shape[1]
    with_loss = tgt is not None

    def body(*refs):
        if with_loss:
            a_ref, b_ref, x_ref, g_ref, be_ref, t_ref, o_ref, z_ref, l_ref, acc_ref = refs
        else:
            a_ref, b_ref, x_ref, g_ref, be_ref, o_ref, ob_ref, z_ref, acc_ref = refs
        i, k = pl.program_id(0), pl.program_id(1)

        @pl.when(k == 0)
        def _():
            acc_ref[...] = jnp.zeros_like(acc_ref)

        acc_ref[...] += _dot(a_ref[...].astype(BF16), b_ref[...].astype(BF16))

        @pl.when(k == nk - 1)
        def _():
            z = ALPHA * x_ref[...] + c * acc_ref[...]
            z_ref[...] = z
            mu = jnp.mean(z, axis=-1, keepdims=True)
            zc = z - mu
            var = jnp.mean(zc * zc, axis=-1, keepdims=True)
            y = zc * lax.rsqrt(var + LN_EPS) * g_ref[...] + be_ref[...]
            if with_loss:
                err = y - t_ref[...]
                o_ref[...] = err * (1.0 / d)
                part = 0.5 * jnp.sum(jnp.sum(err * err, axis=-1, keepdims=True), axis=0, keepdims=True) * (1.0 / d)

                @pl.when(i == 0)
                def _():
                    l_ref[...] = jnp.zeros_like(l_ref)

                l_ref[...] += jnp.broadcast_to(part, l_ref.shape)
            else:
                o_ref[...] = y
                ob_ref[...] = y.astype(BF16)

    row = pl.BlockSpec((tm, d), lambda i, k: (i, 0))
    vec = pl.BlockSpec((1, d), lambda i, k: (0, 0))
    ins = [a, b, xres, g, beta]
    in_specs = [a_spec, pl.BlockSpec((kb, d), lambda i, k: (k, 0)), row, vec, vec]
    out_specs = [row, row]
    out_shape = [jax.ShapeDtypeStruct((n, d), F32), jax.ShapeDtypeStruct((n, d), F32)]
    if not with_loss:
        out_specs.insert(1, row)
        out_shape.insert(1, jax.ShapeDtypeStruct((n, d), BF16))
    if with_loss:
        ins.append(tgt)
        in_specs.append(row)
        out_specs.append(pl.BlockSpec((1, 128), lambda i, k: (0, 0)))
        out_shape.append(jax.ShapeDtypeStruct((1, 128), F32))
    return _call(name, body, (n // tm, nk), ins, in_specs, out_specs, out_shape, [pltpu.VMEM((tm, d), F32)])


def _rowwise(name, fn, rows, params, out_rows, out_accs, tm=256, exch=None):
    specs, ins = [], []
    for r in rows:
        arr, w, cb = r if isinstance(r, tuple) else (r, r.shape[1], 0)
        ins.append(arr)
        specs.append(pl.BlockSpec((tm, w), functools.partial(lambda i, cb: (i, cb), cb=cb)))
    n = ins[0].shape[0]
    for p in params:
        ins.append(p)
        specs.append(pl.BlockSpec(p.shape, lambda i: (0, 0)))
    n_in, n_or = len(ins), len(out_rows)

    def body(*refs):
        outs = fn(*[r[...] for r in refs[:n_in]])
        o_refs = refs[n_in:]
        for o_ref, o in zip(o_refs[:n_or], outs[:n_or]):
            o_ref[...] = o.astype(o_ref.dtype)
        if out_accs:
            @pl.when(pl.program_id(0) == 0)
            def _():
                for a_ref in o_refs[n_or:]:
                    a_ref[...] = jnp.zeros_like(a_ref)

            for a_ref, a in zip(o_refs[n_or:], outs[n_or:]):
                a_ref[...] += a.astype(F32)

    out_specs = [pl.BlockSpec((tm, w), lambda i: (i, 0)) for w, _ in out_rows]
    out_specs += [pl.BlockSpec(s, lambda i: (0, 0)) for s in out_accs]
    out_shape = [jax.ShapeDtypeStruct((n, w), dt) for w, dt in out_rows]
    out_shape += [jax.ShapeDtypeStruct(s, F32) for s in out_accs]
    return _call(name, body, (n // tm,), ins, specs, out_specs, out_shape, exch=exch)


def _vjp_of(fn, n_in):
    def g(*args):
        ins, cts = args[:n_in], args[n_in:]
        outs, pull = jax.vjp(fn, *ins)
        return pull(tuple(c.astype(o.dtype) for c, o in zip(cts, outs)))
    return g


def _ln_bwd(name, z, g, ct):
    def fn(zt, ct_, gt):
        mu = jnp.mean(zt, axis=-1, keepdims=True)
        zc = zt - mu
        rstd = lax.rsqrt(jnp.mean(zc * zc, axis=-1, keepdims=True) + LN_EPS)
        xh = zc * rstd
        dxh = ct_ * gt
        dz = rstd * (dxh - jnp.mean(dxh, axis=-1, keepdims=True)
                     - xh * jnp.mean(dxh * xh, axis=-1, keepdims=True))
        return dz, dz, jnp.sum(ct_ * xh, axis=0, keepdims=True), jnp.sum(ct_, axis=0, keepdims=True)

    d = z.shape[1]
    return _rowwise(name, fn, [z, ct], [g], [(d, F32), (d, BF16)], [(1, d), (1, d)])


def _shift(name, src, mu_a, mu_b, q=None, tt=256, out_dtype=F32):
    bsz, t, _ = src.shape
    nt, r8, w = t // tt, tt // 8, SHIFT_PAD
    with_q = q is not None

    def body(cur_ref, prev_ref, next_ref, *rest):
        b, i = pl.program_id(0), pl.program_id(1)
        cur = cur_ref[...]
        prow = jnp.where(i > 0, prev_ref[7:8, :], 0.0)
        nrow = jnp.where(i < nt - 1, next_ref[0:1, :], 0.0)
        rid = lax.broadcasted_iota(jnp.int32, cur.shape, 0)
        dprev = jnp.where(rid == 0, prow, pltpu.roll(cur, 1, 0)) - cur
        dnext = jnp.where(rid == tt - 1, nrow, pltpu.roll(cur, tt - 1, 0)) - cur
        if with_q:
            q_ref, da_ref, db_ref = rest

            @pl.when((b == 0) & (i == 0))
            def _():
                da_ref[...] = jnp.zeros_like(da_ref)
                db_ref[...] = jnp.zeros_like(db_ref)

            qv = q_ref[...]
            da_ref[...] += jnp.sum(qv * dprev, axis=0, keepdims=True)
            db_ref[...] += jnp.sum(qv * dnext, axis=0, keepdims=True)
        else:
            ma_ref, mb_ref, o_ref = rest
            o_ref[...] = (cur + ma_ref[...] * dprev + mb_ref[...] * dnext).astype(o_ref.dtype)

    cur_spec = pl.BlockSpec((None, tt, w), lambda b, i: (b, i, 0))
    in_specs = [cur_spec,
                pl.BlockSpec((None, 8, w), lambda b, i: (b, jnp.maximum(i * r8 - 1, 0), 0)),
                pl.BlockSpec((None, 8, w), lambda b, i: (b, jnp.minimum((i + 1) * r8, t // 8 - 1), 0))]
    vec = pl.BlockSpec((1, w), lambda b, i: (0, 0))
    if with_q:
        return _call(name, body, (bsz, nt), [src, src, src, q], in_specs + [cur_spec], [vec, vec],
                     [jax.ShapeDtypeStruct((1, w), F32)] * 2)
    return _call(name, body, (bsz, nt), [src, src, src, mu_a, mu_b], in_specs + [vec, vec], cur_spec,
                 jax.ShapeDtypeStruct((bsz, t, w), out_dtype))


CONV_BLK = 128


def _halo_specs(t, tt, w):
    r16 = tt // 16
    return [pl.BlockSpec((None, tt, w), lambda b, i: (b, i, 0)),
            pl.BlockSpec((None, 16, w), lambda b, i: (b, jnp.maximum(i * r16 - 1, 0), 0)),
            pl.BlockSpec((None, 16, w), lambda b, i: (b, jnp.minimum((i + 1) * r16, t // 16 - 1), 0))]


def _fill_pad(pad_ref, cur_ref, prev_ref, next_ref, i, nt, tt):
    pad_ref[0:16, :] = jnp.where(i > 0, prev_ref[...], 0.0)
    pad_ref[16:16 + tt, :] = cur_ref[...]
    pad_ref[16 + tt:32 + tt, :] = jnp.where(i < nt - 1, next_ref[...], 0.0)


def _dwconv(name, u, dw32, bias, flip, tt=512):
    bsz, t, w = u.shape
    tt = min(tt, t)
    nt = t // tt

    def body(cur_ref, prev_ref, next_ref, dw_ref, b_ref, o_ref, pad_ref):
        i = pl.program_id(1)
        _fill_pad(pad_ref, cur_ref, prev_ref, next_ref, i, nt, tt)
        for r0 in range(0, tt, CONV_BLK):
            for cs in (slice(c0, c0 + CONV_BLK) for c0 in range(0, w, CONV_BLK)):
                acc = jnp.broadcast_to(b_ref[:, cs], (CONV_BLK, CONV_BLK))
                for k in range(CONV_K):
                    kk = CONV_K - 1 - k if flip else k
                    acc = acc + pad_ref[pl.ds(r0 + 1 + k, CONV_BLK), cs] * dw_ref[kk:kk + 1, cs]
                o_ref[r0:r0 + CONV_BLK, cs] = acc

    return _call(name, body, (bsz, nt), [u, u, u, dw32, bias],
                 _halo_specs(t, tt, w) + [pl.BlockSpec((32, w), lambda b, i: (0, 0)),
                                          pl.BlockSpec((1, w), lambda b, i: (0, 0))],
                 pl.BlockSpec((None, tt, w), lambda b, i: (b, i, 0)), jax.ShapeDtypeStruct((bsz, t, w), F32),
                 [pltpu.VMEM((tt + 32, w), F32)])


def _dwconv_dw(name, u, dc, tt=512):
    bsz, t, w = u.shape
    tt = min(tt, t)
    nt = t // tt

    def body(cur_ref, prev_ref, next_ref, dc_ref, ddw_ref, db_ref, pad_ref):
        b, i = pl.program_id(0), pl.program_id(1)
        _fill_pad(pad_ref, cur_ref, prev_ref, next_ref, i, nt, tt)

        @pl.when((b == 0) & (i == 0))
        def _():
            ddw_ref[...] = jnp.zeros_like(ddw_ref)
            db_ref[...] = jnp.zeros_like(db_ref)

        dcv = dc_ref[...]
        db_ref[...] += jnp.sum(dcv, axis=0, keepdims=True)
        for k in range(CONV_K):
            ddw_ref[k:k + 1, :] += jnp.sum(dcv * pad_ref[pl.ds(1 + k, tt), :], axis=0, keepdims=True)

    return _call(name, body, (bsz, nt), [u, u, u, dc],
                 _halo_specs(t, tt, w) + [pl.BlockSpec((None, tt, w), lambda b, i: (b, i, 0))],
                 [pl.BlockSpec((32, w), lambda b, i: (0, 0)), pl.BlockSpec((1, w), lambda b, i: (0, 0))],
                 [jax.ShapeDtypeStruct((32, w), F32), jax.ShapeDtypeStruct((1, w), F32)],
                 [pltpu.VMEM((tt + 32, w), F32)])


PAIR = 2 * HEAD
N_PAIRS = RW // PAIR


def _chunk_pairs(s, r, lw, k, v, kk, a, sgn, tm_known=None):
    n, m = CHUNK, 2 * CHUNK
    in_a = lax.broadcasted_iota(jnp.int32, (n, PAIR), 1) < HEAD

    def stack2(z):
        return jnp.concatenate([jnp.where(in_a, z, 0.0), jnp.where(in_a, 0.0, z)], axis=0)

    def each(f, *lists):
        return [f(*z) for z in zip(*lists)]

    sgn_f = sgn.astype(F32)
    row2 = lax.broadcasted_iota(jnp.int32, (m, m), 0)
    col2 = lax.broadcasted_iota(jnp.int32, (m, m), 1)
    same = (row2 >= n) == (col2 >= n)
    dlt = ((row2 & (n - 1)) - (col2 & (n - 1))) * sgn
    incl, strict = same & (dlt >= 0), same & (dlt > 0)
    eye = jnp.where(row2 == col2, 1.0, 0.0)

    cum = each(lambda lw_: _dir_cumsum(lw_, sgn_f), lw)
    tot = each(lambda lw_: jnp.sum(lw_, axis=0, keepdims=True), lw)
    e_neg = each(lambda c_: jnp.exp(-c_), cum)
    e_rest = each(lambda t_, c_: jnp.exp(t_ - c_), tot, cum)
    beta = each(lambda kk_, a_: kk_ * a_, kk, a)
    lhs = each(lambda kk_, c_, lw_, r_: jnp.concatenate(
        [stack2(-kk_ * jnp.exp(c_ - lw_)), stack2(r_ * jnp.exp(c_))], axis=0), kk, cum, lw, r)
    rhs = each(lambda b_, k_, e_: jnp.concatenate([stack2(b_ * e_), stack2(k_ * e_)], axis=0), beta, k, e_neg)
    sc = each(lambda l_, r_: _dot3(l_, r_, "nt"), lhs, rhs)
    l_ab = each(lambda sc_: jnp.where(strict, sc_[0:m, 0:m], 0.0), sc)
    l_ak = each(lambda sc_: jnp.where(strict, sc_[0:m, m:2 * m], 0.0), sc)
    m_r = each(lambda sc_: jnp.where(jnp.concatenate([incl, incl], axis=1), sc_[m:2 * m, :], 0.0), sc)
    tm = _tri_inv(l_ab, eye) if tm_known is None else _tri_inv_known(l_ab, tm_known)
    z = _dot1_two(lhs, s, "nt")
    v2 = each(stack2, v)
    u2 = _dot1_two(tm, each(lambda z_, lv_: z_[0:m] + lv_, z, _dot1_two(l_ak, v2)))
    uv = each(lambda u_, v_: jnp.concatenate([u_, v_], axis=0), u2, v2)
    y2 = each(lambda z_, mu_: z_[m:2 * m] + mu_, z, _dot1_two(m_r, uv))
    bk = each(lambda b_, k_, e_: jnp.concatenate([stack2(b_ * e_), stack2(k_ * e_)], axis=0), beta, k, e_rest)
    s_new = each(lambda s_, t_, d_: s_ * jnp.exp(t_) + d_, s, tot, _dot1_two(uv, bk, "tn"))
    return each(lambda y_: y_[0:n] + y_[n:m], y2), s_new, tm


SCAN_SEQS = 4
N_CHAINS = SCAN_SEQS * N_PAIRS


def _pair_tiles(ref):
    return [ref[q, :, p * PAIR:(p + 1) * PAIR] for q in range(SCAN_SEQS) for p in range(N_PAIRS)]


def _store_tiles(ref, tiles):
    for q in range(SCAN_SEQS):
        for p in range(N_PAIRS):
            ref[q, :, p * PAIR:(p + 1) * PAIR] = tiles[q * N_PAIRS + p]


def _scan_specs(order):
    shared = pl.BlockSpec((SCAN_SEQS, CHUNK, RW), lambda d, b, c: (b, order(d, c), 0))
    per_dir = pl.BlockSpec((SCAN_SEQS, CHUNK, RW), lambda d, b, c: (b, order(d, c), d))
    state = pl.BlockSpec((None, SCAN_SEQS, None, N_PAIRS, PAIR, PAIR), lambda d, b, c: (d, b, order(d, c), 0, 0, 0))
    return shared, per_dir, state


def _scan_fwd(r, v, kk, lw, kd, a, bsz, exch=None):
    n = r.shape[0]
    t = n // bsz
    nc = t // CHUNK

    def order(d, c):
        return c + d * (nc - 1 - 2 * c)

    def body(r_ref, v_ref, kk_ref, lw_ref, kd_ref, a_ref, y_ref, s0_ref, tm_ref, s_ref):
        d, c = pl.program_id(0), pl.program_id(2)

        @pl.when(c == 0)
        def _():
            s_ref[...] = jnp.zeros_like(s_ref)

        s = [s_ref[i] for i in range(N_CHAINS)]
        y, s_new, tm = _chunk_pairs(s, *[_pair_tiles(ref) for ref in (r_ref, lw_ref, kd_ref, v_ref, kk_ref, a_ref)],
                                    1 - 2 * d)
        _store_tiles(y_ref, y)
        for i in range(N_CHAINS):
            s0_ref[i // N_PAIRS, i % N_PAIRS] = s[i]
            tm_ref[i // N_PAIRS, i % N_PAIRS] = tm[i].astype(BF16)
            s_ref[i] = s_new[i]

    shared, per_dir, state = _scan_specs(order)
    seq = lambda z: z.reshape(bsz, t, z.shape[1])
    res = _call("scan_fwd", body, (2, bsz // SCAN_SEQS, nc), [seq(z) for z in (r, v, kk, lw, kd, a)],
                [shared, shared, shared, per_dir, per_dir, per_dir], [per_dir, state, state],
                [jax.ShapeDtypeStruct((bsz, t, 2 * RW), F32),
                 jax.ShapeDtypeStruct((2, bsz, nc, N_PAIRS, PAIR, PAIR), F32),
                 jax.ShapeDtypeStruct((2, bsz, nc, N_PAIRS, PAIR, PAIR), BF16)],
                [pltpu.VMEM((N_CHAINS, PAIR, PAIR), F32)], exch)
    (y, s0, tm), got = res if exch else (res, None)
    y = y.reshape(n, 2 * RW)
    return ([y, s0, tm], got) if exch else [y, s0, tm]


def _scan_bwd(r, v, kk, lw, kd, a, s0, tm, dy, bsz, exch=None):
    n = r.shape[0]
    t = n // bsz
    nc = t // CHUNK

    def order(d, c):
        cc = nc - 1 - c
        return cc + d * (nc - 1 - 2 * cc)

    def body(r_ref, v_ref, kk_ref, lw_ref, kd_ref, a_ref, dy_ref, s0_ref, tm_ref,
             dr_ref, dv_ref, dkk_ref, dlw_ref, dkd_ref, da_ref, ds_ref):
        d, c = pl.program_id(0), pl.program_id(2)

        @pl.when(c == 0)
        def _():
            ds_ref[...] = jnp.zeros_like(ds_ref)

        sgn = 1 - 2 * d
        tm_known = [tm_ref[i // N_PAIRS, i % N_PAIRS].astype(F32) for i in range(N_CHAINS)]
        _, pull = jax.vjp(lambda *ops: _chunk_pairs(*ops, sgn, tm_known)[:2],
                          [s0_ref[i // N_PAIRS, i % N_PAIRS] for i in range(N_CHAINS)],
                          *[_pair_tiles(ref) for ref in (r_ref, lw_ref, kd_ref, v_ref, kk_ref, a_ref)])
        grads = pull((_pair_tiles(dy_ref), [ds_ref[i] for i in range(N_CHAINS)]))
        for i in range(N_CHAINS):
            ds_ref[i] = grads[0][i]
        for o_ref, gx in zip((dr_ref, dlw_ref, dkd_ref, dv_ref, dkk_ref, da_ref), grads[1:]):
            _store_tiles(o_ref, gx)

    shared, per_dir, state = _scan_specs(order)
    shp = jax.ShapeDtypeStruct((bsz, t, 2 * RW), F32)
    seq = lambda z: z.reshape(bsz, t, z.shape[1])
    res = _call("scan_bwd", body, (2, bsz // SCAN_SEQS, nc), [seq(z) for z in (r, v, kk, lw, kd, a, dy)] + [s0, tm],
                [shared, shared, shared, per_dir, per_dir, per_dir, per_dir, state, state],
                [per_dir] * 6, [shp] * 6, [pltpu.VMEM((N_CHAINS, PAIR, PAIR), F32)], exch)
    outs, got = res if exch else (res, None)
    outs = [z.reshape(n, 2 * RW) for z in outs]
    return (outs, got) if exch else outs


def _prep_fn(ps, w0, w2bd, a0, a2bd, g2p, k_k, k_a, hsum):
    head_sum = lambda z: _head_sum(z, hsum)
    r, k, v = ps[:, 0:RW], ps[:, RW:2 * RW], ps[:, 2 * RW:3 * RW]
    wd, ad, gd = ps[:, 1536:1664], ps[:, 1664:1792], ps[:, 1792:2048]
    logw = -DECAY_SCALE * _sigmoid(_dot1(jnp.tanh(wd), w2bd) + w0)
    a = _sigmoid(_dot1(ad, a2bd) + a0)
    g = _dot1(_sigmoid(gd), g2p)
    kkr = k * k_k
    kk = kkr / jnp.maximum(jnp.sqrt(head_sum(kkr * kkr)), NORM_EPS)
    k2 = jnp.concatenate([k, k], axis=1)
    ka2 = jnp.concatenate([k_a, k_a], axis=1)
    kd = k2 * (1.0 + (a - 1.0) * ka2)
    return r, v, kk, logw, a, kd, g


def _post_fn(y2, r, v, kd, g, lnx_g, lnx_b, r_k, hsum):
    head_sum = lambda z: _head_sum(z, hsum)
    y = y2[:, 0:RW] + y2[:, RW:2 * RW]
    mu = head_sum(y) * (1.0 / HEAD)
    yc = y - mu
    var = head_sum(yc * yc) * (1.0 / HEAD)
    yn = yc * lax.rsqrt(var + GN_EPS) * lnx_g + lnx_b
    bonus = head_sum(r * (kd[:, 0:RW] + kd[:, RW:2 * RW]) * r_k) * v
    return ((yn + bonus) * g,)


def _glu_fn(pa, pb):
    return (pa * _sigmoid(pb),)


def _conv_out_fn(cv, ln_g, ln_b):
    mu = jnp.mean(cv, axis=-1, keepdims=True)
    cc = cv - mu
    var = jnp.mean(cc * cc, axis=-1, keepdims=True)
    y = cc * lax.rsqrt(var + LN_EPS) * ln_g + ln_b
    return (y * _sigmoid(y),)


def _local_step(x, tgt, w, ex=None):
    bsz, t, d = x.shape
    n = bsz * t
    x2d, tgt2d = x.reshape(n, d), tgt.reshape(n, d)
    hsum = jnp.tile(jnp.kron(jnp.eye(N_HEADS, dtype=BF16), jnp.ones((HEAD, HEAD), BF16)), (3, 1))
    w = dict(w)
    parts = {} if ex else None

    def hosted(result, finish=None):
        if not ex:
            return result
        outs, got = result
        if finish is not None:
            w.update(finish(got))
        return outs

    xb = x2d.astype(BF16)
    hg1, hu1, act1 = hosted(_ffn_in("ffn1_in", xb, w["ffn1_w_in"], exch=(ex["g1"][0], True) if ex else None),
                            ex["g1"][1] if ex else None)
    x1, x1b, z1 = _mm_ln("ffn1_out_ln1", act1, w["ffn1_w_out"], x2d, w["ln1_g"], w["ln1_b"], 0.5, a_blocked=True)
    p = _mm("w_in_proj", x1b, w["w_in"], "nn", F32)
    p3 = p.reshape(bsz, t, IN_PAD)
    ps = _shift("shift_fwd", p3, w["mu_prev"], w["mu_next"]).reshape(n, SHIFT_PAD)
    prep_params = [w["w0"], w["w2"], w["a0"], w["a2"], w["g2"], w["k_k"], w["k_a"], hsum]
    r, v, kk, logw, a, kd, g = _rowwise(
        "rwkv_prep", _prep_fn, [ps], prep_params,
        [(RW, F32), (RW, F32), (RW, F32), (2 * RW, F32), (2 * RW, F32), (2 * RW, F32), (RW, F32)], [])
    y2, s0, tm = hosted(_scan_fwd(r, v, kk, logw, kd, a, bsz, exch=(ex["g2"][0], True) if ex else None),
                    ex["g2"][1] if ex else None)
    post_params = [w["lnx_g"], w["lnx_b"], w["r_k"], hsum]
    (y_rwkv,) = _rowwise("rwkv_post", _post_fn, [y2, r, v, kd, g], post_params, [(RW, BF16)], [])
    (u,) = _rowwise("conv_glu", _glu_fn, [(p, CW, 4), (p, CW, 5)], [], [(CW, F32)], [])
    cv = _dwconv("conv_dw", u.reshape(bsz, t, CW), w["conv_dw"], w["conv_b"], False).reshape(n, CW)
    (y_conv,) = _rowwise("conv_out", _conv_out_fn, [cv], [w["conv_ln_g"], w["conv_ln_b"]], [(CW, BF16)], [])
    ycat = jnp.concatenate([y_rwkv, y_conv], axis=1)
    x2, x2b, z2 = _mm_ln("w_out_ln2", ycat, w["w_out"], x1, w["ln2_g"], w["ln2_b"], 1.0)
    hg2, hu2, act2 = _ffn_in("ffn2_in", x2b, w["ffn2_w_in"])
    dx3, z3, loss = _mm_ln("ffn2_out_ln3", act2, w["ffn2_w_out"], x2, w["ln3_g"], w["ln3_b"], 0.5,
                           tgt=tgt2d, a_blocked=True)

    gr = {}
    dz3, dz3b, gr["ln3_g"], gr["ln3_b"] = _ln_bwd("ln3_bwd", z3, w["ln3_g"], dx3)
    dx2, _ = _ffn_bwd("ffn2", gr, dz3, dz3b, x2b, w["ffn2_w_in"], w["ffn2_w_out"], hg2, hu2, act2)
    dz2, dz2b, gr["ln2_g"], gr["ln2_b"] = _ln_bwd("ln2_bwd", z2, w["ln2_g"], dx2)
    gr["w_out"] = _mm("w_out_wgrad", ycat, dz2b, "tn", BF16, tk=512)
    dycat = _mm("w_out_dgrad", dz2b, w["w_out"], "nt", F32)
    conv_out_bwd = _vjp_of(_conv_out_fn, 3)
    dcv, gr["conv_ln_g"], gr["conv_ln_b"] = _rowwise(
        "conv_out_bwd", lambda cv_, ct_, g_, b_: conv_out_bwd(cv_, g_, b_, ct_),
        [cv, (dycat, CW, 1)], [w["conv_ln_g"], w["conv_ln_b"]], [(CW, F32)], [(1, CW), (1, CW)])
    dcv3 = dcv.reshape(bsz, t, CW)
    gr["conv_dw"], gr["conv_b"] = _dwconv_dw("conv_dw_wgrad", u.reshape(bsz, t, CW), dcv3)
    du = _dwconv("conv_dw_dgrad", dcv3, w["conv_dw"], jnp.zeros((1, CW), F32), True).reshape(n, CW)

    def glu_bwd(pa, pb, ct):
        return (jnp.concatenate(_vjp_of(_glu_fn, 2)(pa, pb, ct), axis=1),)

    (dp_conv,) = _rowwise("conv_glu_bwd", glu_bwd, [(p, CW, 4), (p, CW, 5), du], [], [(2 * CW, BF16)], [])

    def post_bwd(y2_, r_, v_, kd_, g_, ct, lg, lb, rk, hs):
        return _vjp_of(lambda *z: _post_fn(*z, hs), 8)(y2_, r_, v_, kd_, g_, lg, lb, rk, ct)

    dy2, dr_post, dv_post, dkd_post, dg, gr["lnx_g"], gr["lnx_b"], gr["r_k"] = _rowwise(
        "rwkv_post_bwd", post_bwd, [y2, r, v, kd, g, (dycat, RW, 0)], post_params,
        [(2 * RW, F32), (RW, F32), (RW, F32), (2 * RW, F32), (RW, F32)], [(1, RW), (1, RW), (1, RW)])
    sends = [jnp.concatenate(gr["ffn2_w_in"], axis=0), gr["ffn2_w_out"].reshape(N_DEV, D_FF // N_DEV, D_MODEL),
             gr["w_out"].reshape(N_DEV, D_MODEL // N_DEV, D_MODEL)]
    res = _scan_bwd(r, v, kk, logw, kd, a, s0, tm, dy2, bsz, exch=(sends, False) if ex else None)
    if ex:
        res, got = res
        parts.update(zip(("ffn2_w_in", "ffn2_w_out", "w_out"), got))
    dr_s, dv_s, dkk_s, dlw, dkd_s, da = res

    def prep_bwd(ps_, dr2, dr1, dv2, dv1, dkk2, dlw_, da_, dkd2, dkd1, dg_, *prm):
        half = lambda z: z[:, 0:RW] + z[:, RW:2 * RW]
        return _vjp_of(lambda *z: _prep_fn(*z, prm[-1]), 8)(
            ps_, *prm[:-1], half(dr2) + dr1, half(dv2) + dv1, half(dkk2), dlw_, da_, dkd2 + dkd1, dg_)

    dps, gr["w0"], gr["w2"], gr["a0"], gr["a2"], gr["g2"], gr["k_k"], gr["k_a"] = _rowwise(
        "rwkv_prep_bwd", prep_bwd,
        [ps, dr_s, dr_post, dv_s, dv_post, dkk_s, dlw, da, dkd_s, dkd_post, dg], prep_params,
        [(SHIFT_PAD, F32)], [q.shape for q in prep_params[:-1]])
    dps3 = dps.reshape(bsz, t, SHIFT_PAD)
    gr["mu_prev"], gr["mu_next"] = _shift("shift_dmu", p3, None, None, q=dps3)
    dp_shift = _shift("shift_bwd", dps3, w["mu_next"], w["mu_prev"], out_dtype=BF16).reshape(n, SHIFT_PAD)
    dp = jnp.concatenate([dp_shift, dp_conv], axis=1)
    gr["w_in"] = _mm("w_in_wgrad", x1b, dp, "tn", BF16, tk=512)
    dx1 = _mm("w_in_dgrad", dp, w["w_in"], "nt", F32, add=dz2, add_scale=ALPHA)
    dz1, dz1b, gr["ln1_g"], gr["ln1_b"] = _ln_bwd("ln1_bwd", z1, w["ln1_g"], dx1)
    riders = None
    if ex:
        gw_in = _unpad_in_cols(gr["w_in"]).reshape(D_MODEL, N_DEV, IN_COLS // N_DEV).transpose(1, 0, 2)
        small = _pack([_grad_small(nm, gr[nm]) for nm in SMALL_SHARDED + SMALL_REPL])
        riders = {"out_bwd": [gw_in], "in_wgrad": [jnp.broadcast_to(small[None], (N_DEV,) + small.shape)]}
    grad_x, got = _ffn_bwd("ffn1", gr, dz1, dz1b, xb, w["ffn1_w_in"], w["ffn1_w_out"], hg1, hu1, act1, riders)
    if ex:
        parts.update(w_in=got["out_bwd"][0], ffn1_w_out=got["in_wgrad"][0], small=got["in_wgrad"][1],
                     ffn1_w_in=got["in_dgrad"][0])
    return loss[0, 0], grad_x.reshape(bsz, t, d), gr, parts


def _ffn_in_wgrad(name, xin, dhg, dhu, tk=1024, exch=None):
    n = xin.shape[0]
    nj, nt = N_DEV // 2, n // tk

    def body(x_ref, g_ref, u_ref, og_ref, ou_ref, accg_ref, accu_ref):
        i = pl.program_id(1)

        @pl.when(i == 0)
        def _():
            accg_ref[...] = jnp.zeros_like(accg_ref)
            accu_ref[...] = jnp.zeros_like(accu_ref)

        xt = x_ref[...].astype(BF16).T
        accg_ref[...] += _dot(xt, g_ref[...])
        accu_ref[...] += _dot(xt, u_ref[...])

        @pl.when(i == nt - 1)
        def _():
            og_ref[...] = accg_ref[...].astype(og_ref.dtype)
            ou_ref[...] = accu_ref[...].astype(ou_ref.dtype)

    dh_blk = pl.BlockSpec((None, tk, FF_BLK), lambda j, i: (j, i, 0))
    o_blk = pl.BlockSpec((None, D_MODEL, FF_BLK), lambda j, i: (j, 0, 0))
    shp = jax.ShapeDtypeStruct((nj, D_MODEL, FF_BLK), BF16)
    return _call(name, body, (nj, nt), [xin, dhg, dhu],
                 [pl.BlockSpec((tk, D_MODEL), lambda j, i: (i, 0)), dh_blk, dh_blk], [o_blk, o_blk], [shp, shp],
                 [pltpu.VMEM((D_MODEL, FF_BLK), F32)] * 2, exch)


def _ffn_in_dgrad(name, dhg, dhu, wg, add, add_scale, tm=512, exch=None):
    n = add.shape[0]
    nj = N_DEV // 2

    def body(g_ref, u_ref, wgate_ref, wup_ref, add_ref, o_ref, acc_ref):
        j = pl.program_id(1)

        @pl.when(j == 0)
        def _():
            acc_ref[...] = jnp.zeros_like(acc_ref)

        acc_ref[...] += _dot(g_ref[...], wgate_ref[...], "nt") + _dot(u_ref[...], wup_ref[...], "nt")

        @pl.when(j == nj - 1)
        def _():
            o_ref[...] = acc_ref[...] + add_scale * add_ref[...]

    dh_blk = pl.BlockSpec((None, tm, FF_BLK), lambda i, j: (j, i, 0))
    row = pl.BlockSpec((tm, D_MODEL), lambda i, j: (i, 0))
    return _call(name, body, (n // tm, nj), [dhg, dhu, wg, wg, add],
                 [dh_blk, dh_blk, pl.BlockSpec((None, D_MODEL, FF_BLK), lambda i, j: (j, 0, 0)),
                  pl.BlockSpec((None, D_MODEL, FF_BLK), lambda i, j: (j + nj, 0, 0)), row],
                 row, jax.ShapeDtypeStruct((n, D_MODEL), F32), [pltpu.VMEM((tm, D_MODEL), F32)], exch)


def _ffn_bwd(tag, gr, dz, dzb, xin, wg, wout, hg, hu, act, riders=None):
    own = riders is not None

    def hosted(result):
        return result if own else (result, None)

    (dhg, dhu, gwo), got_a = hosted(_ffn_out_bwd(
        tag + "_out_bwd", dzb, wout, hg, hu, act, exch=(riders["out_bwd"], False) if own else None))
    gr[tag + "_w_out"] = gwo
    send = ([gwo.reshape(N_DEV, D_FF // N_DEV, D_MODEL)] + riders["in_wgrad"], False) if own else None
    dw, got_b = hosted(_ffn_in_wgrad(tag + "_in_wgrad", xin, dhg, dhu, exch=send))
    gr[tag + "_w_in"] = dw
    send = ([jnp.concatenate(dw, axis=0)], False) if own else None
    dx, got_c = hosted(_ffn_in_dgrad(tag + "_in_dgrad", dhg, dhu, wg, dz, ALPHA, exch=send))
    return dx, {"out_bwd": got_a, "in_wgrad": got_b, "in_dgrad": got_c}


def _adam_math(g, w, m, v):
    m = ADAM_B1 * m + (1.0 - ADAM_B1) * g
    v = ADAM_B2 * v + (1.0 - ADAM_B2) * (g * g)
    m_hat = m / (1.0 - ADAM_B1 ** ADAM_STEP)
    v_hat = v / (1.0 - ADAM_B2 ** ADAM_STEP)
    delta = -ADAM_LR * (m_hat / (jnp.sqrt(v_hat) + ADAM_EPS) + ADAM_WD * w)
    return delta, m, v


def _adam(name, parts, w, m, v, tr=128):
    rows, cols = w.shape
    tr = min(tr, rows)
    while rows % tr:
        tr -= 8

    def body(p_ref, w_ref, m_ref, v_ref, g_ref, d_ref, mo_ref, vo_ref):
        g = p_ref[0].astype(F32)
        for s in range(1, N_DEV):
            g = g + p_ref[s].astype(F32)
        g_ref[...] = g
        d_ref[...], mo_ref[...], vo_ref[...] = _adam_math(g, w_ref[...], m_ref[...], v_ref[...])

    blk = pl.BlockSpec((tr, cols), lambda i: (i, 0))
    shp = jax.ShapeDtypeStruct((rows, cols), F32)
    return _call(name, body, (rows // tr,), [parts, w, m, v],
                 [pl.BlockSpec((N_DEV, tr, cols), lambda i: (0, i, 0)), blk, blk, blk], [blk] * 4, [shp] * 4)


def _sum8(name, parts):
    _, rows, cols = parts.shape

    def body(p_ref, o_ref):
        g = p_ref[0]
        for s in range(1, N_DEV):
            g = g + p_ref[s]
        o_ref[...] = g

    return pl.pallas_call(body, name=name, out_shape=jax.ShapeDtypeStruct((rows, cols), F32),
                          compiler_params=_params())(parts)


def _adam_small(name, g, w, m, v):
    def body(g_ref, w_ref, m_ref, v_ref, d_ref, mo_ref, vo_ref):
        d_ref[...], mo_ref[...], vo_ref[...] = _adam_math(g_ref[...], w_ref[...], m_ref[...], v_ref[...])

    shp = jax.ShapeDtypeStruct(g.shape, F32)
    return pl.pallas_call(body, name=name, out_shape=[shp] * 3, compiler_params=_params())(g, w, m, v)


def _pack(arrs, lane=128):
    flat = jnp.concatenate([a.reshape(-1).astype(F32) for a in arrs])
    pad = (-flat.shape[0]) % (8 * lane)
    return jnp.pad(flat, (0, pad)).reshape(-1, lane)


def _unpack(packed, shapes):
    flat, out, off = packed.reshape(-1), [], 0
    for s in shapes:
        sz = math.prod(s)
        out.append(flat[off:off + sz].reshape(s))
        off += sz
    return out


def _pad_in_cols(wfull):
    zeros = jnp.zeros((wfull.shape[0], SHIFT_PAD - SHIFT_COLS), wfull.dtype)
    return jnp.concatenate([wfull[:, :SHIFT_COLS], zeros, wfull[:, SHIFT_COLS:]], axis=1)


def _unpad_in_cols(gfull):
    return jnp.concatenate([gfull[:, :SHIFT_COLS], gfull[:, SHIFT_PAD:]], axis=1)


def _block_diag2(wd):
    z = jnp.zeros_like(wd[0])
    return jnp.concatenate([jnp.concatenate([wd[0], z], axis=1), jnp.concatenate([z, wd[1]], axis=1)], axis=0)


def _unblock_diag2(g):
    return jnp.stack([g[0:64, 0:RW], g[64:128, RW:2 * RW]])


SMALL_SHARDED = ("w0", "w2", "a0", "a2", "g2", "conv_dw")
SMALL_REPL = ("mu_prev", "mu_next", "k_k", "k_a", "r_k", "lnx_g", "lnx_b", "conv_b", "conv_ln_g", "conv_ln_b",
              "ln1_g", "ln1_b", "ln2_g", "ln2_b", "ln3_g", "ln3_b")
BIG = ("ffn1_w_in", "ffn1_w_out", "w_in", "w_out", "ffn2_w_in", "ffn2_w_out")
WEIGHTS = ("ffn1_w_in", "ffn1_w_out", "w_in", "mu_prev", "mu_next", "w0", "w2", "a0", "a2", "g2", "k_k", "k_a",
           "r_k", "lnx_g", "lnx_b", "conv_dw", "conv_b", "conv_ln_g", "conv_ln_b", "w_out", "ffn2_w_in",
           "ffn2_w_out", "ln1_g", "ln1_b", "ln2_g", "ln2_b", "ln3_g", "ln3_b")


def _full_small(name, full):
    if name in ("w0", "a0"):
        return full.reshape(1, 2 * RW)
    if name in ("w2", "a2"):
        return _block_diag2(full)
    if name == "g2":
        return jnp.pad(full, ((0, 256 - GATE_LORA), (0, 0)))
    if name == "conv_dw":
        return jnp.pad(full, ((0, 1), (0, 0)))
    if name in ("mu_prev", "mu_next"):
        return jnp.pad(full.reshape(1, SHIFT_COLS), ((0, 0), (0, SHIFT_PAD - SHIFT_COLS)))
    return full.reshape(1, -1)


def _grad_small(name, g):
    if name in ("w0", "a0"):
        return g.reshape(2, RW)
    if name in ("w2", "a2"):
        return _unblock_diag2(g)
    if name == "g2":
        return g[:GATE_LORA]
    if name == "conv_dw":
        return g[:CONV_K]
    if name in ("mu_prev", "mu_next"):
        return g[0, :SHIFT_COLS]
    if name == "r_k":
        return g.reshape(N_HEADS, HEAD)
    return g.reshape(-1)


def kernel(x, ffn1_w_in, ffn1_w_out, w_in, mu_prev, mu_next, w0, w2, a0, a2, g2, k_k, k_a, r_k, lnx_g, lnx_b, conv_dw, conv_b, conv_ln_g, conv_ln_b, w_out, ffn2_w_in, ffn2_w_out, ln1_g, ln1_b, ln2_g, ln2_b, ln3_g, ln3_b, loss_target, m_ffn1_w_in, m_ffn1_w_out, m_w_in, m_mu_prev, m_mu_next, m_w0, m_w2, m_a0, m_a2, m_g2, m_k_k, m_k_a, m_r_k, m_lnx_g, m_lnx_b, m_conv_dw, m_conv_b, m_conv_ln_g, m_conv_ln_b, m_w_out, m_ffn2_w_in, m_ffn2_w_out, m_ln1_g, m_ln1_b, m_ln2_g, m_ln2_b, m_ln3_g, m_ln3_b, v_ffn1_w_in, v_ffn1_w_out, v_w_in, v_mu_prev, v_mu_next, v_w0, v_w2, v_a0, v_a2, v_g2, v_k_k, v_k_a, v_r_k, v_lnx_g, v_lnx_b, v_conv_dw, v_conv_b, v_conv_ln_g, v_conv_ln_b, v_w_out, v_ffn2_w_in, v_ffn2_w_out, v_ln1_g, v_ln1_b, v_ln2_g, v_ln2_b, v_ln3_g, v_ln3_b):
    args = dict(locals())
    drop = lambda z: z.reshape(z.shape[1:])
    wsh = {n: drop(args[n]) for n in WEIGHTS}
    msh = {n: drop(args["m_" + n]) for n in WEIGHTS}
    vsh = {n: drop(args["v_" + n]) for n in WEIGHTS}
    me = 4 * lax.axis_index("x") + 2 * lax.axis_index("y") + lax.axis_index("c")
    bf = {n: wsh[n].astype(BF16) for n in BIG}

    w = {"ffn1_w_in": _gather_two_level("gather_ffn1_w_in", bf["ffn1_w_in"])}
    for n in SMALL_REPL:
        w[n] = _full_small(n, wsh[n])
    small_shapes = [wsh[n].shape for n in SMALL_SHARDED]

    def finish1(got):
        f1_out, w_in_g, small = got
        cols = zip(*[_unpack(small[dv], small_shapes) for dv in range(N_DEV)])
        out = {n: _full_small(n, jnp.concatenate(s, axis=-1)) for n, s in zip(SMALL_SHARDED, cols)}
        out["ffn1_w_out"] = f1_out.reshape(D_FF, D_MODEL)
        out["w_in"] = _pad_in_cols(w_in_g.transpose(1, 0, 2).reshape(D_MODEL, IN_COLS))
        return out

    def finish2(got):
        w_out_g, f2_in, f2_out = got
        return {"w_out": w_out_g.reshape(D_MODEL, D_MODEL), "ffn2_w_in": f2_in,
                "ffn2_w_out": f2_out.reshape(D_FF, D_MODEL)}

    ex = {"g1": ([bf["ffn1_w_out"], bf["w_in"], _pack([wsh[n] for n in SMALL_SHARDED])], finish1),
          "g2": ([bf["w_out"], bf["ffn2_w_in"], bf["ffn2_w_out"]], finish2)}
    loss_part, grad_x, gr, parts = _local_step(x, loss_target, w, ex)
    loss = lax.psum(loss_part, ("x", "y", "c"))

    out = {n: _adam("adam_" + n, parts[n], wsh[n], msh[n], vsh[n]) for n in BIG}
    small_names = SMALL_SHARDED + SMALL_REPL
    full_shapes = [_grad_small(n, gr[n]).shape for n in small_names]
    summed = _unpack(_sum8("sum_small_grads", parts["small"]), full_shapes)
    mine = []
    for n, g in zip(small_names, summed):
        if n in SMALL_SHARDED:
            g = lax.dynamic_slice_in_dim(g, me * HEAD, HEAD, axis=g.ndim - 1)
        mine.append(g)
    shapes = [g.shape for g in mine]
    d_s, m_s, v_s = _adam_small("adam_small", _pack(mine), _pack([wsh[n] for n in small_names]),
                                _pack([msh[n] for n in small_names]), _pack([vsh[n] for n in small_names]))
    for n, g, dl, mn, vn in zip(small_names, mine, _unpack(d_s, shapes), _unpack(m_s, shapes), _unpack(v_s, shapes)):
        out[n] = (g, dl, mn, vn)

    res = [loss, grad_x]
    for k in range(4):
        res += [out[n][k].reshape((1,) + out[n][k].shape) for n in WEIGHTS]
    return tuple(res)
```

```python
import functools
import math

import jax
import jax.numpy as jnp
from jax import lax
from jax.experimental import pallas as pl
from jax.experimental.pallas import tpu as pltpu

F32 = jnp.float32
BF16 = jnp.bfloat16

N_DEV = 8
D_MODEL = 1024
RW = 512
N_HEADS = 8
HEAD = 64
CW = 512
CONV_K = 31
D_FF = 2816
FF_BLK = 704
GATE_LORA = 160
SHIFT_COLS = 1952
SHIFT_PAD = 2048
IN_COLS = 2976
IN_PAD = 3072
LN_EPS = 1e-5
GN_EPS = 64e-5
NORM_EPS = 1e-12
ALPHA = 2.0 ** 0.25
DECAY_SCALE = math.exp(-0.5)
CHUNK = 64
ADAM_LR, ADAM_B1, ADAM_B2, ADAM_EPS, ADAM_WD, ADAM_STEP = 0.001, 0.9, 0.999, 1e-8, 0.01, 10
VMEM_LIMIT = 56 * 1024 * 1024
MXU_DIM = 256

_DN = {"nn": (((1,), (0,)), ((), ())), "nt": (((1,), (1,)), ((), ())), "tn": (((0,), (0,)), ((), ()))}


def _params():
    return pltpu.CompilerParams(vmem_limit_bytes=VMEM_LIMIT)


def _dot(a, b, dims="nn"):
    return lax.dot_general(a, b, _DN[dims], preferred_element_type=F32)


def _split(x):
    hi = x.astype(BF16)
    return hi, (x - hi.astype(F32)).astype(BF16)


def _dot3_impl(a, b, dims):
    ah, al = _split(a)
    bh, bl = _split(b)
    ka, kb = _DN[dims][0][0][0], _DN[dims][0][1][0]
    return _dot(jnp.concatenate([ah, ah, al], axis=ka), jnp.concatenate([bh, bl, bh], axis=kb), dims)


@functools.partial(jax.custom_vjp, nondiff_argnums=(2,))
def _dot3(a, b, dims="nn"):
    return _dot3_impl(a, b, dims)


def _dot3_fwd(a, b, dims):
    return _dot3_impl(a, b, dims), (a, b)


def _dot2_ct(x, y, dims, ct_left):
    ka, kb = _DN[dims][0][0][0], _DN[dims][0][1][0]
    if ct_left:
        c, (h, l) = x.astype(BF16), _split(y)
        return _dot(jnp.concatenate([c, c], axis=ka), jnp.concatenate([h, l], axis=kb), dims)
    (h, l), c = _split(x), y.astype(BF16)
    return _dot(jnp.concatenate([h, l], axis=ka), jnp.concatenate([c, c], axis=kb), dims)


def _dot3_bwd(dims, res, ct):
    a, b = res
    if dims == "nn":
        return _dot2_ct(ct, b, "nt", True), _dot2_ct(a, ct, "tn", False)
    if dims == "nt":
        return _dot2_ct(ct, b, "nn", True), _dot2_ct(ct, a, "tn", True)
    return _dot2_ct(b, ct, "nt", False), _dot2_ct(a, ct, "nn", False)


_dot3.defvjp(_dot3_fwd, _dot3_bwd)


def _dot1_impl(a, b, dims):
    return _dot(a.astype(BF16), b.astype(BF16), dims)


@functools.partial(jax.custom_vjp, nondiff_argnums=(2,))
def _dot1(a, b, dims="nn"):
    return _dot1_impl(a, b, dims)


def _dot1_bwd(dims, res, ct):
    a, b = res
    if dims == "nn":
        return _dot1_impl(ct, b, "nt"), _dot1_impl(a, ct, "tn")
    if dims == "nt":
        return _dot1_impl(ct, b, "nn"), _dot1_impl(ct, a, "tn")
    return _dot1_impl(b, ct, "nt"), _dot1_impl(a, ct, "nn")


_dot1.defvjp(lambda a, b, dims: (_dot1_impl(a, b, dims), (a, b)), _dot1_bwd)


def _dot1_two(a, b, dims="nn"):
    ax_a, ax_b = {"nn": (0, 1), "nt": (0, 0), "tn": (1, 1)}[dims]
    shallow = dims != "tn" and 2 * a[0].shape[1] <= MXU_DIM
    out = []
    for i in range(0, len(a), 2):
        if shallow:
            z = jnp.zeros_like(b[i])
            bd = jnp.concatenate([jnp.concatenate([b[i], z], axis=1), jnp.concatenate([z, b[i + 1]], axis=1)], axis=0)
            r = _dot1(jnp.concatenate(a[i:i + 2], axis=1), bd, dims)
            n = r.shape[1] // 2
            out += [r[:, :n], r[:, n:]]
        else:
            r = _dot1(jnp.concatenate(a[i:i + 2], axis=ax_a), jnp.concatenate(b[i:i + 2], axis=ax_b), dims)
            m, n = r.shape[0] // 2, r.shape[1] // 2
            out += [r[:m, :n], r[m:, n:]]
    return out


def _tri_inv_impl(l, eye):
    steps = int(math.log2(CHUNK)) - 1
    m = l[0].shape[0]
    tm = [eye + x for x in l]
    lp = _dot1_two(l, l)
    for k in range(steps):
        if k < steps - 1:
            both = _dot1_two([jnp.concatenate([t, p], axis=0) for t, p in zip(tm, lp)], lp)
            tm = [t + b[:m] for t, b in zip(tm, both)]
            lp = [b[m:] for b in both]
        else:
            tm = [t + x for t, x in zip(tm, _dot1_two(tm, lp))]
    return tm


@jax.custom_vjp
def _tri_inv(l, eye):
    return _tri_inv_impl(l, eye)


def _tri_inv_fwd(l, eye):
    tm = _tri_inv_impl(l, eye)
    return tm, (tm, eye)


def _tri_inv_bwd(res, ct):
    tm, eye = res
    return _dot1_two(_dot1_two(tm, ct, "tn"), tm, "nt"), jnp.zeros_like(eye)


_tri_inv.defvjp(_tri_inv_fwd, _tri_inv_bwd)


@jax.custom_vjp
def _tri_inv_known(l, tm):
    return tm


_tri_inv_known.defvjp(lambda l, tm: (tm, tm),
                      lambda tm, ct: (_dot1_two(_dot1_two(tm, ct, "tn"), tm, "nt"), [jnp.zeros_like(t) for t in tm]))


def _ones_impl(x, g3):
    x1 = x.astype(BF16)
    r1 = x - x1.astype(F32)
    x2 = r1.astype(BF16)
    x3 = (r1 - x2.astype(F32)).astype(BF16)
    return _dot(jnp.concatenate([x1, x2, x3], axis=1), g3)


@jax.custom_vjp
def _head_sum(x, g3):
    return _ones_impl(x, g3)


_head_sum.defvjp(lambda x, g3: (_ones_impl(x, g3), g3), lambda g3, ct: (_ones_impl(ct, g3), jnp.zeros_like(g3)))


def _prefix_sum(x):
    row = lax.broadcasted_iota(jnp.int32, x.shape, 0)
    sh = 1
    while sh < x.shape[0]:
        x = x + jnp.where(row >= sh, pltpu.roll(x, sh, 0), 0.0)
        sh *= 2
    return x


def _dir_cumsum_impl(x, sgn):
    pre = _prefix_sum(x)
    return jnp.where(sgn > 0.0, pre, jnp.sum(x, axis=0, keepdims=True) - pre + x)


@jax.custom_vjp
def _dir_cumsum(x, sgn):
    return _dir_cumsum_impl(x, sgn)


_dir_cumsum.defvjp(lambda x, sgn: (_dir_cumsum_impl(x, sgn), sgn),
                   lambda sgn, ct: (_dir_cumsum_impl(ct, -sgn), jnp.zeros_like(sgn)))


def _sigmoid(x):
    return 1.0 / (1.0 + jnp.exp(-x))


def _mesh_pos():
    return lax.axis_index("x"), lax.axis_index("y"), lax.axis_index("c")


def _peer(pos, q):
    x, y, c = pos
    return (1 - x if q & 4 else x, 1 - y if q & 2 else y, 1 - c if q & 1 else c)


def _linear(pos):
    return 4 * pos[0] + 2 * pos[1] + pos[2]


def _exchange_copies(x_refs, o_refs, send_sems, recv_sems, local_sems, gather):
    pos = _mesh_pos()
    me = _linear(pos)
    starts, wait_recv, wait_send, wait_local = [], [], [], []
    for t in range(len(x_refs)):
        src = x_refs[t] if gather else x_refs[t].at[me]
        cp = pltpu.make_async_copy(src, o_refs[t].at[me], local_sems.at[t])
        starts.append(cp.start)
        wait_local.append(cp.wait)
    for q in range(1, N_DEV):
        peer = _peer(pos, q)
        for t in range(len(x_refs)):
            src = x_refs[t] if gather else x_refs[t].at[_linear(peer)]
            sems = dict(send_sem=send_sems.at[t, q - 1], recv_sem=recv_sems.at[t, q - 1],
                        device_id=peer, device_id_type=pl.DeviceIdType.MESH)
            send = pltpu.make_async_remote_copy(src_ref=src, dst_ref=o_refs[t].at[me], **sems)
            recv = pltpu.make_async_remote_copy(src_ref=src, dst_ref=o_refs[t].at[_linear(peer)], **sems)
            starts.append(send.start)
            wait_recv.append(recv.wait_recv)
            wait_send.append(send.wait_send)
    return starts, wait_recv + wait_send + wait_local


def _exchange_shapes(xs, gather):
    return [jax.ShapeDtypeStruct((N_DEV,) + (x.shape if gather else x.shape[1:]), x.dtype) for x in xs]


def _exchange_sems(nt):
    return [pltpu.SemaphoreType.DMA((nt, N_DEV - 1)), pltpu.SemaphoreType.DMA((nt, N_DEV - 1)),
            pltpu.SemaphoreType.DMA((nt,))]


def _gather_two_level(name, x):
    def body(x_ref, out_ref, send_sems, recv_sems, local_sem):
        px, py, pc = _mesh_pos()
        me, sibling = (px, py, pc), (px, py, 1 - pc)
        chips = [(1 - px, py), (px, 1 - py), (1 - px, 1 - py)]

        def slot(pos):
            return out_ref.at[_linear(pos)]

        def copy(k, block, to, src=None):
            return pltpu.make_async_remote_copy(
                src_ref=slot(block) if src is None else src, dst_ref=slot(block), send_sem=send_sems.at[k],
                recv_sem=recv_sems.at[k], device_id=to, device_id_type=pl.DeviceIdType.MESH)

        mine = pltpu.make_async_copy(x_ref, slot(me), local_sem)
        mine.start()
        first = [copy(0, me, sibling, src=x_ref)]
        first += [copy(1 + j, me, (*chip, pc), src=x_ref) for j, chip in enumerate(chips)]
        for cp in first:
            cp.start()
        passed = [copy(4 + j, (*chip, pc), sibling) for j, chip in enumerate(chips)]
        for j, chip in enumerate(chips):
            copy(1 + j, (*chip, pc), me).wait_recv()
            passed[j].start()
        copy(0, sibling, me).wait_recv()
        for j, chip in enumerate(chips):
            copy(4 + j, (*chip, 1 - pc), me).wait_recv()
        for cp in first + passed:
            cp.wait_send()
        mine.wait()

    any_spec = pl.BlockSpec(memory_space=pl.ANY)
    return pl.pallas_call(
        body, name=name, in_specs=[any_spec], out_specs=any_spec,
        out_shape=jax.ShapeDtypeStruct((N_DEV,) + x.shape, x.dtype),
        scratch_shapes=[pltpu.SemaphoreType.DMA((N_DEV - 1,)), pltpu.SemaphoreType.DMA((N_DEV - 1,)),
                        pltpu.SemaphoreType.DMA])(x)


def _call(name, body, grid, ins, in_specs, out_specs, out_shape, scratch=(), exch=None):
    if exch is None:
        return pl.pallas_call(body, name=name, grid=grid, in_specs=in_specs, out_specs=out_specs,
                              out_shape=out_shape, scratch_shapes=list(scratch), compiler_params=_params())(*ins)
    xs, gather = exch
    single = not isinstance(out_shape, (list, tuple))
    o_specs = [out_specs] if single else list(out_specs)
    o_shape = [out_shape] if single else list(out_shape)
    n_in, n_out, n_x, n_scr = len(ins), len(o_shape), len(xs), len(scratch)

    def wrapped(*refs):
        in_refs = refs[:n_in]
        x_refs = refs[n_in:n_in + n_x]
        out_refs = refs[n_in + n_x:n_in + n_x + n_out]
        got_refs = refs[n_in + n_x + n_out:n_in + 2 * n_x + n_out]
        rest = refs[n_in + 2 * n_x + n_out:]
        starts, waits = _exchange_copies(x_refs, got_refs, *rest[n_scr:], gather)
        ids = [pl.program_id(i) for i in range(len(grid))]
        first = functools.reduce(lambda p, q: p & q, [i == 0 for i in ids])
        last = functools.reduce(lambda p, q: p & q, [i == g - 1 for i, g in zip(ids, grid)])

        @pl.when(first)
        def _():
            for f in starts:
                f()

        body(*in_refs, *out_refs, *rest[:n_scr])

        @pl.when(last)
        def _():
            for f in waits:
                f()

    any_spec = pl.BlockSpec(memory_space=pl.ANY)
    outs = pl.pallas_call(
        wrapped, name=name, grid=grid, in_specs=list(in_specs) + [any_spec] * n_x,
        out_specs=o_specs + [any_spec] * n_x, out_shape=o_shape + _exchange_shapes(xs, gather),
        scratch_shapes=list(scratch) + _exchange_sems(n_x), compiler_params=_params())(*ins, *xs)
    res = outs[:n_out]
    return (res[0] if single else res), outs[n_out:]


def _mm_call(name, dims, grid, red_axis, ins, in_specs, out_shape, out_spec, acc_shape,
             scale=1.0, add_scale=None, exch=None):
    nred = grid[red_axis]

    def body(*refs):
        if add_scale is None:
            a_ref, b_ref, o_ref, acc_ref = refs
            add_ref = None
        else:
            a_ref, b_ref, add_ref, o_ref, acc_ref = refs
        k = pl.program_id(red_axis)

        @pl.when(k == 0)
        def _():
            acc_ref[...] = jnp.zeros_like(acc_ref)

        acc_ref[...] += _dot(a_ref[...].astype(BF16), b_ref[...].astype(BF16), dims)

        @pl.when(k == nred - 1)
        def _():
            r = acc_ref[...]
            if scale != 1.0:
                r = r * scale
            if add_ref is not None:
                r = r + add_scale * add_ref[...].astype(F32)
            o_ref[...] = r.astype(o_ref.dtype)

    return _call(name, body, grid, ins, in_specs, out_spec, out_shape, [pltpu.VMEM(acc_shape, F32)], exch)


def _tile(dim, cap):
    t = min(dim, cap)
    while dim % t or t % 128:
        t -= 128
        assert t > 0, (dim, cap)
    return t


def _mm(name, a, b, dims, out_dtype, scale=1.0, add=None, add_scale=None, tm=512, tn=1024, tk=1024):
    if dims == "tn":
        kd, m = a.shape
        n = b.shape[1]
    else:
        m, kd = a.shape
        n = b.shape[1] if dims == "nn" else b.shape[0]
    tm, tn, tk = _tile(m, tm), _tile(n, tn), _tile(kd, tk)
    a_spec = (pl.BlockSpec((tk, tm), lambda i, j, k: (k, i)) if dims == "tn"
              else pl.BlockSpec((tm, tk), lambda i, j, k: (i, k)))
    b_spec = (pl.BlockSpec((tn, tk), lambda i, j, k: (j, k)) if dims == "nt"
              else pl.BlockSpec((tk, tn), lambda i, j, k: (k, j)))
    o_spec = pl.BlockSpec((tm, tn), lambda i, j, k: (i, j))
    ins, specs = [a, b], [a_spec, b_spec]
    if add is not None:
        ins.append(add)
        specs.append(o_spec)
    return _mm_call(name, dims, (m // tm, n // tn, kd // tk), 2, ins, specs,
                    jax.ShapeDtypeStruct((m, n), out_dtype), o_spec, (tm, tn),
                    scale=scale, add_scale=add_scale if add is not None else None)


def _ffn_in(name, x, wg, tm=512, exch=None):
    n = x.shape[0]
    nj = N_DEV // 2

    def body(x_ref, wgate_ref, wup_ref, hg_ref, hu_ref, act_ref):
        xb = x_ref[...].astype(BF16)
        g = _dot(xb, wgate_ref[...])
        u = _dot(xb, wup_ref[...])
        hg_ref[...] = g.astype(BF16)
        hu_ref[...] = u.astype(BF16)
        act_ref[...] = (g * _sigmoid(g) * u).astype(BF16)

    blk = pl.BlockSpec((None, tm, FF_BLK), lambda j, i: (j, i, 0))
    shp = jax.ShapeDtypeStruct((nj, n, FF_BLK), BF16)
    return _call(name, body, (nj, n // tm), [x, wg, wg],
                 [pl.BlockSpec((tm, D_MODEL), lambda j, i: (i, 0)),
                  pl.BlockSpec((None, D_MODEL, FF_BLK), lambda j, i: (j, 0, 0)),
                  pl.BlockSpec((None, D_MODEL, FF_BLK), lambda j, i: (j + nj, 0, 0))],
                 [blk, blk, blk], [shp, shp, shp], exch=exch)


def _ffn_out_bwd(name, dz, wout, hg, hu, act, tm=512, exch=None):
    n = dz.shape[0]
    nj, ni = N_DEV // 2, n // tm

    def body(dz_ref, w_ref, hg_ref, hu_ref, act_ref, dhg_ref, dhu_ref, dw_ref, acc_ref):
        i = pl.program_id(1)
        dzb = dz_ref[...].astype(BF16)
        dact = 0.5 * _dot(dzb, w_ref[...], "nt")
        g = hg_ref[...].astype(F32)
        u = hu_ref[...].astype(F32)
        s = _sigmoid(g)
        dhg_ref[...] = (dact * u * (s * (1.0 + g * (1.0 - s)))).astype(BF16)
        dhu_ref[...] = (dact * (g * s)).astype(BF16)

        @pl.when(i == 0)
        def _():
            acc_ref[...] = jnp.zeros_like(acc_ref)

        acc_ref[...] += _dot(act_ref[...], dzb, "tn")

        @pl.when(i == ni - 1)
        def _():
            dw_ref[...] = (0.5 * acc_ref[...]).astype(dw_ref.dtype)

    blk = pl.BlockSpec((None, tm, FF_BLK), lambda j, i: (j, i, 0))
    wblk = pl.BlockSpec((FF_BLK, D_MODEL), lambda j, i: (j, 0))
    shp = jax.ShapeDtypeStruct((nj, n, FF_BLK), BF16)
    return _call(name, body, (nj, ni), [dz, wout, hg, hu, act],
                 [pl.BlockSpec((tm, D_MODEL), lambda j, i: (i, 0)), wblk, blk, blk, blk],
                 [blk, blk, wblk], [shp, shp, jax.ShapeDtypeStruct((D_FF, D_MODEL), BF16)],
                 [pltpu.VMEM((FF_BLK, D_MODEL), F32)], exch=exch)


def _mm_ln(name, a, b, xres, g, beta, c, tgt=None, a_blocked=False, tm=512, tk=512):
    if a_blocked:
        nk, n, kb = a.shape
        a_spec = pl.BlockSpec((None, tm, kb), lambda i, k: (k, i, 0))
    else:
        n, kd = a.shape
        kb = _tile(kd, tk)
        nk = kd // kb
        a_spec = pl.BlockSpec((tm, kb), lambda i, k: (i, k))
    d = b.shape[1]
    with_loss = tgt is not None

    def body(*refs):
        if with_loss:
            a_ref, b_ref, x_ref, g_ref, be_ref, t_ref, o_ref, z_ref, l_ref, acc_ref = refs
        else:
            a_ref, b_ref, x_ref, g_ref, be_ref, o_ref, ob_ref, z_ref, acc_ref = refs
        i, k = pl.program_id(0), pl.program_id(1)

        @pl.when(k == 0)
        def _():
            acc_ref[...] = jnp.zeros_like(acc_ref)

        acc_ref[...] += _dot(a_ref[...].astype(BF16), b_ref[...].astype(BF16))

        @pl.when(k == nk - 1)
        def _():
            z = ALPHA * x_ref[...] + c * acc_ref[...]
            z_ref[...] = z
            mu = jnp.mean(z, axis=-1, keepdims=True)
            zc = z - mu
            var = jnp.mean(zc * zc, axis=-1, keepdims=True)
            y = zc * lax.rsqrt(var + LN_EPS) * g_ref[...] + be_ref[...]
            if with_loss:
                err = y - t_ref[...]
                o_ref[...] = err * (1.0 / d)
                part = 0.5 * jnp.sum(jnp.sum(err * err, axis=-1, keepdims=True), axis=0, keepdims=True) * (1.0 / d)

                @pl.when(i == 0)
                def _():
                    l_ref[...] = jnp.zeros_like(l_ref)

                l_ref[...] += jnp.broadcast_to(part, l_ref.shape)
            else:
                o_ref[...] = y
                ob_ref[...] = y.astype(BF16)

    row = pl.BlockSpec((tm, d), lambda i, k: (i, 0))
    vec = pl.BlockSpec((1, d), lambda i, k: (0, 0))
    ins = [a, b, xres, g, beta]
    in_specs = [a_spec, pl.BlockSpec((kb, d), lambda i, k: (k, 0)), row, vec, vec]
    out_specs = [row, row]
    out_shape = [jax.ShapeDtypeStruct((n, d), F32), jax.ShapeDtypeStruct((n, d), F32)]
    if not with_loss:
        out_specs.insert(1, row)
        out_shape.insert(1, jax.ShapeDtypeStruct((n, d), BF16))
    if with_loss:
        ins.append(tgt)
        in_specs.append(row)
        out_specs.append(pl.BlockSpec((1, 128), lambda i, k: (0, 0)))
        out_shape.append(jax.ShapeDtypeStruct((1, 128), F32))
    return _call(name, body, (n // tm, nk), ins, in_specs, out_specs, out_shape, [pltpu.VMEM((tm, d), F32)])


def _rowwise(name, fn, rows, params, out_rows, out_accs, tm=256, exch=None):
    specs, ins = [], []
    for r in rows:
        arr, w, cb = r if isinstance(r, tuple) else (r, r.shape[1], 0)
        ins.append(arr)
        specs.append(pl.BlockSpec((tm, w), functools.partial(lambda i, cb: (i, cb), cb=cb)))
    n = ins[0].shape[0]
    for p in params:
        ins.append(p)
        specs.append(pl.BlockSpec(p.shape, lambda i: (0, 0)))
    n_in, n_or = len(ins), len(out_rows)

    def body(*refs):
        outs = fn(*[r[...] for r in refs[:n_in]])
        o_refs = refs[n_in:]
        for o_ref, o in zip(o_refs[:n_or], outs[:n_or]):
            o_ref[...] = o.astype(o_ref.dtype)
        if out_accs:
            @pl.when(pl.program_id(0) == 0)
            def _():
                for a_ref in o_refs[n_or:]:
                    a_ref[...] = jnp.zeros_like(a_ref)

            for a_ref, a in zip(o_refs[n_or:], outs[n_or:]):
                a_ref[...] += a.astype(F32)

    out_specs = [pl.BlockSpec((tm, w), lambda i: (i, 0)) for w, _ in out_rows]
    out_specs += [pl.BlockSpec(s, lambda i: (0, 0)) for s in out_accs]
    out_shape = [jax.ShapeDtypeStruct((n, w), dt) for w, dt in out_rows]
    out_shape += [jax.ShapeDtypeStruct(s, F32) for s in out_accs]
    return _call(name, body, (n // tm,), ins, specs, out_specs, out_shape, exch=exch)


def _vjp_of(fn, n_in):
    def g(*args):
        ins, cts = args[:n_in], args[n_in:]
        outs, pull = jax.vjp(fn, *ins)
        return pull(tuple(c.astype(o.dtype) for c, o in zip(cts, outs)))
    return g


def _ln_bwd(name, z, g, ct):
    def fn(zt, ct_, gt):
        mu = jnp.mean(zt, axis=-1, keepdims=True)
        zc = zt - mu
        rstd = lax.rsqrt(jnp.mean(zc * zc, axis=-1, keepdims=True) + LN_EPS)
        xh = zc * rstd
        dxh = ct_ * gt
        dz = rstd * (dxh - jnp.mean(dxh, axis=-1, keepdims=True)
                     - xh * jnp.mean(dxh * xh, axis=-1, keepdims=True))
        return dz, dz, jnp.sum(ct_ * xh, axis=0, keepdims=True), jnp.sum(ct_, axis=0, keepdims=True)

    d = z.shape[1]
    return _rowwise(name, fn, [z, ct], [g], [(d, F32), (d, BF16)], [(1, d), (1, d)])


def _shift(name, src, mu_a, mu_b, q=None, tt=256, out_dtype=F32):
    bsz, t, _ = src.shape
    nt, r8, w = t // tt, tt // 8, SHIFT_PAD
    with_q = q is not None

    def body(cur_ref, prev_ref, next_ref, *rest):
        b, i = pl.program_id(0), pl.program_id(1)
        cur = cur_ref[...]
        prow = jnp.where(i > 0, prev_ref[7:8, :], 0.0)
        nrow = jnp.where(i < nt - 1, next_ref[0:1, :], 0.0)
        rid = lax.broadcasted_iota(jnp.int32, cur.shape, 0)
        dprev = jnp.where(rid == 0, prow, pltpu.roll(cur, 1, 0)) - cur
        dnext = jnp.where(rid == tt - 1, nrow, pltpu.roll(cur, tt - 1, 0)) - cur
        if with_q:
            q_ref, da_ref, db_ref = rest

            @pl.when((b == 0) & (i == 0))
            def _():
                da_ref[...] = jnp.zeros_like(da_ref)
                db_ref[...] = jnp.zeros_like(db_ref)

            qv = q_ref[...]
            da_ref[...] += jnp.sum(qv * dprev, axis=0, keepdims=True)
            db_ref[...] += jnp.sum(qv * dnext, axis=0, keepdims=True)
        else:
            ma_ref, mb_ref, o_ref = rest
            o_ref[...] = (cur + ma_ref[...] * dprev + mb_ref[...] * dnext).astype(o_ref.dtype)

    cur_spec = pl.BlockSpec((None, tt, w), lambda b, i: (b, i, 0))
    in_specs = [cur_spec,
                pl.BlockSpec((None, 8, w), lambda b, i: (b, jnp.maximum(i * r8 - 1, 0), 0)),
                pl.BlockSpec((None, 8, w), lambda b, i: (b, jnp.minimum((i + 1) * r8, t // 8 - 1), 0))]
    vec = pl.BlockSpec((1, w), lambda b, i: (0, 0))
    if with_q:
        return _call(name, body, (bsz, nt), [src, src, src, q], in_specs + [cur_spec], [vec, vec],
                     [jax.ShapeDtypeStruct((1, w), F32)] * 2)
    return _call(name, body, (bsz, nt), [src, src, src, mu_a, mu_b], in_specs + [vec, vec], cur_spec,
                 jax.ShapeDtypeStruct((bsz, t, w), out_dtype))


CONV_BLK = 128


def _halo_specs(t, tt, w):
    r16 = tt // 16
    return [pl.BlockSpec((None, tt, w), lambda b, i: (b, i, 0)),
            pl.BlockSpec((None, 16, w), lambda b, i: (b, jnp.maximum(i * r16 - 1, 0), 0)),
            pl.BlockSpec((None, 16, w), lambda b, i: (b, jnp.minimum((i + 1) * r16, t // 16 - 1), 0))]


def _fill_pad(pad_ref, cur_ref, prev_ref, next_ref, i, nt, tt):
    pad_ref[0:16, :] = jnp.where(i > 0, prev_ref[...], 0.0)
    pad_ref[16:16 + tt, :] = cur_ref[...]
    pad_ref[16 + tt:32 + tt, :] = jnp.where(i < nt - 1, next_ref[...], 0.0)


def _dwconv(name, u, dw32, bias, flip, tt=512):
    bsz, t, w = u.shape
    tt = min(tt, t)
    nt = t // tt

    def body(cur_ref, prev_ref, next_ref, dw_ref, b_ref, o_ref, pad_ref):
        i = pl.program_id(1)
        _fill_pad(pad_ref, cur_ref, prev_ref, next_ref, i, nt, tt)
        for r0 in range(0, tt, CONV_BLK):
            for cs in (slice(c0, c0 + CONV_BLK) for c0 in range(0, w, CONV_BLK)):
                acc = jnp.broadcast_to(b_ref[:, cs], (CONV_BLK, CONV_BLK))
                for k in range(CONV_K):
                    kk = CONV_K - 1 - k if flip else k
                    acc = acc + pad_ref[pl.ds(r0 + 1 + k, CONV_BLK), cs] * dw_ref[kk:kk + 1, cs]
                o_ref[r0:r0 + CONV_BLK, cs] = acc

    return _call(name, body, (bsz, nt), [u, u, u, dw32, bias],
                 _halo_specs(t, tt, w) + [pl.BlockSpec((32, w), lambda b, i: (0, 0)),
                                          pl.BlockSpec((1, w), lambda b, i: (0, 0))],
                 pl.BlockSpec((None, tt, w), lambda b, i: (b, i, 0)), jax.ShapeDtypeStruct((bsz, t, w), F32),
                 [pltpu.VMEM((tt + 32, w), F32)])


def _dwconv_dw(name, u, dc, tt=512):
    bsz, t, w = u.shape
    tt = min(tt, t)
    nt = t // tt

    def body(cur_ref, prev_ref, next_ref, dc_ref, ddw_ref, db_ref, pad_ref):
        b, i = pl.program_id(0), pl.program_id(1)
        _fill_pad(pad_ref, cur_ref, prev_ref, next_ref, i, nt, tt)

        @pl.when((b == 0) & (i == 0))
        def _():
            ddw_ref[...] = jnp.zeros_like(ddw_ref)
            db_ref[...] = jnp.zeros_like(db_ref)

        dcv = dc_ref[...]
        db_ref[...] += jnp.sum(dcv, axis=0, keepdims=True)
        for k in range(CONV_K):
            ddw_ref[k:k + 1, :] += jnp.sum(dcv * pad_ref[pl.ds(1 + k, tt), :], axis=0, keepdims=True)

    return _call(name, body, (bsz, nt), [u, u, u, dc],
                 _halo_specs(t, tt, w) + [pl.BlockSpec((None, tt, w), lambda b, i: (b, i, 0))],
                 [pl.BlockSpec((32, w), lambda b, i: (0, 0)), pl.BlockSpec((1, w), lambda b, i: (0, 0))],
                 [jax.ShapeDtypeStruct((32, w), F32), jax.ShapeDtypeStruct((1, w), F32)],
                 [pltpu.VMEM((tt + 32, w), F32)])


PAIR = 2 * HEAD
N_PAIRS = RW // PAIR


def _chunk_pairs(s, r, lw, k, v, kk, a, sgn, tm_known=None):
    n, m = CHUNK, 2 * CHUNK
    in_a = lax.broadcasted_iota(jnp.int32, (n, PAIR), 1) < HEAD

    def stack2(z):
        return jnp.concatenate([jnp.where(in_a, z, 0.0), jnp.where(in_a, 0.0, z)], axis=0)

    def each(f, *lists):
        return [f(*z) for z in zip(*lists)]

    sgn_f = sgn.astype(F32)
    row2 = lax.broadcasted_iota(jnp.int32, (m, m), 0)
    col2 = lax.broadcasted_iota(jnp.int32, (m, m), 1)
    same = (row2 >= n) == (col2 >= n)
    dlt = ((row2 & (n - 1)) - (col2 & (n - 1))) * sgn
    incl, strict = same & (dlt >= 0), same & (dlt > 0)
    eye = jnp.where(row2 == col2, 1.0, 0.0)

    cum = each(lambda lw_: _dir_cumsum(lw_, sgn_f), lw)
    tot = each(lambda lw_: jnp.sum(lw_, axis=0, keepdims=True), lw)
    e_neg = each(lambda c_: jnp.exp(-c_), cum)
    e_rest = each(lambda t_, c_: jnp.exp(t_ - c_), tot, cum)
    beta = each(lambda kk_, a_: kk_ * a_, kk, a)
    lhs = each(lambda kk_, c_, lw_, r_: jnp.concatenate(
        [stack2(-kk_ * jnp.exp(c_ - lw_)), stack2(r_ * jnp.exp(c_))], axis=0), kk, cum, lw, r)
    rhs = each(lambda b_, k_, e_: jnp.concatenate([stack2(b_ * e_), stack2(k_ * e_)], axis=0), beta, k, e_neg)
    sc = each(lambda l_, r_: _dot3(l_, r_, "nt"), lhs, rhs)
    l_ab = each(lambda sc_: jnp.where(strict, sc_[0:m, 0:m], 0.0), sc)
    l_ak = each(lambda sc_: jnp.where(strict, sc_[0:m, m:2 * m], 0.0), sc)
    m_r = each(lambda sc_: jnp.where(jnp.concatenate([incl, incl], axis=1), sc_[m:2 * m, :], 0.0), sc)
    tm = _tri_inv(l_ab, eye) if tm_known is None else _tri_inv_known(l_ab, tm_known)
    z = _dot1_two(lhs, s, "nt")
    v2 = each(stack2, v)
    u2 = _dot1_two(tm, each(lambda z_, lv_: z_[0:m] + lv_, z, _dot1_two(l_ak, v2)))
    uv = each(lambda u_, v_: jnp.concatenate([u_, v_], axis=0), u2, v2)
    y2 = each(lambda z_, mu_: z_[m:2 * m] + mu_, z, _dot1_two(m_r, uv))
    bk = each(lambda b_, k_, e_: jnp.concatenate([stack2(b_ * e_), stack2(k_ * e_)], axis=0), beta, k, e_rest)
    s_new = each(lambda s_, t_, d_: s_ * jnp.exp(t_) + d_, s, tot, _dot1_two(uv, bk, "tn"))
    return each(lambda y_: y_[0:n] + y_[n:m], y2), s_new, tm


SCAN_SEQS = 4
N_CHAINS = SCAN_SEQS * N_PAIRS


def _pair_tiles(ref):
    return [ref[q, :, p * PAIR:(p + 1) * PAIR] for q in range(SCAN_SEQS) for p in range(N_PAIRS)]


def _store_tiles(ref, tiles):
    for q in range(SCAN_SEQS):
        for p in range(N_PAIRS):
            ref[q, :, p * PAIR:(p + 1) * PAIR] = tiles[q * N_PAIRS + p]


def _scan_specs(order):
    shared = pl.BlockSpec((SCAN_SEQS, CHUNK, RW), lambda d, b, c: (b, order(d, c), 0))
    per_dir = pl.BlockSpec((SCAN_SEQS, CHUNK, RW), lambda d, b, c: (b, order(d, c), d))
    state = pl.BlockSpec((None, SCAN_SEQS, None, N_PAIRS, PAIR, PAIR), lambda d, b, c: (d, b, order(d, c), 0, 0, 0))
    return shared, per_dir, state


def _scan_fwd(r, v, kk, lw, kd, a, bsz, exch=None):
    n = r.shape[0]
    t = n // bsz
    nc = t // CHUNK

    def order(d, c):
        return c + d * (nc - 1 - 2 * c)

    def body(r_ref, v_ref, kk_ref, lw_ref, kd_ref, a_ref, y_ref, s0_ref, tm_ref, s_ref):
        d, c = pl.program_id(0), pl.program_id(2)

        @pl.when(c == 0)
        def _():
            s_ref[...] = jnp.zeros_like(s_ref)

        s = [s_ref[i] for i in range(N_CHAINS)]
        y, s_new, tm = _chunk_pairs(s, *[_pair_tiles(ref) for ref in (r_ref, lw_ref, kd_ref, v_ref, kk_ref, a_ref)],
                                    1 - 2 * d)
        _store_tiles(y_ref, y)
        for i in range(N_CHAINS):
            s0_ref[i // N_PAIRS, i % N_PAIRS] = s[i]
            tm_ref[i // N_PAIRS, i % N_PAIRS] = tm[i].astype(BF16)
            s_ref[i] = s_new[i]

    shared, per_dir, state = _scan_specs(order)
    seq = lambda z: z.reshape(bsz, t, z.shape[1])
    res = _call("scan_fwd", body, (2, bsz // SCAN_SEQS, nc), [seq(z) for z in (r, v, kk, lw, kd, a)],
                [shared, shared, shared, per_dir, per_dir, per_dir], [per_dir, state, state],
                [jax.ShapeDtypeStruct((bsz, t, 2 * RW), F32),
                 jax.ShapeDtypeStruct((2, bsz, nc, N_PAIRS, PAIR, PAIR), F32),
                 jax.ShapeDtypeStruct((2, bsz, nc, N_PAIRS, PAIR, PAIR), BF16)],
                [pltpu.VMEM((N_CHAINS, PAIR, PAIR), F32)], exch)
    (y, s0, tm), got = res if exch else (res, None)
    y = y.reshape(n, 2 * RW)
    return ([y, s0, tm], got) if exch else [y, s0, tm]


def _scan_bwd(r, v, kk, lw, kd, a, s0, tm, dy, bsz, exch=None):
    n = r.shape[0]
    t = n // bsz
    nc = t // CHUNK

    def order(d, c):
        cc = nc - 1 - c
        return cc + d * (nc - 1 - 2 * cc)

    def body(r_ref, v_ref, kk_ref, lw_ref, kd_ref, a_ref, dy_ref, s0_ref, tm_ref,
             dr_ref, dv_ref, dkk_ref, dlw_ref, dkd_ref, da_ref, ds_ref):
        d, c = pl.program_id(0), pl.program_id(2)

        @pl.when(c == 0)
        def _():
            ds_ref[...] = jnp.zeros_like(ds_ref)

        sgn = 1 - 2 * d
        tm_known = [tm_ref[i // N_PAIRS, i % N_PAIRS].astype(F32) for i in range(N_CHAINS)]
        _, pull = jax.vjp(lambda *ops: _chunk_pairs(*ops, sgn, tm_known)[:2],
                          [s0_ref[i // N_PAIRS, i % N_PAIRS] for i in range(N_CHAINS)],
                          *[_pair_tiles(ref) for ref in (r_ref, lw_ref, kd_ref, v_ref, kk_ref, a_ref)])
        grads = pull((_pair_tiles(dy_ref), [ds_ref[i] for i in range(N_CHAINS)]))
        for i in range(N_CHAINS):
            ds_ref[i] = grads[0][i]
        for o_ref, gx in zip((dr_ref, dlw_ref, dkd_ref, dv_ref, dkk_ref, da_ref), grads[1:]):
            _store_tiles(o_ref, gx)

    shared, per_dir, state = _scan_specs(order)
    shp = jax.ShapeDtypeStruct((bsz, t, 2 * RW), F32)
    seq = lambda z: z.reshape(bsz, t, z.shape[1])
    res = _call("scan_bwd", body, (2, bsz // SCAN_SEQS, nc), [seq(z) for z in (r, v, kk, lw, kd, a, dy)] + [s0, tm],
                [shared, shared, shared, per_dir, per_dir, per_dir, per_dir, state, state],
                [per_dir] * 6, [shp] * 6, [pltpu.VMEM((N_CHAINS, PAIR, PAIR), F32)], exch)
    outs, got = res if exch else (res, None)
    outs = [z.reshape(n, 2 * RW) for z in outs]
    return (outs, got) if exch else outs


def _prep_fn(ps, w0, w2bd, a0, a2bd, g2p, k_k, k_a, hsum):
    head_sum = lambda z: _head_sum(z, hsum)
    r, k, v = ps[:, 0:RW], ps[:, RW:2 * RW], ps[:, 2 * RW:3 * RW]
    wd, ad, gd = ps[:, 1536:1664], ps[:, 1664:1792], ps[:, 1792:2048]
    logw = -DECAY_SCALE * _sigmoid(_dot1(jnp.tanh(wd), w2bd) + w0)
    a = _sigmoid(_dot1(ad, a2bd) + a0)
    g = _dot1(_sigmoid(gd), g2p)
    kkr = k * k_k
    kk = kkr / jnp.maximum(jnp.sqrt(head_sum(kkr * kkr)), NORM_EPS)
    k2 = jnp.concatenate([k, k], axis=1)
    ka2 = jnp.concatenate([k_a, k_a], axis=1)
    kd = k2 * (1.0 + (a - 1.0) * ka2)
    return r, v, kk, logw, a, kd, g


def _post_fn(y2, r, v, kd, g, lnx_g, lnx_b, r_k, hsum):
    head_sum = lambda z: _head_sum(z, hsum)
    y = y2[:, 0:RW] + y2[:, RW:2 * RW]
    mu = head_sum(y) * (1.0 / HEAD)
    yc = y - mu
    var = head_sum(yc * yc) * (1.0 / HEAD)
    yn = yc * lax.rsqrt(var + GN_EPS) * lnx_g + lnx_b
    bonus = head_sum(r * (kd[:, 0:RW] + kd[:, RW:2 * RW]) * r_k) * v
    return ((yn + bonus) * g,)


def _glu_fn(pa, pb):
    return (pa * _sigmoid(pb),)


def _conv_out_fn(cv, ln_g, ln_b):
    mu = jnp.mean(cv, axis=-1, keepdims=True)
    cc = cv - mu
    var = jnp.mean(cc * cc, axis=-1, keepdims=True)
    y = cc * lax.rsqrt(var + LN_EPS) * ln_g + ln_b
    return (y * _sigmoid(y),)


def _local_step(x, tgt, w, ex=None):
    bsz, t, d = x.shape
    n = bsz * t
    x2d, tgt2d = x.reshape(n, d), tgt.reshape(n, d)
    hsum = jnp.tile(jnp.kron(jnp.eye(N_HEADS, dtype=BF16), jnp.ones((HEAD, HEAD), BF16)), (3, 1))
    w = dict(w)
    parts = {} if ex else None

    def hosted(result, finish=None):
        if not ex:
            return result
        outs, got = result
        if finish is not None:
            w.update(finish(got))
        return outs

    xb = x2d.astype(BF16)
    hg1, hu1, act1 = hosted(_ffn_in("ffn1_in", xb, w["ffn1_w_in"], exch=(ex["g1"][0], True) if ex else None),
                            ex["g1"][1] if ex else None)
    x1, x1b, z1 = _mm_ln("ffn1_out_ln1", act1, w["ffn1_w_out"], x2d, w["ln1_g"], w["ln1_b"], 0.5, a_blocked=True)
    p = _mm("w_in_proj", x1b, w["w_in"], "nn", F32)
    p3 = p.reshape(bsz, t, IN_PAD)
    ps = _shift("shift_fwd", p3, w["mu_prev"], w["mu_next"]).reshape(n, SHIFT_PAD)
    prep_params = [w["w0"], w["w2"], w["a0"], w["a2"], w["g2"], w["k_k"], w["k_a"], hsum]
    r, v, kk, logw, a, kd, g = _rowwise(
        "rwkv_prep", _prep_fn, [ps], prep_params,
        [(RW, F32), (RW, F32), (RW, F32), (2 * RW, F32), (2 * RW, F32), (2 * RW, F32), (RW, F32)], [])
    y2, s0, tm = hosted(_scan_fwd(r, v, kk, logw, kd, a, bsz, exch=(ex["g2"][0], True) if ex else None),
                    ex["g2"][1] if ex else None)
    post_params = [w["lnx_g"], w["lnx_b"], w["r_k"], hsum]
    (y_rwkv,) = _rowwise("rwkv_post", _post_fn, [y2, r, v, kd, g], post_params, [(RW, BF16)], [])
    (u,) = _rowwise("conv_glu", _glu_fn, [(p, CW, 4), (p, CW, 5)], [], [(CW, F32)], [])
    cv = _dwconv("conv_dw", u.reshape(bsz, t, CW), w["conv_dw"], w["conv_b"], False).reshape(n, CW)
    (y_conv,) = _rowwise("conv_out", _conv_out_fn, [cv], [w["conv_ln_g"], w["conv_ln_b"]], [(CW, BF16)], [])
    ycat = jnp.concatenate([y_rwkv, y_conv], axis=1)
    x2, x2b, z2 = _mm_ln("w_out_ln2", ycat, w["w_out"], x1, w["ln2_g"], w["ln2_b"], 1.0)
    hg2, hu2, act2 = _ffn_in("ffn2_in", x2b, w["ffn2_w_in"])
    dx3, z3, loss = _mm_ln("ffn2_out_ln3", act2, w["ffn2_w_out"], x2, w["ln3_g"], w["ln3_b"], 0.5,
                           tgt=tgt2d, a_blocked=True)

    gr = {}
    dz3, dz3b, gr["ln3_g"], gr["ln3_b"] = _ln_bwd("ln3_bwd", z3, w["ln3_g"], dx3)
    dx2, _ = _ffn_bwd("ffn2", gr, dz3, dz3b, x2b, w["ffn2_w_in"], w["ffn2_w_out"], hg2, hu2, act2)
    dz2, dz2b, gr["ln2_g"], gr["ln2_b"] = _ln_bwd("ln2_bwd", z2, w["ln2_g"], dx2)
    gr["w_out"] = _mm("w_out_wgrad", ycat, dz2b, "tn", BF16, tk=512)
    dycat = _mm("w_out_dgrad", dz2b, w["w_out"], "nt", F32)
    conv_out_bwd = _vjp_of(_conv_out_fn, 3)
    dcv, gr["conv_ln_g"], gr["conv_ln_b"] = _rowwise(
        "conv_out_bwd", lambda cv_, ct_, g_, b_: conv_out_bwd(cv_, g_, b_, ct_),
        [cv, (dycat, CW, 1)], [w["conv_ln_g"], w["conv_ln_b"]], [(CW, F32)], [(1, CW), (1, CW)])
    dcv3 = dcv.reshape(bsz, t, CW)
    gr["conv_dw"], gr["conv_b"] = _dwconv_dw("conv_dw_wgrad", u.reshape(bsz, t, CW), dcv3)
    du = _dwconv("conv_dw_dgrad", dcv3, w["conv_dw"], jnp.zeros((1, CW), F32), True).reshape(n, CW)

    def glu_bwd(pa, pb, ct):
        return (jnp.concatenate(_vjp_of(_glu_fn, 2)(pa, pb, ct), axis=1),)

    (dp_conv,) = _rowwise("conv_glu_bwd", glu_bwd, [(p, CW, 4), (p, CW, 5), du], [], [(2 * CW, BF16)], [])

    def post_bwd(y2_, r_, v_, kd_, g_, ct, lg, lb, rk, hs):
        return _vjp_of(lambda *z: _post_fn(*z, hs), 8)(y2_, r_, v_, kd_, g_, lg, lb, rk, ct)

    dy2, dr_post, dv_post, dkd_post, dg, gr["lnx_g"], gr["lnx_b"], gr["r_k"] = _rowwise(
        "rwkv_post_bwd", post_bwd, [y2, r, v, kd, g, (dycat, RW, 0)], post_params,
        [(2 * RW, F32), (RW, F32), (RW, F32), (2 * RW, F32), (RW, F32)], [(1, RW), (1, RW), (1, RW)])
    sends = [jnp.concatenate(gr["ffn2_w_in"], axis=0), gr["ffn2_w_out"].reshape(N_DEV, D_FF // N_DEV, D_MODEL),
             gr["w_out"].reshape(N_DEV, D_MODEL // N_DEV, D_MODEL)]
    res = _scan_bwd(r, v, kk, logw, kd, a, s0, tm, dy2, bsz, exch=(sends, False) if ex else None)
    if ex:
        res, got = res
        parts.update(zip(("ffn2_w_in", "ffn2_w_out", "w_out"), got))
    dr_s, dv_s, dkk_s, dlw, dkd_s, da = res

    def prep_bwd(ps_, dr2, dr1, dv2, dv1, dkk2, dlw_, da_, dkd2, dkd1, dg_, *prm):
        half = lambda z: z[:, 0:RW] + z[:, RW:2 * RW]
        return _vjp_of(lambda *z: _prep_fn(*z, prm[-1]), 8)(
            ps_, *prm[:-1], half(dr2) + dr1, half(dv2) + dv1, half(dkk2), dlw_, da_, dkd2 + dkd1, dg_)

    dps, gr["w0"], gr["w2"], gr["a0"], gr["a2"], gr["g2"], gr["k_k"], gr["k_a"] = _rowwise(
        "rwkv_prep_bwd", prep_bwd,
        [ps, dr_s, dr_post, dv_s, dv_post, dkk_s, dlw, da, dkd_s, dkd_post, dg], prep_params,
        [(SHIFT_PAD, F32)], [q.shape for q in prep_params[:-1]])
    dps3 = dps.reshape(bsz, t, SHIFT_PAD)
    gr["mu_prev"], gr["mu_next"] = _shift("shift_dmu", p3, None, None, q=dps3)
    dp_shift = _shift("shift_bwd", dps3, w["mu_next"], w["mu_prev"], out_dtype=BF16).reshape(n, SHIFT_PAD)
    dp = jnp.concatenate([dp_shift, dp_conv], axis=1)
    gr["w_in"] = _mm("w_in_wgrad", x1b, dp, "tn", BF16, tk=512)
    dx1 = _mm("w_in_dgrad", dp, w["w_in"], "nt", F32, add=dz2, add_scale=ALPHA)
    dz1, dz1b, gr["ln1_g"], gr["ln1_b"] = _ln_bwd("ln1_bwd", z1, w["ln1_g"], dx1)
    riders = None
    if ex:
        gw_in = _unpad_in_cols(gr["w_in"]).reshape(D_MODEL, N_DEV, IN_COLS // N_DEV).transpose(1, 0, 2)
        small = _pack([_grad_small(nm, gr[nm]) for nm in SMALL_SHARDED + SMALL_REPL])
        riders = {"out_bwd": [gw_in], "in_wgrad": [jnp.broadcast_to(small[None], (N_DEV,) + small.shape)]}
    grad_x, got = _ffn_bwd("ffn1", gr, dz1, dz1b, xb, w["ffn1_w_in"], w["ffn1_w_out"], hg1, hu1, act1, riders)
    if ex:
        parts.update(w_in=got["out_bwd"][0], ffn1_w_out=got["in_wgrad"][0], small=got["in_wgrad"][1],
                     ffn1_w_in=got["in_dgrad"][0])
    return loss[0, 0], grad_x.reshape(bsz, t, d), gr, parts


def _ffn_in_wgrad(name, xin, dhg, dhu, tk=1024, exch=None):
    n = xin.shape[0]
    nj, nt = N_DEV // 2, n // tk

    def body(x_ref, g_ref, u_ref, og_ref, ou_ref, accg_ref, accu_ref):
        i = pl.program_id(1)

        @pl.when(i == 0)
        def _():
            accg_ref[...] = jnp.zeros_like(accg_ref)
            accu_ref[...] = jnp.zeros_like(accu_ref)

        xt = x_ref[...].astype(BF16).T
        accg_ref[...] += _dot(xt, g_ref[...])
        accu_ref[...] += _dot(xt, u_ref[...])

        @pl.when(i == nt - 1)
        def _():
            og_ref[...] = accg_ref[...].astype(og_ref.dtype)
            ou_ref[...] = accu_ref[...].astype(ou_ref.dtype)

    dh_blk = pl.BlockSpec((None, tk, FF_BLK), lambda j, i: (j, i, 0))
    o_blk = pl.BlockSpec((None, D_MODEL, FF_BLK), lambda j, i: (j, 0, 0))
    shp = jax.ShapeDtypeStruct((nj, D_MODEL, FF_BLK), BF16)
    return _call(name, body, (nj, nt), [xin, dhg, dhu],
                 [pl.BlockSpec((tk, D_MODEL), lambda j, i: (i, 0)), dh_blk, dh_blk], [o_blk, o_blk], [shp, shp],
                 [pltpu.VMEM((D_MODEL, FF_BLK), F32)] * 2, exch)


def _ffn_in_dgrad(name, dhg, dhu, wg, add, add_scale, tm=512, exch=None):
    n = add.shape[0]
    nj = N_DEV // 2

    def body(g_ref, u_ref, wgate_ref, wup_ref, add_ref, o_ref, acc_ref):
        j = pl.program_id(1)

        @pl.when(j == 0)
        def _():
            acc_ref[...] = jnp.zeros_like(acc_ref)

        acc_ref[...] += _dot(g_ref[...], wgate_ref[...], "nt") + _dot(u_ref[...], wup_ref[...], "nt")

        @pl.when(j == nj - 1)
        def _():
            o_ref[...] = acc_ref[...] + add_scale * add_ref[...]

    dh_blk = pl.BlockSpec((None, tm, FF_BLK), lambda i, j: (j, i, 0))
    row = pl.BlockSpec((tm, D_MODEL), lambda i, j: (i, 0))
    return _call(name, body, (n // tm, nj), [dhg, dhu, wg, wg, add],
                 [dh_blk, dh_blk, pl.BlockSpec((None, D_MODEL, FF_BLK), lambda i, j: (j, 0, 0)),
                  pl.BlockSpec((None, D_MODEL, FF_BLK), lambda i, j: (j + nj, 0, 0)), row],
                 row, jax.ShapeDtypeStruct((n, D_MODEL), F32), [pltpu.VMEM((tm, D_MODEL), F32)], exch)


def _ffn_bwd(tag, gr, dz, dzb, xin, wg, wout, hg, hu, act, riders=None):
    own = riders is not None

    def hosted(result):
        return result if own else (result, None)

    (dhg, dhu, gwo), got_a = hosted(_ffn_out_bwd(
        tag + "_out_bwd", dzb, wout, hg, hu, act, exch=(riders["out_bwd"], False) if own else None))
    gr[tag + "_w_out"] = gwo
    send = ([gwo.reshape(N_DEV, D_FF // N_DEV, D_MODEL)] + riders["in_wgrad"], False) if own else None
    dw, got_b = hosted(_ffn_in_wgrad(tag + "_in_wgrad", xin, dhg, dhu, exch=send))
    gr[tag + "_w_in"] = dw
    send = ([jnp.concatenate(dw, axis=0)], False) if own else None
    dx, got_c = hosted(_ffn_in_dgrad(tag + "_in_dgrad", dhg, dhu, wg, dz, ALPHA, exch=send))
    return dx, {"out_bwd": got_a, "in_wgrad": got_b, "in_dgrad": got_c}


def _adam_math(g, w, m, v):
    m = ADAM_B1 * m + (1.0 - ADAM_B1) * g
    v = ADAM_B2 * v + (1.0 - ADAM_B2) * (g * g)
    m_hat = m / (1.0 - ADAM_B1 ** ADAM_STEP)
    v_hat = v / (1.0 - ADAM_B2 ** ADAM_STEP)
    delta = -ADAM_LR * (m_hat / (jnp.sqrt(v_hat) + ADAM_EPS) + ADAM_WD * w)
    return delta, m, v


def _adam(name, parts, w, m, v, tr=128):
    rows, cols = w.shape
    tr = min(tr, rows)
    while rows % tr:
        tr -= 8

    def body(p_ref, w_ref, m_ref, v_ref, g_ref, d_ref, mo_ref, vo_ref):
        g = p_ref[0].astype(F32)
        for s in range(1, N_DEV):
            g = g + p_ref[s].astype(F32)
        g_ref[...] = g
        d_ref[...], mo_ref[...], vo_ref[...] = _adam_math(g, w_ref[...], m_ref[...], v_ref[...])

    blk = pl.BlockSpec((tr, cols), lambda i: (i, 0))
    shp = jax.ShapeDtypeStruct((rows, cols), F32)
    return _call(name, body, (rows // tr,), [parts, w, m, v],
                 [pl.BlockSpec((N_DEV, tr, cols), lambda i: (0, i, 0)), blk, blk, blk], [blk] * 4, [shp] * 4)


def _sum8(name, parts):
    _, rows, cols = parts.shape

    def body(p_ref, o_ref):
        g = p_ref[0]
        for s in range(1, N_DEV):
            g = g + p_ref[s]
        o_ref[...] = g

    return pl.pallas_call(body, name=name, out_shape=jax.ShapeDtypeStruct((rows, cols), F32),
                          compiler_params=_params())(parts)


def _adam_small(name, g, w, m, v):
    def body(g_ref, w_ref, m_ref, v_ref, d_ref, mo_ref, vo_ref):
        d_ref[...], mo_ref[...], vo_ref[...] = _adam_math(g_ref[...], w_ref[...], m_ref[...], v_ref[...])

    shp = jax.ShapeDtypeStruct(g.shape, F32)
    return pl.pallas_call(body, name=name, out_shape=[shp] * 3, compiler_params=_params())(g, w, m, v)


def _pack(arrs, lane=128):
    flat = jnp.concatenate([a.reshape(-1).astype(F32) for a in arrs])
    pad = (-flat.shape[0]) % (8 * lane)
    return jnp.pad(flat, (0, pad)).reshape(-1, lane)


def _unpack(packed, shapes):
    flat, out, off = packed.reshape(-1), [], 0
    for s in shapes:
        sz = math.prod(s)
        out.append(flat[off:off + sz].reshape(s))
        off += sz
    return out


def _pad_in_cols(wfull):
    zeros = jnp.zeros((wfull.shape[0], SHIFT_PAD - SHIFT_COLS), wfull.dtype)
    return jnp.concatenate([wfull[:, :SHIFT_COLS], zeros, wfull[:, SHIFT_COLS:]], axis=1)


def _unpad_in_cols(gfull):
    return jnp.concatenate([gfull[:, :SHIFT_COLS], gfull[:, SHIFT_PAD:]], axis=1)


def _block_diag2(wd):
    z = jnp.zeros_like(wd[0])
    return jnp.concatenate([jnp.concatenate([wd[0], z], axis=1), jnp.concatenate([z, wd[1]], axis=1)], axis=0)


def _unblock_diag2(g):
    return jnp.stack([g[0:64, 0:RW], g[64:128, RW:2 * RW]])


SMALL_SHARDED = ("w0", "w2", "a0", "a2", "g2", "conv_dw")
SMALL_REPL = ("mu_prev", "mu_next", "k_k", "k_a", "r_k", "lnx_g", "lnx_b", "conv_b", "conv_ln_g", "conv_ln_b",
              "ln1_g", "ln1_b", "ln2_g", "ln2_b", "ln3_g", "ln3_b")
BIG = ("ffn1_w_in", "ffn1_w_out", "w_in", "w_out", "ffn2_w_in", "ffn2_w_out")
WEIGHTS = ("ffn1_w_in", "ffn1_w_out", "w_in", "mu_prev", "mu_next", "w0", "w2", "a0", "a2", "g2", "k_k", "k_a",
           "r_k", "lnx_g", "lnx_b", "conv_dw", "conv_b", "conv_ln_g", "conv_ln_b", "w_out", "ffn2_w_in",
           "ffn2_w_out", "ln1_g", "ln1_b", "ln2_g", "ln2_b", "ln3_g", "ln3_b")


def _full_small(name, full):
    if name in ("w0", "a0"):
        return full.reshape(1, 2 * RW)
    if name in ("w2", "a2"):
        return _block_diag2(full)
    if name == "g2":
        return jnp.pad(full, ((0, 256 - GATE_LORA), (0, 0)))
    if name == "conv_dw":
        return jnp.pad(full, ((0, 1), (0, 0)))
    if name in ("mu_prev", "mu_next"):
        return jnp.pad(full.reshape(1, SHIFT_COLS), ((0, 0), (0, SHIFT_PAD - SHIFT_COLS)))
    return full.reshape(1, -1)


def _grad_small(name, g):
    if name in ("w0", "a0"):
        return g.reshape(2, RW)
    if name in ("w2", "a2"):
        return _unblock_diag2(g)
    if name == "g2":
        return g[:GATE_LORA]
    if name == "conv_dw":
        return g[:CONV_K]
    if name in ("mu_prev", "mu_next"):
        return g[0, :SHIFT_COLS]
    if name == "r_k":
        return g.reshape(N_HEADS, HEAD)
    return g.reshape(-1)


def kernel(x, ffn1_w_in, ffn1_w_out, w_in, mu_prev, mu_next, w0, w2, a0, a2, g2, k_k, k_a, r_k, lnx_g, lnx_b, conv_dw, conv_b, conv_ln_g, conv_ln_b, w_out, ffn2_w_in, ffn2_w_out, ln1_g, ln1_b, ln2_g, ln2_b, ln3_g, ln3_b, loss_target, m_ffn1_w_in, m_ffn1_w_out, m_w_in, m_mu_prev, m_mu_next, m_w0, m_w2, m_a0, m_a2, m_g2, m_k_k, m_k_a, m_r_k, m_lnx_g, m_lnx_b, m_conv_dw, m_conv_b, m_conv_ln_g, m_conv_ln_b, m_w_out, m_ffn2_w_in, m_ffn2_w_out, m_ln1_g, m_ln1_b, m_ln2_g, m_ln2_b, m_ln3_g, m_ln3_b, v_ffn1_w_in, v_ffn1_w_out, v_w_in, v_mu_prev, v_mu_next, v_w0, v_w2, v_a0, v_a2, v_g2, v_k_k, v_k_a, v_r_k, v_lnx_g, v_lnx_b, v_conv_dw, v_conv_b, v_conv_ln_g, v_conv_ln_b, v_w_out, v_ffn2_w_in, v_ffn2_w_out, v_ln1_g, v_ln1_b, v_ln2_g, v_ln2_b, v_ln3_g, v_ln3_b):
    args = dict(locals())
    drop = lambda z: z.reshape(z.shape[1:])
    wsh = {n: drop(args[n]) for n in WEIGHTS}
    msh = {n: drop(args["m_" + n]) for n in WEIGHTS}
    vsh = {n: drop(args["v_" + n]) for n in WEIGHTS}
    me = 4 * lax.axis_index("x") + 2 * lax.axis_index("y") + lax.axis_index("c")
    bf = {n: wsh[n].astype(BF16) for n in BIG}

    w = {"ffn1_w_in": _gather_two_level("gather_ffn1_w_in", bf["ffn1_w_in"])}
    for n in SMALL_REPL:
        w[n] = _full_small(n, wsh[n])
    small_shapes = [wsh[n].shape for n in SMALL_SHARDED]

    def finish1(got):
        f1_out, w_in_g, small = got
        cols = zip(*[_unpack(small[dv], small_shapes) for dv in range(N_DEV)])
        out = {n: _full_small(n, jnp.concatenate(s, axis=-1)) for n, s in zip(SMALL_SHARDED, cols)}
        out["ffn1_w_out"] = f1_out.reshape(D_FF, D_MODEL)
        out["w_in"] = _pad_in_cols(w_in_g.transpose(1, 0, 2).reshape(D_MODEL, IN_COLS))
        return out

    def finish2(got):
        w_out_g, f2_in, f2_out = got
        return {"w_out": w_out_g.reshape(D_MODEL, D_MODEL), "ffn2_w_in": f2_in,
                "ffn2_w_out": f2_out.reshape(D_FF, D_MODEL)}

    ex = {"g1": ([bf["ffn1_w_out"], bf["w_in"], _pack([wsh[n] for n in SMALL_SHARDED])], finish1),
          "g2": ([bf["w_out"], bf["ffn2_w_in"], bf["ffn2_w_out"]], finish2)}
    loss_part, grad_x, gr, parts = _local_step(x, loss_target, w, ex)
    loss = lax.psum(loss_part, ("x", "y", "c"))

    out = {n: _adam("adam_" + n, parts[n], wsh[n], msh[n], vsh[n]) for n in BIG}
    small_names = SMALL_SHARDED + SMALL_REPL
    full_shapes = [_grad_small(n, gr[n]).shape for n in small_names]
    summed = _unpack(_sum8("sum_small_grads", parts["small"]), full_shapes)
    mine = []
    for n, g in zip(small_names, summed):
        if n in SMALL_SHARDED:
            g = lax.dynamic_slice_in_dim(g, me * HEAD, HEAD, axis=g.ndim - 1)
        mine.append(g)
    shapes = [g.shape for g in mine]
    d_s, m_s, v_s = _adam_small("adam_small", _pack(mine), _pack([wsh[n] for n in small_names]),
                                _pack([msh[n] for n in small_names]), _pack([vsh[n] for n in small_names]))
    for n, g, dl, mn, vn in zip(small_names, mine, _unpack(d_s, shapes), _unpack(m_s, shapes), _unpack(v_s, shapes)):
        out[n] = (g, dl, mn, vn)

    res = [loss, grad_x]
    for k in range(4):
        res += [out[n][k].reshape((1,) + out[n][k].shape) for n in WEIGHTS]
    return tuple(res)
```

```python
import functools
import math

import jax
import jax.numpy as jnp
from jax import lax
from jax.experimental import pallas as pl
from jax.experimental.pallas import tpu as pltpu

F32 = jnp.float32
BF16 = jnp.bfloat16

N_DEV = 8
D_MODEL = 1024
RW = 512
N_HEADS = 8
HEAD = 64
CW = 512
CONV_K = 31
D_FF = 2816
FF_BLK = 704
GATE_LORA = 160
SHIFT_COLS = 1952
SHIFT_PAD = 2048
IN_COLS = 2976
IN_PAD = 3072
LN_EPS = 1e-5
GN_EPS = 64e-5
NORM_EPS = 1e-12
ALPHA = 2.0 ** 0.25
DECAY_SCALE = math.exp(-0.5)
CHUNK = 64
ADAM_LR, ADAM_B1, ADAM_B2, ADAM_EPS, ADAM_WD, ADAM_STEP = 0.001, 0.9, 0.999, 1e-8, 0.01, 10
VMEM_LIMIT = 56 * 1024 * 1024
MXU_DIM = 256

_DN = {"nn": (((1,), (0,)), ((), ())), "nt": (((1,), (1,)), ((), ())), "tn": (((0,), (0,)), ((), ()))}


def _params():
    return pltpu.CompilerParams(vmem_limit_bytes=VMEM_LIMIT)


def _dot(a, b, dims="nn"):
    return lax.dot_general(a, b, _DN[dims], preferred_element_type=F32)


def _split(x):
    hi = x.astype(BF16)
    return hi, (x - hi.astype(F32)).astype(BF16)


def _dot3_impl(a, b, dims):
    ah, al = _split(a)
    bh, bl = _split(b)
    ka, kb = _DN[dims][0][0][0], _DN[dims][0][1][0]
    return _dot(jnp.concatenate([ah, ah, al], axis=ka), jnp.concatenate([bh, bl, bh], axis=kb), dims)


@functools.partial(jax.custom_vjp, nondiff_argnums=(2,))
def _dot3(a, b, dims="nn"):
    return _dot3_impl(a, b, dims)


def _dot3_fwd(a, b, dims):
    return _dot3_impl(a, b, dims), (a, b)


def _dot2_ct(x, y, dims, ct_left):
    ka, kb = _DN[dims][0][0][0], _DN[dims][0][1][0]
    if ct_left:
        c, (h, l) = x.astype(BF16), _split(y)
        return _dot(jnp.concatenate([c, c], axis=ka), jnp.concatenate([h, l], axis=kb), dims)
    (h, l), c = _split(x), y.astype(BF16)
    return _dot(jnp.concatenate([h, l], axis=ka), jnp.concatenate([c, c], axis=kb), dims)


def _dot3_bwd(dims, res, ct):
    a, b = res
    if dims == "nn":
        return _dot2_ct(ct, b, "nt", True), _dot2_ct(a, ct, "tn", False)
    if dims == "nt":
        return _dot2_ct(ct, b, "nn", True), _dot2_ct(ct, a, "tn", True)
    return _dot2_ct(b, ct, "nt", False), _dot2_ct(a, ct, "nn", False)


_dot3.defvjp(_dot3_fwd, _dot3_bwd)


def _dot1_impl(a, b, dims):
    return _dot(a.astype(BF16), b.astype(BF16), dims)


@functools.partial(jax.custom_vjp, nondiff_argnums=(2,))
def _dot1(a, b, dims="nn"):
    return _dot1_impl(a, b, dims)


def _dot1_bwd(dims, res, ct):
    a, b = res
    if dims == "nn":
        return _dot1_impl(ct, b, "nt"), _dot1_impl(a, ct, "tn")
    if dims == "nt":
        return _dot1_impl(ct, b, "nn"), _dot1_impl(ct, a, "tn")
    return _dot1_impl(b, ct, "nt"), _dot1_impl(a, ct, "nn")


_dot1.defvjp(lambda a, b, dims: (_dot1_impl(a, b, dims), (a, b)), _dot1_bwd)


def _dot1_two(a, b, dims="nn"):
    ax_a, ax_b = {"nn": (0, 1), "nt": (0, 0), "tn": (1, 1)}[dims]
    shallow = dims != "tn" and 2 * a[0].shape[1] <= MXU_DIM
    out = []
    for i in range(0, len(a), 2):
        if shallow:
            z = jnp.zeros_like(b[i])
            bd = jnp.concatenate([jnp.concatenate([b[i], z], axis=1), jnp.concatenate([z, b[i + 1]], axis=1)], axis=0)
            r = _dot1(jnp.concatenate(a[i:i + 2], axis=1), bd, dims)
            n = r.shape[1] // 2
            out += [r[:, :n], r[:, n:]]
        else:
            r = _dot1(jnp.concatenate(a[i:i + 2], axis=ax_a), jnp.concatenate(b[i:i + 2], axis=ax_b), dims)
            m, n = r.shape[0] // 2, r.shape[1] // 2
            out += [r[:m, :n], r[m:, n:]]
    return out


def _tri_inv_impl(l, eye):
    steps = int(math.log2(CHUNK)) - 1
    m = l[0].shape[0]
    tm = [eye + x for x in l]
    lp = _dot1_two(l, l)
    for k in range(steps):
        if k < steps - 1:
            both = _dot1_two([jnp.concatenate([t, p], axis=0) for t, p in zip(tm, lp)], lp)
            tm = [t + b[:m] for t, b in zip(tm, both)]
            lp = [b[m:] for b in both]
        else:
            tm = [t + x for t, x in zip(tm, _dot1_two(tm, lp))]
    return tm


@jax.custom_vjp
def _tri_inv(l, eye):
    return _tri_inv_impl(l, eye)


def _tri_inv_fwd(l, eye):
    tm = _tri_inv_impl(l, eye)
    return tm, (tm, eye)


def _tri_inv_bwd(res, ct):
    tm, eye = res
    return _dot1_two(_dot1_two(tm, ct, "tn"), tm, "nt"), jnp.zeros_like(eye)


_tri_inv.defvjp(_tri_inv_fwd, _tri_inv_bwd)


@jax.custom_vjp
def _tri_inv_known(l, tm):
    return tm


_tri_inv_known.defvjp(lambda l, tm: (tm, tm),
                      lambda tm, ct: (_dot1_two(_dot1_two(tm, ct, "tn"), tm, "nt"), [jnp.zeros_like(t) for t in tm]))


def _ones_impl(x, g3):
    x1 = x.astype(BF16)
    r1 = x - x1.astype(F32)
    x2 = r1.astype(BF16)
    x3 = (r1 - x2.astype(F32)).astype(BF16)
    return _dot(jnp.concatenate([x1, x2, x3], axis=1), g3)


@jax.custom_vjp
def _head_sum(x, g3):
    return _ones_impl(x, g3)


_head_sum.defvjp(lambda x, g3: (_ones_impl(x, g3), g3), lambda g3, ct: (_ones_impl(ct, g3), jnp.zeros_like(g3)))


def _prefix_sum(x):
    row = lax.broadcasted_iota(jnp.int32, x.shape, 0)
    sh = 1
    while sh < x.shape[0]:
        x = x + jnp.where(row >= sh, pltpu.roll(x, sh, 0), 0.0)
        sh *= 2
    return x


def _dir_cumsum_impl(x, sgn):
    pre = _prefix_sum(x)
    return jnp.where(sgn > 0.0, pre, jnp.sum(x, axis=0, keepdims=True) - pre + x)


@jax.custom_vjp
def _dir_cumsum(x, sgn):
    return _dir_cumsum_impl(x, sgn)


_dir_cumsum.defvjp(lambda x, sgn: (_dir_cumsum_impl(x, sgn), sgn),
                   lambda sgn, ct: (_dir_cumsum_impl(ct, -sgn), jnp.zeros_like(sgn)))


def _sigmoid(x):
    return 1.0 / (1.0 + jnp.exp(-x))


def _mesh_pos():
    return lax.axis_index("x"), lax.axis_index("y"), lax.axis_index("c")


def _peer(pos, q):
    x, y, c = pos
    return (1 - x if q & 4 else x, 1 - y if q & 2 else y, 1 - c if q & 1 else c)


def _linear(pos):
    return 4 * pos[0] + 2 * pos[1] + pos[2]


def _exchange_copies(x_refs, o_refs, send_sems, recv_sems, local_sems, gather):
    pos = _mesh_pos()
    me = _linear(pos)
    starts, wait_recv, wait_send, wait_local = [], [], [], []
    for t in range(len(x_refs)):
        src = x_refs[t] if gather else x_refs[t].at[me]
        cp = pltpu.make_async_copy(src, o_refs[t].at[me], local_sems.at[t])
        starts.append(cp.start)
        wait_local.append(cp.wait)
    for q in range(1, N_DEV):
        peer = _peer(pos, q)
        for t in range(len(x_refs)):
            src = x_refs[t] if gather else x_refs[t].at[_linear(peer)]
            sems = dict(send_sem=send_sems.at[t, q - 1], recv_sem=recv_sems.at[t, q - 1],
                        device_id=peer, device_id_type=pl.DeviceIdType.MESH)
            send = pltpu.make_async_remote_copy(src_ref=src, dst_ref=o_refs[t].at[me], **sems)
            recv = pltpu.make_async_remote_copy(src_ref=src, dst_ref=o_refs[t].at[_linear(peer)], **sems)
            starts.append(send.start)
            wait_recv.append(recv.wait_recv)
            wait_send.append(send.wait_send)
    return starts, wait_recv + wait_send + wait_local


def _exchange_shapes(xs, gather):
    return [jax.ShapeDtypeStruct((N_DEV,) + (x.shape if gather else x.shape[1:]), x.dtype) for x in xs]


def _exchange_sems(nt):
    return [pltpu.SemaphoreType.DMA((nt, N_DEV - 1)), pltpu.SemaphoreType.DMA((nt, N_DEV - 1)),
            pltpu.SemaphoreType.DMA((nt,))]


def _gather_two_level(name, x):
    def body(x_ref, out_ref, send_sems, recv_sems, local_sem):
        px, py, pc = _mesh_pos()
        me, sibling = (px, py, pc), (px, py, 1 - pc)
        chips = [(1 - px, py), (px, 1 - py), (1 - px, 1 - py)]

        def slot(pos):
            return out_ref.at[_linear(pos)]

        def copy(k, block, to, src=None):
            return pltpu.make_async_remote_copy(
                src_ref=slot(block) if src is None else src, dst_ref=slot(block), send_sem=send_sems.at[k],
                recv_sem=recv_sems.at[k], device_id=to, device_id_type=pl.DeviceIdType.MESH)

        mine = pltpu.make_async_copy(x_ref, slot(me), local_sem)
        mine.start()
        first = [copy(0, me, sibling, src=x_ref)]
        first += [copy(1 + j, me, (*chip, pc), src=x_ref) for j, chip in enumerate(chips)]
        for cp in first:
            cp.start()
        passed = [copy(4 + j, (*chip, pc), sibling) for j, chip in enumerate(chips)]
        for j, chip in enumerate(chips):
            copy(1 + j, (*chip, pc), me).wait_recv()
            passed[j].start()
        copy(0, sibling, me).wait_recv()
        for j, chip in enumerate(chips):
            copy(4 + j, (*chip, 1 - pc), me).wait_recv()
        for cp in first + passed:
            cp.wait_send()
        mine.wait()

    any_spec = pl.BlockSpec(memory_space=pl.ANY)
    return pl.pallas_call(
        body, name=name, in_specs=[any_spec], out_specs=any_spec,
        out_shape=jax.ShapeDtypeStruct((N_DEV,) + x.shape, x.dtype),
        scratch_shapes=[pltpu.SemaphoreType.DMA((N_DEV - 1,)), pltpu.SemaphoreType.DMA((N_DEV - 1,)),
                        pltpu.SemaphoreType.DMA])(x)


def _call(name, body, grid, ins, in_specs, out_specs, out_shape, scratch=(), exch=None):
    if exch is None:
        return pl.pallas_call(body, name=name, grid=grid, in_specs=in_specs, out_specs=out_specs,
                              out_shape=out_shape, scratch_shapes=list(scratch), compiler_params=_params())(*ins)
    xs, gather = exch
    single = not isinstance(out_shape, (list, tuple))
    o_specs = [out_specs] if single else list(out_specs)
    o_shape = [out_shape] if single else list(out_shape)
    n_in, n_out, n_x, n_scr = len(ins), len(o_shape), len(xs), len(scratch)

    def wrapped(*refs):
        in_refs = refs[:n_in]
        x_refs = refs[n_in:n_in + n_x]
        out_refs = refs[n_in + n_x:n_in + n_x + n_out]
        got_refs = refs[n_in + n_x + n_out:n_in + 2 * n_x + n_out]
        rest = refs[n_in + 2 * n_x + n_out:]
        starts, waits = _exchange_copies(x_refs, got_refs, *rest[n_scr:], gather)
        ids = [pl.program_id(i) for i in range(len(grid))]
        first = functools.reduce(lambda p, q: p & q, [i == 0 for i in ids])
        last = functools.reduce(lambda p, q: p & q, [i == g - 1 for i, g in zip(ids, grid)])

        @pl.when(first)
        def _():
            for f in starts:
                f()

        body(*in_refs, *out_refs, *rest[:n_scr])

        @pl.when(last)
        def _():
            for f in waits:
                f()

    any_spec = pl.BlockSpec(memory_space=pl.ANY)
    outs = pl.pallas_call(
        wrapped, name=name, grid=grid, in_specs=list(in_specs) + [any_spec] * n_x,
        out_specs=o_specs + [any_spec] * n_x, out_shape=o_shape + _exchange_shapes(xs, gather),
        scratch_shapes=list(scratch) + _exchange_sems(n_x), compiler_params=_params())(*ins, *xs)
    res = outs[:n_out]
    return (res[0] if single else res), outs[n_out:]


def _mm_call(name, dims, grid, red_axis, ins, in_specs, out_shape, out_spec, acc_shape,
             scale=1.0, add_scale=None, exch=None):
    nred = grid[red_axis]

    def body(*refs):
        if add_scale is None:
            a_ref, b_ref, o_ref, acc_ref = refs
            add_ref = None
        else:
            a_ref, b_ref, add_ref, o_ref, acc_ref = refs
        k = pl.program_id(red_axis)

        @pl.when(k == 0)
        def _():
            acc_ref[...] = jnp.zeros_like(acc_ref)

        acc_ref[...] += _dot(a_ref[...].astype(BF16), b_ref[...].astype(BF16), dims)

        @pl.when(k == nred - 1)
        def _():
            r = acc_ref[...]
            if scale != 1.0:
                r = r * scale
            if add_ref is not None:
                r = r + add_scale * add_ref[...].astype(F32)
            o_ref[...] = r.astype(o_ref.dtype)

    return _call(name, body, grid, ins, in_specs, out_spec, out_shape, [pltpu.VMEM(acc_shape, F32)], exch)


def _tile(dim, cap):
    t = min(dim, cap)
    while dim % t or t % 128:
        t -= 128
        assert t > 0, (dim, cap)
    return t


def _mm(name, a, b, dims, out_dtype, scale=1.0, add=None, add_scale=None, tm=512, tn=1024, tk=1024):
    if dims == "tn":
        kd, m = a.shape
        n = b.shape[1]
    else:
        m, kd = a.shape
        n = b.shape[1] if dims == "nn" else b.shape[0]
    tm, tn, tk = _tile(m, tm), _tile(n, tn), _tile(kd, tk)
    a_spec = (pl.BlockSpec((tk, tm), lambda i, j, k: (k, i)) if dims == "tn"
              else pl.BlockSpec((tm, tk), lambda i, j, k: (i, k)))
    b_spec = (pl.BlockSpec((tn, tk), lambda i, j, k: (j, k)) if dims == "nt"
              else pl.BlockSpec((tk, tn), lambda i, j, k: (k, j)))
    o_spec = pl.BlockSpec((tm, tn), lambda i, j, k: (i, j))
    ins, specs = [a, b], [a_spec, b_spec]
    if add is not None:
        ins.append(add)
        specs.append(o_spec)
    return _mm_call(name, dims, (m // tm, n // tn, kd // tk), 2, ins, specs,
                    jax.ShapeDtypeStruct((m, n), out_dtype), o_spec, (tm, tn),
                    scale=scale, add_scale=add_scale if add is not None else None)


def _ffn_in(name, x, wg, tm=512, exch=None):
    n = x.shape[0]
    nj = N_DEV // 2

    def body(x_ref, wgate_ref, wup_ref, hg_ref, hu_ref, act_ref):
        xb = x_ref[...].astype(BF16)
        g = _dot(xb, wgate_ref[...])
        u = _dot(xb, wup_ref[...])
        hg_ref[...] = g.astype(BF16)
        hu_ref[...] = u.astype(BF16)
        act_ref[...] = (g * _sigmoid(g) * u).astype(BF16)

    blk = pl.BlockSpec((None, tm, FF_BLK), lambda j, i: (j, i, 0))
    shp = jax.ShapeDtypeStruct((nj, n, FF_BLK), BF16)
    return _call(name, body, (nj, n // tm), [x, wg, wg],
                 [pl.BlockSpec((tm, D_MODEL), lambda j, i: (i, 0)),
                  pl.BlockSpec((None, D_MODEL, FF_BLK), lambda j, i: (j, 0, 0)),
                  pl.BlockSpec((None, D_MODEL, FF_BLK), lambda j, i: (j + nj, 0, 0))],
                 [blk, blk, blk], [shp, shp, shp], exch=exch)


def _ffn_out_bwd(name, dz, wout, hg, hu, act, tm=512, exch=None):
    n = dz.shape[0]
    nj, ni = N_DEV // 2, n // tm

    def body(dz_ref, w_ref, hg_ref, hu_ref, act_ref, dhg_ref, dhu_ref, dw_ref, acc_ref):
        i = pl.program_id(1)
        dzb = dz_ref[...].astype(BF16)
        dact = 0.5 * _dot(dzb, w_ref[...], "nt")
        g = hg_ref[...].astype(F32)
        u = hu_ref[...].astype(F32)
        s = _sigmoid(g)
        dhg_ref[...] = (dact * u * (s * (1.0 + g * (1.0 - s)))).astype(BF16)
        dhu_ref[...] = (dact * (g * s)).astype(BF16)

        @pl.when(i == 0)
        def _():
            acc_ref[...] = jnp.zeros_like(acc_ref)

        acc_ref[...] += _dot(act_ref[...], dzb, "tn")

        @pl.when(i == ni - 1)
        def _():
            dw_ref[...] = (0.5 * acc_ref[...]).astype(dw_ref.dtype)

    blk = pl.BlockSpec((None, tm, FF_BLK), lambda j, i: (j, i, 0))
    wblk = pl.BlockSpec((FF_BLK, D_MODEL), lambda j, i: (j, 0))
    shp = jax.ShapeDtypeStruct((nj, n, FF_BLK), BF16)
    return _call(name, body, (nj, ni), [dz, wout, hg, hu, act],
                 [pl.BlockSpec((tm, D_MODEL), lambda j, i: (i, 0)), wblk, blk, blk, blk],
                 [blk, blk, wblk], [shp, shp, jax.ShapeDtypeStruct((D_FF, D_MODEL), BF16)],
                 [pltpu.VMEM((FF_BLK, D_MODEL), F32)], exch=exch)


def _mm_ln(name, a, b, xres, g, beta, c, tgt=None, a_blocked=False, tm=512, tk=512):
    if a_blocked:
        nj, n, kj = a.shape
        nk, kb = nj // 2, 2 * kj
        a_spec = pl.BlockSpec((2, tm, kj), lambda i, k: (k, i, 0))
    else:
        n, kd = a.shape
        kb = _tile(kd, tk)
        nk = kd // kb
        a_spec = pl.BlockSpec((tm, kb), lambda i, k: (i, k))
    d = b.shape[1]
    with_loss = tgt is not None

    def body(*refs):
        if with_loss:
            a_ref, b_ref, x_ref, g_ref, be_ref, t_ref, o_ref, z_ref, l_ref, acc_ref = refs
        else:
            a_ref, b_ref, x_ref, g_ref, be_ref, o_ref, ob_ref, z_ref, acc_ref = refs
        i, k = pl.program_id(0), pl.program_id(1)

        @pl.when(k == 0)
        def _():
            acc_ref[...] = jnp.zeros_like(acc_ref)

        if a_blocked:
            acc_ref[...] += (_dot(a_ref[0].astype(BF16), b_ref[0:kb // 2, :].astype(BF16))
                             + _dot(a_ref[1].astype(BF16), b_ref[kb // 2:kb, :].astype(BF16)))
        else:
            acc_ref[...] += _dot(a_ref[...].astype(BF16), b_ref[...].astype(BF16))

        @pl.when(k == nk - 1)
        def _():
            z = ALPHA * x_ref[...] + c * acc_ref[...]
            z_ref[...] = z
            mu = jnp.mean(z, axis=-1, keepdims=True)
            zc = z - mu
            var = jnp.mean(zc * zc, axis=-1, keepdims=True)
            y = zc * lax.rsqrt(var + LN_EPS) * g_ref[...] + be_ref[...]
            if with_loss:
                err = y - t_ref[...]
                o_ref[...] = err * (1.0 / d)
                part = 0.5 * jnp.sum(jnp.sum(err * err, axis=-1, keepdims=True), axis=0, keepdims=True) * (1.0 / d)

                @pl.when(i == 0)
                def _():
                    l_ref[...] = jnp.zeros_like(l_ref)

                l_ref[...] += jnp.broadcast_to(part, l_ref.shape)
            else:
                o_ref[...] = y
                ob_ref[...] = y.astype(BF16)

    row = pl.BlockSpec((tm, d), lambda i, k: (i, 0))
    vec = pl.BlockSpec((1, d), lambda i, k: (0, 0))
    ins = [a, b, xres, g, beta]
    in_specs = [a_spec, pl.BlockSpec((kb, d), lambda i, k: (k, 0)), row, vec, vec]
    out_specs = [row, row]
    out_shape = [jax.ShapeDtypeStruct((n, d), F32), jax.ShapeDtypeStruct((n, d), F32)]
    if not with_loss:
        out_specs.insert(1, row)
        out_shape.insert(1, jax.ShapeDtypeStruct((n, d), BF16))
    if with_loss:
        ins.append(tgt)
        in_specs.append(row)
        out_specs.append(pl.BlockSpec((1, 128), lambda i, k: (0, 0)))
        out_shape.append(jax.ShapeDtypeStruct((1, 128), F32))
    return _call(name, body, (n // tm, nk), ins, in_specs, out_specs, out_shape, [pltpu.VMEM((tm, d), F32)])


def _rowwise(name, fn, rows, params, out_rows, out_accs, tm=256, exch=None):
    specs, ins = [], []
    for r in rows:
        arr, w, cb = r if isinstance(r, tuple) else (r, r.shape[1], 0)
        ins.append(arr)
        specs.append(pl.BlockSpec((tm, w), functools.partial(lambda i, cb: (i, cb), cb=cb)))
    n = ins[0].shape[0]
    for p in params:
        ins.append(p)
        specs.append(pl.BlockSpec(p.shape, lambda i: (0, 0)))
    n_in, n_or = len(ins), len(out_rows)

    def body(*refs):
        outs = fn(*[r[...] for r in refs[:n_in]])
        o_refs = refs[n_in:]
        for o_ref, o in zip(o_refs[:n_or], outs[:n_or]):
            o_ref[...] = o.astype(o_ref.dtype)
        if out_accs:
            @pl.when(pl.program_id(0) == 0)
            def _():
                for a_ref in o_refs[n_or:]:
                    a_ref[...] = jnp.zeros_like(a_ref)

            for a_ref, a in zip(o_refs[n_or:], outs[n_or:]):
                a_ref[...] += a.astype(F32)

    out_specs = [pl.BlockSpec((tm, w), lambda i: (i, 0)) for w, _ in out_rows]
    out_specs += [pl.BlockSpec(s, lambda i: (0, 0)) for s in out_accs]
    out_shape = [jax.ShapeDtypeStruct((n, w), dt) for w, dt in out_rows]
    out_shape += [jax.ShapeDtypeStruct(s, F32) for s in out_accs]
    return _call(name, body, (n // tm,), ins, specs, out_specs, out_shape, exch=exch)


def _vjp_of(fn, n_in):
    def g(*args):
        ins, cts = args[:n_in], args[n_in:]
        outs, pull = jax.vjp(fn, *ins)
        return pull(tuple(c.astype(o.dtype) for c, o in zip(cts, outs)))
    return g


def _ln_bwd(name, z, g, ct):
    def fn(zt, ct_, gt):
        mu = jnp.mean(zt, axis=-1, keepdims=True)
        zc = zt - mu
        rstd = lax.rsqrt(jnp.mean(zc * zc, axis=-1, keepdims=True) + LN_EPS)
        xh = zc * rstd
        dxh = ct_ * gt
        dz = rstd * (dxh - jnp.mean(dxh, axis=-1, keepdims=True)
                     - xh * jnp.mean(dxh * xh, axis=-1, keepdims=True))
        return dz, dz, jnp.sum(ct_ * xh, axis=0, keepdims=True), jnp.sum(ct_, axis=0, keepdims=True)

    d = z.shape[1]
    return _rowwise(name, fn, [z, ct], [g], [(d, F32), (d, BF16)], [(1, d), (1, d)])


SHIFT_TILE = 256


def _shift_specs(t):
    r8 = SHIFT_TILE // 8
    return [pl.BlockSpec((None, SHIFT_TILE, SHIFT_PAD), lambda b, i: (b, i, 0)),
            pl.BlockSpec((None, 8, SHIFT_PAD), lambda b, i: (b, jnp.maximum(i * r8 - 1, 0), 0)),
            pl.BlockSpec((None, 8, SHIFT_PAD), lambda b, i: (b, jnp.minimum((i + 1) * r8, t // 8 - 1), 0))]


def _neighbour_diffs(cur, prev_ref, next_ref, i, nt):
    prow = jnp.where(i > 0, prev_ref[7:8, :], 0.0)
    nrow = jnp.where(i < nt - 1, next_ref[0:1, :], 0.0)
    rid = lax.broadcasted_iota(jnp.int32, cur.shape, 0)
    return (jnp.where(rid == 0, prow, pltpu.roll(cur, 1, 0)) - cur,
            jnp.where(rid == SHIFT_TILE - 1, nrow, pltpu.roll(cur, SHIFT_TILE - 1, 0)) - cur)


def _shift(name, src, mu_prev, mu_next):
    bsz, t, _ = src.shape
    nt, w = t // SHIFT_TILE, SHIFT_PAD

    def body(cur_ref, prev_ref, next_ref, mp_ref, mn_ref, o_ref):
        cur = cur_ref[...]
        dprev, dnext = _neighbour_diffs(cur, prev_ref, next_ref, pl.program_id(1), nt)
        o_ref[...] = cur + mp_ref[...] * dprev + mn_ref[...] * dnext

    specs = _shift_specs(t)
    vec = pl.BlockSpec((1, w), lambda b, i: (0, 0))
    return _call(name, body, (bsz, nt), [src, src, src, mu_prev, mu_next], specs + [vec, vec], specs[0],
                 jax.ShapeDtypeStruct((bsz, t, w), F32))


def _shift_bwd(name, dps, p, mu_prev, mu_next):
    bsz, t, _ = dps.shape
    nt, w = t // SHIFT_TILE, SHIFT_PAD

    def body(d_ref, dprev_ref, dnext_ref, p_ref, pprev_ref, pnext_ref, mp_ref, mn_ref, o_ref, da_ref, db_ref):
        b, i = pl.program_id(0), pl.program_id(1)
        dcur = d_ref[...]
        d_dprev, d_dnext = _neighbour_diffs(dcur, dprev_ref, dnext_ref, i, nt)
        o_ref[...] = (dcur + mn_ref[...] * d_dprev + mp_ref[...] * d_dnext).astype(o_ref.dtype)
        p_dprev, p_dnext = _neighbour_diffs(p_ref[...], pprev_ref, pnext_ref, i, nt)

        @pl.when((b == 0) & (i == 0))
        def _():
            da_ref[...] = jnp.zeros_like(da_ref)
            db_ref[...] = jnp.zeros_like(db_ref)

        da_ref[...] += jnp.sum(dcur * p_dprev, axis=0, keepdims=True)
        db_ref[...] += jnp.sum(dcur * p_dnext, axis=0, keepdims=True)

    specs = _shift_specs(t)
    vec = pl.BlockSpec((1, w), lambda b, i: (0, 0))
    return _call(name, body, (bsz, nt), [dps, dps, dps, p, p, p, mu_prev, mu_next], specs + specs + [vec, vec],
                 [specs[0], vec, vec],
                 [jax.ShapeDtypeStruct((bsz, t, w), BF16), jax.ShapeDtypeStruct((1, w), F32),
                  jax.ShapeDtypeStruct((1, w), F32)])


CONV_BLK = 128


def _halo_specs(t, tt, w):
    r16 = tt // 16
    return [pl.BlockSpec((None, tt, w), lambda b, i: (b, i, 0)),
            pl.BlockSpec((None, 16, w), lambda b, i: (b, jnp.maximum(i * r16 - 1, 0), 0)),
            pl.BlockSpec((None, 16, w), lambda b, i: (b, jnp.minimum((i + 1) * r16, t // 16 - 1), 0))]


def _fill_pad(pad_ref, cur_ref, prev_ref, next_ref, i, nt, tt):
    pad_ref[0:16, :] = jnp.where(i > 0, prev_ref[...], 0.0)
    pad_ref[16:16 + tt, :] = cur_ref[...]
    pad_ref[16 + tt:32 + tt, :] = jnp.where(i < nt - 1, next_ref[...], 0.0)


def _dwconv(name, u, dw32, bias, flip, tt=512):
    bsz, t, w = u.shape
    tt = min(tt, t)
    nt = t // tt

    def body(cur_ref, prev_ref, next_ref, dw_ref, b_ref, o_ref, pad_ref):
        i = pl.program_id(1)
        _fill_pad(pad_ref, cur_ref, prev_ref, next_ref, i, nt, tt)
        for r0 in range(0, tt, CONV_BLK):
            for cs in (slice(c0, c0 + CONV_BLK) for c0 in range(0, w, CONV_BLK)):
                acc = jnp.broadcast_to(b_ref[:, cs], (CONV_BLK, CONV_BLK))
                for k in range(CONV_K):
                    kk = CONV_K - 1 - k if flip else k
                    acc = acc + pad_ref[pl.ds(r0 + 1 + k, CONV_BLK), cs] * dw_ref[kk:kk + 1, cs]
                o_ref[r0:r0 + CONV_BLK, cs] = acc

    return _call(name, body, (bsz, nt), [u, u, u, dw32, bias],
                 _halo_specs(t, tt, w) + [pl.BlockSpec((32, w), lambda b, i: (0, 0)),
                                          pl.BlockSpec((1, w), lambda b, i: (0, 0))],
                 pl.BlockSpec((None, tt, w), lambda b, i: (b, i, 0)), jax.ShapeDtypeStruct((bsz, t, w), F32),
                 [pltpu.VMEM((tt + 32, w), F32)])


def _dwconv_dw(name, u, dc, tt=512):
    bsz, t, w = u.shape
    tt = min(tt, t)
    nt = t // tt

    def body(cur_ref, prev_ref, next_ref, dc_ref, ddw_ref, db_ref, pad_ref):
        b, i = pl.program_id(0), pl.program_id(1)
        _fill_pad(pad_ref, cur_ref, prev_ref, next_ref, i, nt, tt)

        @pl.when((b == 0) & (i == 0))
        def _():
            ddw_ref[...] = jnp.zeros_like(ddw_ref)
            db_ref[...] = jnp.zeros_like(db_ref)

        dcv = dc_ref[...]
        db_ref[...] += jnp.sum(dcv, axis=0, keepdims=True)
        for k in range(CONV_K):
            ddw_ref[k:k + 1, :] += jnp.sum(dcv * pad_ref[pl.ds(1 + k, tt), :], axis=0, keepdims=True)

    return _call(name, body, (bsz, nt), [u, u, u, dc],
                 _halo_specs(t, tt, w) + [pl.BlockSpec((None, tt, w), lambda b, i: (b, i, 0))],
                 [pl.BlockSpec((32, w), lambda b, i: (0, 0)), pl.BlockSpec((1, w), lambda b, i: (0, 0))],
                 [jax.ShapeDtypeStruct((32, w), F32), jax.ShapeDtypeStruct((1, w), F32)],
                 [pltpu.VMEM((tt + 32, w), F32)])


PAIR = 2 * HEAD
N_PAIRS = RW // PAIR


def _chunk_pairs(s, r, lw, k, v, kk, a, sgn, tm_known=None):
    n, m = CHUNK, 2 * CHUNK
    in_a = lax.broadcasted_iota(jnp.int32, (n, PAIR), 1) < HEAD

    def stack2(z):
        return jnp.concatenate([jnp.where(in_a, z, 0.0), jnp.where(in_a, 0.0, z)], axis=0)

    def each(f, *lists):
        return [f(*z) for z in zip(*lists)]

    sgn_f = sgn.astype(F32)
    row2 = lax.broadcasted_iota(jnp.int32, (m, m), 0)
    col2 = lax.broadcasted_iota(jnp.int32, (m, m), 1)
    same = (row2 >= n) == (col2 >= n)
    dlt = ((row2 & (n - 1)) - (col2 & (n - 1))) * sgn
    incl, strict = same & (dlt >= 0), same & (dlt > 0)
    eye = jnp.where(row2 == col2, 1.0, 0.0)

    cum = each(lambda lw_: _dir_cumsum(lw_, sgn_f), lw)
    tot = each(lambda lw_: jnp.sum(lw_, axis=0, keepdims=True), lw)
    e_neg = each(lambda c_: jnp.exp(-c_), cum)
    e_rest = each(lambda t_, c_: jnp.exp(t_ - c_), tot, cum)
    beta = each(lambda kk_, a_: kk_ * a_, kk, a)
    lhs = each(lambda kk_, c_, lw_, r_: jnp.concatenate(
        [stack2(-kk_ * jnp.exp(c_ - lw_)), stack2(r_ * jnp.exp(c_))], axis=0), kk, cum, lw, r)
    rhs = each(lambda b_, k_, e_: jnp.concatenate([stack2(b_ * e_), stack2(k_ * e_)], axis=0), beta, k, e_neg)
    sc = each(lambda l_, r_: _dot3(l_, r_, "nt"), lhs, rhs)
    l_ab = each(lambda sc_: jnp.where(strict, sc_[0:m, 0:m], 0.0), sc)
    l_ak = each(lambda sc_: jnp.where(strict, sc_[0:m, m:2 * m], 0.0), sc)
    m_r = each(lambda sc_: jnp.where(jnp.concatenate([incl, incl], axis=1), sc_[m:2 * m, :], 0.0), sc)
    tm = _tri_inv(l_ab, eye) if tm_known is None else _tri_inv_known(l_ab, tm_known)
    z = _dot1_two(lhs, s, "nt")
    v2 = each(stack2, v)
    u2 = _dot1_two(tm, each(lambda z_, lv_: z_[0:m] + lv_, z, _dot1_two(l_ak, v2)))
    uv = each(lambda u_, v_: jnp.concatenate([u_, v_], axis=0), u2, v2)
    y2 = each(lambda z_, mu_: z_[m:2 * m] + mu_, z, _dot1_two(m_r, uv))
    bk = each(lambda b_, k_, e_: jnp.concatenate([stack2(b_ * e_), stack2(k_ * e_)], axis=0), beta, k, e_rest)
    s_new = each(lambda s_, t_, d_: s_ * jnp.exp(t_) + d_, s, tot, _dot1_two(uv, bk, "tn"))
    return each(lambda y_: y_[0:n] + y_[n:m], y2), s_new, tm


SCAN_SEQS = 4
N_CHAINS = SCAN_SEQS * N_PAIRS


def _pair_tiles(ref):
    return [ref[q, :, p * PAIR:(p + 1) * PAIR] for q in range(SCAN_SEQS) for p in range(N_PAIRS)]


def _store_tiles(ref, tiles):
    for q in range(SCAN_SEQS):
        for p in range(N_PAIRS):
            ref[q, :, p * PAIR:(p + 1) * PAIR] = tiles[q * N_PAIRS + p]


def _scan_specs(order):
    shared = pl.BlockSpec((SCAN_SEQS, CHUNK, RW), lambda d, b, c: (b, order(d, c), 0))
    per_dir = pl.BlockSpec((SCAN_SEQS, CHUNK, RW), lambda d, b, c: (b, order(d, c), d))
    state = pl.BlockSpec((None, SCAN_SEQS, None, N_PAIRS, PAIR, PAIR), lambda d, b, c: (d, b, order(d, c), 0, 0, 0))
    return shared, per_dir, state


def _scan_fwd(r, v, kk, lw, kd, a, bsz, exch=None):
    n = r.shape[0]
    t = n // bsz
    nc = t // CHUNK

    def order(d, c):
        return c + d * (nc - 1 - 2 * c)

    def body(r_ref, v_ref, kk_ref, lw_ref, kd_ref, a_ref, y_ref, s0_ref, tm_ref, s_ref):
        d, c = pl.program_id(0), pl.program_id(2)

        @pl.when(c == 0)
        def _():
            s_ref[...] = jnp.zeros_like(s_ref)

        s = [s_ref[i] for i in range(N_CHAINS)]
        y, s_new, tm = _chunk_pairs(s, *[_pair_tiles(ref) for ref in (r_ref, lw_ref, kd_ref, v_ref, kk_ref, a_ref)],
                                    1 - 2 * d)
        _store_tiles(y_ref, y)
        for i in range(N_CHAINS):
            s0_ref[i // N_PAIRS, i % N_PAIRS] = s[i]
            tm_ref[i // N_PAIRS, i % N_PAIRS] = tm[i].astype(BF16)
            s_ref[i] = s_new[i]

    shared, per_dir, state = _scan_specs(order)
    seq = lambda z: z.reshape(bsz, t, z.shape[1])
    res = _call("scan_fwd", body, (2, bsz // SCAN_SEQS, nc), [seq(z) for z in (r, v, kk, lw, kd, a)],
                [shared, shared, shared, per_dir, per_dir, per_dir], [per_dir, state, state],
                [jax.ShapeDtypeStruct((bsz, t, 2 * RW), F32),
                 jax.ShapeDtypeStruct((2, bsz, nc, N_PAIRS, PAIR, PAIR), F32),
                 jax.ShapeDtypeStruct((2, bsz, nc, N_PAIRS, PAIR, PAIR), BF16)],
                [pltpu.VMEM((N_CHAINS, PAIR, PAIR), F32)], exch)
    (y, s0, tm), got = res if exch else (res, None)
    y = y.reshape(n, 2 * RW)
    return ([y, s0, tm], got) if exch else [y, s0, tm]


def _scan_bwd(r, v, kk, lw, kd, a, s0, tm, dy, bsz, exch=None):
    n = r.shape[0]
    t = n // bsz
    nc = t // CHUNK

    def order(d, c):
        cc = nc - 1 - c
        return cc + d * (nc - 1 - 2 * cc)

    def body(r_ref, v_ref, kk_ref, lw_ref, kd_ref, a_ref, dy_ref, s0_ref, tm_ref,
             dr_ref, dv_ref, dkk_ref, dlw_ref, dkd_ref, da_ref, ds_ref):
        d, c = pl.program_id(0), pl.program_id(2)

        @pl.when(c == 0)
        def _():
            ds_ref[...] = jnp.zeros_like(ds_ref)

        sgn = 1 - 2 * d
        tm_known = [tm_ref[i // N_PAIRS, i % N_PAIRS].astype(F32) for i in range(N_CHAINS)]
        _, pull = jax.vjp(lambda *ops: _chunk_pairs(*ops, sgn, tm_known)[:2],
                          [s0_ref[i // N_PAIRS, i % N_PAIRS] for i in range(N_CHAINS)],
                          *[_pair_tiles(ref) for ref in (r_ref, lw_ref, kd_ref, v_ref, kk_ref, a_ref)])
        grads = pull((_pair_tiles(dy_ref), [ds_ref[i] for i in range(N_CHAINS)]))
        for i in range(N_CHAINS):
            ds_ref[i] = grads[0][i]
        for o_ref, gx in zip((dr_ref, dlw_ref, dkd_ref, dv_ref, dkk_ref, da_ref), grads[1:]):
            _store_tiles(o_ref, gx)

    shared, per_dir, state = _scan_specs(order)
    shp = jax.ShapeDtypeStruct((bsz, t, 2 * RW), F32)
    seq = lambda z: z.reshape(bsz, t, z.shape[1])
    res = _call("scan_bwd", body, (2, bsz // SCAN_SEQS, nc), [seq(z) for z in (r, v, kk, lw, kd, a, dy)] + [s0, tm],
                [shared, shared, shared, per_dir, per_dir, per_dir, per_dir, state, state],
                [per_dir] * 6, [shp] * 6, [pltpu.VMEM((N_CHAINS, PAIR, PAIR), F32)], exch)
    outs, got = res if exch else (res, None)
    outs = [z.reshape(n, 2 * RW) for z in outs]
    return (outs, got) if exch else outs


def _prep_fn(ps, w0, w2bd, a0, a2bd, g2p, k_k, k_a, hsum):
    head_sum = lambda z: _head_sum(z, hsum)
    r, k, v = ps[:, 0:RW], ps[:, RW:2 * RW], ps[:, 2 * RW:3 * RW]
    wd, ad, gd = ps[:, 1536:1664], ps[:, 1664:1792], ps[:, 1792:2048]
    logw = -DECAY_SCALE * _sigmoid(_dot1(jnp.tanh(wd), w2bd) + w0)
    a = _sigmoid(_dot1(ad, a2bd) + a0)
    g = _dot1(_sigmoid(gd), g2p)
    kkr = k * k_k
    kk = kkr / jnp.maximum(jnp.sqrt(head_sum(kkr * kkr)), NORM_EPS)
    k2 = jnp.concatenate([k, k], axis=1)
    ka2 = jnp.concatenate([k_a, k_a], axis=1)
    kd = k2 * (1.0 + (a - 1.0) * ka2)
    return r, v, kk, logw, a, kd, g


def _post_fn(y2, r, v, kd, g, lnx_g, lnx_b, r_k, hsum):
    head_sum = lambda z: _head_sum(z, hsum)
    y = y2[:, 0:RW] + y2[:, RW:2 * RW]
    mu = head_sum(y) * (1.0 / HEAD)
    yc = y - mu
    var = head_sum(yc * yc) * (1.0 / HEAD)
    yn = yc * lax.rsqrt(var + GN_EPS) * lnx_g + lnx_b
    bonus = head_sum(r * (kd[:, 0:RW] + kd[:, RW:2 * RW]) * r_k) * v
    return ((yn + bonus) * g,)


def _glu_fn(pa, pb):
    return (pa * _sigmoid(pb),)


def _conv_out_fn(cv, ln_g, ln_b):
    mu = jnp.mean(cv, axis=-1, keepdims=True)
    cc = cv - mu
    var = jnp.mean(cc * cc, axis=-1, keepdims=True)
    y = cc * lax.rsqrt(var + LN_EPS) * ln_g + ln_b
    return (y * _sigmoid(y),)


def _local_step(x, tgt, w, ex=None):
    bsz, t, d = x.shape
    n = bsz * t
    x2d, tgt2d = x.reshape(n, d), tgt.reshape(n, d)
    hsum = jnp.tile(jnp.kron(jnp.eye(N_HEADS, dtype=BF16), jnp.ones((HEAD, HEAD), BF16)), (3, 1))
    w = dict(w)
    parts = {} if ex else None

    def hosted(result, finish=None):
        if not ex:
            return result
        outs, got = result
        if finish is not None:
            w.update(finish(got))
        return outs

    xb = x2d.astype(BF16)
    hg1, hu1, act1 = hosted(_ffn_in("ffn1_in", xb, w["ffn1_w_in"], exch=(ex["g1"][0], True) if ex else None),
                            ex["g1"][1] if ex else None)
    x1, x1b, z1 = _mm_ln("ffn1_out_ln1", act1, w["ffn1_w_out"], x2d, w["ln1_g"], w["ln1_b"], 0.5, a_blocked=True)
    p = _mm("w_in_proj", x1b, w["w_in"], "nn", F32)
    p3 = p.reshape(bsz, t, IN_PAD)
    ps = _shift("shift_fwd", p3, w["mu_prev"], w["mu_next"]).reshape(n, SHIFT_PAD)
    prep_params = [w["w0"], w["w2"], w["a0"], w["a2"], w["g2"], w["k_k"], w["k_a"], hsum]
    r, v, kk, logw, a, kd, g = _rowwise(
        "rwkv_prep", _prep_fn, [ps], prep_params,
        [(RW, F32), (RW, F32), (RW, F32), (2 * RW, F32), (2 * RW, F32), (2 * RW, F32), (RW, F32)], [])
    y2, s0, tm = hosted(_scan_fwd(r, v, kk, logw, kd, a, bsz, exch=(ex["g2"][0], True) if ex else None),
                    ex["g2"][1] if ex else None)
    post_params = [w["lnx_g"], w["lnx_b"], w["r_k"], hsum]
    (y_rwkv,) = _rowwise("rwkv_post", _post_fn, [y2, r, v, kd, g], post_params, [(RW, BF16)], [])
    (u,) = _rowwise("conv_glu", _glu_fn, [(p, CW, 4), (p, CW, 5)], [], [(CW, F32)], [])
    cv = _dwconv("conv_dw", u.reshape(bsz, t, CW), w["conv_dw"], w["conv_b"], False).reshape(n, CW)
    (y_conv,) = _rowwise("conv_out", _conv_out_fn, [cv], [w["conv_ln_g"], w["conv_ln_b"]], [(CW, BF16)], [])
    ycat = jnp.concatenate([y_rwkv, y_conv], axis=1)
    x2, x2b, z2 = _mm_ln("w_out_ln2", ycat, w["w_out"], x1, w["ln2_g"], w["ln2_b"], 1.0)
    hg2, hu2, act2 = _ffn_in("ffn2_in", x2b, w["ffn2_w_in"])
    dx3, z3, loss = _mm_ln("ffn2_out_ln3", act2, w["ffn2_w_out"], x2, w["ln3_g"], w["ln3_b"], 0.5,
                           tgt=tgt2d, a_blocked=True)

    gr = {}
    dz3, dz3b, gr["ln3_g"], gr["ln3_b"] = _ln_bwd("ln3_bwd", z3, w["ln3_g"], dx3)
    dx2, _ = _ffn_bwd("ffn2", gr, dz3, dz3b, x2b, w["ffn2_w_in"], w["ffn2_w_out"], hg2, hu2, act2)
    dz2, dz2b, gr["ln2_g"], gr["ln2_b"] = _ln_bwd("ln2_bwd", z2, w["ln2_g"], dx2)
    gr["w_out"] = _mm("w_out_wgrad", ycat, dz2b, "tn", BF16, tk=512)
    dycat = _mm("w_out_dgrad", dz2b, w["w_out"], "nt", F32)
    conv_out_bwd = _vjp_of(_conv_out_fn, 3)
    dcv, gr["conv_ln_g"], gr["conv_ln_b"] = _rowwise(
        "conv_out_bwd", lambda cv_, ct_, g_, b_: conv_out_bwd(cv_, g_, b_, ct_),
        [cv, (dycat, CW, 1)], [w["conv_ln_g"], w["conv_ln_b"]], [(CW, F32)], [(1, CW), (1, CW)])
    dcv3 = dcv.reshape(bsz, t, CW)
    gr["conv_dw"], gr["conv_b"] = _dwconv_dw("conv_dw_wgrad", u.reshape(bsz, t, CW), dcv3)
    du = _dwconv("conv_dw_dgrad", dcv3, w["conv_dw"], jnp.zeros((1, CW), F32), True).reshape(n, CW)

    def glu_bwd(pa, pb, ct):
        return (jnp.concatenate(_vjp_of(_glu_fn, 2)(pa, pb, ct), axis=1),)

    (dp_conv,) = _rowwise("conv_glu_bwd", glu_bwd, [(p, CW, 4), (p, CW, 5), du], [], [(2 * CW, BF16)], [])

    def post_bwd(y2_, r_, v_, kd_, g_, ct, lg, lb, rk, hs):
        return _vjp_of(lambda *z: _post_fn(*z, hs), 8)(y2_, r_, v_, kd_, g_, lg, lb, rk, ct)

    dy2, dr_post, dv_post, dkd_post, dg, gr["lnx_g"], gr["lnx_b"], gr["r_k"] = _rowwise(
        "rwkv_post_bwd", post_bwd, [y2, r, v, kd, g, (dycat, RW, 0)], post_params,
        [(2 * RW, F32), (RW, F32), (RW, F32), (2 * RW, F32), (RW, F32)], [(1, RW), (1, RW), (1, RW)])
    sends = [jnp.concatenate(gr["ffn2_w_in"], axis=0), gr["ffn2_w_out"].reshape(N_DEV, D_FF // N_DEV, D_MODEL),
             gr["w_out"].reshape(N_DEV, D_MODEL // N_DEV, D_MODEL)]
    res = _scan_bwd(r, v, kk, logw, kd, a, s0, tm, dy2, bsz, exch=(sends, False) if ex else None)
    if ex:
        res, got = res
        parts.update(zip(("ffn2_w_in", "ffn2_w_out", "w_out"), got))
    dr_s, dv_s, dkk_s, dlw, dkd_s, da = res

    def prep_bwd(ps_, dr2, dr1, dv2, dv1, dkk2, dlw_, da_, dkd2, dkd1, dg_, *prm):
        half = lambda z: z[:, 0:RW] + z[:, RW:2 * RW]
        return _vjp_of(lambda *z: _prep_fn(*z, prm[-1]), 8)(
            ps_, *prm[:-1], half(dr2) + dr1, half(dv2) + dv1, half(dkk2), dlw_, da_, dkd2 + dkd1, dg_)

    dps, gr["w0"], gr["w2"], gr["a0"], gr["a2"], gr["g2"], gr["k_k"], gr["k_a"] = _rowwise(
        "rwkv_prep_bwd", prep_bwd,
        [ps, dr_s, dr_post, dv_s, dv_post, dkk_s, dlw, da, dkd_s, dkd_post, dg], prep_params,
        [(SHIFT_PAD, F32)], [q.shape for q in prep_params[:-1]])
    dps3 = dps.reshape(bsz, t, SHIFT_PAD)
    dp_shift, gr["mu_prev"], gr["mu_next"] = _shift_bwd("shift_bwd", dps3, p3, w["mu_prev"], w["mu_next"])
    dp_shift = dp_shift.reshape(n, SHIFT_PAD)
    dp = jnp.concatenate([dp_shift, dp_conv], axis=1)
    gr["w_in"] = _mm("w_in_wgrad", x1b, dp, "tn", BF16, tk=512)
    dx1 = _mm("w_in_dgrad", dp, w["w_in"], "nt", F32, add=dz2, add_scale=ALPHA)
    dz1, dz1b, gr["ln1_g"], gr["ln1_b"] = _ln_bwd("ln1_bwd", z1, w["ln1_g"], dx1)
    riders = None
    if ex:
        gw_in = _unpad_in_cols(gr["w_in"]).reshape(D_MODEL, N_DEV, IN_COLS // N_DEV).transpose(1, 0, 2)
        small = _pack([_grad_small(nm, gr[nm]) for nm in SMALL_SHARDED + SMALL_REPL])
        riders = {"out_bwd": [gw_in, jnp.broadcast_to(small[None], (N_DEV,) + small.shape)], "in_wgrad": []}
    grad_x, got = _ffn_bwd("ffn1", gr, dz1, dz1b, xb, w["ffn1_w_in"], w["ffn1_w_out"], hg1, hu1, act1, riders)
    if ex:
        parts.update(w_in=got["out_bwd"][0], small=got["out_bwd"][1], ffn1_w_out=got["in_wgrad"][0],
                     ffn1_w_in=got["in_dgrad"][0])
    return loss[0, 0], grad_x.reshape(bsz, t, d), gr, parts


def _ffn_in_wgrad(name, xin, dhg, dhu, tk=1024, exch=None):
    n = xin.shape[0]
    nj, nt = N_DEV // 2, n // tk

    def body(x_ref, g_ref, u_ref, og_ref, ou_ref, accg_ref, accu_ref):
        i = pl.program_id(1)

        @pl.when(i == 0)
        def _():
            accg_ref[...] = jnp.zeros_like(accg_ref)
            accu_ref[...] = jnp.zeros_like(accu_ref)

        xt = x_ref[...].astype(BF16).T
        accg_ref[...] += _dot(xt, g_ref[...])
        accu_ref[...] += _dot(xt, u_ref[...])

        @pl.when(i == nt - 1)
        def _():
            og_ref[...] = accg_ref[...].astype(og_ref.dtype)
            ou_ref[...] = accu_ref[...].astype(ou_ref.dtype)

    dh_blk = pl.BlockSpec((None, tk, FF_BLK), lambda j, i: (j, i, 0))
    o_blk = pl.BlockSpec((None, D_MODEL, FF_BLK), lambda j, i: (j, 0, 0))
    shp = jax.ShapeDtypeStruct((nj, D_MODEL, FF_BLK), BF16)
    return _call(name, body, (nj, nt), [xin, dhg, dhu],
                 [pl.BlockSpec((tk, D_MODEL), lambda j, i: (i, 0)), dh_blk, dh_blk], [o_blk, o_blk], [shp, shp],
                 [pltpu.VMEM((D_MODEL, FF_BLK), F32)] * 2, exch)


def _ffn_in_dgrad(name, dhg, dhu, wg, add, add_scale, tm=512, exch=None):
    n = add.shape[0]
    nj = N_DEV // 2

    def body(g_ref, u_ref, wgate_ref, wup_ref, add_ref, o_ref, acc_ref):
        j = pl.program_id(1)

        @pl.when(j == 0)
        def _():
            acc_ref[...] = jnp.zeros_like(acc_ref)

        acc_ref[...] += _dot(g_ref[...], wgate_ref[...], "nt") + _dot(u_ref[...], wup_ref[...], "nt")

        @pl.when(j == nj - 1)
        def _():
            o_ref[...] = acc_ref[...] + add_scale * add_ref[...]

    dh_blk = pl.BlockSpec((None, tm, FF_BLK), lambda i, j: (j, i, 0))
    row = pl.BlockSpec((tm, D_MODEL), lambda i, j: (i, 0))
    return _call(name, body, (n // tm, nj), [dhg, dhu, wg, wg, add],
                 [dh_blk, dh_blk, pl.BlockSpec((None, D_MODEL, FF_BLK), lambda i, j: (j, 0, 0)),
                  pl.BlockSpec((None, D_MODEL, FF_BLK), lambda i, j: (j + nj, 0, 0)), row],
                 row, jax.ShapeDtypeStruct((n, D_MODEL), F32), [pltpu.VMEM((tm, D_MODEL), F32)], exch)


def _ffn_bwd(tag, gr, dz, dzb, xin, wg, wout, hg, hu, act, riders=None):
    own = riders is not None

    def hosted(result):
        return result if own else (result, None)

    (dhg, dhu, gwo), got_a = hosted(_ffn_out_bwd(
        tag + "_out_bwd", dzb, wout, hg, hu, act, exch=(riders["out_bwd"], False) if own else None))
    gr[tag + "_w_out"] = gwo
    send = ([gwo.reshape(N_DEV, D_FF // N_DEV, D_MODEL)] + riders["in_wgrad"], False) if own else None
    dw, got_b = hosted(_ffn_in_wgrad(tag + "_in_wgrad", xin, dhg, dhu, exch=send))
    gr[tag + "_w_in"] = dw
    send = ([jnp.concatenate(dw, axis=0)], False) if own else None
    dx, got_c = hosted(_ffn_in_dgrad(tag + "_in_dgrad", dhg, dhu, wg, dz, ALPHA, exch=send))
    return dx, {"out_bwd": got_a, "in_wgrad": got_b, "in_dgrad": got_c}


def _adam_math(g, w, m, v):
    m = ADAM_B1 * m + (1.0 - ADAM_B1) * g
    v = ADAM_B2 * v + (1.0 - ADAM_B2) * (g * g)
    m_hat = m / (1.0 - ADAM_B1 ** ADAM_STEP)
    v_hat = v / (1.0 - ADAM_B2 ** ADAM_STEP)
    delta = -ADAM_LR * (m_hat / (jnp.sqrt(v_hat) + ADAM_EPS) + ADAM_WD * w)
    return delta, m, v


def _adam(name, parts, w, m, v, tr=128):
    rows, cols = w.shape
    tr = min(tr, rows)
    while rows % tr:
        tr -= 8

    def body(p_ref, w_ref, m_ref, v_ref, g_ref, d_ref, mo_ref, vo_ref):
        g = p_ref[0].astype(F32)
        for s in range(1, N_DEV):
            g = g + p_ref[s].astype(F32)
        g_ref[...] = g
        d_ref[...], mo_ref[...], vo_ref[...] = _adam_math(g, w_ref[...], m_ref[...], v_ref[...])

    blk = pl.BlockSpec((tr, cols), lambda i: (i, 0))
    shp = jax.ShapeDtypeStruct((rows, cols), F32)
    return _call(name, body, (rows // tr,), [parts, w, m, v],
                 [pl.BlockSpec((N_DEV, tr, cols), lambda i: (0, i, 0)), blk, blk, blk], [blk] * 4, [shp] * 4)


def _sum8(name, parts):
    _, rows, cols = parts.shape

    def body(p_ref, o_ref):
        g = p_ref[0]
        for s in range(1, N_DEV):
            g = g + p_ref[s]
        o_ref[...] = g

    return pl.pallas_call(body, name=name, out_shape=jax.ShapeDtypeStruct((rows, cols), F32),
                          compiler_params=_params())(parts)


def _adam_small(name, g, w, m, v):
    def body(g_ref, w_ref, m_ref, v_ref, d_ref, mo_ref, vo_ref):
        d_ref[...], mo_ref[...], vo_ref[...] = _adam_math(g_ref[...], w_ref[...], m_ref[...], v_ref[...])

    shp = jax.ShapeDtypeStruct(g.shape, F32)
    return pl.pallas_call(body, name=name, out_shape=[shp] * 3, compiler_params=_params())(g, w, m, v)


def _pack(arrs, lane=128):
    flat = jnp.concatenate([a.reshape(-1).astype(F32) for a in arrs])
    pad = (-flat.shape[0]) % (8 * lane)
    return jnp.pad(flat, (0, pad)).reshape(-1, lane)


def _unpack(packed, shapes):
    flat, out, off = packed.reshape(-1), [], 0
    for s in shapes:
        sz = math.prod(s)
        out.append(flat[off:off + sz].reshape(s))
        off += sz
    return out


def _pad_in_cols(wfull):
    zeros = jnp.zeros((wfull.shape[0], SHIFT_PAD - SHIFT_COLS), wfull.dtype)
    return jnp.concatenate([wfull[:, :SHIFT_COLS], zeros, wfull[:, SHIFT_COLS:]], axis=1)


def _unpad_in_cols(gfull):
    return jnp.concatenate([gfull[:, :SHIFT_COLS], gfull[:, SHIFT_PAD:]], axis=1)


def _block_diag2(wd):
    z = jnp.zeros_like(wd[0])
    return jnp.concatenate([jnp.concatenate([wd[0], z], axis=1), jnp.concatenate([z, wd[1]], axis=1)], axis=0)


def _unblock_diag2(g):
    return jnp.stack([g[0:64, 0:RW], g[64:128, RW:2 * RW]])


SMALL_SHARDED = ("w0", "w2", "a0", "a2", "g2", "conv_dw")
SMALL_REPL = ("mu_prev", "mu_next", "k_k", "k_a", "r_k", "lnx_g", "lnx_b", "conv_b", "conv_ln_g", "conv_ln_b",
              "ln1_g", "ln1_b", "ln2_g", "ln2_b", "ln3_g", "ln3_b")
BIG = ("ffn1_w_in", "ffn1_w_out", "w_in", "w_out", "ffn2_w_in", "ffn2_w_out")
WEIGHTS = ("ffn1_w_in", "ffn1_w_out", "w_in", "mu_prev", "mu_next", "w0", "w2", "a0", "a2", "g2", "k_k", "k_a",
           "r_k", "lnx_g", "lnx_b", "conv_dw", "conv_b", "conv_ln_g", "conv_ln_b", "w_out", "ffn2_w_in",
           "ffn2_w_out", "ln1_g", "ln1_b", "ln2_g", "ln2_b", "ln3_g", "ln3_b")


def _full_small(name, full):
    if name in ("w0", "a0"):
        return full.reshape(1, 2 * RW)
    if name in ("w2", "a2"):
        return _block_diag2(full)
    if name == "g2":
        return jnp.pad(full, ((0, 256 - GATE_LORA), (0, 0)))
    if name == "conv_dw":
        return jnp.pad(full, ((0, 1), (0, 0)))
    if name in ("mu_prev", "mu_next"):
        return jnp.pad(full.reshape(1, SHIFT_COLS), ((0, 0), (0, SHIFT_PAD - SHIFT_COLS)))
    return full.reshape(1, -1)


def _grad_small(name, g):
    if name in ("w0", "a0"):
        return g.reshape(2, RW)
    if name in ("w2", "a2"):
        return _unblock_diag2(g)
    if name == "g2":
        return g[:GATE_LORA]
    if name == "conv_dw":
        return g[:CONV_K]
    if name in ("mu_prev", "mu_next"):
        return g[0, :SHIFT_COLS]
    if name == "r_k":
        return g.reshape(N_HEADS, HEAD)
    return g.reshape(-1)


def kernel(x, ffn1_w_in, ffn1_w_out, w_in, mu_prev, mu_next, w0, w2, a0, a2, g2, k_k, k_a, r_k, lnx_g, lnx_b, conv_dw, conv_b, conv_ln_g, conv_ln_b, w_out, ffn2_w_in, ffn2_w_out, ln1_g, ln1_b, ln2_g, ln2_b, ln3_g, ln3_b, loss_target, m_ffn1_w_in, m_ffn1_w_out, m_w_in, m_mu_prev, m_mu_next, m_w0, m_w2, m_a0, m_a2, m_g2, m_k_k, m_k_a, m_r_k, m_lnx_g, m_lnx_b, m_conv_dw, m_conv_b, m_conv_ln_g, m_conv_ln_b, m_w_out, m_ffn2_w_in, m_ffn2_w_out, m_ln1_g, m_ln1_b, m_ln2_g, m_ln2_b, m_ln3_g, m_ln3_b, v_ffn1_w_in, v_ffn1_w_out, v_w_in, v_mu_prev, v_mu_next, v_w0, v_w2, v_a0, v_a2, v_g2, v_k_k, v_k_a, v_r_k, v_lnx_g, v_lnx_b, v_conv_dw, v_conv_b, v_conv_ln_g, v_conv_ln_b, v_w_out, v_ffn2_w_in, v_ffn2_w_out, v_ln1_g, v_ln1_b, v_ln2_g, v_ln2_b, v_ln3_g, v_ln3_b):
    args = dict(locals())
    drop = lambda z: z.reshape(z.shape[1:])
    wsh = {n: drop(args[n]) for n in WEIGHTS}
    msh = {n: drop(args["m_" + n]) for n in WEIGHTS}
    vsh = {n: drop(args["v_" + n]) for n in WEIGHTS}
    me = 4 * lax.axis_index("x") + 2 * lax.axis_index("y") + lax.axis_index("c")
    bf = {n: wsh[n].astype(BF16) for n in BIG}

    w = {"ffn1_w_in": _gather_two_level("gather_ffn1_w_in", bf["ffn1_w_in"])}
    for n in SMALL_REPL:
        w[n] = _full_small(n, wsh[n])
    small_shapes = [wsh[n].shape for n in SMALL_SHARDED]

    def finish1(got):
        f1_out, w_in_g, small = got
        cols = zip(*[_unpack(small[dv], small_shapes) for dv in range(N_DEV)])
        out = {n: _full_small(n, jnp.concatenate(s, axis=-1)) for n, s in zip(SMALL_SHARDED, cols)}
        out["ffn1_w_out"] = f1_out.reshape(D_FF, D_MODEL)
        out["w_in"] = _pad_in_cols(w_in_g.transpose(1, 0, 2).reshape(D_MODEL, IN_COLS))
        return out

    def finish2(got):
        w_out_g, f2_in, f2_out = got
        return {"w_out": w_out_g.reshape(D_MODEL, D_MODEL), "ffn2_w_in": f2_in,
                "ffn2_w_out": f2_out.reshape(D_FF, D_MODEL)}

    ex = {"g1": ([bf["ffn1_w_out"], bf["w_in"], _pack([wsh[n] for n in SMALL_SHARDED])], finish1),
          "g2": ([bf["w_out"], bf["ffn2_w_in"], bf["ffn2_w_out"]], finish2)}
    loss_part, grad_x, gr, parts = _local_step(x, loss_target, w, ex)
    loss = lax.psum(loss_part, ("x", "y", "c"))

    out = {n: _adam("adam_" + n, parts[n], wsh[n], msh[n], vsh[n]) for n in BIG}
    small_names = SMALL_SHARDED + SMALL_REPL
    full_shapes = [_grad_small(n, gr[n]).shape for n in small_names]
    summed = _unpack(_sum8("sum_small_grads", parts["small"]), full_shapes)
    mine = []
    for n, g in zip(small_names, summed):
        if n in SMALL_SHARDED:
            g = lax.dynamic_slice_in_dim(g, me * HEAD, HEAD, axis=g.ndim - 1)
        mine.append(g)
    shapes = [g.shape for g in mine]
    d_s, m_s, v_s = _adam_small("adam_small", _pack(mine), _pack([wsh[n] for n in small_names]),
                                _pack([msh[n] for n in small_names]), _pack([vsh[n] for n in small_names]))
    for n, g, dl, mn, vn in zip(small_names, mine, _unpack(d_s, shapes), _unpack(m_s, shapes), _unpack(v_s, shapes)):
        out[n] = (g, dl, mn, vn)

    res = [loss, grad_x]
    for k in range(4):
        res += [out[n][k].reshape((1,) + out[n][k].shape) for n in WEIGHTS]
    return tuple(res)
```

```python
import functools
import math

import jax
import jax.numpy as jnp
from jax import lax
from jax.experimental import pallas as pl
from jax.experimental.pallas import tpu as pltpu

F32 = jnp.float32
BF16 = jnp.bfloat16

N_DEV = 8
D_MODEL = 1024
RW = 512
N_HEADS = 8
HEAD = 64
CW = 512
CONV_K = 31
D_FF = 2816
FF_BLK = 704
GATE_LORA = 160
SHIFT_COLS = 1952
SHIFT_PAD = 2048
IN_COLS = 2976
IN_PAD = 3072
LN_EPS = 1e-5
GN_EPS = 64e-5
NORM_EPS = 1e-12
ALPHA = 2.0 ** 0.25
DECAY_SCALE = math.exp(-0.5)
CHUNK = 64
ADAM_LR, ADAM_B1, ADAM_B2, ADAM_EPS, ADAM_WD, ADAM_STEP = 0.001, 0.9, 0.999, 1e-8, 0.01, 10
VMEM_LIMIT = 56 * 1024 * 1024
MXU_DIM = 256

_DN = {"nn": (((1,), (0,)), ((), ())), "nt": (((1,), (1,)), ((), ())), "tn": (((0,), (0,)), ((), ()))}


def _params():
    return pltpu.CompilerParams(vmem_limit_bytes=VMEM_LIMIT)


def _dot(a, b, dims="nn"):
    return lax.dot_general(a, b, _DN[dims], preferred_element_type=F32)


def _split(x):
    hi = x.astype(BF16)
    return hi, (x - hi.astype(F32)).astype(BF16)


def _dot3_impl(a, b, dims):
    ah, al = _split(a)
    bh, bl = _split(b)
    ka, kb = _DN[dims][0][0][0], _DN[dims][0][1][0]
    return _dot(jnp.concatenate([ah, ah, al], axis=ka), jnp.concatenate([bh, bl, bh], axis=kb), dims)


@functools.partial(jax.custom_vjp, nondiff_argnums=(2,))
def _dot3(a, b, dims="nn"):
    return _dot3_impl(a, b, dims)


def _dot3_fwd(a, b, dims):
    return _dot3_impl(a, b, dims), (a, b)


def _dot2_ct(x, y, dims, ct_left):
    ka, kb = _DN[dims][0][0][0], _DN[dims][0][1][0]
    if ct_left:
        c, (h, l) = x.astype(BF16), _split(y)
        return _dot(jnp.concatenate([c, c], axis=ka), jnp.concatenate([h, l], axis=kb), dims)
    (h, l), c = _split(x), y.astype(BF16)
    return _dot(jnp.concatenate([h, l], axis=ka), jnp.concatenate([c, c], axis=kb), dims)


def _dot3_bwd(dims, res, ct):
    a, b = res
    if dims == "nn":
        return _dot2_ct(ct, b, "nt", True), _dot2_ct(a, ct, "tn", False)
    if dims == "nt":
        return _dot2_ct(ct, b, "nn", True), _dot2_ct(ct, a, "tn", True)
    return _dot2_ct(b, ct, "nt", False), _dot2_ct(a, ct, "nn", False)


_dot3.defvjp(_dot3_fwd, _dot3_bwd)


def _dot1_impl(a, b, dims):
    return _dot(a.astype(BF16), b.astype(BF16), dims)


@functools.partial(jax.custom_vjp, nondiff_argnums=(2,))
def _dot1(a, b, dims="nn"):
    return _dot1_impl(a, b, dims)


def _dot1_bwd(dims, res, ct):
    a, b = res
    if dims == "nn":
        return _dot1_impl(ct, b, "nt"), _dot1_impl(a, ct, "tn")
    if dims == "nt":
        return _dot1_impl(ct, b, "nn"), _dot1_impl(ct, a, "tn")
    return _dot1_impl(b, ct, "nt"), _dot1_impl(a, ct, "nn")


_dot1.defvjp(lambda a, b, dims: (_dot1_impl(a, b, dims), (a, b)), _dot1_bwd)


def _dot1_two(a, b, dims="nn"):
    ax_a, ax_b = {"nn": (0, 1), "nt": (0, 0), "tn": (1, 1)}[dims]
    shallow = dims != "tn" and 2 * a[0].shape[1] <= MXU_DIM
    out = []
    for i in range(0, len(a), 2):
        if shallow:
            z = jnp.zeros_like(b[i])
            bd = jnp.concatenate([jnp.concatenate([b[i], z], axis=1), jnp.concatenate([z, b[i + 1]], axis=1)], axis=0)
            r = _dot1(jnp.concatenate(a[i:i + 2], axis=1), bd, dims)
            n = r.shape[1] // 2
            out += [r[:, :n], r[:, n:]]
        else:
            r = _dot1(jnp.concatenate(a[i:i + 2], axis=ax_a), jnp.concatenate(b[i:i + 2], axis=ax_b), dims)
            m, n = r.shape[0] // 2, r.shape[1] // 2
            out += [r[:m, :n], r[m:, n:]]
    return out


def _tri_inv_impl(l, eye):
    steps = int(math.log2(CHUNK)) - 1
    m = l[0].shape[0]
    tm = [eye + x for x in l]
    lp = _dot1_two(l, l)
    for k in range(steps):
        if k < steps - 1:
            both = _dot1_two([jnp.concatenate([t, p], axis=0) for t, p in zip(tm, lp)], lp)
            tm = [t + b[:m] for t, b in zip(tm, both)]
            lp = [b[m:] for b in both]
        else:
            tm = [t + x for t, x in zip(tm, _dot1_two(tm, lp))]
    return tm


@jax.custom_vjp
def _tri_inv(l, eye):
    return _tri_inv_impl(l, eye)


def _tri_inv_fwd(l, eye):
    tm = _tri_inv_impl(l, eye)
    return tm, (tm, eye)


def _tri_inv_bwd(res, ct):
    tm, eye = res
    return _dot1_two(_dot1_two(tm, ct, "tn"), tm, "nt"), jnp.zeros_like(eye)


_tri_inv.defvjp(_tri_inv_fwd, _tri_inv_bwd)


@jax.custom_vjp
def _tri_inv_known(l, tm):
    return tm


_tri_inv_known.defvjp(lambda l, tm: (tm, tm),
                      lambda tm, ct: (_dot1_two(_dot1_two(tm, ct, "tn"), tm, "nt"), [jnp.zeros_like(t) for t in tm]))


def _ones_impl(x, g2):
    x1 = x.astype(BF16)
    x2 = (x - x1.astype(F32)).astype(BF16)
    return _dot(jnp.concatenate([x1, x2], axis=1), g2)


@jax.custom_vjp
def _head_sum(x, g2):
    return _ones_impl(x, g2)


_head_sum.defvjp(lambda x, g2: (_ones_impl(x, g2), g2), lambda g2, ct: (_ones_impl(ct, g2), jnp.zeros_like(g2)))


def _prefix_sum(x):
    row = lax.broadcasted_iota(jnp.int32, x.shape, 0)
    sh = 1
    while sh < x.shape[0]:
        x = x + jnp.where(row >= sh, pltpu.roll(x, sh, 0), 0.0)
        sh *= 2
    return x


def _dir_cumsum_impl(x, sgn):
    pre = _prefix_sum(x)
    return jnp.where(sgn > 0.0, pre, jnp.sum(x, axis=0, keepdims=True) - pre + x)


@jax.custom_vjp
def _dir_cumsum(x, sgn):
    return _dir_cumsum_impl(x, sgn)


_dir_cumsum.defvjp(lambda x, sgn: (_dir_cumsum_impl(x, sgn), sgn),
                   lambda sgn, ct: (_dir_cumsum_impl(ct, -sgn), jnp.zeros_like(sgn)))


def _sigmoid(x):
    return 1.0 / (1.0 + jnp.exp(-x))


def _mesh_pos():
    return lax.axis_index("x"), lax.axis_index("y"), lax.axis_index("c")


def _peer(pos, q):
    x, y, c = pos
    return (1 - x if q & 4 else x, 1 - y if q & 2 else y, 1 - c if q & 1 else c)


def _linear(pos):
    return 4 * pos[0] + 2 * pos[1] + pos[2]


def _exchange_copies(x_refs, o_refs, send_sems, recv_sems, local_sems, gather):
    pos = _mesh_pos()
    me = _linear(pos)
    starts, wait_recv, wait_send, wait_local = [], [], [], []
    for t in range(len(x_refs)):
        src = x_refs[t] if gather else x_refs[t].at[me]
        cp = pltpu.make_async_copy(src, o_refs[t].at[me], local_sems.at[t])
        starts.append(cp.start)
        wait_local.append(cp.wait)
    for q in range(1, N_DEV):
        peer = _peer(pos, q)
        for t in range(len(x_refs)):
            src = x_refs[t] if gather else x_refs[t].at[_linear(peer)]
            sems = dict(send_sem=send_sems.at[t, q - 1], recv_sem=recv_sems.at[t, q - 1],
                        device_id=peer, device_id_type=pl.DeviceIdType.MESH)
            send = pltpu.make_async_remote_copy(src_ref=src, dst_ref=o_refs[t].at[me], **sems)
            recv = pltpu.make_async_remote_copy(src_ref=src, dst_ref=o_refs[t].at[_linear(peer)], **sems)
            starts.append(send.start)
            wait_recv.append(recv.wait_recv)
            wait_send.append(send.wait_send)
    return starts, wait_recv + wait_send + wait_local


def _exchange_shapes(xs, gather):
    return [jax.ShapeDtypeStruct((N_DEV,) + (x.shape if gather else x.shape[1:]), x.dtype) for x in xs]


def _exchange_sems(nt):
    return [pltpu.SemaphoreType.DMA((nt, N_DEV - 1)), pltpu.SemaphoreType.DMA((nt, N_DEV - 1)),
            pltpu.SemaphoreType.DMA((nt,))]


def _gather_two_level(name, x):
    def body(x_ref, out_ref, send_sems, recv_sems, local_sem):
        px, py, pc = _mesh_pos()
        me, sibling = (px, py, pc), (px, py, 1 - pc)
        chips = [(1 - px, py), (px, 1 - py), (1 - px, 1 - py)]

        def slot(pos):
            return out_ref.at[_linear(pos)]

        def copy(k, block, to, src=None):
            return pltpu.make_async_remote_copy(
                src_ref=slot(block) if src is None else src, dst_ref=slot(block), send_sem=send_sems.at[k],
                recv_sem=recv_sems.at[k], device_id=to, device_id_type=pl.DeviceIdType.MESH)

        mine = pltpu.make_async_copy(x_ref, slot(me), local_sem)
        mine.start()
        first = [copy(0, me, sibling, src=x_ref)]
        first += [copy(1 + j, me, (*chip, pc), src=x_ref) for j, chip in enumerate(chips)]
        for cp in first:
            cp.start()
        passed = [copy(4 + j, (*chip, pc), sibling) for j, chip in enumerate(chips)]
        for j, chip in enumerate(chips):
            copy(1 + j, (*chip, pc), me).wait_recv()
            passed[j].start()
        copy(0, sibling, me).wait_recv()
        for j, chip in enumerate(chips):
            copy(4 + j, (*chip, 1 - pc), me).wait_recv()
        for cp in first + passed:
            cp.wait_send()
        mine.wait()

    any_spec = pl.BlockSpec(memory_space=pl.ANY)
    return pl.pallas_call(
        body, name=name, in_specs=[any_spec], out_specs=any_spec,
        out_shape=jax.ShapeDtypeStruct((N_DEV,) + x.shape, x.dtype),
        scratch_shapes=[pltpu.SemaphoreType.DMA((N_DEV - 1,)), pltpu.SemaphoreType.DMA((N_DEV - 1,)),
                        pltpu.SemaphoreType.DMA])(x)


def _call(name, body, grid, ins, in_specs, out_specs, out_shape, scratch=(), exch=None):
    if exch is None:
        return pl.pallas_call(body, name=name, grid=grid, in_specs=in_specs, out_specs=out_specs,
                              out_shape=out_shape, scratch_shapes=list(scratch), compiler_params=_params())(*ins)
    xs, gather = exch
    single = not isinstance(out_shape, (list, tuple))
    o_specs = [out_specs] if single else list(out_specs)
    o_shape = [out_shape] if single else list(out_shape)
    n_in, n_out, n_x, n_scr = len(ins), len(o_shape), len(xs), len(scratch)

    def wrapped(*refs):
        in_refs = refs[:n_in]
        x_refs = refs[n_in:n_in + n_x]
        out_refs = refs[n_in + n_x:n_in + n_x + n_out]
        got_refs = refs[n_in + n_x + n_out:n_in + 2 * n_x + n_out]
        rest = refs[n_in + 2 * n_x + n_out:]
        starts, waits = _exchange_copies(x_refs, got_refs, *rest[n_scr:], gather)
        ids = [pl.program_id(i) for i in range(len(grid))]
        first = functools.reduce(lambda p, q: p & q, [i == 0 for i in ids])
        last = functools.reduce(lambda p, q: p & q, [i == g - 1 for i, g in zip(ids, grid)])

        @pl.when(first)
        def _():
            for f in starts:
                f()

        body(*in_refs, *out_refs, *rest[:n_scr])

        @pl.when(last)
        def _():
            for f in waits:
                f()

    any_spec = pl.BlockSpec(memory_space=pl.ANY)
    outs = pl.pallas_call(
        wrapped, name=name, grid=grid, in_specs=list(in_specs) + [any_spec] * n_x,
        out_specs=o_specs + [any_spec] * n_x, out_shape=o_shape + _exchange_shapes(xs, gather),
        scratch_shapes=list(scratch) + _exchange_sems(n_x), compiler_params=_params())(*ins, *xs)
    res = outs[:n_out]
    return (res[0] if single else res), outs[n_out:]


def _mm_call(name, dims, grid, red_axis, ins, in_specs, out_shape, out_spec, acc_shape,
             scale=1.0, add_scale=None, exch=None):
    nred = grid[red_axis]

    def body(*refs):
        if add_scale is None:
            a_ref, b_ref, o_ref, acc_ref = refs
            add_ref = None
        else:
            a_ref, b_ref, add_ref, o_ref, acc_ref = refs
        k = pl.program_id(red_axis)

        @pl.when(k == 0)
        def _():
            acc_ref[...] = jnp.zeros_like(acc_ref)

        acc_ref[...] += _dot(a_ref[...].astype(BF16), b_ref[...].astype(BF16), dims)

        @pl.when(k == nred - 1)
        def _():
            r = acc_ref[...]
            if scale != 1.0:
                r = r * scale
            if add_ref is not None:
                r = r + add_scale * add_ref[...].astype(F32)
            o_ref[...] = r.astype(o_ref.dtype)

    return _call(name, body, grid, ins, in_specs, out_spec, out_shape, [pltpu.VMEM(acc_shape, F32)], exch)


def _tile(dim, cap):
    t = min(dim, cap)
    while dim % t or t % 128:
        t -= 128
        assert t > 0, (dim, cap)
    return t


def _mm(name, a, b, dims, out_dtype, scale=1.0, add=None, add_scale=None, tm=512, tn=1024, tk=1024):
    if dims == "tn":
        kd, m = a.shape
        n = b.shape[1]
    else:
        m, kd = a.shape
        n = b.shape[1] if dims == "nn" else b.shape[0]
    tm, tn, tk = _tile(m, tm), _tile(n, tn), _tile(kd, tk)
    a_spec = (pl.BlockSpec((tk, tm), lambda i, j, k: (k, i)) if dims == "tn"
              else pl.BlockSpec((tm, tk), lambda i, j, k: (i, k)))
    b_spec = (pl.BlockSpec((tn, tk), lambda i, j, k: (j, k)) if dims == "nt"
              else pl.BlockSpec((tk, tn), lambda i, j, k: (k, j)))
    o_spec = pl.BlockSpec((tm, tn), lambda i, j, k: (i, j))
    ins, specs = [a, b], [a_spec, b_spec]
    if add is not None:
        ins.append(add)
        specs.append(o_spec)
    return _mm_call(name, dims, (m // tm, n // tn, kd // tk), 2, ins, specs,
                    jax.ShapeDtypeStruct((m, n), out_dtype), o_spec, (tm, tn),
                    scale=scale, add_scale=add_scale if add is not None else None)


def _ffn_in(name, x, wg, tm=512, exch=None):
    n = x.shape[0]
    nj = N_DEV // 2

    def body(x_ref, wgate_ref, wup_ref, hg_ref, hu_ref, act_ref):
        xb = x_ref[...].astype(BF16)
        g = _dot(xb, wgate_ref[...])
        u = _dot(xb, wup_ref[...])
        hg_ref[...] = g.astype(BF16)
        hu_ref[...] = u.astype(BF16)
        act_ref[...] = (g * _sigmoid(g) * u).astype(BF16)

    blk = pl.BlockSpec((None, tm, FF_BLK), lambda j, i: (j, i, 0))
    shp = jax.ShapeDtypeStruct((nj, n, FF_BLK), BF16)
    return _call(name, body, (nj, n // tm), [x, wg, wg],
                 [pl.BlockSpec((tm, D_MODEL), lambda j, i: (i, 0)),
                  pl.BlockSpec((None, D_MODEL, FF_BLK), lambda j, i: (j, 0, 0)),
                  pl.BlockSpec((None, D_MODEL, FF_BLK), lambda j, i: (j + nj, 0, 0))],
                 [blk, blk, blk], [shp, shp, shp], exch=exch)


def _ffn_out_bwd(name, dz, wout, hg, hu, act, tm=512, exch=None):
    n = dz.shape[0]
    nj, ni = N_DEV // 2, n // tm

    def body(dz_ref, w_ref, hg_ref, hu_ref, act_ref, dhg_ref, dhu_ref, dw_ref, acc_ref):
        i = pl.program_id(1)
        dzb = dz_ref[...].astype(BF16)
        dact = 0.5 * _dot(dzb, w_ref[...], "nt")
        g = hg_ref[...].astype(F32)
        u = hu_ref[...].astype(F32)
        s = _sigmoid(g)
        dhg_ref[...] = (dact * u * (s * (1.0 + g * (1.0 - s)))).astype(BF16)
        dhu_ref[...] = (dact * (g * s)).astype(BF16)

        @pl.when(i == 0)
        def _():
            acc_ref[...] = jnp.zeros_like(acc_ref)

        acc_ref[...] += _dot(act_ref[...], dzb, "tn")

        @pl.when(i == ni - 1)
        def _():
            dw_ref[...] = (0.5 * acc_ref[...]).astype(dw_ref.dtype)

    blk = pl.BlockSpec((None, tm, FF_BLK), lambda j, i: (j, i, 0))
    wblk = pl.BlockSpec((FF_BLK, D_MODEL), lambda j, i: (j, 0))
    shp = jax.ShapeDtypeStruct((nj, n, FF_BLK), BF16)
    return _call(name, body, (nj, ni), [dz, wout, hg, hu, act],
                 [pl.BlockSpec((tm, D_MODEL), lambda j, i: (i, 0)), wblk, blk, blk, blk],
                 [blk, blk, wblk], [shp, shp, jax.ShapeDtypeStruct((D_FF, D_MODEL), BF16)],
                 [pltpu.VMEM((FF_BLK, D_MODEL), F32)], exch=exch)


def _mm_ln(name, a, b, xres, g, beta, c, tgt=None, a_blocked=False, tm=512, tk=512):
    if a_blocked:
        nj, n, kj = a.shape
        nk, kb = nj // 2, 2 * kj
        a_spec = pl.BlockSpec((2, tm, kj), lambda i, k: (k, i, 0))
    else:
        n, kd = a.shape
        kb = _tile(kd, tk)
        nk = kd // kb
        a_spec = pl.BlockSpec((tm, kb), lambda i, k: (i, k))
    d = b.shape[1]
    with_loss = tgt is not None

    def body(*refs):
        if with_loss:
            a_ref, b_ref, x_ref, g_ref, be_ref, t_ref, dz_ref, dzb_ref, l_ref, dg_ref, db_ref, acc_ref = refs
        else:
            a_ref, b_ref, x_ref, g_ref, be_ref, o_ref, ob_ref, z_ref, acc_ref = refs
        i, k = pl.program_id(0), pl.program_id(1)

        @pl.when(k == 0)
        def _():
            acc_ref[...] = jnp.zeros_like(acc_ref)

        if a_blocked:
            acc_ref[...] += (_dot(a_ref[0].astype(BF16), b_ref[0:kb // 2, :].astype(BF16))
                             + _dot(a_ref[1].astype(BF16), b_ref[kb // 2:kb, :].astype(BF16)))
        else:
            acc_ref[...] += _dot(a_ref[...].astype(BF16), b_ref[...].astype(BF16))

        @pl.when(k == nk - 1)
        def _():
            z = ALPHA * x_ref[...] + c * acc_ref[...]
            mu = jnp.mean(z, axis=-1, keepdims=True)
            zc = z - mu
            rstd = lax.rsqrt(jnp.mean(zc * zc, axis=-1, keepdims=True) + LN_EPS)
            xh = zc * rstd
            y = xh * g_ref[...] + be_ref[...]
            if with_loss:
                err = y - t_ref[...]
                ct = err * (1.0 / d)
                dxh = ct * g_ref[...]
                dz = rstd * (dxh - jnp.mean(dxh, axis=-1, keepdims=True)
                             - xh * jnp.mean(dxh * xh, axis=-1, keepdims=True))
                dz_ref[...] = dz
                dzb_ref[...] = dz.astype(BF16)
                part = 0.5 * jnp.sum(jnp.sum(err * err, axis=-1, keepdims=True), axis=0, keepdims=True) * (1.0 / d)

                @pl.when(i == 0)
                def _():
                    l_ref[...] = jnp.zeros_like(l_ref)
                    dg_ref[...] = jnp.zeros_like(dg_ref)
                    db_ref[...] = jnp.zeros_like(db_ref)

                l_ref[...] += jnp.broadcast_to(part, l_ref.shape)
                dg_ref[...] += jnp.sum(ct * xh, axis=0, keepdims=True)
                db_ref[...] += jnp.sum(ct, axis=0, keepdims=True)
            else:
                z_ref[...] = z
                o_ref[...] = y
                ob_ref[...] = y.astype(BF16)

    row = pl.BlockSpec((tm, d), lambda i, k: (i, 0))
    vec = pl.BlockSpec((1, d), lambda i, k: (0, 0))
    ins = [a, b, xres, g, beta]
    in_specs = [a_spec, pl.BlockSpec((kb, d), lambda i, k: (k, 0)), row, vec, vec]
    out_specs = [row, row]
    out_shape = [jax.ShapeDtypeStruct((n, d), F32), jax.ShapeDtypeStruct((n, d), BF16)]
    if with_loss:
        ins.append(tgt)
        in_specs.append(row)
        out_specs += [pl.BlockSpec((1, 128), lambda i, k: (0, 0)), vec, vec]
        out_shape += [jax.ShapeDtypeStruct((1, 128), F32)] + [jax.ShapeDtypeStruct((1, d), F32)] * 2
    else:
        out_specs.append(row)
        out_shape.append(jax.ShapeDtypeStruct((n, d), F32))
    return _call(name, body, (n // tm, nk), ins, in_specs, out_specs, out_shape, [pltpu.VMEM((tm, d), F32)])


def _rowwise(name, fn, rows, params, out_rows, out_accs, tm=256, exch=None):
    specs, ins = [], []
    for r in rows:
        arr, w, cb = r if isinstance(r, tuple) else (r, r.shape[1], 0)
        ins.append(arr)
        specs.append(pl.BlockSpec((tm, w), functools.partial(lambda i, cb: (i, cb), cb=cb)))
    n = ins[0].shape[0]
    for p in params:
        ins.append(p)
        specs.append(pl.BlockSpec(p.shape, lambda i: (0, 0)))
    n_in, n_or = len(ins), len(out_rows)

    def body(*refs):
        outs = fn(*[r[...] for r in refs[:n_in]])
        o_refs = refs[n_in:]
        for o_ref, o in zip(o_refs[:n_or], outs[:n_or]):
            o_ref[...] = o.astype(o_ref.dtype)
        if out_accs:
            @pl.when(pl.program_id(0) == 0)
            def _():
                for a_ref in o_refs[n_or:]:
                    a_ref[...] = jnp.zeros_like(a_ref)

            for a_ref, a in zip(o_refs[n_or:], outs[n_or:]):
                a_ref[...] += a.astype(F32)

    out_specs = [pl.BlockSpec((tm, w), lambda i: (i, 0)) for w, _ in out_rows]
    out_specs += [pl.BlockSpec(s, lambda i: (0, 0)) for s in out_accs]
    out_shape = [jax.ShapeDtypeStruct((n, w), dt) for w, dt in out_rows]
    out_shape += [jax.ShapeDtypeStruct(s, F32) for s in out_accs]
    return _call(name, body, (n // tm,), ins, specs, out_specs, out_shape, exch=exch)


def _vjp_of(fn, n_in):
    def g(*args):
        ins, cts = args[:n_in], args[n_in:]
        outs, pull = jax.vjp(fn, *ins)
        return pull(tuple(c.astype(o.dtype) for c, o in zip(cts, outs)))
    return g


def _ln_bwd(name, z, g, ct):
    def fn(zt, ct_, gt):
        mu = jnp.mean(zt, axis=-1, keepdims=True)
        zc = zt - mu
        rstd = lax.rsqrt(jnp.mean(zc * zc, axis=-1, keepdims=True) + LN_EPS)
        xh = zc * rstd
        dxh = ct_ * gt
        dz = rstd * (dxh - jnp.mean(dxh, axis=-1, keepdims=True)
                     - xh * jnp.mean(dxh * xh, axis=-1, keepdims=True))
        return dz, dz, jnp.sum(ct_ * xh, axis=0, keepdims=True), jnp.sum(ct_, axis=0, keepdims=True)

    d = z.shape[1]
    return _rowwise(name, fn, [z, ct], [g], [(d, F32), (d, BF16)], [(1, d), (1, d)])


SHIFT_TILE = 256


def _shift_specs(t):
    r8 = SHIFT_TILE // 8
    return [pl.BlockSpec((None, SHIFT_TILE, SHIFT_PAD), lambda b, i: (b, i, 0)),
            pl.BlockSpec((None, 8, SHIFT_PAD), lambda b, i: (b, jnp.maximum(i * r8 - 1, 0), 0)),
            pl.BlockSpec((None, 8, SHIFT_PAD), lambda b, i: (b, jnp.minimum((i + 1) * r8, t // 8 - 1), 0))]


def _neighbour_diffs(cur, prev_ref, next_ref, i, nt):
    prow = jnp.where(i > 0, prev_ref[7:8, :], 0.0)
    nrow = jnp.where(i < nt - 1, next_ref[0:1, :], 0.0)
    rid = lax.broadcasted_iota(jnp.int32, cur.shape, 0)
    return (jnp.where(rid == 0, prow, pltpu.roll(cur, 1, 0)) - cur,
            jnp.where(rid == SHIFT_TILE - 1, nrow, pltpu.roll(cur, SHIFT_TILE - 1, 0)) - cur)


def _shift(name, src, mu_prev, mu_next):
    bsz, t, _ = src.shape
    nt, w = t // SHIFT_TILE, SHIFT_PAD

    def body(cur_ref, prev_ref, next_ref, mp_ref, mn_ref, o_ref):
        cur = cur_ref[...]
        dprev, dnext = _neighbour_diffs(cur, prev_ref, next_ref, pl.program_id(1), nt)
        o_ref[...] = cur + mp_ref[...] * dprev + mn_ref[...] * dnext

    specs = _shift_specs(t)
    vec = pl.BlockSpec((1, w), lambda b, i: (0, 0))
    return _call(name, body, (bsz, nt), [src, src, src, mu_prev, mu_next], specs + [vec, vec], specs[0],
                 jax.ShapeDtypeStruct((bsz, t, w), F32))


def _shift_bwd(name, dps, p, mu_prev, mu_next):
    bsz, t, _ = dps.shape
    nt, w = t // SHIFT_TILE, SHIFT_PAD

    def body(d_ref, dprev_ref, dnext_ref, p_ref, pprev_ref, pnext_ref, mp_ref, mn_ref, o_ref, da_ref, db_ref):
        b, i = pl.program_id(0), pl.program_id(1)
        dcur = d_ref[...]
        d_dprev, d_dnext = _neighbour_diffs(dcur, dprev_ref, dnext_ref, i, nt)
        o_ref[...] = (dcur + mn_ref[...] * d_dprev + mp_ref[...] * d_dnext).astype(o_ref.dtype)
        p_dprev, p_dnext = _neighbour_diffs(p_ref[...], pprev_ref, pnext_ref, i, nt)

        @pl.when((b == 0) & (i == 0))
        def _():
            da_ref[...] = jnp.zeros_like(da_ref)
            db_ref[...] = jnp.zeros_like(db_ref)

        da_ref[...] += jnp.sum(dcur * p_dprev, axis=0, keepdims=True)
        db_ref[...] += jnp.sum(dcur * p_dnext, axis=0, keepdims=True)

    specs = _shift_specs(t)
    vec = pl.BlockSpec((1, w), lambda b, i: (0, 0))
    return _call(name, body, (bsz, nt), [dps, dps, dps, p, p, p, mu_prev, mu_next], specs + specs + [vec, vec],
                 [specs[0], vec, vec],
                 [jax.ShapeDtypeStruct((bsz, t, w), BF16), jax.ShapeDtypeStruct((1, w), F32),
                  jax.ShapeDtypeStruct((1, w), F32)])


CONV_BLK = 128


def _halo_specs(t, tt, w):
    r16 = tt // 16
    return [pl.BlockSpec((None, tt, w), lambda b, i: (b, i, 0)),
            pl.BlockSpec((None, 16, w), lambda b, i: (b, jnp.maximum(i * r16 - 1, 0), 0)),
            pl.BlockSpec((None, 16, w), lambda b, i: (b, jnp.minimum((i + 1) * r16, t // 16 - 1), 0))]


def _fill_pad(pad_ref, cur_ref, prev_ref, next_ref, i, nt, tt):
    pad_ref[0:16, :] = jnp.where(i > 0, prev_ref[...], 0.0)
    pad_ref[16:16 + tt, :] = cur_ref[...]
    pad_ref[16 + tt:32 + tt, :] = jnp.where(i < nt - 1, next_ref[...], 0.0)


def _dwconv(name, u, dw32, bias, flip, tt=512):
    bsz, t, w = u.shape
    tt = min(tt, t)
    nt = t // tt

    def body(cur_ref, prev_ref, next_ref, dw_ref, b_ref, o_ref, pad_ref):
        i = pl.program_id(1)
        _fill_pad(pad_ref, cur_ref, prev_ref, next_ref, i, nt, tt)
        for r0 in range(0, tt, CONV_BLK):
            for cs in (slice(c0, c0 + CONV_BLK) for c0 in range(0, w, CONV_BLK)):
                acc = jnp.broadcast_to(b_ref[:, cs], (CONV_BLK, CONV_BLK))
                for k in range(CONV_K):
                    kk = CONV_K - 1 - k if flip else k
                    acc = acc + pad_ref[pl.ds(r0 + 1 + k, CONV_BLK), cs] * dw_ref[kk:kk + 1, cs]
                o_ref[r0:r0 + CONV_BLK, cs] = acc

    return _call(name, body, (bsz, nt), [u, u, u, dw32, bias],
                 _halo_specs(t, tt, w) + [pl.BlockSpec((32, w), lambda b, i: (0, 0)),
                                          pl.BlockSpec((1, w), lambda b, i: (0, 0))],
                 pl.BlockSpec((None, tt, w), lambda b, i: (b, i, 0)), jax.ShapeDtypeStruct((bsz, t, w), F32),
                 [pltpu.VMEM((tt + 32, w), F32)])


def _dwconv_dw(name, u, dc, tt=512):
    bsz, t, w = u.shape
    tt = min(tt, t)
    nt = t // tt

    def body(cur_ref, prev_ref, next_ref, dc_ref, ddw_ref, db_ref, pad_ref):
        b, i = pl.program_id(0), pl.program_id(1)
        _fill_pad(pad_ref, cur_ref, prev_ref, next_ref, i, nt, tt)

        @pl.when((b == 0) & (i == 0))
        def _():
            ddw_ref[...] = jnp.zeros_like(ddw_ref)
            db_ref[...] = jnp.zeros_like(db_ref)

        dcv = dc_ref[...]
        db_ref[...] += jnp.sum(dcv, axis=0, keepdims=True)
        for k in range(CONV_K):
            ddw_ref[k:k + 1, :] += jnp.sum(dcv * pad_ref[pl.ds(1 + k, tt), :], axis=0, keepdims=True)

    return _call(name, body, (bsz, nt), [u, u, u, dc],
                 _halo_specs(t, tt, w) + [pl.BlockSpec((None, tt, w), lambda b, i: (b, i, 0))],
                 [pl.BlockSpec((32, w), lambda b, i: (0, 0)), pl.BlockSpec((1, w), lambda b, i: (0, 0))],
                 [jax.ShapeDtypeStruct((32, w), F32), jax.ShapeDtypeStruct((1, w), F32)],
                 [pltpu.VMEM((tt + 32, w), F32)])


PAIR = 2 * HEAD
N_PAIRS = RW // PAIR


def _chunk_pairs(s, r, lw, k, v, kk, a, sgn, tm_known=None):
    n, m = CHUNK, 2 * CHUNK
    in_a = lax.broadcasted_iota(jnp.int32, (n, PAIR), 1) < HEAD

    def stack2(z):
        return jnp.concatenate([jnp.where(in_a, z, 0.0), jnp.where(in_a, 0.0, z)], axis=0)

    def each(f, *lists):
        return [f(*z) for z in zip(*lists)]

    sgn_f = sgn.astype(F32)
    row2 = lax.broadcasted_iota(jnp.int32, (m, m), 0)
    col2 = lax.broadcasted_iota(jnp.int32, (m, m), 1)
    same = (row2 >= n) == (col2 >= n)
    dlt = ((row2 & (n - 1)) - (col2 & (n - 1))) * sgn
    incl, strict = same & (dlt >= 0), same & (dlt > 0)
    eye = jnp.where(row2 == col2, 1.0, 0.0)

    cum = each(lambda lw_: _dir_cumsum(lw_, sgn_f), lw)
    tot = each(lambda lw_: jnp.sum(lw_, axis=0, keepdims=True), lw)
    e_neg = each(lambda c_: jnp.exp(-c_), cum)
    e_rest = each(lambda t_, c_: jnp.exp(t_ - c_), tot, cum)
    beta = each(lambda kk_, a_: kk_ * a_, kk, a)
    lhs = each(lambda kk_, c_, lw_, r_: jnp.concatenate(
        [stack2(-kk_ * jnp.exp(c_ - lw_)), stack2(r_ * jnp.exp(c_))], axis=0), kk, cum, lw, r)
    rhs = each(lambda b_, k_, e_: jnp.concatenate([stack2(b_ * e_), stack2(k_ * e_)], axis=0), beta, k, e_neg)
    sc = each(lambda l_, r_: _dot3(l_, r_, "nt"), lhs, rhs)
    l_ab = each(lambda sc_: jnp.where(strict, sc_[0:m, 0:m], 0.0), sc)
    l_ak = each(lambda sc_: jnp.where(strict, sc_[0:m, m:2 * m], 0.0), sc)
    m_r = each(lambda sc_: jnp.where(jnp.concatenate([incl, incl], axis=1), sc_[m:2 * m, :], 0.0), sc)
    tm = _tri_inv(l_ab, eye) if tm_known is None else _tri_inv_known(l_ab, tm_known)
    z = _dot1_two(lhs, s, "nt")
    v2 = each(stack2, v)
    u2 = _dot1_two(tm, each(lambda z_, lv_: z_[0:m] + lv_, z, _dot1_two(l_ak, v2)))
    uv = each(lambda u_, v_: jnp.concatenate([u_, v_], axis=0), u2, v2)
    y2 = each(lambda z_, mu_: z_[m:2 * m] + mu_, z, _dot1_two(m_r, uv))
    bk = each(lambda b_, k_, e_: jnp.concatenate([stack2(b_ * e_), stack2(k_ * e_)], axis=0), beta, k, e_rest)
    s_new = each(lambda s_, t_, d_: s_ * jnp.exp(t_) + d_, s, tot, _dot1_two(uv, bk, "tn"))
    return each(lambda y_: y_[0:n] + y_[n:m], y2), s_new, tm


SCAN_SEQS = 4
N_CHAINS = SCAN_SEQS * N_PAIRS


def _pair_tiles(ref):
    return [ref[q, :, p * PAIR:(p + 1) * PAIR] for q in range(SCAN_SEQS) for p in range(N_PAIRS)]


def _store_tiles(ref, tiles):
    for q in range(SCAN_SEQS):
        for p in range(N_PAIRS):
            ref[q, :, p * PAIR:(p + 1) * PAIR] = tiles[q * N_PAIRS + p]


def _scan_specs(order):
    shared = pl.BlockSpec((SCAN_SEQS, CHUNK, RW), lambda d, b, c: (b, order(d, c), 0))
    per_dir = pl.BlockSpec((SCAN_SEQS, CHUNK, RW), lambda d, b, c: (b, order(d, c), d))
    state = pl.BlockSpec((None, SCAN_SEQS, None, N_PAIRS, PAIR, PAIR), lambda d, b, c: (d, b, order(d, c), 0, 0, 0))
    return shared, per_dir, state


def _scan_fwd(r, v, kk, lw, kd, a, bsz, exch=None):
    n = r.shape[0]
    t = n // bsz
    nc = t // CHUNK

    def order(d, c):
        return c + d * (nc - 1 - 2 * c)

    def body(r_ref, v_ref, kk_ref, lw_ref, kd_ref, a_ref, y_ref, s0_ref, tm_ref, s_ref):
        d, c = pl.program_id(0), pl.program_id(2)

        @pl.when(c == 0)
        def _():
            s_ref[...] = jnp.zeros_like(s_ref)

        s = [s_ref[i] for i in range(N_CHAINS)]
        y, s_new, tm = _chunk_pairs(s, *[_pair_tiles(ref) for ref in (r_ref, lw_ref, kd_ref, v_ref, kk_ref, a_ref)],
                                    1 - 2 * d)
        _store_tiles(y_ref, y)
        for i in range(N_CHAINS):
            s0_ref[i // N_PAIRS, i % N_PAIRS] = s[i]
            tm_ref[i // N_PAIRS, i % N_PAIRS] = tm[i].astype(BF16)
            s_ref[i] = s_new[i]

    shared, per_dir, state = _scan_specs(order)
    seq = lambda z: z.reshape(bsz, t, z.shape[1])
    res = _call("scan_fwd", body, (2, bsz // SCAN_SEQS, nc), [seq(z) for z in (r, v, kk, lw, kd, a)],
                [shared, shared, shared, per_dir, per_dir, per_dir], [per_dir, state, state],
                [jax.ShapeDtypeStruct((bsz, t, 2 * RW), F32),
                 jax.ShapeDtypeStruct((2, bsz, nc, N_PAIRS, PAIR, PAIR), F32),
                 jax.ShapeDtypeStruct((2, bsz, nc, N_PAIRS, PAIR, PAIR), BF16)],
                [pltpu.VMEM((N_CHAINS, PAIR, PAIR), F32)], exch)
    (y, s0, tm), got = res if exch else (res, None)
    y = y.reshape(n, 2 * RW)
    return ([y, s0, tm], got) if exch else [y, s0, tm]


def _scan_bwd(r, v, kk, lw, kd, a, s0, tm, dy, bsz, exch=None):
    n = r.shape[0]
    t = n // bsz
    nc = t // CHUNK

    def order(d, c):
        cc = nc - 1 - c
        return cc + d * (nc - 1 - 2 * cc)

    def body(r_ref, v_ref, kk_ref, lw_ref, kd_ref, a_ref, dy_ref, s0_ref, tm_ref,
             dr_ref, dv_ref, dkk_ref, dlw_ref, dkd_ref, da_ref, ds_ref):
        d, c = pl.program_id(0), pl.program_id(2)

        @pl.when(c == 0)
        def _():
            ds_ref[...] = jnp.zeros_like(ds_ref)

        sgn = 1 - 2 * d
        tm_known = [tm_ref[i // N_PAIRS, i % N_PAIRS].astype(F32) for i in range(N_CHAINS)]
        _, pull = jax.vjp(lambda *ops: _chunk_pairs(*ops, sgn, tm_known)[:2],
                          [s0_ref[i // N_PAIRS, i % N_PAIRS] for i in range(N_CHAINS)],
                          *[_pair_tiles(ref) for ref in (r_ref, lw_ref, kd_ref, v_ref, kk_ref, a_ref)])
        grads = pull((_pair_tiles(dy_ref), [ds_ref[i] for i in range(N_CHAINS)]))
        for i in range(N_CHAINS):
            ds_ref[i] = grads[0][i]
        for o_ref, gx in zip((dr_ref, dlw_ref, dkd_ref, dv_ref, dkk_ref, da_ref), grads[1:]):
            _store_tiles(o_ref, gx)

    shared, per_dir, state = _scan_specs(order)
    shp = jax.ShapeDtypeStruct((bsz, t, 2 * RW), F32)
    seq = lambda z: z.reshape(bsz, t, z.shape[1])
    res = _call("scan_bwd", body, (2, bsz // SCAN_SEQS, nc), [seq(z) for z in (r, v, kk, lw, kd, a, dy)] + [s0, tm],
                [shared, shared, shared, per_dir, per_dir, per_dir, per_dir, state, state],
                [per_dir] * 6, [shp] * 6, [pltpu.VMEM((N_CHAINS, PAIR, PAIR), F32)], exch)
    outs, got = res if exch else (res, None)
    outs = [z.reshape(n, 2 * RW) for z in outs]
    return (outs, got) if exch else outs


def _prep_fn(ps, w0, w2bd, a0, a2bd, g2p, k_k, k_a, hsum):
    head_sum = lambda z: _head_sum(z, hsum)
    r, k, v = ps[:, 0:RW], ps[:, RW:2 * RW], ps[:, 2 * RW:3 * RW]
    wd, ad, gd = ps[:, 1536:1664], ps[:, 1664:1792], ps[:, 1792:2048]
    logw = -DECAY_SCALE * _sigmoid(_dot1(jnp.tanh(wd), w2bd) + w0)
    a = _sigmoid(_dot1(ad, a2bd) + a0)
    g = _dot1(_sigmoid(gd), g2p)
    kkr = k * k_k
    kk = kkr / jnp.maximum(jnp.sqrt(head_sum(kkr * kkr)), NORM_EPS)
    k2 = jnp.concatenate([k, k], axis=1)
    ka2 = jnp.concatenate([k_a, k_a], axis=1)
    kd = k2 * (1.0 + (a - 1.0) * ka2)
    return r, v, kk, logw, a, kd, g


def _post_fn(y2, r, v, kd, g, lnx_g, lnx_b, r_k, hsum):
    head_sum = lambda z: _head_sum(z, hsum)
    y = y2[:, 0:RW] + y2[:, RW:2 * RW]
    mu = head_sum(y) * (1.0 / HEAD)
    yc = y - mu
    var = head_sum(yc * yc) * (1.0 / HEAD)
    yn = yc * lax.rsqrt(var + GN_EPS) * lnx_g + lnx_b
    bonus = head_sum(r * (kd[:, 0:RW] + kd[:, RW:2 * RW]) * r_k) * v
    return ((yn + bonus) * g,)


def _glu_fn(pa, pb):
    return (pa * _sigmoid(pb),)


def _conv_out_fn(cv, ln_g, ln_b):
    mu = jnp.mean(cv, axis=-1, keepdims=True)
    cc = cv - mu
    var = jnp.mean(cc * cc, axis=-1, keepdims=True)
    y = cc * lax.rsqrt(var + LN_EPS) * ln_g + ln_b
    return (y * _sigmoid(y),)


def _local_step(x, tgt, w, ex=None):
    bsz, t, d = x.shape
    n = bsz * t
    x2d, tgt2d = x.reshape(n, d), tgt.reshape(n, d)
    hsum = jnp.tile(jnp.kron(jnp.eye(N_HEADS, dtype=BF16), jnp.ones((HEAD, HEAD), BF16)), (2, 1))
    w = dict(w)
    parts = {} if ex else None

    def hosted(result, finish=None):
        if not ex:
            return result
        outs, got = result
        if finish is not None:
            w.update(finish(got))
        return outs

    xb = x2d.astype(BF16)
    hg1, hu1, act1 = hosted(_ffn_in("ffn1_in", xb, w["ffn1_w_in"], exch=(ex["g1"][0], True) if ex else None),
                            ex["g1"][1] if ex else None)
    x1, x1b, z1 = _mm_ln("ffn1_out_ln1", act1, w["ffn1_w_out"], x2d, w["ln1_g"], w["ln1_b"], 0.5, a_blocked=True)
    p = _mm("w_in_proj", x1b, w["w_in"], "nn", F32)
    p3 = p.reshape(bsz, t, IN_PAD)
    ps = _shift("shift_fwd", p3, w["mu_prev"], w["mu_next"]).reshape(n, SHIFT_PAD)
    prep_params = [w["w0"], w["w2"], w["a0"], w["a2"], w["g2"], w["k_k"], w["k_a"], hsum]
    r, v, kk, logw, a, kd, g = _rowwise(
        "rwkv_prep", _prep_fn, [ps], prep_params,
        [(RW, F32), (RW, F32), (RW, F32), (2 * RW, F32), (2 * RW, F32), (2 * RW, F32), (RW, F32)], [])
    y2, s0, tm = hosted(_scan_fwd(r, v, kk, logw, kd, a, bsz, exch=(ex["g2"][0], True) if ex else None),
                    ex["g2"][1] if ex else None)
    post_params = [w["lnx_g"], w["lnx_b"], w["r_k"], hsum]
    (y_rwkv,) = _rowwise("rwkv_post", _post_fn, [y2, r, v, kd, g], post_params, [(RW, BF16)], [])
    (u,) = _rowwise("conv_glu", _glu_fn, [(p, CW, 4), (p, CW, 5)], [], [(CW, F32)], [])
    cv = _dwconv("conv_dw", u.reshape(bsz, t, CW), w["conv_dw"], w["conv_b"], False).reshape(n, CW)
    (y_conv,) = _rowwise("conv_out", _conv_out_fn, [cv], [w["conv_ln_g"], w["conv_ln_b"]], [(CW, BF16)], [])
    ycat = jnp.concatenate([y_rwkv, y_conv], axis=1)
    x2, x2b, z2 = _mm_ln("w_out_ln2", ycat, w["w_out"], x1, w["ln2_g"], w["ln2_b"], 1.0)
    hg2, hu2, act2 = _ffn_in("ffn2_in", x2b, w["ffn2_w_in"])
    gr = {}
    dz3, dz3b, loss, gr["ln3_g"], gr["ln3_b"] = _mm_ln(
        "ffn2_out_ln3", act2, w["ffn2_w_out"], x2, w["ln3_g"], w["ln3_b"], 0.5, tgt=tgt2d, a_blocked=True)

    dx2, _ = _ffn_bwd("ffn2", gr, dz3, dz3b, x2b, w["ffn2_w_in"], w["ffn2_w_out"], hg2, hu2, act2)
    dz2, dz2b, gr["ln2_g"], gr["ln2_b"] = _ln_bwd("ln2_bwd", z2, w["ln2_g"], dx2)
    gr["w_out"] = _mm("w_out_wgrad", ycat, dz2b, "tn", BF16, tk=512)
    dycat = _mm("w_out_dgrad", dz2b, w["w_out"], "nt", F32)
    conv_out_bwd = _vjp_of(_conv_out_fn, 3)
    dcv, gr["conv_ln_g"], gr["conv_ln_b"] = _rowwise(
        "conv_out_bwd", lambda cv_, ct_, g_, b_: conv_out_bwd(cv_, g_, b_, ct_),
        [cv, (dycat, CW, 1)], [w["conv_ln_g"], w["conv_ln_b"]], [(CW, F32)], [(1, CW), (1, CW)])
    dcv3 = dcv.reshape(bsz, t, CW)
    gr["conv_dw"], gr["conv_b"] = _dwconv_dw("conv_dw_wgrad", u.reshape(bsz, t, CW), dcv3)
    du = _dwconv("conv_dw_dgrad", dcv3, w["conv_dw"], jnp.zeros((1, CW), F32), True).reshape(n, CW)

    def glu_bwd(pa, pb, ct):
        return (jnp.concatenate(_vjp_of(_glu_fn, 2)(pa, pb, ct), axis=1),)

    (dp_conv,) = _rowwise("conv_glu_bwd", glu_bwd, [(p, CW, 4), (p, CW, 5), du], [], [(2 * CW, BF16)], [])

    def post_bwd(y2_, r_, v_, kd_, g_, ct, lg, lb, rk, hs):
        return _vjp_of(lambda *z: _post_fn(*z, hs), 8)(y2_, r_, v_, kd_, g_, lg, lb, rk, ct)

    dy2, dr_post, dv_post, dkd_post, dg, gr["lnx_g"], gr["lnx_b"], gr["r_k"] = _rowwise(
        "rwkv_post_bwd", post_bwd, [y2, r, v, kd, g, (dycat, RW, 0)], post_params,
        [(2 * RW, F32), (RW, F32), (RW, F32), (2 * RW, F32), (RW, F32)], [(1, RW), (1, RW), (1, RW)])
    sends = [jnp.concatenate(gr["ffn2_w_in"], axis=0), gr["ffn2_w_out"].reshape(N_DEV, D_FF // N_DEV, D_MODEL),
             gr["w_out"].reshape(N_DEV, D_MODEL // N_DEV, D_MODEL)]
    res = _scan_bwd(r, v, kk, logw, kd, a, s0, tm, dy2, bsz, exch=(sends, False) if ex else None)
    if ex:
        res, got = res
        parts.update(zip(("ffn2_w_in", "ffn2_w_out", "w_out"), got))
    dr_s, dv_s, dkk_s, dlw, dkd_s, da = res

    def prep_bwd(ps_, dr2, dr1, dv2, dv1, dkk2, dlw_, da_, dkd2, dkd1, dg_, *prm):
        half = lambda z: z[:, 0:RW] + z[:, RW:2 * RW]
        return _vjp_of(lambda *z: _prep_fn(*z, prm[-1]), 8)(
            ps_, *prm[:-1], half(dr2) + dr1, half(dv2) + dv1, half(dkk2), dlw_, da_, dkd2 + dkd1, dg_)

    dps, gr["w0"], gr["w2"], gr["a0"], gr["a2"], gr["g2"], gr["k_k"], gr["k_a"] = _rowwise(
        "rwkv_prep_bwd", prep_bwd,
        [ps, dr_s, dr_post, dv_s, dv_post, dkk_s, dlw, da, dkd_s, dkd_post, dg], prep_params,
        [(SHIFT_PAD, F32)], [q.shape for q in prep_params[:-1]])
    dps3 = dps.reshape(bsz, t, SHIFT_PAD)
    dp_shift, gr["mu_prev"], gr["mu_next"] = _shift_bwd("shift_bwd", dps3, p3, w["mu_prev"], w["mu_next"])
    dp_shift = dp_shift.reshape(n, SHIFT_PAD)
    dp = jnp.concatenate([dp_shift, dp_conv], axis=1)
    gr["w_in"] = _mm("w_in_wgrad", x1b, dp, "tn", BF16, tk=512)
    dx1 = _mm("w_in_dgrad", dp, w["w_in"], "nt", F32, add=dz2, add_scale=ALPHA)
    dz1, dz1b, gr["ln1_g"], gr["ln1_b"] = _ln_bwd("ln1_bwd", z1, w["ln1_g"], dx1)
    riders = None
    if ex:
        gw_in = _unpad_in_cols(gr["w_in"]).reshape(D_MODEL, N_DEV, IN_COLS // N_DEV).transpose(1, 0, 2)
        small = _pack([_grad_small(nm, gr[nm]) for nm in SMALL_SHARDED + SMALL_REPL])
        riders = {"out_bwd": [gw_in, jnp.broadcast_to(small[None], (N_DEV,) + small.shape)], "in_wgrad": []}
    grad_x, got = _ffn_bwd("ffn1", gr, dz1, dz1b, xb, w["ffn1_w_in"], w["ffn1_w_out"], hg1, hu1, act1, riders)
    if ex:
        parts.update(w_in=got["out_bwd"][0], small=got["out_bwd"][1], ffn1_w_out=got["in_wgrad"][0],
                     ffn1_w_in=got["in_dgrad"][0])
    return loss[0, 0], grad_x.reshape(bsz, t, d), gr, parts


def _ffn_in_wgrad(name, xin, dhg, dhu, tk=1024, exch=None):
    n = xin.shape[0]
    nj, nt = N_DEV // 2, n // tk

    def body(x_ref, g_ref, u_ref, og_ref, ou_ref, accg_ref, accu_ref):
        i = pl.program_id(1)

        @pl.when(i == 0)
        def _():
            accg_ref[...] = jnp.zeros_like(accg_ref)
            accu_ref[...] = jnp.zeros_like(accu_ref)

        xt = x_ref[...].astype(BF16).T
        accg_ref[...] += _dot(xt, g_ref[...])
        accu_ref[...] += _dot(xt, u_ref[...])

        @pl.when(i == nt - 1)
        def _():
            og_ref[...] = accg_ref[...].astype(og_ref.dtype)
            ou_ref[...] = accu_ref[...].astype(ou_ref.dtype)

    dh_blk = pl.BlockSpec((None, tk, FF_BLK), lambda j, i: (j, i, 0))
    o_blk = pl.BlockSpec((None, D_MODEL, FF_BLK), lambda j, i: (j, 0, 0))
    shp = jax.ShapeDtypeStruct((nj, D_MODEL, FF_BLK), BF16)
    return _call(name, body, (nj, nt), [xin, dhg, dhu],
                 [pl.BlockSpec((tk, D_MODEL), lambda j, i: (i, 0)), dh_blk, dh_blk], [o_blk, o_blk], [shp, shp],
                 [pltpu.VMEM((D_MODEL, FF_BLK), F32)] * 2, exch)


def _ffn_in_dgrad(name, dhg, dhu, wg, add, add_scale, tm=512, exch=None):
    n = add.shape[0]
    nj = N_DEV // 2

    def body(g_ref, u_ref, wgate_ref, wup_ref, add_ref, o_ref, acc_ref):
        j = pl.program_id(1)

        @pl.when(j == 0)
        def _():
            acc_ref[...] = jnp.zeros_like(acc_ref)

        acc_ref[...] += _dot(g_ref[...], wgate_ref[...], "nt") + _dot(u_ref[...], wup_ref[...], "nt")

        @pl.when(j == nj - 1)
        def _():
            o_ref[...] = acc_ref[...] + add_scale * add_ref[...]

    dh_blk = pl.BlockSpec((None, tm, FF_BLK), lambda i, j: (j, i, 0))
    row = pl.BlockSpec((tm, D_MODEL), lambda i, j: (i, 0))
    return _call(name, body, (n // tm, nj), [dhg, dhu, wg, wg, add],
                 [dh_blk, dh_blk, pl.BlockSpec((None, D_MODEL, FF_BLK), lambda i, j: (j, 0, 0)),
                  pl.BlockSpec((None, D_MODEL, FF_BLK), lambda i, j: (j + nj, 0, 0)), row],
                 row, jax.ShapeDtypeStruct((n, D_MODEL), F32), [pltpu.VMEM((tm, D_MODEL), F32)], exch)


def _ffn_bwd(tag, gr, dz, dzb, xin, wg, wout, hg, hu, act, riders=None):
    own = riders is not None

    def hosted(result):
        return result if own else (result, None)

    (dhg, dhu, gwo), got_a = hosted(_ffn_out_bwd(
        tag + "_out_bwd", dzb, wout, hg, hu, act, exch=(riders["out_bwd"], False) if own else None))
    gr[tag + "_w_out"] = gwo
    send = ([gwo.reshape(N_DEV, D_FF // N_DEV, D_MODEL)] + riders["in_wgrad"], False) if own else None
    dw, got_b = hosted(_ffn_in_wgrad(tag + "_in_wgrad", xin, dhg, dhu, exch=send))
    gr[tag + "_w_in"] = dw
    send = ([jnp.concatenate(dw, axis=0)], False) if own else None
    dx, got_c = hosted(_ffn_in_dgrad(tag + "_in_dgrad", dhg, dhu, wg, dz, ALPHA, exch=send))
    return dx, {"out_bwd": got_a, "in_wgrad": got_b, "in_dgrad": got_c}


def _adam_math(g, w, m, v):
    m = ADAM_B1 * m + (1.0 - ADAM_B1) * g
    v = ADAM_B2 * v + (1.0 - ADAM_B2) * (g * g)
    m_hat = m / (1.0 - ADAM_B1 ** ADAM_STEP)
    v_hat = v / (1.0 - ADAM_B2 ** ADAM_STEP)
    delta = -ADAM_LR * (m_hat / (jnp.sqrt(v_hat) + ADAM_EPS) + ADAM_WD * w)
    return delta, m, v


def _adam(name, parts, w, m, v, tr=128):
    rows, cols = w.shape
    tr = min(tr, rows)
    while rows % tr:
        tr -= 8

    def body(p_ref, w_ref, m_ref, v_ref, g_ref, d_ref, mo_ref, vo_ref):
        g = p_ref[0].astype(F32)
        for s in range(1, N_DEV):
            g = g + p_ref[s].astype(F32)
        g_ref[...] = g
        d_ref[...], mo_ref[...], vo_ref[...] = _adam_math(g, w_ref[...], m_ref[...], v_ref[...])

    blk = pl.BlockSpec((tr, cols), lambda i: (i, 0))
    shp = jax.ShapeDtypeStruct((rows, cols), F32)
    return _call(name, body, (rows // tr,), [parts, w, m, v],
                 [pl.BlockSpec((N_DEV, tr, cols), lambda i: (0, i, 0)), blk, blk, blk], [blk] * 4, [shp] * 4)


def _sum8(name, parts):
    _, rows, cols = parts.shape

    def body(p_ref, o_ref):
        g = p_ref[0]
        for s in range(1, N_DEV):
            g = g + p_ref[s]
        o_ref[...] = g

    return pl.pallas_call(body, name=name, out_shape=jax.ShapeDtypeStruct((rows, cols), F32),
                          compiler_params=_params())(parts)


def _adam_small(name, g, w, m, v):
    def body(g_ref, w_ref, m_ref, v_ref, d_ref, mo_ref, vo_ref):
        d_ref[...], mo_ref[...], vo_ref[...] = _adam_math(g_ref[...], w_ref[...], m_ref[...], v_ref[...])

    shp = jax.ShapeDtypeStruct(g.shape, F32)
    return pl.pallas_call(body, name=name, out_shape=[shp] * 3, compiler_params=_params())(g, w, m, v)


def _pack(arrs, lane=128):
    flat = jnp.concatenate([a.reshape(-1).astype(F32) for a in arrs])
    pad = (-flat.shape[0]) % (8 * lane)
    return jnp.pad(flat, (0, pad)).reshape(-1, lane)


def _unpack(packed, shapes):
    flat, out, off = packed.reshape(-1), [], 0
    for s in shapes:
        sz = math.prod(s)
        out.append(flat[off:off + sz].reshape(s))
        off += sz
    return out


def _pad_in_cols(wfull):
    zeros = jnp.zeros((wfull.shape[0], SHIFT_PAD - SHIFT_COLS), wfull.dtype)
    return jnp.concatenate([wfull[:, :SHIFT_COLS], zeros, wfull[:, SHIFT_COLS:]], axis=1)


def _unpad_in_cols(gfull):
    return jnp.concatenate([gfull[:, :SHIFT_COLS], gfull[:, SHIFT_PAD:]], axis=1)


def _block_diag2(wd):
    z = jnp.zeros_like(wd[0])
    return jnp.concatenate([jnp.concatenate([wd[0], z], axis=1), jnp.concatenate([z, wd[1]], axis=1)], axis=0)


def _unblock_diag2(g):
    return jnp.stack([g[0:64, 0:RW], g[64:128, RW:2 * RW]])


SMALL_SHARDED = ("w0", "w2", "a0", "a2", "g2", "conv_dw")
SMALL_REPL = ("mu_prev", "mu_next", "k_k", "k_a", "r_k", "lnx_g", "lnx_b", "conv_b", "conv_ln_g", "conv_ln_b",
              "ln1_g", "ln1_b", "ln2_g", "ln2_b", "ln3_g", "ln3_b")
BIG = ("ffn1_w_in", "ffn1_w_out", "w_in", "w_out", "ffn2_w_in", "ffn2_w_out")
WEIGHTS = ("ffn1_w_in", "ffn1_w_out", "w_in", "mu_prev", "mu_next", "w0", "w2", "a0", "a2", "g2", "k_k", "k_a",
           "r_k", "lnx_g", "lnx_b", "conv_dw", "conv_b", "conv_ln_g", "conv_ln_b", "w_out", "ffn2_w_in",
           "ffn2_w_out", "ln1_g", "ln1_b", "ln2_g", "ln2_b", "ln3_g", "ln3_b")


def _full_small(name, full):
    if name in ("w0", "a0"):
        return full.reshape(1, 2 * RW)
    if name in ("w2", "a2"):
        return _block_diag2(full)
    if name == "g2":
        return jnp.pad(full, ((0, 256 - GATE_LORA), (0, 0)))
    if name == "conv_dw":
        return jnp.pad(full, ((0, 1), (0, 0)))
    if name in ("mu_prev", "mu_next"):
        return jnp.pad(full.reshape(1, SHIFT_COLS), ((0, 0), (0, SHIFT_PAD - SHIFT_COLS)))
    return full.reshape(1, -1)


def _grad_small(name, g):
    if name in ("w0", "a0"):
        return g.reshape(2, RW)
    if name in ("w2", "a2"):
        return _unblock_diag2(g)
    if name == "g2":
        return g[:GATE_LORA]
    if name == "conv_dw":
        return g[:CONV_K]
    if name in ("mu_prev", "mu_next"):
        return g[0, :SHIFT_COLS]
    if name == "r_k":
        return g.reshape(N_HEADS, HEAD)
    return g.reshape(-1)


def kernel(x, ffn1_w_in, ffn1_w_out, w_in, mu_prev, mu_next, w0, w2, a0, a2, g2, k_k, k_a, r_k, lnx_g, lnx_b, conv_dw, conv_b, conv_ln_g, conv_ln_b, w_out, ffn2_w_in, ffn2_w_out, ln1_g, ln1_b, ln2_g, ln2_b, ln3_g, ln3_b, loss_target, m_ffn1_w_in, m_ffn1_w_out, m_w_in, m_mu_prev, m_mu_next, m_w0, m_w2, m_a0, m_a2, m_g2, m_k_k, m_k_a, m_r_k, m_lnx_g, m_lnx_b, m_conv_dw, m_conv_b, m_conv_ln_g, m_conv_ln_b, m_w_out, m_ffn2_w_in, m_ffn2_w_out, m_ln1_g, m_ln1_b, m_ln2_g, m_ln2_b, m_ln3_g, m_ln3_b, v_ffn1_w_in, v_ffn1_w_out, v_w_in, v_mu_prev, v_mu_next, v_w0, v_w2, v_a0, v_a2, v_g2, v_k_k, v_k_a, v_r_k, v_lnx_g, v_lnx_b, v_conv_dw, v_conv_b, v_conv_ln_g, v_conv_ln_b, v_w_out, v_ffn2_w_in, v_ffn2_w_out, v_ln1_g, v_ln1_b, v_ln2_g, v_ln2_b, v_ln3_g, v_ln3_b):
    args = dict(locals())
    drop = lambda z: z.reshape(z.shape[1:])
    wsh = {n: drop(args[n]) for n in WEIGHTS}
    msh = {n: drop(args["m_" + n]) for n in WEIGHTS}
    vsh = {n: drop(args["v_" + n]) for n in WEIGHTS}
    me = 4 * lax.axis_index("x") + 2 * lax.axis_index("y") + lax.axis_index("c")
    bf = {n: wsh[n].astype(BF16) for n in BIG}

    w = {"ffn1_w_in": _gather_two_level("gather_ffn1_w_in", bf["ffn1_w_in"])}
    for n in SMALL_REPL:
        w[n] = _full_small(n, wsh[n])
    small_shapes = [wsh[n].shape for n in SMALL_SHARDED]

    def finish1(got):
        f1_out, w_in_g, small = got
        cols = zip(*[_unpack(small[dv], small_shapes) for dv in range(N_DEV)])
        out = {n: _full_small(n, jnp.concatenate(s, axis=-1)) for n, s in zip(SMALL_SHARDED, cols)}
        out["ffn1_w_out"] = f1_out.reshape(D_FF, D_MODEL)
        out["w_in"] = _pad_in_cols(w_in_g.transpose(1, 0, 2).reshape(D_MODEL, IN_COLS))
        return out

    def finish2(got):
        w_out_g, f2_in, f2_out = got
        return {"w_out": w_out_g.reshape(D_MODEL, D_MODEL), "ffn2_w_in": f2_in,
                "ffn2_w_out": f2_out.reshape(D_FF, D_MODEL)}

    ex = {"g1": ([bf["ffn1_w_out"], bf["w_in"], _pack([wsh[n] for n in SMALL_SHARDED])], finish1),
          "g2": ([bf["w_out"], bf["ffn2_w_in"], bf["ffn2_w_out"]], finish2)}
    loss_part, grad_x, gr, parts = _local_step(x, loss_target, w, ex)
    loss = lax.psum(loss_part, ("x", "y", "c"))

    out = {n: _adam("adam_" + n, parts[n], wsh[n], msh[n], vsh[n]) for n in BIG}
    small_names = SMALL_SHARDED + SMALL_REPL
    full_shapes = [_grad_small(n, gr[n]).shape for n in small_names]
    summed = _unpack(_sum8("sum_small_grads", parts["small"]), full_shapes)
    mine = []
    for n, g in zip(small_names, summed):
        if n in SMALL_SHARDED:
            g = lax.dynamic_slice_in_dim(g, me * HEAD, HEAD, axis=g.ndim - 1)
        mine.append(g)
    shapes = [g.shape for g in mine]
    d_s, m_s, v_s = _adam_small("adam_small", _pack(mine), _pack([wsh[n] for n in small_names]),
                                _pack([msh[n] for n in small_names]), _pack([vsh[n] for n in small_names]))
    for n, g, dl, mn, vn in zip(small_names, mine, _unpack(d_s, shapes), _unpack(m_s, shapes), _unpack(v_s, shapes)):
        out[n] = (g, dl, mn, vn)

    res = [loss, grad_x]
    for k in range(4):
        res += [out[n][k].reshape((1,) + out[n][k].shape) for n in WEIGHTS]
    return tuple(res)
```

```python
import functools
import math

import jax
import jax.numpy as jnp
from jax import lax
from jax.experimental import pallas as pl
from jax.experimental.pallas import tpu as pltpu

F32 = jnp.float32
BF16 = jnp.bfloat16

N_DEV = 8
D_MODEL = 1024
RW = 512
N_HEADS = 8
HEAD = 64
CW = 512
CONV_K = 31
D_FF = 2816
FF_BLK = 704
GATE_LORA = 160
SHIFT_COLS = 1952
SHIFT_PAD = 2048
IN_COLS = 2976
IN_PAD = 3072
LN_EPS = 1e-5
GN_EPS = 64e-5
NORM_EPS = 1e-12
ALPHA = 2.0 ** 0.25
DECAY_SCALE = math.exp(-0.5)
CHUNK = 64
ADAM_LR, ADAM_B1, ADAM_B2, ADAM_EPS, ADAM_WD, ADAM_STEP = 0.001, 0.9, 0.999, 1e-8, 0.01, 10
VMEM_LIMIT = 56 * 1024 * 1024
MXU_DIM = 256

_DN = {"nn": (((1,), (0,)), ((), ())), "nt": (((1,), (1,)), ((), ())), "tn": (((0,), (0,)), ((), ()))}


def _params():
    return pltpu.CompilerParams(vmem_limit_bytes=VMEM_LIMIT)


def _dot(a, b, dims="nn"):
    return lax.dot_general(a, b, _DN[dims], preferred_element_type=F32)


def _split(x):
    hi = x.astype(BF16)
    return hi, (x - hi.astype(F32)).astype(BF16)


def _dot3_impl(a, b, dims):
    ah, al = _split(a)
    bh, bl = _split(b)
    ka, kb = _DN[dims][0][0][0], _DN[dims][0][1][0]
    return _dot(jnp.concatenate([ah, ah, al], axis=ka), jnp.concatenate([bh, bl, bh], axis=kb), dims)


@functools.partial(jax.custom_vjp, nondiff_argnums=(2,))
def _dot3(a, b, dims="nn"):
    return _dot3_impl(a, b, dims)


def _dot3_fwd(a, b, dims):
    return _dot3_impl(a, b, dims), (a, b)


def _dot2_ct(x, y, dims, ct_left):
    ka, kb = _DN[dims][0][0][0], _DN[dims][0][1][0]
    if ct_left:
        c, (h, l) = x.astype(BF16), _split(y)
        return _dot(jnp.concatenate([c, c], axis=ka), jnp.concatenate([h, l], axis=kb), dims)
    (h, l), c = _split(x), y.astype(BF16)
    return _dot(jnp.concatenate([h, l], axis=ka), jnp.concatenate([c, c], axis=kb), dims)


def _dot3_bwd(dims, res, ct):
    a, b = res
    if dims == "nn":
        return _dot2_ct(ct, b, "nt", True), _dot2_ct(a, ct, "tn", False)
    if dims == "nt":
        return _dot2_ct(ct, b, "nn", True), _dot2_ct(ct, a, "tn", True)
    return _dot2_ct(b, ct, "nt", False), _dot2_ct(a, ct, "nn", False)


_dot3.defvjp(_dot3_fwd, _dot3_bwd)


def _dot1_impl(a, b, dims):
    return _dot(a.astype(BF16), b.astype(BF16), dims)


@functools.partial(jax.custom_vjp, nondiff_argnums=(2,))
def _dot1(a, b, dims="nn"):
    return _dot1_impl(a, b, dims)


def _dot1_bwd(dims, res, ct):
    a, b = res
    if dims == "nn":
        return _dot1_impl(ct, b, "nt"), _dot1_impl(a, ct, "tn")
    if dims == "nt":
        return _dot1_impl(ct, b, "nn"), _dot1_impl(ct, a, "tn")
    return _dot1_impl(b, ct, "nt"), _dot1_impl(a, ct, "nn")


_dot1.defvjp(lambda a, b, dims: (_dot1_impl(a, b, dims), (a, b)), _dot1_bwd)


def _dot1_two(a, b, dims="nn"):
    ax_a, ax_b = {"nn": (0, 1), "nt": (0, 0), "tn": (1, 1)}[dims]
    shallow = dims != "tn" and 2 * a[0].shape[1] <= MXU_DIM
    out = []
    for i in range(0, len(a), 2):
        if shallow:
            z = jnp.zeros_like(b[i])
            bd = jnp.concatenate([jnp.concatenate([b[i], z], axis=1), jnp.concatenate([z, b[i + 1]], axis=1)], axis=0)
            r = _dot1(jnp.concatenate(a[i:i + 2], axis=1), bd, dims)
            n = r.shape[1] // 2
            out += [r[:, :n], r[:, n:]]
        else:
            r = _dot1(jnp.concatenate(a[i:i + 2], axis=ax_a), jnp.concatenate(b[i:i + 2], axis=ax_b), dims)
            m, n = r.shape[0] // 2, r.shape[1] // 2
            out += [r[:m, :n], r[m:, n:]]
    return out


def _tri_inv_impl(l, eye):
    steps = int(math.log2(CHUNK)) - 1
    m = l[0].shape[0]
    tm = [eye + x for x in l]
    lp = _dot1_two(l, l)
    for k in range(steps):
        if k < steps - 1:
            both = _dot1_two([jnp.concatenate([t, p], axis=0) for t, p in zip(tm, lp)], lp)
            tm = [t + b[:m] for t, b in zip(tm, both)]
            lp = [b[m:] for b in both]
        else:
            tm = [t + x for t, x in zip(tm, _dot1_two(tm, lp))]
    return tm


@jax.custom_vjp
def _tri_inv(l, eye):
    return _tri_inv_impl(l, eye)


def _tri_inv_fwd(l, eye):
    tm = _tri_inv_impl(l, eye)
    return tm, (tm, eye)


def _tri_inv_bwd(res, ct):
    tm, eye = res
    return _dot1_two(_dot1_two(tm, ct, "tn"), tm, "nt"), jnp.zeros_like(eye)


_tri_inv.defvjp(_tri_inv_fwd, _tri_inv_bwd)


@jax.custom_vjp
def _tri_inv_known(l, tm):
    return tm


_tri_inv_known.defvjp(lambda l, tm: (tm, tm),
                      lambda tm, ct: (_dot1_two(_dot1_two(tm, ct, "tn"), tm, "nt"), [jnp.zeros_like(t) for t in tm]))


def _ones_impl(x, g2):
    x1 = x.astype(BF16)
    x2 = (x - x1.astype(F32)).astype(BF16)
    return _dot(jnp.concatenate([x1, x2], axis=1), g2)


@jax.custom_vjp
def _head_sum(x, g2):
    return _ones_impl(x, g2)


_head_sum.defvjp(lambda x, g2: (_ones_impl(x, g2), g2), lambda g2, ct: (_ones_impl(ct, g2), jnp.zeros_like(g2)))


def _prefix_sum(x):
    row = lax.broadcasted_iota(jnp.int32, x.shape, 0)
    sh = 1
    while sh < x.shape[0]:
        x = x + jnp.where(row >= sh, pltpu.roll(x, sh, 0), 0.0)
        sh *= 2
    return x


def _dir_cumsum_impl(x, sgn):
    pre = _prefix_sum(x)
    return jnp.where(sgn > 0.0, pre, jnp.sum(x, axis=0, keepdims=True) - pre + x)


@jax.custom_vjp
def _dir_cumsum(x, sgn):
    return _dir_cumsum_impl(x, sgn)


_dir_cumsum.defvjp(lambda x, sgn: (_dir_cumsum_impl(x, sgn), sgn),
                   lambda sgn, ct: (_dir_cumsum_impl(ct, -sgn), jnp.zeros_like(sgn)))


def _sigmoid(x):
    return 1.0 / (1.0 + jnp.exp(-x))


def _mesh_pos():
    return lax.axis_index("x"), lax.axis_index("y"), lax.axis_index("c")


def _peer(pos, q):
    x, y, c = pos
    return (1 - x if q & 4 else x, 1 - y if q & 2 else y, 1 - c if q & 1 else c)


def _linear(pos):
    return 4 * pos[0] + 2 * pos[1] + pos[2]


def _exchange_copies(x_refs, o_refs, send_sems, recv_sems, local_sems, gather):
    pos = _mesh_pos()
    me = _linear(pos)
    starts, wait_recv, wait_send, wait_local = [], [], [], []
    for t in range(len(x_refs)):
        src = x_refs[t] if gather else x_refs[t].at[me]
        cp = pltpu.make_async_copy(src, o_refs[t].at[me], local_sems.at[t])
        starts.append(cp.start)
        wait_local.append(cp.wait)
    for q in range(1, N_DEV):
        peer = _peer(pos, q)
        for t in range(len(x_refs)):
            src = x_refs[t] if gather else x_refs[t].at[_linear(peer)]
            sems = dict(send_sem=send_sems.at[t, q - 1], recv_sem=recv_sems.at[t, q - 1],
                        device_id=peer, device_id_type=pl.DeviceIdType.MESH)
            send = pltpu.make_async_remote_copy(src_ref=src, dst_ref=o_refs[t].at[me], **sems)
            recv = pltpu.make_async_remote_copy(src_ref=src, dst_ref=o_refs[t].at[_linear(peer)], **sems)
            starts.append(send.start)
            wait_recv.append(recv.wait_recv)
            wait_send.append(send.wait_send)
    return starts, wait_recv + wait_send + wait_local


def _exchange_shapes(xs, gather):
    return [jax.ShapeDtypeStruct((N_DEV,) + (x.shape if gather else x.shape[1:]), x.dtype) for x in xs]


def _exchange_sems(nt):
    return [pltpu.SemaphoreType.DMA((nt, N_DEV - 1)), pltpu.SemaphoreType.DMA((nt, N_DEV - 1)),
            pltpu.SemaphoreType.DMA((nt,))]


def _gather_two_level(name, x):
    def body(x_ref, out_ref, send_sems, recv_sems, local_sem):
        px, py, pc = _mesh_pos()
        me, sibling = (px, py, pc), (px, py, 1 - pc)
        chips = [(1 - px, py), (px, 1 - py), (1 - px, 1 - py)]

        def slot(pos):
            return out_ref.at[_linear(pos)]

        def copy(k, block, to, src=None):
            return pltpu.make_async_remote_copy(
                src_ref=slot(block) if src is None else src, dst_ref=slot(block), send_sem=send_sems.at[k],
                recv_sem=recv_sems.at[k], device_id=to, device_id_type=pl.DeviceIdType.MESH)

        mine = pltpu.make_async_copy(x_ref, slot(me), local_sem)
        mine.start()
        first = [copy(0, me, sibling, src=x_ref)]
        first += [copy(1 + j, me, (*chip, pc), src=x_ref) for j, chip in enumerate(chips)]
        for cp in first:
            cp.start()
        passed = [copy(4 + j, (*chip, pc), sibling) for j, chip in enumerate(chips)]
        for j, chip in enumerate(chips):
            copy(1 + j, (*chip, pc), me).wait_recv()
            passed[j].start()
        copy(0, sibling, me).wait_recv()
        for j, chip in enumerate(chips):
            copy(4 + j, (*chip, 1 - pc), me).wait_recv()
        for cp in first + passed:
            cp.wait_send()
        mine.wait()

    any_spec = pl.BlockSpec(memory_space=pl.ANY)
    return pl.pallas_call(
        body, name=name, in_specs=[any_spec], out_specs=any_spec,
        out_shape=jax.ShapeDtypeStruct((N_DEV,) + x.shape, x.dtype),
        scratch_shapes=[pltpu.SemaphoreType.DMA((N_DEV - 1,)), pltpu.SemaphoreType.DMA((N_DEV - 1,)),
                        pltpu.SemaphoreType.DMA])(x)


def _call(name, body, grid, ins, in_specs, out_specs, out_shape, scratch=(), exch=None):
    if exch is None:
        return pl.pallas_call(body, name=name, grid=grid, in_specs=in_specs, out_specs=out_specs,
                              out_shape=out_shape, scratch_shapes=list(scratch), compiler_params=_params())(*ins)
    xs, gather = exch
    single = not isinstance(out_shape, (list, tuple))
    o_specs = [out_specs] if single else list(out_specs)
    o_shape = [out_shape] if single else list(out_shape)
    n_in, n_out, n_x, n_scr = len(ins), len(o_shape), len(xs), len(scratch)

    def wrapped(*refs):
        in_refs = refs[:n_in]
        x_refs = refs[n_in:n_in + n_x]
        out_refs = refs[n_in + n_x:n_in + n_x + n_out]
        got_refs = refs[n_in + n_x + n_out:n_in + 2 * n_x + n_out]
        rest = refs[n_in + 2 * n_x + n_out:]
        starts, waits = _exchange_copies(x_refs, got_refs, *rest[n_scr:], gather)
        ids = [pl.program_id(i) for i in range(len(grid))]
        first = functools.reduce(lambda p, q: p & q, [i == 0 for i in ids])
        last = functools.reduce(lambda p, q: p & q, [i == g - 1 for i, g in zip(ids, grid)])

        @pl.when(first)
        def _():
            for f in starts:
                f()

        body(*in_refs, *out_refs, *rest[:n_scr])

        @pl.when(last)
        def _():
            for f in waits:
                f()

    any_spec = pl.BlockSpec(memory_space=pl.ANY)
    outs = pl.pallas_call(
        wrapped, name=name, grid=grid, in_specs=list(in_specs) + [any_spec] * n_x,
        out_specs=o_specs + [any_spec] * n_x, out_shape=o_shape + _exchange_shapes(xs, gather),
        scratch_shapes=list(scratch) + _exchange_sems(n_x), compiler_params=_params())(*ins, *xs)
    res = outs[:n_out]
    return (res[0] if single else res), outs[n_out:]


def _mm_call(name, dims, grid, red_axis, ins, in_specs, out_shape, out_spec, acc_shape,
             scale=1.0, add_scale=None, exch=None):
    nred = grid[red_axis]

    def body(*refs):
        if add_scale is None:
            a_ref, b_ref, o_ref, acc_ref = refs
            add_ref = None
        else:
            a_ref, b_ref, add_ref, o_ref, acc_ref = refs
        k = pl.program_id(red_axis)

        @pl.when(k == 0)
        def _():
            acc_ref[...] = jnp.zeros_like(acc_ref)

        acc_ref[...] += _dot(a_ref[...].astype(BF16), b_ref[...].astype(BF16), dims)

        @pl.when(k == nred - 1)
        def _():
            r = acc_ref[...]
            if scale != 1.0:
                r = r * scale
            if add_ref is not None:
                r = r + add_scale * add_ref[...].astype(F32)
            o_ref[...] = r.astype(o_ref.dtype)

    return _call(name, body, grid, ins, in_specs, out_spec, out_shape, [pltpu.VMEM(acc_shape, F32)], exch)


def _tile(dim, cap):
    t = min(dim, cap)
    while dim % t or t % 128:
        t -= 128
        assert t > 0, (dim, cap)
    return t


def _mm(name, a, b, dims, out_dtype, scale=1.0, add=None, add_scale=None, tm=512, tn=1024, tk=1024):
    if dims == "tn":
        kd, m = a.shape
        n = b.shape[1]
    else:
        m, kd = a.shape
        n = b.shape[1] if dims == "nn" else b.shape[0]
    tm, tn, tk = _tile(m, tm), _tile(n, tn), _tile(kd, tk)
    a_spec = (pl.BlockSpec((tk, tm), lambda i, j, k: (k, i)) if dims == "tn"
              else pl.BlockSpec((tm, tk), lambda i, j, k: (i, k)))
    b_spec = (pl.BlockSpec((tn, tk), lambda i, j, k: (j, k)) if dims == "nt"
              else pl.BlockSpec((tk, tn), lambda i, j, k: (k, j)))
    o_spec = pl.BlockSpec((tm, tn), lambda i, j, k: (i, j))
    ins, specs = [a, b], [a_spec, b_spec]
    if add is not None:
        ins.append(add)
        specs.append(o_spec)
    return _mm_call(name, dims, (m // tm, n // tn, kd // tk), 2, ins, specs,
                    jax.ShapeDtypeStruct((m, n), out_dtype), o_spec, (tm, tn),
                    scale=scale, add_scale=add_scale if add is not None else None)


def _ffn_in(name, x, wg, tm=512, exch=None):
    n = x.shape[0]
    nj = N_DEV // 2

    def body(x_ref, wgate_ref, wup_ref, hg_ref, hu_ref, act_ref):
        xb = x_ref[...].astype(BF16)
        g = _dot(xb, wgate_ref[...])
        u = _dot(xb, wup_ref[...])
        hg_ref[...] = g.astype(BF16)
        hu_ref[...] = u.astype(BF16)
        act_ref[...] = (g * _sigmoid(g) * u).astype(BF16)

    blk = pl.BlockSpec((None, tm, FF_BLK), lambda j, i: (j, i, 0))
    shp = jax.ShapeDtypeStruct((nj, n, FF_BLK), BF16)
    return _call(name, body, (nj, n // tm), [x, wg, wg],
                 [pl.BlockSpec((tm, D_MODEL), lambda j, i: (i, 0)),
                  pl.BlockSpec((None, D_MODEL, FF_BLK), lambda j, i: (j, 0, 0)),
                  pl.BlockSpec((None, D_MODEL, FF_BLK), lambda j, i: (j + nj, 0, 0))],
                 [blk, blk, blk], [shp, shp, shp], exch=exch)


def _ffn_out_bwd(name, dz, wout, hg, hu, act, tm=512, exch=None):
    n = dz.shape[0]
    nj, ni = N_DEV // 2, n // tm

    def body(dz_ref, w_ref, hg_ref, hu_ref, act_ref, dhg_ref, dhu_ref, dw_ref, acc_ref):
        i = pl.program_id(1)
        dzb = dz_ref[...].astype(BF16)
        dact = 0.5 * _dot(dzb, w_ref[...], "nt")
        g = hg_ref[...].astype(F32)
        u = hu_ref[...].astype(F32)
        s = _sigmoid(g)
        dhg_ref[...] = (dact * u * (s * (1.0 + g * (1.0 - s)))).astype(BF16)
        dhu_ref[...] = (dact * (g * s)).astype(BF16)

        @pl.when(i == 0)
        def _():
            acc_ref[...] = jnp.zeros_like(acc_ref)

        acc_ref[...] += _dot(act_ref[...], dzb, "tn")

        @pl.when(i == ni - 1)
        def _():
            dw_ref[...] = (0.5 * acc_ref[...]).astype(dw_ref.dtype)

    blk = pl.BlockSpec((None, tm, FF_BLK), lambda j, i: (j, i, 0))
    wblk = pl.BlockSpec((FF_BLK, D_MODEL), lambda j, i: (j, 0))
    shp = jax.ShapeDtypeStruct((nj, n, FF_BLK), BF16)
    return _call(name, body, (nj, ni), [dz, wout, hg, hu, act],
                 [pl.BlockSpec((tm, D_MODEL), lambda j, i: (i, 0)), wblk, blk, blk, blk],
                 [blk, blk, wblk], [shp, shp, jax.ShapeDtypeStruct((D_FF, D_MODEL), BF16)],
                 [pltpu.VMEM((FF_BLK, D_MODEL), F32)], exch=exch)


def _mm_ln(name, a, b, xres, g, beta, c, tgt=None, a_blocked=False, tm=512, tk=512):
    if a_blocked:
        nj, n, kj = a.shape
        nk, kb = nj // 2, 2 * kj
        a_spec = pl.BlockSpec((2, tm, kj), lambda i, k: (k, i, 0))
    else:
        n, kd = a.shape
        kb = _tile(kd, tk)
        nk = kd // kb
        a_spec = pl.BlockSpec((tm, kb), lambda i, k: (i, k))
    d = b.shape[1]
    with_loss = tgt is not None

    def body(*refs):
        if with_loss:
            a_ref, b_ref, x_ref, g_ref, be_ref, t_ref, dz_ref, dzb_ref, l_ref, dg_ref, db_ref, acc_ref = refs
        else:
            a_ref, b_ref, x_ref, g_ref, be_ref, o_ref, ob_ref, z_ref, acc_ref = refs
        i, k = pl.program_id(0), pl.program_id(1)

        @pl.when(k == 0)
        def _():
            acc_ref[...] = jnp.zeros_like(acc_ref)

        if a_blocked:
            acc_ref[...] += (_dot(a_ref[0].astype(BF16), b_ref[0:kb // 2, :].astype(BF16))
                             + _dot(a_ref[1].astype(BF16), b_ref[kb // 2:kb, :].astype(BF16)))
        else:
            acc_ref[...] += _dot(a_ref[...].astype(BF16), b_ref[...].astype(BF16))

        @pl.when(k == nk - 1)
        def _():
            z = ALPHA * x_ref[...] + c * acc_ref[...]
            mu = jnp.mean(z, axis=-1, keepdims=True)
            zc = z - mu
            rstd = lax.rsqrt(jnp.mean(zc * zc, axis=-1, keepdims=True) + LN_EPS)
            xh = zc * rstd
            y = xh * g_ref[...] + be_ref[...]
            if with_loss:
                err = y - t_ref[...]
                ct = err * (1.0 / d)
                dxh = ct * g_ref[...]
                dz = rstd * (dxh - jnp.mean(dxh, axis=-1, keepdims=True)
                             - xh * jnp.mean(dxh * xh, axis=-1, keepdims=True))
                dz_ref[...] = dz
                dzb_ref[...] = dz.astype(BF16)
                part = 0.5 * jnp.sum(jnp.sum(err * err, axis=-1, keepdims=True), axis=0, keepdims=True) * (1.0 / d)

                @pl.when(i == 0)
                def _():
                    l_ref[...] = jnp.zeros_like(l_ref)
                    dg_ref[...] = jnp.zeros_like(dg_ref)
                    db_ref[...] = jnp.zeros_like(db_ref)

                l_ref[...] += jnp.broadcast_to(part, l_ref.shape)
                dg_ref[...] += jnp.sum(ct * xh, axis=0, keepdims=True)
                db_ref[...] += jnp.sum(ct, axis=0, keepdims=True)
            else:
                z_ref[...] = z
                o_ref[...] = y
                ob_ref[...] = y.astype(BF16)

    row = pl.BlockSpec((tm, d), lambda i, k: (i, 0))
    vec = pl.BlockSpec((1, d), lambda i, k: (0, 0))
    ins = [a, b, xres, g, beta]
    in_specs = [a_spec, pl.BlockSpec((kb, d), lambda i, k: (k, 0)), row, vec, vec]
    out_specs = [row, row]
    out_shape = [jax.ShapeDtypeStruct((n, d), F32), jax.ShapeDtypeStruct((n, d), BF16)]
    if with_loss:
        ins.append(tgt)
        in_specs.append(row)
        out_specs += [pl.BlockSpec((1, 128), lambda i, k: (0, 0)), vec, vec]
        out_shape += [jax.ShapeDtypeStruct((1, 128), F32)] + [jax.ShapeDtypeStruct((1, d), F32)] * 2
    else:
        out_specs.append(row)
        out_shape.append(jax.ShapeDtypeStruct((n, d), F32))
    return _call(name, body, (n // tm, nk), ins, in_specs, out_specs, out_shape, [pltpu.VMEM((tm, d), F32)])


def _rowwise(name, fn, rows, params, out_rows, out_accs, tm=256, exch=None):
    specs, ins = [], []
    for r in rows:
        arr, w, cb = r if isinstance(r, tuple) else (r, r.shape[1], 0)
        ins.append(arr)
        specs.append(pl.BlockSpec((tm, w), functools.partial(lambda i, cb: (i, cb), cb=cb)))
    n = ins[0].shape[0]
    for p in params:
        ins.append(p)
        specs.append(pl.BlockSpec(p.shape, lambda i: (0, 0)))
    n_in, n_or = len(ins), len(out_rows)

    def body(*refs):
        outs = fn(*[r[...] for r in refs[:n_in]])
        o_refs = refs[n_in:]
        for o_ref, o in zip(o_refs[:n_or], outs[:n_or]):
            o_ref[...] = o.astype(o_ref.dtype)
        if out_accs:
            @pl.when(pl.program_id(0) == 0)
            def _():
                for a_ref in o_refs[n_or:]:
                    a_ref[...] = jnp.zeros_like(a_ref)

            for a_ref, a in zip(o_refs[n_or:], outs[n_or:]):
                a_ref[...] += a.astype(F32)

    out_specs = [pl.BlockSpec((tm, w), lambda i: (i, 0)) for w, _ in out_rows]
    out_specs += [pl.BlockSpec(s, lambda i: (0, 0)) for s in out_accs]
    out_shape = [jax.ShapeDtypeStruct((n, w), dt) for w, dt in out_rows]
    out_shape += [jax.ShapeDtypeStruct(s, F32) for s in out_accs]
    return _call(name, body, (n // tm,), ins, specs, out_specs, out_shape, exch=exch)


def _vjp_of(fn, n_in):
    def g(*args):
        ins, cts = args[:n_in], args[n_in:]
        outs, pull = jax.vjp(fn, *ins)
        return pull(tuple(c.astype(o.dtype) for c, o in zip(cts, outs)))
    return g


def _ln_bwd(name, z, g, ct):
    def fn(zt, ct_, gt):
        mu = jnp.mean(zt, axis=-1, keepdims=True)
        zc = zt - mu
        rstd = lax.rsqrt(jnp.mean(zc * zc, axis=-1, keepdims=True) + LN_EPS)
        xh = zc * rstd
        dxh = ct_ * gt
        dz = rstd * (dxh - jnp.mean(dxh, axis=-1, keepdims=True)
                     - xh * jnp.mean(dxh * xh, axis=-1, keepdims=True))
        return dz, dz, jnp.sum(ct_ * xh, axis=0, keepdims=True), jnp.sum(ct_, axis=0, keepdims=True)

    d = z.shape[1]
    return _rowwise(name, fn, [z, ct], [g], [(d, F32), (d, BF16)], [(1, d), (1, d)])


SHIFT_TILE = 256


def _shift_specs(t):
    r8 = SHIFT_TILE // 8
    return [pl.BlockSpec((None, SHIFT_TILE, SHIFT_PAD), lambda b, i: (b, i, 0)),
            pl.BlockSpec((None, 8, SHIFT_PAD), lambda b, i: (b, jnp.maximum(i * r8 - 1, 0), 0)),
            pl.BlockSpec((None, 8, SHIFT_PAD), lambda b, i: (b, jnp.minimum((i + 1) * r8, t // 8 - 1), 0))]


def _neighbour_diffs(cur, prev_ref, next_ref, i, nt):
    prow = jnp.where(i > 0, prev_ref[7:8, :], 0.0)
    nrow = jnp.where(i < nt - 1, next_ref[0:1, :], 0.0)
    rid = lax.broadcasted_iota(jnp.int32, cur.shape, 0)
    return (jnp.where(rid == 0, prow, pltpu.roll(cur, 1, 0)) - cur,
            jnp.where(rid == SHIFT_TILE - 1, nrow, pltpu.roll(cur, SHIFT_TILE - 1, 0)) - cur)


def _shift(name, src, mu_prev, mu_next):
    bsz, t, _ = src.shape
    nt, w = t // SHIFT_TILE, SHIFT_PAD

    def body(cur_ref, prev_ref, next_ref, mp_ref, mn_ref, o_ref):
        cur = cur_ref[...]
        dprev, dnext = _neighbour_diffs(cur, prev_ref, next_ref, pl.program_id(1), nt)
        o_ref[...] = cur + mp_ref[...] * dprev + mn_ref[...] * dnext

    specs = _shift_specs(t)
    vec = pl.BlockSpec((1, w), lambda b, i: (0, 0))
    return _call(name, body, (bsz, nt), [src, src, src, mu_prev, mu_next], specs + [vec, vec], specs[0],
                 jax.ShapeDtypeStruct((bsz, t, w), F32))


def _shift_bwd(name, dps, p, mu_prev, mu_next):
    bsz, t, _ = dps.shape
    nt, w = t // SHIFT_TILE, SHIFT_PAD

    def body(d_ref, dprev_ref, dnext_ref, p_ref, pprev_ref, pnext_ref, mp_ref, mn_ref, o_ref, da_ref, db_ref):
        b, i = pl.program_id(0), pl.program_id(1)
        dcur = d_ref[...]
        d_dprev, d_dnext = _neighbour_diffs(dcur, dprev_ref, dnext_ref, i, nt)
        o_ref[...] = (dcur + mn_ref[...] * d_dprev + mp_ref[...] * d_dnext).astype(o_ref.dtype)
        p_dprev, p_dnext = _neighbour_diffs(p_ref[...], pprev_ref, pnext_ref, i, nt)

        @pl.when((b == 0) & (i == 0))
        def _():
            da_ref[...] = jnp.zeros_like(da_ref)
            db_ref[...] = jnp.zeros_like(db_ref)

        da_ref[...] += jnp.sum(dcur * p_dprev, axis=0, keepdims=True)
        db_ref[...] += jnp.sum(dcur * p_dnext, axis=0, keepdims=True)

    specs = _shift_specs(t)
    vec = pl.BlockSpec((1, w), lambda b, i: (0, 0))
    return _call(name, body, (bsz, nt), [dps, dps, dps, p, p, p, mu_prev, mu_next], specs + specs + [vec, vec],
                 [specs[0], vec, vec],
                 [jax.ShapeDtypeStruct((bsz, t, w), BF16), jax.ShapeDtypeStruct((1, w), F32),
                  jax.ShapeDtypeStruct((1, w), F32)])


CONV_BLK = 128


def _halo_specs(t, tt, w):
    r16 = tt // 16
    return [pl.BlockSpec((None, tt, w), lambda b, i: (b, i, 0)),
            pl.BlockSpec((None, 16, w), lambda b, i: (b, jnp.maximum(i * r16 - 1, 0), 0)),
            pl.BlockSpec((None, 16, w), lambda b, i: (b, jnp.minimum((i + 1) * r16, t // 16 - 1), 0))]


def _fill_pad(pad_ref, cur_ref, prev_ref, next_ref, i, nt, tt):
    pad_ref[0:16, :] = jnp.where(i > 0, prev_ref[...], 0.0)
    pad_ref[16:16 + tt, :] = cur_ref[...]
    pad_ref[16 + tt:32 + tt, :] = jnp.where(i < nt - 1, next_ref[...], 0.0)


def _dwconv(name, u, dw32, bias, flip, tt=512):
    bsz, t, w = u.shape
    tt = min(tt, t)
    nt = t // tt

    def body(cur_ref, prev_ref, next_ref, dw_ref, b_ref, o_ref, pad_ref):
        i = pl.program_id(1)
        _fill_pad(pad_ref, cur_ref, prev_ref, next_ref, i, nt, tt)
        for r0 in range(0, tt, CONV_BLK):
            for cs in (slice(c0, c0 + CONV_BLK) for c0 in range(0, w, CONV_BLK)):
                acc = jnp.broadcast_to(b_ref[:, cs], (CONV_BLK, CONV_BLK))
                for k in range(CONV_K):
                    kk = CONV_K - 1 - k if flip else k
                    acc = acc + pad_ref[pl.ds(r0 + 1 + k, CONV_BLK), cs] * dw_ref[kk:kk + 1, cs]
                o_ref[r0:r0 + CONV_BLK, cs] = acc

    return _call(name, body, (bsz, nt), [u, u, u, dw32, bias],
                 _halo_specs(t, tt, w) + [pl.BlockSpec((32, w), lambda b, i: (0, 0)),
                                          pl.BlockSpec((1, w), lambda b, i: (0, 0))],
                 pl.BlockSpec((None, tt, w), lambda b, i: (b, i, 0)), jax.ShapeDtypeStruct((bsz, t, w), F32),
                 [pltpu.VMEM((tt + 32, w), F32)])


def _dwconv_dw(name, u, dc, tt=512):
    bsz, t, w = u.shape
    tt = min(tt, t)
    nt = t // tt

    def body(cur_ref, prev_ref, next_ref, dc_ref, ddw_ref, db_ref, pad_ref):
        b, i = pl.program_id(0), pl.program_id(1)
        _fill_pad(pad_ref, cur_ref, prev_ref, next_ref, i, nt, tt)

        @pl.when((b == 0) & (i == 0))
        def _():
            ddw_ref[...] = jnp.zeros_like(ddw_ref)
            db_ref[...] = jnp.zeros_like(db_ref)

        dcv = dc_ref[...]
        db_ref[...] += jnp.sum(dcv, axis=0, keepdims=True)
        for k in range(CONV_K):
            ddw_ref[k:k + 1, :] += jnp.sum(dcv * pad_ref[pl.ds(1 + k, tt), :], axis=0, keepdims=True)

    return _call(name, body, (bsz, nt), [u, u, u, dc],
                 _halo_specs(t, tt, w) + [pl.BlockSpec((None, tt, w), lambda b, i: (b, i, 0))],
                 [pl.BlockSpec((32, w), lambda b, i: (0, 0)), pl.BlockSpec((1, w), lambda b, i: (0, 0))],
                 [jax.ShapeDtypeStruct((32, w), F32), jax.ShapeDtypeStruct((1, w), F32)],
                 [pltpu.VMEM((tt + 32, w), F32)])


PAIR = 2 * HEAD
N_PAIRS = RW // PAIR


def _chunk_pairs(s, r, lw, k, v, kk, a, sgn, tm_known=None):
    n, m = CHUNK, 2 * CHUNK
    in_a = lax.broadcasted_iota(jnp.int32, (n, PAIR), 1) < HEAD

    def stack2(z):
        return jnp.concatenate([jnp.where(in_a, z, 0.0), jnp.where(in_a, 0.0, z)], axis=0)

    def each(f, *lists):
        return [f(*z) for z in zip(*lists)]

    sgn_f = sgn.astype(F32)
    row2 = lax.broadcasted_iota(jnp.int32, (m, m), 0)
    col2 = lax.broadcasted_iota(jnp.int32, (m, m), 1)
    same = (row2 >= n) == (col2 >= n)
    dlt = ((row2 & (n - 1)) - (col2 & (n - 1))) * sgn
    incl, strict = same & (dlt >= 0), same & (dlt > 0)
    eye = jnp.where(row2 == col2, 1.0, 0.0)

    cum = each(lambda lw_: _dir_cumsum(lw_, sgn_f), lw)
    tot = each(lambda lw_: jnp.sum(lw_, axis=0, keepdims=True), lw)
    e_neg = each(lambda c_: jnp.exp(-c_), cum)
    e_rest = each(lambda t_, c_: jnp.exp(t_ - c_), tot, cum)
    beta = each(lambda kk_, a_: kk_ * a_, kk, a)
    lhs = each(lambda kk_, c_, lw_, r_: jnp.concatenate(
        [stack2(-kk_ * jnp.exp(c_ - lw_)), stack2(r_ * jnp.exp(c_))], axis=0), kk, cum, lw, r)
    rhs = each(lambda b_, k_, e_: jnp.concatenate([stack2(b_ * e_), stack2(k_ * e_)], axis=0), beta, k, e_neg)
    sc = each(lambda l_, r_: _dot3(l_, r_, "nt"), lhs, rhs)
    l_ab = each(lambda sc_: jnp.where(strict, sc_[0:m, 0:m], 0.0), sc)
    l_ak = each(lambda sc_: jnp.where(strict, sc_[0:m, m:2 * m], 0.0), sc)
    m_r = each(lambda sc_: jnp.where(jnp.concatenate([incl, incl], axis=1), sc_[m:2 * m, :], 0.0), sc)
    tm = _tri_inv(l_ab, eye) if tm_known is None else _tri_inv_known(l_ab, tm_known)
    z = _dot1_two(lhs, s, "nt")
    v2 = each(stack2, v)
    u2 = _dot1_two(tm, each(lambda z_, lv_: z_[0:m] + lv_, z, _dot1_two(l_ak, v2)))
    uv = each(lambda u_, v_: jnp.concatenate([u_, v_], axis=0), u2, v2)
    y2 = each(lambda z_, mu_: z_[m:2 * m] + mu_, z, _dot1_two(m_r, uv))
    bk = each(lambda b_, k_, e_: jnp.concatenate([stack2(b_ * e_), stack2(k_ * e_)], axis=0), beta, k, e_rest)
    s_new = each(lambda s_, t_, d_: s_ * jnp.exp(t_) + d_, s, tot, _dot1_two(uv, bk, "tn"))
    return each(lambda y_: y_[0:n] + y_[n:m], y2), s_new, tm


SCAN_SEQS = 4
N_CHAINS = SCAN_SEQS * N_PAIRS


def _pair_tiles(ref):
    return [ref[q, :, p * PAIR:(p + 1) * PAIR] for q in range(SCAN_SEQS) for p in range(N_PAIRS)]


def _store_tiles(ref, tiles):
    for q in range(SCAN_SEQS):
        for p in range(N_PAIRS):
            ref[q, :, p * PAIR:(p + 1) * PAIR] = tiles[q * N_PAIRS + p]


def _scan_specs(order):
    shared = pl.BlockSpec((SCAN_SEQS, CHUNK, RW), lambda d, b, c: (b, order(d, c), 0))
    per_dir = pl.BlockSpec((SCAN_SEQS, CHUNK, RW), lambda d, b, c: (b, order(d, c), d))
    state = pl.BlockSpec((None, SCAN_SEQS, None, N_PAIRS, PAIR, PAIR), lambda d, b, c: (d, b, order(d, c), 0, 0, 0))
    return shared, per_dir, state


def _scan_fwd(r, v, kk, lw, kd, a, bsz, exch=None):
    n = r.shape[0]
    t = n // bsz
    nc = t // CHUNK

    def order(d, c):
        return c + d * (nc - 1 - 2 * c)

    def body(r_ref, v_ref, kk_ref, lw_ref, kd_ref, a_ref, y_ref, s0_ref, tm_ref, s_ref):
        d, c = pl.program_id(0), pl.program_id(2)

        @pl.when(c == 0)
        def _():
            s_ref[...] = jnp.zeros_like(s_ref)

        s = [s_ref[i] for i in range(N_CHAINS)]
        y, s_new, tm = _chunk_pairs(s, *[_pair_tiles(ref) for ref in (r_ref, lw_ref, kd_ref, v_ref, kk_ref, a_ref)],
                                    1 - 2 * d)
        _store_tiles(y_ref, y)
        for i in range(N_CHAINS):
            s0_ref[i // N_PAIRS, i % N_PAIRS] = s[i]
            tm_ref[i // N_PAIRS, i % N_PAIRS] = tm[i].astype(BF16)
            s_ref[i] = s_new[i]

    shared, per_dir, state = _scan_specs(order)
    seq = lambda z: z.reshape(bsz, t, z.shape[1])
    res = _call("scan_fwd", body, (2, bsz // SCAN_SEQS, nc), [seq(z) for z in (r, v, kk, lw, kd, a)],
                [shared, shared, shared, per_dir, per_dir, per_dir], [per_dir, state, state],
                [jax.ShapeDtypeStruct((bsz, t, 2 * RW), F32),
                 jax.ShapeDtypeStruct((2, bsz, nc, N_PAIRS, PAIR, PAIR), F32),
                 jax.ShapeDtypeStruct((2, bsz, nc, N_PAIRS, PAIR, PAIR), BF16)],
                [pltpu.VMEM((N_CHAINS, PAIR, PAIR), F32)], exch)
    (y, s0, tm), got = res if exch else (res, None)
    y = y.reshape(n, 2 * RW)
    return ([y, s0, tm], got) if exch else [y, s0, tm]


def _scan_bwd(r, v, kk, lw, kd, a, s0, tm, dy, bsz, exch=None):
    n = r.shape[0]
    t = n // bsz
    nc = t // CHUNK

    def order(d, c):
        cc = nc - 1 - c
        return cc + d * (nc - 1 - 2 * cc)

    def body(r_ref, v_ref, kk_ref, lw_ref, kd_ref, a_ref, dy_ref, s0_ref, tm_ref,
             dr_ref, dv_ref, dkk_ref, dlw_ref, dkd_ref, da_ref, ds_ref):
        d, c = pl.program_id(0), pl.program_id(2)

        @pl.when(c == 0)
        def _():
            ds_ref[...] = jnp.zeros_like(ds_ref)

        sgn = 1 - 2 * d
        tm_known = [tm_ref[i // N_PAIRS, i % N_PAIRS].astype(F32) for i in range(N_CHAINS)]
        _, pull = jax.vjp(lambda *ops: _chunk_pairs(*ops, sgn, tm_known)[:2],
                          [s0_ref[i // N_PAIRS, i % N_PAIRS] for i in range(N_CHAINS)],
                          *[_pair_tiles(ref) for ref in (r_ref, lw_ref, kd_ref, v_ref, kk_ref, a_ref)])
        grads = pull((_pair_tiles(dy_ref), [ds_ref[i] for i in range(N_CHAINS)]))
        for i in range(N_CHAINS):
            ds_ref[i] = grads[0][i]
        for o_ref, gx in zip((dr_ref, dlw_ref, dkd_ref, dv_ref, dkk_ref, da_ref), grads[1:]):
            _store_tiles(o_ref, gx)

    shared, per_dir, state = _scan_specs(order)
    shp = jax.ShapeDtypeStruct((bsz, t, 2 * RW), F32)
    seq = lambda z: z.reshape(bsz, t, z.shape[1])
    res = _call("scan_bwd", body, (2, bsz // SCAN_SEQS, nc), [seq(z) for z in (r, v, kk, lw, kd, a, dy)] + [s0, tm],
                [shared, shared, shared, per_dir, per_dir, per_dir, per_dir, state, state],
                [per_dir] * 6, [shp] * 6, [pltpu.VMEM((N_CHAINS, PAIR, PAIR), F32)], exch)
    outs, got = res if exch else (res, None)
    outs = [z.reshape(n, 2 * RW) for z in outs]
    return (outs, got) if exch else outs


def _prep_fn(ps, w0, w2bd, a0, a2bd, g2p, k_k, k_a, hsum):
    head_sum = lambda z: _head_sum(z, hsum)
    r, k, v = ps[:, 0:RW], ps[:, RW:2 * RW], ps[:, 2 * RW:3 * RW]
    lora = 2 * HEAD
    wd, ad, gd = (ps[:, 3 * RW:3 * RW + lora], ps[:, 3 * RW + lora:3 * RW + 2 * lora],
                  ps[:, 3 * RW + 2 * lora:SHIFT_PAD])
    logw = -DECAY_SCALE * _sigmoid(_dot1(jnp.tanh(wd), w2bd) + w0)
    a = _sigmoid(_dot1(ad, a2bd) + a0)
    g = _dot1(_sigmoid(gd), g2p)
    kkr = k * k_k
    kk = kkr / jnp.maximum(jnp.sqrt(head_sum(kkr * kkr)), NORM_EPS)
    k2 = jnp.concatenate([k, k], axis=1)
    ka2 = jnp.concatenate([k_a, k_a], axis=1)
    kd = k2 * (1.0 + (a - 1.0) * ka2)
    return r, v, kk, logw, a, kd, g


def _post_fn(y2, r, v, kd, g, lnx_g, lnx_b, r_k, hsum):
    head_sum = lambda z: _head_sum(z, hsum)
    y = y2[:, 0:RW] + y2[:, RW:2 * RW]
    mu = head_sum(y) * (1.0 / HEAD)
    yc = y - mu
    var = head_sum(yc * yc) * (1.0 / HEAD)
    yn = yc * lax.rsqrt(var + GN_EPS) * lnx_g + lnx_b
    bonus = head_sum(r * (kd[:, 0:RW] + kd[:, RW:2 * RW]) * r_k) * v
    return ((yn + bonus) * g,)


def _glu_fn(pa, pb):
    return (pa * _sigmoid(pb),)


def _conv_out_fn(cv, ln_g, ln_b):
    mu = jnp.mean(cv, axis=-1, keepdims=True)
    cc = cv - mu
    var = jnp.mean(cc * cc, axis=-1, keepdims=True)
    y = cc * lax.rsqrt(var + LN_EPS) * ln_g + ln_b
    return (y * _sigmoid(y),)


def _local_step(x, tgt, w, ex=None):
    bsz, t, d = x.shape
    n = bsz * t
    x2d, tgt2d = x.reshape(n, d), tgt.reshape(n, d)
    hsum = jnp.tile(jnp.kron(jnp.eye(N_HEADS, dtype=BF16), jnp.ones((HEAD, HEAD), BF16)), (2, 1))
    w = dict(w)
    parts = {} if ex else None

    def hosted(result, finish=None):
        if not ex:
            return result
        outs, got = result
        if finish is not None:
            w.update(finish(got))
        return outs

    xb = x2d.astype(BF16)
    hg1, hu1, act1 = hosted(_ffn_in("ffn1_in", xb, w["ffn1_w_in"], exch=(ex["g1"][0], True) if ex else None),
                            ex["g1"][1] if ex else None)
    x1, x1b, z1 = _mm_ln("ffn1_out_ln1", act1, w["ffn1_w_out"], x2d, w["ln1_g"], w["ln1_b"], 0.5, a_blocked=True)
    p = _mm("w_in_proj", x1b, w["w_in"], "nn", F32)
    p3 = p.reshape(bsz, t, IN_PAD)
    ps = _shift("shift_fwd", p3, w["mu_prev"], w["mu_next"]).reshape(n, SHIFT_PAD)
    prep_params = [w["w0"], w["w2"], w["a0"], w["a2"], w["g2"], w["k_k"], w["k_a"], hsum]
    r, v, kk, logw, a, kd, g = _rowwise(
        "rwkv_prep", _prep_fn, [ps], prep_params,
        [(RW, F32), (RW, F32), (RW, F32), (2 * RW, F32), (2 * RW, F32), (2 * RW, F32), (RW, F32)], [])
    y2, s0, tm = hosted(_scan_fwd(r, v, kk, logw, kd, a, bsz, exch=(ex["g2"][0], True) if ex else None),
                    ex["g2"][1] if ex else None)
    post_params = [w["lnx_g"], w["lnx_b"], w["r_k"], hsum]
    (y_rwkv,) = _rowwise("rwkv_post", _post_fn, [y2, r, v, kd, g], post_params, [(RW, BF16)], [])
    (u,) = _rowwise("conv_glu", _glu_fn, [(p, CW, 4), (p, CW, 5)], [], [(CW, F32)], [])
    cv = _dwconv("conv_dw", u.reshape(bsz, t, CW), w["conv_dw"], w["conv_b"], False).reshape(n, CW)
    (y_conv,) = _rowwise("conv_out", _conv_out_fn, [cv], [w["conv_ln_g"], w["conv_ln_b"]], [(CW, BF16)], [])
    ycat = jnp.concatenate([y_rwkv, y_conv], axis=1)
    x2, x2b, z2 = _mm_ln("w_out_ln2", ycat, w["w_out"], x1, w["ln2_g"], w["ln2_b"], 1.0)
    hg2, hu2, act2 = _ffn_in("ffn2_in", x2b, w["ffn2_w_in"])
    gr = {}
    dz3, dz3b, loss, gr["ln3_g"], gr["ln3_b"] = _mm_ln(
        "ffn2_out_ln3", act2, w["ffn2_w_out"], x2, w["ln3_g"], w["ln3_b"], 0.5, tgt=tgt2d, a_blocked=True)

    dx2, _ = _ffn_bwd("ffn2", gr, dz3, dz3b, x2b, w["ffn2_w_in"], w["ffn2_w_out"], hg2, hu2, act2)
    dz2, dz2b, gr["ln2_g"], gr["ln2_b"] = _ln_bwd("ln2_bwd", z2, w["ln2_g"], dx2)
    gr["w_out"] = _mm("w_out_wgrad", ycat, dz2b, "tn", BF16)
    dycat = _mm("w_out_dgrad", dz2b, w["w_out"], "nt", F32)
    conv_out_bwd = _vjp_of(_conv_out_fn, 3)
    dcv, gr["conv_ln_g"], gr["conv_ln_b"] = _rowwise(
        "conv_out_bwd", lambda cv_, ct_, g_, b_: conv_out_bwd(cv_, g_, b_, ct_),
        [cv, (dycat, CW, 1)], [w["conv_ln_g"], w["conv_ln_b"]], [(CW, F32)], [(1, CW), (1, CW)])
    dcv3 = dcv.reshape(bsz, t, CW)
    gr["conv_dw"], gr["conv_b"] = _dwconv_dw("conv_dw_wgrad", u.reshape(bsz, t, CW), dcv3)
    du = _dwconv("conv_dw_dgrad", dcv3, w["conv_dw"], jnp.zeros((1, CW), F32), True).reshape(n, CW)

    def glu_bwd(pa, pb, ct):
        return (jnp.concatenate(_vjp_of(_glu_fn, 2)(pa, pb, ct), axis=1),)

    (dp_conv,) = _rowwise("conv_glu_bwd", glu_bwd, [(p, CW, 4), (p, CW, 5), du], [], [(2 * CW, BF16)], [])

    def post_bwd(y2_, r_, v_, kd_, g_, ct, lg, lb, rk, hs):
        return _vjp_of(lambda *z: _post_fn(*z, hs), 8)(y2_, r_, v_, kd_, g_, lg, lb, rk, ct)

    dy2, dr_post, dv_post, dkd_post, dg, gr["lnx_g"], gr["lnx_b"], gr["r_k"] = _rowwise(
        "rwkv_post_bwd", post_bwd, [y2, r, v, kd, g, (dycat, RW, 0)], post_params,
        [(2 * RW, F32), (RW, F32), (RW, F32), (2 * RW, F32), (RW, F32)], [(1, RW), (1, RW), (1, RW)])
    sends = [jnp.concatenate(gr["ffn2_w_in"], axis=0), gr["ffn2_w_out"].reshape(N_DEV, D_FF // N_DEV, D_MODEL),
             gr["w_out"].reshape(N_DEV, D_MODEL // N_DEV, D_MODEL)]
    res = _scan_bwd(r, v, kk, logw, kd, a, s0, tm, dy2, bsz, exch=(sends, False) if ex else None)
    if ex:
        res, got = res
        parts.update(zip(("ffn2_w_in", "ffn2_w_out", "w_out"), got))
    dr_s, dv_s, dkk_s, dlw, dkd_s, da = res

    def prep_bwd(ps_, dr2, dr1, dv2, dv1, dkk2, dlw_, da_, dkd2, dkd1, dg_, *prm):
        half = lambda z: z[:, 0:RW] + z[:, RW:2 * RW]
        return _vjp_of(lambda *z: _prep_fn(*z, prm[-1]), 8)(
            ps_, *prm[:-1], half(dr2) + dr1, half(dv2) + dv1, half(dkk2), dlw_, da_, dkd2 + dkd1, dg_)

    dps, gr["w0"], gr["w2"], gr["a0"], gr["a2"], gr["g2"], gr["k_k"], gr["k_a"] = _rowwise(
        "rwkv_prep_bwd", prep_bwd,
        [ps, dr_s, dr_post, dv_s, dv_post, dkk_s, dlw, da, dkd_s, dkd_post, dg], prep_params,
        [(SHIFT_PAD, F32)], [q.shape for q in prep_params[:-1]])
    dps3 = dps.reshape(bsz, t, SHIFT_PAD)
    dp_shift, gr["mu_prev"], gr["mu_next"] = _shift_bwd("shift_bwd", dps3, p3, w["mu_prev"], w["mu_next"])
    dp_shift = dp_shift.reshape(n, SHIFT_PAD)
    dp = jnp.concatenate([dp_shift, dp_conv], axis=1)
    gr["w_in"] = _mm("w_in_wgrad", x1b, dp, "tn", BF16)
    dx1 = _mm("w_in_dgrad", dp, w["w_in"], "nt", F32, add=dz2, add_scale=ALPHA)
    dz1, dz1b, gr["ln1_g"], gr["ln1_b"] = _ln_bwd("ln1_bwd", z1, w["ln1_g"], dx1)
    riders = None
    if ex:
        gw_in = _unpad_in_cols(gr["w_in"]).reshape(D_MODEL, N_DEV, IN_COLS // N_DEV).transpose(1, 0, 2)
        small = _pack([_grad_small(nm, gr[nm]) for nm in SMALL_SHARDED + SMALL_REPL])
        riders = {"out_bwd": [gw_in, jnp.broadcast_to(small[None], (N_DEV,) + small.shape)], "in_wgrad": []}
    grad_x, got = _ffn_bwd("ffn1", gr, dz1, dz1b, xb, w["ffn1_w_in"], w["ffn1_w_out"], hg1, hu1, act1, riders)
    if ex:
        parts.update(w_in=got["out_bwd"][0], small=got["out_bwd"][1], ffn1_w_out=got["in_wgrad"][0],
                     ffn1_w_in=got["in_dgrad"][0])
    return loss[0, 0], grad_x.reshape(bsz, t, d), gr, parts


def _ffn_in_wgrad(name, xin, dhg, dhu, tk=1024, exch=None):
    n = xin.shape[0]
    nj, nt = N_DEV // 2, n // tk

    def body(x_ref, g_ref, u_ref, og_ref, ou_ref, accg_ref, accu_ref):
        i = pl.program_id(1)

        @pl.when(i == 0)
        def _():
            accg_ref[...] = jnp.zeros_like(accg_ref)
            accu_ref[...] = jnp.zeros_like(accu_ref)

        xt = x_ref[...].astype(BF16).T
        accg_ref[...] += _dot(xt, g_ref[...])
        accu_ref[...] += _dot(xt, u_ref[...])

        @pl.when(i == nt - 1)
        def _():
            og_ref[...] = accg_ref[...].astype(og_ref.dtype)
            ou_ref[...] = accu_ref[...].astype(ou_ref.dtype)

    dh_blk = pl.BlockSpec((None, tk, FF_BLK), lambda j, i: (j, i, 0))
    o_blk = pl.BlockSpec((None, D_MODEL, FF_BLK), lambda j, i: (j, 0, 0))
    shp = jax.ShapeDtypeStruct((nj, D_MODEL, FF_BLK), BF16)
    return _call(name, body, (nj, nt), [xin, dhg, dhu],
                 [pl.BlockSpec((tk, D_MODEL), lambda j, i: (i, 0)), dh_blk, dh_blk], [o_blk, o_blk], [shp, shp],
                 [pltpu.VMEM((D_MODEL, FF_BLK), F32)] * 2, exch)


def _ffn_in_dgrad(name, dhg, dhu, wg, add, add_scale, tm=512, exch=None):
    n = add.shape[0]
    nj = N_DEV // 2

    def body(g_ref, u_ref, wgate_ref, wup_ref, add_ref, o_ref, acc_ref):
        j = pl.program_id(1)

        @pl.when(j == 0)
        def _():
            acc_ref[...] = jnp.zeros_like(acc_ref)

        acc_ref[...] += _dot(g_ref[...], wgate_ref[...], "nt") + _dot(u_ref[...], wup_ref[...], "nt")

        @pl.when(j == nj - 1)
        def _():
            o_ref[...] = acc_ref[...] + add_scale * add_ref[...]

    dh_blk = pl.BlockSpec((None, tm, FF_BLK), lambda i, j: (j, i, 0))
    row = pl.BlockSpec((tm, D_MODEL), lambda i, j: (i, 0))
    return _call(name, body, (n // tm, nj), [dhg, dhu, wg, wg, add],
                 [dh_blk, dh_blk, pl.BlockSpec((None, D_MODEL, FF_BLK), lambda i, j: (j, 0, 0)),
                  pl.BlockSpec((None, D_MODEL, FF_BLK), lambda i, j: (j + nj, 0, 0)), row],
                 row, jax.ShapeDtypeStruct((n, D_MODEL), F32), [pltpu.VMEM((tm, D_MODEL), F32)], exch)


def _ffn_bwd(tag, gr, dz, dzb, xin, wg, wout, hg, hu, act, riders=None):
    own = riders is not None

    def hosted(result):
        return result if own else (result, None)

    (dhg, dhu, gwo), got_a = hosted(_ffn_out_bwd(
        tag + "_out_bwd", dzb, wout, hg, hu, act, exch=(riders["out_bwd"], False) if own else None))
    gr[tag + "_w_out"] = gwo
    send = ([gwo.reshape(N_DEV, D_FF // N_DEV, D_MODEL)] + riders["in_wgrad"], False) if own else None
    dw, got_b = hosted(_ffn_in_wgrad(tag + "_in_wgrad", xin, dhg, dhu, exch=send))
    gr[tag + "_w_in"] = dw
    send = ([jnp.concatenate(dw, axis=0)], False) if own else None
    dx, got_c = hosted(_ffn_in_dgrad(tag + "_in_dgrad", dhg, dhu, wg, dz, ALPHA, exch=send))
    return dx, {"out_bwd": got_a, "in_wgrad": got_b, "in_dgrad": got_c}


def _adam_math(g, w, m, v):
    m = ADAM_B1 * m + (1.0 - ADAM_B1) * g
    v = ADAM_B2 * v + (1.0 - ADAM_B2) * (g * g)
    m_hat = m / (1.0 - ADAM_B1 ** ADAM_STEP)
    v_hat = v / (1.0 - ADAM_B2 ** ADAM_STEP)
    delta = -ADAM_LR * (m_hat / (jnp.sqrt(v_hat) + ADAM_EPS) + ADAM_WD * w)
    return delta, m, v


def _adam(name, parts, w, m, v, tr=128):
    rows, cols = w.shape
    tr = min(tr, rows)
    while rows % tr:
        tr -= 8

    def body(p_ref, w_ref, m_ref, v_ref, g_ref, d_ref, mo_ref, vo_ref):
        g = p_ref[0].astype(F32)
        for s in range(1, N_DEV):
            g = g + p_ref[s].astype(F32)
        g_ref[...] = g
        d_ref[...], mo_ref[...], vo_ref[...] = _adam_math(g, w_ref[...], m_ref[...], v_ref[...])

    blk = pl.BlockSpec((tr, cols), lambda i: (i, 0))
    shp = jax.ShapeDtypeStruct((rows, cols), F32)
    return _call(name, body, (rows // tr,), [parts, w, m, v],
                 [pl.BlockSpec((N_DEV, tr, cols), lambda i: (0, i, 0)), blk, blk, blk], [blk] * 4, [shp] * 4)


def _sum8(name, parts):
    _, rows, cols = parts.shape

    def body(p_ref, o_ref):
        g = p_ref[0]
        for s in range(1, N_DEV):
            g = g + p_ref[s]
        o_ref[...] = g

    return pl.pallas_call(body, name=name, out_shape=jax.ShapeDtypeStruct((rows, cols), F32),
                          compiler_params=_params())(parts)


def _adam_small(name, g, w, m, v):
    def body(g_ref, w_ref, m_ref, v_ref, d_ref, mo_ref, vo_ref):
        d_ref[...], mo_ref[...], vo_ref[...] = _adam_math(g_ref[...], w_ref[...], m_ref[...], v_ref[...])

    shp = jax.ShapeDtypeStruct(g.shape, F32)
    return pl.pallas_call(body, name=name, out_shape=[shp] * 3, compiler_params=_params())(g, w, m, v)


def _pack(arrs, lane=128):
    flat = jnp.concatenate([a.reshape(-1).astype(F32) for a in arrs])
    pad = (-flat.shape[0]) % (8 * lane)
    return jnp.pad(flat, (0, pad)).reshape(-1, lane)


def _unpack(packed, shapes):
    flat, out, off = packed.reshape(-1), [], 0
    for s in shapes:
        sz = math.prod(s)
        out.append(flat[off:off + sz].reshape(s))
        off += sz
    return out


def _pad_in_cols(wfull):
    zeros = jnp.zeros((wfull.shape[0], SHIFT_PAD - SHIFT_COLS), wfull.dtype)
    return jnp.concatenate([wfull[:, :SHIFT_COLS], zeros, wfull[:, SHIFT_COLS:]], axis=1)


def _unpad_in_cols(gfull):
    return jnp.concatenate([gfull[:, :SHIFT_COLS], gfull[:, SHIFT_PAD:]], axis=1)


def _block_diag2(wd):
    z = jnp.zeros_like(wd[0])
    return jnp.concatenate([jnp.concatenate([wd[0], z], axis=1), jnp.concatenate([z, wd[1]], axis=1)], axis=0)


def _unblock_diag2(g):
    return jnp.stack([g[0:64, 0:RW], g[64:128, RW:2 * RW]])


SMALL_SHARDED = ("w0", "w2", "a0", "a2", "g2", "conv_dw")
SMALL_REPL = ("mu_prev", "mu_next", "k_k", "k_a", "r_k", "lnx_g", "lnx_b", "conv_b", "conv_ln_g", "conv_ln_b",
              "ln1_g", "ln1_b", "ln2_g", "ln2_b", "ln3_g", "ln3_b")
BIG = ("ffn1_w_in", "ffn1_w_out", "w_in", "w_out", "ffn2_w_in", "ffn2_w_out")
WEIGHTS = ("ffn1_w_in", "ffn1_w_out", "w_in", "mu_prev", "mu_next", "w0", "w2", "a0", "a2", "g2", "k_k", "k_a",
           "r_k", "lnx_g", "lnx_b", "conv_dw", "conv_b", "conv_ln_g", "conv_ln_b", "w_out", "ffn2_w_in",
           "ffn2_w_out", "ln1_g", "ln1_b", "ln2_g", "ln2_b", "ln3_g", "ln3_b")


def _full_small(name, full):
    if name in ("w0", "a0"):
        return full.reshape(1, 2 * RW)
    if name in ("w2", "a2"):
        return _block_diag2(full)
    if name == "g2":
        return jnp.pad(full, ((0, 256 - GATE_LORA), (0, 0)))
    if name == "conv_dw":
        return jnp.pad(full, ((0, 1), (0, 0)))
    if name in ("mu_prev", "mu_next"):
        return jnp.pad(full.reshape(1, SHIFT_COLS), ((0, 0), (0, SHIFT_PAD - SHIFT_COLS)))
    return full.reshape(1, -1)


def _grad_small(name, g):
    if name in ("w0", "a0"):
        return g.reshape(2, RW)
    if name in ("w2", "a2"):
        return _unblock_diag2(g)
    if name == "g2":
        return g[:GATE_LORA]
    if name == "conv_dw":
        return g[:CONV_K]
    if name in ("mu_prev", "mu_next"):
        return g[0, :SHIFT_COLS]
    if name == "r_k":
        return g.reshape(N_HEADS, HEAD)
    return g.reshape(-1)


def kernel(x, ffn1_w_in, ffn1_w_out, w_in, mu_prev, mu_next, w0, w2, a0, a2, g2, k_k, k_a, r_k, lnx_g, lnx_b, conv_dw, conv_b, conv_ln_g, conv_ln_b, w_out, ffn2_w_in, ffn2_w_out, ln1_g, ln1_b, ln2_g, ln2_b, ln3_g, ln3_b, loss_target, m_ffn1_w_in, m_ffn1_w_out, m_w_in, m_mu_prev, m_mu_next, m_w0, m_w2, m_a0, m_a2, m_g2, m_k_k, m_k_a, m_r_k, m_lnx_g, m_lnx_b, m_conv_dw, m_conv_b, m_conv_ln_g, m_conv_ln_b, m_w_out, m_ffn2_w_in, m_ffn2_w_out, m_ln1_g, m_ln1_b, m_ln2_g, m_ln2_b, m_ln3_g, m_ln3_b, v_ffn1_w_in, v_ffn1_w_out, v_w_in, v_mu_prev, v_mu_next, v_w0, v_w2, v_a0, v_a2, v_g2, v_k_k, v_k_a, v_r_k, v_lnx_g, v_lnx_b, v_conv_dw, v_conv_b, v_conv_ln_g, v_conv_ln_b, v_w_out, v_ffn2_w_in, v_ffn2_w_out, v_ln1_g, v_ln1_b, v_ln2_g, v_ln2_b, v_ln3_g, v_ln3_b):
    args = dict(locals())
    drop = lambda z: z.reshape(z.shape[1:])
    wsh = {n: drop(args[n]) for n in WEIGHTS}
    msh = {n: drop(args["m_" + n]) for n in WEIGHTS}
    vsh = {n: drop(args["v_" + n]) for n in WEIGHTS}
    me = 4 * lax.axis_index("x") + 2 * lax.axis_index("y") + lax.axis_index("c")
    bf = {n: wsh[n].astype(BF16) for n in BIG}

    w = {"ffn1_w_in": _gather_two_level("gather_ffn1_w_in", bf["ffn1_w_in"])}
    for n in SMALL_REPL:
        w[n] = _full_small(n, wsh[n])
    small_shapes = [wsh[n].shape for n in SMALL_SHARDED]

    def finish1(got):
        f1_out, w_in_g, small = got
        cols = zip(*[_unpack(small[dv], small_shapes) for dv in range(N_DEV)])
        out = {n: _full_small(n, jnp.concatenate(s, axis=-1)) for n, s in zip(SMALL_SHARDED, cols)}
        out["ffn1_w_out"] = f1_out.reshape(D_FF, D_MODEL)
        out["w_in"] = _pad_in_cols(w_in_g.transpose(1, 0, 2).reshape(D_MODEL, IN_COLS))
        return out

    def finish2(got):
        w_out_g, f2_in, f2_out = got
        return {"w_out": w_out_g.reshape(D_MODEL, D_MODEL), "ffn2_w_in": f2_in,
                "ffn2_w_out": f2_out.reshape(D_FF, D_MODEL)}

    ex = {"g1": ([bf["ffn1_w_out"], bf["w_in"], _pack([wsh[n] for n in SMALL_SHARDED])], finish1),
          "g2": ([bf["w_out"], bf["ffn2_w_in"], bf["ffn2_w_out"]], finish2)}
    loss_part, grad_x, gr, parts = _local_step(x, loss_target, w, ex)
    loss = lax.psum(loss_part, ("x", "y", "c"))

    out = {n: _adam("adam_" + n, parts[n], wsh[n], msh[n], vsh[n]) for n in BIG}
    small_names = SMALL_SHARDED + SMALL_REPL
    full_shapes = [_grad_small(n, gr[n]).shape for n in small_names]
    summed = _unpack(_sum8("sum_small_grads", parts["small"]), full_shapes)
    mine = []
    for n, g in zip(small_names, summed):
        if n in SMALL_SHARDED:
            g = lax.dynamic_slice_in_dim(g, me * HEAD, HEAD, axis=g.ndim - 1)
        mine.append(g)
    shapes = [g.shape for g in mine]
    d_s, m_s, v_s = _adam_small("adam_small", _pack(mine), _pack([wsh[n] for n in small_names]),
                                _pack([msh[n] for n in small_names]), _pack([vsh[n] for n in small_names]))
    for n, g, dl, mn, vn in zip(small_names, mine, _unpack(d_s, shapes), _unpack(m_s, shapes), _unpack(v_s, shapes)):
        out[n] = (g, dl, mn, vn)

    res = [loss, grad_x]
    for k in range(4):
        res += [out[n][k].reshape((1,) + out[n][k].shape) for n in WEIGHTS]
    return tuple(res)
```

```python
import functools
import math

import jax
import jax.numpy as jnp
from jax import lax
from jax.experimental import pallas as pl
from jax.experimental.pallas import tpu as pltpu

F32 = jnp.float32
BF16 = jnp.bfloat16

N_DEV = 8
D_MODEL = 1024
RW = 512
N_HEADS = 8
HEAD = 64
CW = 512
CONV_K = 31
D_FF = 2816
FF_BLK = 704
GATE_LORA = 160
SHIFT_COLS = 1952
SHIFT_PAD = 2048
IN_COLS = 2976
IN_PAD = 3072
LN_EPS = 1e-5
GN_EPS = 64e-5
NORM_EPS = 1e-12
ALPHA = 2.0 ** 0.25
DECAY_SCALE = math.exp(-0.5)
CHUNK = 64
ADAM_LR, ADAM_B1, ADAM_B2, ADAM_EPS, ADAM_WD, ADAM_STEP = 0.001, 0.9, 0.999, 1e-8, 0.01, 10
VMEM_LIMIT = 56 * 1024 * 1024
MXU_DIM = 256

_DN = {"nn": (((1,), (0,)), ((), ())), "nt": (((1,), (1,)), ((), ())), "tn": (((0,), (0,)), ((), ()))}


def _params():
    return pltpu.CompilerParams(vmem_limit_bytes=VMEM_LIMIT)


def _dot(a, b, dims="nn"):
    return lax.dot_general(a, b, _DN[dims], preferred_element_type=F32)


def _split(x):
    hi = x.astype(BF16)
    return hi, (x - hi.astype(F32)).astype(BF16)


def _dot3_impl(a, b, dims):
    ah, al = _split(a)
    bh, bl = _split(b)
    ka, kb = _DN[dims][0][0][0], _DN[dims][0][1][0]
    return _dot(jnp.concatenate([ah, ah, al], axis=ka), jnp.concatenate([bh, bl, bh], axis=kb), dims)


@functools.partial(jax.custom_vjp, nondiff_argnums=(2,))
def _dot3(a, b, dims="nn"):
    return _dot3_impl(a, b, dims)


def _dot3_fwd(a, b, dims):
    return _dot3_impl(a, b, dims), (a, b)


def _dot2_ct(x, y, dims, ct_left):
    ka, kb = _DN[dims][0][0][0], _DN[dims][0][1][0]
    if ct_left:
        c, (h, l) = x.astype(BF16), _split(y)
        return _dot(jnp.concatenate([c, c], axis=ka), jnp.concatenate([h, l], axis=kb), dims)
    (h, l), c = _split(x), y.astype(BF16)
    return _dot(jnp.concatenate([h, l], axis=ka), jnp.concatenate([c, c], axis=kb), dims)


def _dot3_bwd(dims, res, ct):
    a, b = res
    if dims == "nn":
        return _dot2_ct(ct, b, "nt", True), _dot2_ct(a, ct, "tn", False)
    if dims == "nt":
        return _dot2_ct(ct, b, "nn", True), _dot2_ct(ct, a, "tn", True)
    return _dot2_ct(b, ct, "nt", False), _dot2_ct(a, ct, "nn", False)


_dot3.defvjp(_dot3_fwd, _dot3_bwd)


def _dot1_impl(a, b, dims):
    return _dot(a.astype(BF16), b.astype(BF16), dims)


@functools.partial(jax.custom_vjp, nondiff_argnums=(2,))
def _dot1(a, b, dims="nn"):
    return _dot1_impl(a, b, dims)


def _dot1_bwd(dims, res, ct):
    a, b = res
    if dims == "nn":
        return _dot1_impl(ct, b, "nt"), _dot1_impl(a, ct, "tn")
    if dims == "nt":
        return _dot1_impl(ct, b, "nn"), _dot1_impl(ct, a, "tn")
    return _dot1_impl(b, ct, "nt"), _dot1_impl(a, ct, "nn")


_dot1.defvjp(lambda a, b, dims: (_dot1_impl(a, b, dims), (a, b)), _dot1_bwd)


def _dot1_two(a, b, dims="nn"):
    ax_a, ax_b = {"nn": (0, 1), "nt": (0, 0), "tn": (1, 1)}[dims]
    shallow = dims != "tn" and 2 * a[0].shape[1] <= MXU_DIM
    out = []
    for i in range(0, len(a), 2):
        if shallow:
            z = jnp.zeros_like(b[i])
            bd = jnp.concatenate([jnp.concatenate([b[i], z], axis=1), jnp.concatenate([z, b[i + 1]], axis=1)], axis=0)
            r = _dot1(jnp.concatenate(a[i:i + 2], axis=1), bd, dims)
            n = r.shape[1] // 2
            out += [r[:, :n], r[:, n:]]
        else:
            r = _dot1(jnp.concatenate(a[i:i + 2], axis=ax_a), jnp.concatenate(b[i:i + 2], axis=ax_b), dims)
            m, n = r.shape[0] // 2, r.shape[1] // 2
            out += [r[:m, :n], r[m:, n:]]
    return out


def _tri_inv_impl(l, eye):
    steps = int(math.log2(CHUNK)) - 1
    m = l[0].shape[0]
    tm = [eye + x for x in l]
    lp = _dot1_two(l, l)
    for k in range(steps):
        if k < steps - 1:
            both = _dot1_two([jnp.concatenate([t, p], axis=0) for t, p in zip(tm, lp)], lp)
            tm = [t + b[:m] for t, b in zip(tm, both)]
            lp = [b[m:] for b in both]
        else:
            tm = [t + x for t, x in zip(tm, _dot1_two(tm, lp))]
    return tm


@jax.custom_vjp
def _tri_inv(l, eye):
    return _tri_inv_impl(l, eye)


def _tri_inv_fwd(l, eye):
    tm = _tri_inv_impl(l, eye)
    return tm, (tm, eye)


def _tri_inv_bwd(res, ct):
    tm, eye = res
    return _dot1_two(_dot1_two(tm, ct, "tn"), tm, "nt"), jnp.zeros_like(eye)


_tri_inv.defvjp(_tri_inv_fwd, _tri_inv_bwd)


@jax.custom_vjp
def _tri_inv_known(l, tm):
    return tm


_tri_inv_known.defvjp(lambda l, tm: (tm, tm),
                      lambda tm, ct: (_dot1_two(_dot1_two(tm, ct, "tn"), tm, "nt"), [jnp.zeros_like(t) for t in tm]))


def _ones_impl(x, g2):
    x1 = x.astype(BF16)
    x2 = (x - x1.astype(F32)).astype(BF16)
    return _dot(jnp.concatenate([x1, x2], axis=1), g2)


@jax.custom_vjp
def _head_sum(x, g2):
    return _ones_impl(x, g2)


_head_sum.defvjp(lambda x, g2: (_ones_impl(x, g2), g2), lambda g2, ct: (_ones_impl(ct, g2), jnp.zeros_like(g2)))


def _prefix_sum(x):
    row = lax.broadcasted_iota(jnp.int32, x.shape, 0)
    sh = 1
    while sh < x.shape[0]:
        x = x + jnp.where(row >= sh, pltpu.roll(x, sh, 0), 0.0)
        sh *= 2
    return x


def _dir_cumsum_impl(x, sgn):
    pre = _prefix_sum(x)
    return jnp.where(sgn > 0.0, pre, jnp.sum(x, axis=0, keepdims=True) - pre + x)


@jax.custom_vjp
def _dir_cumsum(x, sgn):
    return _dir_cumsum_impl(x, sgn)


_dir_cumsum.defvjp(lambda x, sgn: (_dir_cumsum_impl(x, sgn), sgn),
                   lambda sgn, ct: (_dir_cumsum_impl(ct, -sgn), jnp.zeros_like(sgn)))


def _sigmoid(x):
    return 1.0 / (1.0 + jnp.exp(-x))


def _mesh_pos():
    return lax.axis_index("x"), lax.axis_index("y"), lax.axis_index("c")


def _peer(pos, q):
    x, y, c = pos
    return (1 - x if q & 4 else x, 1 - y if q & 2 else y, 1 - c if q & 1 else c)


def _linear(pos):
    return 4 * pos[0] + 2 * pos[1] + pos[2]


def _exchange_copies(x_refs, o_refs, send_sems, recv_sems, local_sems, gather):
    pos = _mesh_pos()
    me = _linear(pos)
    starts, wait_recv, wait_send, wait_local = [], [], [], []
    for t in range(len(x_refs)):
        src = x_refs[t] if gather else x_refs[t].at[me]
        cp = pltpu.make_async_copy(src, o_refs[t].at[me], local_sems.at[t])
        starts.append(cp.start)
        wait_local.append(cp.wait)
    for q in range(1, N_DEV):
        peer = _peer(pos, q)
        for t in range(len(x_refs)):
            src = x_refs[t] if gather else x_refs[t].at[_linear(peer)]
            sems = dict(send_sem=send_sems.at[t, q - 1], recv_sem=recv_sems.at[t, q - 1],
                        device_id=peer, device_id_type=pl.DeviceIdType.MESH)
            send = pltpu.make_async_remote_copy(src_ref=src, dst_ref=o_refs[t].at[me], **sems)
            recv = pltpu.make_async_remote_copy(src_ref=src, dst_ref=o_refs[t].at[_linear(peer)], **sems)
            starts.append(send.start)
            wait_recv.append(recv.wait_recv)
            wait_send.append(send.wait_send)
    return starts, wait_recv + wait_send + wait_local


def _exchange_shapes(xs, gather):
    return [jax.ShapeDtypeStruct((N_DEV,) + (x.shape if gather else x.shape[1:]), x.dtype) for x in xs]


def _exchange_sems(nt):
    return [pltpu.SemaphoreType.DMA((nt, N_DEV - 1)), pltpu.SemaphoreType.DMA((nt, N_DEV - 1)),
            pltpu.SemaphoreType.DMA((nt,))]


def _gather_two_level(name, x):
    def body(x_ref, out_ref, send_sems, recv_sems, local_sem):
        px, py, pc = _mesh_pos()
        me, sibling = (px, py, pc), (px, py, 1 - pc)
        chips = [(1 - px, py), (px, 1 - py), (1 - px, 1 - py)]

        def slot(pos):
            return out_ref.at[_linear(pos)]

        def copy(k, block, to, src=None):
            return pltpu.make_async_remote_copy(
                src_ref=slot(block) if src is None else src, dst_ref=slot(block), send_sem=send_sems.at[k],
                recv_sem=recv_sems.at[k], device_id=to, device_id_type=pl.DeviceIdType.MESH)

        mine = pltpu.make_async_copy(x_ref, slot(me), local_sem)
        mine.start()
        first = [copy(0, me, sibling, src=x_ref)]
        first += [copy(1 + j, me, (*chip, pc), src=x_ref) for j, chip in enumerate(chips)]
        for cp in first:
            cp.start()
        passed = [copy(4 + j, (*chip, pc), sibling) for j, chip in enumerate(chips)]
        for j, chip in enumerate(chips):
            copy(1 + j, (*chip, pc), me).wait_recv()
            passed[j].start()
        copy(0, sibling, me).wait_recv()
        for j, chip in enumerate(chips):
            copy(4 + j, (*chip, 1 - pc), me).wait_recv()
        for cp in first + passed:
            cp.wait_send()
        mine.wait()

    any_spec = pl.BlockSpec(memory_space=pl.ANY)
    return pl.pallas_call(
        body, name=name, in_specs=[any_spec], out_specs=any_spec,
        out_shape=jax.ShapeDtypeStruct((N_DEV,) + x.shape, x.dtype),
        scratch_shapes=[pltpu.SemaphoreType.DMA((N_DEV - 1,)), pltpu.SemaphoreType.DMA((N_DEV - 1,)),
                        pltpu.SemaphoreType.DMA])(x)


def _call(name, body, grid, ins, in_specs, out_specs, out_shape, scratch=(), exch=None):
    if exch is None:
        return pl.pallas_call(body, name=name, grid=grid, in_specs=in_specs, out_specs=out_specs,
                              out_shape=out_shape, scratch_shapes=list(scratch), compiler_params=_params())(*ins)
    xs, gather = exch
    single = not isinstance(out_shape, (list, tuple))
    o_specs = [out_specs] if single else list(out_specs)
    o_shape = [out_shape] if single else list(out_shape)
    n_in, n_out, n_x, n_scr = len(ins), len(o_shape), len(xs), len(scratch)

    def wrapped(*refs):
        in_refs = refs[:n_in]
        x_refs = refs[n_in:n_in + n_x]
        out_refs = refs[n_in + n_x:n_in + n_x + n_out]
        got_refs = refs[n_in + n_x + n_out:n_in + 2 * n_x + n_out]
        rest = refs[n_in + 2 * n_x + n_out:]
        starts, waits = _exchange_copies(x_refs, got_refs, *rest[n_scr:], gather)
        ids = [pl.program_id(i) for i in range(len(grid))]
        first = functools.reduce(lambda p, q: p & q, [i == 0 for i in ids])
        last = functools.reduce(lambda p, q: p & q, [i == g - 1 for i, g in zip(ids, grid)])

        @pl.when(first)
        def _():
            for f in starts:
                f()

        body(*in_refs, *out_refs, *rest[:n_scr])

        @pl.when(last)
        def _():
            for f in waits:
                f()

    any_spec = pl.BlockSpec(memory_space=pl.ANY)
    outs = pl.pallas_call(
        wrapped, name=name, grid=grid, in_specs=list(in_specs) + [any_spec] * n_x,
        out_specs=o_specs + [any_spec] * n_x, out_shape=o_shape + _exchange_shapes(xs, gather),
        scratch_shapes=list(scratch) + _exchange_sems(n_x), compiler_params=_params())(*ins, *xs)
    res = outs[:n_out]
    return (res[0] if single else res), outs[n_out:]


def _mm_call(name, dims, grid, red_axis, ins, in_specs, out_shape, out_spec, acc_shape,
             scale=1.0, add_scale=None, exch=None):
    nred = grid[red_axis]

    def body(*refs):
        if add_scale is None:
            a_ref, b_ref, o_ref, acc_ref = refs
            add_ref = None
        else:
            a_ref, b_ref, add_ref, o_ref, acc_ref = refs
        k = pl.program_id(red_axis)

        @pl.when(k == 0)
        def _():
            acc_ref[...] = jnp.zeros_like(acc_ref)

        acc_ref[...] += _dot(a_ref[...].astype(BF16), b_ref[...].astype(BF16), dims)

        @pl.when(k == nred - 1)
        def _():
            r = acc_ref[...]
            if scale != 1.0:
                r = r * scale
            if add_ref is not None:
                r = r + add_scale * add_ref[...].astype(F32)
            o_ref[...] = r.astype(o_ref.dtype)

    return _call(name, body, grid, ins, in_specs, out_spec, out_shape, [pltpu.VMEM(acc_shape, F32)], exch)


def _tile(dim, cap):
    t = min(dim, cap)
    while dim % t or t % 128:
        t -= 128
        assert t > 0, (dim, cap)
    return t


def _mm(name, a, b, dims, out_dtype, scale=1.0, add=None, add_scale=None, tm=512, tn=1024, tk=1024):
    if dims == "tn":
        kd, m = a.shape
        n = b.shape[1]
    else:
        m, kd = a.shape
        n = b.shape[1] if dims == "nn" else b.shape[0]
    tm, tn, tk = _tile(m, tm), _tile(n, tn), _tile(kd, tk)
    a_spec = (pl.BlockSpec((tk, tm), lambda i, j, k: (k, i)) if dims == "tn"
              else pl.BlockSpec((tm, tk), lambda i, j, k: (i, k)))
    b_spec = (pl.BlockSpec((tn, tk), lambda i, j, k: (j, k)) if dims == "nt"
              else pl.BlockSpec((tk, tn), lambda i, j, k: (k, j)))
    o_spec = pl.BlockSpec((tm, tn), lambda i, j, k: (i, j))
    ins, specs = [a, b], [a_spec, b_spec]
    if add is not None:
        ins.append(add)
        specs.append(o_spec)
    return _mm_call(name, dims, (m // tm, n // tn, kd // tk), 2, ins, specs,
                    jax.ShapeDtypeStruct((m, n), out_dtype), o_spec, (tm, tn),
                    scale=scale, add_scale=add_scale if add is not None else None)


def _ffn_in(name, x, wg, tm=512, exch=None):
    n = x.shape[0]
    nj = N_DEV // 2

    def body(x_ref, wgate_ref, wup_ref, hg_ref, hu_ref, act_ref):
        xb = x_ref[...].astype(BF16)
        g = _dot(xb, wgate_ref[...])
        u = _dot(xb, wup_ref[...])
        hg_ref[...] = g.astype(BF16)
        hu_ref[...] = u.astype(BF16)
        act_ref[...] = (g * _sigmoid(g) * u).astype(BF16)

    blk = pl.BlockSpec((None, tm, FF_BLK), lambda j, i: (j, i, 0))
    shp = jax.ShapeDtypeStruct((nj, n, FF_BLK), BF16)
    return _call(name, body, (nj, n // tm), [x, wg, wg],
                 [pl.BlockSpec((tm, D_MODEL), lambda j, i: (i, 0)),
                  pl.BlockSpec((None, D_MODEL, FF_BLK), lambda j, i: (j, 0, 0)),
                  pl.BlockSpec((None, D_MODEL, FF_BLK), lambda j, i: (j + nj, 0, 0))],
                 [blk, blk, blk], [shp, shp, shp], exch=exch)


def _ffn_out_bwd(name, dz, wout, hg, hu, act, tm=512, exch=None):
    n = dz.shape[0]
    nj, ni = N_DEV // 2, n // tm

    def body(dz_ref, w_ref, hg_ref, hu_ref, act_ref, dhg_ref, dhu_ref, dw_ref, acc_ref):
        i = pl.program_id(1)
        dzb = dz_ref[...].astype(BF16)
        dact = 0.5 * _dot(dzb, w_ref[...], "nt")
        g = hg_ref[...].astype(F32)
        u = hu_ref[...].astype(F32)
        s = _sigmoid(g)
        dhg_ref[...] = (dact * u * (s * (1.0 + g * (1.0 - s)))).astype(BF16)
        dhu_ref[...] = (dact * (g * s)).astype(BF16)

        @pl.when(i == 0)
        def _():
            acc_ref[...] = jnp.zeros_like(acc_ref)

        acc_ref[...] += _dot(act_ref[...], dzb, "tn")

        @pl.when(i == ni - 1)
        def _():
            dw_ref[...] = (0.5 * acc_ref[...]).astype(dw_ref.dtype)

    blk = pl.BlockSpec((None, tm, FF_BLK), lambda j, i: (j, i, 0))
    wblk = pl.BlockSpec((FF_BLK, D_MODEL), lambda j, i: (j, 0))
    shp = jax.ShapeDtypeStruct((nj, n, FF_BLK), BF16)
    return _call(name, body, (nj, ni), [dz, wout, hg, hu, act],
                 [pl.BlockSpec((tm, D_MODEL), lambda j, i: (i, 0)), wblk, blk, blk, blk],
                 [blk, blk, wblk], [shp, shp, jax.ShapeDtypeStruct((D_FF, D_MODEL), BF16)],
                 [pltpu.VMEM((FF_BLK, D_MODEL), F32)], exch=exch)


def _mm_ln(name, a, b, xres, g, beta, c, tgt=None, a_blocked=False, tm=512, tk=512):
    if a_blocked:
        nj, n, kj = a.shape
        nk, kb = nj // 2, 2 * kj
        a_spec = pl.BlockSpec((2, tm, kj), lambda i, k: (k, i, 0))
    else:
        n, kd = a.shape
        kb = _tile(kd, tk)
        nk = kd // kb
        a_spec = pl.BlockSpec((tm, kb), lambda i, k: (i, k))
    d = b.shape[1]
    with_loss = tgt is not None

    def body(*refs):
        if with_loss:
            a_ref, b_ref, x_ref, g_ref, be_ref, t_ref, dz_ref, dzb_ref, l_ref, dg_ref, db_ref, acc_ref = refs
        else:
            a_ref, b_ref, x_ref, g_ref, be_ref, o_ref, ob_ref, z_ref, acc_ref = refs
        i, k = pl.program_id(0), pl.program_id(1)

        @pl.when(k == 0)
        def _():
            acc_ref[...] = jnp.zeros_like(acc_ref)

        if a_blocked:
            acc_ref[...] += (_dot(a_ref[0].astype(BF16), b_ref[0:kb // 2, :].astype(BF16))
                             + _dot(a_ref[1].astype(BF16), b_ref[kb // 2:kb, :].astype(BF16)))
        else:
            acc_ref[...] += _dot(a_ref[...].astype(BF16), b_ref[...].astype(BF16))

        @pl.when(k == nk - 1)
        def _():
            z = ALPHA * x_ref[...] + c * acc_ref[...]
            mu = jnp.mean(z, axis=-1, keepdims=True)
            zc = z - mu
            rstd = lax.rsqrt(jnp.mean(zc * zc, axis=-1, keepdims=True) + LN_EPS)
            xh = zc * rstd
            y = xh * g_ref[...] + be_ref[...]
            if with_loss:
                err = y - t_ref[...]
                ct = err * (1.0 / d)
                dxh = ct * g_ref[...]
                dz = rstd * (dxh - jnp.mean(dxh, axis=-1, keepdims=True)
                             - xh * jnp.mean(dxh * xh, axis=-1, keepdims=True))
                dz_ref[...] = dz
                dzb_ref[...] = dz.astype(BF16)
                part = 0.5 * jnp.sum(jnp.sum(err * err, axis=-1, keepdims=True), axis=0, keepdims=True) * (1.0 / d)

                @pl.when(i == 0)
                def _():
                    l_ref[...] = jnp.zeros_like(l_ref)
                    dg_ref[...] = jnp.zeros_like(dg_ref)
                    db_ref[...] = jnp.zeros_like(db_ref)

                l_ref[...] += jnp.broadcast_to(part, l_ref.shape)
                dg_ref[...] += jnp.sum(ct * xh, axis=0, keepdims=True)
                db_ref[...] += jnp.sum(ct, axis=0, keepdims=True)
            else:
                z_ref[...] = z
                o_ref[...] = y
                ob_ref[...] = y.astype(BF16)

    row = pl.BlockSpec((tm, d), lambda i, k: (i, 0))
    vec = pl.BlockSpec((1, d), lambda i, k: (0, 0))
    ins = [a, b, xres, g, beta]
    in_specs = [a_spec, pl.BlockSpec((kb, d), lambda i, k: (k, 0)), row, vec, vec]
    out_specs = [row, row]
    out_shape = [jax.ShapeDtypeStruct((n, d), F32), jax.ShapeDtypeStruct((n, d), BF16)]
    if with_loss:
        ins.append(tgt)
        in_specs.append(row)
        out_specs += [pl.BlockSpec((1, 128), lambda i, k: (0, 0)), vec, vec]
        out_shape += [jax.ShapeDtypeStruct((1, 128), F32)] + [jax.ShapeDtypeStruct((1, d), F32)] * 2
    else:
        out_specs.append(row)
        out_shape.append(jax.ShapeDtypeStruct((n, d), F32))
    return _call(name, body, (n // tm, nk), ins, in_specs, out_specs, out_shape, [pltpu.VMEM((tm, d), F32)])


def _rowwise(name, fn, rows, params, out_rows, out_accs, tm=256, exch=None):
    specs, ins = [], []
    for r in rows:
        arr, w, cb = r if isinstance(r, tuple) else (r, r.shape[1], 0)
        ins.append(arr)
        specs.append(pl.BlockSpec((tm, w), functools.partial(lambda i, cb: (i, cb), cb=cb)))
    n = ins[0].shape[0]
    for p in params:
        ins.append(p)
        specs.append(pl.BlockSpec(p.shape, lambda i: (0, 0)))
    n_in, n_or = len(ins), len(out_rows)

    def body(*refs):
        outs = fn(*[r[...] for r in refs[:n_in]])
        o_refs = refs[n_in:]
        for o_ref, o in zip(o_refs[:n_or], outs[:n_or]):
            o_ref[...] = o.astype(o_ref.dtype)
        if out_accs:
            @pl.when(pl.program_id(0) == 0)
            def _():
                for a_ref in o_refs[n_or:]:
                    a_ref[...] = jnp.zeros_like(a_ref)

            for a_ref, a in zip(o_refs[n_or:], outs[n_or:]):
                a_ref[...] += a.astype(F32)

    out_specs = [pl.BlockSpec((tm, w), lambda i: (i, 0)) for w, _ in out_rows]
    out_specs += [pl.BlockSpec(s, lambda i: (0, 0)) for s in out_accs]
    out_shape = [jax.ShapeDtypeStruct((n, w), dt) for w, dt in out_rows]
    out_shape += [jax.ShapeDtypeStruct(s, F32) for s in out_accs]
    return _call(name, body, (n // tm,), ins, specs, out_specs, out_shape, exch=exch)


def _vjp_of(fn, n_in):
    def g(*args):
        ins, cts = args[:n_in], args[n_in:]
        outs, pull = jax.vjp(fn, *ins)
        return pull(tuple(c.astype(o.dtype) for c, o in zip(cts, outs)))
    return g


def _ln_bwd(name, z, g, ct):
    def fn(zt, ct_, gt):
        mu = jnp.mean(zt, axis=-1, keepdims=True)
        zc = zt - mu
        rstd = lax.rsqrt(jnp.mean(zc * zc, axis=-1, keepdims=True) + LN_EPS)
        xh = zc * rstd
        dxh = ct_ * gt
        dz = rstd * (dxh - jnp.mean(dxh, axis=-1, keepdims=True)
                     - xh * jnp.mean(dxh * xh, axis=-1, keepdims=True))
        return dz, dz, jnp.sum(ct_ * xh, axis=0, keepdims=True), jnp.sum(ct_, axis=0, keepdims=True)

    d = z.shape[1]
    return _rowwise(name, fn, [z, ct], [g], [(d, F32), (d, BF16)], [(1, d), (1, d)])


SHIFT_TILE = 256


def _shift_specs(t):
    r8 = SHIFT_TILE // 8
    return [pl.BlockSpec((None, SHIFT_TILE, SHIFT_PAD), lambda b, i: (b, i, 0)),
            pl.BlockSpec((None, 8, SHIFT_PAD), lambda b, i: (b, jnp.maximum(i * r8 - 1, 0), 0)),
            pl.BlockSpec((None, 8, SHIFT_PAD), lambda b, i: (b, jnp.minimum((i + 1) * r8, t // 8 - 1), 0))]


def _neighbour_diffs(cur, prev_ref, next_ref, i, nt):
    prow = jnp.where(i > 0, prev_ref[7:8, :], 0.0)
    nrow = jnp.where(i < nt - 1, next_ref[0:1, :], 0.0)
    rid = lax.broadcasted_iota(jnp.int32, cur.shape, 0)
    return (jnp.where(rid == 0, prow, pltpu.roll(cur, 1, 0)) - cur,
            jnp.where(rid == SHIFT_TILE - 1, nrow, pltpu.roll(cur, SHIFT_TILE - 1, 0)) - cur)


def _shift(name, src, mu_prev, mu_next):
    bsz, t, _ = src.shape
    nt, w = t // SHIFT_TILE, SHIFT_PAD

    def body(cur_ref, prev_ref, next_ref, mp_ref, mn_ref, o_ref):
        cur = cur_ref[...]
        dprev, dnext = _neighbour_diffs(cur, prev_ref, next_ref, pl.program_id(1), nt)
        o_ref[...] = cur + mp_ref[...] * dprev + mn_ref[...] * dnext

    specs = _shift_specs(t)
    vec = pl.BlockSpec((1, w), lambda b, i: (0, 0))
    return _call(name, body, (bsz, nt), [src, src, src, mu_prev, mu_next], specs + [vec, vec], specs[0],
                 jax.ShapeDtypeStruct((bsz, t, w), F32))


def _shift_bwd(name, dps, p, mu_prev, mu_next):
    bsz, t, _ = dps.shape
    nt, w = t // SHIFT_TILE, SHIFT_PAD

    def body(d_ref, dprev_ref, dnext_ref, p_ref, pprev_ref, pnext_ref, mp_ref, mn_ref, o_ref, da_ref, db_ref):
        b, i = pl.program_id(0), pl.program_id(1)
        dcur = d_ref[...]
        d_dprev, d_dnext = _neighbour_diffs(dcur, dprev_ref, dnext_ref, i, nt)
        o_ref[...] = (dcur + mn_ref[...] * d_dprev + mp_ref[...] * d_dnext).astype(o_ref.dtype)
        p_dprev, p_dnext = _neighbour_diffs(p_ref[...], pprev_ref, pnext_ref, i, nt)

        @pl.when((b == 0) & (i == 0))
        def _():
            da_ref[...] = jnp.zeros_like(da_ref)
            db_ref[...] = jnp.zeros_like(db_ref)

        da_ref[...] += jnp.sum(dcur * p_dprev, axis=0, keepdims=True)
        db_ref[...] += jnp.sum(dcur * p_dnext, axis=0, keepdims=True)

    specs = _shift_specs(t)
    vec = pl.BlockSpec((1, w), lambda b, i: (0, 0))
    return _call(name, body, (bsz, nt), [dps, dps, dps, p, p, p, mu_prev, mu_next], specs + specs + [vec, vec],
                 [specs[0], vec, vec],
                 [jax.ShapeDtypeStruct((bsz, t, w), BF16), jax.ShapeDtypeStruct((1, w), F32),
                  jax.ShapeDtypeStruct((1, w), F32)])


CONV_BLK = 128


def _halo_specs(t, tt, w):
    r16 = tt // 16
    return [pl.BlockSpec((None, tt, w), lambda b, i: (b, i, 0)),
            pl.BlockSpec((None, 16, w), lambda b, i: (b, jnp.maximum(i * r16 - 1, 0), 0)),
            pl.BlockSpec((None, 16, w), lambda b, i: (b, jnp.minimum((i + 1) * r16, t // 16 - 1), 0))]


def _fill_pad(pad_ref, cur_ref, prev_ref, next_ref, i, nt, tt):
    pad_ref[0:16, :] = jnp.where(i > 0, prev_ref[...], 0.0)
    pad_ref[16:16 + tt, :] = cur_ref[...]
    pad_ref[16 + tt:32 + tt, :] = jnp.where(i < nt - 1, next_ref[...], 0.0)


def _dwconv(name, u, dw32, bias, flip, tt=512):
    bsz, t, w = u.shape
    tt = min(tt, t)
    nt = t // tt

    def body(cur_ref, prev_ref, next_ref, dw_ref, b_ref, o_ref, pad_ref):
        i = pl.program_id(1)
        _fill_pad(pad_ref, cur_ref, prev_ref, next_ref, i, nt, tt)
        for r0 in range(0, tt, CONV_BLK):
            for cs in (slice(c0, c0 + CONV_BLK) for c0 in range(0, w, CONV_BLK)):
                acc = jnp.broadcast_to(b_ref[:, cs], (CONV_BLK, CONV_BLK))
                for k in range(CONV_K):
                    kk = CONV_K - 1 - k if flip else k
                    acc = acc + pad_ref[pl.ds(r0 + 1 + k, CONV_BLK), cs] * dw_ref[kk:kk + 1, cs]
                o_ref[r0:r0 + CONV_BLK, cs] = acc

    return _call(name, body, (bsz, nt), [u, u, u, dw32, bias],
                 _halo_specs(t, tt, w) + [pl.BlockSpec((32, w), lambda b, i: (0, 0)),
                                          pl.BlockSpec((1, w), lambda b, i: (0, 0))],
                 pl.BlockSpec((None, tt, w), lambda b, i: (b, i, 0)), jax.ShapeDtypeStruct((bsz, t, w), F32),
                 [pltpu.VMEM((tt + 32, w), F32)])


def _dwconv_dw(name, u, dc, tt=512):
    bsz, t, w = u.shape
    tt = min(tt, t)
    nt = t // tt

    def body(cur_ref, prev_ref, next_ref, dc_ref, ddw_ref, db_ref, pad_ref):
        b, i = pl.program_id(0), pl.program_id(1)
        _fill_pad(pad_ref, cur_ref, prev_ref, next_ref, i, nt, tt)

        @pl.when((b == 0) & (i == 0))
        def _():
            ddw_ref[...] = jnp.zeros_like(ddw_ref)
            db_ref[...] = jnp.zeros_like(db_ref)

        dcv = dc_ref[...]
        db_ref[...] += jnp.sum(dcv, axis=0, keepdims=True)
        for k in range(CONV_K):
            ddw_ref[k:k + 1, :] += jnp.sum(dcv * pad_ref[pl.ds(1 + k, tt), :], axis=0, keepdims=True)

    return _call(name, body, (bsz, nt), [u, u, u, dc],
                 _halo_specs(t, tt, w) + [pl.BlockSpec((None, tt, w), lambda b, i: (b, i, 0))],
                 [pl.BlockSpec((32, w), lambda b, i: (0, 0)), pl.BlockSpec((1, w), lambda b, i: (0, 0))],
                 [jax.ShapeDtypeStruct((32, w), F32), jax.ShapeDtypeStruct((1, w), F32)],
                 [pltpu.VMEM((tt + 32, w), F32)])


PAIR = 2 * HEAD
N_PAIRS = RW // PAIR


def _chunk_pairs(s, r, lw, k, v, kk, a, sgn, tm_known=None):
    n, m = CHUNK, 2 * CHUNK
    in_a = lax.broadcasted_iota(jnp.int32, (n, PAIR), 1) < HEAD

    def stack2(z):
        return jnp.concatenate([jnp.where(in_a, z, 0.0), jnp.where(in_a, 0.0, z)], axis=0)

    def each(f, *lists):
        return [f(*z) for z in zip(*lists)]

    sgn_f = sgn.astype(F32)
    row2 = lax.broadcasted_iota(jnp.int32, (m, m), 0)
    col2 = lax.broadcasted_iota(jnp.int32, (m, m), 1)
    same = (row2 >= n) == (col2 >= n)
    dlt = ((row2 & (n - 1)) - (col2 & (n - 1))) * sgn
    incl, strict = same & (dlt >= 0), same & (dlt > 0)
    eye = jnp.where(row2 == col2, 1.0, 0.0)

    cum = each(lambda lw_: _dir_cumsum(lw_, sgn_f), lw)
    tot = each(lambda lw_: jnp.sum(lw_, axis=0, keepdims=True), lw)
    e_neg = each(lambda c_: jnp.exp(-c_), cum)
    e_rest = each(lambda t_, c_: jnp.exp(t_ - c_), tot, cum)
    beta = each(lambda kk_, a_: kk_ * a_, kk, a)
    lhs = each(lambda kk_, c_, lw_, r_: jnp.concatenate(
        [stack2(-kk_ * jnp.exp(c_ - lw_)), stack2(r_ * jnp.exp(c_))], axis=0), kk, cum, lw, r)
    rhs = each(lambda b_, k_, e_: jnp.concatenate([stack2(b_ * e_), stack2(k_ * e_)], axis=0), beta, k, e_neg)
    sc = each(lambda l_, r_: _dot3(l_, r_, "nt"), lhs, rhs)
    l_ab = each(lambda sc_: jnp.where(strict, sc_[0:m, 0:m], 0.0), sc)
    l_ak = each(lambda sc_: jnp.where(strict, sc_[0:m, m:2 * m], 0.0), sc)
    m_r = each(lambda sc_: jnp.where(jnp.concatenate([incl, incl], axis=1), sc_[m:2 * m, :], 0.0), sc)
    tm = _tri_inv(l_ab, eye) if tm_known is None else _tri_inv_known(l_ab, tm_known)
    z = _dot1_two(lhs, s, "nt")
    v2 = each(stack2, v)
    u2 = _dot1_two(tm, each(lambda z_, lv_: z_[0:m] + lv_, z, _dot1_two(l_ak, v2)))
    uv = each(lambda u_, v_: jnp.concatenate([u_, v_], axis=0), u2, v2)
    y2 = each(lambda z_, mu_: z_[m:2 * m] + mu_, z, _dot1_two(m_r, uv))
    bk = each(lambda b_, k_, e_: jnp.concatenate([stack2(b_ * e_), stack2(k_ * e_)], axis=0), beta, k, e_rest)
    s_new = each(lambda s_, t_, d_: s_ * jnp.exp(t_) + d_, s, tot, _dot1_two(uv, bk, "tn"))
    return each(lambda y_: y_[0:n] + y_[n:m], y2), s_new, tm


SCAN_SEQS = 4
N_CHAINS = SCAN_SEQS * N_PAIRS


def _pair_tiles(ref):
    return [ref[q, :, p * PAIR:(p + 1) * PAIR] for q in range(SCAN_SEQS) for p in range(N_PAIRS)]


def _store_tiles(ref, tiles):
    for q in range(SCAN_SEQS):
        for p in range(N_PAIRS):
            ref[q, :, p * PAIR:(p + 1) * PAIR] = tiles[q * N_PAIRS + p]


def _scan_specs(order):
    shared = pl.BlockSpec((SCAN_SEQS, CHUNK, RW), lambda d, b, c: (b, order(d, c), 0))
    per_dir = pl.BlockSpec((SCAN_SEQS, CHUNK, RW), lambda d, b, c: (b, order(d, c), d))
    state = pl.BlockSpec((None, SCAN_SEQS, None, N_PAIRS, PAIR, PAIR), lambda d, b, c: (d, b, order(d, c), 0, 0, 0))
    return shared, per_dir, state


def _scan_fwd(r, v, kk, lw, kd, a, bsz, exch=None):
    n = r.shape[0]
    t = n // bsz
    nc = t // CHUNK

    def order(d, c):
        return c + d * (nc - 1 - 2 * c)

    def body(r_ref, v_ref, kk_ref, lw_ref, kd_ref, a_ref, y_ref, s0_ref, tm_ref, s_ref):
        d, c = pl.program_id(0), pl.program_id(2)

        @pl.when(c == 0)
        def _():
            s_ref[...] = jnp.zeros_like(s_ref)

        s = [s_ref[i] for i in range(N_CHAINS)]
        y, s_new, tm = _chunk_pairs(s, *[_pair_tiles(ref) for ref in (r_ref, lw_ref, kd_ref, v_ref, kk_ref, a_ref)],
                                    1 - 2 * d)
        _store_tiles(y_ref, y)
        for i in range(N_CHAINS):
            s0_ref[i // N_PAIRS, i % N_PAIRS] = s[i]
            tm_ref[i // N_PAIRS, i % N_PAIRS] = tm[i].astype(BF16)
            s_ref[i] = s_new[i]

    shared, per_dir, state = _scan_specs(order)
    seq = lambda z: z.reshape(bsz, t, z.shape[1])
    res = _call("scan_fwd", body, (2, bsz // SCAN_SEQS, nc), [seq(z) for z in (r, v, kk, lw, kd, a)],
                [shared, shared, shared, per_dir, per_dir, per_dir], [per_dir, state, state],
                [jax.ShapeDtypeStruct((bsz, t, 2 * RW), F32),
                 jax.ShapeDtypeStruct((2, bsz, nc, N_PAIRS, PAIR, PAIR), F32),
                 jax.ShapeDtypeStruct((2, bsz, nc, N_PAIRS, PAIR, PAIR), BF16)],
                [pltpu.VMEM((N_CHAINS, PAIR, PAIR), F32)], exch)
    (y, s0, tm), got = res if exch else (res, None)
    y = y.reshape(n, 2 * RW)
    return ([y, s0, tm], got) if exch else [y, s0, tm]


def _scan_bwd(r, v, kk, lw, kd, a, s0, tm, dy, bsz, exch=None):
    n = r.shape[0]
    t = n // bsz
    nc = t // CHUNK

    def order(d, c):
        cc = nc - 1 - c
        return cc + d * (nc - 1 - 2 * cc)

    def body(r_ref, v_ref, kk_ref, lw_ref, kd_ref, a_ref, dy_ref, s0_ref, tm_ref,
             dr_ref, dv_ref, dkk_ref, dlw_ref, dkd_ref, da_ref, ds_ref):
        d, c = pl.program_id(0), pl.program_id(2)

        @pl.when(c == 0)
        def _():
            ds_ref[...] = jnp.zeros_like(ds_ref)

        sgn = 1 - 2 * d
        tm_known = [tm_ref[i // N_PAIRS, i % N_PAIRS].astype(F32) for i in range(N_CHAINS)]
        _, pull = jax.vjp(lambda *ops: _chunk_pairs(*ops, sgn, tm_known)[:2],
                          [s0_ref[i // N_PAIRS, i % N_PAIRS] for i in range(N_CHAINS)],
                          *[_pair_tiles(ref) for ref in (r_ref, lw_ref, kd_ref, v_ref, kk_ref, a_ref)])
        grads = pull((_pair_tiles(dy_ref), [ds_ref[i] for i in range(N_CHAINS)]))
        for i in range(N_CHAINS):
            ds_ref[i] = grads[0][i]
        for o_ref, gx in zip((dr_ref, dlw_ref, dkd_ref, dv_ref, dkk_ref, da_ref), grads[1:]):
            _store_tiles(o_ref, gx)

    shared, per_dir, state = _scan_specs(order)
    shp = jax.ShapeDtypeStruct((bsz, t, 2 * RW), F32)
    seq = lambda z: z.reshape(bsz, t, z.shape[1])
    res = _call("scan_bwd", body, (2, bsz // SCAN_SEQS, nc), [seq(z) for z in (r, v, kk, lw, kd, a, dy)] + [s0, tm],
                [shared, shared, shared, per_dir, per_dir, per_dir, per_dir, state, state],
                [per_dir] * 6, [shp] * 6, [pltpu.VMEM((N_CHAINS, PAIR, PAIR), F32)], exch)
    outs, got = res if exch else (res, None)
    outs = [z.reshape(n, 2 * RW) for z in outs]
    return (outs, got) if exch else outs


def _prep_fn(ps, w0, w2bd, a0, a2bd, g2p, k_k, k_a, hsum):
    head_sum = lambda z: _head_sum(z, hsum)
    r, k, v = ps[:, 0:RW], ps[:, RW:2 * RW], ps[:, 2 * RW:3 * RW]
    lora = 2 * HEAD
    wd, ad, gd = (ps[:, 3 * RW:3 * RW + lora], ps[:, 3 * RW + lora:3 * RW + 2 * lora],
                  ps[:, 3 * RW + 2 * lora:SHIFT_PAD])
    logw = -DECAY_SCALE * _sigmoid(_dot1(jnp.tanh(wd), w2bd) + w0)
    a = _sigmoid(_dot1(ad, a2bd) + a0)
    g = _dot1(_sigmoid(gd), g2p)
    kkr = k * k_k
    kk = kkr / jnp.maximum(jnp.sqrt(head_sum(kkr * kkr)), NORM_EPS)
    k2 = jnp.concatenate([k, k], axis=1)
    ka2 = jnp.concatenate([k_a, k_a], axis=1)
    kd = k2 * (1.0 + (a - 1.0) * ka2)
    return r, v, kk, logw, a, kd, g


def _post_fn(y2, r, v, kd, g, lnx_g, lnx_b, r_k, hsum):
    head_sum = lambda z: _head_sum(z, hsum)
    y = y2[:, 0:RW] + y2[:, RW:2 * RW]
    mu = head_sum(y) * (1.0 / HEAD)
    yc = y - mu
    var = head_sum(yc * yc) * (1.0 / HEAD)
    yn = yc * lax.rsqrt(var + GN_EPS) * lnx_g + lnx_b
    bonus = head_sum(r * (kd[:, 0:RW] + kd[:, RW:2 * RW]) * r_k) * v
    return ((yn + bonus) * g,)


def _glu_fn(pa, pb):
    return (pa * _sigmoid(pb),)


def _conv_out_fn(cv, ln_g, ln_b):
    mu = jnp.mean(cv, axis=-1, keepdims=True)
    cc = cv - mu
    var = jnp.mean(cc * cc, axis=-1, keepdims=True)
    y = cc * lax.rsqrt(var + LN_EPS) * ln_g + ln_b
    return (y * _sigmoid(y),)


def _local_step(x, tgt, w, ex=None):
    bsz, t, d = x.shape
    n = bsz * t
    x2d, tgt2d = x.reshape(n, d), tgt.reshape(n, d)
    hsum = jnp.tile(jnp.kron(jnp.eye(N_HEADS, dtype=BF16), jnp.ones((HEAD, HEAD), BF16)), (2, 1))
    w = dict(w)
    parts = {} if ex else None

    def hosted(result, finish=None):
        if not ex:
            return result
        outs, got = result
        if finish is not None:
            w.update(finish(got))
        return outs

    xb = x2d.astype(BF16)
    hg1, hu1, act1 = hosted(_ffn_in("ffn1_in", xb, w["ffn1_w_in"], exch=(ex["g1"][0], True) if ex else None),
                            ex["g1"][1] if ex else None)
    x1, x1b, z1 = _mm_ln("ffn1_out_ln1", act1, w["ffn1_w_out"], x2d, w["ln1_g"], w["ln1_b"], 0.5, a_blocked=True)
    p = _mm("w_in_proj", x1b, w["w_in"], "nn", F32)
    p3 = p.reshape(bsz, t, IN_PAD)
    ps = _shift("shift_fwd", p3, w["mu_prev"], w["mu_next"]).reshape(n, SHIFT_PAD)
    prep_params = [w["w0"], w["w2"], w["a0"], w["a2"], w["g2"], w["k_k"], w["k_a"], hsum]
    r, v, kk, logw, a, kd, g = _rowwise(
        "rwkv_prep", _prep_fn, [ps], prep_params,
        [(RW, F32), (RW, F32), (RW, F32), (2 * RW, F32), (2 * RW, F32), (2 * RW, F32), (RW, F32)], [])
    y2, s0, tm = hosted(_scan_fwd(r, v, kk, logw, kd, a, bsz, exch=(ex["g2"][0], True) if ex else None),
                    ex["g2"][1] if ex else None)
    post_params = [w["lnx_g"], w["lnx_b"], w["r_k"], hsum]
    (y_rwkv,) = _rowwise("rwkv_post", _post_fn, [y2, r, v, kd, g], post_params, [(RW, BF16)], [])
    (u,) = _rowwise("conv_glu", _glu_fn, [(p, CW, 4), (p, CW, 5)], [], [(CW, F32)], [])
    cv = _dwconv("conv_dw", u.reshape(bsz, t, CW), w["conv_dw"], w["conv_b"], False).reshape(n, CW)
    (y_conv,) = _rowwise("conv_out", _conv_out_fn, [cv], [w["conv_ln_g"], w["conv_ln_b"]], [(CW, BF16)], [])
    ycat = jnp.concatenate([y_rwkv, y_conv], axis=1)
    x2, x2b, z2 = _mm_ln("w_out_ln2", ycat, w["w_out"], x1, w["ln2_g"], w["ln2_b"], 1.0)
    hg2, hu2, act2 = _ffn_in("ffn2_in", x2b, w["ffn2_w_in"])
    gr = {}
    dz3, dz3b, loss, gr["ln3_g"], gr["ln3_b"] = _mm_ln(
        "ffn2_out_ln3", act2, w["ffn2_w_out"], x2, w["ln3_g"], w["ln3_b"], 0.5, tgt=tgt2d, a_blocked=True)

    dx2, _ = _ffn_bwd("ffn2", gr, dz3, dz3b, x2b, w["ffn2_w_in"], w["ffn2_w_out"], hg2, hu2, act2)
    dz2, dz2b, gr["ln2_g"], gr["ln2_b"] = _ln_bwd("ln2_bwd", z2, w["ln2_g"], dx2)
    gr["w_out"] = _mm("w_out_wgrad", ycat, dz2b, "tn", BF16)
    dycat = _mm("w_out_dgrad", dz2b, w["w_out"], "nt", F32)
    conv_out_bwd = _vjp_of(_conv_out_fn, 3)
    dcv, gr["conv_ln_g"], gr["conv_ln_b"] = _rowwise(
        "conv_out_bwd", lambda cv_, ct_, g_, b_: conv_out_bwd(cv_, g_, b_, ct_),
        [cv, (dycat, CW, 1)], [w["conv_ln_g"], w["conv_ln_b"]], [(CW, F32)], [(1, CW), (1, CW)])
    dcv3 = dcv.reshape(bsz, t, CW)
    gr["conv_dw"], gr["conv_b"] = _dwconv_dw("conv_dw_wgrad", u.reshape(bsz, t, CW), dcv3)
    du = _dwconv("conv_dw_dgrad", dcv3, w["conv_dw"], jnp.zeros((1, CW), F32), True).reshape(n, CW)

    def glu_bwd(pa, pb, ct):
        return (jnp.concatenate(_vjp_of(_glu_fn, 2)(pa, pb, ct), axis=1),)

    (dp_conv,) = _rowwise("conv_glu_bwd", glu_bwd, [(p, CW, 4), (p, CW, 5), du], [], [(2 * CW, BF16)], [])

    def post_bwd(y2_, r_, v_, kd_, g_, ct, lg, lb, rk, hs):
        return _vjp_of(lambda *z: _post_fn(*z, hs), 8)(y2_, r_, v_, kd_, g_, lg, lb, rk, ct)

    dy2, dr_post, dv_post, dkd_post, dg, gr["lnx_g"], gr["lnx_b"], gr["r_k"] = _rowwise(
        "rwkv_post_bwd", post_bwd, [y2, r, v, kd, g, (dycat, RW, 0)], post_params,
        [(2 * RW, F32), (RW, F32), (RW, F32), (2 * RW, F32), (RW, F32)], [(1, RW), (1, RW), (1, RW)])
    sends = [jnp.concatenate(gr["ffn2_w_in"], axis=0), gr["ffn2_w_out"].reshape(N_DEV, D_FF // N_DEV, D_MODEL),
             gr["w_out"].reshape(N_DEV, D_MODEL // N_DEV, D_MODEL)]
    res = _scan_bwd(r, v, kk, logw, kd, a, s0, tm, dy2, bsz, exch=(sends, False) if ex else None)
    if ex:
        res, got = res
        parts.update(zip(("ffn2_w_in", "ffn2_w_out", "w_out"), got))
    dr_s, dv_s, dkk_s, dlw, dkd_s, da = res

    def prep_bwd(ps_, dr2, dr1, dv2, dv1, dkk2, dlw_, da_, dkd2, dkd1, dg_, *prm):
        half = lambda z: z[:, 0:RW] + z[:, RW:2 * RW]
        return _vjp_of(lambda *z: _prep_fn(*z, prm[-1]), 8)(
            ps_, *prm[:-1], half(dr2) + dr1, half(dv2) + dv1, half(dkk2), dlw_, da_, dkd2 + dkd1, dg_)

    dps, gr["w0"], gr["w2"], gr["a0"], gr["a2"], gr["g2"], gr["k_k"], gr["k_a"] = _rowwise(
        "rwkv_prep_bwd", prep_bwd,
        [ps, dr_s, dr_post, dv_s, dv_post, dkk_s, dlw, da, dkd_s, dkd_post, dg], prep_params,
        [(SHIFT_PAD, F32)], [q.shape for q in prep_params[:-1]])
    dps3 = dps.reshape(bsz, t, SHIFT_PAD)
    dp_shift, gr["mu_prev"], gr["mu_next"] = _shift_bwd("shift_bwd", dps3, p3, w["mu_prev"], w["mu_next"])
    dp_shift = dp_shift.reshape(n, SHIFT_PAD)
    dp = jnp.concatenate([dp_shift, dp_conv], axis=1)
    gr["w_in"] = _mm("w_in_wgrad", x1b, dp, "tn", BF16)
    dx1 = _mm("w_in_dgrad", dp, w["w_in"], "nt", F32, add=dz2, add_scale=ALPHA)
    dz1, dz1b, gr["ln1_g"], gr["ln1_b"] = _ln_bwd("ln1_bwd", z1, w["ln1_g"], dx1)
    riders = None
    if ex:
        gw_in = _unpad_in_cols(gr["w_in"]).reshape(D_MODEL, N_DEV, IN_COLS // N_DEV).transpose(1, 0, 2)
        small = _pack([_grad_small(nm, gr[nm]) for nm in SMALL_SHARDED + SMALL_REPL])
        riders = {"out_bwd": [gw_in, jnp.broadcast_to(small[None], (N_DEV,) + small.shape)], "in_wgrad": []}
    grad_x, got = _ffn_bwd("ffn1", gr, dz1, dz1b, xb, w["ffn1_w_in"], w["ffn1_w_out"], hg1, hu1, act1, riders)
    if ex:
        parts.update(w_in=got["out_bwd"][0], small=got["out_bwd"][1], ffn1_w_out=got["in_wgrad"][0],
                     ffn1_w_in=got["in_dgrad"][0])
    return loss[0, 0], grad_x.reshape(bsz, t, d), gr, parts


def _ffn_in_wgrad(name, xin, dhg, dhu, tk=1024, exch=None):
    n = xin.shape[0]
    nj, nt = N_DEV // 2, n // tk

    def body(x_ref, g_ref, u_ref, og_ref, ou_ref, accg_ref, accu_ref):
        i = pl.program_id(1)

        @pl.when(i == 0)
        def _():
            accg_ref[...] = jnp.zeros_like(accg_ref)
            accu_ref[...] = jnp.zeros_like(accu_ref)

        xt = x_ref[...].astype(BF16).T
        accg_ref[...] += _dot(xt, g_ref[...])
        accu_ref[...] += _dot(xt, u_ref[...])

        @pl.when(i == nt - 1)
        def _():
            og_ref[...] = accg_ref[...].astype(og_ref.dtype)
            ou_ref[...] = accu_ref[...].astype(ou_ref.dtype)

    dh_blk = pl.BlockSpec((None, tk, FF_BLK), lambda j, i: (j, i, 0))
    o_blk = pl.BlockSpec((None, D_MODEL, FF_BLK), lambda j, i: (j, 0, 0))
    shp = jax.ShapeDtypeStruct((nj, D_MODEL, FF_BLK), BF16)
    return _call(name, body, (nj, nt), [xin, dhg, dhu],
                 [pl.BlockSpec((tk, D_MODEL), lambda j, i: (i, 0)), dh_blk, dh_blk], [o_blk, o_blk], [shp, shp],
                 [pltpu.VMEM((D_MODEL, FF_BLK), F32)] * 2, exch)


def _ffn_in_dgrad(name, dhg, dhu, wg, add, add_scale, tm=1024, exch=None):
    n = add.shape[0]
    nj = N_DEV // 2

    def body(g_ref, u_ref, wgate_ref, wup_ref, add_ref, o_ref, acc_ref):
        j = pl.program_id(1)

        @pl.when(j == 0)
        def _():
            acc_ref[...] = jnp.zeros_like(acc_ref)

        acc_ref[...] += _dot(g_ref[...], wgate_ref[...], "nt") + _dot(u_ref[...], wup_ref[...], "nt")

        @pl.when(j == nj - 1)
        def _():
            o_ref[...] = acc_ref[...] + add_scale * add_ref[...]

    dh_blk = pl.BlockSpec((None, tm, FF_BLK), lambda i, j: (j, i, 0))
    row = pl.BlockSpec((tm, D_MODEL), lambda i, j: (i, 0))
    return _call(name, body, (n // tm, nj), [dhg, dhu, wg, wg, add],
                 [dh_blk, dh_blk, pl.BlockSpec((None, D_MODEL, FF_BLK), lambda i, j: (j, 0, 0)),
                  pl.BlockSpec((None, D_MODEL, FF_BLK), lambda i, j: (j + nj, 0, 0)), row],
                 row, jax.ShapeDtypeStruct((n, D_MODEL), F32), [pltpu.VMEM((tm, D_MODEL), F32)], exch)


def _ffn_bwd(tag, gr, dz, dzb, xin, wg, wout, hg, hu, act, riders=None):
    own = riders is not None

    def hosted(result):
        return result if own else (result, None)

    (dhg, dhu, gwo), got_a = hosted(_ffn_out_bwd(
        tag + "_out_bwd", dzb, wout, hg, hu, act, exch=(riders["out_bwd"], False) if own else None))
    gr[tag + "_w_out"] = gwo
    send = ([gwo.reshape(N_DEV, D_FF // N_DEV, D_MODEL)] + riders["in_wgrad"], False) if own else None
    dw, got_b = hosted(_ffn_in_wgrad(tag + "_in_wgrad", xin, dhg, dhu, exch=send))
    gr[tag + "_w_in"] = dw
    send = ([jnp.concatenate(dw, axis=0)], False) if own else None
    dx, got_c = hosted(_ffn_in_dgrad(tag + "_in_dgrad", dhg, dhu, wg, dz, ALPHA, exch=send))
    return dx, {"out_bwd": got_a, "in_wgrad": got_b, "in_dgrad": got_c}


def _adam_math(g, w, m, v):
    m = ADAM_B1 * m + (1.0 - ADAM_B1) * g
    v = ADAM_B2 * v + (1.0 - ADAM_B2) * (g * g)
    m_hat = m / (1.0 - ADAM_B1 ** ADAM_STEP)
    v_hat = v / (1.0 - ADAM_B2 ** ADAM_STEP)
    delta = -ADAM_LR * (m_hat / (jnp.sqrt(v_hat) + ADAM_EPS) + ADAM_WD * w)
    return delta, m, v


def _adam(name, parts, w, m, v, tr=128):
    rows, cols = w.shape
    tr = min(tr, rows)
    while rows % tr:
        tr -= 8

    def body(p_ref, w_ref, m_ref, v_ref, g_ref, d_ref, mo_ref, vo_ref):
        g = p_ref[0].astype(F32)
        for s in range(1, N_DEV):
            g = g + p_ref[s].astype(F32)
        g_ref[...] = g
        d_ref[...], mo_ref[...], vo_ref[...] = _adam_math(g, w_ref[...], m_ref[...], v_ref[...])

    blk = pl.BlockSpec((tr, cols), lambda i: (i, 0))
    shp = jax.ShapeDtypeStruct((rows, cols), F32)
    return _call(name, body, (rows // tr,), [parts, w, m, v],
                 [pl.BlockSpec((N_DEV, tr, cols), lambda i: (0, i, 0)), blk, blk, blk], [blk] * 4, [shp] * 4)


def _sum8(name, parts):
    _, rows, cols = parts.shape

    def body(p_ref, o_ref):
        g = p_ref[0]
        for s in range(1, N_DEV):
            g = g + p_ref[s]
        o_ref[...] = g

    return pl.pallas_call(body, name=name, out_shape=jax.ShapeDtypeStruct((rows, cols), F32),
                          compiler_params=_params())(parts)


def _adam_small(name, g, w, m, v):
    def body(g_ref, w_ref, m_ref, v_ref, d_ref, mo_ref, vo_ref):
        d_ref[...], mo_ref[...], vo_ref[...] = _adam_math(g_ref[...], w_ref[...], m_ref[...], v_ref[...])

    shp = jax.ShapeDtypeStruct(g.shape, F32)
    return pl.pallas_call(body, name=name, out_shape=[shp] * 3, compiler_params=_params())(g, w, m, v)


def _pack(arrs, lane=128):
    flat = jnp.concatenate([a.reshape(-1).astype(F32) for a in arrs])
    pad = (-flat.shape[0]) % (8 * lane)
    return jnp.pad(flat, (0, pad)).reshape(-1, lane)


def _unpack(packed, shapes):
    flat, out, off = packed.reshape(-1), [], 0
    for s in shapes:
        sz = math.prod(s)
        out.append(flat[off:off + sz].reshape(s))
        off += sz
    return out


def _pad_in_cols(wfull):
    zeros = jnp.zeros((wfull.shape[0], SHIFT_PAD - SHIFT_COLS), wfull.dtype)
    return jnp.concatenate([wfull[:, :SHIFT_COLS], zeros, wfull[:, SHIFT_COLS:]], axis=1)


def _unpad_in_cols(gfull):
    return jnp.concatenate([gfull[:, :SHIFT_COLS], gfull[:, SHIFT_PAD:]], axis=1)


def _block_diag2(wd):
    z = jnp.zeros_like(wd[0])
    return jnp.concatenate([jnp.concatenate([wd[0], z], axis=1), jnp.concatenate([z, wd[1]], axis=1)], axis=0)


def _unblock_diag2(g):
    return jnp.stack([g[0:64, 0:RW], g[64:128, RW:2 * RW]])


SMALL_SHARDED = ("w0", "w2", "a0", "a2", "g2", "conv_dw")
SMALL_REPL = ("mu_prev", "mu_next", "k_k", "k_a", "r_k", "lnx_g", "lnx_b", "conv_b", "conv_ln_g", "conv_ln_b",
              "ln1_g", "ln1_b", "ln2_g", "ln2_b", "ln3_g", "ln3_b")
BIG = ("ffn1_w_in", "ffn1_w_out", "w_in", "w_out", "ffn2_w_in", "ffn2_w_out")
WEIGHTS = ("ffn1_w_in", "ffn1_w_out", "w_in", "mu_prev", "mu_next", "w0", "w2", "a0", "a2", "g2", "k_k", "k_a",
           "r_k", "lnx_g", "lnx_b", "conv_dw", "conv_b", "conv_ln_g", "conv_ln_b", "w_out", "ffn2_w_in",
           "ffn2_w_out", "ln1_g", "ln1_b", "ln2_g", "ln2_b", "ln3_g", "ln3_b")


def _full_small(name, full):
    if name in ("w0", "a0"):
        return full.reshape(1, 2 * RW)
    if name in ("w2", "a2"):
        return _block_diag2(full)
    if name == "g2":
        return jnp.pad(full, ((0, 256 - GATE_LORA), (0, 0)))
    if name == "conv_dw":
        return jnp.pad(full, ((0, 1), (0, 0)))
    if name in ("mu_prev", "mu_next"):
        return jnp.pad(full.reshape(1, SHIFT_COLS), ((0, 0), (0, SHIFT_PAD - SHIFT_COLS)))
    return full.reshape(1, -1)


def _grad_small(name, g):
    if name in ("w0", "a0"):
        return g.reshape(2, RW)
    if name in ("w2", "a2"):
        return _unblock_diag2(g)
    if name == "g2":
        return g[:GATE_LORA]
    if name == "conv_dw":
        return g[:CONV_K]
    if name in ("mu_prev", "mu_next"):
        return g[0, :SHIFT_COLS]
    if name == "r_k":
        return g.reshape(N_HEADS, HEAD)
    return g.reshape(-1)


def kernel(x, ffn1_w_in, ffn1_w_out, w_in, mu_prev, mu_next, w0, w2, a0, a2, g2, k_k, k_a, r_k, lnx_g, lnx_b, conv_dw, conv_b, conv_ln_g, conv_ln_b, w_out, ffn2_w_in, ffn2_w_out, ln1_g, ln1_b, ln2_g, ln2_b, ln3_g, ln3_b, loss_target, m_ffn1_w_in, m_ffn1_w_out, m_w_in, m_mu_prev, m_mu_next, m_w0, m_w2, m_a0, m_a2, m_g2, m_k_k, m_k_a, m_r_k, m_lnx_g, m_lnx_b, m_conv_dw, m_conv_b, m_conv_ln_g, m_conv_ln_b, m_w_out, m_ffn2_w_in, m_ffn2_w_out, m_ln1_g, m_ln1_b, m_ln2_g, m_ln2_b, m_ln3_g, m_ln3_b, v_ffn1_w_in, v_ffn1_w_out, v_w_in, v_mu_prev, v_mu_next, v_w0, v_w2, v_a0, v_a2, v_g2, v_k_k, v_k_a, v_r_k, v_lnx_g, v_lnx_b, v_conv_dw, v_conv_b, v_conv_ln_g, v_conv_ln_b, v_w_out, v_ffn2_w_in, v_ffn2_w_out, v_ln1_g, v_ln1_b, v_ln2_g, v_ln2_b, v_ln3_g, v_ln3_b):
    args = dict(locals())
    drop = lambda z: z.reshape(z.shape[1:])
    wsh = {n: drop(args[n]) for n in WEIGHTS}
    msh = {n: drop(args["m_" + n]) for n in WEIGHTS}
    vsh = {n: drop(args["v_" + n]) for n in WEIGHTS}
    me = 4 * lax.axis_index("x") + 2 * lax.axis_index("y") + lax.axis_index("c")
    bf = {n: wsh[n].astype(BF16) for n in BIG}

    w = {"ffn1_w_in": _gather_two_level("gather_ffn1_w_in", bf["ffn1_w_in"])}
    for n in SMALL_REPL:
        w[n] = _full_small(n, wsh[n])
    small_shapes = [wsh[n].shape for n in SMALL_SHARDED]

    def finish1(got):
        f1_out, w_in_g, small = got
        cols = zip(*[_unpack(small[dv], small_shapes) for dv in range(N_DEV)])
        out = {n: _full_small(n, jnp.concatenate(s, axis=-1)) for n, s in zip(SMALL_SHARDED, cols)}
        out["ffn1_w_out"] = f1_out.reshape(D_FF, D_MODEL)
        out["w_in"] = _pad_in_cols(w_in_g.transpose(1, 0, 2).reshape(D_MODEL, IN_COLS))
        return out

    def finish2(got):
        w_out_g, f2_in, f2_out = got
        return {"w_out": w_out_g.reshape(D_MODEL, D_MODEL), "ffn2_w_in": f2_in,
                "ffn2_w_out": f2_out.reshape(D_FF, D_MODEL)}

    ex = {"g1": ([bf["ffn1_w_out"], bf["w_in"], _pack([wsh[n] for n in SMALL_SHARDED])], finish1),
          "g2": ([bf["w_out"], bf["ffn2_w_in"], bf["ffn2_w_out"]], finish2)}
    loss_part, grad_x, gr, parts = _local_step(x, loss_target, w, ex)
    loss = lax.psum(loss_part, ("x", "y", "c"))

    out = {n: _adam("adam_" + n, parts[n], wsh[n], msh[n], vsh[n]) for n in BIG}
    small_names = SMALL_SHARDED + SMALL_REPL
    full_shapes = [_grad_small(n, gr[n]).shape for n in small_names]
    summed = _unpack(_sum8("sum_small_grads", parts["small"]), full_shapes)
    mine = []
    for n, g in zip(small_names, summed):
        if n in SMALL_SHARDED:
            g = lax.dynamic_slice_in_dim(g, me * HEAD, HEAD, axis=g.ndim - 1)
        mine.append(g)
    shapes = [g.shape for g in mine]
    d_s, m_s, v_s = _adam_small("adam_small", _pack(mine), _pack([wsh[n] for n in small_names]),
                                _pack([msh[n] for n in small_names]), _pack([vsh[n] for n in small_names]))
    for n, g, dl, mn, vn in zip(small_names, mine, _unpack(d_s, shapes), _unpack(m_s, shapes), _unpack(v_s, shapes)):
        out[n] = (g, dl, mn, vn)

    res = [loss, grad_x]
    for k in range(4):
        res += [out[n][k].reshape((1,) + out[n][k].shape) for n in WEIGHTS]
    return tuple(res)
```

```python
import functools
import math

import jax
import jax.numpy as jnp
from jax import lax
from jax.experimental import pallas as pl
from jax.experimental.pallas import tpu as pltpu

F32 = jnp.float32
BF16 = jnp.bfloat16

N_DEV = 8
D_MODEL = 1024
RW = 512
N_HEADS = 8
HEAD = 64
CW = 512
CONV_K = 31
D_FF = 2816
FF_BLK = 704
GATE_LORA = 160
SHIFT_COLS = 1952
SHIFT_PAD = 2048
IN_COLS = 2976
IN_PAD = 3072
LN_EPS = 1e-5
GN_EPS = 64e-5
NORM_EPS = 1e-12
ALPHA = 2.0 ** 0.25
DECAY_SCALE = math.exp(-0.5)
CHUNK = 64
ADAM_LR, ADAM_B1, ADAM_B2, ADAM_EPS, ADAM_WD, ADAM_STEP = 0.001, 0.9, 0.999, 1e-8, 0.01, 10
VMEM_LIMIT = 56 * 1024 * 1024
MXU_DIM = 256

_DN = {"nn": (((1,), (0,)), ((), ())), "nt": (((1,), (1,)), ((), ())), "tn": (((0,), (0,)), ((), ()))}


def _params():
    return pltpu.CompilerParams(vmem_limit_bytes=VMEM_LIMIT)


def _dot(a, b, dims="nn"):
    return lax.dot_general(a, b, _DN[dims], preferred_element_type=F32)


def _split(x):
    hi = x.astype(BF16)
    return hi, (x - hi.astype(F32)).astype(BF16)


def _dot3_impl(a, b, dims):
    ah, al = _split(a)
    bh, bl = _split(b)
    ka, kb = _DN[dims][0][0][0], _DN[dims][0][1][0]
    return _dot(jnp.concatenate([ah, ah, al], axis=ka), jnp.concatenate([bh, bl, bh], axis=kb), dims)


@functools.partial(jax.custom_vjp, nondiff_argnums=(2,))
def _dot3(a, b, dims="nn"):
    return _dot3_impl(a, b, dims)


def _dot3_fwd(a, b, dims):
    return _dot3_impl(a, b, dims), (a, b)


def _dot2_ct(x, y, dims, ct_left):
    ka, kb = _DN[dims][0][0][0], _DN[dims][0][1][0]
    if ct_left:
        c, (h, l) = x.astype(BF16), _split(y)
        return _dot(jnp.concatenate([c, c], axis=ka), jnp.concatenate([h, l], axis=kb), dims)
    (h, l), c = _split(x), y.astype(BF16)
    return _dot(jnp.concatenate([h, l], axis=ka), jnp.concatenate([c, c], axis=kb), dims)


def _dot3_bwd(dims, res, ct):
    a, b = res
    if dims == "nn":
        return _dot2_ct(ct, b, "nt", True), _dot2_ct(a, ct, "tn", False)
    if dims == "nt":
        return _dot2_ct(ct, b, "nn", True), _dot2_ct(ct, a, "tn", True)
    return _dot2_ct(b, ct, "nt", False), _dot2_ct(a, ct, "nn", False)


_dot3.defvjp(_dot3_fwd, _dot3_bwd)


def _dot1_impl(a, b, dims):
    return _dot(a.astype(BF16), b.astype(BF16), dims)


@functools.partial(jax.custom_vjp, nondiff_argnums=(2,))
def _dot1(a, b, dims="nn"):
    return _dot1_impl(a, b, dims)


def _dot1_bwd(dims, res, ct):
    a, b = res
    if dims == "nn":
        return _dot1_impl(ct, b, "nt"), _dot1_impl(a, ct, "tn")
    if dims == "nt":
        return _dot1_impl(ct, b, "nn"), _dot1_impl(ct, a, "tn")
    return _dot1_impl(b, ct, "nt"), _dot1_impl(a, ct, "nn")


_dot1.defvjp(lambda a, b, dims: (_dot1_impl(a, b, dims), (a, b)), _dot1_bwd)


def _dot1_two(a, b, dims="nn"):
    ax_a, ax_b = {"nn": (0, 1), "nt": (0, 0), "tn": (1, 1)}[dims]
    shallow = dims != "tn" and 2 * a[0].shape[1] <= MXU_DIM
    out = []
    for i in range(0, len(a), 2):
        if shallow:
            z = jnp.zeros_like(b[i])
            bd = jnp.concatenate([jnp.concatenate([b[i], z], axis=1), jnp.concatenate([z, b[i + 1]], axis=1)], axis=0)
            r = _dot1(jnp.concatenate(a[i:i + 2], axis=1), bd, dims)
            n = r.shape[1] // 2
            out += [r[:, :n], r[:, n:]]
        else:
            r = _dot1(jnp.concatenate(a[i:i + 2], axis=ax_a), jnp.concatenate(b[i:i + 2], axis=ax_b), dims)
            m, n = r.shape[0] // 2, r.shape[1] // 2
            out += [r[:m, :n], r[m:, n:]]
    return out


def _tri_inv_impl(l, eye):
    steps = int(math.log2(CHUNK)) - 1
    m = l[0].shape[0]
    tm = [eye + x for x in l]
    lp = _dot1_two(l, l)
    for k in range(steps):
        if k < steps - 1:
            both = _dot1_two([jnp.concatenate([t, p], axis=0) for t, p in zip(tm, lp)], lp)
            tm = [t + b[:m] for t, b in zip(tm, both)]
            lp = [b[m:] for b in both]
        else:
            tm = [t + x for t, x in zip(tm, _dot1_two(tm, lp))]
    return tm


@jax.custom_vjp
def _tri_inv(l, eye):
    return _tri_inv_impl(l, eye)


def _tri_inv_fwd(l, eye):
    tm = _tri_inv_impl(l, eye)
    return tm, (tm, eye)


def _tri_inv_bwd(res, ct):
    tm, eye = res
    return _dot1_two(_dot1_two(tm, ct, "tn"), tm, "nt"), jnp.zeros_like(eye)


_tri_inv.defvjp(_tri_inv_fwd, _tri_inv_bwd)


@jax.custom_vjp
def _tri_inv_known(l, tm):
    return tm


_tri_inv_known.defvjp(lambda l, tm: (tm, tm),
                      lambda tm, ct: (_dot1_two(_dot1_two(tm, ct, "tn"), tm, "nt"), [jnp.zeros_like(t) for t in tm]))


def _ones_impl(x, g2):
    x1 = x.astype(BF16)
    x2 = (x - x1.astype(F32)).astype(BF16)
    return _dot(jnp.concatenate([x1, x2], axis=1), g2)


@jax.custom_vjp
def _head_sum(x, g2):
    return _ones_impl(x, g2)


_head_sum.defvjp(lambda x, g2: (_ones_impl(x, g2), g2), lambda g2, ct: (_ones_impl(ct, g2), jnp.zeros_like(g2)))


def _prefix_sum(x):
    row = lax.broadcasted_iota(jnp.int32, x.shape, 0)
    sh = 1
    while sh < x.shape[0]:
        x = x + jnp.where(row >= sh, pltpu.roll(x, sh, 0), 0.0)
        sh *= 2
    return x


def _dir_cumsum_impl(x, sgn):
    pre = _prefix_sum(x)
    return jnp.where(sgn > 0.0, pre, jnp.sum(x, axis=0, keepdims=True) - pre + x)


@jax.custom_vjp
def _dir_cumsum(x, sgn):
    return _dir_cumsum_impl(x, sgn)


_dir_cumsum.defvjp(lambda x, sgn: (_dir_cumsum_impl(x, sgn), sgn),
                   lambda sgn, ct: (_dir_cumsum_impl(ct, -sgn), jnp.zeros_like(sgn)))


def _sigmoid(x):
    return 1.0 / (1.0 + jnp.exp(-x))


def _mesh_pos():
    return lax.axis_index("x"), lax.axis_index("y"), lax.axis_index("c")


def _peer(pos, q):
    x, y, c = pos
    return (1 - x if q & 4 else x, 1 - y if q & 2 else y, 1 - c if q & 1 else c)


def _linear(pos):
    return 4 * pos[0] + 2 * pos[1] + pos[2]


def _exchange_copies(x_refs, o_refs, send_sems, recv_sems, local_sems, gather):
    pos = _mesh_pos()
    me = _linear(pos)
    starts, wait_recv, wait_send, wait_local = [], [], [], []
    for t in range(len(x_refs)):
        src = x_refs[t] if gather else x_refs[t].at[me]
        cp = pltpu.make_async_copy(src, o_refs[t].at[me], local_sems.at[t])
        starts.append(cp.start)
        wait_local.append(cp.wait)
    for q in range(1, N_DEV):
        peer = _peer(pos, q)
        for t in range(len(x_refs)):
            src = x_refs[t] if gather else x_refs[t].at[_linear(peer)]
            sems = dict(send_sem=send_sems.at[t, q - 1], recv_sem=recv_sems.at[t, q - 1],
                        device_id=peer, device_id_type=pl.DeviceIdType.MESH)
            send = pltpu.make_async_remote_copy(src_ref=src, dst_ref=o_refs[t].at[me], **sems)
            recv = pltpu.make_async_remote_copy(src_ref=src, dst_ref=o_refs[t].at[_linear(peer)], **sems)
            starts.append(send.start)
            wait_recv.append(recv.wait_recv)
            wait_send.append(send.wait_send)
    return starts, wait_recv + wait_send + wait_local


def _exchange_shapes(xs, gather):
    return [jax.ShapeDtypeStruct((N_DEV,) + (x.shape if gather else x.shape[1:]), x.dtype) for x in xs]


def _exchange_sems(nt):
    return [pltpu.SemaphoreType.DMA((nt, N_DEV - 1)), pltpu.SemaphoreType.DMA((nt, N_DEV - 1)),
            pltpu.SemaphoreType.DMA((nt,))]


def _gather_two_level(name, x):
    def body(x_ref, out_ref, send_sems, recv_sems, local_sem):
        px, py, pc = _mesh_pos()
        me, sibling = (px, py, pc), (px, py, 1 - pc)
        chips = [(1 - px, py), (px, 1 - py), (1 - px, 1 - py)]

        def slot(pos):
            return out_ref.at[_linear(pos)]

        def copy(k, block, to, src=None):
            return pltpu.make_async_remote_copy(
                src_ref=slot(block) if src is None else src, dst_ref=slot(block), send_sem=send_sems.at[k],
                recv_sem=recv_sems.at[k], device_id=to, device_id_type=pl.DeviceIdType.MESH)

        mine = pltpu.make_async_copy(x_ref, slot(me), local_sem)
        mine.start()
        first = [copy(0, me, sibling, src=x_ref)]
        first += [copy(1 + j, me, (*chip, pc), src=x_ref) for j, chip in enumerate(chips)]
        for cp in first:
            cp.start()
        passed = [copy(4 + j, (*chip, pc), sibling) for j, chip in enumerate(chips)]
        for j, chip in enumerate(chips):
            copy(1 + j, (*chip, pc), me).wait_recv()
            passed[j].start()
        copy(0, sibling, me).wait_recv()
        for j, chip in enumerate(chips):
            copy(4 + j, (*chip, 1 - pc), me).wait_recv()
        for cp in first + passed:
            cp.wait_send()
        mine.wait()

    any_spec = pl.BlockSpec(memory_space=pl.ANY)
    return pl.pallas_call(
        body, name=name, in_specs=[any_spec], out_specs=any_spec,
        out_shape=jax.ShapeDtypeStruct((N_DEV,) + x.shape, x.dtype),
        scratch_shapes=[pltpu.SemaphoreType.DMA((N_DEV - 1,)), pltpu.SemaphoreType.DMA((N_DEV - 1,)),
                        pltpu.SemaphoreType.DMA])(x)


def _call(name, body, grid, ins, in_specs, out_specs, out_shape, scratch=(), exch=None):
    if exch is None:
        return pl.pallas_call(body, name=name, grid=grid, in_specs=in_specs, out_specs=out_specs,
                              out_shape=out_shape, scratch_shapes=list(scratch), compiler_params=_params())(*ins)
    xs, gather = exch
    single = not isinstance(out_shape, (list, tuple))
    o_specs = [out_specs] if single else list(out_specs)
    o_shape = [out_shape] if single else list(out_shape)
    n_in, n_out, n_x, n_scr = len(ins), len(o_shape), len(xs), len(scratch)

    def wrapped(*refs):
        in_refs = refs[:n_in]
        x_refs = refs[n_in:n_in + n_x]
        out_refs = refs[n_in + n_x:n_in + n_x + n_out]
        got_refs = refs[n_in + n_x + n_out:n_in + 2 * n_x + n_out]
        rest = refs[n_in + 2 * n_x + n_out:]
        starts, waits = _exchange_copies(x_refs, got_refs, *rest[n_scr:], gather)
        ids = [pl.program_id(i) for i in range(len(grid))]
        first = functools.reduce(lambda p, q: p & q, [i == 0 for i in ids])
        last = functools.reduce(lambda p, q: p & q, [i == g - 1 for i, g in zip(ids, grid)])

        @pl.when(first)
        def _():
            for f in starts:
                f()

        body(*in_refs, *out_refs, *rest[:n_scr])

        @pl.when(last)
        def _():
            for f in waits:
                f()

    any_spec = pl.BlockSpec(memory_space=pl.ANY)
    outs = pl.pallas_call(
        wrapped, name=name, grid=grid, in_specs=list(in_specs) + [any_spec] * n_x,
        out_specs=o_specs + [any_spec] * n_x, out_shape=o_shape + _exchange_shapes(xs, gather),
        scratch_shapes=list(scratch) + _exchange_sems(n_x), compiler_params=_params())(*ins, *xs)
    res = outs[:n_out]
    return (res[0] if single else res), outs[n_out:]


def _mm_call(name, dims, grid, red_axis, ins, in_specs, out_shape, out_spec, acc_shape,
             scale=1.0, add_scale=None, exch=None):
    nred = grid[red_axis]

    def body(*refs):
        if add_scale is None:
            a_ref, b_ref, o_ref, acc_ref = refs
            add_ref = None
        else:
            a_ref, b_ref, add_ref, o_ref, acc_ref = refs
        k = pl.program_id(red_axis)

        @pl.when(k == 0)
        def _():
            acc_ref[...] = jnp.zeros_like(acc_ref)

        acc_ref[...] += _dot(a_ref[...].astype(BF16), b_ref[...].astype(BF16), dims)

        @pl.when(k == nred - 1)
        def _():
            r = acc_ref[...]
            if scale != 1.0:
                r = r * scale
            if add_ref is not None:
                r = r + add_scale * add_ref[...].astype(F32)
            o_ref[...] = r.astype(o_ref.dtype)

    return _call(name, body, grid, ins, in_specs, out_spec, out_shape, [pltpu.VMEM(acc_shape, F32)], exch)


def _tile(dim, cap):
    t = min(dim, cap)
    while dim % t or t % 128:
        t -= 128
        assert t > 0, (dim, cap)
    return t


def _mm(name, a, b, dims, out_dtype, scale=1.0, add=None, add_scale=None, tm=512, tn=1024, tk=1024):
    if dims == "tn":
        kd, m = a.shape
        n = b.shape[1]
    else:
        m, kd = a.shape
        n = b.shape[1] if dims == "nn" else b.shape[0]
    tm, tn, tk = _tile(m, tm), _tile(n, tn), _tile(kd, tk)
    a_spec = (pl.BlockSpec((tk, tm), lambda i, j, k: (k, i)) if dims == "tn"
              else pl.BlockSpec((tm, tk), lambda i, j, k: (i, k)))
    b_spec = (pl.BlockSpec((tn, tk), lambda i, j, k: (j, k)) if dims == "nt"
              else pl.BlockSpec((tk, tn), lambda i, j, k: (k, j)))
    o_spec = pl.BlockSpec((tm, tn), lambda i, j, k: (i, j))
    ins, specs = [a, b], [a_spec, b_spec]
    if add is not None:
        ins.append(add)
        specs.append(o_spec)
    return _mm_call(name, dims, (m // tm, n // tn, kd // tk), 2, ins, specs,
                    jax.ShapeDtypeStruct((m, n), out_dtype), o_spec, (tm, tn),
                    scale=scale, add_scale=add_scale if add is not None else None)


def _ffn_in(name, x, wg, tm=512, exch=None):
    n = x.shape[0]
    nj = N_DEV // 2

    def body(x_ref, wgate_ref, wup_ref, hg_ref, hu_ref, act_ref):
        xb = x_ref[...].astype(BF16)
        g = _dot(xb, wgate_ref[...])
        u = _dot(xb, wup_ref[...])
        hg_ref[...] = g.astype(BF16)
        hu_ref[...] = u.astype(BF16)
        act_ref[...] = (g * _sigmoid(g) * u).astype(BF16)

    blk = pl.BlockSpec((None, tm, FF_BLK), lambda j, i: (j, i, 0))
    shp = jax.ShapeDtypeStruct((nj, n, FF_BLK), BF16)
    return _call(name, body, (nj, n // tm), [x, wg, wg],
                 [pl.BlockSpec((tm, D_MODEL), lambda j, i: (i, 0)),
                  pl.BlockSpec((None, D_MODEL, FF_BLK), lambda j, i: (j, 0, 0)),
                  pl.BlockSpec((None, D_MODEL, FF_BLK), lambda j, i: (j + nj, 0, 0))],
                 [blk, blk, blk], [shp, shp, shp], exch=exch)


def _ffn_out_bwd(name, dz, wout, hg, hu, act, tm=512, exch=None):
    n = dz.shape[0]
    nj, ni = N_DEV // 2, n // tm

    def body(dz_ref, w_ref, hg_ref, hu_ref, act_ref, dhg_ref, dhu_ref, dw_ref, acc_ref):
        i = pl.program_id(1)
        dzb = dz_ref[...].astype(BF16)
        dact = 0.5 * _dot(dzb, w_ref[...], "nt")
        g = hg_ref[...].astype(F32)
        u = hu_ref[...].astype(F32)
        s = _sigmoid(g)
        dhg_ref[...] = (dact * u * (s * (1.0 + g * (1.0 - s)))).astype(BF16)
        dhu_ref[...] = (dact * (g * s)).astype(BF16)

        @pl.when(i == 0)
        def _():
            acc_ref[...] = jnp.zeros_like(acc_ref)

        acc_ref[...] += _dot(act_ref[...], dzb, "tn")

        @pl.when(i == ni - 1)
        def _():
            dw_ref[...] = (0.5 * acc_ref[...]).astype(dw_ref.dtype)

    blk = pl.BlockSpec((None, tm, FF_BLK), lambda j, i: (j, i, 0))
    wblk = pl.BlockSpec((FF_BLK, D_MODEL), lambda j, i: (j, 0))
    shp = jax.ShapeDtypeStruct((nj, n, FF_BLK), BF16)
    return _call(name, body, (nj, ni), [dz, wout, hg, hu, act],
                 [pl.BlockSpec((tm, D_MODEL), lambda j, i: (i, 0)), wblk, blk, blk, blk],
                 [blk, blk, wblk], [shp, shp, jax.ShapeDtypeStruct((D_FF, D_MODEL), BF16)],
                 [pltpu.VMEM((FF_BLK, D_MODEL), F32)], exch=exch)


def _mm_ln(name, a, b, xres, g, beta, c, tgt=None, a_blocked=False, tm=512, tk=512):
    if a_blocked:
        nj, n, kj = a.shape
        nk, kb = nj // 2, 2 * kj
        a_spec = pl.BlockSpec((2, tm, kj), lambda i, k: (k, i, 0))
    else:
        n, kd = a.shape
        kb = _tile(kd, tk)
        nk = kd // kb
        a_spec = pl.BlockSpec((tm, kb), lambda i, k: (i, k))
    d = b.shape[1]
    with_loss = tgt is not None

    def body(*refs):
        if with_loss:
            a_ref, b_ref, x_ref, g_ref, be_ref, t_ref, dz_ref, dzb_ref, l_ref, dg_ref, db_ref, acc_ref = refs
        else:
            a_ref, b_ref, x_ref, g_ref, be_ref, o_ref, ob_ref, z_ref, acc_ref = refs
        i, k = pl.program_id(0), pl.program_id(1)

        @pl.when(k == 0)
        def _():
            acc_ref[...] = jnp.zeros_like(acc_ref)

        if a_blocked:
            acc_ref[...] += (_dot(a_ref[0].astype(BF16), b_ref[0:kb // 2, :].astype(BF16))
                             + _dot(a_ref[1].astype(BF16), b_ref[kb // 2:kb, :].astype(BF16)))
        else:
            acc_ref[...] += _dot(a_ref[...].astype(BF16), b_ref[...].astype(BF16))

        @pl.when(k == nk - 1)
        def _():
            z = ALPHA * x_ref[...] + c * acc_ref[...]
            mu = jnp.mean(z, axis=-1, keepdims=True)
            zc = z - mu
            rstd = lax.rsqrt(jnp.mean(zc * zc, axis=-1, keepdims=True) + LN_EPS)
            xh = zc * rstd
            y = xh * g_ref[...] + be_ref[...]
            if with_loss:
                err = y - t_ref[...]
                ct = err * (1.0 / d)
                dxh = ct * g_ref[...]
                dz = rstd * (dxh - jnp.mean(dxh, axis=-1, keepdims=True)
                             - xh * jnp.mean(dxh * xh, axis=-1, keepdims=True))
                dz_ref[...] = dz
                dzb_ref[...] = dz.astype(BF16)
                part = 0.5 * jnp.sum(jnp.sum(err * err, axis=-1, keepdims=True), axis=0, keepdims=True) * (1.0 / d)

                @pl.when(i == 0)
                def _():
                    l_ref[...] = jnp.zeros_like(l_ref)
                    dg_ref[...] = jnp.zeros_like(dg_ref)
                    db_ref[...] = jnp.zeros_like(db_ref)

                l_ref[...] += jnp.broadcast_to(part, l_ref.shape)
                dg_ref[...] += jnp.sum(ct * xh, axis=0, keepdims=True)
                db_ref[...] += jnp.sum(ct, axis=0, keepdims=True)
            else:
                z_ref[...] = z
                o_ref[...] = y
                ob_ref[...] = y.astype(BF16)

    row = pl.BlockSpec((tm, d), lambda i, k: (i, 0))
    vec = pl.BlockSpec((1, d), lambda i, k: (0, 0))
    ins = [a, b, xres, g, beta]
    in_specs = [a_spec, pl.BlockSpec((kb, d), lambda i, k: (k, 0)), row, vec, vec]
    out_specs = [row, row]
    out_shape = [jax.ShapeDtypeStruct((n, d), F32), jax.ShapeDtypeStruct((n, d), BF16)]
    if with_loss:
        ins.append(tgt)
        in_specs.append(row)
        out_specs += [pl.BlockSpec((1, 128), lambda i, k: (0, 0)), vec, vec]
        out_shape += [jax.ShapeDtypeStruct((1, 128), F32)] + [jax.ShapeDtypeStruct((1, d), F32)] * 2
    else:
        out_specs.append(row)
        out_shape.append(jax.ShapeDtypeStruct((n, d), F32))
    return _call(name, body, (n // tm, nk), ins, in_specs, out_specs, out_shape, [pltpu.VMEM((tm, d), F32)])


def _rowwise(name, fn, rows, params, out_rows, out_accs, tm=256, exch=None):
    specs, ins = [], []
    for r in rows:
        arr, w, cb = r if isinstance(r, tuple) else (r, r.shape[1], 0)
        ins.append(arr)
        specs.append(pl.BlockSpec((tm, w), functools.partial(lambda i, cb: (i, cb), cb=cb)))
    n = ins[0].shape[0]
    for p in params:
        ins.append(p)
        specs.append(pl.BlockSpec(p.shape, lambda i: (0, 0)))
    n_in, n_or = len(ins), len(out_rows)

    def body(*refs):
        outs = fn(*[r[...] for r in refs[:n_in]])
        o_refs = refs[n_in:]
        for o_ref, o in zip(o_refs[:n_or], outs[:n_or]):
            o_ref[...] = o.astype(o_ref.dtype)
        if out_accs:
            @pl.when(pl.program_id(0) == 0)
            def _():
                for a_ref in o_refs[n_or:]:
                    a_ref[...] = jnp.zeros_like(a_ref)

            for a_ref, a in zip(o_refs[n_or:], outs[n_or:]):
                a_ref[...] += a.astype(F32)

    out_specs = [pl.BlockSpec((tm, w), lambda i: (i, 0)) for w, _ in out_rows]
    out_specs += [pl.BlockSpec(s, lambda i: (0, 0)) for s in out_accs]
    out_shape = [jax.ShapeDtypeStruct((n, w), dt) for w, dt in out_rows]
    out_shape += [jax.ShapeDtypeStruct(s, F32) for s in out_accs]
    return _call(name, body, (n // tm,), ins, specs, out_specs, out_shape, exch=exch)


def _vjp_of(fn, n_in):
    def g(*args):
        ins, cts = args[:n_in], args[n_in:]
        outs, pull = jax.vjp(fn, *ins)
        return pull(tuple(c.astype(o.dtype) for c, o in zip(cts, outs)))
    return g


def _ln_bwd(name, z, g, ct):
    def fn(zt, ct_, gt):
        mu = jnp.mean(zt, axis=-1, keepdims=True)
        zc = zt - mu
        rstd = lax.rsqrt(jnp.mean(zc * zc, axis=-1, keepdims=True) + LN_EPS)
        xh = zc * rstd
        dxh = ct_ * gt
        dz = rstd * (dxh - jnp.mean(dxh, axis=-1, keepdims=True)
                     - xh * jnp.mean(dxh * xh, axis=-1, keepdims=True))
        return dz, dz, jnp.sum(ct_ * xh, axis=0, keepdims=True), jnp.sum(ct_, axis=0, keepdims=True)

    d = z.shape[1]
    return _rowwise(name, fn, [z, ct], [g], [(d, F32), (d, BF16)], [(1, d), (1, d)])


SHIFT_TILE = 256


def _shift_specs(t):
    r8 = SHIFT_TILE // 8
    return [pl.BlockSpec((None, SHIFT_TILE, SHIFT_PAD), lambda b, i: (b, i, 0)),
            pl.BlockSpec((None, 8, SHIFT_PAD), lambda b, i: (b, jnp.maximum(i * r8 - 1, 0), 0)),
            pl.BlockSpec((None, 8, SHIFT_PAD), lambda b, i: (b, jnp.minimum((i + 1) * r8, t // 8 - 1), 0))]


def _neighbour_diffs(cur, prev_ref, next_ref, i, nt):
    prow = jnp.where(i > 0, prev_ref[7:8, :], 0.0)
    nrow = jnp.where(i < nt - 1, next_ref[0:1, :], 0.0)
    rid = lax.broadcasted_iota(jnp.int32, cur.shape, 0)
    return (jnp.where(rid == 0, prow, pltpu.roll(cur, 1, 0)) - cur,
            jnp.where(rid == SHIFT_TILE - 1, nrow, pltpu.roll(cur, SHIFT_TILE - 1, 0)) - cur)


def _shift(name, src, mu_prev, mu_next):
    bsz, t, _ = src.shape
    nt, w = t // SHIFT_TILE, SHIFT_PAD

    def body(cur_ref, prev_ref, next_ref, mp_ref, mn_ref, o_ref):
        cur = cur_ref[...]
        dprev, dnext = _neighbour_diffs(cur, prev_ref, next_ref, pl.program_id(1), nt)
        o_ref[...] = cur + mp_ref[...] * dprev + mn_ref[...] * dnext

    specs = _shift_specs(t)
    vec = pl.BlockSpec((1, w), lambda b, i: (0, 0))
    return _call(name, body, (bsz, nt), [src, src, src, mu_prev, mu_next], specs + [vec, vec], specs[0],
                 jax.ShapeDtypeStruct((bsz, t, w), F32))


def _shift_bwd(name, dps, p, mu_prev, mu_next):
    bsz, t, _ = dps.shape
    nt, w = t // SHIFT_TILE, SHIFT_PAD

    def body(d_ref, dprev_ref, dnext_ref, p_ref, pprev_ref, pnext_ref, mp_ref, mn_ref, o_ref, da_ref, db_ref):
        b, i = pl.program_id(0), pl.program_id(1)
        dcur = d_ref[...]
        d_dprev, d_dnext = _neighbour_diffs(dcur, dprev_ref, dnext_ref, i, nt)
        o_ref[...] = (dcur + mn_ref[...] * d_dprev + mp_ref[...] * d_dnext).astype(o_ref.dtype)
        p_dprev, p_dnext = _neighbour_diffs(p_ref[...], pprev_ref, pnext_ref, i, nt)

        @pl.when((b == 0) & (i == 0))
        def _():
            da_ref[...] = jnp.zeros_like(da_ref)
            db_ref[...] = jnp.zeros_like(db_ref)

        da_ref[...] += jnp.sum(dcur * p_dprev, axis=0, keepdims=True)
        db_ref[...] += jnp.sum(dcur * p_dnext, axis=0, keepdims=True)

    specs = _shift_specs(t)
    vec = pl.BlockSpec((1, w), lambda b, i: (0, 0))
    return _call(name, body, (bsz, nt), [dps, dps, dps, p, p, p, mu_prev, mu_next], specs + specs + [vec, vec],
                 [specs[0], vec, vec],
                 [jax.ShapeDtypeStruct((bsz, t, w), BF16), jax.ShapeDtypeStruct((1, w), F32),
                  jax.ShapeDtypeStruct((1, w), F32)])


CONV_BLK = 128


def _halo_specs(t, tt, w):
    r16 = tt // 16
    return [pl.BlockSpec((None, tt, w), lambda b, i: (b, i, 0)),
            pl.BlockSpec((None, 16, w), lambda b, i: (b, jnp.maximum(i * r16 - 1, 0), 0)),
            pl.BlockSpec((None, 16, w), lambda b, i: (b, jnp.minimum((i + 1) * r16, t // 16 - 1), 0))]


def _fill_pad(pad_ref, cur_ref, prev_ref, next_ref, i, nt, tt):
    pad_ref[0:16, :] = jnp.where(i > 0, prev_ref[...], 0.0)
    pad_ref[16:16 + tt, :] = cur_ref[...]
    pad_ref[16 + tt:32 + tt, :] = jnp.where(i < nt - 1, next_ref[...], 0.0)


def _dwconv(name, u, dw32, bias, flip, tt=512):
    bsz, t, w = u.shape
    tt = min(tt, t)
    nt = t // tt

    def body(cur_ref, prev_ref, next_ref, dw_ref, b_ref, o_ref, pad_ref):
        i = pl.program_id(1)
        _fill_pad(pad_ref, cur_ref, prev_ref, next_ref, i, nt, tt)
        for r0 in range(0, tt, CONV_BLK):
            for cs in (slice(c0, c0 + CONV_BLK) for c0 in range(0, w, CONV_BLK)):
                acc = jnp.broadcast_to(b_ref[:, cs], (CONV_BLK, CONV_BLK))
                for k in range(CONV_K):
                    kk = CONV_K - 1 - k if flip else k
                    acc = acc + pad_ref[pl.ds(r0 + 1 + k, CONV_BLK), cs] * dw_ref[kk:kk + 1, cs]
                o_ref[r0:r0 + CONV_BLK, cs] = acc

    return _call(name, body, (bsz, nt), [u, u, u, dw32, bias],
                 _halo_specs(t, tt, w) + [pl.BlockSpec((32, w), lambda b, i: (0, 0)),
                                          pl.BlockSpec((1, w), lambda b, i: (0, 0))],
                 pl.BlockSpec((None, tt, w), lambda b, i: (b, i, 0)), jax.ShapeDtypeStruct((bsz, t, w), F32),
                 [pltpu.VMEM((tt + 32, w), F32)])


def _dwconv_dw(name, u, dc, tt=512):
    bsz, t, w = u.shape
    tt = min(tt, t)
    nt = t // tt

    def body(cur_ref, prev_ref, next_ref, dc_ref, ddw_ref, db_ref, pad_ref):
        b, i = pl.program_id(0), pl.program_id(1)
        _fill_pad(pad_ref, cur_ref, prev_ref, next_ref, i, nt, tt)

        @pl.when((b == 0) & (i == 0))
        def _():
            ddw_ref[...] = jnp.zeros_like(ddw_ref)
            db_ref[...] = jnp.zeros_like(db_ref)

        dcv = dc_ref[...]
        db_ref[...] += jnp.sum(dcv, axis=0, keepdims=True)
        for k in range(CONV_K):
            ddw_ref[k:k + 1, :] += jnp.sum(dcv * pad_ref[pl.ds(1 + k, tt), :], axis=0, keepdims=True)

    return _call(name, body, (bsz, nt), [u, u, u, dc],
                 _halo_specs(t, tt, w) + [pl.BlockSpec((None, tt, w), lambda b, i: (b, i, 0))],
                 [pl.BlockSpec((32, w), lambda b, i: (0, 0)), pl.BlockSpec((1, w), lambda b, i: (0, 0))],
                 [jax.ShapeDtypeStruct((32, w), F32), jax.ShapeDtypeStruct((1, w), F32)],
                 [pltpu.VMEM((tt + 32, w), F32)])


PAIR = 2 * HEAD
N_PAIRS = RW // PAIR


def _chunk_pairs(s, r, lw, k, v, kk, a, sgn, tm_known=None):
    n, m = CHUNK, 2 * CHUNK
    in_a = lax.broadcasted_iota(jnp.int32, (n, PAIR), 1) < HEAD

    def stack2(z):
        return jnp.concatenate([jnp.where(in_a, z, 0.0), jnp.where(in_a, 0.0, z)], axis=0)

    def each(f, *lists):
        return [f(*z) for z in zip(*lists)]

    sgn_f = sgn.astype(F32)
    row2 = lax.broadcasted_iota(jnp.int32, (m, m), 0)
    col2 = lax.broadcasted_iota(jnp.int32, (m, m), 1)
    same = (row2 >= n) == (col2 >= n)
    dlt = ((row2 & (n - 1)) - (col2 & (n - 1))) * sgn
    incl, strict = same & (dlt >= 0), same & (dlt > 0)
    eye = jnp.where(row2 == col2, 1.0, 0.0)

    cum = each(lambda lw_: _dir_cumsum(lw_, sgn_f), lw)
    tot = each(lambda lw_: jnp.sum(lw_, axis=0, keepdims=True), lw)
    e_neg = each(lambda c_: jnp.exp(-c_), cum)
    e_rest = each(lambda t_, c_: jnp.exp(t_ - c_), tot, cum)
    beta = each(lambda kk_, a_: kk_ * a_, kk, a)
    lhs = each(lambda kk_, c_, lw_, r_: jnp.concatenate(
        [stack2(-kk_ * jnp.exp(c_ - lw_)), stack2(r_ * jnp.exp(c_))], axis=0), kk, cum, lw, r)
    rhs = each(lambda b_, k_, e_: jnp.concatenate([stack2(b_ * e_), stack2(k_ * e_)], axis=0), beta, k, e_neg)
    sc = each(lambda l_, r_: _dot3(l_, r_, "nt"), lhs, rhs)
    l_ab = each(lambda sc_: jnp.where(strict, sc_[0:m, 0:m], 0.0), sc)
    l_ak = each(lambda sc_: jnp.where(strict, sc_[0:m, m:2 * m], 0.0), sc)
    m_r = each(lambda sc_: jnp.where(jnp.concatenate([incl, incl], axis=1), sc_[m:2 * m, :], 0.0), sc)
    tm = _tri_inv(l_ab, eye) if tm_known is None else _tri_inv_known(l_ab, tm_known)
    z = _dot1_two(lhs, s, "nt")
    v2 = each(stack2, v)
    u2 = _dot1_two(tm, each(lambda z_, lv_: z_[0:m] + lv_, z, _dot1_two(l_ak, v2)))
    uv = each(lambda u_, v_: jnp.concatenate([u_, v_], axis=0), u2, v2)
    y2 = each(lambda z_, mu_: z_[m:2 * m] + mu_, z, _dot1_two(m_r, uv))
    bk = each(lambda b_, k_, e_: jnp.concatenate([stack2(b_ * e_), stack2(k_ * e_)], axis=0), beta, k, e_rest)
    s_new = each(lambda s_, t_, d_: s_ * jnp.exp(t_) + d_, s, tot, _dot1_two(uv, bk, "tn"))
    return each(lambda y_: y_[0:n] + y_[n:m], y2), s_new, tm


SCAN_SEQS = 4
N_CHAINS = SCAN_SEQS * N_PAIRS


def _pair_tiles(ref):
    return [ref[q, :, p * PAIR:(p + 1) * PAIR] for q in range(SCAN_SEQS) for p in range(N_PAIRS)]


def _store_tiles(ref, tiles):
    for q in range(SCAN_SEQS):
        for p in range(N_PAIRS):
            ref[q, :, p * PAIR:(p + 1) * PAIR] = tiles[q * N_PAIRS + p]


def _scan_specs(order):
    shared = pl.BlockSpec((SCAN_SEQS, CHUNK, RW), lambda d, b, c: (b, order(d, c), 0))
    per_dir = pl.BlockSpec((SCAN_SEQS, CHUNK, RW), lambda d, b, c: (b, order(d, c), d))
    state = pl.BlockSpec((None, SCAN_SEQS, None, N_PAIRS, PAIR, PAIR), lambda d, b, c: (d, b, order(d, c), 0, 0, 0))
    return shared, per_dir, state


def _scan_fwd(r, v, kk, lw, kd, a, bsz, exch=None):
    n = r.shape[0]
    t = n // bsz
    nc = t // CHUNK

    def order(d, c):
        return c + d * (nc - 1 - 2 * c)

    def body(r_ref, v_ref, kk_ref, lw_ref, kd_ref, a_ref, y_ref, s0_ref, tm_ref, s_ref):
        d, c = pl.program_id(0), pl.program_id(2)

        @pl.when(c == 0)
        def _():
            s_ref[...] = jnp.zeros_like(s_ref)

        s = [s_ref[i] for i in range(N_CHAINS)]
        y, s_new, tm = _chunk_pairs(s, *[_pair_tiles(ref) for ref in (r_ref, lw_ref, kd_ref, v_ref, kk_ref, a_ref)],
                                    1 - 2 * d)
        _store_tiles(y_ref, y)
        for i in range(N_CHAINS):
            s0_ref[i // N_PAIRS, i % N_PAIRS] = s[i]
            tm_ref[i // N_PAIRS, i % N_PAIRS] = tm[i].astype(BF16)
            s_ref[i] = s_new[i]

    shared, per_dir, state = _scan_specs(order)
    seq = lambda z: z.reshape(bsz, t, z.shape[1])
    res = _call("scan_fwd", body, (2, bsz // SCAN_SEQS, nc), [seq(z) for z in (r, v, kk, lw, kd, a)],
                [shared, shared, shared, per_dir, per_dir, per_dir], [per_dir, state, state],
                [jax.ShapeDtypeStruct((bsz, t, 2 * RW), F32),
                 jax.ShapeDtypeStruct((2, bsz, nc, N_PAIRS, PAIR, PAIR), F32),
                 jax.ShapeDtypeStruct((2, bsz, nc, N_PAIRS, PAIR, PAIR), BF16)],
                [pltpu.VMEM((N_CHAINS, PAIR, PAIR), F32)], exch)
    (y, s0, tm), got = res if exch else (res, None)
    y = y.reshape(n, 2 * RW)
    return ([y, s0, tm], got) if exch else [y, s0, tm]


def _scan_bwd(r, v, kk, lw, kd, a, s0, tm, dy, bsz, exch=None):
    n = r.shape[0]
    t = n // bsz
    nc = t // CHUNK

    def order(d, c):
        cc = nc - 1 - c
        return cc + d * (nc - 1 - 2 * cc)

    def body(r_ref, v_ref, kk_ref, lw_ref, kd_ref, a_ref, dy_ref, s0_ref, tm_ref,
             dr_ref, dv_ref, dkk_ref, dlw_ref, dkd_ref, da_ref, ds_ref):
        d, c = pl.program_id(0), pl.program_id(2)

        @pl.when(c == 0)
        def _():
            ds_ref[...] = jnp.zeros_like(ds_ref)

        sgn = 1 - 2 * d
        tm_known = [tm_ref[i // N_PAIRS, i % N_PAIRS].astype(F32) for i in range(N_CHAINS)]
        _, pull = jax.vjp(lambda *ops: _chunk_pairs(*ops, sgn, tm_known)[:2],
                          [s0_ref[i // N_PAIRS, i % N_PAIRS] for i in range(N_CHAINS)],
                          *[_pair_tiles(ref) for ref in (r_ref, lw_ref, kd_ref, v_ref, kk_ref, a_ref)])
        grads = pull((_pair_tiles(dy_ref), [ds_ref[i] for i in range(N_CHAINS)]))
        for i in range(N_CHAINS):
            ds_ref[i] = grads[0][i]
        for o_ref, gx in zip((dr_ref, dlw_ref, dkd_ref, dv_ref, dkk_ref, da_ref), grads[1:]):
            _store_tiles(o_ref, gx)

    shared, per_dir, state = _scan_specs(order)
    shp = jax.ShapeDtypeStruct((bsz, t, 2 * RW), F32)
    seq = lambda z: z.reshape(bsz, t, z.shape[1])
    res = _call("scan_bwd", body, (2, bsz // SCAN_SEQS, nc), [seq(z) for z in (r, v, kk, lw, kd, a, dy)] + [s0, tm],
                [shared, shared, shared, per_dir, per_dir, per_dir, per_dir, state, state],
                [per_dir] * 6, [shp] * 6, [pltpu.VMEM((N_CHAINS, PAIR, PAIR), F32)], exch)
    outs, got = res if exch else (res, None)
    outs = [z.reshape(n, 2 * RW) for z in outs]
    return (outs, got) if exch else outs


def _prep_fn(ps, w0, w2bd, a0, a2bd, g2p, k_k, k_a, hsum):
    head_sum = lambda z: _head_sum(z, hsum)
    r, k, v = ps[:, 0:RW], ps[:, RW:2 * RW], ps[:, 2 * RW:3 * RW]
    lora = 2 * HEAD
    wd, ad, gd = (ps[:, 3 * RW:3 * RW + lora], ps[:, 3 * RW + lora:3 * RW + 2 * lora],
                  ps[:, 3 * RW + 2 * lora:SHIFT_PAD])
    logw = -DECAY_SCALE * _sigmoid(_dot1(jnp.tanh(wd), w2bd) + w0)
    a = _sigmoid(_dot1(ad, a2bd) + a0)
    g = _dot1(_sigmoid(gd), g2p)
    kkr = k * k_k
    kk = kkr / jnp.maximum(jnp.sqrt(head_sum(kkr * kkr)), NORM_EPS)
    k2 = jnp.concatenate([k, k], axis=1)
    ka2 = jnp.concatenate([k_a, k_a], axis=1)
    kd = k2 * (1.0 + (a - 1.0) * ka2)
    return r, v, kk, logw, a, kd, g


def _post_fn(y2, r, v, kd, g, lnx_g, lnx_b, r_k, hsum):
    head_sum = lambda z: _head_sum(z, hsum)
    y = y2[:, 0:RW] + y2[:, RW:2 * RW]
    mu = head_sum(y) * (1.0 / HEAD)
    yc = y - mu
    var = head_sum(yc * yc) * (1.0 / HEAD)
    yn = yc * lax.rsqrt(var + GN_EPS) * lnx_g + lnx_b
    bonus = head_sum(r * (kd[:, 0:RW] + kd[:, RW:2 * RW]) * r_k) * v
    return ((yn + bonus) * g,)


def _glu_fn(pa, pb):
    return (pa * _sigmoid(pb),)


def _conv_out_fn(cv, ln_g, ln_b):
    mu = jnp.mean(cv, axis=-1, keepdims=True)
    cc = cv - mu
    var = jnp.mean(cc * cc, axis=-1, keepdims=True)
    y = cc * lax.rsqrt(var + LN_EPS) * ln_g + ln_b
    return (y * _sigmoid(y),)


def _local_step(x, tgt, w, ex=None):
    bsz, t, d = x.shape
    n = bsz * t
    x2d, tgt2d = x.reshape(n, d), tgt.reshape(n, d)
    hsum = jnp.tile(jnp.kron(jnp.eye(N_HEADS, dtype=BF16), jnp.ones((HEAD, HEAD), BF16)), (2, 1))
    w = dict(w)
    parts = {} if ex else None

    def hosted(result, finish=None):
        if not ex:
            return result
        outs, got = result
        if finish is not None:
            w.update(finish(got))
        return outs

    xb = x2d.astype(BF16)
    hg1, hu1, act1 = hosted(_ffn_in("ffn1_in", xb, w["ffn1_w_in"], exch=(ex["g1"][0], True) if ex else None),
                            ex["g1"][1] if ex else None)
    x1, x1b, z1 = _mm_ln("ffn1_out_ln1", act1, w["ffn1_w_out"], x2d, w["ln1_g"], w["ln1_b"], 0.5, a_blocked=True)
    p = _mm("w_in_proj", x1b, w["w_in"], "nn", F32, tm=1024)
    p3 = p.reshape(bsz, t, IN_PAD)
    ps = _shift("shift_fwd", p3, w["mu_prev"], w["mu_next"]).reshape(n, SHIFT_PAD)
    prep_params = [w["w0"], w["w2"], w["a0"], w["a2"], w["g2"], w["k_k"], w["k_a"], hsum]
    r, v, kk, logw, a, kd, g = _rowwise(
        "rwkv_prep", _prep_fn, [ps], prep_params,
        [(RW, F32), (RW, F32), (RW, F32), (2 * RW, F32), (2 * RW, F32), (2 * RW, F32), (RW, F32)], [])
    y2, s0, tm = hosted(_scan_fwd(r, v, kk, logw, kd, a, bsz, exch=(ex["g2"][0], True) if ex else None),
                    ex["g2"][1] if ex else None)
    post_params = [w["lnx_g"], w["lnx_b"], w["r_k"], hsum]
    (y_rwkv,) = _rowwise("rwkv_post", _post_fn, [y2, r, v, kd, g], post_params, [(RW, BF16)], [])
    (u,) = _rowwise("conv_glu", _glu_fn, [(p, CW, 4), (p, CW, 5)], [], [(CW, F32)], [])
    cv = _dwconv("conv_dw", u.reshape(bsz, t, CW), w["conv_dw"], w["conv_b"], False).reshape(n, CW)
    (y_conv,) = _rowwise("conv_out", _conv_out_fn, [cv], [w["conv_ln_g"], w["conv_ln_b"]], [(CW, BF16)], [])
    ycat = jnp.concatenate([y_rwkv, y_conv], axis=1)
    x2, x2b, z2 = _mm_ln("w_out_ln2", ycat, w["w_out"], x1, w["ln2_g"], w["ln2_b"], 1.0)
    hg2, hu2, act2 = _ffn_in("ffn2_in", x2b, w["ffn2_w_in"])
    gr = {}
    dz3, dz3b, loss, gr["ln3_g"], gr["ln3_b"] = _mm_ln(
        "ffn2_out_ln3", act2, w["ffn2_w_out"], x2, w["ln3_g"], w["ln3_b"], 0.5, tgt=tgt2d, a_blocked=True)

    dx2, _ = _ffn_bwd("ffn2", gr, dz3, dz3b, x2b, w["ffn2_w_in"], w["ffn2_w_out"], hg2, hu2, act2)
    dz2, dz2b, gr["ln2_g"], gr["ln2_b"] = _ln_bwd("ln2_bwd", z2, w["ln2_g"], dx2)
    gr["w_out"] = _mm("w_out_wgrad", ycat, dz2b, "tn", BF16)
    dycat = _mm("w_out_dgrad", dz2b, w["w_out"], "nt", F32)
    conv_out_bwd = _vjp_of(_conv_out_fn, 3)
    dcv, gr["conv_ln_g"], gr["conv_ln_b"] = _rowwise(
        "conv_out_bwd", lambda cv_, ct_, g_, b_: conv_out_bwd(cv_, g_, b_, ct_),
        [cv, (dycat, CW, 1)], [w["conv_ln_g"], w["conv_ln_b"]], [(CW, F32)], [(1, CW), (1, CW)])
    dcv3 = dcv.reshape(bsz, t, CW)
    gr["conv_dw"], gr["conv_b"] = _dwconv_dw("conv_dw_wgrad", u.reshape(bsz, t, CW), dcv3)
    du = _dwconv("conv_dw_dgrad", dcv3, w["conv_dw"], jnp.zeros((1, CW), F32), True).reshape(n, CW)

    def glu_bwd(pa, pb, ct):
        return (jnp.concatenate(_vjp_of(_glu_fn, 2)(pa, pb, ct), axis=1),)

    (dp_conv,) = _rowwise("conv_glu_bwd", glu_bwd, [(p, CW, 4), (p, CW, 5), du], [], [(2 * CW, BF16)], [])

    def post_bwd(y2_, r_, v_, kd_, g_, ct, lg, lb, rk, hs):
        return _vjp_of(lambda *z: _post_fn(*z, hs), 8)(y2_, r_, v_, kd_, g_, lg, lb, rk, ct)

    dy2, dr_post, dv_post, dkd_post, dg, gr["lnx_g"], gr["lnx_b"], gr["r_k"] = _rowwise(
        "rwkv_post_bwd", post_bwd, [y2, r, v, kd, g, (dycat, RW, 0)], post_params,
        [(2 * RW, F32), (RW, F32), (RW, F32), (2 * RW, F32), (RW, F32)], [(1, RW), (1, RW), (1, RW)])
    sends = [jnp.concatenate(gr["ffn2_w_in"], axis=0), gr["ffn2_w_out"].reshape(N_DEV, D_FF // N_DEV, D_MODEL),
             gr["w_out"].reshape(N_DEV, D_MODEL // N_DEV, D_MODEL)]
    res = _scan_bwd(r, v, kk, logw, kd, a, s0, tm, dy2, bsz, exch=(sends, False) if ex else None)
    if ex:
        res, got = res
        parts.update(zip(("ffn2_w_in", "ffn2_w_out", "w_out"), got))
    dr_s, dv_s, dkk_s, dlw, dkd_s, da = res

    def prep_bwd(ps_, dr2, dr1, dv2, dv1, dkk2, dlw_, da_, dkd2, dkd1, dg_, *prm):
        half = lambda z: z[:, 0:RW] + z[:, RW:2 * RW]
        return _vjp_of(lambda *z: _prep_fn(*z, prm[-1]), 8)(
            ps_, *prm[:-1], half(dr2) + dr1, half(dv2) + dv1, half(dkk2), dlw_, da_, dkd2 + dkd1, dg_)

    dps, gr["w0"], gr["w2"], gr["a0"], gr["a2"], gr["g2"], gr["k_k"], gr["k_a"] = _rowwise(
        "rwkv_prep_bwd", prep_bwd,
        [ps, dr_s, dr_post, dv_s, dv_post, dkk_s, dlw, da, dkd_s, dkd_post, dg], prep_params,
        [(SHIFT_PAD, F32)], [q.shape for q in prep_params[:-1]])
    dps3 = dps.reshape(bsz, t, SHIFT_PAD)
    dp_shift, gr["mu_prev"], gr["mu_next"] = _shift_bwd("shift_bwd", dps3, p3, w["mu_prev"], w["mu_next"])
    dp_shift = dp_shift.reshape(n, SHIFT_PAD)
    dp = jnp.concatenate([dp_shift, dp_conv], axis=1)
    gr["w_in"] = _mm("w_in_wgrad", x1b, dp, "tn", BF16)
    dx1 = _mm("w_in_dgrad", dp, w["w_in"], "nt", F32, add=dz2, add_scale=ALPHA, tm=1024)
    dz1, dz1b, gr["ln1_g"], gr["ln1_b"] = _ln_bwd("ln1_bwd", z1, w["ln1_g"], dx1)
    riders = None
    if ex:
        gw_in = _unpad_in_cols(gr["w_in"]).reshape(D_MODEL, N_DEV, IN_COLS // N_DEV).transpose(1, 0, 2)
        small = _pack([_grad_small(nm, gr[nm]) for nm in SMALL_SHARDED + SMALL_REPL])
        riders = {"out_bwd": [gw_in, jnp.broadcast_to(small[None], (N_DEV,) + small.shape)], "in_wgrad": []}
    grad_x, got = _ffn_bwd("ffn1", gr, dz1, dz1b, xb, w["ffn1_w_in"], w["ffn1_w_out"], hg1, hu1, act1, riders)
    if ex:
        parts.update(w_in=got["out_bwd"][0], small=got["out_bwd"][1], ffn1_w_out=got["in_wgrad"][0],
                     ffn1_w_in=got["in_dgrad"][0])
    return loss[0, 0], grad_x.reshape(bsz, t, d), gr, parts


def _ffn_in_wgrad(name, xin, dhg, dhu, tk=1024, exch=None):
    n = xin.shape[0]
    nj, nt = N_DEV // 2, n // tk

    def body(x_ref, g_ref, u_ref, og_ref, ou_ref, accg_ref, accu_ref):
        i = pl.program_id(1)

        @pl.when(i == 0)
        def _():
            accg_ref[...] = jnp.zeros_like(accg_ref)
            accu_ref[...] = jnp.zeros_like(accu_ref)

        xt = x_ref[...].astype(BF16).T
        accg_ref[...] += _dot(xt, g_ref[...])
        accu_ref[...] += _dot(xt, u_ref[...])

        @pl.when(i == nt - 1)
        def _():
            og_ref[...] = accg_ref[...].astype(og_ref.dtype)
            ou_ref[...] = accu_ref[...].astype(ou_ref.dtype)

    dh_blk = pl.BlockSpec((None, tk, FF_BLK), lambda j, i: (j, i, 0))
    o_blk = pl.BlockSpec((None, D_MODEL, FF_BLK), lambda j, i: (j, 0, 0))
    shp = jax.ShapeDtypeStruct((nj, D_MODEL, FF_BLK), BF16)
    return _call(name, body, (nj, nt), [xin, dhg, dhu],
                 [pl.BlockSpec((tk, D_MODEL), lambda j, i: (i, 0)), dh_blk, dh_blk], [o_blk, o_blk], [shp, shp],
                 [pltpu.VMEM((D_MODEL, FF_BLK), F32)] * 2, exch)


def _ffn_in_dgrad(name, dhg, dhu, wg, add, add_scale, tm=1024, exch=None):
    n = add.shape[0]
    nj = N_DEV // 2

    def body(g_ref, u_ref, wgate_ref, wup_ref, add_ref, o_ref, acc_ref):
        j = pl.program_id(1)

        @pl.when(j == 0)
        def _():
            acc_ref[...] = jnp.zeros_like(acc_ref)

        acc_ref[...] += _dot(g_ref[...], wgate_ref[...], "nt") + _dot(u_ref[...], wup_ref[...], "nt")

        @pl.when(j == nj - 1)
        def _():
            o_ref[...] = acc_ref[...] + add_scale * add_ref[...]

    dh_blk = pl.BlockSpec((None, tm, FF_BLK), lambda i, j: (j, i, 0))
    row = pl.BlockSpec((tm, D_MODEL), lambda i, j: (i, 0))
    return _call(name, body, (n // tm, nj), [dhg, dhu, wg, wg, add],
                 [dh_blk, dh_blk, pl.BlockSpec((None, D_MODEL, FF_BLK), lambda i, j: (j, 0, 0)),
                  pl.BlockSpec((None, D_MODEL, FF_BLK), lambda i, j: (j + nj, 0, 0)), row],
                 row, jax.ShapeDtypeStruct((n, D_MODEL), F32), [pltpu.VMEM((tm, D_MODEL), F32)], exch)


def _ffn_bwd(tag, gr, dz, dzb, xin, wg, wout, hg, hu, act, riders=None):
    own = riders is not None

    def hosted(result):
        return result if own else (result, None)

    (dhg, dhu, gwo), got_a = hosted(_ffn_out_bwd(
        tag + "_out_bwd", dzb, wout, hg, hu, act, exch=(riders["out_bwd"], False) if own else None))
    gr[tag + "_w_out"] = gwo
    send = ([gwo.reshape(N_DEV, D_FF // N_DEV, D_MODEL)] + riders["in_wgrad"], False) if own else None
    dw, got_b = hosted(_ffn_in_wgrad(tag + "_in_wgrad", xin, dhg, dhu, exch=send))
    gr[tag + "_w_in"] = dw
    send = ([jnp.concatenate(dw, axis=0)], False) if own else None
    dx, got_c = hosted(_ffn_in_dgrad(tag + "_in_dgrad", dhg, dhu, wg, dz, ALPHA, exch=send))
    return dx, {"out_bwd": got_a, "in_wgrad": got_b, "in_dgrad": got_c}


def _adam_math(g, w, m, v):
    m = ADAM_B1 * m + (1.0 - ADAM_B1) * g
    v = ADAM_B2 * v + (1.0 - ADAM_B2) * (g * g)
    m_hat = m / (1.0 - ADAM_B1 ** ADAM_STEP)
    v_hat = v / (1.0 - ADAM_B2 ** ADAM_STEP)
    delta = -ADAM_LR * (m_hat / (jnp.sqrt(v_hat) + ADAM_EPS) + ADAM_WD * w)
    return delta, m, v


def _adam(name, parts, w, m, v, tr=128):
    rows, cols = w.shape
    tr = min(tr, rows)
    while rows % tr:
        tr -= 8

    def body(p_ref, w_ref, m_ref, v_ref, g_ref, d_ref, mo_ref, vo_ref):
        g = p_ref[0].astype(F32)
        for s in range(1, N_DEV):
            g = g + p_ref[s].astype(F32)
        g_ref[...] = g
        d_ref[...], mo_ref[...], vo_ref[...] = _adam_math(g, w_ref[...], m_ref[...], v_ref[...])

    blk = pl.BlockSpec((tr, cols), lambda i: (i, 0))
    shp = jax.ShapeDtypeStruct((rows, cols), F32)
    return _call(name, body, (rows // tr,), [parts, w, m, v],
                 [pl.BlockSpec((N_DEV, tr, cols), lambda i: (0, i, 0)), blk, blk, blk], [blk] * 4, [shp] * 4)


def _sum8(name, parts):
    _, rows, cols = parts.shape

    def body(p_ref, o_ref):
        g = p_ref[0]
        for s in range(1, N_DEV):
            g = g + p_ref[s]
        o_ref[...] = g

    return pl.pallas_call(body, name=name, out_shape=jax.ShapeDtypeStruct((rows, cols), F32),
                          compiler_params=_params())(parts)


def _adam_small(name, g, w, m, v):
    def body(g_ref, w_ref, m_ref, v_ref, d_ref, mo_ref, vo_ref):
        d_ref[...], mo_ref[...], vo_ref[...] = _adam_math(g_ref[...], w_ref[...], m_ref[...], v_ref[...])

    shp = jax.ShapeDtypeStruct(g.shape, F32)
    return pl.pallas_call(body, name=name, out_shape=[shp] * 3, compiler_params=_params())(g, w, m, v)


def _pack(arrs, lane=128):
    flat = jnp.concatenate([a.reshape(-1).astype(F32) for a in arrs])
    pad = (-flat.shape[0]) % (8 * lane)
    return jnp.pad(flat, (0, pad)).reshape(-1, lane)


def _unpack(packed, shapes):
    flat, out, off = packed.reshape(-1), [], 0
    for s in shapes:
        sz = math.prod(s)
        out.append(flat[off:off + sz].reshape(s))
        off += sz
    return out


def _pad_in_cols(wfull):
    zeros = jnp.zeros((wfull.shape[0], SHIFT_PAD - SHIFT_COLS), wfull.dtype)
    return jnp.concatenate([wfull[:, :SHIFT_COLS], zeros, wfull[:, SHIFT_COLS:]], axis=1)


def _unpad_in_cols(gfull):
    return jnp.concatenate([gfull[:, :SHIFT_COLS], gfull[:, SHIFT_PAD:]], axis=1)


def _block_diag2(wd):
    z = jnp.zeros_like(wd[0])
    return jnp.concatenate([jnp.concatenate([wd[0], z], axis=1), jnp.concatenate([z, wd[1]], axis=1)], axis=0)


def _unblock_diag2(g):
    return jnp.stack([g[0:64, 0:RW], g[64:128, RW:2 * RW]])


SMALL_SHARDED = ("w0", "w2", "a0", "a2", "g2", "conv_dw")
SMALL_REPL = ("mu_prev", "mu_next", "k_k", "k_a", "r_k", "lnx_g", "lnx_b", "conv_b", "conv_ln_g", "conv_ln_b",
              "ln1_g", "ln1_b", "ln2_g", "ln2_b", "ln3_g", "ln3_b")
BIG = ("ffn1_w_in", "ffn1_w_out", "w_in", "w_out", "ffn2_w_in", "ffn2_w_out")
WEIGHTS = ("ffn1_w_in", "ffn1_w_out", "w_in", "mu_prev", "mu_next", "w0", "w2", "a0", "a2", "g2", "k_k", "k_a",
           "r_k", "lnx_g", "lnx_b", "conv_dw", "conv_b", "conv_ln_g", "conv_ln_b", "w_out", "ffn2_w_in",
           "ffn2_w_out", "ln1_g", "ln1_b", "ln2_g", "ln2_b", "ln3_g", "ln3_b")


def _full_small(name, full):
    if name in ("w0", "a0"):
        return full.reshape(1, 2 * RW)
    if name in ("w2", "a2"):
        return _block_diag2(full)
    if name == "g2":
        return jnp.pad(full, ((0, 256 - GATE_LORA), (0, 0)))
    if name == "conv_dw":
        return jnp.pad(full, ((0, 1), (0, 0)))
    if name in ("mu_prev", "mu_next"):
        return jnp.pad(full.reshape(1, SHIFT_COLS), ((0, 0), (0, SHIFT_PAD - SHIFT_COLS)))
    return full.reshape(1, -1)


def _grad_small(name, g):
    if name in ("w0", "a0"):
        return g.reshape(2, RW)
    if name in ("w2", "a2"):
        return _unblock_diag2(g)
    if name == "g2":
        return g[:GATE_LORA]
    if name == "conv_dw":
        return g[:CONV_K]
    if name in ("mu_prev", "mu_next"):
        return g[0, :SHIFT_COLS]
    if name == "r_k":
        return g.reshape(N_HEADS, HEAD)
    return g.reshape(-1)


def kernel(x, ffn1_w_in, ffn1_w_out, w_in, mu_prev, mu_next, w0, w2, a0, a2, g2, k_k, k_a, r_k, lnx_g, lnx_b, conv_dw, conv_b, conv_ln_g, conv_ln_b, w_out, ffn2_w_in, ffn2_w_out, ln1_g, ln1_b, ln2_g, ln2_b, ln3_g, ln3_b, loss_target, m_ffn1_w_in, m_ffn1_w_out, m_w_in, m_mu_prev, m_mu_next, m_w0, m_w2, m_a0, m_a2, m_g2, m_k_k, m_k_a, m_r_k, m_lnx_g, m_lnx_b, m_conv_dw, m_conv_b, m_conv_ln_g, m_conv_ln_b, m_w_out, m_ffn2_w_in, m_ffn2_w_out, m_ln1_g, m_ln1_b, m_ln2_g, m_ln2_b, m_ln3_g, m_ln3_b, v_ffn1_w_in, v_ffn1_w_out, v_w_in, v_mu_prev, v_mu_next, v_w0, v_w2, v_a0, v_a2, v_g2, v_k_k, v_k_a, v_r_k, v_lnx_g, v_lnx_b, v_conv_dw, v_conv_b, v_conv_ln_g, v_conv_ln_b, v_w_out, v_ffn2_w_in, v_ffn2_w_out, v_ln1_g, v_ln1_b, v_ln2_g, v_ln2_b, v_ln3_g, v_ln3_b):
    args = dict(locals())
    drop = lambda z: z.reshape(z.shape[1:])
    wsh = {n: drop(args[n]) for n in WEIGHTS}
    msh = {n: drop(args["m_" + n]) for n in WEIGHTS}
    vsh = {n: drop(args["v_" + n]) for n in WEIGHTS}
    me = 4 * lax.axis_index("x") + 2 * lax.axis_index("y") + lax.axis_index("c")
    bf = {n: wsh[n].astype(BF16) for n in BIG}

    w = {"ffn1_w_in": _gather_two_level("gather_ffn1_w_in", bf["ffn1_w_in"])}
    for n in SMALL_REPL:
        w[n] = _full_small(n, wsh[n])
    small_shapes = [wsh[n].shape for n in SMALL_SHARDED]

    def finish1(got):
        f1_out, w_in_g, small = got
        cols = zip(*[_unpack(small[dv], small_shapes) for dv in range(N_DEV)])
        out = {n: _full_small(n, jnp.concatenate(s, axis=-1)) for n, s in zip(SMALL_SHARDED, cols)}
        out["ffn1_w_out"] = f1_out.reshape(D_FF, D_MODEL)
        out["w_in"] = _pad_in_cols(w_in_g.transpose(1, 0, 2).reshape(D_MODEL, IN_COLS))
        return out

    def finish2(got):
        w_out_g, f2_in, f2_out = got
        return {"w_out": w_out_g.reshape(D_MODEL, D_MODEL), "ffn2_w_in": f2_in,
                "ffn2_w_out": f2_out.reshape(D_FF, D_MODEL)}

    ex = {"g1": ([bf["ffn1_w_out"], bf["w_in"], _pack([wsh[n] for n in SMALL_SHARDED])], finish1),
          "g2": ([bf["w_out"], bf["ffn2_w_in"], bf["ffn2_w_out"]], finish2)}
    loss_part, grad_x, gr, parts = _local_step(x, loss_target, w, ex)
    loss = lax.psum(loss_part, ("x", "y", "c"))

    out = {n: _adam("adam_" + n, parts[n], wsh[n], msh[n], vsh[n]) for n in BIG}
    small_names = SMALL_SHARDED + SMALL_REPL
    full_shapes = [_grad_small(n, gr[n]).shape for n in small_names]
    summed = _unpack(_sum8("sum_small_grads", parts["small"]), full_shapes)
    mine = []
    for n, g in zip(small_names, summed):
        if n in SMALL_SHARDED:
            g = lax.dynamic_slice_in_dim(g, me * HEAD, HEAD, axis=g.ndim - 1)
        mine.append(g)
    shapes = [g.shape for g in mine]
    d_s, m_s, v_s = _adam_small("adam_small", _pack(mine), _pack([wsh[n] for n in small_names]),
                                _pack([msh[n] for n in small_names]), _pack([vsh[n] for n in small_names]))
    for n, g, dl, mn, vn in zip(small_names, mine, _unpack(d_s, shapes), _unpack(m_s, shapes), _unpack(v_s, shapes)):
        out[n] = (g, dl, mn, vn)

    res = [loss, grad_x]
    for k in range(4):
        res += [out[n][k].reshape((1,) + out[n][k].shape) for n in WEIGHTS]
    return tuple(res)
```

```python
import functools
import math

import jax
import jax.numpy as jnp
from jax import lax
from jax.experimental import pallas as pl
from jax.experimental.pallas import tpu as pltpu

F32 = jnp.float32
BF16 = jnp.bfloat16

N_DEV = 8
D_MODEL = 1024
RW = 512
N_HEADS = 8
HEAD = 64
CW = 512
CONV_K = 31
D_FF = 2816
FF_BLK = 704
GATE_LORA = 160
SHIFT_COLS = 1952
SHIFT_PAD = 2048
IN_COLS = 2976
IN_PAD = 3072
LN_EPS = 1e-5
GN_EPS = 64e-5
NORM_EPS = 1e-12
ALPHA = 2.0 ** 0.25
DECAY_SCALE = math.exp(-0.5)
CHUNK = 64
ADAM_LR, ADAM_B1, ADAM_B2, ADAM_EPS, ADAM_WD, ADAM_STEP = 0.001, 0.9, 0.999, 1e-8, 0.01, 10
VMEM_LIMIT = 56 * 1024 * 1024
MXU_DIM = 256

_DN = {"nn": (((1,), (0,)), ((), ())), "nt": (((1,), (1,)), ((), ())), "tn": (((0,), (0,)), ((), ()))}


def _params():
    return pltpu.CompilerParams(vmem_limit_bytes=VMEM_LIMIT)


def _dot(a, b, dims="nn"):
    return lax.dot_general(a, b, _DN[dims], preferred_element_type=F32)


def _split(x):
    hi = x.astype(BF16)
    return hi, (x - hi.astype(F32)).astype(BF16)


def _dot3_impl(a, b, dims):
    ah, al = _split(a)
    bh, bl = _split(b)
    ka, kb = _DN[dims][0][0][0], _DN[dims][0][1][0]
    return _dot(jnp.concatenate([ah, ah, al], axis=ka), jnp.concatenate([bh, bl, bh], axis=kb), dims)


@functools.partial(jax.custom_vjp, nondiff_argnums=(2,))
def _dot3(a, b, dims="nn"):
    return _dot3_impl(a, b, dims)


def _dot3_fwd(a, b, dims):
    return _dot3_impl(a, b, dims), (a, b)


def _dot2_ct(x, y, dims, ct_left):
    ka, kb = _DN[dims][0][0][0], _DN[dims][0][1][0]
    if ct_left:
        c, (h, l) = x.astype(BF16), _split(y)
        return _dot(jnp.concatenate([c, c], axis=ka), jnp.concatenate([h, l], axis=kb), dims)
    (h, l), c = _split(x), y.astype(BF16)
    return _dot(jnp.concatenate([h, l], axis=ka), jnp.concatenate([c, c], axis=kb), dims)


def _dot3_bwd(dims, res, ct):
    a, b = res
    if dims == "nn":
        return _dot2_ct(ct, b, "nt", True), _dot2_ct(a, ct, "tn", False)
    if dims == "nt":
        return _dot2_ct(ct, b, "nn", True), _dot2_ct(ct, a, "tn", True)
    return _dot2_ct(b, ct, "nt", False), _dot2_ct(a, ct, "nn", False)


_dot3.defvjp(_dot3_fwd, _dot3_bwd)


def _dot1_impl(a, b, dims):
    return _dot(a.astype(BF16), b.astype(BF16), dims)


@functools.partial(jax.custom_vjp, nondiff_argnums=(2,))
def _dot1(a, b, dims="nn"):
    return _dot1_impl(a, b, dims)


def _dot1_bwd(dims, res, ct):
    a, b = res
    if dims == "nn":
        return _dot1_impl(ct, b, "nt"), _dot1_impl(a, ct, "tn")
    if dims == "nt":
        return _dot1_impl(ct, b, "nn"), _dot1_impl(ct, a, "tn")
    return _dot1_impl(b, ct, "nt"), _dot1_impl(a, ct, "nn")


_dot1.defvjp(lambda a, b, dims: (_dot1_impl(a, b, dims), (a, b)), _dot1_bwd)


def _dot1_two(a, b, dims="nn"):
    ax_a, ax_b = {"nn": (0, 1), "nt": (0, 0), "tn": (1, 1)}[dims]
    shallow = dims != "tn" and 2 * a[0].shape[1] <= MXU_DIM
    out = []
    for i in range(0, len(a), 2):
        if shallow:
            z = jnp.zeros_like(b[i])
            bd = jnp.concatenate([jnp.concatenate([b[i], z], axis=1), jnp.concatenate([z, b[i + 1]], axis=1)], axis=0)
            r = _dot1(jnp.concatenate(a[i:i + 2], axis=1), bd, dims)
            n = r.shape[1] // 2
            out += [r[:, :n], r[:, n:]]
        else:
            r = _dot1(jnp.concatenate(a[i:i + 2], axis=ax_a), jnp.concatenate(b[i:i + 2], axis=ax_b), dims)
            m, n = r.shape[0] // 2, r.shape[1] // 2
            out += [r[:m, :n], r[m:, n:]]
    return out


def _tri_inv_impl(l, eye):
    steps = int(math.log2(CHUNK)) - 1
    m = l[0].shape[0]
    tm = [eye + x for x in l]
    lp = _dot1_two(l, l)
    for k in range(steps):
        if k < steps - 1:
            both = _dot1_two([jnp.concatenate([t, p], axis=0) for t, p in zip(tm, lp)], lp)
            tm = [t + b[:m] for t, b in zip(tm, both)]
            lp = [b[m:] for b in both]
        else:
            tm = [t + x for t, x in zip(tm, _dot1_two(tm, lp))]
    return tm


@jax.custom_vjp
def _tri_inv(l, eye):
    return _tri_inv_impl(l, eye)


def _tri_inv_fwd(l, eye):
    tm = _tri_inv_impl(l, eye)
    return tm, (tm, eye)


def _tri_inv_bwd(res, ct):
    tm, eye = res
    return _dot1_two(_dot1_two(tm, ct, "tn"), tm, "nt"), jnp.zeros_like(eye)


_tri_inv.defvjp(_tri_inv_fwd, _tri_inv_bwd)


@jax.custom_vjp
def _tri_inv_known(l, tm):
    return tm


_tri_inv_known.defvjp(lambda l, tm: (tm, tm),
                      lambda tm, ct: (_dot1_two(_dot1_two(tm, ct, "tn"), tm, "nt"), [jnp.zeros_like(t) for t in tm]))


def _ones_impl(x, g2):
    x1 = x.astype(BF16)
    x2 = (x - x1.astype(F32)).astype(BF16)
    return _dot(jnp.concatenate([x1, x2], axis=1), g2)


@jax.custom_vjp
def _head_sum(x, g2):
    return _ones_impl(x, g2)


_head_sum.defvjp(lambda x, g2: (_ones_impl(x, g2), g2), lambda g2, ct: (_ones_impl(ct, g2), jnp.zeros_like(g2)))


def _prefix_sum(x):
    row = lax.broadcasted_iota(jnp.int32, x.shape, 0)
    sh = 1
    while sh < x.shape[0]:
        x = x + jnp.where(row >= sh, pltpu.roll(x, sh, 0), 0.0)
        sh *= 2
    return x


def _dir_cumsum_impl(x, sgn):
    pre = _prefix_sum(x)
    return jnp.where(sgn > 0.0, pre, jnp.sum(x, axis=0, keepdims=True) - pre + x)


@jax.custom_vjp
def _dir_cumsum(x, sgn):
    return _dir_cumsum_impl(x, sgn)


_dir_cumsum.defvjp(lambda x, sgn: (_dir_cumsum_impl(x, sgn), sgn),
                   lambda sgn, ct: (_dir_cumsum_impl(ct, -sgn), jnp.zeros_like(sgn)))


def _sigmoid(x):
    return 1.0 / (1.0 + jnp.exp(-x))


def _mesh_pos():
    return lax.axis_index("x"), lax.axis_index("y"), lax.axis_index("c")


def _peer(pos, q):
    x, y, c = pos
    return (1 - x if q & 4 else x, 1 - y if q & 2 else y, 1 - c if q & 1 else c)


def _linear(pos):
    return 4 * pos[0] + 2 * pos[1] + pos[2]


def _exchange_copies(x_refs, o_refs, send_sems, recv_sems, local_sems, gather):
    pos = _mesh_pos()
    me = _linear(pos)
    starts, wait_recv, wait_send, wait_local = [], [], [], []
    for t in range(len(x_refs)):
        src = x_refs[t] if gather else x_refs[t].at[me]
        cp = pltpu.make_async_copy(src, o_refs[t].at[me], local_sems.at[t])
        starts.append(cp.start)
        wait_local.append(cp.wait)
    for q in range(1, N_DEV):
        peer = _peer(pos, q)
        for t in range(len(x_refs)):
            src = x_refs[t] if gather else x_refs[t].at[_linear(peer)]
            sems = dict(send_sem=send_sems.at[t, q - 1], recv_sem=recv_sems.at[t, q - 1],
                        device_id=peer, device_id_type=pl.DeviceIdType.MESH)
            send = pltpu.make_async_remote_copy(src_ref=src, dst_ref=o_refs[t].at[me], **sems)
            recv = pltpu.make_async_remote_copy(src_ref=src, dst_ref=o_refs[t].at[_linear(peer)], **sems)
            starts.append(send.start)
            wait_recv.append(recv.wait_recv)
            wait_send.append(send.wait_send)
    return starts, wait_recv + wait_send + wait_local


def _exchange_shapes(xs, gather):
    return [jax.ShapeDtypeStruct((N_DEV,) + (x.shape if gather else x.shape[1:]), x.dtype) for x in xs]


def _exchange_sems(nt):
    return [pltpu.SemaphoreType.DMA((nt, N_DEV - 1)), pltpu.SemaphoreType.DMA((nt, N_DEV - 1)),
            pltpu.SemaphoreType.DMA((nt,))]


def _gather_two_level(name, x):
    def body(x_ref, out_ref, send_sems, recv_sems, local_sem):
        px, py, pc = _mesh_pos()
        me, sibling = (px, py, pc), (px, py, 1 - pc)
        chips = [(1 - px, py), (px, 1 - py), (1 - px, 1 - py)]

        def slot(pos):
            return out_ref.at[_linear(pos)]

        def copy(k, block, to, src=None):
            return pltpu.make_async_remote_copy(
                src_ref=slot(block) if src is None else src, dst_ref=slot(block), send_sem=send_sems.at[k],
                recv_sem=recv_sems.at[k], device_id=to, device_id_type=pl.DeviceIdType.MESH)

        mine = pltpu.make_async_copy(x_ref, slot(me), local_sem)
        mine.start()
        first = [copy(0, me, sibling, src=x_ref)]
        first += [copy(1 + j, me, (*chip, pc), src=x_ref) for j, chip in enumerate(chips)]
        for cp in first:
            cp.start()
        passed = [copy(4 + j, (*chip, pc), sibling) for j, chip in enumerate(chips)]
        for j, chip in enumerate(chips):
            copy(1 + j, (*chip, pc), me).wait_recv()
            passed[j].start()
        copy(0, sibling, me).wait_recv()
        for j, chip in enumerate(chips):
            copy(4 + j, (*chip, 1 - pc), me).wait_recv()
        for cp in first + passed:
            cp.wait_send()
        mine.wait()

    any_spec = pl.BlockSpec(memory_space=pl.ANY)
    return pl.pallas_call(
        body, name=name, in_specs=[any_spec], out_specs=any_spec,
        out_shape=jax.ShapeDtypeStruct((N_DEV,) + x.shape, x.dtype),
        scratch_shapes=[pltpu.SemaphoreType.DMA((N_DEV - 1,)), pltpu.SemaphoreType.DMA((N_DEV - 1,)),
                        pltpu.SemaphoreType.DMA])(x)


def _call(name, body, grid, ins, in_specs, out_specs, out_shape, scratch=(), exch=None):
    if exch is None:
        return pl.pallas_call(body, name=name, grid=grid, in_specs=in_specs, out_specs=out_specs,
                              out_shape=out_shape, scratch_shapes=list(scratch), compiler_params=_params())(*ins)
    xs, gather = exch
    single = not isinstance(out_shape, (list, tuple))
    o_specs = [out_specs] if single else list(out_specs)
    o_shape = [out_shape] if single else list(out_shape)
    n_in, n_out, n_x, n_scr = len(ins), len(o_shape), len(xs), len(scratch)

    def wrapped(*refs):
        in_refs = refs[:n_in]
        x_refs = refs[n_in:n_in + n_x]
        out_refs = refs[n_in + n_x:n_in + n_x + n_out]
        got_refs = refs[n_in + n_x + n_out:n_in + 2 * n_x + n_out]
        rest = refs[n_in + 2 * n_x + n_out:]
        starts, waits = _exchange_copies(x_refs, got_refs, *rest[n_scr:], gather)
        ids = [pl.program_id(i) for i in range(len(grid))]
        first = functools.reduce(lambda p, q: p & q, [i == 0 for i in ids])
        last = functools.reduce(lambda p, q: p & q, [i == g - 1 for i, g in zip(ids, grid)])

        @pl.when(first)
        def _():
            for f in starts:
                f()

        body(*in_refs, *out_refs, *rest[:n_scr])

        @pl.when(last)
        def _():
            for f in waits:
                f()

    any_spec = pl.BlockSpec(memory_space=pl.ANY)
    outs = pl.pallas_call(
        wrapped, name=name, grid=grid, in_specs=list(in_specs) + [any_spec] * n_x,
        out_specs=o_specs + [any_spec] * n_x, out_shape=o_shape + _exchange_shapes(xs, gather),
        scratch_shapes=list(scratch) + _exchange_sems(n_x), compiler_params=_params())(*ins, *xs)
    res = outs[:n_out]
    return (res[0] if single else res), outs[n_out:]


def _mm_call(name, dims, grid, red_axis, ins, in_specs, out_shape, out_spec, acc_shape,
             scale=1.0, add_scale=None, exch=None):
    nred = grid[red_axis]

    def body(*refs):
        if add_scale is None:
            a_ref, b_ref, o_ref, acc_ref = refs
            add_ref = None
        else:
            a_ref, b_ref, add_ref, o_ref, acc_ref = refs
        k = pl.program_id(red_axis)

        @pl.when(k == 0)
        def _():
            acc_ref[...] = jnp.zeros_like(acc_ref)

        acc_ref[...] += _dot(a_ref[...].astype(BF16), b_ref[...].astype(BF16), dims)

        @pl.when(k == nred - 1)
        def _():
            r = acc_ref[...]
            if scale != 1.0:
                r = r * scale
            if add_ref is not None:
                r = r + add_scale * add_ref[...].astype(F32)
            o_ref[...] = r.astype(o_ref.dtype)

    return _call(name, body, grid, ins, in_specs, out_spec, out_shape, [pltpu.VMEM(acc_shape, F32)], exch)


def _tile(dim, cap):
    t = min(dim, cap)
    while dim % t or t % 128:
        t -= 128
        assert t > 0, (dim, cap)
    return t


def _mm(name, a, b, dims, out_dtype, scale=1.0, add=None, add_scale=None, tm=512, tn=1024, tk=1024):
    if dims == "tn":
        kd, m = a.shape
        n = b.shape[1]
    else:
        m, kd = a.shape
        n = b.shape[1] if dims == "nn" else b.shape[0]
    tm, tn, tk = _tile(m, tm), _tile(n, tn), _tile(kd, tk)
    a_spec = (pl.BlockSpec((tk, tm), lambda i, j, k: (k, i)) if dims == "tn"
              else pl.BlockSpec((tm, tk), lambda i, j, k: (i, k)))
    b_spec = (pl.BlockSpec((tn, tk), lambda i, j, k: (j, k)) if dims == "nt"
              else pl.BlockSpec((tk, tn), lambda i, j, k: (k, j)))
    o_spec = pl.BlockSpec((tm, tn), lambda i, j, k: (i, j))
    ins, specs = [a, b], [a_spec, b_spec]
    if add is not None:
        ins.append(add)
        specs.append(o_spec)
    return _mm_call(name, dims, (m // tm, n // tn, kd // tk), 2, ins, specs,
                    jax.ShapeDtypeStruct((m, n), out_dtype), o_spec, (tm, tn),
                    scale=scale, add_scale=add_scale if add is not None else None)


def _ffn_in(name, x, wg, tm=512, exch=None):
    n = x.shape[0]
    nj = N_DEV // 2

    def body(x_ref, wgate_ref, wup_ref, hg_ref, hu_ref, act_ref):
        xb = x_ref[...].astype(BF16)
        g = _dot(xb, wgate_ref[...])
        u = _dot(xb, wup_ref[...])
        hg_ref[...] = g.astype(BF16)
        hu_ref[...] = u.astype(BF16)
        act_ref[...] = (g * _sigmoid(g) * u).astype(BF16)

    blk = pl.BlockSpec((None, tm, FF_BLK), lambda j, i: (j, i, 0))
    shp = jax.ShapeDtypeStruct((nj, n, FF_BLK), BF16)
    return _call(name, body, (nj, n // tm), [x, wg, wg],
                 [pl.BlockSpec((tm, D_MODEL), lambda j, i: (i, 0)),
                  pl.BlockSpec((None, D_MODEL, FF_BLK), lambda j, i: (j, 0, 0)),
                  pl.BlockSpec((None, D_MODEL, FF_BLK), lambda j, i: (j + nj, 0, 0))],
                 [blk, blk, blk], [shp, shp, shp], exch=exch)


def _ffn_out_bwd(name, dz, wout, hg, hu, act, tm=1024, exch=None):
    n = dz.shape[0]
    nj, ni = N_DEV // 2, n // tm

    def body(dz_ref, w_ref, hg_ref, hu_ref, act_ref, dhg_ref, dhu_ref, dw_ref, acc_ref):
        i = pl.program_id(1)
        dzb = dz_ref[...].astype(BF16)
        dact = 0.5 * _dot(dzb, w_ref[...], "nt")
        g = hg_ref[...].astype(F32)
        u = hu_ref[...].astype(F32)
        s = _sigmoid(g)
        dhg_ref[...] = (dact * u * (s * (1.0 + g * (1.0 - s)))).astype(BF16)
        dhu_ref[...] = (dact * (g * s)).astype(BF16)

        @pl.when(i == 0)
        def _():
            acc_ref[...] = jnp.zeros_like(acc_ref)

        acc_ref[...] += _dot(act_ref[...], dzb, "tn")

        @pl.when(i == ni - 1)
        def _():
            dw_ref[...] = (0.5 * acc_ref[...]).astype(dw_ref.dtype)

    blk = pl.BlockSpec((None, tm, FF_BLK), lambda j, i: (j, i, 0))
    wblk = pl.BlockSpec((FF_BLK, D_MODEL), lambda j, i: (j, 0))
    shp = jax.ShapeDtypeStruct((nj, n, FF_BLK), BF16)
    return _call(name, body, (nj, ni), [dz, wout, hg, hu, act],
                 [pl.BlockSpec((tm, D_MODEL), lambda j, i: (i, 0)), wblk, blk, blk, blk],
                 [blk, blk, wblk], [shp, shp, jax.ShapeDtypeStruct((D_FF, D_MODEL), BF16)],
                 [pltpu.VMEM((FF_BLK, D_MODEL), F32)], exch=exch)


def _mm_ln(name, a, b, xres, g, beta, c, tgt=None, a_blocked=False, tm=512, tk=512):
    if a_blocked:
        nj, n, kj = a.shape
        nk, kb = nj // 2, 2 * kj
        a_spec = pl.BlockSpec((2, tm, kj), lambda i, k: (k, i, 0))
    else:
        n, kd = a.shape
        kb = _tile(kd, tk)
        nk = kd // kb
        a_spec = pl.BlockSpec((tm, kb), lambda i, k: (i, k))
    d = b.shape[1]
    with_loss = tgt is not None

    def body(*refs):
        if with_loss:
            a_ref, b_ref, x_ref, g_ref, be_ref, t_ref, dz_ref, dzb_ref, l_ref, dg_ref, db_ref, acc_ref = refs
        else:
            a_ref, b_ref, x_ref, g_ref, be_ref, o_ref, ob_ref, z_ref, acc_ref = refs
        i, k = pl.program_id(0), pl.program_id(1)

        @pl.when(k == 0)
        def _():
            acc_ref[...] = jnp.zeros_like(acc_ref)

        if a_blocked:
            acc_ref[...] += (_dot(a_ref[0].astype(BF16), b_ref[0:kb // 2, :].astype(BF16))
                             + _dot(a_ref[1].astype(BF16), b_ref[kb // 2:kb, :].astype(BF16)))
        else:
            acc_ref[...] += _dot(a_ref[...].astype(BF16), b_ref[...].astype(BF16))

        @pl.when(k == nk - 1)
        def _():
            z = ALPHA * x_ref[...] + c * acc_ref[...]
            mu = jnp.mean(z, axis=-1, keepdims=True)
            zc = z - mu
            rstd = lax.rsqrt(jnp.mean(zc * zc, axis=-1, keepdims=True) + LN_EPS)
            xh = zc * rstd
            y = xh * g_ref[...] + be_ref[...]
            if with_loss:
                err = y - t_ref[...]
                ct = err * (1.0 / d)
                dxh = ct * g_ref[...]
                dz = rstd * (dxh - jnp.mean(dxh, axis=-1, keepdims=True)
                             - xh * jnp.mean(dxh * xh, axis=-1, keepdims=True))
                dz_ref[...] = dz
                dzb_ref[...] = dz.astype(BF16)
                part = 0.5 * jnp.sum(jnp.sum(err * err, axis=-1, keepdims=True), axis=0, keepdims=True) * (1.0 / d)

                @pl.when(i == 0)
                def _():
                    l_ref[...] = jnp.zeros_like(l_ref)
                    dg_ref[...] = jnp.zeros_like(dg_ref)
                    db_ref[...] = jnp.zeros_like(db_ref)

                l_ref[...] += jnp.broadcast_to(part, l_ref.shape)
                dg_ref[...] += jnp.sum(ct * xh, axis=0, keepdims=True)
                db_ref[...] += jnp.sum(ct, axis=0, keepdims=True)
            else:
                z_ref[...] = z
                o_ref[...] = y
                ob_ref[...] = y.astype(BF16)

    row = pl.BlockSpec((tm, d), lambda i, k: (i, 0))
    vec = pl.BlockSpec((1, d), lambda i, k: (0, 0))
    ins = [a, b, xres, g, beta]
    in_specs = [a_spec, pl.BlockSpec((kb, d), lambda i, k: (k, 0)), row, vec, vec]
    out_specs = [row, row]
    out_shape = [jax.ShapeDtypeStruct((n, d), F32), jax.ShapeDtypeStruct((n, d), BF16)]
    if with_loss:
        ins.append(tgt)
        in_specs.append(row)
        out_specs += [pl.BlockSpec((1, 128), lambda i, k: (0, 0)), vec, vec]
        out_shape += [jax.ShapeDtypeStruct((1, 128), F32)] + [jax.ShapeDtypeStruct((1, d), F32)] * 2
    else:
        out_specs.append(row)
        out_shape.append(jax.ShapeDtypeStruct((n, d), F32))
    return _call(name, body, (n // tm, nk), ins, in_specs, out_specs, out_shape, [pltpu.VMEM((tm, d), F32)])


def _rowwise(name, fn, rows, params, out_rows, out_accs, tm=256, exch=None):
    specs, ins = [], []
    for r in rows:
        arr, w, cb = r if isinstance(r, tuple) else (r, r.shape[1], 0)
        ins.append(arr)
        specs.append(pl.BlockSpec((tm, w), functools.partial(lambda i, cb: (i, cb), cb=cb)))
    n = ins[0].shape[0]
    for p in params:
        ins.append(p)
        specs.append(pl.BlockSpec(p.shape, lambda i: (0, 0)))
    n_in, n_or = len(ins), len(out_rows)

    def body(*refs):
        outs = fn(*[r[...] for r in refs[:n_in]])
        o_refs = refs[n_in:]
        for o_ref, o in zip(o_refs[:n_or], outs[:n_or]):
            o_ref[...] = o.astype(o_ref.dtype)
        if out_accs:
            @pl.when(pl.program_id(0) == 0)
            def _():
                for a_ref in o_refs[n_or:]:
                    a_ref[...] = jnp.zeros_like(a_ref)

            for a_ref, a in zip(o_refs[n_or:], outs[n_or:]):
                a_ref[...] += a.astype(F32)

    out_specs = [pl.BlockSpec((tm, w), lambda i: (i, 0)) for w, _ in out_rows]
    out_specs += [pl.BlockSpec(s, lambda i: (0, 0)) for s in out_accs]
    out_shape = [jax.ShapeDtypeStruct((n, w), dt) for w, dt in out_rows]
    out_shape += [jax.ShapeDtypeStruct(s, F32) for s in out_accs]
    return _call(name, body, (n // tm,), ins, specs, out_specs, out_shape, exch=exch)


def _vjp_of(fn, n_in):
    def g(*args):
        ins, cts = args[:n_in], args[n_in:]
        outs, pull = jax.vjp(fn, *ins)
        return pull(tuple(c.astype(o.dtype) for c, o in zip(cts, outs)))
    return g


def _ln_bwd(name, z, g, ct):
    def fn(zt, ct_, gt):
        mu = jnp.mean(zt, axis=-1, keepdims=True)
        zc = zt - mu
        rstd = lax.rsqrt(jnp.mean(zc * zc, axis=-1, keepdims=True) + LN_EPS)
        xh = zc * rstd
        dxh = ct_ * gt
        dz = rstd * (dxh - jnp.mean(dxh, axis=-1, keepdims=True)
                     - xh * jnp.mean(dxh * xh, axis=-1, keepdims=True))
        return dz, dz, jnp.sum(ct_ * xh, axis=0, keepdims=True), jnp.sum(ct_, axis=0, keepdims=True)

    d = z.shape[1]
    return _rowwise(name, fn, [z, ct], [g], [(d, F32), (d, BF16)], [(1, d), (1, d)])


SHIFT_TILE = 256


def _shift_specs(t):
    r8 = SHIFT_TILE // 8
    return [pl.BlockSpec((None, SHIFT_TILE, SHIFT_PAD), lambda b, i: (b, i, 0)),
            pl.BlockSpec((None, 8, SHIFT_PAD), lambda b, i: (b, jnp.maximum(i * r8 - 1, 0), 0)),
            pl.BlockSpec((None, 8, SHIFT_PAD), lambda b, i: (b, jnp.minimum((i + 1) * r8, t // 8 - 1), 0))]


def _neighbour_diffs(cur, prev_ref, next_ref, i, nt):
    prow = jnp.where(i > 0, prev_ref[7:8, :], 0.0)
    nrow = jnp.where(i < nt - 1, next_ref[0:1, :], 0.0)
    rid = lax.broadcasted_iota(jnp.int32, cur.shape, 0)
    return (jnp.where(rid == 0, prow, pltpu.roll(cur, 1, 0)) - cur,
            jnp.where(rid == SHIFT_TILE - 1, nrow, pltpu.roll(cur, SHIFT_TILE - 1, 0)) - cur)


def _shift(name, src, mu_prev, mu_next):
    bsz, t, _ = src.shape
    nt, w = t // SHIFT_TILE, SHIFT_PAD

    def body(cur_ref, prev_ref, next_ref, mp_ref, mn_ref, o_ref):
        cur = cur_ref[...]
        dprev, dnext = _neighbour_diffs(cur, prev_ref, next_ref, pl.program_id(1), nt)
        o_ref[...] = cur + mp_ref[...] * dprev + mn_ref[...] * dnext

    specs = _shift_specs(t)
    vec = pl.BlockSpec((1, w), lambda b, i: (0, 0))
    return _call(name, body, (bsz, nt), [src, src, src, mu_prev, mu_next], specs + [vec, vec], specs[0],
                 jax.ShapeDtypeStruct((bsz, t, w), F32))


def _shift_bwd(name, dps, p, mu_prev, mu_next):
    bsz, t, _ = dps.shape
    nt, w = t // SHIFT_TILE, SHIFT_PAD

    def body(d_ref, dprev_ref, dnext_ref, p_ref, pprev_ref, pnext_ref, mp_ref, mn_ref, o_ref, da_ref, db_ref):
        b, i = pl.program_id(0), pl.program_id(1)
        dcur = d_ref[...]
        d_dprev, d_dnext = _neighbour_diffs(dcur, dprev_ref, dnext_ref, i, nt)
        o_ref[...] = (dcur + mn_ref[...] * d_dprev + mp_ref[...] * d_dnext).astype(o_ref.dtype)
        p_dprev, p_dnext = _neighbour_diffs(p_ref[...], pprev_ref, pnext_ref, i, nt)

        @pl.when((b == 0) & (i == 0))
        def _():
            da_ref[...] = jnp.zeros_like(da_ref)
            db_ref[...] = jnp.zeros_like(db_ref)

        da_ref[...] += jnp.sum(dcur * p_dprev, axis=0, keepdims=True)
        db_ref[...] += jnp.sum(dcur * p_dnext, axis=0, keepdims=True)

    specs = _shift_specs(t)
    vec = pl.BlockSpec((1, w), lambda b, i: (0, 0))
    return _call(name, body, (bsz, nt), [dps, dps, dps, p, p, p, mu_prev, mu_next], specs + specs + [vec, vec],
                 [specs[0], vec, vec],
                 [jax.ShapeDtypeStruct((bsz, t, w), BF16), jax.ShapeDtypeStruct((1, w), F32),
                  jax.ShapeDtypeStruct((1, w), F32)])


CONV_BLK = 128


def _halo_specs(t, tt, w):
    r16 = tt // 16
    return [pl.BlockSpec((None, tt, w), lambda b, i: (b, i, 0)),
            pl.BlockSpec((None, 16, w), lambda b, i: (b, jnp.maximum(i * r16 - 1, 0), 0)),
            pl.BlockSpec((None, 16, w), lambda b, i: (b, jnp.minimum((i + 1) * r16, t // 16 - 1), 0))]


def _fill_pad(pad_ref, cur_ref, prev_ref, next_ref, i, nt, tt):
    pad_ref[0:16, :] = jnp.where(i > 0, prev_ref[...], 0.0)
    pad_ref[16:16 + tt, :] = cur_ref[...]
    pad_ref[16 + tt:32 + tt, :] = jnp.where(i < nt - 1, next_ref[...], 0.0)


def _dwconv(name, u, dw32, bias, flip, tt=512):
    bsz, t, w = u.shape
    tt = min(tt, t)
    nt = t // tt

    def body(cur_ref, prev_ref, next_ref, dw_ref, b_ref, o_ref, pad_ref):
        i = pl.program_id(1)
        _fill_pad(pad_ref, cur_ref, prev_ref, next_ref, i, nt, tt)
        for r0 in range(0, tt, CONV_BLK):
            for cs in (slice(c0, c0 + CONV_BLK) for c0 in range(0, w, CONV_BLK)):
                acc = jnp.broadcast_to(b_ref[:, cs], (CONV_BLK, CONV_BLK))
                for k in range(CONV_K):
                    kk = CONV_K - 1 - k if flip else k
                    acc = acc + pad_ref[pl.ds(r0 + 1 + k, CONV_BLK), cs] * dw_ref[kk:kk + 1, cs]
                o_ref[r0:r0 + CONV_BLK, cs] = acc

    return _call(name, body, (bsz, nt), [u, u, u, dw32, bias],
                 _halo_specs(t, tt, w) + [pl.BlockSpec((32, w), lambda b, i: (0, 0)),
                                          pl.BlockSpec((1, w), lambda b, i: (0, 0))],
                 pl.BlockSpec((None, tt, w), lambda b, i: (b, i, 0)), jax.ShapeDtypeStruct((bsz, t, w), F32),
                 [pltpu.VMEM((tt + 32, w), F32)])


def _dwconv_dw(name, u, dc, tt=512):
    bsz, t, w = u.shape
    tt = min(tt, t)
    nt = t // tt

    def body(cur_ref, prev_ref, next_ref, dc_ref, ddw_ref, db_ref, pad_ref):
        b, i = pl.program_id(0), pl.program_id(1)
        _fill_pad(pad_ref, cur_ref, prev_ref, next_ref, i, nt, tt)

        @pl.when((b == 0) & (i == 0))
        def _():
            ddw_ref[...] = jnp.zeros_like(ddw_ref)
            db_ref[...] = jnp.zeros_like(db_ref)

        dcv = dc_ref[...]
        db_ref[...] += jnp.sum(dcv, axis=0, keepdims=True)
        for k in range(CONV_K):
            ddw_ref[k:k + 1, :] += jnp.sum(dcv * pad_ref[pl.ds(1 + k, tt), :], axis=0, keepdims=True)

    return _call(name, body, (bsz, nt), [u, u, u, dc],
                 _halo_specs(t, tt, w) + [pl.BlockSpec((None, tt, w), lambda b, i: (b, i, 0))],
                 [pl.BlockSpec((32, w), lambda b, i: (0, 0)), pl.BlockSpec((1, w), lambda b, i: (0, 0))],
                 [jax.ShapeDtypeStruct((32, w), F32), jax.ShapeDtypeStruct((1, w), F32)],
                 [pltpu.VMEM((tt + 32, w), F32)])


PAIR = 2 * HEAD
N_PAIRS = RW // PAIR


def _chunk_pairs(s, r, lw, k, v, kk, a, sgn, tm_known=None):
    n, m = CHUNK, 2 * CHUNK
    in_a = lax.broadcasted_iota(jnp.int32, (n, PAIR), 1) < HEAD

    def stack2(z):
        return jnp.concatenate([jnp.where(in_a, z, 0.0), jnp.where(in_a, 0.0, z)], axis=0)

    def each(f, *lists):
        return [f(*z) for z in zip(*lists)]

    sgn_f = sgn.astype(F32)
    row2 = lax.broadcasted_iota(jnp.int32, (m, m), 0)
    col2 = lax.broadcasted_iota(jnp.int32, (m, m), 1)
    same = (row2 >= n) == (col2 >= n)
    dlt = ((row2 & (n - 1)) - (col2 & (n - 1))) * sgn
    incl, strict = same & (dlt >= 0), same & (dlt > 0)
    eye = jnp.where(row2 == col2, 1.0, 0.0)

    cum = each(lambda lw_: _dir_cumsum(lw_, sgn_f), lw)
    tot = each(lambda lw_: jnp.sum(lw_, axis=0, keepdims=True), lw)
    e_neg = each(lambda c_: jnp.exp(-c_), cum)
    e_rest = each(lambda t_, c_: jnp.exp(t_ - c_), tot, cum)
    beta = each(lambda kk_, a_: kk_ * a_, kk, a)
    lhs = each(lambda kk_, c_, lw_, r_: jnp.concatenate(
        [stack2(-kk_ * jnp.exp(c_ - lw_)), stack2(r_ * jnp.exp(c_))], axis=0), kk, cum, lw, r)
    rhs = each(lambda b_, k_, e_: jnp.concatenate([stack2(b_ * e_), stack2(k_ * e_)], axis=0), beta, k, e_neg)
    sc = each(lambda l_, r_: _dot3(l_, r_, "nt"), lhs, rhs)
    l_ab = each(lambda sc_: jnp.where(strict, sc_[0:m, 0:m], 0.0), sc)
    l_ak = each(lambda sc_: jnp.where(strict, sc_[0:m, m:2 * m], 0.0), sc)
    m_r = each(lambda sc_: jnp.where(jnp.concatenate([incl, incl], axis=1), sc_[m:2 * m, :], 0.0), sc)
    tm = _tri_inv(l_ab, eye) if tm_known is None else _tri_inv_known(l_ab, tm_known)
    z = _dot1_two(lhs, s, "nt")
    v2 = each(stack2, v)
    u2 = _dot1_two(tm, each(lambda z_, lv_: z_[0:m] + lv_, z, _dot1_two(l_ak, v2)))
    uv = each(lambda u_, v_: jnp.concatenate([u_, v_], axis=0), u2, v2)
    y2 = each(lambda z_, mu_: z_[m:2 * m] + mu_, z, _dot1_two(m_r, uv))
    bk = each(lambda b_, k_, e_: jnp.concatenate([stack2(b_ * e_), stack2(k_ * e_)], axis=0), beta, k, e_rest)
    s_new = each(lambda s_, t_, d_: s_ * jnp.exp(t_) + d_, s, tot, _dot1_two(uv, bk, "tn"))
    return each(lambda y_: y_[0:n] + y_[n:m], y2), s_new, tm


SCAN_SEQS = 4
N_CHAINS = SCAN_SEQS * N_PAIRS


def _pair_tiles(ref):
    return [ref[q, :, p * PAIR:(p + 1) * PAIR] for q in range(SCAN_SEQS) for p in range(N_PAIRS)]


def _store_tiles(ref, tiles):
    for q in range(SCAN_SEQS):
        for p in range(N_PAIRS):
            ref[q, :, p * PAIR:(p + 1) * PAIR] = tiles[q * N_PAIRS + p]


def _scan_specs(order):
    shared = pl.BlockSpec((SCAN_SEQS, CHUNK, RW), lambda d, b, c: (b, order(d, c), 0))
    per_dir = pl.BlockSpec((SCAN_SEQS, CHUNK, RW), lambda d, b, c: (b, order(d, c), d))
    state = pl.BlockSpec((None, SCAN_SEQS, None, N_PAIRS, PAIR, PAIR), lambda d, b, c: (d, b, order(d, c), 0, 0, 0))
    return shared, per_dir, state


def _scan_fwd(r, v, kk, lw, kd, a, bsz, exch=None):
    n = r.shape[0]
    t = n // bsz
    nc = t // CHUNK

    def order(d, c):
        return c + d * (nc - 1 - 2 * c)

    def body(r_ref, v_ref, kk_ref, lw_ref, kd_ref, a_ref, y_ref, s0_ref, tm_ref, s_ref):
        d, c = pl.program_id(0), pl.program_id(2)

        @pl.when(c == 0)
        def _():
            s_ref[...] = jnp.zeros_like(s_ref)

        s = [s_ref[i] for i in range(N_CHAINS)]
        y, s_new, tm = _chunk_pairs(s, *[_pair_tiles(ref) for ref in (r_ref, lw_ref, kd_ref, v_ref, kk_ref, a_ref)],
                                    1 - 2 * d)
        _store_tiles(y_ref, y)
        for i in range(N_CHAINS):
            s0_ref[i // N_PAIRS, i % N_PAIRS] = s[i]
            tm_ref[i // N_PAIRS, i % N_PAIRS] = tm[i].astype(BF16)
            s_ref[i] = s_new[i]

    shared, per_dir, state = _scan_specs(order)
    seq = lambda z: z.reshape(bsz, t, z.shape[1])
    res = _call("scan_fwd", body, (2, bsz // SCAN_SEQS, nc), [seq(z) for z in (r, v, kk, lw, kd, a)],
                [shared, shared, shared, per_dir, per_dir, per_dir], [per_dir, state, state],
                [jax.ShapeDtypeStruct((bsz, t, 2 * RW), F32),
                 jax.ShapeDtypeStruct((2, bsz, nc, N_PAIRS, PAIR, PAIR), F32),
                 jax.ShapeDtypeStruct((2, bsz, nc, N_PAIRS, PAIR, PAIR), BF16)],
                [pltpu.VMEM((N_CHAINS, PAIR, PAIR), F32)], exch)
    (y, s0, tm), got = res if exch else (res, None)
    y = y.reshape(n, 2 * RW)
    return ([y, s0, tm], got) if exch else [y, s0, tm]


def _scan_bwd(r, v, kk, lw, kd, a, s0, tm, dy, bsz, exch=None):
    n = r.shape[0]
    t = n // bsz
    nc = t // CHUNK

    def order(d, c):
        cc = nc - 1 - c
        return cc + d * (nc - 1 - 2 * cc)

    def body(r_ref, v_ref, kk_ref, lw_ref, kd_ref, a_ref, dy_ref, s0_ref, tm_ref,
             dr_ref, dv_ref, dkk_ref, dlw_ref, dkd_ref, da_ref, ds_ref):
        d, c = pl.program_id(0), pl.program_id(2)

        @pl.when(c == 0)
        def _():
            ds_ref[...] = jnp.zeros_like(ds_ref)

        sgn = 1 - 2 * d
        tm_known = [tm_ref[i // N_PAIRS, i % N_PAIRS].astype(F32) for i in range(N_CHAINS)]
        _, pull = jax.vjp(lambda *ops: _chunk_pairs(*ops, sgn, tm_known)[:2],
                          [s0_ref[i // N_PAIRS, i % N_PAIRS] for i in range(N_CHAINS)],
                          *[_pair_tiles(ref) for ref in (r_ref, lw_ref, kd_ref, v_ref, kk_ref, a_ref)])
        grads = pull((_pair_tiles(dy_ref), [ds_ref[i] for i in range(N_CHAINS)]))
        for i in range(N_CHAINS):
            ds_ref[i] = grads[0][i]
        for o_ref, gx in zip((dr_ref, dlw_ref, dkd_ref, dv_ref, dkk_ref, da_ref), grads[1:]):
            _store_tiles(o_ref, gx)

    shared, per_dir, state = _scan_specs(order)
    shp = jax.ShapeDtypeStruct((bsz, t, 2 * RW), F32)
    seq = lambda z: z.reshape(bsz, t, z.shape[1])
    res = _call("scan_bwd", body, (2, bsz // SCAN_SEQS, nc), [seq(z) for z in (r, v, kk, lw, kd, a, dy)] + [s0, tm],
                [shared, shared, shared, per_dir, per_dir, per_dir, per_dir, state, state],
                [per_dir] * 6, [shp] * 6, [pltpu.VMEM((N_CHAINS, PAIR, PAIR), F32)], exch)
    outs, got = res if exch else (res, None)
    outs = [z.reshape(n, 2 * RW) for z in outs]
    return (outs, got) if exch else outs


def _prep_fn(ps, w0, w2bd, a0, a2bd, g2p, k_k, k_a, hsum):
    head_sum = lambda z: _head_sum(z, hsum)
    r, k, v = ps[:, 0:RW], ps[:, RW:2 * RW], ps[:, 2 * RW:3 * RW]
    lora = 2 * HEAD
    wd, ad, gd = (ps[:, 3 * RW:3 * RW + lora], ps[:, 3 * RW + lora:3 * RW + 2 * lora],
                  ps[:, 3 * RW + 2 * lora:SHIFT_PAD])
    logw = -DECAY_SCALE * _sigmoid(_dot1(jnp.tanh(wd), w2bd) + w0)
    a = _sigmoid(_dot1(ad, a2bd) + a0)
    g = _dot1(_sigmoid(gd), g2p)
    kkr = k * k_k
    kk = kkr / jnp.maximum(jnp.sqrt(head_sum(kkr * kkr)), NORM_EPS)
    k2 = jnp.concatenate([k, k], axis=1)
    ka2 = jnp.concatenate([k_a, k_a], axis=1)
    kd = k2 * (1.0 + (a - 1.0) * ka2)
    return r, v, kk, logw, a, kd, g


def _post_fn(y2, r, v, kd, g, lnx_g, lnx_b, r_k, hsum):
    head_sum = lambda z: _head_sum(z, hsum)
    y = y2[:, 0:RW] + y2[:, RW:2 * RW]
    mu = head_sum(y) * (1.0 / HEAD)
    yc = y - mu
    var = head_sum(yc * yc) * (1.0 / HEAD)
    yn = yc * lax.rsqrt(var + GN_EPS) * lnx_g + lnx_b
    bonus = head_sum(r * (kd[:, 0:RW] + kd[:, RW:2 * RW]) * r_k) * v
    return ((yn + bonus) * g,)


def _glu_fn(pa, pb):
    return (pa * _sigmoid(pb),)


def _conv_out_fn(cv, ln_g, ln_b):
    mu = jnp.mean(cv, axis=-1, keepdims=True)
    cc = cv - mu
    var = jnp.mean(cc * cc, axis=-1, keepdims=True)
    y = cc * lax.rsqrt(var + LN_EPS) * ln_g + ln_b
    return (y * _sigmoid(y),)


def _local_step(x, tgt, w, ex=None):
    bsz, t, d = x.shape
    n = bsz * t
    x2d, tgt2d = x.reshape(n, d), tgt.reshape(n, d)
    hsum = jnp.tile(jnp.kron(jnp.eye(N_HEADS, dtype=BF16), jnp.ones((HEAD, HEAD), BF16)), (2, 1))
    w = dict(w)
    parts = {} if ex else None

    def hosted(result, finish=None):
        if not ex:
            return result
        outs, got = result
        if finish is not None:
            w.update(finish(got))
        return outs

    xb = x2d.astype(BF16)
    hg1, hu1, act1 = hosted(_ffn_in("ffn1_in", xb, w["ffn1_w_in"], exch=(ex["g1"][0], True) if ex else None),
                            ex["g1"][1] if ex else None)
    x1, x1b, z1 = _mm_ln("ffn1_out_ln1", act1, w["ffn1_w_out"], x2d, w["ln1_g"], w["ln1_b"], 0.5, a_blocked=True)
    p = _mm("w_in_proj", x1b, w["w_in"], "nn", F32, tm=1024)
    p3 = p.reshape(bsz, t, IN_PAD)
    ps = _shift("shift_fwd", p3, w["mu_prev"], w["mu_next"]).reshape(n, SHIFT_PAD)
    prep_params = [w["w0"], w["w2"], w["a0"], w["a2"], w["g2"], w["k_k"], w["k_a"], hsum]
    r, v, kk, logw, a, kd, g = _rowwise(
        "rwkv_prep", _prep_fn, [ps], prep_params,
        [(RW, F32), (RW, F32), (RW, F32), (2 * RW, F32), (2 * RW, F32), (2 * RW, F32), (RW, F32)], [])
    y2, s0, tm = hosted(_scan_fwd(r, v, kk, logw, kd, a, bsz, exch=(ex["g2"][0], True) if ex else None),
                    ex["g2"][1] if ex else None)
    post_params = [w["lnx_g"], w["lnx_b"], w["r_k"], hsum]
    (y_rwkv,) = _rowwise("rwkv_post", _post_fn, [y2, r, v, kd, g], post_params, [(RW, BF16)], [])
    (u,) = _rowwise("conv_glu", _glu_fn, [(p, CW, 4), (p, CW, 5)], [], [(CW, F32)], [])
    cv = _dwconv("conv_dw", u.reshape(bsz, t, CW), w["conv_dw"], w["conv_b"], False).reshape(n, CW)
    (y_conv,) = _rowwise("conv_out", _conv_out_fn, [cv], [w["conv_ln_g"], w["conv_ln_b"]], [(CW, BF16)], [])
    ycat = jnp.concatenate([y_rwkv, y_conv], axis=1)
    x2, x2b, z2 = _mm_ln("w_out_ln2", ycat, w["w_out"], x1, w["ln2_g"], w["ln2_b"], 1.0)
    hg2, hu2, act2 = _ffn_in("ffn2_in", x2b, w["ffn2_w_in"])
    gr = {}
    dz3, dz3b, loss, gr["ln3_g"], gr["ln3_b"] = _mm_ln(
        "ffn2_out_ln3", act2, w["ffn2_w_out"], x2, w["ln3_g"], w["ln3_b"], 0.5, tgt=tgt2d, a_blocked=True)

    dx2, _ = _ffn_bwd("ffn2", gr, dz3, dz3b, x2b, w["ffn2_w_in"], w["ffn2_w_out"], hg2, hu2, act2)
    dz2, dz2b, gr["ln2_g"], gr["ln2_b"] = _ln_bwd("ln2_bwd", z2, w["ln2_g"], dx2)
    gr["w_out"] = _mm("w_out_wgrad", ycat, dz2b, "tn", BF16)
    dycat = _mm("w_out_dgrad", dz2b, w["w_out"], "nt", F32)
    conv_out_bwd = _vjp_of(_conv_out_fn, 3)
    dcv, gr["conv_ln_g"], gr["conv_ln_b"] = _rowwise(
        "conv_out_bwd", lambda cv_, ct_, g_, b_: conv_out_bwd(cv_, g_, b_, ct_),
        [cv, (dycat, CW, 1)], [w["conv_ln_g"], w["conv_ln_b"]], [(CW, F32)], [(1, CW), (1, CW)])
    dcv3 = dcv.reshape(bsz, t, CW)
    gr["conv_dw"], gr["conv_b"] = _dwconv_dw("conv_dw_wgrad", u.reshape(bsz, t, CW), dcv3)
    du = _dwconv("conv_dw_dgrad", dcv3, w["conv_dw"], jnp.zeros((1, CW), F32), True).reshape(n, CW)

    def glu_bwd(pa, pb, ct):
        return (jnp.concatenate(_vjp_of(_glu_fn, 2)(pa, pb, ct), axis=1),)

    (dp_conv,) = _rowwise("conv_glu_bwd", glu_bwd, [(p, CW, 4), (p, CW, 5), du], [], [(2 * CW, BF16)], [])

    def post_bwd(y2_, r_, v_, kd_, g_, ct, lg, lb, rk, hs):
        return _vjp_of(lambda *z: _post_fn(*z, hs), 8)(y2_, r_, v_, kd_, g_, lg, lb, rk, ct)

    dy2, dr_post, dv_post, dkd_post, dg, gr["lnx_g"], gr["lnx_b"], gr["r_k"] = _rowwise(
        "rwkv_post_bwd", post_bwd, [y2, r, v, kd, g, (dycat, RW, 0)], post_params,
        [(2 * RW, F32), (RW, F32), (RW, F32), (2 * RW, F32), (RW, F32)], [(1, RW), (1, RW), (1, RW)])
    sends = [jnp.concatenate(gr["ffn2_w_in"], axis=0), gr["ffn2_w_out"].reshape(N_DEV, D_FF // N_DEV, D_MODEL),
             gr["w_out"].reshape(N_DEV, D_MODEL // N_DEV, D_MODEL)]
    res = _scan_bwd(r, v, kk, logw, kd, a, s0, tm, dy2, bsz, exch=(sends, False) if ex else None)
    if ex:
        res, got = res
        parts.update(zip(("ffn2_w_in", "ffn2_w_out", "w_out"), got))
    dr_s, dv_s, dkk_s, dlw, dkd_s, da = res

    def prep_bwd(ps_, dr2, dr1, dv2, dv1, dkk2, dlw_, da_, dkd2, dkd1, dg_, *prm):
        half = lambda z: z[:, 0:RW] + z[:, RW:2 * RW]
        return _vjp_of(lambda *z: _prep_fn(*z, prm[-1]), 8)(
            ps_, *prm[:-1], half(dr2) + dr1, half(dv2) + dv1, half(dkk2), dlw_, da_, dkd2 + dkd1, dg_)

    dps, gr["w0"], gr["w2"], gr["a0"], gr["a2"], gr["g2"], gr["k_k"], gr["k_a"] = _rowwise(
        "rwkv_prep_bwd", prep_bwd,
        [ps, dr_s, dr_post, dv_s, dv_post, dkk_s, dlw, da, dkd_s, dkd_post, dg], prep_params,
        [(SHIFT_PAD, F32)], [q.shape for q in prep_params[:-1]])
    dps3 = dps.reshape(bsz, t, SHIFT_PAD)
    dp_shift, gr["mu_prev"], gr["mu_next"] = _shift_bwd("shift_bwd", dps3, p3, w["mu_prev"], w["mu_next"])
    dp_shift = dp_shift.reshape(n, SHIFT_PAD)
    dp = jnp.concatenate([dp_shift, dp_conv], axis=1)
    gr["w_in"] = _mm("w_in_wgrad", x1b, dp, "tn", BF16)
    dx1 = _mm("w_in_dgrad", dp, w["w_in"], "nt", F32, add=dz2, add_scale=ALPHA, tm=1024)
    dz1, dz1b, gr["ln1_g"], gr["ln1_b"] = _ln_bwd("ln1_bwd", z1, w["ln1_g"], dx1)
    riders = None
    if ex:
        gw_in = _unpad_in_cols(gr["w_in"]).reshape(D_MODEL, N_DEV, IN_COLS // N_DEV).transpose(1, 0, 2)
        small = _pack([_grad_small(nm, gr[nm]) for nm in SMALL_SHARDED + SMALL_REPL])
        riders = {"out_bwd": [gw_in, jnp.broadcast_to(small[None], (N_DEV,) + small.shape)], "in_wgrad": []}
    grad_x, got = _ffn_bwd("ffn1", gr, dz1, dz1b, xb, w["ffn1_w_in"], w["ffn1_w_out"], hg1, hu1, act1, riders)
    if ex:
        parts.update(w_in=got["out_bwd"][0], small=got["out_bwd"][1], ffn1_w_out=got["in_wgrad"][0],
                     ffn1_w_in=got["in_dgrad"][0])
    return loss[0, 0], grad_x.reshape(bsz, t, d), gr, parts


def _ffn_in_wgrad(name, xin, dhg, dhu, tk=1024, exch=None):
    n = xin.shape[0]
    nj, nt = N_DEV // 2, n // tk

    def body(x_ref, g_ref, u_ref, og_ref, ou_ref, accg_ref, accu_ref):
        i = pl.program_id(1)

        @pl.when(i == 0)
        def _():
            accg_ref[...] = jnp.zeros_like(accg_ref)
            accu_ref[...] = jnp.zeros_like(accu_ref)

        xt = x_ref[...].astype(BF16).T
        accg_ref[...] += _dot(xt, g_ref[...])
        accu_ref[...] += _dot(xt, u_ref[...])

        @pl.when(i == nt - 1)
        def _():
            og_ref[...] = accg_ref[...].astype(og_ref.dtype)
            ou_ref[...] = accu_ref[...].astype(ou_ref.dtype)

    dh_blk = pl.BlockSpec((None, tk, FF_BLK), lambda j, i: (j, i, 0))
    o_blk = pl.BlockSpec((None, D_MODEL, FF_BLK), lambda j, i: (j, 0, 0))
    shp = jax.ShapeDtypeStruct((nj, D_MODEL, FF_BLK), BF16)
    return _call(name, body, (nj, nt), [xin, dhg, dhu],
                 [pl.BlockSpec((tk, D_MODEL), lambda j, i: (i, 0)), dh_blk, dh_blk], [o_blk, o_blk], [shp, shp],
                 [pltpu.VMEM((D_MODEL, FF_BLK), F32)] * 2, exch)


def _ffn_in_dgrad(name, dhg, dhu, wg, add, add_scale, tm=1024, exch=None):
    n = add.shape[0]
    nj = N_DEV // 2

    def body(g_ref, u_ref, wgate_ref, wup_ref, add_ref, o_ref, acc_ref):
        j = pl.program_id(1)

        @pl.when(j == 0)
        def _():
            acc_ref[...] = jnp.zeros_like(acc_ref)

        acc_ref[...] += _dot(g_ref[...], wgate_ref[...], "nt") + _dot(u_ref[...], wup_ref[...], "nt")

        @pl.when(j == nj - 1)
        def _():
            o_ref[...] = acc_ref[...] + add_scale * add_ref[...]

    dh_blk = pl.BlockSpec((None, tm, FF_BLK), lambda i, j: (j, i, 0))
    row = pl.BlockSpec((tm, D_MODEL), lambda i, j: (i, 0))
    return _call(name, body, (n // tm, nj), [dhg, dhu, wg, wg, add],
                 [dh_blk, dh_blk, pl.BlockSpec((None, D_MODEL, FF_BLK), lambda i, j: (j, 0, 0)),
                  pl.BlockSpec((None, D_MODEL, FF_BLK), lambda i, j: (j + nj, 0, 0)), row],
                 row, jax.ShapeDtypeStruct((n, D_MODEL), F32), [pltpu.VMEM((tm, D_MODEL), F32)], exch)


def _ffn_bwd(tag, gr, dz, dzb, xin, wg, wout, hg, hu, act, riders=None):
    own = riders is not None

    def hosted(result):
        return result if own else (result, None)

    (dhg, dhu, gwo), got_a = hosted(_ffn_out_bwd(
        tag + "_out_bwd", dzb, wout, hg, hu, act, exch=(riders["out_bwd"], False) if own else None))
    gr[tag + "_w_out"] = gwo
    send = ([gwo.reshape(N_DEV, D_FF // N_DEV, D_MODEL)] + riders["in_wgrad"], False) if own else None
    dw, got_b = hosted(_ffn_in_wgrad(tag + "_in_wgrad", xin, dhg, dhu, exch=send))
    gr[tag + "_w_in"] = dw
    send = ([jnp.concatenate(dw, axis=0)], False) if own else None
    dx, got_c = hosted(_ffn_in_dgrad(tag + "_in_dgrad", dhg, dhu, wg, dz, ALPHA, exch=send))
    return dx, {"out_bwd": got_a, "in_wgrad": got_b, "in_dgrad": got_c}


def _adam_math(g, w, m, v):
    m = ADAM_B1 * m + (1.0 - ADAM_B1) * g
    v = ADAM_B2 * v + (1.0 - ADAM_B2) * (g * g)
    m_hat = m / (1.0 - ADAM_B1 ** ADAM_STEP)
    v_hat = v / (1.0 - ADAM_B2 ** ADAM_STEP)
    delta = -ADAM_LR * (m_hat / (jnp.sqrt(v_hat) + ADAM_EPS) + ADAM_WD * w)
    return delta, m, v


def _adam(name, parts, w, m, v, tr=128):
    rows, cols = w.shape
    tr = min(tr, rows)
    while rows % tr:
        tr -= 8

    def body(p_ref, w_ref, m_ref, v_ref, g_ref, d_ref, mo_ref, vo_ref):
        g = p_ref[0].astype(F32)
        for s in range(1, N_DEV):
            g = g + p_ref[s].astype(F32)
        g_ref[...] = g
        d_ref[...], mo_ref[...], vo_ref[...] = _adam_math(g, w_ref[...], m_ref[...], v_ref[...])

    blk = pl.BlockSpec((tr, cols), lambda i: (i, 0))
    shp = jax.ShapeDtypeStruct((rows, cols), F32)
    return _call(name, body, (rows // tr,), [parts, w, m, v],
                 [pl.BlockSpec((N_DEV, tr, cols), lambda i: (0, i, 0)), blk, blk, blk], [blk] * 4, [shp] * 4)


def _sum8(name, parts):
    _, rows, cols = parts.shape

    def body(p_ref, o_ref):
        g = p_ref[0]
        for s in range(1, N_DEV):
            g = g + p_ref[s]
        o_ref[...] = g

    return pl.pallas_call(body, name=name, out_shape=jax.ShapeDtypeStruct((rows, cols), F32),
                          compiler_params=_params())(parts)


def _adam_small(name, g, w, m, v):
    def body(g_ref, w_ref, m_ref, v_ref, d_ref, mo_ref, vo_ref):
        d_ref[...], mo_ref[...], vo_ref[...] = _adam_math(g_ref[...], w_ref[...], m_ref[...], v_ref[...])

    shp = jax.ShapeDtypeStruct(g.shape, F32)
    return pl.pallas_call(body, name=name, out_shape=[shp] * 3, compiler_params=_params())(g, w, m, v)


def _pack(arrs, lane=128):
    flat = jnp.concatenate([a.reshape(-1).astype(F32) for a in arrs])
    pad = (-flat.shape[0]) % (8 * lane)
    return jnp.pad(flat, (0, pad)).reshape(-1, lane)


def _unpack(packed, shapes):
    flat, out, off = packed.reshape(-1), [], 0
    for s in shapes:
        sz = math.prod(s)
        out.append(flat[off:off + sz].reshape(s))
        off += sz
    return out


def _pad_in_cols(wfull):
    zeros = jnp.zeros((wfull.shape[0], SHIFT_PAD - SHIFT_COLS), wfull.dtype)
    return jnp.concatenate([wfull[:, :SHIFT_COLS], zeros, wfull[:, SHIFT_COLS:]], axis=1)


def _unpad_in_cols(gfull):
    return jnp.concatenate([gfull[:, :SHIFT_COLS], gfull[:, SHIFT_PAD:]], axis=1)


def _block_diag2(wd):
    z = jnp.zeros_like(wd[0])
    return jnp.concatenate([jnp.concatenate([wd[0], z], axis=1), jnp.concatenate([z, wd[1]], axis=1)], axis=0)


def _unblock_diag2(g):
    return jnp.stack([g[0:64, 0:RW], g[64:128, RW:2 * RW]])


SMALL_SHARDED = ("w0", "w2", "a0", "a2", "g2", "conv_dw")
SMALL_REPL = ("mu_prev", "mu_next", "k_k", "k_a", "r_k", "lnx_g", "lnx_b", "conv_b", "conv_ln_g", "conv_ln_b",
              "ln1_g", "ln1_b", "ln2_g", "ln2_b", "ln3_g", "ln3_b")
BIG = ("ffn1_w_in", "ffn1_w_out", "w_in", "w_out", "ffn2_w_in", "ffn2_w_out")
WEIGHTS = ("ffn1_w_in", "ffn1_w_out", "w_in", "mu_prev", "mu_next", "w0", "w2", "a0", "a2", "g2", "k_k", "k_a",
           "r_k", "lnx_g", "lnx_b", "conv_dw", "conv_b", "conv_ln_g", "conv_ln_b", "w_out", "ffn2_w_in",
           "ffn2_w_out", "ln1_g", "ln1_b", "ln2_g", "ln2_b", "ln3_g", "ln3_b")


def _full_small(name, full):
    if name in ("w0", "a0"):
        return full.reshape(1, 2 * RW)
    if name in ("w2", "a2"):
        return _block_diag2(full)
    if name == "g2":
        return jnp.pad(full, ((0, 256 - GATE_LORA), (0, 0)))
    if name == "conv_dw":
        return jnp.pad(full, ((0, 1), (0, 0)))
    if name in ("mu_prev", "mu_next"):
        return jnp.pad(full.reshape(1, SHIFT_COLS), ((0, 0), (0, SHIFT_PAD - SHIFT_COLS)))
    return full.reshape(1, -1)


def _grad_small(name, g):
    if name in ("w0", "a0"):
        return g.reshape(2, RW)
    if name in ("w2", "a2"):
        return _unblock_diag2(g)
    if name == "g2":
        return g[:GATE_LORA]
    if name == "conv_dw":
        return g[:CONV_K]
    if name in ("mu_prev", "mu_next"):
        return g[0, :SHIFT_COLS]
    if name == "r_k":
        return g.reshape(N_HEADS, HEAD)
    return g.reshape(-1)


def kernel(x, ffn1_w_in, ffn1_w_out, w_in, mu_prev, mu_next, w0, w2, a0, a2, g2, k_k, k_a, r_k, lnx_g, lnx_b, conv_dw, conv_b, conv_ln_g, conv_ln_b, w_out, ffn2_w_in, ffn2_w_out, ln1_g, ln1_b, ln2_g, ln2_b, ln3_g, ln3_b, loss_target, m_ffn1_w_in, m_ffn1_w_out, m_w_in, m_mu_prev, m_mu_next, m_w0, m_w2, m_a0, m_a2, m_g2, m_k_k, m_k_a, m_r_k, m_lnx_g, m_lnx_b, m_conv_dw, m_conv_b, m_conv_ln_g, m_conv_ln_b, m_w_out, m_ffn2_w_in, m_ffn2_w_out, m_ln1_g, m_ln1_b, m_ln2_g, m_ln2_b, m_ln3_g, m_ln3_b, v_ffn1_w_in, v_ffn1_w_out, v_w_in, v_mu_prev, v_mu_next, v_w0, v_w2, v_a0, v_a2, v_g2, v_k_k, v_k_a, v_r_k, v_lnx_g, v_lnx_b, v_conv_dw, v_conv_b, v_conv_ln_g, v_conv_ln_b, v_w_out, v_ffn2_w_in, v_ffn2_w_out, v_ln1_g, v_ln1_b, v_ln2_g, v_ln2_b, v_ln3_g, v_ln3_b):
    args = dict(locals())
    drop = lambda z: z.reshape(z.shape[1:])
    wsh = {n: drop(args[n]) for n in WEIGHTS}
    msh = {n: drop(args["m_" + n]) for n in WEIGHTS}
    vsh = {n: drop(args["v_" + n]) for n in WEIGHTS}
    me = 4 * lax.axis_index("x") + 2 * lax.axis_index("y") + lax.axis_index("c")
    bf = {n: wsh[n].astype(BF16) for n in BIG}

    w = {"ffn1_w_in": _gather_two_level("gather_ffn1_w_in", bf["ffn1_w_in"])}
    for n in SMALL_REPL:
        w[n] = _full_small(n, wsh[n])
    small_shapes = [wsh[n].shape for n in SMALL_SHARDED]

    def finish1(got):
        f1_out, w_in_g, small = got
        cols = zip(*[_unpack(small[dv], small_shapes) for dv in range(N_DEV)])
        out = {n: _full_small(n, jnp.concatenate(s, axis=-1)) for n, s in zip(SMALL_SHARDED, cols)}
        out["ffn1_w_out"] = f1_out.reshape(D_FF, D_MODEL)
        out["w_in"] = _pad_in_cols(w_in_g.transpose(1, 0, 2).reshape(D_MODEL, IN_COLS))
        return out

    def finish2(got):
        w_out_g, f2_in, f2_out = got
        return {"w_out": w_out_g.reshape(D_MODEL, D_MODEL), "ffn2_w_in": f2_in,
                "ffn2_w_out": f2_out.reshape(D_FF, D_MODEL)}

    ex = {"g1": ([bf["ffn1_w_out"], bf["w_in"], _pack([wsh[n] for n in SMALL_SHARDED])], finish1),
          "g2": ([bf["w_out"], bf["ffn2_w_in"], bf["ffn2_w_out"]], finish2)}
    loss_part, grad_x, gr, parts = _local_step(x, loss_target, w, ex)
    loss = lax.psum(loss_part, ("x", "y", "c"))

    out = {n: _adam("adam_" + n, parts[n], wsh[n], msh[n], vsh[n]) for n in BIG}
    small_names = SMALL_SHARDED + SMALL_REPL
    full_shapes = [_grad_small(n, gr[n]).shape for n in small_names]
    summed = _unpack(_sum8("sum_small_grads", parts["small"]), full_shapes)
    mine = []
    for n, g in zip(small_names, summed):
        if n in SMALL_SHARDED:
            g = lax.dynamic_slice_in_dim(g, me * HEAD, HEAD, axis=g.ndim - 1)
        mine.append(g)
    shapes = [g.shape for g in mine]
    d_s, m_s, v_s = _adam_small("adam_small", _pack(mine), _pack([wsh[n] for n in small_names]),
                                _pack([msh[n] for n in small_names]), _pack([vsh[n] for n in small_names]))
    for n, g, dl, mn, vn in zip(small_names, mine, _unpack(d_s, shapes), _unpack(m_s, shapes), _unpack(v_s, shapes)):
        out[n] = (g, dl, mn, vn)

    res = [loss, grad_x]
    for k in range(4):
        res += [out[n][k].reshape((1,) + out[n][k].shape) for n in WEIGHTS]
    return tuple(res)
```
